```python
import functools
import jax, jax.numpy as jnp
from jax import lax
import numpy as np

D_MODEL = 1024
BATCH = 32
SEQ = 256
DEPTH = 2
DEC_BATCH = 2
DEC_SEQ = 1024
PAST_LEN = 512

GRID_W = 64
HEAD_DIM = 64
N_Q_HEADS = 8
N_KV_HEADS = 2
GQA_GROUP = N_Q_HEADS // N_KV_HEADS
ATTN_WIDTH = N_Q_HEADS * HEAD_DIM
KV_WIDTH = N_KV_HEADS * HEAD_DIM
ATTN_SCALE = HEAD_DIM ** -0.5
Q_BLOCK = 128
ROPE_THETA = 10000.0
AXIS_ROT = HEAD_DIM // 2
CONV_WIDTH = 256
SGU_WIDTH = 256
SGU_HEADS = 4
SGU_HEAD_DIM = SGU_WIDTH // SGU_HEADS
CHUNK = 128
MIX_WIDTH = ATTN_WIDTH + CONV_WIDTH + SGU_WIDTH
_Q_END = ATTN_WIDTH
_K_END = _Q_END + KV_WIDTH
_V_END = _K_END + KV_WIDTH
_CI_END = _V_END + CONV_WIDTH
_CB_END = _CI_END + CONV_WIDTH
_CC_END = _CB_END + CONV_WIDTH
_SU_END = _CC_END + SGU_WIDTH
IN_WIDTH = _SU_END + SGU_WIDTH
IN_SPLITS = (_Q_END, _K_END, _V_END, _CI_END, _CB_END, _CC_END, _SU_END)
D_FF = 2816
N_EXPERTS = 8
TOP_K = 2
D_FF_EXPERT = 1408
EXPERT_BLOCK = 128
EPS = 1e-6
DEEPNORM_ALPHA = (2 * DEPTH) ** 0.25
DEEPNORM_BETA = (8 * DEPTH) ** -0.25
F32 = jnp.float32

kernel_name = "hybrid_dit_prefix_ctx_step"


def _standardize(xf):
    mu = jnp.mean(xf, -1, keepdims=True)
    d = xf - mu
    return d * lax.rsqrt(jnp.mean(jnp.square(d), -1, keepdims=True) + EPS)


def _ln_plain(x):
    return _standardize(x.astype(F32)).astype(x.dtype)


def _ln_affine(x, g, b):
    return (_standardize(x.astype(F32)) * g.astype(F32) + b.astype(F32)).astype(x.dtype)


def _rms_heads(x, g):
    xf = x.astype(F32)
    y = xf * lax.rsqrt(jnp.mean(jnp.square(xf), -1, keepdims=True) + EPS) * g.astype(F32)
    return y.astype(x.dtype)


def _axial_rope_tables(n_tokens):
    rows = n_tokens // GRID_W
    row = jnp.repeat(jnp.arange(rows, dtype=F32), GRID_W)
    col = jnp.tile(jnp.arange(GRID_W, dtype=F32), rows)
    inv_freq = ROPE_THETA ** (-jnp.arange(0, AXIS_ROT, 2, dtype=F32) / AXIS_ROT)
    ang = jnp.stack([row[:, None] * inv_freq, col[:, None] * inv_freq], axis=1)
    return jnp.cos(ang), jnp.sin(ang)


def _apply_rope(x, cos, sin):
    B, S, H, _ = x.shape
    xf = x.astype(F32).reshape(B, S, H, 2, 2, AXIS_ROT // 2)
    x1, x2 = xf[..., 0, :], xf[..., 1, :]
    c = cos[None, :, None]
    s = sin[None, :, None]
    out = jnp.stack([x1 * c - x2 * s, x2 * c + x1 * s], axis=-2)
    return out.reshape(x.shape).astype(x.dtype)


def _block_attention(q, k, v):
    B, S = q.shape[:2]
    nb = S // Q_BLOCK
    qb = q.reshape(B, nb, Q_BLOCK, N_KV_HEADS, GQA_GROUP, HEAD_DIM).transpose(1, 0, 2, 3, 4, 5)

    def one_block(qblk):
        s = jnp.einsum('bqhgd,bkhd->bhgqk', qblk, k).astype(F32) * ATTN_SCALE
        p = jax.nn.softmax(s, axis=-1).astype(v.dtype)
        return jnp.einsum('bhgqk,bkhd->bqhgd', p, v)

    o = lax.map(one_block, qb)
    return o.transpose(1, 0, 2, 3, 4, 5).reshape(B, S, ATTN_WIDTH)


def _latent_attention(q, k, v, k_ctx, v_ctx, cos, sin):
    q = _apply_rope(q, cos, sin)
    k = _apply_rope(k, cos, sin)
    k_all = jnp.concatenate([k, k_ctx.astype(k.dtype)], axis=1)
    v_all = jnp.concatenate([v, v_ctx.astype(v.dtype)], axis=1)
    return _block_attention(q, k_all, v_all)


def _short_conv(x, w, b):
    xp = jnp.pad(x, ((0, 0), (1, 1), (0, 0)))
    return xp[:, :-2] * w[0] + xp[:, 1:-1] * w[1] + xp[:, 2:] * w[2] + b


def _chunk_mlp(u, v, g, w_s, b_s):
    B, S, _ = v.shape
    n = S // CHUNK
    vh = v.reshape(B, n, CHUNK, SGU_HEADS, SGU_HEAD_DIM)
    vn = (_standardize(vh.astype(F32)) * g.reshape(SGU_HEADS, SGU_HEAD_DIM).astype(F32)).astype(v.dtype)
    s = jnp.einsum('hpq,bnqhc->bnphc', w_s, vn) + b_s.T[:, :, None]
    return u * s.reshape(B, S, SGU_WIDTH)


def _token_mixers(h, attend, p):
    B, S, _ = h.shape
    z = h @ p['w_in']
    q, k, v, c_in, c_b, c_c, s_u, s_v = jnp.split(z, IN_SPLITS, axis=-1)
    q = _rms_heads(q.reshape(B, S, N_Q_HEADS, HEAD_DIM), p['q_g'])
    k = _rms_heads(k.reshape(B, S, N_KV_HEADS, HEAD_DIM), p['k_g'])
    v = v.reshape(B, S, N_KV_HEADS, HEAD_DIM)
    a_out = attend(q, k, v)
    conv_out = c_b * _short_conv(c_c * c_in, p['conv_w'], p['conv_b'])
    sgu_out = _chunk_mlp(s_u, s_v, p['sgu_g'], p['sgu_w'], p['sgu_b'])
    mix = jnp.concatenate([a_out, conv_out, sgu_out], axis=-1) @ p['w_out']
    return mix, k, v


def _modulation(cond, w, b):
    m = (jax.nn.silu(cond) @ w + b)[:, None, :]
    return jnp.split(m, 6, axis=-1)


def _swiglu(h, w1, w3, w2):
    return (jax.nn.silu(h @ w1) * (h @ w3)) @ w2


def _moe_swiglu(h, router_w, w1, w3, w2):
    shp = h.shape
    x = h.reshape(-1, shp[-1])
    T = x.shape[0]
    logits = (x @ router_w).astype(F32)
    top_v, top_i = lax.top_k(logits, TOP_K)
    gates = jax.nn.softmax(top_v, axis=-1)
    n_assign = T * TOP_K
    e_flat = top_i.reshape(-1).astype(jnp.int32)
    tok_flat = jnp.repeat(jnp.arange(T, dtype=jnp.int32), TOP_K)
    counts = jnp.bincount(e_flat, length=N_EXPERTS)
    padded = (counts + EXPERT_BLOCK - 1) // EXPERT_BLOCK * EXPERT_BLOCK
    pad_end = jnp.cumsum(padded)
    pad_start = pad_end - padded
    cnt_start = jnp.cumsum(counts) - counts
    order = jnp.argsort(e_flat * n_assign + jnp.arange(n_assign, dtype=jnp.int32))
    e_sorted = e_flat[order]
    dest = pad_start[e_sorted] + jnp.arange(n_assign, dtype=jnp.int32) - cnt_start[e_sorted]
    n_blocks = -(-n_assign // EXPERT_BLOCK) + N_EXPERTS
    n_rows = n_blocks * EXPERT_BLOCK
    row_tok = jnp.full((n_rows,), T, jnp.int32).at[dest].set(tok_flat[order])
    row_gate = jnp.zeros((n_rows,), x.dtype).at[dest].set(gates.reshape(-1)[order].astype(x.dtype))
    block_e = jnp.minimum(
        jnp.searchsorted(pad_end, jnp.arange(n_blocks, dtype=jnp.int32) * EXPERT_BLOCK, side='right'),
        N_EXPERTS - 1)
    x_pad = jnp.concatenate([x, jnp.zeros((1, x.shape[1]), x.dtype)], axis=0)
    x_rows = x_pad[row_tok].reshape(n_blocks, EXPERT_BLOCK, x.shape[1])

    def expert_block(args):
        xb, e = args
        return _swiglu(xb, w1[e], w3[e], w2[e])

    y_rows = lax.map(expert_block, (x_rows, block_e)).reshape(n_rows, x.shape[1])
    y = jnp.zeros((T + 1, x.shape[1]), x.dtype).at[row_tok].add(y_rows * row_gate[:, None])
    return y[:T].reshape(shp)


def _layer(x, cond, attend, p, ffn):
    shift_a, scale_a, gate_a, shift_f, scale_f, gate_f = _modulation(cond, p['ada_w'], p['ada_b'])
    h = _ln_plain(x) * (1 + scale_a) + shift_a
    mix, k, v = _token_mixers(h, attend, p)
    x = _ln_affine(DEEPNORM_ALPHA * x + gate_a * mix, p['ln1_g'], p['ln1_b'])
    h = _ln_plain(x) * (1 + scale_f) + shift_f
    x = _ln_affine(DEEPNORM_ALPHA * x + gate_f * ffn(h), p['ln2_g'], p['ln2_b'])
    return x, k, v


def setup_inputs(seed: int = 0) -> dict:
    key = jax.random.key(seed)
    ks = jax.random.split(key, 28)
    n_dense = (DEPTH + 1) // 2
    n_moe = DEPTH // 2

    def nrm(k, shape, s):
        return jax.random.normal(k, shape, F32) * s

    return {
        'x_prompt': nrm(ks[0], (BATCH, SEQ, D_MODEL), 1.0),
        'x_sample': nrm(ks[1], (DEC_BATCH, DEC_SEQ, D_MODEL), 1.0),
        'cache_k': nrm(ks[2], (DEC_BATCH, DEPTH, PAST_LEN, N_KV_HEADS, HEAD_DIM), 1.0),
        'cache_v': nrm(ks[3], (DEC_BATCH, DEPTH, PAST_LEN, N_KV_HEADS, HEAD_DIM), 1.0),
        'c': nrm(ks[4], (DEC_BATCH, D_MODEL), 1.0),
        'c_ctx': nrm(ks[5], (D_MODEL,), 1.0),
        'ada_w': nrm(ks[6], (DEPTH, D_MODEL, 6 * D_MODEL), 0.5 * D_MODEL ** -0.5),
        'ada_b': nrm(ks[7], (DEPTH, 6 * D_MODEL), 0.02),
        'w_in': nrm(ks[8], (DEPTH, D_MODEL, IN_WIDTH), D_MODEL ** -0.5),
        'q_norm_g': 1.0 + nrm(ks[9], (DEPTH, HEAD_DIM), 0.02),
        'k_norm_g': 1.0 + nrm(ks[10], (DEPTH, HEAD_DIM), 0.02),
        'conv_w': nrm(ks[11], (DEPTH, 3, CONV_WIDTH), 3 ** -0.5),
        'conv_b': nrm(ks[12], (DEPTH, CONV_WIDTH), 0.02),
        'sgu_norm_g': 1.0 + nrm(ks[13], (DEPTH, SGU_WIDTH), 0.02),
        'sgu_w': nrm(ks[14], (DEPTH, SGU_HEADS, CHUNK, CHUNK), CHUNK ** -0.5),
        'sgu_b': 1.0 + nrm(ks[15], (DEPTH, SGU_HEADS, CHUNK), 0.02),
        'w_out': nrm(ks[16], (DEPTH, MIX_WIDTH, D_MODEL), MIX_WIDTH ** -0.5 * DEEPNORM_BETA),
        'ln1_g': 1.0 + nrm(ks[17], (DEPTH, D_MODEL), 0.02),
        'ln1_b': nrm(ks[18], (DEPTH, D_MODEL), 0.02),
        'ln2_g': 1.0 + nrm(ks[19], (DEPTH, D_MODEL), 0.02),
        'ln2_b': nrm(ks[20], (DEPTH, D_MODEL), 0.02),
        'ffn_w1': nrm(ks[21], (n_dense, D_MODEL, D_FF), D_MODEL ** -0.5),
        'ffn_w3': nrm(ks[22], (n_dense, D_MODEL, D_FF), D_MODEL ** -0.5),
        'ffn_w2': nrm(ks[23], (n_dense, D_FF, D_MODEL), D_FF ** -0.5 * DEEPNORM_BETA),
        'router_w': nrm(ks[24], (n_moe, D_MODEL, N_EXPERTS), D_MODEL ** -0.5),
        'moe_w1': nrm(ks[25], (n_moe, N_EXPERTS, D_MODEL, D_FF_EXPERT), D_MODEL ** -0.5),
        'moe_w3': nrm(ks[26], (n_moe, N_EXPERTS, D_MODEL, D_FF_EXPERT), D_MODEL ** -0.5),
        'moe_w2': nrm(ks[27], (n_moe, N_EXPERTS, D_FF_EXPERT, D_MODEL), D_FF_EXPERT ** -0.5 * DEEPNORM_BETA),
    }


def reference(x_prompt, x_sample, cache_k, cache_v, c, c_ctx, ada_w, ada_b, w_in, q_norm_g, k_norm_g,
              conv_w, conv_b, sgu_norm_g, sgu_w, sgu_b, w_out, ln1_g, ln1_b, ln2_g, ln2_b,
              ffn_w1, ffn_w3, ffn_w2, router_w, moe_w1, moe_w3, moe_w2):
    cos, sin = _axial_rope_tables(x_sample.shape[1])
    ctx_cond = c_ctx[None, :]
    y_p = x_prompt
    y_s = x_sample
    new_ks = []
    new_vs = []
    for l in range(DEPTH):
        p = {
            'ada_w': ada_w[l], 'ada_b': ada_b[l], 'w_in': w_in[l],
            'q_g': q_norm_g[l], 'k_g': k_norm_g[l],
            'conv_w': conv_w[l], 'conv_b': conv_b[l],
            'sgu_g': sgu_norm_g[l], 'sgu_w': sgu_w[l], 'sgu_b': sgu_b[l],
            'w_out': w_out[l],
            'ln1_g': ln1_g[l], 'ln1_b': ln1_b[l], 'ln2_g': ln2_g[l], 'ln2_b': ln2_b[l],
        }
        i = l // 2
        if l % 2 == 0:
            ffn = functools.partial(_swiglu, w1=ffn_w1[i], w3=ffn_w3[i], w2=ffn_w2[i])
        else:
            ffn = functools.partial(_moe_swiglu, router_w=router_w[i], w1=moe_w1[i], w3=moe_w3[i], w2=moe_w2[i])
        y_p, k_ctx, v_ctx = _layer(y_p, ctx_cond, _block_attention, p, ffn)
        new_ks.append(k_ctx)
        new_vs.append(v_ctx)
        lat_attend = functools.partial(_latent_attention, k_ctx=cache_k[:, l], v_ctx=cache_v[:, l], cos=cos, sin=sin)
        y_s, _, _ = _layer(y_s, c, lat_attend, p, ffn)
    new_k = jnp.stack(new_ks, axis=1)
    new_v = jnp.stack(new_vs, axis=1)
    return (y_p, y_s, new_k, new_v)
```

```python
import functools

import jax
import jax.numpy as jnp
from jax import lax
from jax.experimental import pallas as pl
from jax.experimental.pallas import tpu as pltpu

F32 = jnp.float32
BF16 = jnp.bfloat16
I32 = jnp.int32

D_MODEL = 1024
DEPTH = 2
GRID_W = 64
HEAD_DIM = 64
N_Q_HEADS = 8
N_KV_HEADS = 2
ATTN_WIDTH = N_Q_HEADS * HEAD_DIM
KV_WIDTH = N_KV_HEADS * HEAD_DIM
ATTN_SCALE = HEAD_DIM ** -0.5
ROPE_THETA = 10000.0
AXIS_ROT = HEAD_DIM // 2
CONV_WIDTH = 256
SGU_WIDTH = 256
SGU_HEADS = 4
CHUNK = 128
IN_WIDTH = 2048
D_FF = 2816
N_EXPERTS = 8
D_FF_EXPERT = 1408
EPS = 1e-6
DEEPNORM_ALPHA = (2 * DEPTH) ** 0.25

LANES = 128
SUBLANES = 8
ROW_BLOCK = 256
FFN_TILE = 512
MOE_CHUNK = 1024
MOE_GRAN = 128
MOE_BLOCK = 256
MOE_STEPS = 2 * MOE_CHUNK // MOE_BLOCK + N_EXPERTS
MOE_ROWS = 2 * MOE_CHUNK + N_EXPERTS * MOE_GRAN
VMEM_LIMIT = 56 * 1024 * 1024

_Q0, _K0, _V0, _CI0, _CB0, _CC0, _SU0, _SV0 = 0, 512, 640, 768, 1024, 1280, 1536, 1792


def _dot(a, b):
    return jnp.dot(a, b, preferred_element_type=F32)


def _dot_nt(a, b):
    return lax.dot_general(a, b, (((1,), (1,)), ((), ())), preferred_element_type=F32)


def _split(x):
    hi = x.astype(BF16)
    lo = (x - hi.astype(F32)).astype(BF16)
    return hi, lo


def _group_sum(x, ones_bd):
    hi, lo = _split(x)
    return _dot(hi, ones_bd) + _dot(lo, ones_bd)


def _standardize(x):
    mu = jnp.mean(x, axis=-1, keepdims=True)
    d = x - mu
    return d * lax.rsqrt(jnp.mean(d * d, axis=-1, keepdims=True) + EPS)


def _silu(x):
    return x / (1.0 + jnp.exp(-x))


def _modulation_kernel(cond_ref, w_ref, b_ref, o_ref):
    s_hi, s_lo = _split(_silu(cond_ref[...]))
    w_hi, w_lo = _split(w_ref[...])
    o_ref[...] = _dot(s_hi, w_hi) + _dot(s_hi, w_lo) + _dot(s_lo, w_hi) + b_ref[...]


def _modulation(cond, ada_w, ada_b):
    n_out = ada_w.shape[-1]
    tn = 1536
    return pl.pallas_call(
        _modulation_kernel,
        grid=(DEPTH, n_out // tn),
        in_specs=[
            pl.BlockSpec((SUBLANES, D_MODEL), lambda l, j: (0, 0)),
            pl.BlockSpec((None, D_MODEL, tn), lambda l, j: (l, 0, j)),
            pl.BlockSpec((None, 1, tn), lambda l, j: (l, 0, j)),
        ],
        out_specs=pl.BlockSpec((None, SUBLANES, tn), lambda l, j: (l, 0, j)),
        out_shape=jax.ShapeDtypeStruct((DEPTH, SUBLANES, n_out), F32),
        compiler_params=pltpu.CompilerParams(
            dimension_semantics=("arbitrary", "arbitrary"), vmem_limit_bytes=VMEM_LIMIT),
        name="modulation",
    )(cond, ada_w, ada_b.reshape(DEPTH, 1, n_out))


def _rope(x, cos, sin_signed):
    w = x.shape[1]
    lane = lax.broadcasted_iota(I32, x.shape, 1)
    first_half = (lane & 31) < 16
    partner = jnp.where(first_half, pltpu.roll(x, w - 16, 1), pltpu.roll(x, 16, 1))
    return x * cos + partner * sin_signed


def _head_variants(x):
    lane = lax.broadcasted_iota(I32, x.shape, 1)
    lo = lane < HEAD_DIM
    xr = pltpu.roll(x, HEAD_DIM, 1)
    zero = jnp.zeros_like(x)
    return (jnp.where(lo, x, zero).astype(BF16), jnp.where(lo, zero, xr).astype(BF16),
            jnp.where(lo, xr, zero).astype(BF16), jnp.where(lo, zero, x).astype(BF16))


def _mixer_kernel(*refs, seq, n_cache, rope):
    if rope:
        (x_ref, mod_ref, win_ref, wout_ref, qg_ref, kg_ref, ones_ref, convw_ref, convb_ref, sgug_ref,
         sguw_ref, sgub_ref, ln1g_ref, ln1b_ref, cos_ref, sin_ref, kc_ref, vc_ref, _, x1_ref,
         q_scr, kvar_scr, vvar_scr, u_scr, cb_scr, su_scr, vn_scr, mix_scr) = refs
    else:
        (x_ref, mod_ref, win_ref, wout_ref, qg_ref, kg_ref, ones_ref, convw_ref, convb_ref, sgug_ref,
         sguw_ref, sgub_ref, ln1g_ref, ln1b_ref, x1_ref, k_ref, v_ref,
         q_scr, kvar_scr, vvar_scr, u_scr, cb_scr, su_scr, vn_scr, mix_scr) = refs
    n_blocks = seq // ROW_BLOCK

    def rows_of(r):
        if isinstance(r, int):
            return pl.ds(r * ROW_BLOCK, ROW_BLOCK)
        return pl.ds(pl.multiple_of(r * ROW_BLOCK, ROW_BLOCK), ROW_BLOCK)

    def loop(body):
        if n_blocks == 1:
            body(0)
        else:
            def step(r, carry):
                body(r)
                return carry
            lax.fori_loop(0, n_blocks, step, 0)

    if n_cache:
        for i, var in enumerate(_head_variants(kc_ref[...])):
            kvar_scr[i, pl.ds(seq, n_cache), :] = var
        for i, var in enumerate(_head_variants(vc_ref[...])):
            vvar_scr[i, pl.ds(seq, n_cache), :] = var

    def project(r):
        rows = rows_of(r)
        x = x_ref[rows, :]
        h = _standardize(x) * (1.0 + mod_ref[1:2, :]) + mod_ref[0:1, :]
        z = _dot(h.astype(BF16), win_ref[...])
        ones_bd = ones_ref[...]
        zq = z[:, _Q0:_K0]
        q = zq * lax.rsqrt(_group_sum(zq * zq, ones_bd) * (1.0 / HEAD_DIM) + EPS) * qg_ref[...]
        zk = z[:, _K0:_V0]
        k = zk * lax.rsqrt(_group_sum(zk * zk, ones_bd[:KV_WIDTH, :KV_WIDTH]) * (1.0 / HEAD_DIM) + EPS) * kg_ref[...]
        v = z[:, _V0:_CI0]
        if rope:
            cos = cos_ref[rows, :]
            sin = sin_ref[rows, :]
            q = _rope(q, jnp.concatenate([cos] * 4, axis=1), jnp.concatenate([sin] * 4, axis=1))
            k = _rope(k, cos, sin)
        else:
            k_ref[rows, :] = k
            v_ref[rows, :] = v
        q_scr[rows, :] = (q * ATTN_SCALE).astype(BF16)
        for i, var in enumerate(_head_variants(k)):
            kvar_scr[i, rows, :] = var
        for i, var in enumerate(_head_variants(v)):
            vvar_scr[i, rows, :] = var
        u_scr[rows, :] = z[:, _CC0:_SU0] * z[:, _CI0:_CB0]
        cb_scr[rows, :] = z[:, _CB0:_CC0]
        su_scr[rows, :] = z[:, _SU0:_SV0]
        sv = z[:, _SV0:IN_WIDTH]
        ones_sgu = ones_bd[:SGU_WIDTH, :SGU_WIDTH]
        d = sv - _group_sum(sv, ones_sgu) * (1.0 / HEAD_DIM)
        vn = d * lax.rsqrt(_group_sum(d * d, ones_sgu) * (1.0 / HEAD_DIM) + EPS) * sgug_ref[...]
        vn_scr[rows, :] = vn.astype(BF16)

    loop(project)

    u = u_scr[...]
    pos = lax.broadcasted_iota(I32, u.shape, 0)
    up = jnp.where(pos == 0, 0.0, pltpu.roll(u, 1, 0))
    dn = jnp.where(pos == seq - 1, 0.0, pltpu.roll(u, seq - 1, 0))
    conv = up * convw_ref[0:1, :] + u * convw_ref[1:2, :] + dn * convw_ref[2:3, :] + convb_ref[...]
    mix_scr[:, ATTN_WIDTH:ATTN_WIDTH + CONV_WIDTH] = (cb_scr[...] * conv).astype(BF16)

    for n in range(seq // CHUNK):
        rows = pl.ds(n * CHUNK, CHUNK)
        vn = vn_scr[rows, :]
        lane = lax.broadcasted_iota(I32, vn.shape, 1)
        s = sgub_ref[...]
        for hd in range(SGU_HEADS):
            in_head = (lane >= hd * HEAD_DIM) & (lane < (hd + 1) * HEAD_DIM)
            masked = jnp.where(in_head, vn, jnp.zeros_like(vn))
            s = s + _dot(sguw_ref[hd], masked)
        mix_scr[rows, ATTN_WIDTH + CONV_WIDTH:] = (su_scr[rows, :] * s).astype(BF16)

    def attend(r):
        rows = rows_of(r)
        for pair in range(N_Q_HEADS // 2):
            qp = q_scr[rows, pair * LANES:(pair + 1) * LANES]
            kv = pair // (N_Q_HEADS // N_KV_HEADS // 2)
            acc = jnp.zeros((ROW_BLOCK, LANES), F32)
            for parity in range(2):
                s = _dot_nt(qp, kvar_scr[2 * kv + parity])
                p = jnp.exp(s - jnp.max(s, axis=1, keepdims=True))
                denom = jnp.sum(p, axis=1, keepdims=True)
                acc = acc + _dot(p.astype(BF16), vvar_scr[2 * kv + parity]) / denom
            mix_scr[rows, pair * LANES:(pair + 1) * LANES] = acc.astype(BF16)

    loop(attend)

    def finish(r):
        rows = rows_of(r)
        mix = _dot(mix_scr[rows, :], wout_ref[...])
        y = DEEPNORM_ALPHA * x_ref[rows, :] + mod_ref[2:3, :] * mix
        x1_ref[rows, :] = _standardize(y) * ln1g_ref[...] + ln1b_ref[...]

    loop(finish)


def _full(shape):
    n = len(shape)
    return pl.BlockSpec(shape, lambda *_: (0,) * n)


def _resident(shape):
    n = len(shape)
    return pl.BlockSpec(shape, lambda *_: (0,) * n, pipeline_mode=pl.Buffered(1))


def _mixer_scratch(seq, n_cache):
    nk = seq + n_cache
    return [
        pltpu.VMEM((seq, ATTN_WIDTH), BF16),
        pltpu.VMEM((4, nk, LANES), BF16),
        pltpu.VMEM((4, nk, LANES), BF16),
        pltpu.VMEM((seq, CONV_WIDTH), F32),
        pltpu.VMEM((seq, CONV_WIDTH), F32),
        pltpu.VMEM((seq, SGU_WIDTH), F32),
        pltpu.VMEM((seq, SGU_WIDTH), BF16),
        pltpu.VMEM((seq, D_MODEL), BF16),
    ]


def _weight_specs():
    return [
        _resident((D_MODEL, IN_WIDTH)), _resident((D_MODEL, D_MODEL)), _full((1, ATTN_WIDTH)), _full((1, KV_WIDTH)),
        _full((ATTN_WIDTH, ATTN_WIDTH)), _full((SUBLANES, CONV_WIDTH)), _full((1, CONV_WIDTH)),
        _full((1, SGU_WIDTH)), _full((SGU_HEADS, CHUNK, CHUNK)), _full((CHUNK, SGU_WIDTH)),
        _full((1, D_MODEL)), _full((1, D_MODEL)),
    ]


def _mixer_ctx(x_all, row0_blocks, n_seq, seq, mods, lw, n_tokens_all):
    kernel = functools.partial(_mixer_kernel, seq=seq, n_cache=0, rope=False)
    return pl.pallas_call(
        kernel,
        grid=(n_seq,),
        in_specs=[pl.BlockSpec((seq, D_MODEL), lambda i: (row0_blocks + i, 0)),
                  pl.BlockSpec((None, SUBLANES, D_MODEL), lambda i: (0, 0, 0))] + _weight_specs(),
        out_specs=[pl.BlockSpec((seq, D_MODEL), lambda i: (i, 0)),
                   pl.BlockSpec((seq, KV_WIDTH), lambda i: (i, 0)),
                   pl.BlockSpec((seq, KV_WIDTH), lambda i: (i, 0))],
        out_shape=[jax.ShapeDtypeStruct((n_tokens_all, D_MODEL), F32),
                   jax.ShapeDtypeStruct((n_seq * seq, KV_WIDTH), F32),
                   jax.ShapeDtypeStruct((n_seq * seq, KV_WIDTH), F32)],
        scratch_shapes=_mixer_scratch(seq, 0),
        compiler_params=pltpu.CompilerParams(dimension_semantics=("arbitrary",), vmem_limit_bytes=VMEM_LIMIT),
        name="mixer_ctx",
    )(x_all, mods, *lw)


def _mixer_lat(x_all, row0_blocks, n_seq, seq, mods, lw, cos, sin, kc, vc, x1_all):
    n_cache = kc.shape[1]
    out_blocks0 = (x1_all.shape[0] - n_seq * seq) // seq
    kernel = functools.partial(_mixer_kernel, seq=seq, n_cache=n_cache, rope=True)
    n_in = 2 + len(lw) + 4
    return pl.pallas_call(
        kernel,
        grid=(n_seq,),
        in_specs=[pl.BlockSpec((seq, D_MODEL), lambda b: (row0_blocks + b, 0)),
                  pl.BlockSpec((None, SUBLANES, D_MODEL), lambda b: (1 + b, 0, 0))] + _weight_specs() + [
                  _full((seq, LANES)), _full((seq, LANES)),
                  pl.BlockSpec((None, n_cache, KV_WIDTH), lambda b: (b, 0, 0)),
                  pl.BlockSpec((None, n_cache, KV_WIDTH), lambda b: (b, 0, 0)),
                  pl.BlockSpec(memory_space=pl.ANY)],
        out_specs=pl.BlockSpec((seq, D_MODEL), lambda b: (out_blocks0 + b, 0)),
        out_shape=jax.ShapeDtypeStruct(x1_all.shape, F32),
        scratch_shapes=_mixer_scratch(seq, n_cache),
        input_output_aliases={n_in: 0},
        compiler_params=pltpu.CompilerParams(dimension_semantics=("arbitrary",), vmem_limit_bytes=VMEM_LIMIT),
        name="mixer_lat",
    )(x_all, mods, *lw, cos, sin, kc, vc, x1_all)


def _ffn_kernel(x_ref, mod_ref, w1_ref, w3_ref, w2_ref, g_ref, b_ref, o_ref):
    x = x_ref[...]
    h = (_standardize(x) * (1.0 + mod_ref[4:5, :]) + mod_ref[3:4, :]).astype(BF16)
    half = D_FF // 2
    acc = jnp.zeros(x.shape, F32)
    for c in range(2):
        cols = slice(c * half, (c + 1) * half)
        a = _dot(h, w1_ref[:, cols])
        b = _dot(h, w3_ref[:, cols])
        acc = acc + _dot((_silu(a) * b).astype(BF16), w2_ref[cols, :])
    y = DEEPNORM_ALPHA * x + mod_ref[5:6, :] * acc
    o_ref[...] = _standardize(y) * g_ref[...] + b_ref[...]


def _group_of_tile(i, tile, n_ctx_tokens, lat_seq):
    n_ctx_tiles = n_ctx_tokens // tile
    return jnp.where(i < n_ctx_tiles, 0, 1 + (i - n_ctx_tiles) // (lat_seq // tile))


def _ffn_dense(x_all, mods, w1, w3, w2, g, b, n_ctx_tokens, lat_seq):
    n_tok = x_all.shape[0]
    grp = functools.partial(_group_of_tile, tile=FFN_TILE, n_ctx_tokens=n_ctx_tokens, lat_seq=lat_seq)
    return pl.pallas_call(
        _ffn_kernel,
        grid=(n_tok // FFN_TILE,),
        in_specs=[pl.BlockSpec((FFN_TILE, D_MODEL), lambda i: (i, 0)),
                  pl.BlockSpec((None, SUBLANES, D_MODEL), lambda i: (grp(i), 0, 0)),
                  _resident((D_MODEL, D_FF)), _resident((D_MODEL, D_FF)), _resident((D_FF, D_MODEL)),
                  _full((1, D_MODEL)), _full((1, D_MODEL))],
        out_specs=pl.BlockSpec((FFN_TILE, D_MODEL), lambda i: (i, 0)),
        out_shape=jax.ShapeDtypeStruct(x_all.shape, F32),
        compiler_params=pltpu.CompilerParams(dimension_semantics=("arbitrary",), vmem_limit_bytes=VMEM_LIMIT),
        name="ffn_dense",
    )(x_all, mods, w1, w3, w2, g, b)


def _router_kernel(x_ref, mod_ref, rw_ref, before_ref, dest_ref, gate_ref, cnt_ref):
    h = _standardize(x_ref[...]) * (1.0 + mod_ref[4:5, :]) + mod_ref[3:4, :]
    h_hi, h_lo = _split(h)
    w_hi, w_lo = _split(rw_ref[...])
    logits = _dot_nt(w_hi, h_hi) + _dot_nt(w_hi, h_lo) + _dot_nt(w_lo, h_hi)
    eid = lax.broadcasted_iota(I32, logits.shape, 0).astype(F32)
    m1 = jnp.max(logits, axis=0, keepdims=True)
    i1 = jnp.min(jnp.where(logits == m1, eid, float(N_EXPERTS)), axis=0, keepdims=True)
    oh1 = eid == i1
    rest = jnp.where(oh1, -jnp.inf, logits)
    m2 = jnp.max(rest, axis=0, keepdims=True)
    i2 = jnp.min(jnp.where(rest == m2, eid, float(N_EXPERTS)), axis=0, keepdims=True)
    oh2 = eid == i2
    e = jnp.exp(m2 - m1)
    gate_ref[0:1, :] = 1.0 / (1.0 + e)
    gate_ref[1:2, :] = e / (1.0 + e)
    sel = jnp.where(oh1 | oh2, 1.0, 0.0)
    rank = _dot(sel.astype(BF16), before_ref[...])
    eid_out = lax.broadcasted_iota(I32, cnt_ref.shape, 0).astype(F32)
    start = jnp.zeros(sel.shape, F32)
    cnt_out = jnp.zeros(cnt_ref.shape, F32)
    for ex in range(N_EXPERTS):
        cnt = jnp.sum(sel[ex:ex + 1, :], axis=1, keepdims=True)
        padded = jnp.ceil(cnt * (1.0 / MOE_GRAN)) * MOE_GRAN
        start = start + jnp.where(eid > ex, padded, 0.0)
        cnt_out = cnt_out + jnp.where(eid_out == ex, cnt, 0.0)
    row = start + rank
    dest_ref[0:1, :] = jnp.sum(jnp.where(oh1, row, 0.0), axis=0, keepdims=True).astype(I32)
    dest_ref[1:2, :] = jnp.sum(jnp.where(oh2, row, 0.0), axis=0, keepdims=True).astype(I32)
    cnt_ref[...] = cnt_out


def _router(x_all, mods, rw_t, n_ctx_tokens, lat_seq):
    n_chunks = x_all.shape[0] // MOE_CHUNK
    grp = functools.partial(_group_of_tile, tile=MOE_CHUNK, n_ctx_tokens=n_ctx_tokens, lat_seq=lat_seq)
    tok = jnp.arange(MOE_CHUNK, dtype=I32)
    before = (tok[:, None] < tok[None, :]).astype(BF16)
    return pl.pallas_call(
        _router_kernel,
        grid=(n_chunks,),
        in_specs=[pl.BlockSpec((MOE_CHUNK, D_MODEL), lambda c: (c, 0)),
                  pl.BlockSpec((None, SUBLANES, D_MODEL), lambda c: (grp(c), 0, 0)),
                  _full((N_EXPERTS, D_MODEL)), _full((MOE_CHUNK, MOE_CHUNK))],
        out_specs=[pl.BlockSpec((None, 2, MOE_CHUNK), lambda c: (c, 0, 0)),
                   pl.BlockSpec((None, 2, MOE_CHUNK), lambda c: (c, 0, 0)),
                   pl.BlockSpec((None, N_EXPERTS, LANES), lambda c: (c, 0, 0))],
        out_shape=[jax.ShapeDtypeStruct((n_chunks, 2, MOE_CHUNK), I32),
                   jax.ShapeDtypeStruct((n_chunks, 2, MOE_CHUNK), F32),
                   jax.ShapeDtypeStruct((n_chunks, N_EXPERTS, LANES), F32)],
        compiler_params=pltpu.CompilerParams(dimension_semantics=("arbitrary",), vmem_limit_bytes=VMEM_LIMIT),
        name="moe_router",
    )(x_all, mods, rw_t, before)


def _step_tables(cnt):
    padded = (cnt + MOE_GRAN - 1) // MOE_GRAN * MOE_GRAN
    end = jnp.cumsum(padded, axis=1)
    start = end - padded
    steps = (padded + MOE_BLOCK - 1) // MOE_BLOCK
    step_end = jnp.cumsum(steps, axis=1)
    j = jnp.arange(MOE_STEPS, dtype=I32)[None, :, None]
    e_of = jnp.sum((j >= step_end[:, None, :]).astype(I32), axis=2)
    used = e_of < N_EXPERTS
    e_idx = jnp.minimum(e_of, N_EXPERTS - 1)
    first = jnp.take_along_axis(step_end - steps, e_idx, axis=1)
    row0 = jnp.take_along_axis(start, e_idx, axis=1) + (j[:, :, 0] - first) * MOE_BLOCK
    n_rows = jnp.minimum(jnp.take_along_axis(end, e_idx, axis=1) - row0, MOE_BLOCK)
    n_rows = jnp.where(used, n_rows, 0)
    last_used = jnp.max(jnp.where(used, e_idx, 0), axis=1, keepdims=True)
    e_idx = jnp.where(used, e_idx, last_used)
    row0 = jnp.where(used, row0, 0)
    return (e_idx.reshape(-1).astype(I32), row0.reshape(-1).astype(I32), n_rows.reshape(-1).astype(I32))


def _moe_kernel(step_e, step_row0, step_n, x_ref, mod_ref, dest_ref, gate_ref, w1_ref, w3_ref, w2_ref,
                g_ref, b_ref, o_ref, tok_scr, rows_scr):
    c = pl.program_id(0)
    j = pl.program_id(1)
    n_lane_blocks = D_MODEL // LANES

    @pl.when(j == 0)
    def _dispatch():
        for r in range(MOE_CHUNK // ROW_BLOCK):
            rows = pl.ds(r * ROW_BLOCK, ROW_BLOCK)
            h = _standardize(x_ref[rows, :]) * (1.0 + mod_ref[4:5, :]) + mod_ref[3:4, :]
            for cc in range(n_lane_blocks):
                tok_scr[rows, cc, :] = h[:, cc * LANES:(cc + 1) * LANES]
        rows_scr[...] = jnp.zeros(rows_scr.shape, F32)

        def body(t, carry):
            row = tok_scr[t]
            rows_scr[dest_ref[0, 0, t]] = row
            rows_scr[dest_ref[0, 1, t]] = row
            return carry
        lax.fori_loop(0, MOE_CHUNK, body, 0, unroll=8)

    step = c * MOE_STEPS + j
    n_rows = step_n[step]
    row0 = pl.multiple_of(step_row0[step], MOE_GRAN)

    def experts(m):
        rows = pl.ds(row0, m)
        xin = jnp.concatenate([rows_scr[rows, cc, :] for cc in range(n_lane_blocks)], axis=1).astype(BF16)
        a = _dot(xin, w1_ref[...])
        b = _dot(xin, w3_ref[...])
        y = _dot((_silu(a) * b).astype(BF16), w2_ref[...])
        for cc in range(n_lane_blocks):
            rows_scr[rows, cc, :] = y[:, cc * LANES:(cc + 1) * LANES]

    for m in range(MOE_GRAN, MOE_BLOCK + 1, MOE_GRAN):
        pl.when(n_rows == m)(functools.partial(experts, m))

    @pl.when(j == MOE_STEPS - 1)
    def _combine():
        def body(t, carry):
            tok_scr[t] = (gate_ref[0, 0, t] * rows_scr[dest_ref[0, 0, t]]
                          + gate_ref[0, 1, t] * rows_scr[dest_ref[0, 1, t]])
            return carry
        lax.fori_loop(0, MOE_CHUNK, body, 0, unroll=8)
        for r in range(MOE_CHUNK // ROW_BLOCK):
            rows = pl.ds(r * ROW_BLOCK, ROW_BLOCK)
            ffn = jnp.concatenate([tok_scr[rows, cc, :] for cc in range(n_lane_blocks)], axis=1)
            y = DEEPNORM_ALPHA * x_ref[rows, :] + mod_ref[5:6, :] * ffn
            o_ref[rows, :] = _standardize(y) * g_ref[...] + b_ref[...]


def _ffn_moe(x_all, mods, rw_t, w1, w3, w2, g, b, n_ctx_tokens, lat_seq):
    n_chunks = x_all.shape[0] // MOE_CHUNK
    dest, gates, cnt = _router(x_all, mods, rw_t, n_ctx_tokens, lat_seq)
    step_e, step_row0, step_n = _step_tables(cnt[:, :, 0].astype(I32))
    grp = functools.partial(_group_of_tile, tile=MOE_CHUNK, n_ctx_tokens=n_ctx_tokens, lat_seq=lat_seq)

    def expert_of(c, j, se, sr, sn):
        return (se[c * MOE_STEPS + j], 0, 0)

    grid_spec = pltpu.PrefetchScalarGridSpec(
        num_scalar_prefetch=3,
        grid=(n_chunks, MOE_STEPS),
        in_specs=[pl.BlockSpec((MOE_CHUNK, D_MODEL), lambda c, j, *_: (c, 0)),
                  pl.BlockSpec((None, SUBLANES, D_MODEL), lambda c, j, *_: (grp(c), 0, 0)),
                  pl.BlockSpec((1, 2, MOE_CHUNK), lambda c, j, *_: (c, 0, 0), memory_space=pltpu.SMEM),
                  pl.BlockSpec((1, 2, MOE_CHUNK), lambda c, j, *_: (c, 0, 0), memory_space=pltpu.SMEM),
                  pl.BlockSpec((None, D_MODEL, D_FF_EXPERT), expert_of),
                  pl.BlockSpec((None, D_MODEL, D_FF_EXPERT), expert_of),
                  pl.BlockSpec((None, D_FF_EXPERT, D_MODEL), expert_of),
                  pl.BlockSpec((1, D_MODEL), lambda c, j, *_: (0, 0)),
                  pl.BlockSpec((1, D_MODEL), lambda c, j, *_: (0, 0))],
        out_specs=pl.BlockSpec((MOE_CHUNK, D_MODEL), lambda c, j, *_: (c, 0)),
        scratch_shapes=[pltpu.VMEM((MOE_CHUNK, SUBLANES, LANES), F32),
                        pltpu.VMEM((MOE_ROWS, SUBLANES, LANES), F32)],
    )
    return pl.pallas_call(
        _moe_kernel,
        grid_spec=grid_spec,
        out_shape=jax.ShapeDtypeStruct(x_all.shape, F32),
        compiler_params=pltpu.CompilerParams(
            dimension_semantics=("arbitrary", "arbitrary"), vmem_limit_bytes=VMEM_LIMIT),
        name="moe_experts",
    )(step_e, step_row0, step_n, x_all, mods, dest, gates, w1, w3, w2, g, b)


def _rope_tables(n_tokens):
    t = jnp.arange(n_tokens, dtype=F32)
    row = jnp.floor(t / GRID_W)
    col = t - row * GRID_W
    inv_freq = ROPE_THETA ** (-jnp.arange(0, AXIS_ROT, 2, dtype=F32) / AXIS_ROT)
    ang_r = row[:, None] * inv_freq
    ang_c = col[:, None] * inv_freq
    cos = jnp.concatenate([jnp.cos(ang_r), jnp.cos(ang_r), jnp.cos(ang_c), jnp.cos(ang_c)], axis=1)
    sin = jnp.concatenate([-jnp.sin(ang_r), jnp.sin(ang_r), -jnp.sin(ang_c), jnp.sin(ang_c)], axis=1)
    return jnp.tile(cos, (1, 2)), jnp.tile(sin, (1, 2))


def kernel(x_prompt, x_sample, cache_k, cache_v, c, c_ctx, ada_w, ada_b, w_in, q_norm_g, k_norm_g, conv_w, conv_b, sgu_norm_g, sgu_w, sgu_b, w_out, ln1_g, ln1_b, ln2_g, ln2_b, ffn_w1, ffn_w3, ffn_w2, router_w, moe_w1, moe_w3, moe_w2):
    batch, seq, _ = x_prompt.shape
    dec_batch, dec_seq, _ = x_sample.shape
    past_len = cache_k.shape[2]
    n_ctx = batch * seq
    n_lat = dec_batch * dec_seq
    n_all = n_ctx + n_lat
    assert 1 + dec_batch <= SUBLANES and seq == ROW_BLOCK and dec_seq % ROW_BLOCK == 0
    assert n_ctx % MOE_CHUNK == 0 and dec_seq % MOE_CHUNK == 0 and n_ctx % dec_seq == 0

    cond = jnp.zeros((SUBLANES, D_MODEL), F32).at[0].set(c_ctx).at[1:1 + dec_batch].set(c)
    mod = _modulation(cond, ada_w, ada_b)
    mod = mod.reshape(DEPTH, SUBLANES, 6, D_MODEL)[:, :1 + dec_batch]
    mod = jnp.pad(mod, ((0, 0), (0, 0), (0, SUBLANES - 6), (0, 0)))

    lane_id = jnp.arange(ATTN_WIDTH, dtype=I32) // HEAD_DIM
    ones_bd = (lane_id[:, None] == lane_id[None, :]).astype(BF16)
    cos, sin = _rope_tables(dec_seq)

    xp = x_prompt.reshape(n_ctx, D_MODEL)
    xs = x_sample.reshape(n_lat, D_MODEL)
    x_all = None
    new_ks, new_vs = [], []
    for l in range(DEPTH):
        lw = (
            w_in[l].astype(BF16), w_out[l].astype(BF16),
            jnp.tile(q_norm_g[l], N_Q_HEADS)[None, :], jnp.tile(k_norm_g[l], N_KV_HEADS)[None, :],
            ones_bd,
            jnp.pad(conv_w[l], ((0, SUBLANES - 3), (0, 0))), conv_b[l][None, :],
            sgu_norm_g[l][None, :], sgu_w[l].astype(BF16),
            jnp.repeat(sgu_b[l].T, HEAD_DIM, axis=1),
            ln1_g[l][None, :], ln1_b[l][None, :],
        )
        kc = cache_k[:, l].reshape(dec_batch, past_len, KV_WIDTH)
        vc = cache_v[:, l].reshape(dec_batch, past_len, KV_WIDTH)
        if l == 0:
            x1, k_ctx, v_ctx = _mixer_ctx(xp, 0, batch, seq, mod[l], lw, n_all)
            x1 = _mixer_lat(xs, 0, dec_batch, dec_seq, mod[l], lw, cos, sin, kc, vc, x1)
        else:
            x1, k_ctx, v_ctx = _mixer_ctx(x_all, 0, batch, seq, mod[l], lw, n_all)
            x1 = _mixer_lat(x_all, n_ctx // dec_seq, dec_batch, dec_seq, mod[l], lw, cos, sin, kc, vc, x1)
        new_ks.append(k_ctx.reshape(batch, seq, N_KV_HEADS, HEAD_DIM))
        new_vs.append(v_ctx.reshape(batch, seq, N_KV_HEADS, HEAD_DIM))
        i = l // 2
        g2, b2 = ln2_g[l][None, :], ln2_b[l][None, :]
        if l % 2 == 0:
            x_all = _ffn_dense(x1, mod[l], ffn_w1[i].astype(BF16), ffn_w3[i].astype(BF16),
                               ffn_w2[i].astype(BF16), g2, b2, n_ctx, dec_seq)
        else:
            x_all = _ffn_moe(x1, mod[l], router_w[i].T, moe_w1[i].astype(BF16), moe_w3[i].astype(BF16),
                             moe_w2[i].astype(BF16), g2, b2, n_ctx, dec_seq)
    y_p = x_all[:n_ctx].reshape(batch, seq, D_MODEL)
    y_s = x_all[n_ctx:].reshape(dec_batch, dec_seq, D_MODEL)
    return (y_p, y_s, jnp.stack(new_ks, axis=1), jnp.stack(new_vs, axis=1))
```

```python
import functools

import jax
import jax.numpy as jnp
from jax import lax
from jax.experimental import pallas as pl
from jax.experimental.pallas import tpu as pltpu

F32 = jnp.float32
BF16 = jnp.bfloat16
I32 = jnp.int32

D_MODEL = 1024
DEPTH = 2
GRID_W = 64
HEAD_DIM = 64
N_Q_HEADS = 8
N_KV_HEADS = 2
ATTN_WIDTH = N_Q_HEADS * HEAD_DIM
KV_WIDTH = N_KV_HEADS * HEAD_DIM
ATTN_SCALE = HEAD_DIM ** -0.5
ROPE_THETA = 10000.0
AXIS_ROT = HEAD_DIM // 2
CONV_WIDTH = 256
SGU_WIDTH = 256
SGU_HEADS = 4
CHUNK = 128
IN_WIDTH = 2048
D_FF = 2816
N_EXPERTS = 8
D_FF_EXPERT = 1408
EPS = 1e-6
DEEPNORM_ALPHA = (2 * DEPTH) ** 0.25

LANES = 128
SUBLANES = 8
ROW_BLOCK = 256
FFN_TILE = 512
MOE_CHUNK = 1024
MOE_GRAN = 128
MOE_BLOCK = 256
MOE_STEPS = 2 * MOE_CHUNK // MOE_BLOCK + N_EXPERTS
MOE_ROWS = 2 * MOE_CHUNK + N_EXPERTS * MOE_GRAN
VMEM_LIMIT = 56 * 1024 * 1024

_Q0, _K0, _V0, _CI0, _CB0, _CC0, _SU0, _SV0 = 0, 512, 640, 768, 1024, 1280, 1536, 1792


def _dot(a, b):
    return jnp.dot(a, b, preferred_element_type=F32)


def _dot_nt(a, b):
    return lax.dot_general(a, b, (((1,), (1,)), ((), ())), preferred_element_type=F32)


def _split(x):
    hi = x.astype(BF16)
    lo = (x - hi.astype(F32)).astype(BF16)
    return hi, lo


def _group_sum(x, ones_bd):
    hi, lo = _split(x)
    return _dot(hi, ones_bd) + _dot(lo, ones_bd)


def _standardize(x):
    mu = jnp.mean(x, axis=-1, keepdims=True)
    d = x - mu
    return d * lax.rsqrt(jnp.mean(d * d, axis=-1, keepdims=True) + EPS)


def _silu(x):
    return x / (1.0 + jnp.exp(-x))


def _modulation_kernel(cond_ref, w_ref, b_ref, o_ref):
    s_hi, s_lo = _split(_silu(cond_ref[...]))
    w_hi, w_lo = _split(w_ref[...])
    o_ref[...] = _dot(s_hi, w_hi) + _dot(s_hi, w_lo) + _dot(s_lo, w_hi) + b_ref[...]


def _modulation(cond, ada_w, ada_b):
    n_out = ada_w.shape[-1]
    tn = 1536
    return pl.pallas_call(
        _modulation_kernel,
        grid=(DEPTH, n_out // tn),
        in_specs=[
            pl.BlockSpec((SUBLANES, D_MODEL), lambda l, j: (0, 0)),
            pl.BlockSpec((None, D_MODEL, tn), lambda l, j: (l, 0, j)),
            pl.BlockSpec((None, 1, tn), lambda l, j: (l, 0, j)),
        ],
        out_specs=pl.BlockSpec((None, SUBLANES, tn), lambda l, j: (l, 0, j)),
        out_shape=jax.ShapeDtypeStruct((DEPTH, SUBLANES, n_out), F32),
        compiler_params=pltpu.CompilerParams(
            dimension_semantics=("arbitrary", "arbitrary"), vmem_limit_bytes=VMEM_LIMIT),
        name="modulation",
    )(cond, ada_w, ada_b.reshape(DEPTH, 1, n_out))


def _rope(x, cos, sin_signed):
    w = x.shape[1]
    lane = lax.broadcasted_iota(I32, x.shape, 1)
    first_half = (lane & 31) < 16
    partner = jnp.where(first_half, pltpu.roll(x, w - 16, 1), pltpu.roll(x, 16, 1))
    return x * cos + partner * sin_signed


def _head_variants(x):
    lane = lax.broadcasted_iota(I32, x.shape, 1)
    lo = lane < HEAD_DIM
    xr = pltpu.roll(x, HEAD_DIM, 1)
    zero = jnp.zeros_like(x)
    return (jnp.where(lo, x, zero).astype(BF16), jnp.where(lo, zero, xr).astype(BF16),
            jnp.where(lo, xr, zero).astype(BF16), jnp.where(lo, zero, x).astype(BF16))


def _mixer_kernel(*refs, seq, n_cache, rope):
    if rope:
        (x_ref, mod_ref, win_ref, wout_ref, qg_ref, kg_ref, ones_ref, convw_ref, convb_ref, sgug_ref,
         sguw_ref, sgub_ref, ln1g_ref, ln1b_ref, cos_ref, sin_ref, kc_ref, vc_ref, _, x1_ref,
         q_scr, kvar_scr, vvar_scr, u_scr, cb_scr, su_scr, vn_scr, mix_scr) = refs
    else:
        (x_ref, mod_ref, win_ref, wout_ref, qg_ref, kg_ref, ones_ref, convw_ref, convb_ref, sgug_ref,
         sguw_ref, sgub_ref, ln1g_ref, ln1b_ref, x1_ref, k_ref, v_ref,
         q_scr, kvar_scr, vvar_scr, u_scr, cb_scr, su_scr, vn_scr, mix_scr) = refs
    n_blocks = seq // ROW_BLOCK

    def rows_of(r):
        if isinstance(r, int):
            return pl.ds(r * ROW_BLOCK, ROW_BLOCK)
        return pl.ds(pl.multiple_of(r * ROW_BLOCK, ROW_BLOCK), ROW_BLOCK)

    def loop(body):
        if n_blocks == 1:
            body(0)
        else:
            def step(r, carry):
                body(r)
                return carry
            lax.fori_loop(0, n_blocks, step, 0)

    if n_cache:
        for i, var in enumerate(_head_variants(kc_ref[...])):
            kvar_scr[i, pl.ds(seq, n_cache), :] = var
        for i, var in enumerate(_head_variants(vc_ref[...])):
            vvar_scr[i, pl.ds(seq, n_cache), :] = var

    def project(r):
        rows = rows_of(r)
        x = x_ref[rows, :]
        h = _standardize(x) * (1.0 + mod_ref[1:2, :]) + mod_ref[0:1, :]
        z = _dot(h.astype(BF16), win_ref[...])
        ones_bd = ones_ref[...]
        zq = z[:, _Q0:_K0]
        q = zq * lax.rsqrt(_group_sum(zq * zq, ones_bd) * (1.0 / HEAD_DIM) + EPS) * qg_ref[...]
        zk = z[:, _K0:_V0]
        k = zk * lax.rsqrt(_group_sum(zk * zk, ones_bd[:KV_WIDTH, :KV_WIDTH]) * (1.0 / HEAD_DIM) + EPS) * kg_ref[...]
        v = z[:, _V0:_CI0]
        if rope:
            cos = cos_ref[rows, :]
            sin = sin_ref[rows, :]
            q = _rope(q, jnp.concatenate([cos] * 4, axis=1), jnp.concatenate([sin] * 4, axis=1))
            k = _rope(k, cos, sin)
        else:
            k_ref[rows, :] = k
            v_ref[rows, :] = v
        q_scr[rows, :] = (q * ATTN_SCALE).astype(BF16)
        for i, var in enumerate(_head_variants(k)):
            kvar_scr[i, rows, :] = var
        for i, var in enumerate(_head_variants(v)):
            vvar_scr[i, rows, :] = var
        u_scr[rows, :] = z[:, _CC0:_SU0] * z[:, _CI0:_CB0]
        cb_scr[rows, :] = z[:, _CB0:_CC0]
        su_scr[rows, :] = z[:, _SU0:_SV0]
        sv = z[:, _SV0:IN_WIDTH]
        ones_sgu = ones_bd[:SGU_WIDTH, :SGU_WIDTH]
        d = sv - _group_sum(sv, ones_sgu) * (1.0 / HEAD_DIM)
        vn = d * lax.rsqrt(_group_sum(d * d, ones_sgu) * (1.0 / HEAD_DIM) + EPS) * sgug_ref[...]
        vn_scr[rows, :] = vn.astype(BF16)

    loop(project)

    u = u_scr[...]
    pos = lax.broadcasted_iota(I32, u.shape, 0)
    up = jnp.where(pos == 0, 0.0, pltpu.roll(u, 1, 0))
    dn = jnp.where(pos == seq - 1, 0.0, pltpu.roll(u, seq - 1, 0))
    conv = up * convw_ref[0:1, :] + u * convw_ref[1:2, :] + dn * convw_ref[2:3, :] + convb_ref[...]
    mix_scr[:, ATTN_WIDTH:ATTN_WIDTH + CONV_WIDTH] = (cb_scr[...] * conv).astype(BF16)

    for n in range(seq // CHUNK):
        rows = pl.ds(n * CHUNK, CHUNK)
        vn = vn_scr[rows, :]
        lane = lax.broadcasted_iota(I32, vn.shape, 1)
        s = sgub_ref[...]
        for hd in range(SGU_HEADS):
            in_head = (lane >= hd * HEAD_DIM) & (lane < (hd + 1) * HEAD_DIM)
            masked = jnp.where(in_head, vn, jnp.zeros_like(vn))
            s = s + _dot(sguw_ref[hd], masked)
        mix_scr[rows, ATTN_WIDTH + CONV_WIDTH:] = (su_scr[rows, :] * s).astype(BF16)

    def attend(r):
        rows = rows_of(r)
        for pair in range(N_Q_HEADS // 2):
            qp = q_scr[rows, pair * LANES:(pair + 1) * LANES]
            kv = pair // (N_Q_HEADS // N_KV_HEADS // 2)
            acc = jnp.zeros((ROW_BLOCK, LANES), F32)
            for parity in range(2):
                s = _dot_nt(qp, kvar_scr[2 * kv + parity])
                p = jnp.exp(s - jnp.max(s, axis=1, keepdims=True))
                denom = jnp.sum(p, axis=1, keepdims=True)
                acc = acc + _dot(p.astype(BF16), vvar_scr[2 * kv + parity]) / denom
            mix_scr[rows, pair * LANES:(pair + 1) * LANES] = acc.astype(BF16)

    loop(attend)

    def finish(r):
        rows = rows_of(r)
        mix = _dot(mix_scr[rows, :], wout_ref[...])
        y = DEEPNORM_ALPHA * x_ref[rows, :] + mod_ref[2:3, :] * mix
        x1_ref[rows, :] = _standardize(y) * ln1g_ref[...] + ln1b_ref[...]

    loop(finish)


def _full(shape):
    n = len(shape)
    return pl.BlockSpec(shape, lambda *_: (0,) * n)


def _resident(shape):
    n = len(shape)
    return pl.BlockSpec(shape, lambda *_: (0,) * n, pipeline_mode=pl.Buffered(1))


def _mixer_scratch(seq, n_cache):
    nk = seq + n_cache
    return [
        pltpu.VMEM((seq, ATTN_WIDTH), BF16),
        pltpu.VMEM((4, nk, LANES), BF16),
        pltpu.VMEM((4, nk, LANES), BF16),
        pltpu.VMEM((seq, CONV_WIDTH), F32),
        pltpu.VMEM((seq, CONV_WIDTH), F32),
        pltpu.VMEM((seq, SGU_WIDTH), F32),
        pltpu.VMEM((seq, SGU_WIDTH), BF16),
        pltpu.VMEM((seq, D_MODEL), BF16),
    ]


def _weight_specs():
    return [
        _resident((D_MODEL, IN_WIDTH)), _resident((D_MODEL, D_MODEL)), _full((1, ATTN_WIDTH)), _full((1, KV_WIDTH)),
        _full((ATTN_WIDTH, ATTN_WIDTH)), _full((SUBLANES, CONV_WIDTH)), _full((1, CONV_WIDTH)),
        _full((1, SGU_WIDTH)), _full((SGU_HEADS, CHUNK, CHUNK)), _full((CHUNK, SGU_WIDTH)),
        _full((1, D_MODEL)), _full((1, D_MODEL)),
    ]


def _mixer_ctx(x_all, row0_blocks, n_seq, seq, mods, lw, n_tokens_all):
    kernel = functools.partial(_mixer_kernel, seq=seq, n_cache=0, rope=False)
    return pl.pallas_call(
        kernel,
        grid=(n_seq,),
        in_specs=[pl.BlockSpec((seq, D_MODEL), lambda i: (row0_blocks + i, 0)),
                  pl.BlockSpec((None, SUBLANES, D_MODEL), lambda i: (0, 0, 0))] + _weight_specs(),
        out_specs=[pl.BlockSpec((seq, D_MODEL), lambda i: (i, 0)),
                   pl.BlockSpec((seq, KV_WIDTH), lambda i: (i, 0)),
                   pl.BlockSpec((seq, KV_WIDTH), lambda i: (i, 0))],
        out_shape=[jax.ShapeDtypeStruct((n_tokens_all, D_MODEL), F32),
                   jax.ShapeDtypeStruct((n_seq * seq, KV_WIDTH), F32),
                   jax.ShapeDtypeStruct((n_seq * seq, KV_WIDTH), F32)],
        scratch_shapes=_mixer_scratch(seq, 0),
        compiler_params=pltpu.CompilerParams(dimension_semantics=("arbitrary",), vmem_limit_bytes=VMEM_LIMIT),
        name="mixer_ctx",
    )(x_all, mods, *lw)


def _mixer_lat(x_all, row0_blocks, n_seq, seq, mods, lw, cos, sin, kc, vc, x1_all):
    n_cache = kc.shape[1]
    out_blocks0 = (x1_all.shape[0] - n_seq * seq) // seq
    kernel = functools.partial(_mixer_kernel, seq=seq, n_cache=n_cache, rope=True)
    n_in = 2 + len(lw) + 4
    return pl.pallas_call(
        kernel,
        grid=(n_seq,),
        in_specs=[pl.BlockSpec((seq, D_MODEL), lambda b: (row0_blocks + b, 0)),
                  pl.BlockSpec((None, SUBLANES, D_MODEL), lambda b: (1 + b, 0, 0))] + _weight_specs() + [
                  _full((seq, LANES)), _full((seq, LANES)),
                  pl.BlockSpec((None, n_cache, KV_WIDTH), lambda b: (b, 0, 0)),
                  pl.BlockSpec((None, n_cache, KV_WIDTH), lambda b: (b, 0, 0)),
                  pl.BlockSpec(memory_space=pl.ANY)],
        out_specs=pl.BlockSpec((seq, D_MODEL), lambda b: (out_blocks0 + b, 0)),
        out_shape=jax.ShapeDtypeStruct(x1_all.shape, F32),
        scratch_shapes=_mixer_scratch(seq, n_cache),
        input_output_aliases={n_in: 0},
        compiler_params=pltpu.CompilerParams(dimension_semantics=("arbitrary",), vmem_limit_bytes=VMEM_LIMIT),
        name="mixer_lat",
    )(x_all, mods, *lw, cos, sin, kc, vc, x1_all)


def _ffn_kernel(x_ref, mod_ref, w1_ref, w3_ref, w2_ref, g_ref, b_ref, o_ref):
    x = x_ref[...]
    h = (_standardize(x) * (1.0 + mod_ref[4:5, :]) + mod_ref[3:4, :]).astype(BF16)
    half = D_FF // 2
    acc = jnp.zeros(x.shape, F32)
    for c in range(2):
        cols = slice(c * half, (c + 1) * half)
        a = _dot(h, w1_ref[:, cols])
        b = _dot(h, w3_ref[:, cols])
        acc = acc + _dot((_silu(a) * b).astype(BF16), w2_ref[cols, :])
    y = DEEPNORM_ALPHA * x + mod_ref[5:6, :] * acc
    o_ref[...] = _standardize(y) * g_ref[...] + b_ref[...]


def _group_of_tile(i, tile, n_ctx_tokens, lat_seq):
    n_ctx_tiles = n_ctx_tokens // tile
    return jnp.where(i < n_ctx_tiles, 0, 1 + (i - n_ctx_tiles) // (lat_seq // tile))


def _ffn_dense(x_all, mods, w1, w3, w2, g, b, n_ctx_tokens, lat_seq):
    n_tok = x_all.shape[0]
    grp = functools.partial(_group_of_tile, tile=FFN_TILE, n_ctx_tokens=n_ctx_tokens, lat_seq=lat_seq)
    return pl.pallas_call(
        _ffn_kernel,
        grid=(n_tok // FFN_TILE,),
        in_specs=[pl.BlockSpec((FFN_TILE, D_MODEL), lambda i: (i, 0)),
                  pl.BlockSpec((None, SUBLANES, D_MODEL), lambda i: (grp(i), 0, 0)),
                  _resident((D_MODEL, D_FF)), _resident((D_MODEL, D_FF)), _resident((D_FF, D_MODEL)),
                  _full((1, D_MODEL)), _full((1, D_MODEL))],
        out_specs=pl.BlockSpec((FFN_TILE, D_MODEL), lambda i: (i, 0)),
        out_shape=jax.ShapeDtypeStruct(x_all.shape, F32),
        compiler_params=pltpu.CompilerParams(dimension_semantics=("arbitrary",), vmem_limit_bytes=VMEM_LIMIT),
        name="ffn_dense",
    )(x_all, mods, w1, w3, w2, g, b)


def _router_kernel(x_ref, mod_ref, rw_ref, before_ref, dest_ref, gate_ref, cnt_ref):
    h = _standardize(x_ref[...]) * (1.0 + mod_ref[4:5, :]) + mod_ref[3:4, :]
    h_hi, h_lo = _split(h)
    w_hi, w_lo = _split(rw_ref[...])
    logits = _dot_nt(w_hi, h_hi) + _dot_nt(w_hi, h_lo) + _dot_nt(w_lo, h_hi)
    eid = lax.broadcasted_iota(I32, logits.shape, 0).astype(F32)
    m1 = jnp.max(logits, axis=0, keepdims=True)
    i1 = jnp.min(jnp.where(logits == m1, eid, float(N_EXPERTS)), axis=0, keepdims=True)
    oh1 = eid == i1
    rest = jnp.where(oh1, -jnp.inf, logits)
    m2 = jnp.max(rest, axis=0, keepdims=True)
    i2 = jnp.min(jnp.where(rest == m2, eid, float(N_EXPERTS)), axis=0, keepdims=True)
    oh2 = eid == i2
    e = jnp.exp(m2 - m1)
    gate_ref[0:1, :] = 1.0 / (1.0 + e)
    gate_ref[1:2, :] = e / (1.0 + e)
    sel = jnp.where(oh1 | oh2, 1.0, 0.0)
    rank = _dot(sel.astype(BF16), before_ref[...])
    eid_out = lax.broadcasted_iota(I32, cnt_ref.shape, 0).astype(F32)
    start = jnp.zeros(sel.shape, F32)
    cnt_out = jnp.zeros(cnt_ref.shape, F32)
    for ex in range(N_EXPERTS):
        cnt = jnp.sum(sel[ex:ex + 1, :], axis=1, keepdims=True)
        padded = jnp.ceil(cnt * (1.0 / MOE_GRAN)) * MOE_GRAN
        start = start + jnp.where(eid > ex, padded, 0.0)
        cnt_out = cnt_out + jnp.where(eid_out == ex, cnt, 0.0)
    row = (start + rank) * SUBLANES
    dest_ref[0:1, :] = jnp.sum(jnp.where(oh1, row, 0.0), axis=0, keepdims=True).astype(I32)
    dest_ref[1:2, :] = jnp.sum(jnp.where(oh2, row, 0.0), axis=0, keepdims=True).astype(I32)
    cnt_ref[...] = cnt_out


def _router(x_all, mods, rw_t, n_ctx_tokens, lat_seq):
    n_chunks = x_all.shape[0] // MOE_CHUNK
    grp = functools.partial(_group_of_tile, tile=MOE_CHUNK, n_ctx_tokens=n_ctx_tokens, lat_seq=lat_seq)
    tok = jnp.arange(MOE_CHUNK, dtype=I32)
    before = (tok[:, None] < tok[None, :]).astype(BF16)
    return pl.pallas_call(
        _router_kernel,
        grid=(n_chunks,),
        in_specs=[pl.BlockSpec((MOE_CHUNK, D_MODEL), lambda c: (c, 0)),
                  pl.BlockSpec((None, SUBLANES, D_MODEL), lambda c: (grp(c), 0, 0)),
                  _full((N_EXPERTS, D_MODEL)), _full((MOE_CHUNK, MOE_CHUNK))],
        out_specs=[pl.BlockSpec((None, 2, MOE_CHUNK), lambda c: (c, 0, 0)),
                   pl.BlockSpec((None, 2, MOE_CHUNK), lambda c: (c, 0, 0)),
                   pl.BlockSpec((None, N_EXPERTS, LANES), lambda c: (c, 0, 0))],
        out_shape=[jax.ShapeDtypeStruct((n_chunks, 2, MOE_CHUNK), I32),
                   jax.ShapeDtypeStruct((n_chunks, 2, MOE_CHUNK), F32),
                   jax.ShapeDtypeStruct((n_chunks, N_EXPERTS, LANES), F32)],
        compiler_params=pltpu.CompilerParams(dimension_semantics=("arbitrary",), vmem_limit_bytes=VMEM_LIMIT),
        name="moe_router",
    )(x_all, mods, rw_t, before)


def _step_tables(cnt):
    padded = (cnt + MOE_GRAN - 1) // MOE_GRAN * MOE_GRAN
    end = jnp.cumsum(padded, axis=1)
    start = end - padded
    steps = (padded + MOE_BLOCK - 1) // MOE_BLOCK
    step_end = jnp.cumsum(steps, axis=1)
    j = jnp.arange(MOE_STEPS, dtype=I32)[None, :, None]
    e_of = jnp.sum((j >= step_end[:, None, :]).astype(I32), axis=2)
    used = e_of < N_EXPERTS
    e_idx = jnp.minimum(e_of, N_EXPERTS - 1)
    k = j[:, :, 0] - jnp.take_along_axis(step_end - steps, e_idx, axis=1)
    n_steps = jnp.take_along_axis(steps, e_idx, axis=1)
    tail = jnp.take_along_axis(padded, e_idx, axis=1) - (n_steps - 1) * MOE_BLOCK
    row0 = jnp.take_along_axis(start, e_idx, axis=1) + jnp.where(k == 0, 0, tail + (k - 1) * MOE_BLOCK)
    n_rows = jnp.where(used, jnp.where(k == 0, tail, MOE_BLOCK), 0)
    last_used = jnp.max(jnp.where(used, e_idx, 0), axis=1, keepdims=True)
    e_idx = jnp.where(used, e_idx, last_used)
    row0 = jnp.where(used, row0, 0)
    return (e_idx.reshape(-1).astype(I32), row0.reshape(-1).astype(I32), n_rows.reshape(-1).astype(I32))


def _token_rows(first_row):
    return pl.ds(pl.multiple_of(first_row, SUBLANES), SUBLANES)


def _store_token_major(ref, tok0, val):
    for cc in range(D_MODEL // LANES):
        ref[pl.ds(tok0 * SUBLANES + cc, val.shape[0], stride=SUBLANES), :] = val[:, cc * LANES:(cc + 1) * LANES]


def _load_token_major(ref, tok0, n):
    return jnp.concatenate(
        [ref[pl.ds(tok0 * SUBLANES + cc, n, stride=SUBLANES), :] for cc in range(D_MODEL // LANES)], axis=1)


def _moe_kernel(step_e, step_row0, step_n, x_ref, mod_ref, dest_ref, gate_ref, w1_ref, w3_ref, w2_ref,
                g_ref, b_ref, o_ref, tok_scr, rows_scr):
    c = pl.program_id(0)
    j = pl.program_id(1)

    @pl.when((c == 0) & (j == 0))
    def _init():
        rows_scr[...] = jnp.zeros(rows_scr.shape, F32)

    @pl.when(j == 0)
    def _dispatch():
        for r in range(MOE_CHUNK // ROW_BLOCK):
            h = (_standardize(x_ref[pl.ds(r * ROW_BLOCK, ROW_BLOCK), :]) * (1.0 + mod_ref[4:5, :])
                 + mod_ref[3:4, :])
            _store_token_major(tok_scr, r * ROW_BLOCK, h)

        def body(t, carry):
            row = tok_scr[_token_rows(t * SUBLANES), :]
            rows_scr[_token_rows(dest_ref[t]), :] = row
            rows_scr[_token_rows(dest_ref[MOE_CHUNK + t]), :] = row
            return carry
        lax.fori_loop(0, MOE_CHUNK, body, 0, unroll=8)

    step = c * MOE_STEPS + j
    n_rows = step_n[step]
    row0 = step_row0[step]

    def experts(m):
        xin = _load_token_major(rows_scr, row0, m).astype(BF16)
        a = _dot(xin, w1_ref[...])
        b = _dot(xin, w3_ref[...])
        y = _dot((_silu(a) * b).astype(BF16), w2_ref[...])
        _store_token_major(rows_scr, row0, y)

    for m in range(MOE_GRAN, MOE_BLOCK + 1, MOE_GRAN):
        pl.when(n_rows == m)(functools.partial(experts, m))

    @pl.when(j == MOE_STEPS - 1)
    def _combine():
        def body(t, carry):
            y0 = rows_scr[_token_rows(dest_ref[t]), :]
            y1 = rows_scr[_token_rows(dest_ref[MOE_CHUNK + t]), :]
            tok_scr[_token_rows(t * SUBLANES), :] = gate_ref[t] * y0 + gate_ref[MOE_CHUNK + t] * y1
            return carry
        lax.fori_loop(0, MOE_CHUNK, body, 0, unroll=8)
        for r in range(MOE_CHUNK // ROW_BLOCK):
            rows = pl.ds(r * ROW_BLOCK, ROW_BLOCK)
            ffn = _load_token_major(tok_scr, r * ROW_BLOCK, ROW_BLOCK)
            y = DEEPNORM_ALPHA * x_ref[rows, :] + mod_ref[5:6, :] * ffn
            o_ref[rows, :] = _standardize(y) * g_ref[...] + b_ref[...]


def _ffn_moe(x_all, mods, rw_t, w1, w3, w2, g, b, n_ctx_tokens, lat_seq):
    n_chunks = x_all.shape[0] // MOE_CHUNK
    dest, gates, cnt = _router(x_all, mods, rw_t, n_ctx_tokens, lat_seq)
    step_e, step_row0, step_n = _step_tables(cnt[:, :, 0].astype(I32))
    grp = functools.partial(_group_of_tile, tile=MOE_CHUNK, n_ctx_tokens=n_ctx_tokens, lat_seq=lat_seq)

    def expert_of(c, j, se, sr, sn):
        return (se[c * MOE_STEPS + j], 0, 0)

    grid_spec = pltpu.PrefetchScalarGridSpec(
        num_scalar_prefetch=3,
        grid=(n_chunks, MOE_STEPS),
        in_specs=[pl.BlockSpec((MOE_CHUNK, D_MODEL), lambda c, j, *_: (c, 0)),
                  pl.BlockSpec((None, SUBLANES, D_MODEL), lambda c, j, *_: (grp(c), 0, 0)),
                  pl.BlockSpec((2 * MOE_CHUNK,), lambda c, j, *_: (c,), memory_space=pltpu.SMEM),
                  pl.BlockSpec((2 * MOE_CHUNK,), lambda c, j, *_: (c,), memory_space=pltpu.SMEM),
                  pl.BlockSpec((None, D_MODEL, D_FF_EXPERT), expert_of),
                  pl.BlockSpec((None, D_MODEL, D_FF_EXPERT), expert_of),
                  pl.BlockSpec((None, D_FF_EXPERT, D_MODEL), expert_of),
                  pl.BlockSpec((1, D_MODEL), lambda c, j, *_: (0, 0)),
                  pl.BlockSpec((1, D_MODEL), lambda c, j, *_: (0, 0))],
        out_specs=pl.BlockSpec((MOE_CHUNK, D_MODEL), lambda c, j, *_: (c, 0)),
        scratch_shapes=[pltpu.VMEM((MOE_CHUNK * SUBLANES, LANES), F32),
                        pltpu.VMEM((MOE_ROWS * SUBLANES, LANES), F32)],
    )
    return pl.pallas_call(
        _moe_kernel,
        grid_spec=grid_spec,
        out_shape=jax.ShapeDtypeStruct(x_all.shape, F32),
        compiler_params=pltpu.CompilerParams(
            dimension_semantics=("arbitrary", "arbitrary"), vmem_limit_bytes=VMEM_LIMIT),
        name="moe_experts",
    )(step_e, step_row0, step_n, x_all, mods, dest.reshape(-1), gates.reshape(-1), w1, w3, w2, g, b)


def _rope_tables(n_tokens):
    t = jnp.arange(n_tokens, dtype=F32)
    row = jnp.floor(t / GRID_W)
    col = t - row * GRID_W
    inv_freq = ROPE_THETA ** (-jnp.arange(0, AXIS_ROT, 2, dtype=F32) / AXIS_ROT)
    ang_r = row[:, None] * inv_freq
    ang_c = col[:, None] * inv_freq
    cos = jnp.concatenate([jnp.cos(ang_r), jnp.cos(ang_r), jnp.cos(ang_c), jnp.cos(ang_c)], axis=1)
    sin = jnp.concatenate([-jnp.sin(ang_r), jnp.sin(ang_r), -jnp.sin(ang_c), jnp.sin(ang_c)], axis=1)
    return jnp.tile(cos, (1, 2)), jnp.tile(sin, (1, 2))


def kernel(x_prompt, x_sample, cache_k, cache_v, c, c_ctx, ada_w, ada_b, w_in, q_norm_g, k_norm_g, conv_w, conv_b, sgu_norm_g, sgu_w, sgu_b, w_out, ln1_g, ln1_b, ln2_g, ln2_b, ffn_w1, ffn_w3, ffn_w2, router_w, moe_w1, moe_w3, moe_w2):
    batch, seq, _ = x_prompt.shape
    dec_batch, dec_seq, _ = x_sample.shape
    past_len = cache_k.shape[2]
    n_ctx = batch * seq
    n_lat = dec_batch * dec_seq
    n_all = n_ctx + n_lat
    assert 1 + dec_batch <= SUBLANES and seq == ROW_BLOCK and dec_seq % ROW_BLOCK == 0
    assert n_ctx % MOE_CHUNK == 0 and dec_seq % MOE_CHUNK == 0 and n_ctx % dec_seq == 0

    cond = jnp.zeros((SUBLANES, D_MODEL), F32).at[0].set(c_ctx).at[1:1 + dec_batch].set(c)
    mod = _modulation(cond, ada_w, ada_b)
    mod = mod.reshape(DEPTH, SUBLANES, 6, D_MODEL)[:, :1 + dec_batch]
    mod = jnp.pad(mod, ((0, 0), (0, 0), (0, SUBLANES - 6), (0, 0)))

    lane_id = jnp.arange(ATTN_WIDTH, dtype=I32) // HEAD_DIM
    ones_bd = (lane_id[:, None] == lane_id[None, :]).astype(BF16)
    cos, sin = _rope_tables(dec_seq)

    xp = x_prompt.reshape(n_ctx, D_MODEL)
    xs = x_sample.reshape(n_lat, D_MODEL)
    x_all = None
    new_ks, new_vs = [], []
    for l in range(DEPTH):
        lw = (
            w_in[l].astype(BF16), w_out[l].astype(BF16),
            jnp.tile(q_norm_g[l], N_Q_HEADS)[None, :], jnp.tile(k_norm_g[l], N_KV_HEADS)[None, :],
            ones_bd,
            jnp.pad(conv_w[l], ((0, SUBLANES - 3), (0, 0))), conv_b[l][None, :],
            sgu_norm_g[l][None, :], sgu_w[l].astype(BF16),
            jnp.repeat(sgu_b[l].T, HEAD_DIM, axis=1),
            ln1_g[l][None, :], ln1_b[l][None, :],
        )
        kc = cache_k[:, l].reshape(dec_batch, past_len, KV_WIDTH)
        vc = cache_v[:, l].reshape(dec_batch, past_len, KV_WIDTH)
        if l == 0:
            x1, k_ctx, v_ctx = _mixer_ctx(xp, 0, batch, seq, mod[l], lw, n_all)
            x1 = _mixer_lat(xs, 0, dec_batch, dec_seq, mod[l], lw, cos, sin, kc, vc, x1)
        else:
            x1, k_ctx, v_ctx = _mixer_ctx(x_all, 0, batch, seq, mod[l], lw, n_all)
            x1 = _mixer_lat(x_all, n_ctx // dec_seq, dec_batch, dec_seq, mod[l], lw, cos, sin, kc, vc, x1)
        new_ks.append(k_ctx.reshape(batch, seq, N_KV_HEADS, HEAD_DIM))
        new_vs.append(v_ctx.reshape(batch, seq, N_KV_HEADS, HEAD_DIM))
        i = l // 2
        g2, b2 = ln2_g[l][None, :], ln2_b[l][None, :]
        if l % 2 == 0:
            x_all = _ffn_dense(x1, mod[l], ffn_w1[i].astype(BF16), ffn_w3[i].astype(BF16),
                               ffn_w2[i].astype(BF16), g2, b2, n_ctx, dec_seq)
        else:
            x_all = _ffn_moe(x1, mod[l], router_w[i].T, moe_w1[i].astype(BF16), moe_w3[i].astype(BF16),
                             moe_w2[i].astype(BF16), g2, b2, n_ctx, dec_seq)
    y_p = x_all[:n_ctx].reshape(batch, seq, D_MODEL)
    y_s = x_all[n_ctx:].reshape(dec_batch, dec_seq, D_MODEL)
    return (y_p, y_s, jnp.stack(new_ks, axis=1), jnp.stack(new_vs, axis=1))
```

```python
import functools

import jax
import jax.numpy as jnp
from jax import lax
from jax.experimental import pallas as pl
from jax.experimental.pallas import tpu as pltpu

F32 = jnp.float32
BF16 = jnp.bfloat16
I32 = jnp.int32

D_MODEL = 1024
DEPTH = 2
GRID_W = 64
HEAD_DIM = 64
N_Q_HEADS = 8
N_KV_HEADS = 2
ATTN_WIDTH = N_Q_HEADS * HEAD_DIM
KV_WIDTH = N_KV_HEADS * HEAD_DIM
ATTN_SCALE = HEAD_DIM ** -0.5
ROPE_THETA = 10000.0
AXIS_ROT = HEAD_DIM // 2
CONV_WIDTH = 256
SGU_WIDTH = 256
SGU_HEADS = 4
CHUNK = 128
IN_WIDTH = 2048
D_FF = 2816
N_EXPERTS = 8
D_FF_EXPERT = 1408
EPS = 1e-6
DEEPNORM_ALPHA = (2 * DEPTH) ** 0.25

LANES = 128
SUBLANES = 8
ROW_BLOCK = 256
FFN_TILE = 512
MOE_CHUNK = 1024
MOE_GRAN = 128
MOE_BLOCK = 256
MOE_STEPS = 2 * MOE_CHUNK // MOE_BLOCK + N_EXPERTS
MOE_ROWS = 2 * MOE_CHUNK + N_EXPERTS * MOE_GRAN
VMEM_LIMIT = 56 * 1024 * 1024

_Q0, _K0, _V0, _CI0, _CB0, _CC0, _SU0, _SV0 = 0, 512, 640, 768, 1024, 1280, 1536, 1792


def _dot(a, b):
    return jnp.dot(a, b, preferred_element_type=F32)


def _dot_nt(a, b):
    return lax.dot_general(a, b, (((1,), (1,)), ((), ())), preferred_element_type=F32)


def _split(x):
    hi = x.astype(BF16)
    lo = (x - hi.astype(F32)).astype(BF16)
    return hi, lo


def _group_sum(x, ones_bd):
    hi, lo = _split(x)
    return _dot(hi, ones_bd) + _dot(lo, ones_bd)


def _standardize(x):
    mu = jnp.mean(x, axis=-1, keepdims=True)
    d = x - mu
    return d * lax.rsqrt(jnp.mean(d * d, axis=-1, keepdims=True) + EPS)


def _silu(x):
    return x / (1.0 + jnp.exp(-x))


def _modulation_kernel(cond_ref, w_ref, b_ref, o_ref):
    s_hi, s_lo = _split(_silu(cond_ref[...]))
    w_hi, w_lo = _split(w_ref[...])
    o_ref[...] = _dot(s_hi, w_hi) + _dot(s_hi, w_lo) + _dot(s_lo, w_hi) + b_ref[...]


def _modulation(cond, ada_w, ada_b):
    n_out = ada_w.shape[-1]
    tn = 1536
    return pl.pallas_call(
        _modulation_kernel,
        grid=(DEPTH, n_out // tn),
        in_specs=[
            pl.BlockSpec((SUBLANES, D_MODEL), lambda l, j: (0, 0)),
            pl.BlockSpec((None, D_MODEL, tn), lambda l, j: (l, 0, j)),
            pl.BlockSpec((None, 1, tn), lambda l, j: (l, 0, j)),
        ],
        out_specs=pl.BlockSpec((None, SUBLANES, tn), lambda l, j: (l, 0, j)),
        out_shape=jax.ShapeDtypeStruct((DEPTH, SUBLANES, n_out), F32),
        compiler_params=pltpu.CompilerParams(
            dimension_semantics=("arbitrary", "arbitrary"), vmem_limit_bytes=VMEM_LIMIT),
        name="modulation",
    )(cond, ada_w, ada_b.reshape(DEPTH, 1, n_out))


def _rope(x, cos, sin_signed):
    w = x.shape[1]
    lane = lax.broadcasted_iota(I32, x.shape, 1)
    first_half = (lane & 31) < 16
    partner = jnp.where(first_half, pltpu.roll(x, w - 16, 1), pltpu.roll(x, 16, 1))
    return x * cos + partner * sin_signed


def _head_variants(x):
    lane = lax.broadcasted_iota(I32, x.shape, 1)
    lo = lane < HEAD_DIM
    xr = pltpu.roll(x, HEAD_DIM, 1)
    zero = jnp.zeros_like(x)
    return (jnp.where(lo, x, zero).astype(BF16), jnp.where(lo, zero, xr).astype(BF16),
            jnp.where(lo, xr, zero).astype(BF16), jnp.where(lo, zero, x).astype(BF16))


def _mixer_kernel(*refs, seq, n_cache, rope):
    if rope:
        (x_ref, mod_ref, win_ref, wout_ref, qg_ref, kg_ref, ones_ref, convw_ref, convb_ref, sgug_ref,
         sguw_ref, sgub_ref, ln1g_ref, ln1b_ref, cos_ref, sin_ref, kc_ref, vc_ref, x1_ref,
         q_scr, kvar_scr, vvar_scr, u_scr, cb_scr, su_scr, vn_scr, mix_scr) = refs
    else:
        (x_ref, mod_ref, win_ref, wout_ref, qg_ref, kg_ref, ones_ref, convw_ref, convb_ref, sgug_ref,
         sguw_ref, sgub_ref, ln1g_ref, ln1b_ref, x1_ref, k_ref, v_ref,
         q_scr, kvar_scr, vvar_scr, u_scr, cb_scr, su_scr, vn_scr, mix_scr) = refs
    n_blocks = seq // ROW_BLOCK

    def rows_of(r):
        if isinstance(r, int):
            return pl.ds(r * ROW_BLOCK, ROW_BLOCK)
        return pl.ds(pl.multiple_of(r * ROW_BLOCK, ROW_BLOCK), ROW_BLOCK)

    def loop(body):
        if n_blocks == 1:
            body(0)
        else:
            def step(r, carry):
                body(r)
                return carry
            lax.fori_loop(0, n_blocks, step, 0)

    if n_cache:
        for i, var in enumerate(_head_variants(kc_ref[...])):
            kvar_scr[i, pl.ds(seq, n_cache), :] = var
        for i, var in enumerate(_head_variants(vc_ref[...])):
            vvar_scr[i, pl.ds(seq, n_cache), :] = var

    def project(r):
        rows = rows_of(r)
        x = x_ref[rows, :]
        h = _standardize(x) * (1.0 + mod_ref[1:2, :]) + mod_ref[0:1, :]
        z = _dot(h.astype(BF16), win_ref[...])
        ones_bd = ones_ref[...]
        zq = z[:, _Q0:_K0]
        q = zq * lax.rsqrt(_group_sum(zq * zq, ones_bd) * (1.0 / HEAD_DIM) + EPS) * qg_ref[...]
        zk = z[:, _K0:_V0]
        k = zk * lax.rsqrt(_group_sum(zk * zk, ones_bd[:KV_WIDTH, :KV_WIDTH]) * (1.0 / HEAD_DIM) + EPS) * kg_ref[...]
        v = z[:, _V0:_CI0]
        if rope:
            cos = cos_ref[rows, :]
            sin = sin_ref[rows, :]
            q = _rope(q, jnp.concatenate([cos] * 4, axis=1), jnp.concatenate([sin] * 4, axis=1))
            k = _rope(k, cos, sin)
        else:
            k_ref[rows, :] = k
            v_ref[rows, :] = v
        q_scr[rows, :] = (q * ATTN_SCALE).astype(BF16)
        for i, var in enumerate(_head_variants(k)):
            kvar_scr[i, rows, :] = var
        for i, var in enumerate(_head_variants(v)):
            vvar_scr[i, rows, :] = var
        u_scr[rows, :] = z[:, _CC0:_SU0] * z[:, _CI0:_CB0]
        cb_scr[rows, :] = z[:, _CB0:_CC0]
        su_scr[rows, :] = z[:, _SU0:_SV0]
        sv = z[:, _SV0:IN_WIDTH]
        ones_sgu = ones_bd[:SGU_WIDTH, :SGU_WIDTH]
        d = sv - _group_sum(sv, ones_sgu) * (1.0 / HEAD_DIM)
        vn = d * lax.rsqrt(_group_sum(d * d, ones_sgu) * (1.0 / HEAD_DIM) + EPS) * sgug_ref[...]
        vn_scr[rows, :] = vn.astype(BF16)

    loop(project)

    u = u_scr[...]
    pos = lax.broadcasted_iota(I32, u.shape, 0)
    up = jnp.where(pos == 0, 0.0, pltpu.roll(u, 1, 0))
    dn = jnp.where(pos == seq - 1, 0.0, pltpu.roll(u, seq - 1, 0))
    conv = up * convw_ref[0:1, :] + u * convw_ref[1:2, :] + dn * convw_ref[2:3, :] + convb_ref[...]
    mix_scr[:, ATTN_WIDTH:ATTN_WIDTH + CONV_WIDTH] = (cb_scr[...] * conv).astype(BF16)

    for n in range(seq // CHUNK):
        rows = pl.ds(n * CHUNK, CHUNK)
        vn = vn_scr[rows, :]
        lane = lax.broadcasted_iota(I32, vn.shape, 1)
        s = sgub_ref[...]
        for hd in range(SGU_HEADS):
            in_head = (lane >= hd * HEAD_DIM) & (lane < (hd + 1) * HEAD_DIM)
            masked = jnp.where(in_head, vn, jnp.zeros_like(vn))
            s = s + _dot(sguw_ref[hd], masked)
        mix_scr[rows, ATTN_WIDTH + CONV_WIDTH:] = (su_scr[rows, :] * s).astype(BF16)

    def attend(r):
        rows = rows_of(r)
        for pair in range(N_Q_HEADS // 2):
            qp = q_scr[rows, pair * LANES:(pair + 1) * LANES]
            kv = pair // (N_Q_HEADS // N_KV_HEADS // 2)
            acc = jnp.zeros((ROW_BLOCK, LANES), F32)
            for parity in range(2):
                s = _dot_nt(qp, kvar_scr[2 * kv + parity])
                p = jnp.exp(s - jnp.max(s, axis=1, keepdims=True))
                denom = jnp.sum(p, axis=1, keepdims=True)
                acc = acc + _dot(p.astype(BF16), vvar_scr[2 * kv + parity]) / denom
            mix_scr[rows, pair * LANES:(pair + 1) * LANES] = acc.astype(BF16)

    loop(attend)

    def finish(r):
        rows = rows_of(r)
        mix = _dot(mix_scr[rows, :], wout_ref[...])
        y = DEEPNORM_ALPHA * x_ref[rows, :] + mod_ref[2:3, :] * mix
        x1_ref[rows, :] = _standardize(y) * ln1g_ref[...] + ln1b_ref[...]

    loop(finish)


def _full(shape):
    n = len(shape)
    return pl.BlockSpec(shape, lambda *_: (0,) * n)


def _resident(shape):
    n = len(shape)
    return pl.BlockSpec(shape, lambda *_: (0,) * n, pipeline_mode=pl.Buffered(1))


def _mixer_scratch(seq, n_cache):
    nk = seq + n_cache
    return [
        pltpu.VMEM((seq, ATTN_WIDTH), BF16),
        pltpu.VMEM((4, nk, LANES), BF16),
        pltpu.VMEM((4, nk, LANES), BF16),
        pltpu.VMEM((seq, CONV_WIDTH), F32),
        pltpu.VMEM((seq, CONV_WIDTH), F32),
        pltpu.VMEM((seq, SGU_WIDTH), F32),
        pltpu.VMEM((seq, SGU_WIDTH), BF16),
        pltpu.VMEM((seq, D_MODEL), BF16),
    ]


def _weight_specs():
    return [
        _resident((D_MODEL, IN_WIDTH)), _resident((D_MODEL, D_MODEL)), _full((1, ATTN_WIDTH)), _full((1, KV_WIDTH)),
        _full((ATTN_WIDTH, ATTN_WIDTH)), _full((SUBLANES, CONV_WIDTH)), _full((1, CONV_WIDTH)),
        _full((1, SGU_WIDTH)), _full((SGU_HEADS, CHUNK, CHUNK)), _full((CHUNK, SGU_WIDTH)),
        _full((1, D_MODEL)), _full((1, D_MODEL)),
    ]


def _mixer_ctx(x, n_seq, seq, mods, lw):
    kernel = functools.partial(_mixer_kernel, seq=seq, n_cache=0, rope=False)
    return pl.pallas_call(
        kernel,
        grid=(n_seq,),
        in_specs=[pl.BlockSpec((seq, D_MODEL), lambda i: (i, 0)),
                  pl.BlockSpec((None, SUBLANES, D_MODEL), lambda i: (0, 0, 0))] + _weight_specs(),
        out_specs=[pl.BlockSpec((seq, D_MODEL), lambda i: (i, 0)),
                   pl.BlockSpec((seq, KV_WIDTH), lambda i: (i, 0)),
                   pl.BlockSpec((seq, KV_WIDTH), lambda i: (i, 0))],
        out_shape=[jax.ShapeDtypeStruct((n_seq * seq, D_MODEL), F32),
                   jax.ShapeDtypeStruct((n_seq * seq, KV_WIDTH), F32),
                   jax.ShapeDtypeStruct((n_seq * seq, KV_WIDTH), F32)],
        scratch_shapes=_mixer_scratch(seq, 0),
        compiler_params=pltpu.CompilerParams(dimension_semantics=("arbitrary",), vmem_limit_bytes=VMEM_LIMIT),
        name="mixer_ctx",
    )(x, mods, *lw)


def _mixer_lat(x, n_seq, seq, mods, lw, cos, sin, kc, vc):
    n_cache = kc.shape[1]
    kernel = functools.partial(_mixer_kernel, seq=seq, n_cache=n_cache, rope=True)
    return pl.pallas_call(
        kernel,
        grid=(n_seq,),
        in_specs=[pl.BlockSpec((seq, D_MODEL), lambda b: (b, 0)),
                  pl.BlockSpec((None, SUBLANES, D_MODEL), lambda b: (1 + b, 0, 0))] + _weight_specs() + [
                  _full((seq, LANES)), _full((seq, LANES)),
                  pl.BlockSpec((None, n_cache, KV_WIDTH), lambda b: (b, 0, 0)),
                  pl.BlockSpec((None, n_cache, KV_WIDTH), lambda b: (b, 0, 0))],
        out_specs=pl.BlockSpec((seq, D_MODEL), lambda b: (b, 0)),
        out_shape=jax.ShapeDtypeStruct((n_seq * seq, D_MODEL), F32),
        scratch_shapes=_mixer_scratch(seq, n_cache),
        compiler_params=pltpu.CompilerParams(dimension_semantics=("arbitrary",), vmem_limit_bytes=VMEM_LIMIT),
        name="mixer_lat",
    )(x, mods, *lw, cos, sin, kc, vc)


def _ffn_kernel(x_ref, mod_ref, w1_ref, w3_ref, w2_ref, g_ref, b_ref, o_ref):
    x = x_ref[...]
    h = (_standardize(x) * (1.0 + mod_ref[4:5, :]) + mod_ref[3:4, :]).astype(BF16)
    half = D_FF // 2
    acc = jnp.zeros(x.shape, F32)
    for c in range(2):
        cols = slice(c * half, (c + 1) * half)
        a = _dot(h, w1_ref[:, cols])
        b = _dot(h, w3_ref[:, cols])
        acc = acc + _dot((_silu(a) * b).astype(BF16), w2_ref[cols, :])
    y = DEEPNORM_ALPHA * x + mod_ref[5:6, :] * acc
    o_ref[...] = _standardize(y) * g_ref[...] + b_ref[...]


def _group_of_tile(i, tile, stream):
    group0, seq = stream
    return group0 if seq is None else group0 + i // (seq // tile)


def _ffn_dense(x_all, mods, w1, w3, w2, g, b, stream):
    n_tok = x_all.shape[0]
    grp = functools.partial(_group_of_tile, tile=FFN_TILE, stream=stream)
    return pl.pallas_call(
        _ffn_kernel,
        grid=(n_tok // FFN_TILE,),
        in_specs=[pl.BlockSpec((FFN_TILE, D_MODEL), lambda i: (i, 0)),
                  pl.BlockSpec((None, SUBLANES, D_MODEL), lambda i: (grp(i), 0, 0)),
                  _resident((D_MODEL, D_FF)), _resident((D_MODEL, D_FF)), _resident((D_FF, D_MODEL)),
                  _full((1, D_MODEL)), _full((1, D_MODEL))],
        out_specs=pl.BlockSpec((FFN_TILE, D_MODEL), lambda i: (i, 0)),
        out_shape=jax.ShapeDtypeStruct(x_all.shape, F32),
        compiler_params=pltpu.CompilerParams(dimension_semantics=("arbitrary",), vmem_limit_bytes=VMEM_LIMIT),
        name="ffn_dense",
    )(x_all, mods, w1, w3, w2, g, b)


def _router_kernel(x_ref, mod_ref, rw_ref, before_ref, dest_ref, gate_ref, cnt_ref):
    h = _standardize(x_ref[...]) * (1.0 + mod_ref[4:5, :]) + mod_ref[3:4, :]
    h_hi, h_lo = _split(h)
    w_hi, w_lo = _split(rw_ref[...])
    logits = _dot_nt(w_hi, h_hi) + _dot_nt(w_hi, h_lo) + _dot_nt(w_lo, h_hi)
    eid = lax.broadcasted_iota(I32, logits.shape, 0).astype(F32)
    m1 = jnp.max(logits, axis=0, keepdims=True)
    i1 = jnp.min(jnp.where(logits == m1, eid, float(N_EXPERTS)), axis=0, keepdims=True)
    oh1 = eid == i1
    rest = jnp.where(oh1, -jnp.inf, logits)
    m2 = jnp.max(rest, axis=0, keepdims=True)
    i2 = jnp.min(jnp.where(rest == m2, eid, float(N_EXPERTS)), axis=0, keepdims=True)
    oh2 = eid == i2
    e = jnp.exp(m2 - m1)
    gate_ref[0:1, :] = 1.0 / (1.0 + e)
    gate_ref[1:2, :] = e / (1.0 + e)
    sel = jnp.where(oh1 | oh2, 1.0, 0.0)
    rank = _dot(sel.astype(BF16), before_ref[...])
    eid_out = lax.broadcasted_iota(I32, cnt_ref.shape, 0).astype(F32)
    start = jnp.zeros(sel.shape, F32)
    cnt_out = jnp.zeros(cnt_ref.shape, F32)
    for ex in range(N_EXPERTS):
        cnt = jnp.sum(sel[ex:ex + 1, :], axis=1, keepdims=True)
        padded = jnp.ceil(cnt * (1.0 / MOE_GRAN)) * MOE_GRAN
        start = start + jnp.where(eid > ex, padded, 0.0)
        cnt_out = cnt_out + jnp.where(eid_out == ex, cnt, 0.0)
    row = (start + rank) * SUBLANES
    dest_ref[0:1, :] = jnp.sum(jnp.where(oh1, row, 0.0), axis=0, keepdims=True).astype(I32)
    dest_ref[1:2, :] = jnp.sum(jnp.where(oh2, row, 0.0), axis=0, keepdims=True).astype(I32)
    cnt_ref[...] = cnt_out


def _router(x_all, mods, rw_t, stream):
    n_chunks = x_all.shape[0] // MOE_CHUNK
    grp = functools.partial(_group_of_tile, tile=MOE_CHUNK, stream=stream)
    tok = jnp.arange(MOE_CHUNK, dtype=I32)
    before = (tok[:, None] < tok[None, :]).astype(BF16)
    return pl.pallas_call(
        _router_kernel,
        grid=(n_chunks,),
        in_specs=[pl.BlockSpec((MOE_CHUNK, D_MODEL), lambda c: (c, 0)),
                  pl.BlockSpec((None, SUBLANES, D_MODEL), lambda c: (grp(c), 0, 0)),
                  _full((N_EXPERTS, D_MODEL)), _full((MOE_CHUNK, MOE_CHUNK))],
        out_specs=[pl.BlockSpec((None, 2, MOE_CHUNK), lambda c: (c, 0, 0)),
                   pl.BlockSpec((None, 2, MOE_CHUNK), lambda c: (c, 0, 0)),
                   pl.BlockSpec((None, N_EXPERTS, LANES), lambda c: (c, 0, 0))],
        out_shape=[jax.ShapeDtypeStruct((n_chunks, 2, MOE_CHUNK), I32),
                   jax.ShapeDtypeStruct((n_chunks, 2, MOE_CHUNK), F32),
                   jax.ShapeDtypeStruct((n_chunks, N_EXPERTS, LANES), F32)],
        compiler_params=pltpu.CompilerParams(dimension_semantics=("arbitrary",), vmem_limit_bytes=VMEM_LIMIT),
        name="moe_router",
    )(x_all, mods, rw_t, before)


def _step_tables(cnt):
    padded = (cnt + MOE_GRAN - 1) // MOE_GRAN * MOE_GRAN
    end = jnp.cumsum(padded, axis=1)
    start = end - padded
    steps = (padded + MOE_BLOCK - 1) // MOE_BLOCK
    step_end = jnp.cumsum(steps, axis=1)
    j = jnp.arange(MOE_STEPS, dtype=I32)[None, :, None]
    e_of = jnp.sum((j >= step_end[:, None, :]).astype(I32), axis=2)
    used = e_of < N_EXPERTS
    e_idx = jnp.minimum(e_of, N_EXPERTS - 1)
    k = j[:, :, 0] - jnp.take_along_axis(step_end - steps, e_idx, axis=1)
    n_steps = jnp.take_along_axis(steps, e_idx, axis=1)
    tail = jnp.take_along_axis(padded, e_idx, axis=1) - (n_steps - 1) * MOE_BLOCK
    row0 = jnp.take_along_axis(start, e_idx, axis=1) + jnp.where(k == 0, 0, tail + (k - 1) * MOE_BLOCK)
    n_rows = jnp.where(used, jnp.where(k == 0, tail, MOE_BLOCK), 0)
    last_used = jnp.max(jnp.where(used, e_idx, 0), axis=1, keepdims=True)
    e_idx = jnp.where(used, e_idx, last_used)
    row0 = jnp.where(used, row0, 0)
    return (e_idx.reshape(-1).astype(I32), row0.reshape(-1).astype(I32), n_rows.reshape(-1).astype(I32))


def _token_rows(first_row):
    return pl.ds(pl.multiple_of(first_row, SUBLANES), SUBLANES)


def _store_token_major(ref, tok0, val):
    for cc in range(D_MODEL // LANES):
        ref[pl.ds(tok0 * SUBLANES + cc, val.shape[0], stride=SUBLANES), :] = val[:, cc * LANES:(cc + 1) * LANES]


def _load_token_major(ref, tok0, n):
    return jnp.concatenate(
        [ref[pl.ds(tok0 * SUBLANES + cc, n, stride=SUBLANES), :] for cc in range(D_MODEL // LANES)], axis=1)


def _moe_kernel(step_e, step_row0, step_n, x_ref, mod_ref, dest_ref, gate_ref, w1_ref, w3_ref, w2_ref,
                g_ref, b_ref, o_ref, tok_scr, rows_scr):
    c = pl.program_id(0)
    j = pl.program_id(1)

    @pl.when((c == 0) & (j == 0))
    def _init():
        rows_scr[...] = jnp.zeros(rows_scr.shape, F32)

    @pl.when(j == 0)
    def _dispatch():
        for r in range(MOE_CHUNK // ROW_BLOCK):
            h = (_standardize(x_ref[pl.ds(r * ROW_BLOCK, ROW_BLOCK), :]) * (1.0 + mod_ref[4:5, :])
                 + mod_ref[3:4, :])
            _store_token_major(tok_scr, r * ROW_BLOCK, h)

        def body(t, carry):
            row = tok_scr[_token_rows(t * SUBLANES), :]
            rows_scr[_token_rows(dest_ref[t]), :] = row
            rows_scr[_token_rows(dest_ref[MOE_CHUNK + t]), :] = row
            return carry
        lax.fori_loop(0, MOE_CHUNK, body, 0, unroll=8)

    step = c * MOE_STEPS + j
    n_rows = step_n[step]
    row0 = step_row0[step]

    def experts(m):
        xin = _load_token_major(rows_scr, row0, m).astype(BF16)
        a = _dot(xin, w1_ref[...])
        b = _dot(xin, w3_ref[...])
        y = _dot((_silu(a) * b).astype(BF16), w2_ref[...])
        _store_token_major(rows_scr, row0, y)

    for m in range(MOE_GRAN, MOE_BLOCK + 1, MOE_GRAN):
        pl.when(n_rows == m)(functools.partial(experts, m))

    @pl.when(j == MOE_STEPS - 1)
    def _combine():
        def body(t, carry):
            y0 = rows_scr[_token_rows(dest_ref[t]), :]
            y1 = rows_scr[_token_rows(dest_ref[MOE_CHUNK + t]), :]
            tok_scr[_token_rows(t * SUBLANES), :] = gate_ref[t] * y0 + gate_ref[MOE_CHUNK + t] * y1
            return carry
        lax.fori_loop(0, MOE_CHUNK, body, 0, unroll=8)
        for r in range(MOE_CHUNK // ROW_BLOCK):
            rows = pl.ds(r * ROW_BLOCK, ROW_BLOCK)
            ffn = _load_token_major(tok_scr, r * ROW_BLOCK, ROW_BLOCK)
            y = DEEPNORM_ALPHA * x_ref[rows, :] + mod_ref[5:6, :] * ffn
            o_ref[rows, :] = _standardize(y) * g_ref[...] + b_ref[...]


def _ffn_moe(x_all, mods, rw_t, w1, w3, w2, g, b, stream):
    n_chunks = x_all.shape[0] // MOE_CHUNK
    dest, gates, cnt = _router(x_all, mods, rw_t, stream)
    step_e, step_row0, step_n = _step_tables(cnt[:, :, 0].astype(I32))
    grp = functools.partial(_group_of_tile, tile=MOE_CHUNK, stream=stream)

    def expert_of(c, j, se, sr, sn):
        return (se[c * MOE_STEPS + j], 0, 0)

    grid_spec = pltpu.PrefetchScalarGridSpec(
        num_scalar_prefetch=3,
        grid=(n_chunks, MOE_STEPS),
        in_specs=[pl.BlockSpec((MOE_CHUNK, D_MODEL), lambda c, j, *_: (c, 0)),
                  pl.BlockSpec((None, SUBLANES, D_MODEL), lambda c, j, *_: (grp(c), 0, 0)),
                  pl.BlockSpec((2 * MOE_CHUNK,), lambda c, j, *_: (c,), memory_space=pltpu.SMEM),
                  pl.BlockSpec((2 * MOE_CHUNK,), lambda c, j, *_: (c,), memory_space=pltpu.SMEM),
                  pl.BlockSpec((None, D_MODEL, D_FF_EXPERT), expert_of),
                  pl.BlockSpec((None, D_MODEL, D_FF_EXPERT), expert_of),
                  pl.BlockSpec((None, D_FF_EXPERT, D_MODEL), expert_of),
                  pl.BlockSpec((1, D_MODEL), lambda c, j, *_: (0, 0)),
                  pl.BlockSpec((1, D_MODEL), lambda c, j, *_: (0, 0))],
        out_specs=pl.BlockSpec((MOE_CHUNK, D_MODEL), lambda c, j, *_: (c, 0)),
        scratch_shapes=[pltpu.VMEM((MOE_CHUNK * SUBLANES, LANES), F32),
                        pltpu.VMEM((MOE_ROWS * SUBLANES, LANES), F32)],
    )
    return pl.pallas_call(
        _moe_kernel,
        grid_spec=grid_spec,
        out_shape=jax.ShapeDtypeStruct(x_all.shape, F32),
        compiler_params=pltpu.CompilerParams(
            dimension_semantics=("arbitrary", "arbitrary"), vmem_limit_bytes=VMEM_LIMIT),
        name="moe_experts",
    )(step_e, step_row0, step_n, x_all, mods, dest.reshape(-1), gates.reshape(-1), w1, w3, w2, g, b)


def _rope_tables(n_tokens):
    t = jnp.arange(n_tokens, dtype=F32)
    row = jnp.floor(t / GRID_W)
    col = t - row * GRID_W
    inv_freq = ROPE_THETA ** (-jnp.arange(0, AXIS_ROT, 2, dtype=F32) / AXIS_ROT)
    ang_r = row[:, None] * inv_freq
    ang_c = col[:, None] * inv_freq
    cos = jnp.concatenate([jnp.cos(ang_r), jnp.cos(ang_r), jnp.cos(ang_c), jnp.cos(ang_c)], axis=1)
    sin = jnp.concatenate([-jnp.sin(ang_r), jnp.sin(ang_r), -jnp.sin(ang_c), jnp.sin(ang_c)], axis=1)
    return jnp.tile(cos, (1, 2)), jnp.tile(sin, (1, 2))


def kernel(x_prompt, x_sample, cache_k, cache_v, c, c_ctx, ada_w, ada_b, w_in, q_norm_g, k_norm_g, conv_w, conv_b, sgu_norm_g, sgu_w, sgu_b, w_out, ln1_g, ln1_b, ln2_g, ln2_b, ffn_w1, ffn_w3, ffn_w2, router_w, moe_w1, moe_w3, moe_w2):
    batch, seq, _ = x_prompt.shape
    dec_batch, dec_seq, _ = x_sample.shape
    past_len = cache_k.shape[2]
    n_ctx = batch * seq
    n_lat = dec_batch * dec_seq
    assert 1 + dec_batch <= SUBLANES and seq == ROW_BLOCK and dec_seq % ROW_BLOCK == 0
    assert n_ctx % MOE_CHUNK == 0 and dec_seq % MOE_CHUNK == 0 and dec_seq % FFN_TILE == 0

    cond = jnp.zeros((SUBLANES, D_MODEL), F32).at[0].set(c_ctx).at[1:1 + dec_batch].set(c)
    mod = _modulation(cond, ada_w, ada_b)
    mod = mod.reshape(DEPTH, SUBLANES, 6, D_MODEL)[:, :1 + dec_batch]
    mod = jnp.pad(mod, ((0, 0), (0, 0), (0, SUBLANES - 6), (0, 0)))

    lane_id = jnp.arange(ATTN_WIDTH, dtype=I32) // HEAD_DIM
    ones_bd = (lane_id[:, None] == lane_id[None, :]).astype(BF16)
    cos, sin = _rope_tables(dec_seq)

    streams = ((0, None), (1, dec_seq))
    xs = [x_prompt.reshape(n_ctx, D_MODEL), x_sample.reshape(n_lat, D_MODEL)]
    new_ks, new_vs = [], []
    for l in range(DEPTH):
        lw = (
            w_in[l].astype(BF16), w_out[l].astype(BF16),
            jnp.tile(q_norm_g[l], N_Q_HEADS)[None, :], jnp.tile(k_norm_g[l], N_KV_HEADS)[None, :],
            ones_bd,
            jnp.pad(conv_w[l], ((0, SUBLANES - 3), (0, 0))), conv_b[l][None, :],
            sgu_norm_g[l][None, :], sgu_w[l].astype(BF16),
            jnp.repeat(sgu_b[l].T, HEAD_DIM, axis=1),
            ln1_g[l][None, :], ln1_b[l][None, :],
        )
        kc = cache_k[:, l].reshape(dec_batch, past_len, KV_WIDTH)
        vc = cache_v[:, l].reshape(dec_batch, past_len, KV_WIDTH)
        x_ctx, k_ctx, v_ctx = _mixer_ctx(xs[0], batch, seq, mod[l], lw)
        x_lat = _mixer_lat(xs[1], dec_batch, dec_seq, mod[l], lw, cos, sin, kc, vc)
        new_ks.append(k_ctx.reshape(batch, seq, N_KV_HEADS, HEAD_DIM))
        new_vs.append(v_ctx.reshape(batch, seq, N_KV_HEADS, HEAD_DIM))
        i = l // 2
        g2, b2 = ln2_g[l][None, :], ln2_b[l][None, :]
        if l % 2 == 0:
            ws = (ffn_w1[i].astype(BF16), ffn_w3[i].astype(BF16), ffn_w2[i].astype(BF16))
            xs = [_ffn_dense(x, mod[l], *ws, g2, b2, st) for x, st in zip((x_ctx, x_lat), streams)]
        else:
            ws = (moe_w1[i].astype(BF16), moe_w3[i].astype(BF16), moe_w2[i].astype(BF16))
            xs = [_ffn_moe(x, mod[l], router_w[i].T, *ws, g2, b2, st) for x, st in zip((x_ctx, x_lat), streams)]
    y_p = xs[0].reshape(batch, seq, D_MODEL)
    y_s = xs[1].reshape(dec_batch, dec_seq, D_MODEL)
    return (y_p, y_s, jnp.stack(new_ks, axis=1), jnp.stack(new_vs, axis=1))
```

```python
import functools

import jax
import jax.numpy as jnp
from jax import lax
from jax.experimental import pallas as pl
from jax.experimental.pallas import tpu as pltpu

F32 = jnp.float32
BF16 = jnp.bfloat16
I32 = jnp.int32

D_MODEL = 1024
DEPTH = 2
GRID_W = 64
HEAD_DIM = 64
N_Q_HEADS = 8
N_KV_HEADS = 2
ATTN_WIDTH = N_Q_HEADS * HEAD_DIM
KV_WIDTH = N_KV_HEADS * HEAD_DIM
ATTN_SCALE = HEAD_DIM ** -0.5
ROPE_THETA = 10000.0
AXIS_ROT = HEAD_DIM // 2
CONV_WIDTH = 256
SGU_WIDTH = 256
SGU_HEADS = 4
CHUNK = 128
IN_WIDTH = 2048
D_FF = 2816
N_EXPERTS = 8
D_FF_EXPERT = 1408
EPS = 1e-6
DEEPNORM_ALPHA = (2 * DEPTH) ** 0.25

LANES = 128
SUBLANES = 8
ROW_BLOCK = 256
FFN_TILE = 512
MOE_CHUNK = 2048
ROUTER_BLOCK = 1024
MOE_GRAN = 128
MOE_BLOCK = 256
MOE_TOK_BLOCK = 512
MOE_TOK_STEPS = MOE_CHUNK // MOE_TOK_BLOCK
MOE_STEPS = 2 * MOE_TOK_STEPS + N_EXPERTS
MOE_ROWS = 2 * MOE_CHUNK + N_EXPERTS * MOE_GRAN
VMEM_LIMIT = 56 * 1024 * 1024

_Q0, _K0, _V0, _CI0, _CB0, _CC0, _SU0, _SV0 = 0, 512, 640, 768, 1024, 1280, 1536, 1792


def _dot(a, b):
    return jnp.dot(a, b, preferred_element_type=F32)


def _dot_nt(a, b):
    return lax.dot_general(a, b, (((1,), (1,)), ((), ())), preferred_element_type=F32)


def _split(x):
    hi = x.astype(BF16)
    lo = (x - hi.astype(F32)).astype(BF16)
    return hi, lo


def _group_sum(x, ones_bd):
    hi, lo = _split(x)
    return _dot(hi, ones_bd) + _dot(lo, ones_bd)


def _standardize(x):
    mu = jnp.mean(x, axis=-1, keepdims=True)
    d = x - mu
    return d * lax.rsqrt(jnp.mean(d * d, axis=-1, keepdims=True) + EPS)


def _silu(x):
    return x / (1.0 + jnp.exp(-x))


def _modulation_kernel(cond_ref, w_ref, b_ref, o_ref):
    s_hi, s_lo = _split(_silu(cond_ref[...]))
    w_hi, w_lo = _split(w_ref[...])
    o_ref[...] = _dot(s_hi, w_hi) + _dot(s_hi, w_lo) + _dot(s_lo, w_hi) + b_ref[...]


def _modulation(cond, ada_w, ada_b):
    n_out = ada_w.shape[-1]
    tn = 1536
    return pl.pallas_call(
        _modulation_kernel,
        grid=(DEPTH, n_out // tn),
        in_specs=[
            pl.BlockSpec((SUBLANES, D_MODEL), lambda l, j: (0, 0)),
            pl.BlockSpec((None, D_MODEL, tn), lambda l, j: (l, 0, j)),
            pl.BlockSpec((None, 1, tn), lambda l, j: (l, 0, j)),
        ],
        out_specs=pl.BlockSpec((None, SUBLANES, tn), lambda l, j: (l, 0, j)),
        out_shape=jax.ShapeDtypeStruct((DEPTH, SUBLANES, n_out), F32),
        compiler_params=pltpu.CompilerParams(
            dimension_semantics=("arbitrary", "arbitrary"), vmem_limit_bytes=VMEM_LIMIT),
        name="modulation",
    )(cond, ada_w, ada_b.reshape(DEPTH, 1, n_out))


def _rope(x, cos, sin_signed):
    w = x.shape[1]
    lane = lax.broadcasted_iota(I32, x.shape, 1)
    first_half = (lane & 31) < 16
    partner = jnp.where(first_half, pltpu.roll(x, w - 16, 1), pltpu.roll(x, 16, 1))
    return x * cos + partner * sin_signed


def _head_variants(x):
    lane = lax.broadcasted_iota(I32, x.shape, 1)
    lo = lane < HEAD_DIM
    xr = pltpu.roll(x, HEAD_DIM, 1)
    zero = jnp.zeros_like(x)
    return (jnp.where(lo, x, zero).astype(BF16), jnp.where(lo, zero, xr).astype(BF16),
            jnp.where(lo, xr, zero).astype(BF16), jnp.where(lo, zero, x).astype(BF16))


def _mixer_kernel(*refs, seq, n_cache, rope):
    if rope:
        (x_ref, mod_ref, win_ref, wout_ref, qg_ref, kg_ref, ones_ref, convw_ref, convb_ref, sgug_ref,
         sguw_ref, sgub_ref, ln1g_ref, ln1b_ref, cos_ref, sin_ref, kc_ref, vc_ref, x1_ref,
         q_scr, kvar_scr, vvar_scr, u_scr, cb_scr, su_scr, vn_scr, mix_scr) = refs
    else:
        (x_ref, mod_ref, win_ref, wout_ref, qg_ref, kg_ref, ones_ref, convw_ref, convb_ref, sgug_ref,
         sguw_ref, sgub_ref, ln1g_ref, ln1b_ref, x1_ref, k_ref, v_ref,
         q_scr, kvar_scr, vvar_scr, u_scr, cb_scr, su_scr, vn_scr, mix_scr) = refs
    n_blocks = seq // ROW_BLOCK

    def rows_of(r):
        if isinstance(r, int):
            return pl.ds(r * ROW_BLOCK, ROW_BLOCK)
        return pl.ds(pl.multiple_of(r * ROW_BLOCK, ROW_BLOCK), ROW_BLOCK)

    def loop(body):
        if n_blocks == 1:
            body(0)
        else:
            def step(r, carry):
                body(r)
                return carry
            lax.fori_loop(0, n_blocks, step, 0)

    if n_cache:
        for i, var in enumerate(_head_variants(kc_ref[...])):
            kvar_scr[i, pl.ds(seq, n_cache), :] = var
        for i, var in enumerate(_head_variants(vc_ref[...])):
            vvar_scr[i, pl.ds(seq, n_cache), :] = var

    def project(r):
        rows = rows_of(r)
        x = x_ref[rows, :]
        h = _standardize(x) * (1.0 + mod_ref[1:2, :]) + mod_ref[0:1, :]
        z = _dot(h.astype(BF16), win_ref[...])
        ones_bd = ones_ref[...]
        zq = z[:, _Q0:_K0]
        q = zq * lax.rsqrt(_group_sum(zq * zq, ones_bd) * (1.0 / HEAD_DIM) + EPS) * qg_ref[...]
        zk = z[:, _K0:_V0]
        k = zk * lax.rsqrt(_group_sum(zk * zk, ones_bd[:KV_WIDTH, :KV_WIDTH]) * (1.0 / HEAD_DIM) + EPS) * kg_ref[...]
        v = z[:, _V0:_CI0]
        if rope:
            cos = cos_ref[rows, :]
            sin = sin_ref[rows, :]
            q = _rope(q, jnp.concatenate([cos] * 4, axis=1), jnp.concatenate([sin] * 4, axis=1))
            k = _rope(k, cos, sin)
        else:
            k_ref[rows, :] = k
            v_ref[rows, :] = v
        q_scr[rows, :] = (q * ATTN_SCALE).astype(BF16)
        for i, var in enumerate(_head_variants(k)):
            kvar_scr[i, rows, :] = var
        for i, var in enumerate(_head_variants(v)):
            vvar_scr[i, rows, :] = var
        u_scr[rows, :] = z[:, _CC0:_SU0] * z[:, _CI0:_CB0]
        cb_scr[rows, :] = z[:, _CB0:_CC0]
        su_scr[rows, :] = z[:, _SU0:_SV0]
        sv = z[:, _SV0:IN_WIDTH]
        ones_sgu = ones_bd[:SGU_WIDTH, :SGU_WIDTH]
        d = sv - _group_sum(sv, ones_sgu) * (1.0 / HEAD_DIM)
        vn = d * lax.rsqrt(_group_sum(d * d, ones_sgu) * (1.0 / HEAD_DIM) + EPS) * sgug_ref[...]
        vn_scr[rows, :] = vn.astype(BF16)

    loop(project)

    u = u_scr[...]
    pos = lax.broadcasted_iota(I32, u.shape, 0)
    up = jnp.where(pos == 0, 0.0, pltpu.roll(u, 1, 0))
    dn = jnp.where(pos == seq - 1, 0.0, pltpu.roll(u, seq - 1, 0))
    conv = up * convw_ref[0:1, :] + u * convw_ref[1:2, :] + dn * convw_ref[2:3, :] + convb_ref[...]
    mix_scr[:, ATTN_WIDTH:ATTN_WIDTH + CONV_WIDTH] = (cb_scr[...] * conv).astype(BF16)

    for n in range(seq // CHUNK):
        rows = pl.ds(n * CHUNK, CHUNK)
        vn = vn_scr[rows, :]
        lane = lax.broadcasted_iota(I32, vn.shape, 1)
        s = sgub_ref[...]
        for hd in range(SGU_HEADS):
            in_head = (lane >= hd * HEAD_DIM) & (lane < (hd + 1) * HEAD_DIM)
            masked = jnp.where(in_head, vn, jnp.zeros_like(vn))
            s = s + _dot(sguw_ref[hd], masked)
        mix_scr[rows, ATTN_WIDTH + CONV_WIDTH:] = (su_scr[rows, :] * s).astype(BF16)

    def attend(r):
        rows = rows_of(r)
        for pair in range(N_Q_HEADS // 2):
            qp = q_scr[rows, pair * LANES:(pair + 1) * LANES]
            kv = pair // (N_Q_HEADS // N_KV_HEADS // 2)
            acc = jnp.zeros((ROW_BLOCK, LANES), F32)
            for parity in range(2):
                s = _dot_nt(qp, kvar_scr[2 * kv + parity])
                p = jnp.exp(s - jnp.max(s, axis=1, keepdims=True))
                denom = jnp.sum(p, axis=1, keepdims=True)
                acc = acc + _dot(p.astype(BF16), vvar_scr[2 * kv + parity]) / denom
            mix_scr[rows, pair * LANES:(pair + 1) * LANES] = acc.astype(BF16)

    loop(attend)

    def finish(r):
        rows = rows_of(r)
        mix = _dot(mix_scr[rows, :], wout_ref[...])
        y = DEEPNORM_ALPHA * x_ref[rows, :] + mod_ref[2:3, :] * mix
        x1_ref[rows, :] = _standardize(y) * ln1g_ref[...] + ln1b_ref[...]

    loop(finish)


def _full(shape):
    n = len(shape)
    return pl.BlockSpec(shape, lambda *_: (0,) * n)


def _resident(shape):
    n = len(shape)
    return pl.BlockSpec(shape, lambda *_: (0,) * n, pipeline_mode=pl.Buffered(1))


def _mixer_scratch(seq, n_cache):
    nk = seq + n_cache
    return [
        pltpu.VMEM((seq, ATTN_WIDTH), BF16),
        pltpu.VMEM((4, nk, LANES), BF16),
        pltpu.VMEM((4, nk, LANES), BF16),
        pltpu.VMEM((seq, CONV_WIDTH), F32),
        pltpu.VMEM((seq, CONV_WIDTH), F32),
        pltpu.VMEM((seq, SGU_WIDTH), F32),
        pltpu.VMEM((seq, SGU_WIDTH), BF16),
        pltpu.VMEM((seq, D_MODEL), BF16),
    ]


def _weight_specs():
    return [
        _resident((D_MODEL, IN_WIDTH)), _resident((D_MODEL, D_MODEL)), _full((1, ATTN_WIDTH)), _full((1, KV_WIDTH)),
        _full((ATTN_WIDTH, ATTN_WIDTH)), _full((SUBLANES, CONV_WIDTH)), _full((1, CONV_WIDTH)),
        _full((1, SGU_WIDTH)), _full((SGU_HEADS, CHUNK, CHUNK)), _full((CHUNK, SGU_WIDTH)),
        _full((1, D_MODEL)), _full((1, D_MODEL)),
    ]


def _mixer_ctx(x, n_seq, seq, mods, lw):
    kernel = functools.partial(_mixer_kernel, seq=seq, n_cache=0, rope=False)
    return pl.pallas_call(
        kernel,
        grid=(n_seq,),
        in_specs=[pl.BlockSpec((seq, D_MODEL), lambda i: (i, 0)),
                  pl.BlockSpec((None, SUBLANES, D_MODEL), lambda i: (0, 0, 0))] + _weight_specs(),
        out_specs=[pl.BlockSpec((seq, D_MODEL), lambda i: (i, 0)),
                   pl.BlockSpec((seq, KV_WIDTH), lambda i: (i, 0)),
                   pl.BlockSpec((seq, KV_WIDTH), lambda i: (i, 0))],
        out_shape=[jax.ShapeDtypeStruct((n_seq * seq, D_MODEL), F32),
                   jax.ShapeDtypeStruct((n_seq * seq, KV_WIDTH), F32),
                   jax.ShapeDtypeStruct((n_seq * seq, KV_WIDTH), F32)],
        scratch_shapes=_mixer_scratch(seq, 0),
        compiler_params=pltpu.CompilerParams(dimension_semantics=("arbitrary",), vmem_limit_bytes=VMEM_LIMIT),
        name="mixer_ctx",
    )(x, mods, *lw)


def _mixer_lat(x, n_seq, seq, mods, lw, cos, sin, kc, vc):
    n_cache = kc.shape[1]
    kernel = functools.partial(_mixer_kernel, seq=seq, n_cache=n_cache, rope=True)
    return pl.pallas_call(
        kernel,
        grid=(n_seq,),
        in_specs=[pl.BlockSpec((seq, D_MODEL), lambda b: (b, 0)),
                  pl.BlockSpec((None, SUBLANES, D_MODEL), lambda b: (1 + b, 0, 0))] + _weight_specs() + [
                  _full((seq, LANES)), _full((seq, LANES)),
                  pl.BlockSpec((None, n_cache, KV_WIDTH), lambda b: (b, 0, 0)),
                  pl.BlockSpec((None, n_cache, KV_WIDTH), lambda b: (b, 0, 0))],
        out_specs=pl.BlockSpec((seq, D_MODEL), lambda b: (b, 0)),
        out_shape=jax.ShapeDtypeStruct((n_seq * seq, D_MODEL), F32),
        scratch_shapes=_mixer_scratch(seq, n_cache),
        compiler_params=pltpu.CompilerParams(dimension_semantics=("arbitrary",), vmem_limit_bytes=VMEM_LIMIT),
        name="mixer_lat",
    )(x, mods, *lw, cos, sin, kc, vc)


def _ffn_kernel(x_ref, mod_ref, w1_ref, w3_ref, w2_ref, g_ref, b_ref, o_ref):
    x = x_ref[...]
    h = (_standardize(x) * (1.0 + mod_ref[4:5, :]) + mod_ref[3:4, :]).astype(BF16)
    half = D_FF // 2
    acc = jnp.zeros(x.shape, F32)
    for c in range(2):
        cols = slice(c * half, (c + 1) * half)
        a = _dot(h, w1_ref[:, cols])
        b = _dot(h, w3_ref[:, cols])
        acc = acc + _dot((_silu(a) * b).astype(BF16), w2_ref[cols, :])
    y = DEEPNORM_ALPHA * x + mod_ref[5:6, :] * acc
    o_ref[...] = _standardize(y) * g_ref[...] + b_ref[...]


def _group_of_tile(i, tile, stream):
    group0, seq = stream
    return group0 if seq is None else group0 + i // (seq // tile)


def _ffn_dense(x_all, mods, w1, w3, w2, g, b, stream):
    n_tok = x_all.shape[0]
    grp = functools.partial(_group_of_tile, tile=FFN_TILE, stream=stream)
    return pl.pallas_call(
        _ffn_kernel,
        grid=(n_tok // FFN_TILE,),
        in_specs=[pl.BlockSpec((FFN_TILE, D_MODEL), lambda i: (i, 0)),
                  pl.BlockSpec((None, SUBLANES, D_MODEL), lambda i: (grp(i), 0, 0)),
                  _resident((D_MODEL, D_FF)), _resident((D_MODEL, D_FF)), _resident((D_FF, D_MODEL)),
                  _full((1, D_MODEL)), _full((1, D_MODEL))],
        out_specs=pl.BlockSpec((FFN_TILE, D_MODEL), lambda i: (i, 0)),
        out_shape=jax.ShapeDtypeStruct(x_all.shape, F32),
        compiler_params=pltpu.CompilerParams(dimension_semantics=("arbitrary",), vmem_limit_bytes=VMEM_LIMIT),
        name="ffn_dense",
    )(x_all, mods, w1, w3, w2, g, b)


def _router_kernel(x_ref, mod_ref, rw_ref, before_ref, dest_ref, gate_ref, start_ref, padded_ref, *, stream):
    w_hi, w_lo = _split(rw_ref[...])
    n_blocks = MOE_CHUNK // ROUTER_BLOCK
    parts = []
    for blk in range(n_blocks):
        rows = pl.ds(blk * ROUTER_BLOCK, ROUTER_BLOCK)
        mod = mod_ref[_group_of_tile(pl.program_id(0) * n_blocks + blk, ROUTER_BLOCK, stream)]
        h = _standardize(x_ref[rows, :]) * (1.0 + mod[4:5, :]) + mod[3:4, :]
        h_hi, h_lo = _split(h)
        parts.append(_dot_nt(w_hi, h_hi) + _dot_nt(w_hi, h_lo) + _dot_nt(w_lo, h_hi))
    logits = jnp.concatenate(parts, axis=1)
    eid = lax.broadcasted_iota(I32, logits.shape, 0).astype(F32)
    m1 = jnp.max(logits, axis=0, keepdims=True)
    i1 = jnp.min(jnp.where(logits == m1, eid, float(N_EXPERTS)), axis=0, keepdims=True)
    oh1 = eid == i1
    rest = jnp.where(oh1, -jnp.inf, logits)
    m2 = jnp.max(rest, axis=0, keepdims=True)
    i2 = jnp.min(jnp.where(rest == m2, eid, float(N_EXPERTS)), axis=0, keepdims=True)
    oh2 = eid == i2
    e = jnp.exp(m2 - m1)
    gate_ref[0:1, :] = 1.0 / (1.0 + e)
    gate_ref[1:2, :] = e / (1.0 + e)
    sel = jnp.where(oh1 | oh2, 1.0, 0.0)
    ranks = []
    seen = jnp.zeros((N_EXPERTS, 1), F32)
    for blk in range(n_blocks):
        s_blk = sel[:, blk * ROUTER_BLOCK:(blk + 1) * ROUTER_BLOCK]
        ranks.append(_dot(s_blk.astype(BF16), before_ref[...]) + seen)
        seen = seen + jnp.sum(s_blk, axis=1, keepdims=True)
    rank = jnp.concatenate(ranks, axis=1)
    eid_out = lax.broadcasted_iota(I32, start_ref.shape, 0).astype(F32)
    start = jnp.zeros(sel.shape, F32)
    start_out = jnp.zeros(start_ref.shape, F32)
    padded_out = jnp.zeros(start_ref.shape, F32)
    for ex in range(N_EXPERTS):
        cnt = jnp.sum(sel[ex:ex + 1, :], axis=1, keepdims=True)
        padded = jnp.ceil(cnt * (1.0 / MOE_GRAN)) * MOE_GRAN
        start = start + jnp.where(eid > ex, padded, 0.0)
        start_out = start_out + jnp.where(eid_out > ex, padded, 0.0)
        padded_out = padded_out + jnp.where(eid_out == ex, padded, 0.0)
    row = (start + rank) * SUBLANES
    dest_ref[0:1, :] = jnp.sum(jnp.where(oh1, row, 0.0), axis=0, keepdims=True).astype(I32)
    dest_ref[1:2, :] = jnp.sum(jnp.where(oh2, row, 0.0), axis=0, keepdims=True).astype(I32)
    start_ref[...] = start_out.astype(I32)
    padded_ref[...] = padded_out.astype(I32)


def _router(x_all, mods, rw_t, stream):
    n_chunks = x_all.shape[0] // MOE_CHUNK
    tok = jnp.arange(ROUTER_BLOCK, dtype=I32)
    before = (tok[:, None] < tok[None, :]).astype(BF16)
    return pl.pallas_call(
        functools.partial(_router_kernel, stream=stream),
        grid=(n_chunks,),
        in_specs=[pl.BlockSpec((MOE_CHUNK, D_MODEL), lambda c: (c, 0)),
                  _full(mods.shape),
                  _full((N_EXPERTS, D_MODEL)), _full((ROUTER_BLOCK, ROUTER_BLOCK))],
        out_specs=[pl.BlockSpec((None, 2, MOE_CHUNK), lambda c: (c, 0, 0)),
                   pl.BlockSpec((None, 2, MOE_CHUNK), lambda c: (c, 0, 0)),
                   pl.BlockSpec((None, N_EXPERTS, LANES), lambda c: (c, 0, 0)),
                   pl.BlockSpec((None, N_EXPERTS, LANES), lambda c: (c, 0, 0))],
        out_shape=[jax.ShapeDtypeStruct((n_chunks, 2, MOE_CHUNK), I32),
                   jax.ShapeDtypeStruct((n_chunks, 2, MOE_CHUNK), F32),
                   jax.ShapeDtypeStruct((n_chunks, N_EXPERTS, LANES), I32),
                   jax.ShapeDtypeStruct((n_chunks, N_EXPERTS, LANES), I32)],
        compiler_params=pltpu.CompilerParams(dimension_semantics=("arbitrary",), vmem_limit_bytes=VMEM_LIMIT),
        name="moe_router",
    )(x_all, mods, rw_t, before)


def _token_rows(first_row):
    return pl.ds(pl.multiple_of(first_row, SUBLANES), SUBLANES)


def _store_token_major(ref, tok0, val):
    for cc in range(D_MODEL // LANES):
        ref[pl.ds(tok0 * SUBLANES + cc, val.shape[0], stride=SUBLANES), :] = val[:, cc * LANES:(cc + 1) * LANES]


def _load_token_major(ref, tok0, n):
    return jnp.concatenate(
        [ref[pl.ds(tok0 * SUBLANES + cc, n, stride=SUBLANES), :] for cc in range(D_MODEL // LANES)], axis=1)


def _moe_kernel(start_ref, padded_ref, x_ref, mod_ref, dest_ref, gate_ref, w1_ref, w3_ref, w2_ref,
                g_ref, b_ref, o_ref, tok_scr, rows_scr):
    c = pl.program_id(0)
    j = pl.program_id(1)
    first_expert_step = MOE_TOK_STEPS
    first_combine_step = MOE_TOK_STEPS + N_EXPERTS

    @pl.when((c == 0) & (j == 0))
    def _init():
        rows_scr[...] = jnp.zeros(rows_scr.shape, F32)

    @pl.when(j < first_expert_step)
    def _dispatch():
        for r in range(MOE_TOK_BLOCK // ROW_BLOCK):
            h = (_standardize(x_ref[pl.ds(r * ROW_BLOCK, ROW_BLOCK), :]) * (1.0 + mod_ref[4:5, :])
                 + mod_ref[3:4, :])
            _store_token_major(tok_scr, r * ROW_BLOCK, h)
        t0 = j * MOE_TOK_BLOCK

        def body(t, carry):
            row = tok_scr[_token_rows(t * SUBLANES), :]
            rows_scr[_token_rows(dest_ref[t0 + t]), :] = row
            rows_scr[_token_rows(dest_ref[MOE_CHUNK + t0 + t]), :] = row
            return carry
        lax.fori_loop(0, MOE_TOK_BLOCK, body, 0, unroll=8)

    def experts(row0, m):
        xin = _load_token_major(rows_scr, row0, m).astype(BF16)
        a = _dot(xin, w1_ref[...])
        b = _dot(xin, w3_ref[...])
        y = _dot((_silu(a) * b).astype(BF16), w2_ref[...])
        _store_token_major(rows_scr, row0, y)

    @pl.when((j >= first_expert_step) & (j < first_combine_step))
    def _experts():
        region = c * N_EXPERTS + (j - first_expert_step)
        start = start_ref[region]
        padded = padded_ref[region]
        n_full = lax.shift_right_logical(padded, MOE_BLOCK.bit_length() - 1)

        def body(i, carry):
            experts(start + i * MOE_BLOCK, MOE_BLOCK)
            return carry
        lax.fori_loop(0, n_full, body, 0)
        for m in range(MOE_GRAN, MOE_BLOCK, MOE_GRAN):
            pl.when(padded - n_full * MOE_BLOCK == m)(
                functools.partial(experts, start + n_full * MOE_BLOCK, m))

    @pl.when(j >= first_combine_step)
    def _combine():
        t0 = (j - first_combine_step) * MOE_TOK_BLOCK

        def body(t, carry):
            y0 = rows_scr[_token_rows(dest_ref[t0 + t]), :]
            y1 = rows_scr[_token_rows(dest_ref[MOE_CHUNK + t0 + t]), :]
            tok_scr[_token_rows(t * SUBLANES), :] = gate_ref[t0 + t] * y0 + gate_ref[MOE_CHUNK + t0 + t] * y1
            return carry
        lax.fori_loop(0, MOE_TOK_BLOCK, body, 0, unroll=8)
        for r in range(MOE_TOK_BLOCK // ROW_BLOCK):
            rows = pl.ds(r * ROW_BLOCK, ROW_BLOCK)
            ffn = _load_token_major(tok_scr, r * ROW_BLOCK, ROW_BLOCK)
            y = DEEPNORM_ALPHA * x_ref[rows, :] + mod_ref[5:6, :] * ffn
            o_ref[rows, :] = _standardize(y) * g_ref[...] + b_ref[...]


def _ffn_moe(x_all, mods, rw_t, w1, w3, w2, g, b, stream):
    n_chunks = x_all.shape[0] // MOE_CHUNK
    dest, gates, start, padded = _router(x_all, mods, rw_t, stream)
    first_expert_step = MOE_TOK_STEPS
    first_combine_step = MOE_TOK_STEPS + N_EXPERTS

    def token_block(c, j):
        blk = jnp.where(j < first_combine_step, jnp.minimum(j, MOE_TOK_STEPS - 1), j - first_combine_step)
        return c * MOE_TOK_STEPS + blk

    def tokens_of(c, j, *_):
        return (token_block(c, j), 0)

    def mod_of(c, j, *_):
        return (_group_of_tile(token_block(c, j), MOE_TOK_BLOCK, stream), 0, 0)

    def out_tokens_of(c, j, *_):
        return (c * MOE_TOK_STEPS + jnp.maximum(j - first_combine_step, 0), 0)

    def expert_of(c, j, *_):
        return (jnp.clip(j - first_expert_step, 0, N_EXPERTS - 1), 0, 0)

    grid_spec = pltpu.PrefetchScalarGridSpec(
        num_scalar_prefetch=2,
        grid=(n_chunks, MOE_STEPS),
        in_specs=[pl.BlockSpec((MOE_TOK_BLOCK, D_MODEL), tokens_of),
                  pl.BlockSpec((None, SUBLANES, D_MODEL), mod_of),
                  pl.BlockSpec((2 * MOE_CHUNK,), lambda c, j, *_: (c,), memory_space=pltpu.SMEM),
                  pl.BlockSpec((2 * MOE_CHUNK,), lambda c, j, *_: (c,), memory_space=pltpu.SMEM),
                  pl.BlockSpec((None, D_MODEL, D_FF_EXPERT), expert_of),
                  pl.BlockSpec((None, D_MODEL, D_FF_EXPERT), expert_of),
                  pl.BlockSpec((None, D_FF_EXPERT, D_MODEL), expert_of),
                  pl.BlockSpec((1, D_MODEL), lambda c, j, *_: (0, 0)),
                  pl.BlockSpec((1, D_MODEL), lambda c, j, *_: (0, 0))],
        out_specs=pl.BlockSpec((MOE_TOK_BLOCK, D_MODEL), out_tokens_of),
        scratch_shapes=[pltpu.VMEM((MOE_TOK_BLOCK * SUBLANES, LANES), F32),
                        pltpu.VMEM((MOE_ROWS * SUBLANES, LANES), F32)],
    )
    return pl.pallas_call(
        _moe_kernel,
        grid_spec=grid_spec,
        out_shape=jax.ShapeDtypeStruct(x_all.shape, F32),
        compiler_params=pltpu.CompilerParams(
            dimension_semantics=("arbitrary", "arbitrary"), vmem_limit_bytes=VMEM_LIMIT),
        name="moe_experts",
    )(start[:, :, 0].reshape(-1), padded[:, :, 0].reshape(-1), x_all, mods,
      dest.reshape(-1), gates.reshape(-1), w1, w3, w2, g, b)


def _rope_tables(n_tokens):
    t = jnp.arange(n_tokens, dtype=F32)
    row = jnp.floor(t / GRID_W)
    col = t - row * GRID_W
    inv_freq = ROPE_THETA ** (-jnp.arange(0, AXIS_ROT, 2, dtype=F32) / AXIS_ROT)
    ang_r = row[:, None] * inv_freq
    ang_c = col[:, None] * inv_freq
    cos = jnp.concatenate([jnp.cos(ang_r), jnp.cos(ang_r), jnp.cos(ang_c), jnp.cos(ang_c)], axis=1)
    sin = jnp.concatenate([-jnp.sin(ang_r), jnp.sin(ang_r), -jnp.sin(ang_c), jnp.sin(ang_c)], axis=1)
    return jnp.tile(cos, (1, 2)), jnp.tile(sin, (1, 2))


def kernel(x_prompt, x_sample, cache_k, cache_v, c, c_ctx, ada_w, ada_b, w_in, q_norm_g, k_norm_g, conv_w, conv_b, sgu_norm_g, sgu_w, sgu_b, w_out, ln1_g, ln1_b, ln2_g, ln2_b, ffn_w1, ffn_w3, ffn_w2, router_w, moe_w1, moe_w3, moe_w2):
    batch, seq, _ = x_prompt.shape
    dec_batch, dec_seq, _ = x_sample.shape
    past_len = cache_k.shape[2]
    n_ctx = batch * seq
    n_lat = dec_batch * dec_seq
    assert 1 + dec_batch <= SUBLANES and seq == ROW_BLOCK and dec_seq % ROW_BLOCK == 0
    assert n_ctx % MOE_CHUNK == 0 and n_lat % MOE_CHUNK == 0
    assert dec_seq % ROUTER_BLOCK == 0 and dec_seq % MOE_TOK_BLOCK == 0 and dec_seq % FFN_TILE == 0

    cond = jnp.zeros((SUBLANES, D_MODEL), F32).at[0].set(c_ctx).at[1:1 + dec_batch].set(c)
    mod = _modulation(cond, ada_w, ada_b)
    mod = mod.reshape(DEPTH, SUBLANES, 6, D_MODEL)[:, :1 + dec_batch]
    mod = jnp.pad(mod, ((0, 0), (0, 0), (0, SUBLANES - 6), (0, 0)))

    lane_id = jnp.arange(ATTN_WIDTH, dtype=I32) // HEAD_DIM
    ones_bd = (lane_id[:, None] == lane_id[None, :]).astype(BF16)
    cos, sin = _rope_tables(dec_seq)

    streams = ((0, None), (1, dec_seq))
    xs = [x_prompt.reshape(n_ctx, D_MODEL), x_sample.reshape(n_lat, D_MODEL)]
    new_ks, new_vs = [], []
    for l in range(DEPTH):
        lw = (
            w_in[l].astype(BF16), w_out[l].astype(BF16),
            jnp.tile(q_norm_g[l], N_Q_HEADS)[None, :], jnp.tile(k_norm_g[l], N_KV_HEADS)[None, :],
            ones_bd,
            jnp.pad(conv_w[l], ((0, SUBLANES - 3), (0, 0))), conv_b[l][None, :],
            sgu_norm_g[l][None, :], sgu_w[l].astype(BF16),
            jnp.repeat(sgu_b[l].T, HEAD_DIM, axis=1),
            ln1_g[l][None, :], ln1_b[l][None, :],
        )
        kc = cache_k[:, l].reshape(dec_batch, past_len, KV_WIDTH)
        vc = cache_v[:, l].reshape(dec_batch, past_len, KV_WIDTH)
        x_ctx, k_ctx, v_ctx = _mixer_ctx(xs[0], batch, seq, mod[l], lw)
        x_lat = _mixer_lat(xs[1], dec_batch, dec_seq, mod[l], lw, cos, sin, kc, vc)
        new_ks.append(k_ctx.reshape(batch, seq, N_KV_HEADS, HEAD_DIM))
        new_vs.append(v_ctx.reshape(batch, seq, N_KV_HEADS, HEAD_DIM))
        i = l // 2
        g2, b2 = ln2_g[l][None, :], ln2_b[l][None, :]
        if l % 2 == 0:
            ws = (ffn_w1[i].astype(BF16), ffn_w3[i].astype(BF16), ffn_w2[i].astype(BF16))
            xs = [_ffn_dense(x, mod[l], *ws, g2, b2, st) for x, st in zip((x_ctx, x_lat), streams)]
        else:
            ws = (moe_w1[i].astype(BF16), moe_w3[i].astype(BF16), moe_w2[i].astype(BF16))
            xs = [_ffn_moe(x, mod[l], router_w[i].T, *ws, g2, b2, st) for x, st in zip((x_ctx, x_lat), streams)]
    y_p = xs[0].reshape(batch, seq, D_MODEL)
    y_s = xs[1].reshape(dec_batch, dec_seq, D_MODEL)
    return (y_p, y_s, jnp.stack(new_ks, axis=1), jnp.stack(new_vs, axis=1))
```

```python
import functools

import numpy as np
import jax
import jax.numpy as jnp
from jax import lax
from jax.experimental import pallas as pl
from jax.experimental.pallas import tpu as pltpu

F32 = jnp.float32
BF16 = jnp.bfloat16
I32 = jnp.int32

D_MODEL = 1024
DEPTH = 2
GRID_W = 64
HEAD_DIM = 64
N_Q_HEADS = 8
N_KV_HEADS = 2
ATTN_WIDTH = N_Q_HEADS * HEAD_DIM
KV_WIDTH = N_KV_HEADS * HEAD_DIM
ATTN_SCALE = HEAD_DIM ** -0.5
ROPE_THETA = 10000.0
AXIS_ROT = HEAD_DIM // 2
CONV_WIDTH = 256
SGU_WIDTH = 256
SGU_HEADS = 4
CHUNK = 128
IN_WIDTH = 2048
D_FF = 2816
N_EXPERTS = 8
D_FF_EXPERT = 1408
EPS = 1e-6
DEEPNORM_ALPHA = (2 * DEPTH) ** 0.25

LANES = 128
SUBLANES = 8
ROW_BLOCK = 512
ATT_BLOCK = 256
CTX_SEQS_PER_STEP = 2
FFN_TILE = 512
MOE_CHUNK = 2048
ROUTER_BLOCK = 1024
MOE_GRAN = 128
MOE_BLOCK = 256
MOE_TOK_BLOCK = 512
MOE_TOK_STEPS = MOE_CHUNK // MOE_TOK_BLOCK
MOE_STEPS = 2 * MOE_TOK_STEPS + N_EXPERTS
MOE_ROWS = 2 * MOE_CHUNK + N_EXPERTS * MOE_GRAN
VMEM_LIMIT = 56 * 1024 * 1024

_Q0, _K0, _V0, _CI0, _CB0, _CC0, _SU0, _SV0 = 0, 512, 640, 768, 1024, 1280, 1536, 1792


def _dot(a, b):
    return jnp.dot(a, b, preferred_element_type=F32)


def _dot_nt(a, b):
    return lax.dot_general(a, b, (((1,), (1,)), ((), ())), preferred_element_type=F32)


def _split(x):
    hi = x.astype(BF16)
    lo = (x - hi.astype(F32)).astype(BF16)
    return hi, lo


def _group_sum(x, ones_bd):
    hi, lo = _split(x)
    return _dot(hi, ones_bd) + _dot(lo, ones_bd)


def _standardize(x):
    mu = jnp.mean(x, axis=-1, keepdims=True)
    d = x - mu
    return d * lax.rsqrt(jnp.mean(d * d, axis=-1, keepdims=True) + EPS)


def _silu(x):
    return x / (1.0 + jnp.exp(-x))


def _modulation_kernel(cond_ref, w_ref, b_ref, o_ref):
    s_hi, s_lo = _split(_silu(cond_ref[...]))
    w_hi, w_lo = _split(w_ref[...])
    o_ref[...] = _dot(s_hi, w_hi) + _dot(s_hi, w_lo) + _dot(s_lo, w_hi) + b_ref[...]


def _modulation(cond, ada_w, ada_b):
    n_out = ada_w.shape[-1]
    tn = 1536
    return pl.pallas_call(
        _modulation_kernel,
        grid=(DEPTH, n_out // tn),
        in_specs=[
            pl.BlockSpec((SUBLANES, D_MODEL), lambda l, j: (0, 0)),
            pl.BlockSpec((None, D_MODEL, tn), lambda l, j: (l, 0, j)),
            pl.BlockSpec((None, 1, tn), lambda l, j: (l, 0, j)),
        ],
        out_specs=pl.BlockSpec((None, SUBLANES, tn), lambda l, j: (l, 0, j)),
        out_shape=jax.ShapeDtypeStruct((DEPTH, SUBLANES, n_out), F32),
        compiler_params=pltpu.CompilerParams(
            dimension_semantics=("arbitrary", "arbitrary"), vmem_limit_bytes=VMEM_LIMIT),
        name="modulation",
    )(cond, ada_w, ada_b.reshape(DEPTH, 1, n_out))


def _rope(x, cos, sin_signed):
    w = x.shape[1]
    lane = lax.broadcasted_iota(I32, x.shape, 1)
    first_half = (lane & 31) < 16
    partner = jnp.where(first_half, pltpu.roll(x, w - 16, 1), pltpu.roll(x, 16, 1))
    return x * cos + partner * sin_signed


def _head_variants(x):
    lane = lax.broadcasted_iota(I32, x.shape, 1)
    lo = lane < HEAD_DIM
    xr = pltpu.roll(x, HEAD_DIM, 1)
    zero = jnp.zeros_like(x)
    return (jnp.where(lo, x, zero).astype(BF16), jnp.where(lo, zero, xr).astype(BF16),
            jnp.where(lo, xr, zero).astype(BF16), jnp.where(lo, zero, x).astype(BF16))


def _mixer_kernel(*refs, seq, n_seq, n_cache, rope, cast_weights, stack_kv):
    refs = list(refs)

    def take(n):
        out, refs[:] = refs[:n], refs[n:]
        return out

    x_ref, mod_ref, win_ref, wout_ref = take(4)
    qg_ref, kg_ref, ones_ref, convw_ref, convb_ref, sgug_ref, sguw_ref, sgub_ref, ln1g_ref, ln1b_ref = take(10)
    if rope:
        cos_ref, sin_ref, kc_ref, vc_ref = take(4)
    if stack_kv:
        kprev_ref, vprev_ref = take(2)
    (x1_ref,) = take(1)
    if not rope:
        k_ref, v_ref = take(2)
    if cast_weights:
        winb_ref, woutb_ref = take(2)

        @pl.when(pl.program_id(0) == 0)
        def _cast():
            winb_ref[...] = win_ref[...].astype(BF16)
            woutb_ref[...] = wout_ref[...].astype(BF16)
        win_ref, wout_ref = winb_ref, woutb_ref
    q_scr, kvar_scr, vvar_scr, u_scr, cb_scr, su_scr, vn_scr, mix_scr = take(8)
    n_rows = n_seq * seq
    assert n_cache == 0 or n_seq == 1

    def loop(n, body):
        if n == 1:
            body(0)
        else:
            def step(r, carry):
                body(r)
                return carry
            lax.fori_loop(0, n, step, 0)

    def block(r, size):
        if isinstance(r, int):
            return pl.ds(r * size, size)
        return pl.ds(pl.multiple_of(r * size, size), size)

    if n_cache:
        for i, var in enumerate(_head_variants(kc_ref[...])):
            kvar_scr[i, pl.ds(seq, n_cache), :] = var
        for i, var in enumerate(_head_variants(vc_ref[...])):
            vvar_scr[i, pl.ds(seq, n_cache), :] = var

    def project(r):
        rows = block(r, ROW_BLOCK)
        x = x_ref[rows, :]
        h = _standardize(x) * (1.0 + mod_ref[1:2, :]) + mod_ref[0:1, :]
        z = _dot(h.astype(BF16), win_ref[...])
        ones_bd = ones_ref[...]
        zq = z[:, _Q0:_K0]
        q = zq * lax.rsqrt(_group_sum(zq * zq, ones_bd) * (1.0 / HEAD_DIM) + EPS) * qg_ref[...]
        zk = z[:, _K0:_V0]
        k = zk * lax.rsqrt(_group_sum(zk * zk, ones_bd[:KV_WIDTH, :KV_WIDTH]) * (1.0 / HEAD_DIM) + EPS) * kg_ref[...]
        v = z[:, _V0:_CI0]
        if rope:
            cos = cos_ref[rows, :]
            sin = sin_ref[rows, :]
            q = _rope(q, jnp.concatenate([cos] * 4, axis=1), jnp.concatenate([sin] * 4, axis=1))
            k = _rope(k, cos, sin)
        elif stack_kv:
            for s in range(ROW_BLOCK // seq):
                sub = slice(s * seq, (s + 1) * seq)
                k_ref[s, 0] = kprev_ref[sub, :]
                v_ref[s, 0] = vprev_ref[sub, :]
                k_ref[s, 1] = k[sub, :]
                v_ref[s, 1] = v[sub, :]
        else:
            k_ref[rows, :] = k
            v_ref[rows, :] = v
        q_scr[rows, :] = (q * ATTN_SCALE).astype(BF16)
        for i, var in enumerate(_head_variants(k)):
            kvar_scr[i, rows, :] = var
        for i, var in enumerate(_head_variants(v)):
            vvar_scr[i, rows, :] = var
        u_scr[rows, :] = z[:, _CC0:_SU0] * z[:, _CI0:_CB0]
        cb_scr[rows, :] = z[:, _CB0:_CC0]
        su_scr[rows, :] = z[:, _SU0:_SV0]
        sv = z[:, _SV0:IN_WIDTH]
        ones_sgu = ones_bd[:SGU_WIDTH, :SGU_WIDTH]
        d = sv - _group_sum(sv, ones_sgu) * (1.0 / HEAD_DIM)
        vn = d * lax.rsqrt(_group_sum(d * d, ones_sgu) * (1.0 / HEAD_DIM) + EPS) * sgug_ref[...]
        vn_scr[rows, :] = vn.astype(BF16)

    loop(n_rows // ROW_BLOCK, project)

    u = u_scr[...]
    pos = lax.broadcasted_iota(I32, u.shape, 0) & (seq - 1)
    up = jnp.where(pos == 0, 0.0, pltpu.roll(u, 1, 0))
    dn = jnp.where(pos == seq - 1, 0.0, pltpu.roll(u, n_rows - 1, 0))
    conv = up * convw_ref[0:1, :] + u * convw_ref[1:2, :] + dn * convw_ref[2:3, :] + convb_ref[...]
    mix_scr[:, ATTN_WIDTH:ATTN_WIDTH + CONV_WIDTH] = (cb_scr[...] * conv).astype(BF16)

    for n in range(n_rows // CHUNK):
        rows = pl.ds(n * CHUNK, CHUNK)
        vn = vn_scr[rows, :]
        lane = lax.broadcasted_iota(I32, vn.shape, 1)
        s = sgub_ref[...]
        for hd in range(SGU_HEADS):
            in_head = (lane >= hd * HEAD_DIM) & (lane < (hd + 1) * HEAD_DIM)
            masked = jnp.where(in_head, vn, jnp.zeros_like(vn))
            s = s + _dot(sguw_ref[hd].astype(BF16), masked)
        mix_scr[rows, ATTN_WIDTH + CONV_WIDTH:] = (su_scr[rows, :] * s).astype(BF16)

    def attend(s, r):
        rows = block(s * (seq // ATT_BLOCK) + r, ATT_BLOCK)
        keys = pl.ds(s * seq, seq + n_cache)
        for pair in range(N_Q_HEADS // 2):
            qp = q_scr[rows, pair * LANES:(pair + 1) * LANES]
            kv = pair // (N_Q_HEADS // N_KV_HEADS // 2)
            acc = jnp.zeros((ATT_BLOCK, LANES), F32)
            for parity in range(2):
                sc = _dot_nt(qp, kvar_scr[2 * kv + parity, keys, :])
                p = jnp.exp(sc - jnp.max(sc, axis=1, keepdims=True))
                denom = jnp.sum(p, axis=1, keepdims=True)
                acc = acc + _dot(p.astype(BF16), vvar_scr[2 * kv + parity, keys, :]) / denom
            mix_scr[rows, pair * LANES:(pair + 1) * LANES] = acc.astype(BF16)

    for s in range(n_seq):
        loop(seq // ATT_BLOCK, functools.partial(attend, s))

    def finish(r):
        rows = block(r, ROW_BLOCK)
        mix = _dot(mix_scr[rows, :], wout_ref[...])
        y = DEEPNORM_ALPHA * x_ref[rows, :] + mod_ref[2:3, :] * mix
        x1_ref[rows, :] = _standardize(y) * ln1g_ref[...] + ln1b_ref[...]

    loop(n_rows // ROW_BLOCK, finish)


def _full(shape):
    n = len(shape)
    return pl.BlockSpec(shape, lambda *_: (0,) * n)


def _resident(shape):
    n = len(shape)
    return pl.BlockSpec(shape, lambda *_: (0,) * n, pipeline_mode=pl.Buffered(1))


def _layer_block(arr, l, resident=False):
    shape = arr.shape[1:]
    kw = dict(pipeline_mode=pl.Buffered(1)) if resident else {}
    return pl.BlockSpec((None,) + shape, lambda *_: (l,) + (0,) * len(shape), **kw)


def _mixer_scratch(n_rows, n_cache):
    nk = n_rows + n_cache
    return [
        pltpu.VMEM((n_rows, ATTN_WIDTH), BF16),
        pltpu.VMEM((4, nk, LANES), BF16),
        pltpu.VMEM((4, nk, LANES), BF16),
        pltpu.VMEM((n_rows, CONV_WIDTH), F32),
        pltpu.VMEM((n_rows, CONV_WIDTH), F32),
        pltpu.VMEM((n_rows, SGU_WIDTH), F32),
        pltpu.VMEM((n_rows, SGU_WIDTH), BF16),
        pltpu.VMEM((n_rows, D_MODEL), BF16),
    ]


def _mixer_ctx(x, l, n_seq, seq, mod, w_in, w_out, small, kv_prev=None):
    per_step = CTX_SEQS_PER_STEP
    rows = per_step * seq
    stack_kv = kv_prev is not None
    kernel = functools.partial(_mixer_kernel, seq=seq, n_seq=per_step, n_cache=0, rope=False,
                               cast_weights=True, stack_kv=stack_kv)
    in_specs = ([pl.BlockSpec((rows, D_MODEL), lambda i: (i, 0)),
                 pl.BlockSpec((None, None, SUBLANES, D_MODEL), lambda i: (l, 0, 0, 0)),
                 _layer_block(w_in, l, resident=True), _layer_block(w_out, l, resident=True)]
                + [_full(a.shape) if a.ndim == 2 else _layer_block(a, l) for a in small])
    args = [x, mod, w_in, w_out, *small]
    if stack_kv:
        in_specs += [pl.BlockSpec((rows, KV_WIDTH), lambda i: (i, 0))] * 2
        args += list(kv_prev)
        kv_spec = pl.BlockSpec((per_step, DEPTH, seq, KV_WIDTH), lambda i: (i, 0, 0, 0))
        kv_shape = jax.ShapeDtypeStruct((n_seq, DEPTH, seq, KV_WIDTH), F32)
    else:
        kv_spec = pl.BlockSpec((rows, KV_WIDTH), lambda i: (i, 0))
        kv_shape = jax.ShapeDtypeStruct((n_seq * seq, KV_WIDTH), F32)
    return pl.pallas_call(
        kernel,
        grid=(n_seq // per_step,),
        in_specs=in_specs,
        out_specs=[pl.BlockSpec((rows, D_MODEL), lambda i: (i, 0)), kv_spec, kv_spec,
                   _full((D_MODEL, IN_WIDTH)), _full((D_MODEL, D_MODEL))],
        out_shape=[jax.ShapeDtypeStruct((n_seq * seq, D_MODEL), F32), kv_shape, kv_shape,
                   jax.ShapeDtypeStruct((D_MODEL, IN_WIDTH), BF16), jax.ShapeDtypeStruct((D_MODEL, D_MODEL), BF16)],
        scratch_shapes=_mixer_scratch(rows, 0),
        compiler_params=pltpu.CompilerParams(dimension_semantics=("arbitrary",), vmem_limit_bytes=VMEM_LIMIT),
        name="mixer_ctx",
    )(*args)


def _mixer_lat(x, l, n_seq, seq, mod, w_in_bf, w_out_bf, small, cos, sin, kc, vc):
    n_cache = kc.shape[2]
    kernel = functools.partial(_mixer_kernel, seq=seq, n_seq=1, n_cache=n_cache, rope=True,
                               cast_weights=False, stack_kv=False)
    cache_spec = pl.BlockSpec((None, None, n_cache, KV_WIDTH), lambda b: (b, l, 0, 0))
    return pl.pallas_call(
        kernel,
        grid=(n_seq,),
        in_specs=([pl.BlockSpec((seq, D_MODEL), lambda b: (b, 0)),
                   pl.BlockSpec((None, None, SUBLANES, D_MODEL), lambda b: (l, 1 + b, 0, 0)),
                   _resident((D_MODEL, IN_WIDTH)), _resident((D_MODEL, D_MODEL))]
                  + [_full(a.shape) if a.ndim == 2 else _layer_block(a, l) for a in small]
                  + [_full((seq, LANES)), _full((seq, LANES)), cache_spec, cache_spec]),
        out_specs=pl.BlockSpec((seq, D_MODEL), lambda b: (b, 0)),
        out_shape=jax.ShapeDtypeStruct((n_seq * seq, D_MODEL), F32),
        scratch_shapes=_mixer_scratch(seq, n_cache),
        compiler_params=pltpu.CompilerParams(dimension_semantics=("arbitrary",), vmem_limit_bytes=VMEM_LIMIT),
        name="mixer_lat",
    )(x, mod, w_in_bf, w_out_bf, *small, cos, sin, kc, vc)


def _ffn_kernel(x_ref, mod_ref, w1_ref, w3_ref, w2_ref, g_ref, b_ref, o_ref):
    x = x_ref[...]
    h = (_standardize(x) * (1.0 + mod_ref[4:5, :]) + mod_ref[3:4, :]).astype(BF16)
    half = D_FF // 2
    acc = jnp.zeros(x.shape, F32)
    for c in range(2):
        cols = slice(c * half, (c + 1) * half)
        a = _dot(h, w1_ref[:, cols])
        b = _dot(h, w3_ref[:, cols])
        acc = acc + _dot((_silu(a) * b).astype(BF16), w2_ref[cols, :])
    y = DEEPNORM_ALPHA * x + mod_ref[5:6, :] * acc
    o_ref[...] = _standardize(y) * g_ref[...] + b_ref[...]


def _group_of_tile(i, tile, stream):
    group0, seq = stream
    return group0 if seq is None else group0 + i // (seq // tile)


def _ffn_dense(x_all, l, mods, w1, w3, w2, g, b, stream):
    n_tok = x_all.shape[0]
    grp = functools.partial(_group_of_tile, tile=FFN_TILE, stream=stream)
    return pl.pallas_call(
        _ffn_kernel,
        grid=(n_tok // FFN_TILE,),
        in_specs=[pl.BlockSpec((FFN_TILE, D_MODEL), lambda i: (i, 0)),
                  pl.BlockSpec((None, None, SUBLANES, D_MODEL), lambda i: (l, grp(i), 0, 0)),
                  _resident((D_MODEL, D_FF)), _resident((D_MODEL, D_FF)), _resident((D_FF, D_MODEL)),
                  _layer_block(g, l), _layer_block(b, l)],
        out_specs=pl.BlockSpec((FFN_TILE, D_MODEL), lambda i: (i, 0)),
        out_shape=jax.ShapeDtypeStruct(x_all.shape, F32),
        compiler_params=pltpu.CompilerParams(dimension_semantics=("arbitrary",), vmem_limit_bytes=VMEM_LIMIT),
        name="ffn_dense",
    )(x_all, mods, w1, w3, w2, g, b)


def _router_kernel(x_ref, mod_ref, rw_ref, before_ref, dest_ref, gate_ref, start_ref, padded_ref, *, stream):
    w_hi, w_lo = _split(rw_ref[...])
    n_blocks = MOE_CHUNK // ROUTER_BLOCK
    parts = []
    for blk in range(n_blocks):
        rows = pl.ds(blk * ROUTER_BLOCK, ROUTER_BLOCK)
        mod = mod_ref[_group_of_tile(pl.program_id(0) * n_blocks + blk, ROUTER_BLOCK, stream)]
        h = _standardize(x_ref[rows, :]) * (1.0 + mod[4:5, :]) + mod[3:4, :]
        h_hi, h_lo = _split(h)
        parts.append(_dot_nt(w_hi, h_hi) + _dot_nt(w_hi, h_lo) + _dot_nt(w_lo, h_hi))
    logits = jnp.concatenate(parts, axis=1)
    eid = lax.broadcasted_iota(I32, logits.shape, 0).astype(F32)
    m1 = jnp.max(logits, axis=0, keepdims=True)
    i1 = jnp.min(jnp.where(logits == m1, eid, float(N_EXPERTS)), axis=0, keepdims=True)
    oh1 = eid == i1
    rest = jnp.where(oh1, -jnp.inf, logits)
    m2 = jnp.max(rest, axis=0, keepdims=True)
    i2 = jnp.min(jnp.where(rest == m2, eid, float(N_EXPERTS)), axis=0, keepdims=True)
    oh2 = eid == i2
    e = jnp.exp(m2 - m1)
    gate_ref[0:1, :] = 1.0 / (1.0 + e)
    gate_ref[1:2, :] = e / (1.0 + e)
    sel = jnp.where(oh1 | oh2, 1.0, 0.0)
    ranks = []
    seen = jnp.zeros((N_EXPERTS, 1), F32)
    for blk in range(n_blocks):
        s_blk = sel[:, blk * ROUTER_BLOCK:(blk + 1) * ROUTER_BLOCK]
        ranks.append(_dot(s_blk.astype(BF16), before_ref[...]) + seen)
        seen = seen + jnp.sum(s_blk, axis=1, keepdims=True)
    rank = jnp.concatenate(ranks, axis=1)
    eid_out = lax.broadcasted_iota(I32, start_ref.shape, 0).astype(F32)
    start = jnp.zeros(sel.shape, F32)
    start_out = jnp.zeros(start_ref.shape, F32)
    padded_out = jnp.zeros(start_ref.shape, F32)
    for ex in range(N_EXPERTS):
        cnt = jnp.sum(sel[ex:ex + 1, :], axis=1, keepdims=True)
        padded = jnp.ceil(cnt * (1.0 / MOE_GRAN)) * MOE_GRAN
        start = start + jnp.where(eid > ex, padded, 0.0)
        start_out = start_out + jnp.where(eid_out > ex, padded, 0.0)
        padded_out = padded_out + jnp.where(eid_out == ex, padded, 0.0)
    row = (start + rank) * SUBLANES
    dest_ref[0:1, :] = jnp.sum(jnp.where(oh1, row, 0.0), axis=0, keepdims=True).astype(I32)
    dest_ref[1:2, :] = jnp.sum(jnp.where(oh2, row, 0.0), axis=0, keepdims=True).astype(I32)
    start_ref[...] = start_out.astype(I32)
    padded_ref[...] = padded_out.astype(I32)


def _router(x_all, l, mods, rw_t, stream):
    n_chunks = x_all.shape[0] // MOE_CHUNK
    tok = np.arange(ROUTER_BLOCK)
    before = jnp.asarray(tok[:, None] < tok[None, :], BF16)
    return pl.pallas_call(
        functools.partial(_router_kernel, stream=stream),
        grid=(n_chunks,),
        in_specs=[pl.BlockSpec((MOE_CHUNK, D_MODEL), lambda c: (c, 0)),
                  _layer_block(mods, l),
                  _full((N_EXPERTS, D_MODEL)), _full((ROUTER_BLOCK, ROUTER_BLOCK))],
        out_specs=[pl.BlockSpec((None, 2, MOE_CHUNK), lambda c: (c, 0, 0)),
                   pl.BlockSpec((None, 2, MOE_CHUNK), lambda c: (c, 0, 0)),
                   pl.BlockSpec((None, N_EXPERTS, LANES), lambda c: (c, 0, 0)),
                   pl.BlockSpec((None, N_EXPERTS, LANES), lambda c: (c, 0, 0))],
        out_shape=[jax.ShapeDtypeStruct((n_chunks, 2, MOE_CHUNK), I32),
                   jax.ShapeDtypeStruct((n_chunks, 2, MOE_CHUNK), F32),
                   jax.ShapeDtypeStruct((n_chunks, N_EXPERTS, LANES), I32),
                   jax.ShapeDtypeStruct((n_chunks, N_EXPERTS, LANES), I32)],
        compiler_params=pltpu.CompilerParams(dimension_semantics=("arbitrary",), vmem_limit_bytes=VMEM_LIMIT),
        name="moe_router",
    )(x_all, mods, rw_t, before)


def _token_rows(first_row):
    return pl.ds(pl.multiple_of(first_row, SUBLANES), SUBLANES)


def _store_token_major(ref, tok0, val):
    for cc in range(D_MODEL // LANES):
        ref[pl.ds(tok0 * SUBLANES + cc, val.shape[0], stride=SUBLANES), :] = val[:, cc * LANES:(cc + 1) * LANES]


def _load_token_major(ref, tok0, n):
    return jnp.concatenate(
        [ref[pl.ds(tok0 * SUBLANES + cc, n, stride=SUBLANES), :] for cc in range(D_MODEL // LANES)], axis=1)


def _moe_kernel(start_ref, padded_ref, x_ref, mod_ref, dest_ref, gate_ref, w1_ref, w3_ref, w2_ref,
                g_ref, b_ref, o_ref, tok_scr, rows_scr):
    c = pl.program_id(0)
    j = pl.program_id(1)
    first_expert_step = MOE_TOK_STEPS
    first_combine_step = MOE_TOK_STEPS + N_EXPERTS

    @pl.when((c == 0) & (j == 0))
    def _init():
        rows_scr[...] = jnp.zeros(rows_scr.shape, F32)

    @pl.when(j < first_expert_step)
    def _dispatch():
        for r in range(MOE_TOK_BLOCK // ROW_BLOCK):
            h = (_standardize(x_ref[pl.ds(r * ROW_BLOCK, ROW_BLOCK), :]) * (1.0 + mod_ref[4:5, :])
                 + mod_ref[3:4, :])
            _store_token_major(tok_scr, r * ROW_BLOCK, h)
        t0 = j * MOE_TOK_BLOCK

        def body(t, carry):
            row = tok_scr[_token_rows(t * SUBLANES), :]
            rows_scr[_token_rows(dest_ref[t0 + t]), :] = row
            rows_scr[_token_rows(dest_ref[MOE_CHUNK + t0 + t]), :] = row
            return carry
        lax.fori_loop(0, MOE_TOK_BLOCK, body, 0, unroll=8)

    def experts(row0, m):
        xin = _load_token_major(rows_scr, row0, m).astype(BF16)
        a = _dot(xin, w1_ref[...])
        b = _dot(xin, w3_ref[...])
        y = _dot((_silu(a) * b).astype(BF16), w2_ref[...])
        _store_token_major(rows_scr, row0, y)

    @pl.when((j >= first_expert_step) & (j < first_combine_step))
    def _experts():
        region = c * N_EXPERTS + (j - first_expert_step)
        start = start_ref[region]
        padded = padded_ref[region]
        n_full = lax.shift_right_logical(padded, MOE_BLOCK.bit_length() - 1)

        def body(i, carry):
            experts(start + i * MOE_BLOCK, MOE_BLOCK)
            return carry
        lax.fori_loop(0, n_full, body, 0)
        for m in range(MOE_GRAN, MOE_BLOCK, MOE_GRAN):
            pl.when(padded - n_full * MOE_BLOCK == m)(
                functools.partial(experts, start + n_full * MOE_BLOCK, m))

    @pl.when(j >= first_combine_step)
    def _combine():
        t0 = (j - first_combine_step) * MOE_TOK_BLOCK

        def body(t, carry):
            y0 = rows_scr[_token_rows(dest_ref[t0 + t]), :]
            y1 = rows_scr[_token_rows(dest_ref[MOE_CHUNK + t0 + t]), :]
            tok_scr[_token_rows(t * SUBLANES), :] = gate_ref[t0 + t] * y0 + gate_ref[MOE_CHUNK + t0 + t] * y1
            return carry
        lax.fori_loop(0, MOE_TOK_BLOCK, body, 0, unroll=8)
        for r in range(MOE_TOK_BLOCK // ROW_BLOCK):
            rows = pl.ds(r * ROW_BLOCK, ROW_BLOCK)
            ffn = _load_token_major(tok_scr, r * ROW_BLOCK, ROW_BLOCK)
            y = DEEPNORM_ALPHA * x_ref[rows, :] + mod_ref[5:6, :] * ffn
            o_ref[rows, :] = _standardize(y) * g_ref[...] + b_ref[...]


def _ffn_moe(x_all, l, mods, rw_t, w1, w3, w2, g, b, stream):
    n_chunks = x_all.shape[0] // MOE_CHUNK
    dest, gates, start, padded = _router(x_all, l, mods, rw_t, stream)
    first_expert_step = MOE_TOK_STEPS
    first_combine_step = MOE_TOK_STEPS + N_EXPERTS

    def token_block(c, j):
        blk = jnp.where(j < first_combine_step, jnp.minimum(j, MOE_TOK_STEPS - 1), j - first_combine_step)
        return c * MOE_TOK_STEPS + blk

    def tokens_of(c, j, *_):
        return (token_block(c, j), 0)

    def mod_of(c, j, *_):
        return (l, _group_of_tile(token_block(c, j), MOE_TOK_BLOCK, stream), 0, 0)

    def out_tokens_of(c, j, *_):
        return (c * MOE_TOK_STEPS + jnp.maximum(j - first_combine_step, 0), 0)

    def expert_of(c, j, *_):
        return (jnp.clip(j - first_expert_step, 0, N_EXPERTS - 1), 0, 0)

    grid_spec = pltpu.PrefetchScalarGridSpec(
        num_scalar_prefetch=2,
        grid=(n_chunks, MOE_STEPS),
        in_specs=[pl.BlockSpec((MOE_TOK_BLOCK, D_MODEL), tokens_of),
                  pl.BlockSpec((None, None, SUBLANES, D_MODEL), mod_of),
                  pl.BlockSpec((2 * MOE_CHUNK,), lambda c, j, *_: (c,), memory_space=pltpu.SMEM),
                  pl.BlockSpec((2 * MOE_CHUNK,), lambda c, j, *_: (c,), memory_space=pltpu.SMEM),
                  pl.BlockSpec((None, D_MODEL, D_FF_EXPERT), expert_of),
                  pl.BlockSpec((None, D_MODEL, D_FF_EXPERT), expert_of),
                  pl.BlockSpec((None, D_FF_EXPERT, D_MODEL), expert_of),
                  _layer_block(g, l), _layer_block(b, l)],
        out_specs=pl.BlockSpec((MOE_TOK_BLOCK, D_MODEL), out_tokens_of),
        scratch_shapes=[pltpu.VMEM((MOE_TOK_BLOCK * SUBLANES, LANES), F32),
                        pltpu.VMEM((MOE_ROWS * SUBLANES, LANES), F32)],
    )
    return pl.pallas_call(
        _moe_kernel,
        grid_spec=grid_spec,
        out_shape=jax.ShapeDtypeStruct(x_all.shape, F32),
        compiler_params=pltpu.CompilerParams(
            dimension_semantics=("arbitrary", "arbitrary"), vmem_limit_bytes=VMEM_LIMIT),
        name="moe_experts",
    )(start[:, :, 0].reshape(-1), padded[:, :, 0].reshape(-1), x_all, mods,
      dest.reshape(-1), gates.reshape(-1), w1, w3, w2, g, b)


def _rope_tables(n_tokens):
    t = np.arange(n_tokens)
    row = (t // GRID_W).astype(np.float32)
    col = (t % GRID_W).astype(np.float32)
    inv_freq = (np.float32(ROPE_THETA) ** (-np.arange(0, AXIS_ROT, 2, dtype=np.float32) / AXIS_ROT)).astype(np.float32)
    ang_r = row[:, None] * inv_freq
    ang_c = col[:, None] * inv_freq
    cos = np.concatenate([np.cos(ang_r), np.cos(ang_r), np.cos(ang_c), np.cos(ang_c)], axis=1)
    sin = np.concatenate([-np.sin(ang_r), np.sin(ang_r), -np.sin(ang_c), np.sin(ang_c)], axis=1)
    return jnp.asarray(np.tile(cos, (1, 2)), F32), jnp.asarray(np.tile(sin, (1, 2)), F32)


def kernel(x_prompt, x_sample, cache_k, cache_v, c, c_ctx, ada_w, ada_b, w_in, q_norm_g, k_norm_g, conv_w, conv_b, sgu_norm_g, sgu_w, sgu_b, w_out, ln1_g, ln1_b, ln2_g, ln2_b, ffn_w1, ffn_w3, ffn_w2, router_w, moe_w1, moe_w3, moe_w2):
    batch, seq, _ = x_prompt.shape
    dec_batch, dec_seq, _ = x_sample.shape
    past_len = cache_k.shape[2]
    n_ctx = batch * seq
    n_lat = dec_batch * dec_seq
    assert DEPTH == 2 and 1 + dec_batch <= SUBLANES
    assert seq == ATT_BLOCK and CTX_SEQS_PER_STEP * seq == ROW_BLOCK and batch % CTX_SEQS_PER_STEP == 0
    assert dec_seq % ROW_BLOCK == 0 and dec_seq & (dec_seq - 1) == 0
    assert n_ctx % MOE_CHUNK == 0 and n_lat % MOE_CHUNK == 0
    assert dec_seq % ROUTER_BLOCK == 0 and dec_seq % MOE_TOK_BLOCK == 0 and dec_seq % FFN_TILE == 0

    cond = jnp.zeros((SUBLANES, D_MODEL), F32).at[0].set(c_ctx).at[1:1 + dec_batch].set(c)
    mod = _modulation(cond, ada_w, ada_b)
    mod = mod.reshape(DEPTH, SUBLANES, 6, D_MODEL)[:, :1 + dec_batch]
    mod = jnp.pad(mod, ((0, 0), (0, 0), (0, SUBLANES - 6), (0, 0)))

    lane_id = np.arange(ATTN_WIDTH) // HEAD_DIM
    ones_bd = jnp.asarray(lane_id[:, None] == lane_id[None, :], BF16)
    cos, sin = _rope_tables(dec_seq)
    small = (
        jnp.tile(q_norm_g, (1, N_Q_HEADS))[:, None, :], jnp.tile(k_norm_g, (1, N_KV_HEADS))[:, None, :],
        ones_bd,
        conv_w, conv_b[:, None, :], sgu_norm_g[:, None, :], sgu_w,
        jnp.repeat(jnp.swapaxes(sgu_b, 1, 2), HEAD_DIM, axis=2),
        ln1_g[:, None, :], ln1_b[:, None, :],
    )
    g2, b2 = ln2_g[:, None, :], ln2_b[:, None, :]
    kc = cache_k.reshape(dec_batch, DEPTH, past_len, KV_WIDTH)
    vc = cache_v.reshape(dec_batch, DEPTH, past_len, KV_WIDTH)

    streams = ((0, None), (1, dec_seq))
    xs = [x_prompt.reshape(n_ctx, D_MODEL), x_sample.reshape(n_lat, D_MODEL)]
    kv = None
    for l in range(DEPTH):
        x_ctx, k_ctx, v_ctx, w_in_bf, w_out_bf = _mixer_ctx(xs[0], l, batch, seq, mod, w_in, w_out, small, kv)
        kv = (k_ctx, v_ctx)
        x_lat = _mixer_lat(xs[1], l, dec_batch, dec_seq, mod, w_in_bf, w_out_bf, small, cos, sin, kc, vc)
        i = l // 2
        if l % 2 == 0:
            ws = (ffn_w1[i].astype(BF16), ffn_w3[i].astype(BF16), ffn_w2[i].astype(BF16))
            xs = [_ffn_dense(x, l, mod, *ws, g2, b2, st) for x, st in zip((x_ctx, x_lat), streams)]
        else:
            ws = (moe_w1[i].astype(BF16), moe_w3[i].astype(BF16), moe_w2[i].astype(BF16))
            xs = [_ffn_moe(x, l, mod, router_w[i].T, *ws, g2, b2, st) for x, st in zip((x_ctx, x_lat), streams)]
    y_p = xs[0].reshape(batch, seq, D_MODEL)
    y_s = xs[1].reshape(dec_batch, dec_seq, D_MODEL)
    new_k = kv[0].reshape(batch, DEPTH, seq, N_KV_HEADS, HEAD_DIM)
    new_v = kv[1].reshape(batch, DEPTH, seq, N_KV_HEADS, HEAD_DIM)
    return (y_p, y_s, new_k, new_v)
```

```python
import functools

import numpy as np
import jax
import jax.numpy as jnp
from jax import lax
from jax.experimental import pallas as pl
from jax.experimental.pallas import tpu as pltpu

F32 = jnp.float32
BF16 = jnp.bfloat16
I32 = jnp.int32

D_MODEL = 1024
DEPTH = 2
GRID_W = 64
HEAD_DIM = 64
N_Q_HEADS = 8
N_KV_HEADS = 2
ATTN_WIDTH = N_Q_HEADS * HEAD_DIM
KV_WIDTH = N_KV_HEADS * HEAD_DIM
ATTN_SCALE = HEAD_DIM ** -0.5
ROPE_THETA = 10000.0
AXIS_ROT = HEAD_DIM // 2
CONV_WIDTH = 256
SGU_WIDTH = 256
SGU_HEADS = 4
CHUNK = 128
IN_WIDTH = 2048
D_FF = 2816
N_EXPERTS = 8
D_FF_EXPERT = 1408
EPS = 1e-6
DEEPNORM_ALPHA = (2 * DEPTH) ** 0.25

LANES = 128
SUBLANES = 8
ROW_BLOCK = 512
ATT_BLOCK = 256
CTX_SEQS_PER_STEP = 2
FFN_TILE = 512
FFN_COLS = 256
MOE_CHUNK = 2048
ROUTER_BLOCK = 1024
MOE_GRAN = 128
MOE_BLOCK = 256
MOE_TOK_BLOCK = 512
MOE_TOK_STEPS = MOE_CHUNK // MOE_TOK_BLOCK
MOE_STEPS = 2 * MOE_TOK_STEPS + N_EXPERTS
MOE_ROWS = 2 * MOE_CHUNK + N_EXPERTS * MOE_GRAN
VMEM_LIMIT = 58 * 1024 * 1024

_Q0, _K0, _V0, _CI0, _CB0, _CC0, _SU0, _SV0 = 0, 512, 640, 768, 1024, 1280, 1536, 1792


def _dot(a, b):
    return jnp.dot(a, b, preferred_element_type=F32)


def _dot_nt(a, b):
    return lax.dot_general(a, b, (((1,), (1,)), ((), ())), preferred_element_type=F32)


def _split(x):
    hi = x.astype(BF16)
    lo = (x - hi.astype(F32)).astype(BF16)
    return hi, lo


def _group_sum(x, ones_bd):
    hi, lo = _split(x)
    return _dot(hi, ones_bd) + _dot(lo, ones_bd)


def _standardize(x):
    mu = jnp.mean(x, axis=-1, keepdims=True)
    d = x - mu
    return d * lax.rsqrt(jnp.mean(d * d, axis=-1, keepdims=True) + EPS)


def _silu(x):
    return x / (1.0 + jnp.exp(-x))


def _modulation_kernel(cond_ref, w_ref, b_ref, o_ref):
    s_hi, s_lo = _split(_silu(cond_ref[...]))
    w_hi, w_lo = _split(w_ref[...])
    o_ref[...] = _dot(s_hi, w_hi) + _dot(s_hi, w_lo) + _dot(s_lo, w_hi) + b_ref[...]


def _modulation(cond, ada_w, ada_b):
    n_out = ada_w.shape[-1]
    tn = 1536
    return pl.pallas_call(
        _modulation_kernel,
        grid=(DEPTH, n_out // tn),
        in_specs=[
            pl.BlockSpec((SUBLANES, D_MODEL), lambda l, j: (0, 0)),
            pl.BlockSpec((None, D_MODEL, tn), lambda l, j: (l, 0, j)),
            pl.BlockSpec((None, 1, tn), lambda l, j: (l, 0, j)),
        ],
        out_specs=pl.BlockSpec((None, SUBLANES, tn), lambda l, j: (l, 0, j)),
        out_shape=jax.ShapeDtypeStruct((DEPTH, SUBLANES, n_out), F32),
        compiler_params=pltpu.CompilerParams(
            dimension_semantics=("arbitrary", "arbitrary"), vmem_limit_bytes=VMEM_LIMIT),
        name="modulation",
    )(cond, ada_w, ada_b.reshape(DEPTH, 1, n_out))


def _rope(x, cos, sin_signed):
    w = x.shape[1]
    lane = lax.broadcasted_iota(I32, x.shape, 1)
    first_half = (lane & 31) < 16
    partner = jnp.where(first_half, pltpu.roll(x, w - 16, 1), pltpu.roll(x, 16, 1))
    return x * cos + partner * sin_signed


def _head_variants(x):
    lane = lax.broadcasted_iota(I32, x.shape, 1)
    lo = lane < HEAD_DIM
    xr = pltpu.roll(x, HEAD_DIM, 1)
    zero = jnp.zeros_like(x)
    return (jnp.where(lo, x, zero).astype(BF16), jnp.where(lo, zero, xr).astype(BF16),
            jnp.where(lo, xr, zero).astype(BF16), jnp.where(lo, zero, x).astype(BF16))


def _mixer_kernel(*refs, seq, n_seq, n_cache, rope, cast_weights, stack_kv):
    refs = list(refs)

    def take(n):
        out, refs[:] = refs[:n], refs[n:]
        return out

    x_ref, mod_ref, win_ref, wout_ref = take(4)
    qg_ref, kg_ref, ones_ref, convw_ref, convb_ref, sgug_ref, sguw_ref, sgub_ref, ln1g_ref, ln1b_ref = take(10)
    if rope:
        cos_ref, sin_ref, kc_ref, vc_ref = take(4)
    if stack_kv:
        kprev_ref, vprev_ref = take(2)
    (x1_ref,) = take(1)
    if not rope:
        k_ref, v_ref = take(2)
    if cast_weights:
        winb_ref, woutb_ref = take(2)

        @pl.when(pl.program_id(0) == 0)
        def _cast():
            winb_ref[...] = win_ref[...].astype(BF16)
            woutb_ref[...] = wout_ref[...].astype(BF16)
        win_ref, wout_ref = winb_ref, woutb_ref
    q_scr, kvar_scr, vvar_scr, u_scr, cb_scr, su_scr, vn_scr, mix_scr = take(8)
    n_rows = n_seq * seq
    assert n_cache == 0 or n_seq == 1

    def loop(n, body):
        if n == 1:
            body(0)
        else:
            def step(r, carry):
                body(r)
                return carry
            lax.fori_loop(0, n, step, 0)

    def block(r, size):
        if isinstance(r, int):
            return pl.ds(r * size, size)
        return pl.ds(pl.multiple_of(r * size, size), size)

    if n_cache:
        for i, var in enumerate(_head_variants(kc_ref[...])):
            kvar_scr[i, pl.ds(seq, n_cache), :] = var
        for i, var in enumerate(_head_variants(vc_ref[...])):
            vvar_scr[i, pl.ds(seq, n_cache), :] = var

    def project(r):
        rows = block(r, ROW_BLOCK)
        x = x_ref[rows, :]
        h = _standardize(x) * (1.0 + mod_ref[1:2, :]) + mod_ref[0:1, :]
        z = _dot(h.astype(BF16), win_ref[...])
        ones_bd = ones_ref[...]
        zq = z[:, _Q0:_K0]
        q = zq * lax.rsqrt(_group_sum(zq * zq, ones_bd) * (1.0 / HEAD_DIM) + EPS) * qg_ref[...]
        zk = z[:, _K0:_V0]
        k = zk * lax.rsqrt(_group_sum(zk * zk, ones_bd[:KV_WIDTH, :KV_WIDTH]) * (1.0 / HEAD_DIM) + EPS) * kg_ref[...]
        v = z[:, _V0:_CI0]
        if rope:
            cos = cos_ref[rows, :]
            sin = sin_ref[rows, :]
            q = _rope(q, jnp.concatenate([cos] * 4, axis=1), jnp.concatenate([sin] * 4, axis=1))
            k = _rope(k, cos, sin)
        elif stack_kv:
            for s in range(ROW_BLOCK // seq):
                sub = slice(s * seq, (s + 1) * seq)
                k_ref[s, 0] = kprev_ref[sub, :]
                v_ref[s, 0] = vprev_ref[sub, :]
                k_ref[s, 1] = k[sub, :]
                v_ref[s, 1] = v[sub, :]
        else:
            k_ref[rows, :] = k
            v_ref[rows, :] = v
        q_scr[rows, :] = (q * ATTN_SCALE).astype(BF16)
        for i, var in enumerate(_head_variants(k)):
            kvar_scr[i, rows, :] = var
        for i, var in enumerate(_head_variants(v)):
            vvar_scr[i, rows, :] = var
        u_scr[rows, :] = z[:, _CC0:_SU0] * z[:, _CI0:_CB0]
        cb_scr[rows, :] = z[:, _CB0:_CC0]
        su_scr[rows, :] = z[:, _SU0:_SV0]
        sv = z[:, _SV0:IN_WIDTH]
        ones_sgu = ones_bd[:SGU_WIDTH, :SGU_WIDTH]
        d = sv - _group_sum(sv, ones_sgu) * (1.0 / HEAD_DIM)
        vn = d * lax.rsqrt(_group_sum(d * d, ones_sgu) * (1.0 / HEAD_DIM) + EPS) * sgug_ref[...]
        vn_scr[rows, :] = vn.astype(BF16)

    loop(n_rows // ROW_BLOCK, project)

    u = u_scr[...]
    pos = lax.broadcasted_iota(I32, u.shape, 0) & (seq - 1)
    up = jnp.where(pos == 0, 0.0, pltpu.roll(u, 1, 0))
    dn = jnp.where(pos == seq - 1, 0.0, pltpu.roll(u, n_rows - 1, 0))
    conv = up * convw_ref[0:1, :] + u * convw_ref[1:2, :] + dn * convw_ref[2:3, :] + convb_ref[...]
    mix_scr[:, ATTN_WIDTH:ATTN_WIDTH + CONV_WIDTH] = (cb_scr[...] * conv).astype(BF16)

    for n in range(n_rows // CHUNK):
        rows = pl.ds(n * CHUNK, CHUNK)
        vn = vn_scr[rows, :]
        lane = lax.broadcasted_iota(I32, vn.shape, 1)
        s = sgub_ref[...]
        for hd in range(SGU_HEADS):
            in_head = (lane >= hd * HEAD_DIM) & (lane < (hd + 1) * HEAD_DIM)
            masked = jnp.where(in_head, vn, jnp.zeros_like(vn))
            s = s + _dot(sguw_ref[hd].astype(BF16), masked)
        mix_scr[rows, ATTN_WIDTH + CONV_WIDTH:] = (su_scr[rows, :] * s).astype(BF16)

    def attend(s, r):
        rows = block(s * (seq // ATT_BLOCK) + r, ATT_BLOCK)
        keys = pl.ds(s * seq, seq + n_cache)
        for pair in range(N_Q_HEADS // 2):
            qp = q_scr[rows, pair * LANES:(pair + 1) * LANES]
            kv = pair // (N_Q_HEADS // N_KV_HEADS // 2)
            acc = jnp.zeros((ATT_BLOCK, LANES), F32)
            for parity in range(2):
                sc = _dot_nt(qp, kvar_scr[2 * kv + parity, keys, :])
                p = jnp.exp(sc - jnp.max(sc, axis=1, keepdims=True))
                denom = jnp.sum(p, axis=1, keepdims=True)
                acc = acc + _dot(p.astype(BF16), vvar_scr[2 * kv + parity, keys, :]) / denom
            mix_scr[rows, pair * LANES:(pair + 1) * LANES] = acc.astype(BF16)

    for s in range(n_seq):
        loop(seq // ATT_BLOCK, functools.partial(attend, s))

    def finish(r):
        rows = block(r, ROW_BLOCK)
        mix = _dot(mix_scr[rows, :], wout_ref[...])
        y = DEEPNORM_ALPHA * x_ref[rows, :] + mod_ref[2:3, :] * mix
        x1_ref[rows, :] = _standardize(y) * ln1g_ref[...] + ln1b_ref[...]

    loop(n_rows // ROW_BLOCK, finish)


def _full(shape):
    n = len(shape)
    return pl.BlockSpec(shape, lambda *_: (0,) * n)


def _resident(shape):
    n = len(shape)
    return pl.BlockSpec(shape, lambda *_: (0,) * n, pipeline_mode=pl.Buffered(1))


def _layer_block(arr, l, resident=False):
    shape = arr.shape[1:]
    kw = dict(pipeline_mode=pl.Buffered(1)) if resident else {}
    return pl.BlockSpec((None,) + shape, lambda *_: (l,) + (0,) * len(shape), **kw)


def _mixer_scratch(n_rows, n_cache):
    nk = n_rows + n_cache
    return [
        pltpu.VMEM((n_rows, ATTN_WIDTH), BF16),
        pltpu.VMEM((4, nk, LANES), BF16),
        pltpu.VMEM((4, nk, LANES), BF16),
        pltpu.VMEM((n_rows, CONV_WIDTH), F32),
        pltpu.VMEM((n_rows, CONV_WIDTH), F32),
        pltpu.VMEM((n_rows, SGU_WIDTH), F32),
        pltpu.VMEM((n_rows, SGU_WIDTH), BF16),
        pltpu.VMEM((n_rows, D_MODEL), BF16),
    ]


def _mixer_ctx(x, l, n_seq, seq, mod, w_in, w_out, small, kv_prev=None):
    per_step = CTX_SEQS_PER_STEP
    rows = per_step * seq
    stack_kv = kv_prev is not None
    kernel = functools.partial(_mixer_kernel, seq=seq, n_seq=per_step, n_cache=0, rope=False,
                               cast_weights=True, stack_kv=stack_kv)
    in_specs = ([pl.BlockSpec((rows, D_MODEL), lambda i: (i, 0)),
                 pl.BlockSpec((None, None, SUBLANES, D_MODEL), lambda i: (l, 0, 0, 0)),
                 _layer_block(w_in, l, resident=True), _layer_block(w_out, l, resident=True)]
                + [_full(a.shape) if a.ndim == 2 else _layer_block(a, l) for a in small])
    args = [x, mod, w_in, w_out, *small]
    if stack_kv:
        in_specs += [pl.BlockSpec((rows, KV_WIDTH), lambda i: (i, 0))] * 2
        args += list(kv_prev)
        kv_spec = pl.BlockSpec((per_step, DEPTH, seq, KV_WIDTH), lambda i: (i, 0, 0, 0))
        kv_shape = jax.ShapeDtypeStruct((n_seq, DEPTH, seq, KV_WIDTH), F32)
    else:
        kv_spec = pl.BlockSpec((rows, KV_WIDTH), lambda i: (i, 0))
        kv_shape = jax.ShapeDtypeStruct((n_seq * seq, KV_WIDTH), F32)
    return pl.pallas_call(
        kernel,
        grid=(n_seq // per_step,),
        in_specs=in_specs,
        out_specs=[pl.BlockSpec((rows, D_MODEL), lambda i: (i, 0)), kv_spec, kv_spec,
                   _full((D_MODEL, IN_WIDTH)), _full((D_MODEL, D_MODEL))],
        out_shape=[jax.ShapeDtypeStruct((n_seq * seq, D_MODEL), F32), kv_shape, kv_shape,
                   jax.ShapeDtypeStruct((D_MODEL, IN_WIDTH), BF16), jax.ShapeDtypeStruct((D_MODEL, D_MODEL), BF16)],
        scratch_shapes=_mixer_scratch(rows, 0),
        compiler_params=pltpu.CompilerParams(dimension_semantics=("arbitrary",), vmem_limit_bytes=VMEM_LIMIT),
        name="mixer_ctx",
    )(*args)


def _mixer_lat(x, l, n_seq, seq, mod, w_in_bf, w_out_bf, small, cos, sin, kc, vc):
    n_cache = kc.shape[2]
    kernel = functools.partial(_mixer_kernel, seq=seq, n_seq=1, n_cache=n_cache, rope=True,
                               cast_weights=False, stack_kv=False)
    cache_spec = pl.BlockSpec((None, None, n_cache, KV_WIDTH), lambda b: (b, l, 0, 0))
    return pl.pallas_call(
        kernel,
        grid=(n_seq,),
        in_specs=([pl.BlockSpec((seq, D_MODEL), lambda b: (b, 0)),
                   pl.BlockSpec((None, None, SUBLANES, D_MODEL), lambda b: (l, 1 + b, 0, 0)),
                   _resident((D_MODEL, IN_WIDTH)), _resident((D_MODEL, D_MODEL))]
                  + [_full(a.shape) if a.ndim == 2 else _layer_block(a, l) for a in small]
                  + [_full((seq, LANES)), _full((seq, LANES)), cache_spec, cache_spec]),
        out_specs=pl.BlockSpec((seq, D_MODEL), lambda b: (b, 0)),
        out_shape=jax.ShapeDtypeStruct((n_seq * seq, D_MODEL), F32),
        scratch_shapes=_mixer_scratch(seq, n_cache),
        compiler_params=pltpu.CompilerParams(dimension_semantics=("arbitrary",), vmem_limit_bytes=VMEM_LIMIT),
        name="mixer_lat",
    )(x, mod, w_in_bf, w_out_bf, *small, cos, sin, kc, vc)


def _two_stream_specs(block_rows, n_a, block_of):
    spec_a = pl.BlockSpec((block_rows, D_MODEL), lambda *ids: (jnp.minimum(block_of(*ids), n_a - 1), 0))
    spec_b = pl.BlockSpec((block_rows, D_MODEL), lambda *ids: (jnp.maximum(block_of(*ids) - n_a, 0), 0))
    return [spec_a, spec_b]


def _group_of_block(blk, block_rows, n_a, lat_seq):
    return jnp.where(blk < n_a, 0, 1 + (blk - n_a) // (lat_seq // block_rows))


def _ffn_kernel(xa_ref, xb_ref, mod_ref, w1_ref, w3_ref, w2_ref, g_ref, b_ref, oa_ref, ob_ref, *, n_a):
    is_a = pl.program_id(0) < n_a
    x = jnp.where(is_a, xa_ref[...], xb_ref[...])
    h = (_standardize(x) * (1.0 + mod_ref[4:5, :]) + mod_ref[3:4, :]).astype(BF16)
    acc = jnp.zeros(x.shape, F32)
    for c in range(D_FF // FFN_COLS):
        cols = slice(c * FFN_COLS, (c + 1) * FFN_COLS)
        a = _dot(h, w1_ref[:, cols].astype(BF16))
        b = _dot(h, w3_ref[:, cols].astype(BF16))
        acc = acc + _dot((_silu(a) * b).astype(BF16), w2_ref[cols, :].astype(BF16))
    y = DEEPNORM_ALPHA * x + mod_ref[5:6, :] * acc
    out = _standardize(y) * g_ref[...] + b_ref[...]

    @pl.when(is_a)
    def _():
        oa_ref[...] = out

    @pl.when(jnp.logical_not(is_a))
    def _():
        ob_ref[...] = out


def _ffn_dense(xa, xb, l, i, mods, w1, w3, w2, g, b, lat_seq):
    n_a, n_b = xa.shape[0] // FFN_TILE, xb.shape[0] // FFN_TILE
    x_specs = _two_stream_specs(FFN_TILE, n_a, lambda t: t)
    return pl.pallas_call(
        functools.partial(_ffn_kernel, n_a=n_a),
        grid=(n_a + n_b,),
        in_specs=x_specs + [
            pl.BlockSpec((None, None, SUBLANES, D_MODEL),
                         lambda t: (l, _group_of_block(t, FFN_TILE, n_a, lat_seq), 0, 0)),
            _layer_block(w1, i, resident=True), _layer_block(w3, i, resident=True),
            _layer_block(w2, i, resident=True), _layer_block(g, l), _layer_block(b, l)],
        out_specs=x_specs,
        out_shape=[jax.ShapeDtypeStruct(xa.shape, F32), jax.ShapeDtypeStruct(xb.shape, F32)],
        compiler_params=pltpu.CompilerParams(dimension_semantics=("arbitrary",), vmem_limit_bytes=VMEM_LIMIT),
        name="ffn_dense",
    )(xa, xb, mods, w1, w3, w2, g, b)


def _router_kernel(xa_ref, xb_ref, mod_ref, rw_ref, before_ref, dest_ref, gate_ref, start_ref, padded_ref, *,
                   n_a, lat_seq):
    c = pl.program_id(0)
    w_hi, w_lo = _split(rw_ref[...])
    n_blocks = MOE_CHUNK // ROUTER_BLOCK
    parts = []
    for blk in range(n_blocks):
        rows = pl.ds(blk * ROUTER_BLOCK, ROUTER_BLOCK)
        mod = mod_ref[_group_of_block(c * n_blocks + blk, ROUTER_BLOCK, n_a * n_blocks, lat_seq)]
        x = jnp.where(c < n_a, xa_ref[rows, :], xb_ref[rows, :])
        h = _standardize(x) * (1.0 + mod[4:5, :]) + mod[3:4, :]
        h_hi, h_lo = _split(h)
        parts.append(_dot_nt(w_hi, h_hi) + _dot_nt(w_hi, h_lo) + _dot_nt(w_lo, h_hi))
    logits = jnp.concatenate(parts, axis=1)
    eid = lax.broadcasted_iota(I32, logits.shape, 0).astype(F32)
    m1 = jnp.max(logits, axis=0, keepdims=True)
    i1 = jnp.min(jnp.where(logits == m1, eid, float(N_EXPERTS)), axis=0, keepdims=True)
    oh1 = eid == i1
    rest = jnp.where(oh1, -jnp.inf, logits)
    m2 = jnp.max(rest, axis=0, keepdims=True)
    i2 = jnp.min(jnp.where(rest == m2, eid, float(N_EXPERTS)), axis=0, keepdims=True)
    oh2 = eid == i2
    e = jnp.exp(m2 - m1)
    gate_ref[0:1, :] = 1.0 / (1.0 + e)
    gate_ref[1:2, :] = e / (1.0 + e)
    sel = jnp.where(oh1 | oh2, 1.0, 0.0)
    ranks = []
    seen = jnp.zeros((N_EXPERTS, 1), F32)
    for blk in range(n_blocks):
        s_blk = sel[:, blk * ROUTER_BLOCK:(blk + 1) * ROUTER_BLOCK]
        ranks.append(_dot(s_blk.astype(BF16), before_ref[...]) + seen)
        seen = seen + jnp.sum(s_blk, axis=1, keepdims=True)
    rank = jnp.concatenate(ranks, axis=1)
    eid_out = lax.broadcasted_iota(I32, start_ref.shape, 0).astype(F32)
    start = jnp.zeros(sel.shape, F32)
    start_out = jnp.zeros(start_ref.shape, F32)
    padded_out = jnp.zeros(start_ref.shape, F32)
    for ex in range(N_EXPERTS):
        cnt = jnp.sum(sel[ex:ex + 1, :], axis=1, keepdims=True)
        padded = jnp.ceil(cnt * (1.0 / MOE_GRAN)) * MOE_GRAN
        start = start + jnp.where(eid > ex, padded, 0.0)
        start_out = start_out + jnp.where(eid_out > ex, padded, 0.0)
        padded_out = padded_out + jnp.where(eid_out == ex, padded, 0.0)
    row = (start + rank) * SUBLANES
    dest_ref[0:1, :] = jnp.sum(jnp.where(oh1, row, 0.0), axis=0, keepdims=True).astype(I32)
    dest_ref[1:2, :] = jnp.sum(jnp.where(oh2, row, 0.0), axis=0, keepdims=True).astype(I32)
    start_ref[...] = start_out.astype(I32)
    padded_ref[...] = padded_out.astype(I32)


def _router(xa, xb, l, mods, rw_t, lat_seq):
    n_a = xa.shape[0] // MOE_CHUNK
    n_chunks = n_a + xb.shape[0] // MOE_CHUNK
    tok = np.arange(ROUTER_BLOCK)
    before = jnp.asarray(tok[:, None] < tok[None, :], BF16)
    return pl.pallas_call(
        functools.partial(_router_kernel, n_a=n_a, lat_seq=lat_seq),
        grid=(n_chunks,),
        in_specs=_two_stream_specs(MOE_CHUNK, n_a, lambda c: c) + [
                  _layer_block(mods, l),
                  _full((N_EXPERTS, D_MODEL)), _full((ROUTER_BLOCK, ROUTER_BLOCK))],
        out_specs=[pl.BlockSpec((None, 2, MOE_CHUNK), lambda c: (c, 0, 0)),
                   pl.BlockSpec((None, 2, MOE_CHUNK), lambda c: (c, 0, 0)),
                   pl.BlockSpec((None, N_EXPERTS, LANES), lambda c: (c, 0, 0)),
                   pl.BlockSpec((None, N_EXPERTS, LANES), lambda c: (c, 0, 0))],
        out_shape=[jax.ShapeDtypeStruct((n_chunks, 2, MOE_CHUNK), I32),
                   jax.ShapeDtypeStruct((n_chunks, 2, MOE_CHUNK), F32),
                   jax.ShapeDtypeStruct((n_chunks, N_EXPERTS, LANES), I32),
                   jax.ShapeDtypeStruct((n_chunks, N_EXPERTS, LANES), I32)],
        compiler_params=pltpu.CompilerParams(dimension_semantics=("arbitrary",), vmem_limit_bytes=VMEM_LIMIT),
        name="moe_router",
    )(xa, xb, mods, rw_t, before)


def _token_rows(first_row):
    return pl.ds(pl.multiple_of(first_row, SUBLANES), SUBLANES)


def _store_token_major(ref, tok0, val):
    for cc in range(D_MODEL // LANES):
        ref[pl.ds(tok0 * SUBLANES + cc, val.shape[0], stride=SUBLANES), :] = val[:, cc * LANES:(cc + 1) * LANES]


def _load_token_major(ref, tok0, n):
    return jnp.concatenate(
        [ref[pl.ds(tok0 * SUBLANES + cc, n, stride=SUBLANES), :] for cc in range(D_MODEL // LANES)], axis=1)


def _moe_kernel(start_ref, padded_ref, xa_ref, xb_ref, mod_ref, dest_ref, gate_ref, w1_ref, w3_ref, w2_ref,
                g_ref, b_ref, oa_ref, ob_ref, tok_scr, rows_scr, *, n_a):
    c = pl.program_id(0)
    j = pl.program_id(1)
    is_a = c < n_a
    first_expert_step = MOE_TOK_STEPS
    first_combine_step = MOE_TOK_STEPS + N_EXPERTS

    @pl.when((c == 0) & (j == 0))
    def _init():
        rows_scr[...] = jnp.zeros(rows_scr.shape, F32)

    @pl.when(j < first_expert_step)
    def _dispatch():
        for r in range(MOE_TOK_BLOCK // ROW_BLOCK):
            rows = pl.ds(r * ROW_BLOCK, ROW_BLOCK)
            x = jnp.where(is_a, xa_ref[rows, :], xb_ref[rows, :])
            h = _standardize(x) * (1.0 + mod_ref[4:5, :]) + mod_ref[3:4, :]
            _store_token_major(tok_scr, r * ROW_BLOCK, h)
        t0 = j * MOE_TOK_BLOCK

        def body(t, carry):
            row = tok_scr[_token_rows(t * SUBLANES), :]
            rows_scr[_token_rows(dest_ref[t0 + t]), :] = row
            rows_scr[_token_rows(dest_ref[MOE_CHUNK + t0 + t]), :] = row
            return carry
        lax.fori_loop(0, MOE_TOK_BLOCK, body, 0, unroll=8)

    def experts(row0, m):
        xin = _load_token_major(rows_scr, row0, m).astype(BF16)
        a = _dot(xin, w1_ref[...])
        b = _dot(xin, w3_ref[...])
        y = _dot((_silu(a) * b).astype(BF16), w2_ref[...])
        _store_token_major(rows_scr, row0, y)

    @pl.when((j >= first_expert_step) & (j < first_combine_step))
    def _experts():
        region = c * N_EXPERTS + (j - first_expert_step)
        start = start_ref[region]
        padded = padded_ref[region]
        n_full = lax.shift_right_logical(padded, MOE_BLOCK.bit_length() - 1)

        def body(i, carry):
            experts(start + i * MOE_BLOCK, MOE_BLOCK)
            return carry
        lax.fori_loop(0, n_full, body, 0)
        for m in range(MOE_GRAN, MOE_BLOCK, MOE_GRAN):
            pl.when(padded - n_full * MOE_BLOCK == m)(
                functools.partial(experts, start + n_full * MOE_BLOCK, m))

    @pl.when(j >= first_combine_step)
    def _combine():
        t0 = (j - first_combine_step) * MOE_TOK_BLOCK

        def body(t, carry):
            y0 = rows_scr[_token_rows(dest_ref[t0 + t]), :]
            y1 = rows_scr[_token_rows(dest_ref[MOE_CHUNK + t0 + t]), :]
            tok_scr[_token_rows(t * SUBLANES), :] = gate_ref[t0 + t] * y0 + gate_ref[MOE_CHUNK + t0 + t] * y1
            return carry
        lax.fori_loop(0, MOE_TOK_BLOCK, body, 0, unroll=8)
        for r in range(MOE_TOK_BLOCK // ROW_BLOCK):
            rows = pl.ds(r * ROW_BLOCK, ROW_BLOCK)
            ffn = _load_token_major(tok_scr, r * ROW_BLOCK, ROW_BLOCK)
            x = jnp.where(is_a, xa_ref[rows, :], xb_ref[rows, :])
            y = DEEPNORM_ALPHA * x + mod_ref[5:6, :] * ffn
            out = _standardize(y) * g_ref[...] + b_ref[...]

            @pl.when(is_a)
            def _():
                oa_ref[rows, :] = out

            @pl.when(jnp.logical_not(is_a))
            def _():
                ob_ref[rows, :] = out


def _ffn_moe(xa, xb, l, mods, rw_t, w1, w3, w2, g, b, lat_seq):
    n_a = xa.shape[0] // MOE_CHUNK
    n_chunks = n_a + xb.shape[0] // MOE_CHUNK
    n_a_blocks = n_a * MOE_TOK_STEPS
    dest, gates, start, padded = _router(xa, xb, l, mods, rw_t, lat_seq)
    first_expert_step = MOE_TOK_STEPS
    first_combine_step = MOE_TOK_STEPS + N_EXPERTS

    def token_block(c, j, *_):
        blk = jnp.where(j < first_combine_step, jnp.minimum(j, MOE_TOK_STEPS - 1), j - first_combine_step)
        return c * MOE_TOK_STEPS + blk

    def out_token_block(c, j, *_):
        return c * MOE_TOK_STEPS + jnp.maximum(j - first_combine_step, 0)

    def mod_of(c, j, *_):
        return (l, _group_of_block(token_block(c, j), MOE_TOK_BLOCK, n_a_blocks, lat_seq), 0, 0)

    def expert_of(c, j, *_):
        return (jnp.clip(j - first_expert_step, 0, N_EXPERTS - 1), 0, 0)

    grid_spec = pltpu.PrefetchScalarGridSpec(
        num_scalar_prefetch=2,
        grid=(n_chunks, MOE_STEPS),
        in_specs=_two_stream_specs(MOE_TOK_BLOCK, n_a_blocks, token_block) + [
                  pl.BlockSpec((None, None, SUBLANES, D_MODEL), mod_of),
                  pl.BlockSpec((2 * MOE_CHUNK,), lambda c, j, *_: (c,), memory_space=pltpu.SMEM),
                  pl.BlockSpec((2 * MOE_CHUNK,), lambda c, j, *_: (c,), memory_space=pltpu.SMEM),
                  pl.BlockSpec((None, D_MODEL, D_FF_EXPERT), expert_of),
                  pl.BlockSpec((None, D_MODEL, D_FF_EXPERT), expert_of),
                  pl.BlockSpec((None, D_FF_EXPERT, D_MODEL), expert_of),
                  _layer_block(g, l), _layer_block(b, l)],
        out_specs=_two_stream_specs(MOE_TOK_BLOCK, n_a_blocks, out_token_block),
        scratch_shapes=[pltpu.VMEM((MOE_TOK_BLOCK * SUBLANES, LANES), F32),
                        pltpu.VMEM((MOE_ROWS * SUBLANES, LANES), F32)],
    )
    return pl.pallas_call(
        functools.partial(_moe_kernel, n_a=n_a),
        grid_spec=grid_spec,
        out_shape=[jax.ShapeDtypeStruct(xa.shape, F32), jax.ShapeDtypeStruct(xb.shape, F32)],
        compiler_params=pltpu.CompilerParams(
            dimension_semantics=("arbitrary", "arbitrary"), vmem_limit_bytes=VMEM_LIMIT),
        name="moe_experts",
    )(start[:, :, 0].reshape(-1), padded[:, :, 0].reshape(-1), xa, xb, mods,
      dest.reshape(-1), gates.reshape(-1), w1, w3, w2, g, b)


def _rope_tables(n_tokens):
    t = np.arange(n_tokens)
    row = (t // GRID_W).astype(np.float32)
    col = (t % GRID_W).astype(np.float32)
    inv_freq = (np.float32(ROPE_THETA) ** (-np.arange(0, AXIS_ROT, 2, dtype=np.float32) / AXIS_ROT)).astype(np.float32)
    ang_r = row[:, None] * inv_freq
    ang_c = col[:, None] * inv_freq
    cos = np.concatenate([np.cos(ang_r), np.cos(ang_r), np.cos(ang_c), np.cos(ang_c)], axis=1)
    sin = np.concatenate([-np.sin(ang_r), np.sin(ang_r), -np.sin(ang_c), np.sin(ang_c)], axis=1)
    return jnp.asarray(np.tile(cos, (1, 2)), F32), jnp.asarray(np.tile(sin, (1, 2)), F32)


def kernel(x_prompt, x_sample, cache_k, cache_v, c, c_ctx, ada_w, ada_b, w_in, q_norm_g, k_norm_g, conv_w, conv_b, sgu_norm_g, sgu_w, sgu_b, w_out, ln1_g, ln1_b, ln2_g, ln2_b, ffn_w1, ffn_w3, ffn_w2, router_w, moe_w1, moe_w3, moe_w2):
    batch, seq, _ = x_prompt.shape
    dec_batch, dec_seq, _ = x_sample.shape
    past_len = cache_k.shape[2]
    n_ctx = batch * seq
    n_lat = dec_batch * dec_seq
    assert DEPTH == 2 and 1 + dec_batch <= SUBLANES
    assert seq == ATT_BLOCK and CTX_SEQS_PER_STEP * seq == ROW_BLOCK and batch % CTX_SEQS_PER_STEP == 0
    assert dec_seq % ROW_BLOCK == 0 and dec_seq & (dec_seq - 1) == 0
    assert n_ctx % MOE_CHUNK == 0 and n_lat % MOE_CHUNK == 0
    assert dec_seq % ROUTER_BLOCK == 0 and dec_seq % MOE_TOK_BLOCK == 0 and dec_seq % FFN_TILE == 0

    cond = jnp.zeros((SUBLANES, D_MODEL), F32).at[0].set(c_ctx).at[1:1 + dec_batch].set(c)
    mod = _modulation(cond, ada_w, ada_b)
    mod = mod.reshape(DEPTH, SUBLANES, 6, D_MODEL)[:, :1 + dec_batch]
    mod = jnp.pad(mod, ((0, 0), (0, 0), (0, SUBLANES - 6), (0, 0)))

    lane_id = np.arange(ATTN_WIDTH) // HEAD_DIM
    ones_bd = jnp.asarray(lane_id[:, None] == lane_id[None, :], BF16)
    cos, sin = _rope_tables(dec_seq)
    small = (
        jnp.tile(q_norm_g, (1, N_Q_HEADS))[:, None, :], jnp.tile(k_norm_g, (1, N_KV_HEADS))[:, None, :],
        ones_bd,
        conv_w, conv_b[:, None, :], sgu_norm_g[:, None, :], sgu_w,
        jnp.repeat(jnp.swapaxes(sgu_b, 1, 2), HEAD_DIM, axis=2),
        ln1_g[:, None, :], ln1_b[:, None, :],
    )
    g2, b2 = ln2_g[:, None, :], ln2_b[:, None, :]
    kc = cache_k.reshape(dec_batch, DEPTH, past_len, KV_WIDTH)
    vc = cache_v.reshape(dec_batch, DEPTH, past_len, KV_WIDTH)

    xs = [x_prompt.reshape(n_ctx, D_MODEL), x_sample.reshape(n_lat, D_MODEL)]
    kv = None
    for l in range(DEPTH):
        x_ctx, k_ctx, v_ctx, w_in_bf, w_out_bf = _mixer_ctx(xs[0], l, batch, seq, mod, w_in, w_out, small, kv)
        kv = (k_ctx, v_ctx)
        x_lat = _mixer_lat(xs[1], l, dec_batch, dec_seq, mod, w_in_bf, w_out_bf, small, cos, sin, kc, vc)
        i = l // 2
        if l % 2 == 0:
            xs = _ffn_dense(x_ctx, x_lat, l, i, mod, ffn_w1, ffn_w3, ffn_w2, g2, b2, dec_seq)
        else:
            ws = (moe_w1[i].astype(BF16), moe_w3[i].astype(BF16), moe_w2[i].astype(BF16))
            xs = _ffn_moe(x_ctx, x_lat, l, mod, router_w[i].T, *ws, g2, b2, dec_seq)
    y_p = xs[0].reshape(batch, seq, D_MODEL)
    y_s = xs[1].reshape(dec_batch, dec_seq, D_MODEL)
    new_k = kv[0].reshape(batch, DEPTH, seq, N_KV_HEADS, HEAD_DIM)
    new_v = kv[1].reshape(batch, DEPTH, seq, N_KV_HEADS, HEAD_DIM)
    return (y_p, y_s, new_k, new_v)
```

```python
import functools

import numpy as np
import jax
import jax.numpy as jnp
from jax import lax
from jax.experimental import pallas as pl
from jax.experimental.pallas import tpu as pltpu

F32 = jnp.float32
BF16 = jnp.bfloat16
I32 = jnp.int32

D_MODEL = 1024
DEPTH = 2
GRID_W = 64
HEAD_DIM = 64
N_Q_HEADS = 8
N_KV_HEADS = 2
ATTN_WIDTH = N_Q_HEADS * HEAD_DIM
KV_WIDTH = N_KV_HEADS * HEAD_DIM
ATTN_SCALE = HEAD_DIM ** -0.5
ROPE_THETA = 10000.0
AXIS_ROT = HEAD_DIM // 2
CONV_WIDTH = 256
SGU_WIDTH = 256
SGU_HEADS = 4
CHUNK = 128
IN_WIDTH = 2048
D_FF = 2816
N_EXPERTS = 8
D_FF_EXPERT = 1408
EPS = 1e-6
DEEPNORM_ALPHA = (2 * DEPTH) ** 0.25

LANES = 128
SUBLANES = 8
ROW_BLOCK = 512
GROUP_TILE = 256
ATT_BLOCK = 256
CTX_SEQS_PER_STEP = 2
FFN_TILE = 512
FFN_COLS = 256
MOE_CHUNK = 2048
ROUTER_BLOCK = 1024
MOE_GRAN = 128
MOE_BLOCK = 512
MOE_TOK_BLOCK = 512
MOE_TOK_STEPS = MOE_CHUNK // MOE_TOK_BLOCK
MOE_STEPS = 2 * MOE_TOK_STEPS + N_EXPERTS
MOE_ROWS = 2 * MOE_CHUNK + N_EXPERTS * MOE_GRAN
VMEM_LIMIT = 58 * 1024 * 1024

_Q0, _K0, _V0, _CI0, _CB0, _CC0, _SU0, _SV0 = 0, 512, 640, 768, 1024, 1280, 1536, 1792


def _dot(a, b):
    return jnp.dot(a, b, preferred_element_type=F32)


def _dot_nt(a, b):
    return lax.dot_general(a, b, (((1,), (1,)), ((), ())), preferred_element_type=F32)


def _split(x):
    hi = x.astype(BF16)
    lo = (x - hi.astype(F32)).astype(BF16)
    return hi, lo


def _group_sum(x, ones_bd):
    outs = []
    for c0 in range(0, x.shape[1], GROUP_TILE):
        width = min(GROUP_TILE, x.shape[1] - c0)
        hi, lo = _split(x[:, c0:c0 + width])
        ones = ones_bd[:width, :width]
        outs.append(_dot(hi, ones) + _dot(lo, ones))
    return outs[0] if len(outs) == 1 else jnp.concatenate(outs, axis=1)


def _standardize(x):
    mu = jnp.mean(x, axis=-1, keepdims=True)
    d = x - mu
    return d * lax.rsqrt(jnp.mean(d * d, axis=-1, keepdims=True) + EPS)


def _silu(x):
    return x / (1.0 + jnp.exp(-x))


def _modulation_kernel(cond_ref, w_ref, b_ref, o_ref):
    s_hi, s_lo = _split(_silu(cond_ref[...]))
    w_hi, w_lo = _split(w_ref[...])
    o_ref[...] = _dot(s_hi, w_hi) + _dot(s_hi, w_lo) + _dot(s_lo, w_hi) + b_ref[...]


def _modulation(cond, ada_w, ada_b):
    n_out = ada_w.shape[-1]
    tn = 1536
    return pl.pallas_call(
        _modulation_kernel,
        grid=(DEPTH, n_out // tn),
        in_specs=[
            pl.BlockSpec((SUBLANES, D_MODEL), lambda l, j: (0, 0)),
            pl.BlockSpec((None, D_MODEL, tn), lambda l, j: (l, 0, j)),
            pl.BlockSpec((None, 1, tn), lambda l, j: (l, 0, j)),
        ],
        out_specs=pl.BlockSpec((None, SUBLANES, tn), lambda l, j: (l, 0, j)),
        out_shape=jax.ShapeDtypeStruct((DEPTH, SUBLANES, n_out), F32),
        compiler_params=pltpu.CompilerParams(
            dimension_semantics=("arbitrary", "arbitrary"), vmem_limit_bytes=VMEM_LIMIT),
        name="modulation",
    )(cond, ada_w, ada_b.reshape(DEPTH, 1, n_out))


def _rope(x, cos, sin_signed):
    w = x.shape[1]
    lane = lax.broadcasted_iota(I32, x.shape, 1)
    first_half = (lane & 31) < 16
    partner = jnp.where(first_half, pltpu.roll(x, w - 16, 1), pltpu.roll(x, 16, 1))
    return x * cos + partner * sin_signed


def _head_variants(x):
    lane = lax.broadcasted_iota(I32, x.shape, 1)
    lo = lane < HEAD_DIM
    xr = pltpu.roll(x, HEAD_DIM, 1)
    zero = jnp.zeros_like(x)
    return (jnp.where(lo, x, zero).astype(BF16), jnp.where(lo, zero, xr).astype(BF16),
            jnp.where(lo, xr, zero).astype(BF16), jnp.where(lo, zero, x).astype(BF16))


def _mixer_kernel(*refs, seq, n_seq, n_cache, rope, cast_weights, stack_kv):
    refs = list(refs)

    def take(n):
        out, refs[:] = refs[:n], refs[n:]
        return out

    x_ref, mod_ref, win_ref, wout_ref = take(4)
    qg_ref, kg_ref, ones_ref, convw_ref, convb_ref, sgug_ref, sguw_ref, sgub_ref, ln1g_ref, ln1b_ref = take(10)
    if rope:
        cos_ref, sin_ref, kc_ref, vc_ref = take(4)
    if stack_kv:
        kprev_ref, vprev_ref = take(2)
    (x1_ref,) = take(1)
    if not rope:
        k_ref, v_ref = take(2)
    if cast_weights:
        winb_ref, woutb_ref = take(2)

        @pl.when(pl.program_id(0) == 0)
        def _cast():
            winb_ref[...] = win_ref[...].astype(BF16)
            woutb_ref[...] = wout_ref[...].astype(BF16)
        win_ref, wout_ref = winb_ref, woutb_ref
    q_scr, kvar_scr, vvar_scr, u_scr, cb_scr, su_scr, vn_scr, mix_scr = take(8)
    n_rows = n_seq * seq
    assert n_cache == 0 or n_seq == 1

    def loop(n, body):
        if n == 1:
            body(0)
        else:
            def step(r, carry):
                body(r)
                return carry
            lax.fori_loop(0, n, step, 0)

    def block(r, size):
        if isinstance(r, int):
            return pl.ds(r * size, size)
        return pl.ds(pl.multiple_of(r * size, size), size)

    if n_cache:
        for i, var in enumerate(_head_variants(kc_ref[...])):
            kvar_scr[i, pl.ds(seq, n_cache), :] = var
        for i, var in enumerate(_head_variants(vc_ref[...])):
            vvar_scr[i, pl.ds(seq, n_cache), :] = var

    def project(r):
        rows = block(r, ROW_BLOCK)
        x = x_ref[rows, :]
        h = _standardize(x) * (1.0 + mod_ref[1:2, :]) + mod_ref[0:1, :]
        z = _dot(h.astype(BF16), win_ref[...])
        ones_bd = ones_ref[...]
        zq = z[:, _Q0:_K0]
        q = zq * lax.rsqrt(_group_sum(zq * zq, ones_bd) * (1.0 / HEAD_DIM) + EPS) * qg_ref[...]
        zk = z[:, _K0:_V0]
        k = zk * lax.rsqrt(_group_sum(zk * zk, ones_bd) * (1.0 / HEAD_DIM) + EPS) * kg_ref[...]
        v = z[:, _V0:_CI0]
        if rope:
            cos = cos_ref[rows, :]
            sin = sin_ref[rows, :]
            q = _rope(q, jnp.concatenate([cos] * 4, axis=1), jnp.concatenate([sin] * 4, axis=1))
            k = _rope(k, cos, sin)
        elif stack_kv:
            for s in range(ROW_BLOCK // seq):
                sub = slice(s * seq, (s + 1) * seq)
                k_ref[s, 0] = kprev_ref[sub, :]
                v_ref[s, 0] = vprev_ref[sub, :]
                k_ref[s, 1] = k[sub, :]
                v_ref[s, 1] = v[sub, :]
        else:
            k_ref[rows, :] = k
            v_ref[rows, :] = v
        q_scr[rows, :] = (q * ATTN_SCALE).astype(BF16)
        for i, var in enumerate(_head_variants(k)):
            kvar_scr[i, rows, :] = var
        for i, var in enumerate(_head_variants(v)):
            vvar_scr[i, rows, :] = var
        u_scr[rows, :] = z[:, _CC0:_SU0] * z[:, _CI0:_CB0]
        cb_scr[rows, :] = z[:, _CB0:_CC0]
        su_scr[rows, :] = z[:, _SU0:_SV0]
        sv = z[:, _SV0:IN_WIDTH]
        d = sv - _group_sum(sv, ones_bd) * (1.0 / HEAD_DIM)
        vn = d * lax.rsqrt(_group_sum(d * d, ones_bd) * (1.0 / HEAD_DIM) + EPS) * sgug_ref[...]
        vn_scr[rows, :] = vn.astype(BF16)

    loop(n_rows // ROW_BLOCK, project)

    u = u_scr[...]
    pos = lax.broadcasted_iota(I32, u.shape, 0) & (seq - 1)
    up = jnp.where(pos == 0, 0.0, pltpu.roll(u, 1, 0))
    dn = jnp.where(pos == seq - 1, 0.0, pltpu.roll(u, n_rows - 1, 0))
    conv = up * convw_ref[0:1, :] + u * convw_ref[1:2, :] + dn * convw_ref[2:3, :] + convb_ref[...]
    mix_scr[:, ATTN_WIDTH:ATTN_WIDTH + CONV_WIDTH] = (cb_scr[...] * conv).astype(BF16)

    for n in range(n_rows // CHUNK):
        rows = pl.ds(n * CHUNK, CHUNK)
        vn = vn_scr[rows, :]
        lane = lax.broadcasted_iota(I32, vn.shape, 1)
        s = sgub_ref[...]
        for hd in range(SGU_HEADS):
            in_head = (lane >= hd * HEAD_DIM) & (lane < (hd + 1) * HEAD_DIM)
            masked = jnp.where(in_head, vn, jnp.zeros_like(vn))
            s = s + _dot(sguw_ref[hd].astype(BF16), masked)
        mix_scr[rows, ATTN_WIDTH + CONV_WIDTH:] = (su_scr[rows, :] * s).astype(BF16)

    def attend(s, r):
        rows = block(s * (seq // ATT_BLOCK) + r, ATT_BLOCK)
        keys = pl.ds(s * seq, seq + n_cache)
        for pair in range(N_Q_HEADS // 2):
            qp = q_scr[rows, pair * LANES:(pair + 1) * LANES]
            kv = pair // (N_Q_HEADS // N_KV_HEADS // 2)
            acc = jnp.zeros((ATT_BLOCK, LANES), F32)
            for parity in range(2):
                sc = _dot_nt(qp, kvar_scr[2 * kv + parity, keys, :])
                p = jnp.exp(sc - jnp.max(sc, axis=1, keepdims=True))
                denom = jnp.sum(p, axis=1, keepdims=True)
                acc = acc + _dot(p.astype(BF16), vvar_scr[2 * kv + parity, keys, :]) / denom
            mix_scr[rows, pair * LANES:(pair + 1) * LANES] = acc.astype(BF16)

    for s in range(n_seq):
        loop(seq // ATT_BLOCK, functools.partial(attend, s))

    def finish(r):
        rows = block(r, ROW_BLOCK)
        mix = _dot(mix_scr[rows, :], wout_ref[...])
        y = DEEPNORM_ALPHA * x_ref[rows, :] + mod_ref[2:3, :] * mix
        x1_ref[rows, :] = _standardize(y) * ln1g_ref[...] + ln1b_ref[...]

    loop(n_rows // ROW_BLOCK, finish)


def _full(shape):
    n = len(shape)
    return pl.BlockSpec(shape, lambda *_: (0,) * n)


def _resident(shape):
    n = len(shape)
    return pl.BlockSpec(shape, lambda *_: (0,) * n, pipeline_mode=pl.Buffered(1))


def _layer_block(arr, l, resident=False):
    shape = arr.shape[1:]
    kw = dict(pipeline_mode=pl.Buffered(1)) if resident else {}
    return pl.BlockSpec((None,) + shape, lambda *_: (l,) + (0,) * len(shape), **kw)


def _mixer_scratch(n_rows, n_cache):
    nk = n_rows + n_cache
    return [
        pltpu.VMEM((n_rows, ATTN_WIDTH), BF16),
        pltpu.VMEM((4, nk, LANES), BF16),
        pltpu.VMEM((4, nk, LANES), BF16),
        pltpu.VMEM((n_rows, CONV_WIDTH), F32),
        pltpu.VMEM((n_rows, CONV_WIDTH), F32),
        pltpu.VMEM((n_rows, SGU_WIDTH), F32),
        pltpu.VMEM((n_rows, SGU_WIDTH), BF16),
        pltpu.VMEM((n_rows, D_MODEL), BF16),
    ]


def _mixer_ctx(x, l, n_seq, seq, mod, w_in, w_out, small, kv_prev=None):
    per_step = CTX_SEQS_PER_STEP
    rows = per_step * seq
    stack_kv = kv_prev is not None
    kernel = functools.partial(_mixer_kernel, seq=seq, n_seq=per_step, n_cache=0, rope=False,
                               cast_weights=True, stack_kv=stack_kv)
    in_specs = ([pl.BlockSpec((rows, D_MODEL), lambda i: (i, 0)),
                 pl.BlockSpec((None, None, SUBLANES, D_MODEL), lambda i: (l, 0, 0, 0)),
                 _layer_block(w_in, l, resident=True), _layer_block(w_out, l, resident=True)]
                + [_full(a.shape) if a.ndim == 2 else _layer_block(a, l) for a in small])
    args = [x, mod, w_in, w_out, *small]
    if stack_kv:
        in_specs += [pl.BlockSpec((rows, KV_WIDTH), lambda i: (i, 0))] * 2
        args += list(kv_prev)
        kv_spec = pl.BlockSpec((per_step, DEPTH, seq, KV_WIDTH), lambda i: (i, 0, 0, 0))
        kv_shape = jax.ShapeDtypeStruct((n_seq, DEPTH, seq, KV_WIDTH), F32)
    else:
        kv_spec = pl.BlockSpec((rows, KV_WIDTH), lambda i: (i, 0))
        kv_shape = jax.ShapeDtypeStruct((n_seq * seq, KV_WIDTH), F32)
    return pl.pallas_call(
        kernel,
        grid=(n_seq // per_step,),
        in_specs=in_specs,
        out_specs=[pl.BlockSpec((rows, D_MODEL), lambda i: (i, 0)), kv_spec, kv_spec,
                   _full((D_MODEL, IN_WIDTH)), _full((D_MODEL, D_MODEL))],
        out_shape=[jax.ShapeDtypeStruct((n_seq * seq, D_MODEL), F32), kv_shape, kv_shape,
                   jax.ShapeDtypeStruct((D_MODEL, IN_WIDTH), BF16), jax.ShapeDtypeStruct((D_MODEL, D_MODEL), BF16)],
        scratch_shapes=_mixer_scratch(rows, 0),
        compiler_params=pltpu.CompilerParams(dimension_semantics=("arbitrary",), vmem_limit_bytes=VMEM_LIMIT),
        name="mixer_ctx",
    )(*args)


def _mixer_lat(x, l, n_seq, seq, mod, w_in_bf, w_out_bf, small, cos, sin, kc, vc):
    n_cache = kc.shape[2]
    kernel = functools.partial(_mixer_kernel, seq=seq, n_seq=1, n_cache=n_cache, rope=True,
                               cast_weights=False, stack_kv=False)
    cache_spec = pl.BlockSpec((None, None, n_cache, KV_WIDTH), lambda b: (b, l, 0, 0))
    return pl.pallas_call(
        kernel,
        grid=(n_seq,),
        in_specs=([pl.BlockSpec((seq, D_MODEL), lambda b: (b, 0)),
                   pl.BlockSpec((None, None, SUBLANES, D_MODEL), lambda b: (l, 1 + b, 0, 0)),
                   _resident((D_MODEL, IN_WIDTH)), _resident((D_MODEL, D_MODEL))]
                  + [_full(a.shape) if a.ndim == 2 else _layer_block(a, l) for a in small]
                  + [_full((seq, LANES)), _full((seq, LANES)), cache_spec, cache_spec]),
        out_specs=pl.BlockSpec((seq, D_MODEL), lambda b: (b, 0)),
        out_shape=jax.ShapeDtypeStruct((n_seq * seq, D_MODEL), F32),
        scratch_shapes=_mixer_scratch(seq, n_cache),
        compiler_params=pltpu.CompilerParams(dimension_semantics=("arbitrary",), vmem_limit_bytes=VMEM_LIMIT),
        name="mixer_lat",
    )(x, mod, w_in_bf, w_out_bf, *small, cos, sin, kc, vc)


def _two_stream_specs(block_rows, n_a, block_of, single_buffer_b=False):
    kw = dict(pipeline_mode=pl.Buffered(1)) if single_buffer_b else {}
    spec_a = pl.BlockSpec((block_rows, D_MODEL), lambda *ids: (jnp.minimum(block_of(*ids), n_a - 1), 0))
    spec_b = pl.BlockSpec((block_rows, D_MODEL), lambda *ids: (jnp.maximum(block_of(*ids) - n_a, 0), 0), **kw)
    return [spec_a, spec_b]


def _group_of_block(blk, block_rows, n_a, lat_seq):
    return jnp.where(blk < n_a, 0, 1 + (blk - n_a) // (lat_seq // block_rows))


def _ffn_kernel(xa_ref, xb_ref, mod_ref, w1_ref, w3_ref, w2_ref, g_ref, b_ref, oa_ref, ob_ref, *, n_a):
    is_a = pl.program_id(0) < n_a
    x = jnp.where(is_a, xa_ref[...], xb_ref[...])
    h = (_standardize(x) * (1.0 + mod_ref[4:5, :]) + mod_ref[3:4, :]).astype(BF16)
    acc = jnp.zeros(x.shape, F32)
    for c in range(D_FF // FFN_COLS):
        cols = slice(c * FFN_COLS, (c + 1) * FFN_COLS)
        a = _dot(h, w1_ref[:, cols].astype(BF16))
        b = _dot(h, w3_ref[:, cols].astype(BF16))
        acc = acc + _dot((_silu(a) * b).astype(BF16), w2_ref[cols, :].astype(BF16))
    y = DEEPNORM_ALPHA * x + mod_ref[5:6, :] * acc
    out = _standardize(y) * g_ref[...] + b_ref[...]

    @pl.when(is_a)
    def _():
        oa_ref[...] = out

    @pl.when(jnp.logical_not(is_a))
    def _():
        ob_ref[...] = out


def _ffn_dense(xa, xb, l, i, mods, w1, w3, w2, g, b, lat_seq):
    n_a, n_b = xa.shape[0] // FFN_TILE, xb.shape[0] // FFN_TILE
    x_specs = _two_stream_specs(FFN_TILE, n_a, lambda t: t)
    return pl.pallas_call(
        functools.partial(_ffn_kernel, n_a=n_a),
        grid=(n_a + n_b,),
        in_specs=x_specs + [
            pl.BlockSpec((None, None, SUBLANES, D_MODEL),
                         lambda t: (l, _group_of_block(t, FFN_TILE, n_a, lat_seq), 0, 0)),
            _layer_block(w1, i, resident=True), _layer_block(w3, i, resident=True),
            _layer_block(w2, i, resident=True), _layer_block(g, l), _layer_block(b, l)],
        out_specs=x_specs,
        out_shape=[jax.ShapeDtypeStruct(xa.shape, F32), jax.ShapeDtypeStruct(xb.shape, F32)],
        compiler_params=pltpu.CompilerParams(dimension_semantics=("arbitrary",), vmem_limit_bytes=VMEM_LIMIT),
        name="ffn_dense",
    )(xa, xb, mods, w1, w3, w2, g, b)


def _router_kernel(xa_ref, xb_ref, mod_ref, rw_ref, before_ref, dest_ref, gate_ref, start_ref, padded_ref, *,
                   n_a, lat_seq):
    c = pl.program_id(0)
    w_hi, w_lo = _split(rw_ref[...])
    n_blocks = MOE_CHUNK // ROUTER_BLOCK
    parts = []
    for blk in range(n_blocks):
        rows = pl.ds(blk * ROUTER_BLOCK, ROUTER_BLOCK)
        mod = mod_ref[_group_of_block(c * n_blocks + blk, ROUTER_BLOCK, n_a * n_blocks, lat_seq)]
        x = jnp.where(c < n_a, xa_ref[rows, :], xb_ref[rows, :])
        h = _standardize(x) * (1.0 + mod[4:5, :]) + mod[3:4, :]
        h_hi, h_lo = _split(h)
        parts.append(_dot_nt(w_hi, h_hi) + _dot_nt(w_hi, h_lo) + _dot_nt(w_lo, h_hi))
    logits = jnp.concatenate(parts, axis=1)
    eid = lax.broadcasted_iota(I32, logits.shape, 0).astype(F32)
    m1 = jnp.max(logits, axis=0, keepdims=True)
    i1 = jnp.min(jnp.where(logits == m1, eid, float(N_EXPERTS)), axis=0, keepdims=True)
    oh1 = eid == i1
    rest = jnp.where(oh1, -jnp.inf, logits)
    m2 = jnp.max(rest, axis=0, keepdims=True)
    i2 = jnp.min(jnp.where(rest == m2, eid, float(N_EXPERTS)), axis=0, keepdims=True)
    oh2 = eid == i2
    e = jnp.exp(m2 - m1)
    gate_ref[0:1, :] = 1.0 / (1.0 + e)
    gate_ref[1:2, :] = e / (1.0 + e)
    sel = jnp.where(oh1 | oh2, 1.0, 0.0)
    ranks = []
    seen = jnp.zeros((N_EXPERTS, 1), F32)
    for blk in range(n_blocks):
        s_blk = sel[:, blk * ROUTER_BLOCK:(blk + 1) * ROUTER_BLOCK]
        ranks.append(_dot(s_blk.astype(BF16), before_ref[...]) + seen)
        seen = seen + jnp.sum(s_blk, axis=1, keepdims=True)
    rank = jnp.concatenate(ranks, axis=1)
    eid_out = lax.broadcasted_iota(I32, start_ref.shape, 0).astype(F32)
    start = jnp.zeros(sel.shape, F32)
    start_out = jnp.zeros(start_ref.shape, F32)
    padded_out = jnp.zeros(start_ref.shape, F32)
    for ex in range(N_EXPERTS):
        cnt = jnp.sum(sel[ex:ex + 1, :], axis=1, keepdims=True)
        padded = jnp.ceil(cnt * (1.0 / MOE_GRAN)) * MOE_GRAN
        start = start + jnp.where(eid > ex, padded, 0.0)
        start_out = start_out + jnp.where(eid_out > ex, padded, 0.0)
        padded_out = padded_out + jnp.where(eid_out == ex, padded, 0.0)
    row = (start + rank) * SUBLANES
    dest_ref[0:1, :] = jnp.sum(jnp.where(oh1, row, 0.0), axis=0, keepdims=True).astype(I32)
    dest_ref[1:2, :] = jnp.sum(jnp.where(oh2, row, 0.0), axis=0, keepdims=True).astype(I32)
    start_ref[...] = start_out.astype(I32)
    padded_ref[...] = padded_out.astype(I32)


def _router(xa, xb, l, mods, rw_t, lat_seq):
    n_a = xa.shape[0] // MOE_CHUNK
    n_chunks = n_a + xb.shape[0] // MOE_CHUNK
    tok = np.arange(ROUTER_BLOCK)
    before = jnp.asarray(tok[:, None] < tok[None, :], BF16)
    return pl.pallas_call(
        functools.partial(_router_kernel, n_a=n_a, lat_seq=lat_seq),
        grid=(n_chunks,),
        in_specs=_two_stream_specs(MOE_CHUNK, n_a, lambda c: c) + [
                  _layer_block(mods, l),
                  _full((N_EXPERTS, D_MODEL)), _full((ROUTER_BLOCK, ROUTER_BLOCK))],
        out_specs=[pl.BlockSpec((None, 2, MOE_CHUNK), lambda c: (c, 0, 0)),
                   pl.BlockSpec((None, 2, MOE_CHUNK), lambda c: (c, 0, 0)),
                   pl.BlockSpec((None, N_EXPERTS, LANES), lambda c: (c, 0, 0)),
                   pl.BlockSpec((None, N_EXPERTS, LANES), lambda c: (c, 0, 0))],
        out_shape=[jax.ShapeDtypeStruct((n_chunks, 2, MOE_CHUNK), I32),
                   jax.ShapeDtypeStruct((n_chunks, 2, MOE_CHUNK), F32),
                   jax.ShapeDtypeStruct((n_chunks, N_EXPERTS, LANES), I32),
                   jax.ShapeDtypeStruct((n_chunks, N_EXPERTS, LANES), I32)],
        compiler_params=pltpu.CompilerParams(dimension_semantics=("arbitrary",), vmem_limit_bytes=VMEM_LIMIT),
        name="moe_router",
    )(xa, xb, mods, rw_t, before)


def _token_rows(first_row):
    return pl.ds(pl.multiple_of(first_row, SUBLANES), SUBLANES)


def _store_token_major(ref, tok0, val):
    for cc in range(D_MODEL // LANES):
        ref[pl.ds(tok0 * SUBLANES + cc, val.shape[0], stride=SUBLANES), :] = val[:, cc * LANES:(cc + 1) * LANES]


def _load_token_major(ref, tok0, n):
    return jnp.concatenate(
        [ref[pl.ds(tok0 * SUBLANES + cc, n, stride=SUBLANES), :] for cc in range(D_MODEL // LANES)], axis=1)


def _moe_kernel(start_ref, padded_ref, xa_ref, xb_ref, mod_ref, dest_ref, gate_ref, w1_ref, w3_ref, w2_ref,
                g_ref, b_ref, oa_ref, ob_ref, tok_scr, rows_scr, *, n_a):
    c = pl.program_id(0)
    j = pl.program_id(1)
    is_a = c < n_a
    first_expert_step = MOE_TOK_STEPS
    first_combine_step = MOE_TOK_STEPS + N_EXPERTS

    @pl.when((c == 0) & (j == 0))
    def _init():
        rows_scr[...] = jnp.zeros(rows_scr.shape, F32)

    @pl.when(j < first_expert_step)
    def _dispatch():
        for r in range(MOE_TOK_BLOCK // ROW_BLOCK):
            rows = pl.ds(r * ROW_BLOCK, ROW_BLOCK)
            x = jnp.where(is_a, xa_ref[rows, :], xb_ref[rows, :])
            h = _standardize(x) * (1.0 + mod_ref[4:5, :]) + mod_ref[3:4, :]
            _store_token_major(tok_scr, r * ROW_BLOCK, h)
        t0 = j * MOE_TOK_BLOCK

        def body(t, carry):
            row = tok_scr[_token_rows(t * SUBLANES), :]
            rows_scr[_token_rows(dest_ref[t0 + t]), :] = row
            rows_scr[_token_rows(dest_ref[MOE_CHUNK + t0 + t]), :] = row
            return carry
        lax.fori_loop(0, MOE_TOK_BLOCK, body, 0, unroll=8)

    def experts(row0, m):
        xin = _load_token_major(rows_scr, row0, m).astype(BF16)
        a = _dot(xin, w1_ref[...])
        b = _dot(xin, w3_ref[...])
        y = _dot((_silu(a) * b).astype(BF16), w2_ref[...])
        _store_token_major(rows_scr, row0, y)

    @pl.when((j >= first_expert_step) & (j < first_combine_step))
    def _experts():
        region = c * N_EXPERTS + (j - first_expert_step)
        start = start_ref[region]
        padded = padded_ref[region]
        n_full = lax.shift_right_logical(padded, MOE_BLOCK.bit_length() - 1)

        def body(i, carry):
            experts(start + i * MOE_BLOCK, MOE_BLOCK)
            return carry
        lax.fori_loop(0, n_full, body, 0)
        for m in range(MOE_GRAN, MOE_BLOCK, MOE_GRAN):
            pl.when(padded - n_full * MOE_BLOCK == m)(
                functools.partial(experts, start + n_full * MOE_BLOCK, m))

    @pl.when(j >= first_combine_step)
    def _combine():
        t0 = (j - first_combine_step) * MOE_TOK_BLOCK

        def body(t, carry):
            y0 = rows_scr[_token_rows(dest_ref[t0 + t]), :]
            y1 = rows_scr[_token_rows(dest_ref[MOE_CHUNK + t0 + t]), :]
            tok_scr[_token_rows(t * SUBLANES), :] = gate_ref[t0 + t] * y0 + gate_ref[MOE_CHUNK + t0 + t] * y1
            return carry
        lax.fori_loop(0, MOE_TOK_BLOCK, body, 0, unroll=8)
        for r in range(MOE_TOK_BLOCK // ROW_BLOCK):
            rows = pl.ds(r * ROW_BLOCK, ROW_BLOCK)
            ffn = _load_token_major(tok_scr, r * ROW_BLOCK, ROW_BLOCK)
            x = jnp.where(is_a, xa_ref[rows, :], xb_ref[rows, :])
            y = DEEPNORM_ALPHA * x + mod_ref[5:6, :] * ffn
            out = _standardize(y) * g_ref[...] + b_ref[...]

            @pl.when(is_a)
            def _():
                oa_ref[rows, :] = out

            @pl.when(jnp.logical_not(is_a))
            def _():
                ob_ref[rows, :] = out


def _ffn_moe(xa, xb, l, mods, rw_t, w1, w3, w2, g, b, lat_seq):
    n_a = xa.shape[0] // MOE_CHUNK
    n_chunks = n_a + xb.shape[0] // MOE_CHUNK
    n_a_blocks = n_a * MOE_TOK_STEPS
    dest, gates, start, padded = _router(xa, xb, l, mods, rw_t, lat_seq)
    first_expert_step = MOE_TOK_STEPS
    first_combine_step = MOE_TOK_STEPS + N_EXPERTS

    def token_block(c, j, *_):
        blk = jnp.where(j < first_combine_step, jnp.minimum(j, MOE_TOK_STEPS - 1), j - first_combine_step)
        return c * MOE_TOK_STEPS + blk

    def out_token_block(c, j, *_):
        return c * MOE_TOK_STEPS + jnp.maximum(j - first_combine_step, 0)

    def mod_of(c, j, *_):
        return (l, _group_of_block(token_block(c, j), MOE_TOK_BLOCK, n_a_blocks, lat_seq), 0, 0)

    def expert_of(c, j, *_):
        return (jnp.clip(j - first_expert_step, 0, N_EXPERTS - 1), 0, 0)

    grid_spec = pltpu.PrefetchScalarGridSpec(
        num_scalar_prefetch=2,
        grid=(n_chunks, MOE_STEPS),
        in_specs=_two_stream_specs(MOE_TOK_BLOCK, n_a_blocks, token_block, single_buffer_b=True) + [
                  pl.BlockSpec((None, None, SUBLANES, D_MODEL), mod_of),
                  pl.BlockSpec((2 * MOE_CHUNK,), lambda c, j, *_: (c,), memory_space=pltpu.SMEM),
                  pl.BlockSpec((2 * MOE_CHUNK,), lambda c, j, *_: (c,), memory_space=pltpu.SMEM),
                  pl.BlockSpec((None, D_MODEL, D_FF_EXPERT), expert_of),
                  pl.BlockSpec((None, D_MODEL, D_FF_EXPERT), expert_of),
                  pl.BlockSpec((None, D_FF_EXPERT, D_MODEL), expert_of),
                  _layer_block(g, l), _layer_block(b, l)],
        out_specs=_two_stream_specs(MOE_TOK_BLOCK, n_a_blocks, out_token_block),
        scratch_shapes=[pltpu.VMEM((MOE_TOK_BLOCK * SUBLANES, LANES), F32),
                        pltpu.VMEM((MOE_ROWS * SUBLANES, LANES), F32)],
    )
    return pl.pallas_call(
        functools.partial(_moe_kernel, n_a=n_a),
        grid_spec=grid_spec,
        out_shape=[jax.ShapeDtypeStruct(xa.shape, F32), jax.ShapeDtypeStruct(xb.shape, F32)],
        compiler_params=pltpu.CompilerParams(
            dimension_semantics=("arbitrary", "arbitrary"), vmem_limit_bytes=VMEM_LIMIT),
        name="moe_experts",
    )(start[:, :, 0].reshape(-1), padded[:, :, 0].reshape(-1), xa, xb, mods,
      dest.reshape(-1), gates.reshape(-1), w1, w3, w2, g, b)


def _rope_tables(n_tokens):
    t = np.arange(n_tokens)
    row = (t // GRID_W).astype(np.float32)
    col = (t % GRID_W).astype(np.float32)
    inv_freq = (np.float32(ROPE_THETA) ** (-np.arange(0, AXIS_ROT, 2, dtype=np.float32) / AXIS_ROT)).astype(np.float32)
    ang_r = row[:, None] * inv_freq
    ang_c = col[:, None] * inv_freq
    cos = np.concatenate([np.cos(ang_r), np.cos(ang_r), np.cos(ang_c), np.cos(ang_c)], axis=1)
    sin = np.concatenate([-np.sin(ang_r), np.sin(ang_r), -np.sin(ang_c), np.sin(ang_c)], axis=1)
    return jnp.asarray(np.tile(cos, (1, 2)), F32), jnp.asarray(np.tile(sin, (1, 2)), F32)


def kernel(x_prompt, x_sample, cache_k, cache_v, c, c_ctx, ada_w, ada_b, w_in, q_norm_g, k_norm_g, conv_w, conv_b, sgu_norm_g, sgu_w, sgu_b, w_out, ln1_g, ln1_b, ln2_g, ln2_b, ffn_w1, ffn_w3, ffn_w2, router_w, moe_w1, moe_w3, moe_w2):
    batch, seq, _ = x_prompt.shape
    dec_batch, dec_seq, _ = x_sample.shape
    past_len = cache_k.shape[2]
    n_ctx = batch * seq
    n_lat = dec_batch * dec_seq
    assert DEPTH == 2 and 1 + dec_batch <= SUBLANES
    assert seq == ATT_BLOCK and CTX_SEQS_PER_STEP * seq == ROW_BLOCK and batch % CTX_SEQS_PER_STEP == 0
    assert dec_seq % ROW_BLOCK == 0 and dec_seq & (dec_seq - 1) == 0
    assert n_ctx % MOE_CHUNK == 0 and n_lat % MOE_CHUNK == 0
    assert dec_seq % ROUTER_BLOCK == 0 and dec_seq % MOE_TOK_BLOCK == 0 and dec_seq % FFN_TILE == 0

    cond = jnp.zeros((SUBLANES, D_MODEL), F32).at[0].set(c_ctx).at[1:1 + dec_batch].set(c)
    mod = _modulation(cond, ada_w, ada_b)
    mod = mod.reshape(DEPTH, SUBLANES, 6, D_MODEL)[:, :1 + dec_batch]
    mod = jnp.pad(mod, ((0, 0), (0, 0), (0, SUBLANES - 6), (0, 0)))

    lane_id = np.arange(GROUP_TILE) // HEAD_DIM
    ones_bd = jnp.asarray(lane_id[:, None] == lane_id[None, :], BF16)
    cos, sin = _rope_tables(dec_seq)
    small = (
        jnp.tile(q_norm_g, (1, N_Q_HEADS))[:, None, :], jnp.tile(k_norm_g, (1, N_KV_HEADS))[:, None, :],
        ones_bd,
        conv_w, conv_b[:, None, :], sgu_norm_g[:, None, :], sgu_w,
        jnp.repeat(jnp.swapaxes(sgu_b, 1, 2), HEAD_DIM, axis=2),
        ln1_g[:, None, :], ln1_b[:, None, :],
    )
    g2, b2 = ln2_g[:, None, :], ln2_b[:, None, :]
    kc = cache_k.reshape(dec_batch, DEPTH, past_len, KV_WIDTH)
    vc = cache_v.reshape(dec_batch, DEPTH, past_len, KV_WIDTH)

    xs = [x_prompt.reshape(n_ctx, D_MODEL), x_sample.reshape(n_lat, D_MODEL)]
    kv = None
    for l in range(DEPTH):
        x_ctx, k_ctx, v_ctx, w_in_bf, w_out_bf = _mixer_ctx(xs[0], l, batch, seq, mod, w_in, w_out, small, kv)
        kv = (k_ctx, v_ctx)
        x_lat = _mixer_lat(xs[1], l, dec_batch, dec_seq, mod, w_in_bf, w_out_bf, small, cos, sin, kc, vc)
        i = l // 2
        if l % 2 == 0:
            xs = _ffn_dense(x_ctx, x_lat, l, i, mod, ffn_w1, ffn_w3, ffn_w2, g2, b2, dec_seq)
        else:
            ws = (moe_w1[i].astype(BF16), moe_w3[i].astype(BF16), moe_w2[i].astype(BF16))
            xs = _ffn_moe(x_ctx, x_lat, l, mod, router_w[i].T, *ws, g2, b2, dec_seq)
    y_p = xs[0].reshape(batch, seq, D_MODEL)
    y_s = xs[1].reshape(dec_batch, dec_seq, D_MODEL)
    new_k = kv[0].reshape(batch, DEPTH, seq, N_KV_HEADS, HEAD_DIM)
    new_v = kv[1].reshape(batch, DEPTH, seq, N_KV_HEADS, HEAD_DIM)
    return (y_p, y_s, new_k, new_v)
```

```python
import functools

import numpy as np
import jax
import jax.numpy as jnp
from jax import lax
from jax.experimental import pallas as pl
from jax.experimental.pallas import tpu as pltpu

F32 = jnp.float32
BF16 = jnp.bfloat16
I32 = jnp.int32

D_MODEL = 1024
DEPTH = 2
GRID_W = 64
HEAD_DIM = 64
N_Q_HEADS = 8
N_KV_HEADS = 2
ATTN_WIDTH = N_Q_HEADS * HEAD_DIM
KV_WIDTH = N_KV_HEADS * HEAD_DIM
ATTN_SCALE = HEAD_DIM ** -0.5
ROPE_THETA = 10000.0
AXIS_ROT = HEAD_DIM // 2
CONV_WIDTH = 256
SGU_WIDTH = 256
SGU_HEADS = 4
CHUNK = 128
IN_WIDTH = 2048
D_FF = 2816
N_EXPERTS = 8
D_FF_EXPERT = 1408
EPS = 1e-6
DEEPNORM_ALPHA = (2 * DEPTH) ** 0.25

LANES = 128
SUBLANES = 8
ROW_BLOCK = 512
GROUP_TILE = 256
ATT_BLOCK = 256
CTX_SEQS_PER_STEP = 2
FFN_TILE = 512
FFN_COLS = 256
MOE_CHUNK = 2048
ROUTER_BLOCK = 1024
MOE_GRAN = 128
MOE_BLOCK = 256
MOE_TOK_BLOCK = 512
MOE_TOK_STEPS = MOE_CHUNK // MOE_TOK_BLOCK
MOE_STEPS = 2 * MOE_TOK_STEPS + N_EXPERTS
MOE_ROWS = 2 * MOE_CHUNK + N_EXPERTS * MOE_GRAN
VMEM_LIMIT = 58 * 1024 * 1024

_Q0, _K0, _V0, _CI0, _CB0, _CC0, _SU0, _SV0 = 0, 512, 640, 768, 1024, 1280, 1536, 1792


def _dot(a, b):
    return jnp.dot(a, b, preferred_element_type=F32)


def _dot_nt(a, b):
    return lax.dot_general(a, b, (((1,), (1,)), ((), ())), preferred_element_type=F32)


def _split(x):
    hi = x.astype(BF16)
    lo = (x - hi.astype(F32)).astype(BF16)
    return hi, lo


def _group_sum(x, ones_bd):
    outs = []
    for c0 in range(0, x.shape[1], GROUP_TILE):
        width = min(GROUP_TILE, x.shape[1] - c0)
        hi, lo = _split(x[:, c0:c0 + width])
        ones = ones_bd[:width, :width]
        outs.append(_dot(hi, ones) + _dot(lo, ones))
    return outs[0] if len(outs) == 1 else jnp.concatenate(outs, axis=1)


def _standardize(x):
    mu = jnp.mean(x, axis=-1, keepdims=True)
    d = x - mu
    return d * lax.rsqrt(jnp.mean(d * d, axis=-1, keepdims=True) + EPS)


def _silu(x):
    return x / (1.0 + jnp.exp(-x))


def _modulation_kernel(cond_ref, w_ref, b_ref, o_ref):
    s_hi, s_lo = _split(_silu(cond_ref[...]))
    w_hi, w_lo = _split(w_ref[...])
    o_ref[...] = _dot(s_hi, w_hi) + _dot(s_hi, w_lo) + _dot(s_lo, w_hi) + b_ref[...]


def _modulation(cond, ada_w, ada_b):
    n_out = ada_w.shape[-1]
    tn = 1536
    return pl.pallas_call(
        _modulation_kernel,
        grid=(DEPTH, n_out // tn),
        in_specs=[
            pl.BlockSpec((SUBLANES, D_MODEL), lambda l, j: (0, 0)),
            pl.BlockSpec((None, D_MODEL, tn), lambda l, j: (l, 0, j)),
            pl.BlockSpec((None, 1, tn), lambda l, j: (l, 0, j)),
        ],
        out_specs=pl.BlockSpec((None, SUBLANES, tn), lambda l, j: (l, 0, j)),
        out_shape=jax.ShapeDtypeStruct((DEPTH, SUBLANES, n_out), F32),
        compiler_params=pltpu.CompilerParams(
            dimension_semantics=("arbitrary", "arbitrary"), vmem_limit_bytes=VMEM_LIMIT),
        name="modulation",
    )(cond, ada_w, ada_b.reshape(DEPTH, 1, n_out))


def _rope(x, cos, sin_signed):
    w = x.shape[1]
    lane = lax.broadcasted_iota(I32, x.shape, 1)
    first_half = (lane & 31) < 16
    partner = jnp.where(first_half, pltpu.roll(x, w - 16, 1), pltpu.roll(x, 16, 1))
    return x * cos + partner * sin_signed


def _head_variants(x):
    lane = lax.broadcasted_iota(I32, x.shape, 1)
    lo = lane < HEAD_DIM
    xr = pltpu.roll(x, HEAD_DIM, 1)
    zero = jnp.zeros_like(x)
    return (jnp.where(lo, x, zero).astype(BF16), jnp.where(lo, zero, xr).astype(BF16),
            jnp.where(lo, xr, zero).astype(BF16), jnp.where(lo, zero, x).astype(BF16))


def _mixer_kernel(*refs, seq, n_seq, n_cache, rope, cast_weights, stack_kv, side_jobs=()):
    refs = list(refs)

    def take(n):
        out, refs[:] = refs[:n], refs[n:]
        return out

    x_ref, mod_ref, win_ref, wout_ref = take(4)
    qg_ref, kg_ref, ones_ref, convw_ref, convb_ref, sgug_ref, sguw_ref, sgub_ref, ln1g_ref, ln1b_ref = take(10)
    if rope:
        cos_ref, sin_ref, kc_ref, vc_ref = take(4)
    if stack_kv:
        kprev_ref, vprev_ref = take(2)
    side_in = [take(2 if job == "merge" else 1) for job in side_jobs]
    (x1_ref,) = take(1)
    if not rope:
        k_ref, v_ref = take(2)
    if cast_weights:
        winb_ref, woutb_ref = take(2)

        @pl.when(pl.program_id(0) == 0)
        def _cast():
            winb_ref[...] = win_ref[...].astype(BF16)
            woutb_ref[...] = wout_ref[...].astype(BF16)
        win_ref, wout_ref = winb_ref, woutb_ref
    for job, srcs, dst in zip(side_jobs, side_in, take(len(side_jobs))):
        if job == "merge":
            half = srcs[1].shape[0]
            dst[:half, :] = srcs[1][...]
            dst[half:, :] = srcs[0][...].astype(BF16)
        else:
            dst[...] = srcs[0][...].astype(BF16)
    q_scr, kvar_scr, vvar_scr, u_scr, cb_scr, su_scr, vn_scr, mix_scr = take(8)
    n_rows = n_seq * seq
    assert n_cache == 0 or n_seq == 1

    def loop(n, body):
        if n == 1:
            body(0)
        else:
            def step(r, carry):
                body(r)
                return carry
            lax.fori_loop(0, n, step, 0)

    def block(r, size):
        if isinstance(r, int):
            return pl.ds(r * size, size)
        return pl.ds(pl.multiple_of(r * size, size), size)

    if n_cache:
        for i, var in enumerate(_head_variants(kc_ref[...])):
            kvar_scr[i, pl.ds(seq, n_cache), :] = var
        for i, var in enumerate(_head_variants(vc_ref[...])):
            vvar_scr[i, pl.ds(seq, n_cache), :] = var

    def project(r):
        rows = block(r, ROW_BLOCK)
        x = x_ref[rows, :]
        h = _standardize(x) * (1.0 + mod_ref[1:2, :]) + mod_ref[0:1, :]
        z = _dot(h.astype(BF16), win_ref[...])
        ones_bd = ones_ref[...]
        zq = z[:, _Q0:_K0]
        q = zq * lax.rsqrt(_group_sum(zq * zq, ones_bd) * (1.0 / HEAD_DIM) + EPS) * qg_ref[...]
        zk = z[:, _K0:_V0]
        k = zk * lax.rsqrt(_group_sum(zk * zk, ones_bd) * (1.0 / HEAD_DIM) + EPS) * kg_ref[...]
        v = z[:, _V0:_CI0]
        if rope:
            cos = cos_ref[rows, :]
            sin = sin_ref[rows, :]
            q = _rope(q, jnp.concatenate([cos] * 4, axis=1), jnp.concatenate([sin] * 4, axis=1))
            k = _rope(k, cos, sin)
        elif stack_kv:
            for s in range(ROW_BLOCK // seq):
                sub = slice(s * seq, (s + 1) * seq)
                k_ref[s, 0] = kprev_ref[sub, :]
                v_ref[s, 0] = vprev_ref[sub, :]
                k_ref[s, 1] = k[sub, :]
                v_ref[s, 1] = v[sub, :]
        else:
            k_ref[rows, :] = k
            v_ref[rows, :] = v
        q_scr[rows, :] = (q * ATTN_SCALE).astype(BF16)
        for i, var in enumerate(_head_variants(k)):
            kvar_scr[i, rows, :] = var
        for i, var in enumerate(_head_variants(v)):
            vvar_scr[i, rows, :] = var
        u_scr[rows, :] = z[:, _CC0:_SU0] * z[:, _CI0:_CB0]
        cb_scr[rows, :] = z[:, _CB0:_CC0]
        su_scr[rows, :] = z[:, _SU0:_SV0]
        sv = z[:, _SV0:IN_WIDTH]
        d = sv - _group_sum(sv, ones_bd) * (1.0 / HEAD_DIM)
        vn = d * lax.rsqrt(_group_sum(d * d, ones_bd) * (1.0 / HEAD_DIM) + EPS) * sgug_ref[...]
        vn_scr[rows, :] = vn.astype(BF16)

    loop(n_rows // ROW_BLOCK, project)

    u = u_scr[...]
    pos = lax.broadcasted_iota(I32, u.shape, 0) & (seq - 1)
    up = jnp.where(pos == 0, 0.0, pltpu.roll(u, 1, 0))
    dn = jnp.where(pos == seq - 1, 0.0, pltpu.roll(u, n_rows - 1, 0))
    conv = up * convw_ref[0:1, :] + u * convw_ref[1:2, :] + dn * convw_ref[2:3, :] + convb_ref[...]
    mix_scr[:, ATTN_WIDTH:ATTN_WIDTH + CONV_WIDTH] = (cb_scr[...] * conv).astype(BF16)

    for n in range(n_rows // CHUNK):
        rows = pl.ds(n * CHUNK, CHUNK)
        vn = vn_scr[rows, :]
        lane = lax.broadcasted_iota(I32, vn.shape, 1)
        s = sgub_ref[...]
        for hd in range(SGU_HEADS):
            in_head = (lane >= hd * HEAD_DIM) & (lane < (hd + 1) * HEAD_DIM)
            masked = jnp.where(in_head, vn, jnp.zeros_like(vn))
            s = s + _dot(sguw_ref[hd].astype(BF16), masked)
        mix_scr[rows, ATTN_WIDTH + CONV_WIDTH:] = (su_scr[rows, :] * s).astype(BF16)

    def attend(s, r):
        rows = block(s * (seq // ATT_BLOCK) + r, ATT_BLOCK)
        keys = pl.ds(s * seq, seq + n_cache)
        for pair in range(N_Q_HEADS // 2):
            qp = q_scr[rows, pair * LANES:(pair + 1) * LANES]
            kv = pair // (N_Q_HEADS // N_KV_HEADS // 2)
            acc = jnp.zeros((ATT_BLOCK, LANES), F32)
            for parity in range(2):
                sc = _dot_nt(qp, kvar_scr[2 * kv + parity, keys, :])
                p = jnp.exp(sc - jnp.max(sc, axis=1, keepdims=True))
                denom = jnp.sum(p, axis=1, keepdims=True)
                acc = acc + _dot(p.astype(BF16), vvar_scr[2 * kv + parity, keys, :]) / denom
            mix_scr[rows, pair * LANES:(pair + 1) * LANES] = acc.astype(BF16)

    for s in range(n_seq):
        loop(seq // ATT_BLOCK, functools.partial(attend, s))

    def finish(r):
        rows = block(r, ROW_BLOCK)
        mix = _dot(mix_scr[rows, :], wout_ref[...])
        y = DEEPNORM_ALPHA * x_ref[rows, :] + mod_ref[2:3, :] * mix
        x1_ref[rows, :] = _standardize(y) * ln1g_ref[...] + ln1b_ref[...]

    loop(n_rows // ROW_BLOCK, finish)


def _full(shape):
    n = len(shape)
    return pl.BlockSpec(shape, lambda *_: (0,) * n)


def _resident(shape):
    n = len(shape)
    return pl.BlockSpec(shape, lambda *_: (0,) * n, pipeline_mode=pl.Buffered(1))


def _layer_block(arr, l, resident=False):
    shape = arr.shape[1:]
    kw = dict(pipeline_mode=pl.Buffered(1)) if resident else {}
    return pl.BlockSpec((None,) + shape, lambda *_: (l,) + (0,) * len(shape), **kw)


def _mixer_scratch(n_rows, n_cache):
    nk = n_rows + n_cache
    return [
        pltpu.VMEM((n_rows, ATTN_WIDTH), BF16),
        pltpu.VMEM((4, nk, LANES), BF16),
        pltpu.VMEM((4, nk, LANES), BF16),
        pltpu.VMEM((n_rows, CONV_WIDTH), F32),
        pltpu.VMEM((n_rows, CONV_WIDTH), F32),
        pltpu.VMEM((n_rows, SGU_WIDTH), F32),
        pltpu.VMEM((n_rows, SGU_WIDTH), BF16),
        pltpu.VMEM((n_rows, D_MODEL), BF16),
    ]


def _mixer_ctx(x, l, n_seq, seq, mod, w_in, w_out, small, kv_prev=None, side_casts=()):
    per_step = CTX_SEQS_PER_STEP
    rows = per_step * seq
    stack_kv = kv_prev is not None
    n_steps = n_seq // per_step
    kernel = functools.partial(_mixer_kernel, seq=seq, n_seq=per_step, n_cache=0, rope=False,
                               cast_weights=True, stack_kv=stack_kv,
                               side_jobs=tuple("merge" if job[0] == "merge" else "cast" for job in side_casts))
    in_specs = ([pl.BlockSpec((rows, D_MODEL), lambda i: (i, 0)),
                 pl.BlockSpec((None, None, SUBLANES, D_MODEL), lambda i: (l, 0, 0, 0)),
                 _layer_block(w_in, l, resident=True), _layer_block(w_out, l, resident=True)]
                + [_full(a.shape) if a.ndim == 2 else _layer_block(a, l) for a in small])
    args = [x, mod, w_in, w_out, *small]
    if stack_kv:
        in_specs += [pl.BlockSpec((rows, KV_WIDTH), lambda i: (i, 0))] * 2
        args += list(kv_prev)
        kv_spec = pl.BlockSpec((per_step, DEPTH, seq, KV_WIDTH), lambda i: (i, 0, 0, 0))
        kv_shape = jax.ShapeDtypeStruct((n_seq, DEPTH, seq, KV_WIDTH), F32)
    else:
        kv_spec = pl.BlockSpec((rows, KV_WIDTH), lambda i: (i, 0))
        kv_shape = jax.ShapeDtypeStruct((n_seq * seq, KV_WIDTH), F32)
    side_out_specs, side_out_shapes = [], []
    for kind, src, *earlier in side_casts:
        n_rows, n_cols = src.shape
        if kind == "cast":
            blk = n_rows // n_steps
            in_specs.append(pl.BlockSpec((blk, n_cols), lambda i: (i, 0)))
            args.append(src)
            out_rows, out_blk = n_rows, blk
        else:
            blk = n_rows // (2 * n_steps)
            odd = int(kind == "merge")
            in_specs.append(pl.BlockSpec((blk, n_cols), lambda i, odd=odd: (2 * i + odd, 0)))
            args.append(src)
            out_rows, out_blk = (n_rows, 2 * blk) if odd else (n_rows // 2, blk)
            if odd:
                in_specs.append(pl.BlockSpec((blk, n_cols), lambda i: (i, 0)))
                args.append(earlier[0])
        side_out_specs.append(pl.BlockSpec((out_blk, n_cols), lambda i: (i, 0)))
        side_out_shapes.append(jax.ShapeDtypeStruct((out_rows, n_cols), BF16))
    return pl.pallas_call(
        kernel,
        grid=(n_steps,),
        in_specs=in_specs,
        out_specs=[pl.BlockSpec((rows, D_MODEL), lambda i: (i, 0)), kv_spec, kv_spec,
                   _full((D_MODEL, IN_WIDTH)), _full((D_MODEL, D_MODEL))] + side_out_specs,
        out_shape=[jax.ShapeDtypeStruct((n_seq * seq, D_MODEL), F32), kv_shape, kv_shape,
                   jax.ShapeDtypeStruct((D_MODEL, IN_WIDTH), BF16), jax.ShapeDtypeStruct((D_MODEL, D_MODEL), BF16)]
        + side_out_shapes,
        scratch_shapes=_mixer_scratch(rows, 0),
        compiler_params=pltpu.CompilerParams(dimension_semantics=("arbitrary",), vmem_limit_bytes=VMEM_LIMIT),
        name="mixer_ctx",
    )(*args)


def _mixer_lat(x, l, n_seq, seq, mod, w_in_bf, w_out_bf, small, cos, sin, kc, vc):
    n_cache = kc.shape[2]
    kernel = functools.partial(_mixer_kernel, seq=seq, n_seq=1, n_cache=n_cache, rope=True,
                               cast_weights=False, stack_kv=False)
    cache_spec = pl.BlockSpec((None, None, n_cache, KV_WIDTH), lambda b: (b, l, 0, 0))
    return pl.pallas_call(
        kernel,
        grid=(n_seq,),
        in_specs=([pl.BlockSpec((seq, D_MODEL), lambda b: (b, 0)),
                   pl.BlockSpec((None, None, SUBLANES, D_MODEL), lambda b: (l, 1 + b, 0, 0)),
                   _resident((D_MODEL, IN_WIDTH)), _resident((D_MODEL, D_MODEL))]
                  + [_full(a.shape) if a.ndim == 2 else _layer_block(a, l) for a in small]
                  + [_full((seq, LANES)), _full((seq, LANES)), cache_spec, cache_spec]),
        out_specs=pl.BlockSpec((seq, D_MODEL), lambda b: (b, 0)),
        out_shape=jax.ShapeDtypeStruct((n_seq * seq, D_MODEL), F32),
        scratch_shapes=_mixer_scratch(seq, n_cache),
        compiler_params=pltpu.CompilerParams(dimension_semantics=("arbitrary",), vmem_limit_bytes=VMEM_LIMIT),
        name="mixer_lat",
    )(x, mod, w_in_bf, w_out_bf, *small, cos, sin, kc, vc)


def _two_stream_specs(block_rows, n_a, block_of, single_buffer_b=False):
    kw = dict(pipeline_mode=pl.Buffered(1)) if single_buffer_b else {}
    spec_a = pl.BlockSpec((block_rows, D_MODEL), lambda *ids: (jnp.minimum(block_of(*ids), n_a - 1), 0))
    spec_b = pl.BlockSpec((block_rows, D_MODEL), lambda *ids: (jnp.maximum(block_of(*ids) - n_a, 0), 0), **kw)
    return [spec_a, spec_b]


def _group_of_block(blk, block_rows, n_a, lat_seq):
    return jnp.where(blk < n_a, 0, 1 + (blk - n_a) // (lat_seq // block_rows))


def _ffn_kernel(xa_ref, xb_ref, mod_ref, w1_ref, w3_ref, w2_ref, g_ref, b_ref, oa_ref, ob_ref, *, n_a):
    is_a = pl.program_id(0) < n_a
    x = jnp.where(is_a, xa_ref[...], xb_ref[...])
    h = (_standardize(x) * (1.0 + mod_ref[4:5, :]) + mod_ref[3:4, :]).astype(BF16)
    acc = jnp.zeros(x.shape, F32)
    for c in range(D_FF // FFN_COLS):
        cols = slice(c * FFN_COLS, (c + 1) * FFN_COLS)
        a = _dot(h, w1_ref[:, cols].astype(BF16))
        b = _dot(h, w3_ref[:, cols].astype(BF16))
        acc = acc + _dot((_silu(a) * b).astype(BF16), w2_ref[cols, :].astype(BF16))
    y = DEEPNORM_ALPHA * x + mod_ref[5:6, :] * acc
    out = _standardize(y) * g_ref[...] + b_ref[...]

    @pl.when(is_a)
    def _():
        oa_ref[...] = out

    @pl.when(jnp.logical_not(is_a))
    def _():
        ob_ref[...] = out


def _ffn_dense(xa, xb, l, i, mods, w1, w3, w2, g, b, lat_seq):
    n_a, n_b = xa.shape[0] // FFN_TILE, xb.shape[0] // FFN_TILE
    x_specs = _two_stream_specs(FFN_TILE, n_a, lambda t: t)
    return pl.pallas_call(
        functools.partial(_ffn_kernel, n_a=n_a),
        grid=(n_a + n_b,),
        in_specs=x_specs + [
            pl.BlockSpec((None, None, SUBLANES, D_MODEL),
                         lambda t: (l, _group_of_block(t, FFN_TILE, n_a, lat_seq), 0, 0)),
            _layer_block(w1, i, resident=True), _layer_block(w3, i, resident=True),
            _layer_block(w2, i, resident=True), _layer_block(g, l), _layer_block(b, l)],
        out_specs=x_specs,
        out_shape=[jax.ShapeDtypeStruct(xa.shape, F32), jax.ShapeDtypeStruct(xb.shape, F32)],
        compiler_params=pltpu.CompilerParams(dimension_semantics=("arbitrary",), vmem_limit_bytes=VMEM_LIMIT),
        name="ffn_dense",
    )(xa, xb, mods, w1, w3, w2, g, b)


def _router_kernel(xa_ref, xb_ref, mod_ref, rw_ref, before_ref, dest_ref, gate_ref, start_ref, padded_ref, *,
                   n_a, lat_seq):
    c = pl.program_id(0)
    w_hi, w_lo = _split(rw_ref[...])
    n_blocks = MOE_CHUNK // ROUTER_BLOCK
    parts = []
    for blk in range(n_blocks):
        rows = pl.ds(blk * ROUTER_BLOCK, ROUTER_BLOCK)
        mod = mod_ref[_group_of_block(c * n_blocks + blk, ROUTER_BLOCK, n_a * n_blocks, lat_seq)]
        x = jnp.where(c < n_a, xa_ref[rows, :], xb_ref[rows, :])
        h = _standardize(x) * (1.0 + mod[4:5, :]) + mod[3:4, :]
        h_hi, h_lo = _split(h)
        parts.append(_dot_nt(w_hi, h_hi) + _dot_nt(w_hi, h_lo) + _dot_nt(w_lo, h_hi))
    logits = jnp.concatenate(parts, axis=1)
    eid = lax.broadcasted_iota(I32, logits.shape, 0).astype(F32)
    m1 = jnp.max(logits, axis=0, keepdims=True)
    i1 = jnp.min(jnp.where(logits == m1, eid, float(N_EXPERTS)), axis=0, keepdims=True)
    oh1 = eid == i1
    rest = jnp.where(oh1, -jnp.inf, logits)
    m2 = jnp.max(rest, axis=0, keepdims=True)
    i2 = jnp.min(jnp.where(rest == m2, eid, float(N_EXPERTS)), axis=0, keepdims=True)
    oh2 = eid == i2
    e = jnp.exp(m2 - m1)
    gate_ref[0:1, :] = 1.0 / (1.0 + e)
    gate_ref[1:2, :] = e / (1.0 + e)
    sel = jnp.where(oh1 | oh2, 1.0, 0.0)
    ranks = []
    seen = jnp.zeros((N_EXPERTS, 1), F32)
    for blk in range(n_blocks):
        s_blk = sel[:, blk * ROUTER_BLOCK:(blk + 1) * ROUTER_BLOCK]
        ranks.append(_dot(s_blk.astype(BF16), before_ref[...]) + seen)
        seen = seen + jnp.sum(s_blk, axis=1, keepdims=True)
    rank = jnp.concatenate(ranks, axis=1)
    eid_out = lax.broadcasted_iota(I32, start_ref.shape, 0).astype(F32)
    start = jnp.zeros(sel.shape, F32)
    start_out = jnp.zeros(start_ref.shape, F32)
    padded_out = jnp.zeros(start_ref.shape, F32)
    for ex in range(N_EXPERTS):
        cnt = jnp.sum(sel[ex:ex + 1, :], axis=1, keepdims=True)
        padded = jnp.ceil(cnt * (1.0 / MOE_GRAN)) * MOE_GRAN
        start = start + jnp.where(eid > ex, padded, 0.0)
        start_out = start_out + jnp.where(eid_out > ex, padded, 0.0)
        padded_out = padded_out + jnp.where(eid_out == ex, padded, 0.0)
    row = (start + rank) * SUBLANES
    dest_ref[0:1, :] = jnp.sum(jnp.where(oh1, row, 0.0), axis=0, keepdims=True).astype(I32)
    dest_ref[1:2, :] = jnp.sum(jnp.where(oh2, row, 0.0), axis=0, keepdims=True).astype(I32)
    start_ref[...] = start_out.astype(I32)
    padded_ref[...] = padded_out.astype(I32)


def _router(xa, xb, l, mods, rw_t, lat_seq):
    n_a = xa.shape[0] // MOE_CHUNK
    n_chunks = n_a + xb.shape[0] // MOE_CHUNK
    tok = np.arange(ROUTER_BLOCK)
    before = jnp.asarray(tok[:, None] < tok[None, :], BF16)
    return pl.pallas_call(
        functools.partial(_router_kernel, n_a=n_a, lat_seq=lat_seq),
        grid=(n_chunks,),
        in_specs=_two_stream_specs(MOE_CHUNK, n_a, lambda c: c) + [
                  _layer_block(mods, l),
                  _full((N_EXPERTS, D_MODEL)), _full((ROUTER_BLOCK, ROUTER_BLOCK))],
        out_specs=[pl.BlockSpec((None, 2, MOE_CHUNK), lambda c: (c, 0, 0)),
                   pl.BlockSpec((None, 2, MOE_CHUNK), lambda c: (c, 0, 0)),
                   pl.BlockSpec((None, N_EXPERTS, LANES), lambda c: (c, 0, 0)),
                   pl.BlockSpec((None, N_EXPERTS, LANES), lambda c: (c, 0, 0))],
        out_shape=[jax.ShapeDtypeStruct((n_chunks, 2, MOE_CHUNK), I32),
                   jax.ShapeDtypeStruct((n_chunks, 2, MOE_CHUNK), F32),
                   jax.ShapeDtypeStruct((n_chunks, N_EXPERTS, LANES), I32),
                   jax.ShapeDtypeStruct((n_chunks, N_EXPERTS, LANES), I32)],
        compiler_params=pltpu.CompilerParams(dimension_semantics=("arbitrary",), vmem_limit_bytes=VMEM_LIMIT),
        name="moe_router",
    )(xa, xb, mods, rw_t, before)


def _token_rows(first_row):
    return pl.ds(pl.multiple_of(first_row, SUBLANES), SUBLANES)


def _store_token_major(ref, tok0, val):
    for cc in range(D_MODEL // LANES):
        ref[pl.ds(tok0 * SUBLANES + cc, val.shape[0], stride=SUBLANES), :] = val[:, cc * LANES:(cc + 1) * LANES]


def _load_token_major(ref, tok0, n):
    return jnp.concatenate(
        [ref[pl.ds(tok0 * SUBLANES + cc, n, stride=SUBLANES), :] for cc in range(D_MODEL // LANES)], axis=1)


def _moe_kernel(start_ref, padded_ref, xa_ref, xb_ref, mod_ref, dest_ref, gate_ref, w1_ref, w3_ref, w2_ref,
                g_ref, b_ref, oa_ref, ob_ref, tok_scr, rows_scr, *, n_a):
    c = pl.program_id(0)
    j = pl.program_id(1)
    is_a = c < n_a
    first_expert_step = MOE_TOK_STEPS
    first_combine_step = MOE_TOK_STEPS + N_EXPERTS

    @pl.when((c == 0) & (j == 0))
    def _init():
        rows_scr[...] = jnp.zeros(rows_scr.shape, F32)

    @pl.when(j < first_expert_step)
    def _dispatch():
        for r in range(MOE_TOK_BLOCK // ROW_BLOCK):
            rows = pl.ds(r * ROW_BLOCK, ROW_BLOCK)
            x = jnp.where(is_a, xa_ref[rows, :], xb_ref[rows, :])
            h = _standardize(x) * (1.0 + mod_ref[4:5, :]) + mod_ref[3:4, :]
            _store_token_major(tok_scr, r * ROW_BLOCK, h)
        t0 = j * MOE_TOK_BLOCK

        def body(t, carry):
            row = tok_scr[_token_rows(t * SUBLANES), :]
            rows_scr[_token_rows(dest_ref[t0 + t]), :] = row
            rows_scr[_token_rows(dest_ref[MOE_CHUNK + t0 + t]), :] = row
            return carry
        lax.fori_loop(0, MOE_TOK_BLOCK, body, 0, unroll=8)

    def experts(row0, m):
        xin = _load_token_major(rows_scr, row0, m).astype(BF16)
        a = _dot(xin, w1_ref[...])
        b = _dot(xin, w3_ref[...])
        y = _dot((_silu(a) * b).astype(BF16), w2_ref[...])
        _store_token_major(rows_scr, row0, y)

    @pl.when((j >= first_expert_step) & (j < first_combine_step))
    def _experts():
        region = c * N_EXPERTS + (j - first_expert_step)
        start = start_ref[region]
        padded = padded_ref[region]
        n_full = lax.shift_right_logical(padded, MOE_BLOCK.bit_length() - 1)

        def body(i, carry):
            experts(start + i * MOE_BLOCK, MOE_BLOCK)
            return carry
        lax.fori_loop(0, n_full, body, 0)
        for m in range(MOE_GRAN, MOE_BLOCK, MOE_GRAN):
            pl.when(padded - n_full * MOE_BLOCK == m)(
                functools.partial(experts, start + n_full * MOE_BLOCK, m))

    @pl.when(j >= first_combine_step)
    def _combine():
        t0 = (j - first_combine_step) * MOE_TOK_BLOCK

        def body(t, carry):
            y0 = rows_scr[_token_rows(dest_ref[t0 + t]), :]
            y1 = rows_scr[_token_rows(dest_ref[MOE_CHUNK + t0 + t]), :]
            tok_scr[_token_rows(t * SUBLANES), :] = gate_ref[t0 + t] * y0 + gate_ref[MOE_CHUNK + t0 + t] * y1
            return carry
        lax.fori_loop(0, MOE_TOK_BLOCK, body, 0, unroll=8)
        for r in range(MOE_TOK_BLOCK // ROW_BLOCK):
            rows = pl.ds(r * ROW_BLOCK, ROW_BLOCK)
            ffn = _load_token_major(tok_scr, r * ROW_BLOCK, ROW_BLOCK)
            x = jnp.where(is_a, xa_ref[rows, :], xb_ref[rows, :])
            y = DEEPNORM_ALPHA * x + mod_ref[5:6, :] * ffn
            out = _standardize(y) * g_ref[...] + b_ref[...]

            @pl.when(is_a)
            def _():
                oa_ref[rows, :] = out

            @pl.when(jnp.logical_not(is_a))
            def _():
                ob_ref[rows, :] = out


def _ffn_moe(xa, xb, l, mods, rw_t, w1, w3, w2, g, b, lat_seq):
    n_a = xa.shape[0] // MOE_CHUNK
    n_chunks = n_a + xb.shape[0] // MOE_CHUNK
    n_a_blocks = n_a * MOE_TOK_STEPS
    dest, gates, start, padded = _router(xa, xb, l, mods, rw_t, lat_seq)
    first_expert_step = MOE_TOK_STEPS
    first_combine_step = MOE_TOK_STEPS + N_EXPERTS

    def token_block(c, j, *_):
        blk = jnp.where(j < first_combine_step, jnp.minimum(j, MOE_TOK_STEPS - 1), j - first_combine_step)
        return c * MOE_TOK_STEPS + blk

    def out_token_block(c, j, *_):
        return c * MOE_TOK_STEPS + jnp.maximum(j - first_combine_step, 0)

    def mod_of(c, j, *_):
        return (l, _group_of_block(token_block(c, j), MOE_TOK_BLOCK, n_a_blocks, lat_seq), 0, 0)

    def expert_of(c, j, *_):
        return (jnp.clip(j - first_expert_step, 0, N_EXPERTS - 1), 0, 0)

    grid_spec = pltpu.PrefetchScalarGridSpec(
        num_scalar_prefetch=2,
        grid=(n_chunks, MOE_STEPS),
        in_specs=_two_stream_specs(MOE_TOK_BLOCK, n_a_blocks, token_block) + [
                  pl.BlockSpec((None, None, SUBLANES, D_MODEL), mod_of),
                  pl.BlockSpec((2 * MOE_CHUNK,), lambda c, j, *_: (c,), memory_space=pltpu.SMEM),
                  pl.BlockSpec((2 * MOE_CHUNK,), lambda c, j, *_: (c,), memory_space=pltpu.SMEM),
                  pl.BlockSpec((None, D_MODEL, D_FF_EXPERT), expert_of),
                  pl.BlockSpec((None, D_MODEL, D_FF_EXPERT), expert_of),
                  pl.BlockSpec((None, D_FF_EXPERT, D_MODEL), expert_of),
                  _layer_block(g, l), _layer_block(b, l)],
        out_specs=_two_stream_specs(MOE_TOK_BLOCK, n_a_blocks, out_token_block),
        scratch_shapes=[pltpu.VMEM((MOE_TOK_BLOCK * SUBLANES, LANES), F32),
                        pltpu.VMEM((MOE_ROWS * SUBLANES, LANES), F32)],
    )
    return pl.pallas_call(
        functools.partial(_moe_kernel, n_a=n_a),
        grid_spec=grid_spec,
        out_shape=[jax.ShapeDtypeStruct(xa.shape, F32), jax.ShapeDtypeStruct(xb.shape, F32)],
        compiler_params=pltpu.CompilerParams(
            dimension_semantics=("arbitrary", "arbitrary"), vmem_limit_bytes=VMEM_LIMIT),
        name="moe_experts",
    )(start[:, :, 0].reshape(-1), padded[:, :, 0].reshape(-1), xa, xb, mods,
      dest.reshape(-1), gates.reshape(-1), w1, w3, w2, g, b)


def _rope_tables(n_tokens):
    t = np.arange(n_tokens)
    row = (t // GRID_W).astype(np.float32)
    col = (t % GRID_W).astype(np.float32)
    inv_freq = (np.float32(ROPE_THETA) ** (-np.arange(0, AXIS_ROT, 2, dtype=np.float32) / AXIS_ROT)).astype(np.float32)
    ang_r = row[:, None] * inv_freq
    ang_c = col[:, None] * inv_freq
    cos = np.concatenate([np.cos(ang_r), np.cos(ang_r), np.cos(ang_c), np.cos(ang_c)], axis=1)
    sin = np.concatenate([-np.sin(ang_r), np.sin(ang_r), -np.sin(ang_c), np.sin(ang_c)], axis=1)
    return jnp.asarray(np.tile(cos, (1, 2)), F32), jnp.asarray(np.tile(sin, (1, 2)), F32)


def kernel(x_prompt, x_sample, cache_k, cache_v, c, c_ctx, ada_w, ada_b, w_in, q_norm_g, k_norm_g, conv_w, conv_b, sgu_norm_g, sgu_w, sgu_b, w_out, ln1_g, ln1_b, ln2_g, ln2_b, ffn_w1, ffn_w3, ffn_w2, router_w, moe_w1, moe_w3, moe_w2):
    batch, seq, _ = x_prompt.shape
    dec_batch, dec_seq, _ = x_sample.shape
    past_len = cache_k.shape[2]
    n_ctx = batch * seq
    n_lat = dec_batch * dec_seq
    assert DEPTH == 2 and 1 + dec_batch <= SUBLANES
    assert seq == ATT_BLOCK and CTX_SEQS_PER_STEP * seq == ROW_BLOCK and batch % CTX_SEQS_PER_STEP == 0
    assert dec_seq % ROW_BLOCK == 0 and dec_seq & (dec_seq - 1) == 0
    assert n_ctx % MOE_CHUNK == 0 and n_lat % MOE_CHUNK == 0
    assert dec_seq % ROUTER_BLOCK == 0 and dec_seq % MOE_TOK_BLOCK == 0 and dec_seq % FFN_TILE == 0

    cond = jnp.zeros((SUBLANES, D_MODEL), F32).at[0].set(c_ctx).at[1:1 + dec_batch].set(c)
    mod = _modulation(cond, ada_w, ada_b)
    mod = mod.reshape(DEPTH, SUBLANES, 6, D_MODEL)[:, :1 + dec_batch]
    mod = jnp.pad(mod, ((0, 0), (0, 0), (0, SUBLANES - 6), (0, 0)))

    lane_id = np.arange(GROUP_TILE) // HEAD_DIM
    ones_bd = jnp.asarray(lane_id[:, None] == lane_id[None, :], BF16)
    cos, sin = _rope_tables(dec_seq)
    small = (
        jnp.tile(q_norm_g, (1, N_Q_HEADS))[:, None, :], jnp.tile(k_norm_g, (1, N_KV_HEADS))[:, None, :],
        ones_bd,
        conv_w, conv_b[:, None, :], sgu_norm_g[:, None, :], sgu_w,
        jnp.repeat(jnp.swapaxes(sgu_b, 1, 2), HEAD_DIM, axis=2),
        ln1_g[:, None, :], ln1_b[:, None, :],
    )
    g2, b2 = ln2_g[:, None, :], ln2_b[:, None, :]
    kc = cache_k.reshape(dec_batch, DEPTH, past_len, KV_WIDTH)
    vc = cache_v.reshape(dec_batch, DEPTH, past_len, KV_WIDTH)

    xs = [x_prompt.reshape(n_ctx, D_MODEL), x_sample.reshape(n_lat, D_MODEL)]
    assert moe_w1.shape[0] == 1
    w1_rows = moe_w1.reshape(-1, D_FF_EXPERT)
    w3_rows = moe_w3.reshape(-1, D_FF_EXPERT)
    w2_rows = moe_w2.reshape(-1, D_MODEL)
    kv = None
    for l in range(DEPTH):
        side = [("cast", w1_rows), ("even", w2_rows)] if l == 0 else [("cast", w3_rows), ("merge", w2_rows, w2_even)]
        x_ctx, k_ctx, v_ctx, w_in_bf, w_out_bf, cast_a, cast_b = _mixer_ctx(
            xs[0], l, batch, seq, mod, w_in, w_out, small, kv, side)
        if l == 0:
            w1_bf, w2_even = cast_a, cast_b
        else:
            w3_bf, w2_bf = cast_a, cast_b
        kv = (k_ctx, v_ctx)
        x_lat = _mixer_lat(xs[1], l, dec_batch, dec_seq, mod, w_in_bf, w_out_bf, small, cos, sin, kc, vc)
        if l % 2 == 0:
            xs = _ffn_dense(x_ctx, x_lat, l, l // 2, mod, ffn_w1, ffn_w3, ffn_w2, g2, b2, dec_seq)
        else:
            ws = (w1_bf.reshape(N_EXPERTS, D_MODEL, D_FF_EXPERT), w3_bf.reshape(N_EXPERTS, D_MODEL, D_FF_EXPERT),
                  w2_bf.reshape(N_EXPERTS, D_FF_EXPERT, D_MODEL))
            xs = _ffn_moe(x_ctx, x_lat, l, mod, router_w[l // 2].T, *ws, g2, b2, dec_seq)
    y_p = xs[0].reshape(batch, seq, D_MODEL)
    y_s = xs[1].reshape(dec_batch, dec_seq, D_MODEL)
    new_k = kv[0].reshape(batch, DEPTH, seq, N_KV_HEADS, HEAD_DIM)
    new_v = kv[1].reshape(batch, DEPTH, seq, N_KV_HEADS, HEAD_DIM)
    return (y_p, y_s, new_k, new_v)
```

```python
import functools

import numpy as np
import jax
import jax.numpy as jnp
from jax import lax
from jax.experimental import pallas as pl
from jax.experimental.pallas import tpu as pltpu

F32 = jnp.float32
BF16 = jnp.bfloat16
I32 = jnp.int32

D_MODEL = 1024
DEPTH = 2
GRID_W = 64
HEAD_DIM = 64
N_Q_HEADS = 8
N_KV_HEADS = 2
ATTN_WIDTH = N_Q_HEADS * HEAD_DIM
KV_WIDTH = N_KV_HEADS * HEAD_DIM
ATTN_SCALE = HEAD_DIM ** -0.5
ROPE_THETA = 10000.0
AXIS_ROT = HEAD_DIM // 2
CONV_WIDTH = 256
SGU_WIDTH = 256
SGU_HEADS = 4
CHUNK = 128
IN_WIDTH = 2048
D_FF = 2816
N_EXPERTS = 8
D_FF_EXPERT = 1408
EPS = 1e-6
DEEPNORM_ALPHA = (2 * DEPTH) ** 0.25

LANES = 128
SUBLANES = 8
ROW_BLOCK = 256
GROUP_TILE = 256
ATT_BLOCK = 256
CTX_SEQS_PER_STEP = 2
FFN_TILE = 512
FFN_COLS = 256
MOE_CHUNK = 2048
ROUTER_BLOCK = 1024
MOE_GRAN = 128
MOE_BLOCK = 256
MOE_TOK_BLOCK = 512
MOE_TOK_STEPS = MOE_CHUNK // MOE_TOK_BLOCK
MOE_STEPS = 2 * MOE_TOK_STEPS + N_EXPERTS
MOE_ROWS = 2 * MOE_CHUNK + N_EXPERTS * MOE_GRAN
VMEM_LIMIT = 58 * 1024 * 1024

_Q0, _K0, _V0, _CI0, _CB0, _CC0, _SU0, _SV0 = 0, 512, 640, 768, 1024, 1280, 1536, 1792


def _dot(a, b):
    return jnp.dot(a, b, preferred_element_type=F32)


def _dot_nt(a, b):
    return lax.dot_general(a, b, (((1,), (1,)), ((), ())), preferred_element_type=F32)


def _split(x):
    hi = x.astype(BF16)
    lo = (x - hi.astype(F32)).astype(BF16)
    return hi, lo


def _group_sum(x, ones_bd):
    outs = []
    for c0 in range(0, x.shape[1], GROUP_TILE):
        width = min(GROUP_TILE, x.shape[1] - c0)
        hi, lo = _split(x[:, c0:c0 + width])
        ones = ones_bd[:width, :width]
        outs.append(_dot(hi, ones) + _dot(lo, ones))
    return outs[0] if len(outs) == 1 else jnp.concatenate(outs, axis=1)


def _standardize(x):
    mu = jnp.mean(x, axis=-1, keepdims=True)
    d = x - mu
    return d * lax.rsqrt(jnp.mean(d * d, axis=-1, keepdims=True) + EPS)


def _silu(x):
    return x / (1.0 + jnp.exp(-x))


def _modulation_kernel(cond_ref, w_ref, b_ref, o_ref):
    s_hi, s_lo = _split(_silu(cond_ref[...]))
    w_hi, w_lo = _split(w_ref[...])
    o_ref[...] = _dot(s_hi, w_hi) + _dot(s_hi, w_lo) + _dot(s_lo, w_hi) + b_ref[...]


def _modulation(cond, ada_w, ada_b):
    n_out = ada_w.shape[-1]
    tn = 1536
    return pl.pallas_call(
        _modulation_kernel,
        grid=(DEPTH, n_out // tn),
        in_specs=[
            pl.BlockSpec((SUBLANES, D_MODEL), lambda l, j: (0, 0)),
            pl.BlockSpec((None, D_MODEL, tn), lambda l, j: (l, 0, j)),
            pl.BlockSpec((None, 1, tn), lambda l, j: (l, 0, j)),
        ],
        out_specs=pl.BlockSpec((None, SUBLANES, tn), lambda l, j: (l, 0, j)),
        out_shape=jax.ShapeDtypeStruct((DEPTH, SUBLANES, n_out), F32),
        compiler_params=pltpu.CompilerParams(
            dimension_semantics=("arbitrary", "arbitrary"), vmem_limit_bytes=VMEM_LIMIT),
        name="modulation",
    )(cond, ada_w, ada_b.reshape(DEPTH, 1, n_out))


def _rope(x, cos, sin_signed):
    w = x.shape[1]
    lane = lax.broadcasted_iota(I32, x.shape, 1)
    first_half = (lane & 31) < 16
    partner = jnp.where(first_half, pltpu.roll(x, w - 16, 1), pltpu.roll(x, 16, 1))
    return x * cos + partner * sin_signed


def _head_variants(x):
    lane = lax.broadcasted_iota(I32, x.shape, 1)
    lo = lane < HEAD_DIM
    xr = pltpu.roll(x, HEAD_DIM, 1)
    zero = jnp.zeros_like(x)
    return (jnp.where(lo, x, zero).astype(BF16), jnp.where(lo, zero, xr).astype(BF16),
            jnp.where(lo, xr, zero).astype(BF16), jnp.where(lo, zero, x).astype(BF16))


def _mixer_kernel(*refs, seq, n_seq, n_cache, rope, cast_weights, stack_kv, side_jobs=()):
    refs = list(refs)

    def take(n):
        out, refs[:] = refs[:n], refs[n:]
        return out

    x_ref, mod_ref, win_ref, wout_ref = take(4)
    qg_ref, kg_ref, ones_ref, convw_ref, convb_ref, sgug_ref, sguw_ref, sgub_ref, ln1g_ref, ln1b_ref = take(10)
    if rope:
        cos_ref, sin_ref, kc_ref, vc_ref = take(4)
    if stack_kv:
        kprev_ref, vprev_ref = take(2)
    side_in = [take(2 if job == "merge" else 1) for job in side_jobs]
    (x1_ref,) = take(1)
    if not rope:
        k_ref, v_ref = take(2)
    if cast_weights:
        winb_ref, woutb_ref = take(2)

        @pl.when(pl.program_id(0) == 0)
        def _cast():
            winb_ref[...] = win_ref[...].astype(BF16)
            woutb_ref[...] = wout_ref[...].astype(BF16)
        win_ref, wout_ref = winb_ref, woutb_ref
    for job, srcs, dst in zip(side_jobs, side_in, take(len(side_jobs))):
        if job == "merge":
            half = srcs[1].shape[0]
            dst[:half, :] = srcs[1][...]
            dst[half:, :] = srcs[0][...].astype(BF16)
        else:
            dst[...] = srcs[0][...].astype(BF16)
    q_scr, kvar_scr, vvar_scr, u_scr, cb_scr, su_scr, vn_scr, mix_scr = take(8)
    n_rows = n_seq * seq
    assert n_cache == 0 or n_seq == 1

    def loop(n, body):
        if n <= 2:
            for r in range(n):
                body(r)
        else:
            def step(r, carry):
                body(r)
                return carry
            lax.fori_loop(0, n, step, 0)

    def block(r, size):
        if isinstance(r, int):
            return pl.ds(r * size, size)
        return pl.ds(pl.multiple_of(r * size, size), size)

    if n_cache:
        for i, var in enumerate(_head_variants(kc_ref[...])):
            kvar_scr[i, pl.ds(seq, n_cache), :] = var
        for i, var in enumerate(_head_variants(vc_ref[...])):
            vvar_scr[i, pl.ds(seq, n_cache), :] = var

    def project(r):
        rows = block(r, ROW_BLOCK)
        x = x_ref[rows, :]
        h = _standardize(x) * (1.0 + mod_ref[1:2, :]) + mod_ref[0:1, :]
        z = _dot(h.astype(BF16), win_ref[...])
        ones_bd = ones_ref[...]
        zq = z[:, _Q0:_K0]
        q = zq * lax.rsqrt(_group_sum(zq * zq, ones_bd) * (1.0 / HEAD_DIM) + EPS) * qg_ref[...]
        zk = z[:, _K0:_V0]
        k = zk * lax.rsqrt(_group_sum(zk * zk, ones_bd) * (1.0 / HEAD_DIM) + EPS) * kg_ref[...]
        v = z[:, _V0:_CI0]
        if rope:
            cos = cos_ref[rows, :]
            sin = sin_ref[rows, :]
            q = _rope(q, jnp.concatenate([cos] * 4, axis=1), jnp.concatenate([sin] * 4, axis=1))
            k = _rope(k, cos, sin)
        elif stack_kv:
            for s in range(ROW_BLOCK // seq):
                sub = slice(s * seq, (s + 1) * seq)
                which = r * (ROW_BLOCK // seq) + s
                prev = pl.ds(which * seq, seq)
                k_ref[which, 0] = kprev_ref[prev, :]
                v_ref[which, 0] = vprev_ref[prev, :]
                k_ref[which, 1] = k[sub, :]
                v_ref[which, 1] = v[sub, :]
        else:
            k_ref[rows, :] = k
            v_ref[rows, :] = v
        q_scr[rows, :] = (q * ATTN_SCALE).astype(BF16)
        for i, var in enumerate(_head_variants(k)):
            kvar_scr[i, rows, :] = var
        for i, var in enumerate(_head_variants(v)):
            vvar_scr[i, rows, :] = var
        u_scr[rows, :] = z[:, _CC0:_SU0] * z[:, _CI0:_CB0]
        cb_scr[rows, :] = z[:, _CB0:_CC0]
        su_scr[rows, :] = z[:, _SU0:_SV0]
        sv = z[:, _SV0:IN_WIDTH]
        d = sv - _group_sum(sv, ones_bd) * (1.0 / HEAD_DIM)
        vn = d * lax.rsqrt(_group_sum(d * d, ones_bd) * (1.0 / HEAD_DIM) + EPS) * sgug_ref[...]
        vn_scr[rows, :] = vn.astype(BF16)

    loop(n_rows // ROW_BLOCK, project)

    u = u_scr[...]
    pos = lax.broadcasted_iota(I32, u.shape, 0) & (seq - 1)
    up = jnp.where(pos == 0, 0.0, pltpu.roll(u, 1, 0))
    dn = jnp.where(pos == seq - 1, 0.0, pltpu.roll(u, n_rows - 1, 0))
    conv = up * convw_ref[0:1, :] + u * convw_ref[1:2, :] + dn * convw_ref[2:3, :] + convb_ref[...]
    mix_scr[:, ATTN_WIDTH:ATTN_WIDTH + CONV_WIDTH] = (cb_scr[...] * conv).astype(BF16)

    for n in range(n_rows // CHUNK):
        rows = pl.ds(n * CHUNK, CHUNK)
        vn = vn_scr[rows, :]
        lane = lax.broadcasted_iota(I32, vn.shape, 1)
        s = sgub_ref[...]
        for hd in range(SGU_HEADS):
            in_head = (lane >= hd * HEAD_DIM) & (lane < (hd + 1) * HEAD_DIM)
            masked = jnp.where(in_head, vn, jnp.zeros_like(vn))
            s = s + _dot(sguw_ref[hd].astype(BF16), masked)
        mix_scr[rows, ATTN_WIDTH + CONV_WIDTH:] = (su_scr[rows, :] * s).astype(BF16)

    def attend(s, r):
        rows = block(s * (seq // ATT_BLOCK) + r, ATT_BLOCK)
        keys = pl.ds(s * seq, seq + n_cache)
        for pair in range(N_Q_HEADS // 2):
            qp = q_scr[rows, pair * LANES:(pair + 1) * LANES]
            kv = pair // (N_Q_HEADS // N_KV_HEADS // 2)
            acc = jnp.zeros((ATT_BLOCK, LANES), F32)
            for parity in range(2):
                sc = _dot_nt(qp, kvar_scr[2 * kv + parity, keys, :])
                p = jnp.exp(sc - jnp.max(sc, axis=1, keepdims=True))
                denom = jnp.sum(p, axis=1, keepdims=True)
                acc = acc + _dot(p.astype(BF16), vvar_scr[2 * kv + parity, keys, :]) / denom
            mix_scr[rows, pair * LANES:(pair + 1) * LANES] = acc.astype(BF16)

    for s in range(n_seq):
        loop(seq // ATT_BLOCK, functools.partial(attend, s))

    def finish(r):
        rows = block(r, ROW_BLOCK)
        mix = _dot(mix_scr[rows, :], wout_ref[...])
        y = DEEPNORM_ALPHA * x_ref[rows, :] + mod_ref[2:3, :] * mix
        x1_ref[rows, :] = _standardize(y) * ln1g_ref[...] + ln1b_ref[...]

    loop(n_rows // ROW_BLOCK, finish)


def _full(shape):
    n = len(shape)
    return pl.BlockSpec(shape, lambda *_: (0,) * n)


def _resident(shape):
    n = len(shape)
    return pl.BlockSpec(shape, lambda *_: (0,) * n, pipeline_mode=pl.Buffered(1))


def _layer_block(arr, l, resident=False):
    shape = arr.shape[1:]
    kw = dict(pipeline_mode=pl.Buffered(1)) if resident else {}
    return pl.BlockSpec((None,) + shape, lambda *_: (l,) + (0,) * len(shape), **kw)


def _mixer_scratch(n_rows, n_cache):
    nk = n_rows + n_cache
    return [
        pltpu.VMEM((n_rows, ATTN_WIDTH), BF16),
        pltpu.VMEM((4, nk, LANES), BF16),
        pltpu.VMEM((4, nk, LANES), BF16),
        pltpu.VMEM((n_rows, CONV_WIDTH), F32),
        pltpu.VMEM((n_rows, CONV_WIDTH), F32),
        pltpu.VMEM((n_rows, SGU_WIDTH), F32),
        pltpu.VMEM((n_rows, SGU_WIDTH), BF16),
        pltpu.VMEM((n_rows, D_MODEL), BF16),
    ]


def _mixer_ctx(x, l, n_seq, seq, mod, w_in, w_out, small, kv_prev=None, side_casts=()):
    per_step = CTX_SEQS_PER_STEP
    rows = per_step * seq
    stack_kv = kv_prev is not None
    n_steps = n_seq // per_step
    kernel = functools.partial(_mixer_kernel, seq=seq, n_seq=per_step, n_cache=0, rope=False,
                               cast_weights=True, stack_kv=stack_kv,
                               side_jobs=tuple("merge" if job[0] == "merge" else "cast" for job in side_casts))
    in_specs = ([pl.BlockSpec((rows, D_MODEL), lambda i: (i, 0)),
                 pl.BlockSpec((None, None, SUBLANES, D_MODEL), lambda i: (l, 0, 0, 0)),
                 _layer_block(w_in, l, resident=True), _layer_block(w_out, l, resident=True)]
                + [_full(a.shape) if a.ndim == 2 else _layer_block(a, l) for a in small])
    args = [x, mod, w_in, w_out, *small]
    if stack_kv:
        in_specs += [pl.BlockSpec((rows, KV_WIDTH), lambda i: (i, 0))] * 2
        args += list(kv_prev)
        kv_spec = pl.BlockSpec((per_step, DEPTH, seq, KV_WIDTH), lambda i: (i, 0, 0, 0))
        kv_shape = jax.ShapeDtypeStruct((n_seq, DEPTH, seq, KV_WIDTH), F32)
    else:
        kv_spec = pl.BlockSpec((rows, KV_WIDTH), lambda i: (i, 0))
        kv_shape = jax.ShapeDtypeStruct((n_seq * seq, KV_WIDTH), F32)
    side_out_specs, side_out_shapes = [], []
    for kind, src, *earlier in side_casts:
        n_rows, n_cols = src.shape
        if kind == "cast":
            blk = n_rows // n_steps
            in_specs.append(pl.BlockSpec((blk, n_cols), lambda i: (i, 0)))
            args.append(src)
            out_rows, out_blk = n_rows, blk
        else:
            blk = n_rows // (2 * n_steps)
            odd = int(kind == "merge")
            in_specs.append(pl.BlockSpec((blk, n_cols), lambda i, odd=odd: (2 * i + odd, 0)))
            args.append(src)
            out_rows, out_blk = (n_rows, 2 * blk) if odd else (n_rows // 2, blk)
            if odd:
                in_specs.append(pl.BlockSpec((blk, n_cols), lambda i: (i, 0)))
                args.append(earlier[0])
        side_out_specs.append(pl.BlockSpec((out_blk, n_cols), lambda i: (i, 0)))
        side_out_shapes.append(jax.ShapeDtypeStruct((out_rows, n_cols), BF16))
    return pl.pallas_call(
        kernel,
        grid=(n_steps,),
        in_specs=in_specs,
        out_specs=[pl.BlockSpec((rows, D_MODEL), lambda i: (i, 0)), kv_spec, kv_spec,
                   _full((D_MODEL, IN_WIDTH)), _full((D_MODEL, D_MODEL))] + side_out_specs,
        out_shape=[jax.ShapeDtypeStruct((n_seq * seq, D_MODEL), F32), kv_shape, kv_shape,
                   jax.ShapeDtypeStruct((D_MODEL, IN_WIDTH), BF16), jax.ShapeDtypeStruct((D_MODEL, D_MODEL), BF16)]
        + side_out_shapes,
        scratch_shapes=_mixer_scratch(rows, 0),
        compiler_params=pltpu.CompilerParams(dimension_semantics=("arbitrary",), vmem_limit_bytes=VMEM_LIMIT),
        name="mixer_ctx",
    )(*args)


def _mixer_lat(x, l, n_seq, seq, mod, w_in_bf, w_out_bf, small, cos, sin, kc, vc, first_block=0):
    n_cache = kc.shape[2]
    kernel = functools.partial(_mixer_kernel, seq=seq, n_seq=1, n_cache=n_cache, rope=True,
                               cast_weights=False, stack_kv=False)
    cache_spec = pl.BlockSpec((None, None, n_cache, KV_WIDTH), lambda b: (b, l, 0, 0))
    return pl.pallas_call(
        kernel,
        grid=(n_seq,),
        in_specs=([pl.BlockSpec((seq, D_MODEL), lambda b: (first_block + b, 0)),
                   pl.BlockSpec((None, None, SUBLANES, D_MODEL), lambda b: (l, 1 + b, 0, 0)),
                   _resident((D_MODEL, IN_WIDTH)), _resident((D_MODEL, D_MODEL))]
                  + [_full(a.shape) if a.ndim == 2 else _layer_block(a, l) for a in small]
                  + [_full((seq, LANES)), _full((seq, LANES)), cache_spec, cache_spec]),
        out_specs=pl.BlockSpec((seq, D_MODEL), lambda b: (b, 0)),
        out_shape=jax.ShapeDtypeStruct((n_seq * seq, D_MODEL), F32),
        scratch_shapes=_mixer_scratch(seq, n_cache),
        compiler_params=pltpu.CompilerParams(dimension_semantics=("arbitrary",), vmem_limit_bytes=VMEM_LIMIT),
        name="mixer_lat",
    )(x, mod, w_in_bf, w_out_bf, *small, cos, sin, kc, vc)


def _two_stream_specs(block_rows, n_a, block_of, single_buffer_b=False):
    kw = dict(pipeline_mode=pl.Buffered(1)) if single_buffer_b else {}
    spec_a = pl.BlockSpec((block_rows, D_MODEL), lambda *ids: (jnp.minimum(block_of(*ids), n_a - 1), 0))
    spec_b = pl.BlockSpec((block_rows, D_MODEL), lambda *ids: (jnp.maximum(block_of(*ids) - n_a, 0), 0), **kw)
    return [spec_a, spec_b]


def _group_of_block(blk, block_rows, n_a, lat_seq):
    return jnp.where(blk < n_a, 0, 1 + (blk - n_a) // (lat_seq // block_rows))


def _ffn_kernel(xa_ref, xb_ref, mod_ref, w1_ref, w3_ref, w2_ref, g_ref, b_ref, o_ref, y_scr, *, n_a, n_tiles):
    t = pl.program_id(0)

    def matmuls():
        x = jnp.where(t < n_a, xa_ref[...], xb_ref[...])
        h = (_standardize(x) * (1.0 + mod_ref[4:5, :]) + mod_ref[3:4, :]).astype(BF16)
        acc = jnp.zeros(x.shape, F32)
        for c in range(D_FF // FFN_COLS):
            cols = slice(c * FFN_COLS, (c + 1) * FFN_COLS)
            a = _dot(h, w1_ref[:, cols].astype(BF16))
            b = _dot(h, w3_ref[:, cols].astype(BF16))
            acc = acc + _dot((_silu(a) * b).astype(BF16), w2_ref[cols, :].astype(BF16))
        y_scr[...] = DEEPNORM_ALPHA * x + mod_ref[5:6, :] * acc

    def normalise():
        o_ref[...] = _standardize(y_scr[...]) * g_ref[...] + b_ref[...]

    pl.when(t == 0)(matmuls)

    @pl.when((t > 0) & (t < n_tiles))
    def _():
        normalise()
        matmuls()

    pl.when(t == n_tiles)(normalise)


def _ffn_dense(xa, xb, l, i, mods, w1, w3, w2, g, b, lat_seq):
    n_a, n_b = xa.shape[0] // FFN_TILE, xb.shape[0] // FFN_TILE
    n_tiles = n_a + n_b
    tile_of = lambda t: jnp.minimum(t, n_tiles - 1)
    return pl.pallas_call(
        functools.partial(_ffn_kernel, n_a=n_a, n_tiles=n_tiles),
        grid=(n_tiles + 1,),
        in_specs=_two_stream_specs(FFN_TILE, n_a, tile_of) + [
            pl.BlockSpec((None, None, SUBLANES, D_MODEL),
                         lambda t: (l, _group_of_block(tile_of(t), FFN_TILE, n_a, lat_seq), 0, 0)),
            _layer_block(w1, i, resident=True), _layer_block(w3, i, resident=True),
            _layer_block(w2, i, resident=True), _layer_block(g, l), _layer_block(b, l)],
        out_specs=pl.BlockSpec((FFN_TILE, D_MODEL), lambda t: (jnp.maximum(t - 1, 0), 0)),
        out_shape=jax.ShapeDtypeStruct((xa.shape[0] + xb.shape[0], D_MODEL), F32),
        scratch_shapes=[pltpu.VMEM((FFN_TILE, D_MODEL), F32)],
        compiler_params=pltpu.CompilerParams(dimension_semantics=("arbitrary",), vmem_limit_bytes=VMEM_LIMIT),
        name="ffn_dense",
    )(xa, xb, mods, w1, w3, w2, g, b)


def _router_kernel(xa_ref, xb_ref, mod_ref, rw_ref, before_ref, dest_ref, gate_ref, start_ref, padded_ref, *,
                   n_a, lat_seq):
    c = pl.program_id(0)
    w_hi, w_lo = _split(rw_ref[...])
    n_blocks = MOE_CHUNK // ROUTER_BLOCK
    parts = []
    for blk in range(n_blocks):
        rows = pl.ds(blk * ROUTER_BLOCK, ROUTER_BLOCK)
        mod = mod_ref[_group_of_block(c * n_blocks + blk, ROUTER_BLOCK, n_a * n_blocks, lat_seq)]
        x = jnp.where(c < n_a, xa_ref[rows, :], xb_ref[rows, :])
        h = _standardize(x) * (1.0 + mod[4:5, :]) + mod[3:4, :]
        h_hi, h_lo = _split(h)
        parts.append(_dot_nt(w_hi, h_hi) + _dot_nt(w_hi, h_lo) + _dot_nt(w_lo, h_hi))
    logits = jnp.concatenate(parts, axis=1)
    eid = lax.broadcasted_iota(I32, logits.shape, 0).astype(F32)
    m1 = jnp.max(logits, axis=0, keepdims=True)
    i1 = jnp.min(jnp.where(logits == m1, eid, float(N_EXPERTS)), axis=0, keepdims=True)
    oh1 = eid == i1
    rest = jnp.where(oh1, -jnp.inf, logits)
    m2 = jnp.max(rest, axis=0, keepdims=True)
    i2 = jnp.min(jnp.where(rest == m2, eid, float(N_EXPERTS)), axis=0, keepdims=True)
    oh2 = eid == i2
    e = jnp.exp(m2 - m1)
    gate_ref[0:1, :] = 1.0 / (1.0 + e)
    gate_ref[1:2, :] = e / (1.0 + e)
    sel = jnp.where(oh1 | oh2, 1.0, 0.0)
    ranks = []
    seen = jnp.zeros((N_EXPERTS, 1), F32)
    for blk in range(n_blocks):
        s_blk = sel[:, blk * ROUTER_BLOCK:(blk + 1) * ROUTER_BLOCK]
        ranks.append(_dot(s_blk.astype(BF16), before_ref[...]) + seen)
        seen = seen + jnp.sum(s_blk, axis=1, keepdims=True)
    rank = jnp.concatenate(ranks, axis=1)
    eid_out = lax.broadcasted_iota(I32, start_ref.shape, 0).astype(F32)
    start = jnp.zeros(sel.shape, F32)
    start_out = jnp.zeros(start_ref.shape, F32)
    padded_out = jnp.zeros(start_ref.shape, F32)
    for ex in range(N_EXPERTS):
        cnt = jnp.sum(sel[ex:ex + 1, :], axis=1, keepdims=True)
        padded = jnp.ceil(cnt * (1.0 / MOE_GRAN)) * MOE_GRAN
        start = start + jnp.where(eid > ex, padded, 0.0)
        start_out = start_out + jnp.where(eid_out > ex, padded, 0.0)
        padded_out = padded_out + jnp.where(eid_out == ex, padded, 0.0)
    row = (start + rank) * SUBLANES
    dest_ref[0:1, :] = jnp.sum(jnp.where(oh1, row, 0.0), axis=0, keepdims=True).astype(I32)
    dest_ref[1:2, :] = jnp.sum(jnp.where(oh2, row, 0.0), axis=0, keepdims=True).astype(I32)
    start_ref[...] = start_out.astype(I32)
    padded_ref[...] = padded_out.astype(I32)


def _router(xa, xb, l, mods, rw_t, lat_seq):
    n_a = xa.shape[0] // MOE_CHUNK
    n_chunks = n_a + xb.shape[0] // MOE_CHUNK
    tok = np.arange(ROUTER_BLOCK)
    before = jnp.asarray(tok[:, None] < tok[None, :], BF16)
    return pl.pallas_call(
        functools.partial(_router_kernel, n_a=n_a, lat_seq=lat_seq),
        grid=(n_chunks,),
        in_specs=_two_stream_specs(MOE_CHUNK, n_a, lambda c: c) + [
                  _layer_block(mods, l),
                  _full((N_EXPERTS, D_MODEL)), _full((ROUTER_BLOCK, ROUTER_BLOCK))],
        out_specs=[pl.BlockSpec((None, 2, MOE_CHUNK), lambda c: (c, 0, 0)),
                   pl.BlockSpec((None, 2, MOE_CHUNK), lambda c: (c, 0, 0)),
                   pl.BlockSpec((None, N_EXPERTS, LANES), lambda c: (c, 0, 0)),
                   pl.BlockSpec((None, N_EXPERTS, LANES), lambda c: (c, 0, 0))],
        out_shape=[jax.ShapeDtypeStruct((n_chunks, 2, MOE_CHUNK), I32),
                   jax.ShapeDtypeStruct((n_chunks, 2, MOE_CHUNK), F32),
                   jax.ShapeDtypeStruct((n_chunks, N_EXPERTS, LANES), I32),
                   jax.ShapeDtypeStruct((n_chunks, N_EXPERTS, LANES), I32)],
        compiler_params=pltpu.CompilerParams(dimension_semantics=("arbitrary",), vmem_limit_bytes=VMEM_LIMIT),
        name="moe_router",
    )(xa, xb, mods, rw_t, before)


def _token_rows(first_row):
    return pl.ds(pl.multiple_of(first_row, SUBLANES), SUBLANES)


def _store_token_major(ref, tok0, val):
    for cc in range(D_MODEL // LANES):
        ref[pl.ds(tok0 * SUBLANES + cc, val.shape[0], stride=SUBLANES), :] = val[:, cc * LANES:(cc + 1) * LANES]


def _load_token_major(ref, tok0, n):
    return jnp.concatenate(
        [ref[pl.ds(tok0 * SUBLANES + cc, n, stride=SUBLANES), :] for cc in range(D_MODEL // LANES)], axis=1)


def _moe_kernel(start_ref, padded_ref, xa_ref, xb_ref, mod_ref, dest_ref, gate_ref, w1_ref, w3_ref, w2_ref,
                g_ref, b_ref, oa_ref, ob_ref, tok_scr, rows_scr, *, n_a):
    c = pl.program_id(0)
    j = pl.program_id(1)
    is_a = c < n_a
    first_expert_step = MOE_TOK_STEPS
    first_combine_step = MOE_TOK_STEPS + N_EXPERTS

    @pl.when((c == 0) & (j == 0))
    def _init():
        rows_scr[...] = jnp.zeros(rows_scr.shape, F32)

    @pl.when(j < first_expert_step)
    def _dispatch():
        for r in range(MOE_TOK_BLOCK // ROW_BLOCK):
            rows = pl.ds(r * ROW_BLOCK, ROW_BLOCK)
            x = jnp.where(is_a, xa_ref[rows, :], xb_ref[rows, :])
            h = _standardize(x) * (1.0 + mod_ref[4:5, :]) + mod_ref[3:4, :]
            _store_token_major(tok_scr, r * ROW_BLOCK, h)
        t0 = j * MOE_TOK_BLOCK

        def body(t, carry):
            row = tok_scr[_token_rows(t * SUBLANES), :]
            rows_scr[_token_rows(dest_ref[t0 + t]), :] = row
            rows_scr[_token_rows(dest_ref[MOE_CHUNK + t0 + t]), :] = row
            return carry
        lax.fori_loop(0, MOE_TOK_BLOCK, body, 0, unroll=8)

    def experts(row0, m):
        xin = _load_token_major(rows_scr, row0, m).astype(BF16)
        a = _dot(xin, w1_ref[...])
        b = _dot(xin, w3_ref[...])
        y = _dot((_silu(a) * b).astype(BF16), w2_ref[...])
        _store_token_major(rows_scr, row0, y)

    @pl.when((j >= first_expert_step) & (j < first_combine_step))
    def _experts():
        region = c * N_EXPERTS + (j - first_expert_step)
        start = start_ref[region]
        padded = padded_ref[region]
        n_full = lax.shift_right_logical(padded, MOE_BLOCK.bit_length() - 1)

        def body(i, carry):
            experts(start + i * MOE_BLOCK, MOE_BLOCK)
            return carry
        lax.fori_loop(0, n_full, body, 0)
        for m in range(MOE_GRAN, MOE_BLOCK, MOE_GRAN):
            pl.when(padded - n_full * MOE_BLOCK == m)(
                functools.partial(experts, start + n_full * MOE_BLOCK, m))

    @pl.when(j >= first_combine_step)
    def _combine():
        t0 = (j - first_combine_step) * MOE_TOK_BLOCK

        def body(t, carry):
            y0 = rows_scr[_token_rows(dest_ref[t0 + t]), :]
            y1 = rows_scr[_token_rows(dest_ref[MOE_CHUNK + t0 + t]), :]
            tok_scr[_token_rows(t * SUBLANES), :] = gate_ref[t0 + t] * y0 + gate_ref[MOE_CHUNK + t0 + t] * y1
            return carry
        lax.fori_loop(0, MOE_TOK_BLOCK, body, 0, unroll=8)
        for r in range(MOE_TOK_BLOCK // ROW_BLOCK):
            rows = pl.ds(r * ROW_BLOCK, ROW_BLOCK)
            ffn = _load_token_major(tok_scr, r * ROW_BLOCK, ROW_BLOCK)
            x = jnp.where(is_a, xa_ref[rows, :], xb_ref[rows, :])
            y = DEEPNORM_ALPHA * x + mod_ref[5:6, :] * ffn
            out = _standardize(y) * g_ref[...] + b_ref[...]

            @pl.when(is_a)
            def _():
                oa_ref[rows, :] = out

            @pl.when(jnp.logical_not(is_a))
            def _():
                ob_ref[rows, :] = out


def _ffn_moe(xa, xb, l, mods, rw_t, w1, w3, w2, g, b, lat_seq):
    n_a = xa.shape[0] // MOE_CHUNK
    n_chunks = n_a + xb.shape[0] // MOE_CHUNK
    n_a_blocks = n_a * MOE_TOK_STEPS
    dest, gates, start, padded = _router(xa, xb, l, mods, rw_t, lat_seq)
    first_expert_step = MOE_TOK_STEPS
    first_combine_step = MOE_TOK_STEPS + N_EXPERTS

    def token_block(c, j, *_):
        blk = jnp.where(j < first_combine_step, jnp.minimum(j, MOE_TOK_STEPS - 1), j - first_combine_step)
        return c * MOE_TOK_STEPS + blk

    def out_token_block(c, j, *_):
        return c * MOE_TOK_STEPS + jnp.maximum(j - first_combine_step, 0)

    def mod_of(c, j, *_):
        return (l, _group_of_block(token_block(c, j), MOE_TOK_BLOCK, n_a_blocks, lat_seq), 0, 0)

    def expert_of(c, j, *_):
        return (jnp.clip(j - first_expert_step, 0, N_EXPERTS - 1), 0, 0)

    grid_spec = pltpu.PrefetchScalarGridSpec(
        num_scalar_prefetch=2,
        grid=(n_chunks, MOE_STEPS),
        in_specs=_two_stream_specs(MOE_TOK_BLOCK, n_a_blocks, token_block) + [
                  pl.BlockSpec((None, None, SUBLANES, D_MODEL), mod_of),
                  pl.BlockSpec((2 * MOE_CHUNK,), lambda c, j, *_: (c,), memory_space=pltpu.SMEM),
                  pl.BlockSpec((2 * MOE_CHUNK,), lambda c, j, *_: (c,), memory_space=pltpu.SMEM),
                  pl.BlockSpec((None, D_MODEL, D_FF_EXPERT), expert_of),
                  pl.BlockSpec((None, D_MODEL, D_FF_EXPERT), expert_of),
                  pl.BlockSpec((None, D_FF_EXPERT, D_MODEL), expert_of),
                  _layer_block(g, l), _layer_block(b, l)],
        out_specs=_two_stream_specs(MOE_TOK_BLOCK, n_a_blocks, out_token_block),
        scratch_shapes=[pltpu.VMEM((MOE_TOK_BLOCK * SUBLANES, LANES), F32),
                        pltpu.VMEM((MOE_ROWS * SUBLANES, LANES), F32)],
    )
    return pl.pallas_call(
        functools.partial(_moe_kernel, n_a=n_a),
        grid_spec=grid_spec,
        out_shape=[jax.ShapeDtypeStruct(xa.shape, F32), jax.ShapeDtypeStruct(xb.shape, F32)],
        compiler_params=pltpu.CompilerParams(
            dimension_semantics=("arbitrary", "arbitrary"), vmem_limit_bytes=VMEM_LIMIT),
        name="moe_experts",
    )(start[:, :, 0].reshape(-1), padded[:, :, 0].reshape(-1), xa, xb, mods,
      dest.reshape(-1), gates.reshape(-1), w1, w3, w2, g, b)


def _rope_tables(n_tokens):
    t = np.arange(n_tokens)
    row = (t // GRID_W).astype(np.float32)
    col = (t % GRID_W).astype(np.float32)
    inv_freq = (np.float32(ROPE_THETA) ** (-np.arange(0, AXIS_ROT, 2, dtype=np.float32) / AXIS_ROT)).astype(np.float32)
    ang_r = row[:, None] * inv_freq
    ang_c = col[:, None] * inv_freq
    cos = np.concatenate([np.cos(ang_r), np.cos(ang_r), np.cos(ang_c), np.cos(ang_c)], axis=1)
    sin = np.concatenate([-np.sin(ang_r), np.sin(ang_r), -np.sin(ang_c), np.sin(ang_c)], axis=1)
    return jnp.asarray(np.tile(cos, (1, 2)), F32), jnp.asarray(np.tile(sin, (1, 2)), F32)


def kernel(x_prompt, x_sample, cache_k, cache_v, c, c_ctx, ada_w, ada_b, w_in, q_norm_g, k_norm_g, conv_w, conv_b, sgu_norm_g, sgu_w, sgu_b, w_out, ln1_g, ln1_b, ln2_g, ln2_b, ffn_w1, ffn_w3, ffn_w2, router_w, moe_w1, moe_w3, moe_w2):
    batch, seq, _ = x_prompt.shape
    dec_batch, dec_seq, _ = x_sample.shape
    past_len = cache_k.shape[2]
    n_ctx = batch * seq
    n_lat = dec_batch * dec_seq
    assert DEPTH == 2 and 1 + dec_batch <= SUBLANES
    assert seq == ATT_BLOCK and seq % ROW_BLOCK == 0 and batch % CTX_SEQS_PER_STEP == 0
    assert dec_seq % ROW_BLOCK == 0 and dec_seq & (dec_seq - 1) == 0
    assert n_ctx % MOE_CHUNK == 0 and n_lat % MOE_CHUNK == 0 and n_ctx % dec_seq == 0
    assert dec_seq % ROUTER_BLOCK == 0 and dec_seq % MOE_TOK_BLOCK == 0 and dec_seq % FFN_TILE == 0

    cond = jnp.zeros((SUBLANES, D_MODEL), F32).at[0].set(c_ctx).at[1:1 + dec_batch].set(c)
    mod = _modulation(cond, ada_w, ada_b)
    mod = mod.reshape(DEPTH, SUBLANES, 6, D_MODEL)[:, :1 + dec_batch]
    mod = jnp.pad(mod, ((0, 0), (0, 0), (0, SUBLANES - 6), (0, 0)))

    lane_id = np.arange(GROUP_TILE) // HEAD_DIM
    ones_bd = jnp.asarray(lane_id[:, None] == lane_id[None, :], BF16)
    cos, sin = _rope_tables(dec_seq)
    small = (
        jnp.tile(q_norm_g, (1, N_Q_HEADS))[:, None, :], jnp.tile(k_norm_g, (1, N_KV_HEADS))[:, None, :],
        ones_bd,
        conv_w, conv_b[:, None, :], sgu_norm_g[:, None, :], sgu_w,
        jnp.repeat(jnp.swapaxes(sgu_b, 1, 2), HEAD_DIM, axis=2),
        ln1_g[:, None, :], ln1_b[:, None, :],
    )
    g2, b2 = ln2_g[:, None, :], ln2_b[:, None, :]
    kc = cache_k.reshape(dec_batch, DEPTH, past_len, KV_WIDTH)
    vc = cache_v.reshape(dec_batch, DEPTH, past_len, KV_WIDTH)

    x_ctx_in, x_lat_in, lat_block0 = x_prompt.reshape(n_ctx, D_MODEL), x_sample.reshape(n_lat, D_MODEL), 0
    assert moe_w1.shape[0] == 1
    w1_rows = moe_w1.reshape(-1, D_FF_EXPERT)
    w3_rows = moe_w3.reshape(-1, D_FF_EXPERT)
    w2_rows = moe_w2.reshape(-1, D_MODEL)
    kv = None
    for l in range(DEPTH):
        side = [("cast", w1_rows), ("even", w2_rows)] if l == 0 else [("cast", w3_rows), ("merge", w2_rows, w2_even)]
        x_ctx, k_ctx, v_ctx, w_in_bf, w_out_bf, cast_a, cast_b = _mixer_ctx(
            x_ctx_in, l, batch, seq, mod, w_in, w_out, small, kv, side)
        if l == 0:
            w1_bf, w2_even = cast_a, cast_b
        else:
            w3_bf, w2_bf = cast_a, cast_b
        kv = (k_ctx, v_ctx)
        x_lat = _mixer_lat(x_lat_in, l, dec_batch, dec_seq, mod, w_in_bf, w_out_bf, small, cos, sin, kc, vc, lat_block0)
        if l % 2 == 0:
            x_ctx_in = x_lat_in = _ffn_dense(x_ctx, x_lat, l, l // 2, mod, ffn_w1, ffn_w3, ffn_w2, g2, b2, dec_seq)
            lat_block0 = n_ctx // dec_seq
        else:
            ws = (w1_bf.reshape(N_EXPERTS, D_MODEL, D_FF_EXPERT), w3_bf.reshape(N_EXPERTS, D_MODEL, D_FF_EXPERT),
                  w2_bf.reshape(N_EXPERTS, D_FF_EXPERT, D_MODEL))
            x_ctx_in, x_lat_in = _ffn_moe(x_ctx, x_lat, l, mod, router_w[l // 2].T, *ws, g2, b2, dec_seq)
            lat_block0 = 0
    y_p = x_ctx_in.reshape(batch, seq, D_MODEL)
    y_s = x_lat_in.reshape(dec_batch, dec_seq, D_MODEL)
    new_k = kv[0].reshape(batch, DEPTH, seq, N_KV_HEADS, HEAD_DIM)
    new_v = kv[1].reshape(batch, DEPTH, seq, N_KV_HEADS, HEAD_DIM)
    return (y_p, y_s, new_k, new_v)
```

```python
import functools

import numpy as np
import jax
import jax.numpy as jnp
from jax import lax
from jax.experimental import pallas as pl
from jax.experimental.pallas import tpu as pltpu

F32 = jnp.float32
BF16 = jnp.bfloat16
I32 = jnp.int32

D_MODEL = 1024
DEPTH = 2
GRID_W = 64
HEAD_DIM = 64
N_Q_HEADS = 8
N_KV_HEADS = 2
ATTN_WIDTH = N_Q_HEADS * HEAD_DIM
KV_WIDTH = N_KV_HEADS * HEAD_DIM
ATTN_SCALE = HEAD_DIM ** -0.5
ROPE_THETA = 10000.0
AXIS_ROT = HEAD_DIM // 2
CONV_WIDTH = 256
SGU_WIDTH = 256
SGU_HEADS = 4
CHUNK = 128
IN_WIDTH = 2048
D_FF = 2816
N_EXPERTS = 8
D_FF_EXPERT = 1408
EPS = 1e-6
DEEPNORM_ALPHA = (2 * DEPTH) ** 0.25

LANES = 128
SUBLANES = 8
ROW_BLOCK = 512
GROUP_TILE = 256
ATT_BLOCK = 256
CTX_SEQS_PER_STEP = 2
FFN_TILE = 512
FFN_COLS = 256
MOE_CHUNK = 2048
ROUTER_BLOCK = 1024
MOE_GRAN = 128
MOE_BLOCK = 256
MOE_TOK_BLOCK = 512
MOE_TOK_STEPS = MOE_CHUNK // MOE_TOK_BLOCK
MOE_STEPS = 2 * MOE_TOK_STEPS + N_EXPERTS
MOE_ROWS = 2 * MOE_CHUNK + N_EXPERTS * MOE_GRAN
VMEM_LIMIT = 58 * 1024 * 1024

_Q0, _K0, _V0, _CI0, _CB0, _CC0, _SU0, _SV0 = 0, 512, 640, 768, 1024, 1280, 1536, 1792


def _dot(a, b):
    return jnp.dot(a, b, preferred_element_type=F32)


def _dot_nt(a, b):
    return lax.dot_general(a, b, (((1,), (1,)), ((), ())), preferred_element_type=F32)


def _split(x):
    hi = x.astype(BF16)
    lo = (x - hi.astype(F32)).astype(BF16)
    return hi, lo


def _group_sum(x, ones_bd):
    outs = []
    for c0 in range(0, x.shape[1], GROUP_TILE):
        width = min(GROUP_TILE, x.shape[1] - c0)
        outs.append(_dot(x[:, c0:c0 + width].astype(BF16), ones_bd[:width, :width]))
    return outs[0] if len(outs) == 1 else jnp.concatenate(outs, axis=1)


def _standardize(x):
    mu = jnp.mean(x, axis=-1, keepdims=True)
    d = x - mu
    return d * lax.rsqrt(jnp.mean(d * d, axis=-1, keepdims=True) + EPS)


def _silu(x):
    return x / (1.0 + jnp.exp(-x))


def _modulation_kernel(cond_ref, w_ref, b_ref, o_ref):
    s_hi, s_lo = _split(_silu(cond_ref[...]))
    w_hi, w_lo = _split(w_ref[...])
    o_ref[...] = _dot(s_hi, w_hi) + _dot(s_hi, w_lo) + _dot(s_lo, w_hi) + b_ref[...]


def _modulation(cond, ada_w, ada_b):
    n_out = ada_w.shape[-1]
    tn = 1536
    return pl.pallas_call(
        _modulation_kernel,
        grid=(DEPTH, n_out // tn),
        in_specs=[
            pl.BlockSpec((SUBLANES, D_MODEL), lambda l, j: (0, 0)),
            pl.BlockSpec((None, D_MODEL, tn), lambda l, j: (l, 0, j)),
            pl.BlockSpec((None, 1, tn), lambda l, j: (l, 0, j)),
        ],
        out_specs=pl.BlockSpec((None, SUBLANES, tn), lambda l, j: (l, 0, j)),
        out_shape=jax.ShapeDtypeStruct((DEPTH, SUBLANES, n_out), F32),
        compiler_params=pltpu.CompilerParams(
            dimension_semantics=("arbitrary", "arbitrary"), vmem_limit_bytes=VMEM_LIMIT),
        name="modulation",
    )(cond, ada_w, ada_b.reshape(DEPTH, 1, n_out))


def _rope(x, cos, sin_signed):
    w = x.shape[1]
    lane = lax.broadcasted_iota(I32, x.shape, 1)
    first_half = (lane & 31) < 16
    partner = jnp.where(first_half, pltpu.roll(x, w - 16, 1), pltpu.roll(x, 16, 1))
    return x * cos + partner * sin_signed


def _head_variants(x):
    lane = lax.broadcasted_iota(I32, x.shape, 1)
    lo = lane < HEAD_DIM
    xr = pltpu.roll(x, HEAD_DIM, 1)
    zero = jnp.zeros_like(x)
    return (jnp.where(lo, x, zero).astype(BF16), jnp.where(lo, zero, xr).astype(BF16),
            jnp.where(lo, xr, zero).astype(BF16), jnp.where(lo, zero, x).astype(BF16))


def _mixer_kernel(*refs, seq, n_seq, n_cache, rope, cast_weights, stack_kv, side_jobs=()):
    refs = list(refs)

    def take(n):
        out, refs[:] = refs[:n], refs[n:]
        return out

    x_ref, mod_ref, win_ref, wout_ref = take(4)
    qg_ref, kg_ref, ones_ref, convw_ref, convb_ref, sgug_ref, sguw_ref, sgub_ref, ln1g_ref, ln1b_ref = take(10)
    if rope:
        cos_ref, sin_ref, kc_ref, vc_ref = take(4)
    if stack_kv:
        kprev_ref, vprev_ref = take(2)
    side_in = [take(2 if job == "merge" else 1) for job in side_jobs]
    (x1_ref,) = take(1)
    if not rope:
        k_ref, v_ref = take(2)
    if cast_weights:
        winb_ref, woutb_ref = take(2)

        @pl.when(pl.program_id(0) == 0)
        def _cast():
            winb_ref[...] = win_ref[...].astype(BF16)
            woutb_ref[...] = wout_ref[...].astype(BF16)
        win_ref, wout_ref = winb_ref, woutb_ref
    for job, srcs, dst in zip(side_jobs, side_in, take(len(side_jobs))):
        if job == "merge":
            half = srcs[1].shape[0]
            dst[:half, :] = srcs[1][...]
            dst[half:, :] = srcs[0][...].astype(BF16)
        else:
            dst[...] = srcs[0][...].astype(BF16)
    q_scr, kvar_scr, vvar_scr, u_scr, cb_scr, su_scr, vn_scr, mix_scr = take(8)
    n_rows = n_seq * seq
    assert n_cache == 0 or n_seq == 1

    def loop(n, body):
        if n == 1:
            body(0)
        else:
            def step(r, carry):
                body(r)
                return carry
            lax.fori_loop(0, n, step, 0)

    def block(r, size):
        if isinstance(r, int):
            return pl.ds(r * size, size)
        return pl.ds(pl.multiple_of(r * size, size), size)

    if n_cache:
        for i, var in enumerate(_head_variants(kc_ref[...])):
            kvar_scr[i, pl.ds(seq, n_cache), :] = var
        for i, var in enumerate(_head_variants(vc_ref[...])):
            vvar_scr[i, pl.ds(seq, n_cache), :] = var

    def project(r):
        rows = block(r, ROW_BLOCK)
        x = x_ref[rows, :]
        h = _standardize(x) * (1.0 + mod_ref[1:2, :]) + mod_ref[0:1, :]
        z = _dot(h.astype(BF16), win_ref[...])
        ones_bd = ones_ref[...]
        zq = z[:, _Q0:_K0]
        q = zq * lax.rsqrt(_group_sum(zq * zq, ones_bd) * (1.0 / HEAD_DIM) + EPS) * qg_ref[...]
        zk = z[:, _K0:_V0]
        k = zk * lax.rsqrt(_group_sum(zk * zk, ones_bd) * (1.0 / HEAD_DIM) + EPS) * kg_ref[...]
        v = z[:, _V0:_CI0]
        if rope:
            cos = cos_ref[rows, :]
            sin = sin_ref[rows, :]
            q = _rope(q, jnp.concatenate([cos] * 4, axis=1), jnp.concatenate([sin] * 4, axis=1))
            k = _rope(k, cos, sin)
        elif stack_kv:
            for s in range(ROW_BLOCK // seq):
                sub = slice(s * seq, (s + 1) * seq)
                k_ref[s, 0] = kprev_ref[sub, :]
                v_ref[s, 0] = vprev_ref[sub, :]
                k_ref[s, 1] = k[sub, :]
                v_ref[s, 1] = v[sub, :]
        else:
            k_ref[rows, :] = k
            v_ref[rows, :] = v
        q_scr[rows, :] = (q * ATTN_SCALE).astype(BF16)
        for i, var in enumerate(_head_variants(k)):
            kvar_scr[i, rows, :] = var
        for i, var in enumerate(_head_variants(v)):
            vvar_scr[i, rows, :] = var
        u_scr[rows, :] = z[:, _CC0:_SU0] * z[:, _CI0:_CB0]
        cb_scr[rows, :] = z[:, _CB0:_CC0]
        su_scr[rows, :] = z[:, _SU0:_SV0]
        sv = z[:, _SV0:IN_WIDTH]
        d = sv - _group_sum(sv, ones_bd) * (1.0 / HEAD_DIM)
        vn = d * lax.rsqrt(_group_sum(d * d, ones_bd) * (1.0 / HEAD_DIM) + EPS) * sgug_ref[...]
        vn_scr[rows, :] = vn.astype(BF16)

    loop(n_rows // ROW_BLOCK, project)

    u = u_scr[...]
    pos = lax.broadcasted_iota(I32, u.shape, 0) & (seq - 1)
    up = jnp.where(pos == 0, 0.0, pltpu.roll(u, 1, 0))
    dn = jnp.where(pos == seq - 1, 0.0, pltpu.roll(u, n_rows - 1, 0))
    conv = up * convw_ref[0:1, :] + u * convw_ref[1:2, :] + dn * convw_ref[2:3, :] + convb_ref[...]
    mix_scr[:, ATTN_WIDTH:ATTN_WIDTH + CONV_WIDTH] = (cb_scr[...] * conv).astype(BF16)

    for n in range(n_rows // CHUNK):
        rows = pl.ds(n * CHUNK, CHUNK)
        vn = vn_scr[rows, :]
        lane = lax.broadcasted_iota(I32, vn.shape, 1)
        s = sgub_ref[...]
        for hd in range(SGU_HEADS):
            in_head = (lane >= hd * HEAD_DIM) & (lane < (hd + 1) * HEAD_DIM)
            masked = jnp.where(in_head, vn, jnp.zeros_like(vn))
            s = s + _dot(sguw_ref[hd].astype(BF16), masked)
        mix_scr[rows, ATTN_WIDTH + CONV_WIDTH:] = (su_scr[rows, :] * s).astype(BF16)

    def attend(s, r):
        rows = block(s * (seq // ATT_BLOCK) + r, ATT_BLOCK)
        keys = pl.ds(s * seq, seq + n_cache)
        for pair in range(N_Q_HEADS // 2):
            qp = q_scr[rows, pair * LANES:(pair + 1) * LANES]
            kv = pair // (N_Q_HEADS // N_KV_HEADS // 2)
            acc = jnp.zeros((ATT_BLOCK, LANES), F32)
            for parity in range(2):
                sc = _dot_nt(qp, kvar_scr[2 * kv + parity, keys, :])
                p = jnp.exp(sc - jnp.max(sc, axis=1, keepdims=True))
                denom = jnp.sum(p, axis=1, keepdims=True)
                acc = acc + _dot(p.astype(BF16), vvar_scr[2 * kv + parity, keys, :]) / denom
            mix_scr[rows, pair * LANES:(pair + 1) * LANES] = acc.astype(BF16)

    for s in range(n_seq):
        loop(seq // ATT_BLOCK, functools.partial(attend, s))

    def finish(r):
        rows = block(r, ROW_BLOCK)
        mix = _dot(mix_scr[rows, :], wout_ref[...])
        y = DEEPNORM_ALPHA * x_ref[rows, :] + mod_ref[2:3, :] * mix
        x1_ref[rows, :] = _standardize(y) * ln1g_ref[...] + ln1b_ref[...]

    loop(n_rows // ROW_BLOCK, finish)


def _full(shape):
    n = len(shape)
    return pl.BlockSpec(shape, lambda *_: (0,) * n)


def _resident(shape):
    n = len(shape)
    return pl.BlockSpec(shape, lambda *_: (0,) * n, pipeline_mode=pl.Buffered(1))


def _layer_block(arr, l, resident=False):
    shape = arr.shape[1:]
    kw = dict(pipeline_mode=pl.Buffered(1)) if resident else {}
    return pl.BlockSpec((None,) + shape, lambda *_: (l,) + (0,) * len(shape), **kw)


def _mixer_scratch(n_rows, n_cache):
    nk = n_rows + n_cache
    return [
        pltpu.VMEM((n_rows, ATTN_WIDTH), BF16),
        pltpu.VMEM((4, nk, LANES), BF16),
        pltpu.VMEM((4, nk, LANES), BF16),
        pltpu.VMEM((n_rows, CONV_WIDTH), F32),
        pltpu.VMEM((n_rows, CONV_WIDTH), F32),
        pltpu.VMEM((n_rows, SGU_WIDTH), F32),
        pltpu.VMEM((n_rows, SGU_WIDTH), BF16),
        pltpu.VMEM((n_rows, D_MODEL), BF16),
    ]


def _mixer_ctx(x, l, n_seq, seq, mod, w_in, w_out, small, kv_prev=None, side_casts=()):
    per_step = CTX_SEQS_PER_STEP
    rows = per_step * seq
    stack_kv = kv_prev is not None
    n_steps = n_seq // per_step
    kernel = functools.partial(_mixer_kernel, seq=seq, n_seq=per_step, n_cache=0, rope=False,
                               cast_weights=True, stack_kv=stack_kv,
                               side_jobs=tuple("merge" if job[0] == "merge" else "cast" for job in side_casts))
    in_specs = ([pl.BlockSpec((rows, D_MODEL), lambda i: (i, 0)),
                 pl.BlockSpec((None, None, SUBLANES, D_MODEL), lambda i: (l, 0, 0, 0)),
                 _layer_block(w_in, l, resident=True), _layer_block(w_out, l, resident=True)]
                + [_full(a.shape) if a.ndim == 2 else _layer_block(a, l) for a in small])
    args = [x, mod, w_in, w_out, *small]
    if stack_kv:
        in_specs += [pl.BlockSpec((rows, KV_WIDTH), lambda i: (i, 0))] * 2
        args += list(kv_prev)
        kv_spec = pl.BlockSpec((per_step, DEPTH, seq, KV_WIDTH), lambda i: (i, 0, 0, 0))
        kv_shape = jax.ShapeDtypeStruct((n_seq, DEPTH, seq, KV_WIDTH), F32)
    else:
        kv_spec = pl.BlockSpec((rows, KV_WIDTH), lambda i: (i, 0))
        kv_shape = jax.ShapeDtypeStruct((n_seq * seq, KV_WIDTH), F32)
    side_out_specs, side_out_shapes = [], []
    for kind, src, *earlier in side_casts:
        n_rows, n_cols = src.shape
        if kind == "cast":
            blk = n_rows // n_steps
            in_specs.append(pl.BlockSpec((blk, n_cols), lambda i: (i, 0)))
            args.append(src)
            out_rows, out_blk = n_rows, blk
        else:
            blk = n_rows // (2 * n_steps)
            odd = int(kind == "merge")
            in_specs.append(pl.BlockSpec((blk, n_cols), lambda i, odd=odd: (2 * i + odd, 0)))
            args.append(src)
            out_rows, out_blk = (n_rows, 2 * blk) if odd else (n_rows // 2, blk)
            if odd:
                in_specs.append(pl.BlockSpec((blk, n_cols), lambda i: (i, 0)))
                args.append(earlier[0])
        side_out_specs.append(pl.BlockSpec((out_blk, n_cols), lambda i: (i, 0)))
        side_out_shapes.append(jax.ShapeDtypeStruct((out_rows, n_cols), BF16))
    return pl.pallas_call(
        kernel,
        grid=(n_steps,),
        in_specs=in_specs,
        out_specs=[pl.BlockSpec((rows, D_MODEL), lambda i: (i, 0)), kv_spec, kv_spec,
                   _full((D_MODEL, IN_WIDTH)), _full((D_MODEL, D_MODEL))] + side_out_specs,
        out_shape=[jax.ShapeDtypeStruct((n_seq * seq, D_MODEL), F32), kv_shape, kv_shape,
                   jax.ShapeDtypeStruct((D_MODEL, IN_WIDTH), BF16), jax.ShapeDtypeStruct((D_MODEL, D_MODEL), BF16)]
        + side_out_shapes,
        scratch_shapes=_mixer_scratch(rows, 0),
        compiler_params=pltpu.CompilerParams(dimension_semantics=("arbitrary",), vmem_limit_bytes=VMEM_LIMIT),
        name="mixer_ctx",
    )(*args)


def _mixer_lat(x, l, n_seq, seq, mod, w_in_bf, w_out_bf, small, cos, sin, kc, vc):
    n_cache = kc.shape[2]
    kernel = functools.partial(_mixer_kernel, seq=seq, n_seq=1, n_cache=n_cache, rope=True,
                               cast_weights=False, stack_kv=False)
    cache_spec = pl.BlockSpec((None, None, n_cache, KV_WIDTH), lambda b: (b, l, 0, 0))
    return pl.pallas_call(
        kernel,
        grid=(n_seq,),
        in_specs=([pl.BlockSpec((seq, D_MODEL), lambda b: (b, 0)),
                   pl.BlockSpec((None, None, SUBLANES, D_MODEL), lambda b: (l, 1 + b, 0, 0)),
                   _resident((D_MODEL, IN_WIDTH)), _resident((D_MODEL, D_MODEL))]
                  + [_full(a.shape) if a.ndim == 2 else _layer_block(a, l) for a in small]
                  + [_full((seq, LANES)), _full((seq, LANES)), cache_spec, cache_spec]),
        out_specs=pl.BlockSpec((seq, D_MODEL), lambda b: (b, 0)),
        out_shape=jax.ShapeDtypeStruct((n_seq * seq, D_MODEL), F32),
        scratch_shapes=_mixer_scratch(seq, n_cache),
        compiler_params=pltpu.CompilerParams(dimension_semantics=("arbitrary",), vmem_limit_bytes=VMEM_LIMIT),
        name="mixer_lat",
    )(x, mod, w_in_bf, w_out_bf, *small, cos, sin, kc, vc)


def _two_stream_specs(block_rows, n_a, block_of, single_buffer_b=False):
    kw = dict(pipeline_mode=pl.Buffered(1)) if single_buffer_b else {}
    spec_a = pl.BlockSpec((block_rows, D_MODEL), lambda *ids: (jnp.minimum(block_of(*ids), n_a - 1), 0))
    spec_b = pl.BlockSpec((block_rows, D_MODEL), lambda *ids: (jnp.maximum(block_of(*ids) - n_a, 0), 0), **kw)
    return [spec_a, spec_b]


def _group_of_block(blk, block_rows, n_a, lat_seq):
    return jnp.where(blk < n_a, 0, 1 + (blk - n_a) // (lat_seq // block_rows))


def _ffn_kernel(xa_ref, xb_ref, mod_ref, w1_ref, w3_ref, w2_ref, g_ref, b_ref, oa_ref, ob_ref, *, n_a):
    is_a = pl.program_id(0) < n_a
    x = jnp.where(is_a, xa_ref[...], xb_ref[...])
    h = (_standardize(x) * (1.0 + mod_ref[4:5, :]) + mod_ref[3:4, :]).astype(BF16)
    acc = jnp.zeros(x.shape, F32)
    for c in range(D_FF // FFN_COLS):
        cols = slice(c * FFN_COLS, (c + 1) * FFN_COLS)
        a = _dot(h, w1_ref[:, cols].astype(BF16))
        b = _dot(h, w3_ref[:, cols].astype(BF16))
        acc = acc + _dot((_silu(a) * b).astype(BF16), w2_ref[cols, :].astype(BF16))
    y = DEEPNORM_ALPHA * x + mod_ref[5:6, :] * acc
    out = _standardize(y) * g_ref[...] + b_ref[...]

    @pl.when(is_a)
    def _():
        oa_ref[...] = out

    @pl.when(jnp.logical_not(is_a))
    def _():
        ob_ref[...] = out


def _ffn_dense(xa, xb, l, i, mods, w1, w3, w2, g, b, lat_seq):
    n_a, n_b = xa.shape[0] // FFN_TILE, xb.shape[0] // FFN_TILE
    x_specs = _two_stream_specs(FFN_TILE, n_a, lambda t: t)
    return pl.pallas_call(
        functools.partial(_ffn_kernel, n_a=n_a),
        grid=(n_a + n_b,),
        in_specs=x_specs + [
            pl.BlockSpec((None, None, SUBLANES, D_MODEL),
                         lambda t: (l, _group_of_block(t, FFN_TILE, n_a, lat_seq), 0, 0)),
            _layer_block(w1, i, resident=True), _layer_block(w3, i, resident=True),
            _layer_block(w2, i, resident=True), _layer_block(g, l), _layer_block(b, l)],
        out_specs=x_specs,
        out_shape=[jax.ShapeDtypeStruct(xa.shape, F32), jax.ShapeDtypeStruct(xb.shape, F32)],
        compiler_params=pltpu.CompilerParams(dimension_semantics=("arbitrary",), vmem_limit_bytes=VMEM_LIMIT),
        name="ffn_dense",
    )(xa, xb, mods, w1, w3, w2, g, b)


def _router_kernel(xa_ref, xb_ref, mod_ref, rw_ref, before_ref, dest_ref, gate_ref, start_ref, padded_ref, *,
                   n_a, lat_seq):
    c = pl.program_id(0)
    w_hi, w_lo = _split(rw_ref[...])
    n_blocks = MOE_CHUNK // ROUTER_BLOCK
    parts = []
    for blk in range(n_blocks):
        rows = pl.ds(blk * ROUTER_BLOCK, ROUTER_BLOCK)
        mod = mod_ref[_group_of_block(c * n_blocks + blk, ROUTER_BLOCK, n_a * n_blocks, lat_seq)]
        x = jnp.where(c < n_a, xa_ref[rows, :], xb_ref[rows, :])
        h = _standardize(x) * (1.0 + mod[4:5, :]) + mod[3:4, :]
        h_hi, h_lo = _split(h)
        parts.append(_dot_nt(w_hi, h_hi) + _dot_nt(w_hi, h_lo) + _dot_nt(w_lo, h_hi))
    logits = jnp.concatenate(parts, axis=1)
    eid = lax.broadcasted_iota(I32, logits.shape, 0).astype(F32)
    m1 = jnp.max(logits, axis=0, keepdims=True)
    i1 = jnp.min(jnp.where(logits == m1, eid, float(N_EXPERTS)), axis=0, keepdims=True)
    oh1 = eid == i1
    rest = jnp.where(oh1, -jnp.inf, logits)
    m2 = jnp.max(rest, axis=0, keepdims=True)
    i2 = jnp.min(jnp.where(rest == m2, eid, float(N_EXPERTS)), axis=0, keepdims=True)
    oh2 = eid == i2
    e = jnp.exp(m2 - m1)
    gate_ref[0:1, :] = 1.0 / (1.0 + e)
    gate_ref[1:2, :] = e / (1.0 + e)
    sel = jnp.where(oh1 | oh2, 1.0, 0.0)
    ranks = []
    seen = jnp.zeros((N_EXPERTS, 1), F32)
    for blk in range(n_blocks):
        s_blk = sel[:, blk * ROUTER_BLOCK:(blk + 1) * ROUTER_BLOCK]
        ranks.append(_dot(s_blk.astype(BF16), before_ref[...]) + seen)
        seen = seen + jnp.sum(s_blk, axis=1, keepdims=True)
    rank = jnp.concatenate(ranks, axis=1)
    eid_out = lax.broadcasted_iota(I32, start_ref.shape, 0).astype(F32)
    start = jnp.zeros(sel.shape, F32)
    start_out = jnp.zeros(start_ref.shape, F32)
    padded_out = jnp.zeros(start_ref.shape, F32)
    for ex in range(N_EXPERTS):
        cnt = jnp.sum(sel[ex:ex + 1, :], axis=1, keepdims=True)
        padded = jnp.ceil(cnt * (1.0 / MOE_GRAN)) * MOE_GRAN
        start = start + jnp.where(eid > ex, padded, 0.0)
        start_out = start_out + jnp.where(eid_out > ex, padded, 0.0)
        padded_out = padded_out + jnp.where(eid_out == ex, padded, 0.0)
    row = (start + rank) * SUBLANES
    dest_ref[0:1, :] = jnp.sum(jnp.where(oh1, row, 0.0), axis=0, keepdims=True).astype(I32)
    dest_ref[1:2, :] = jnp.sum(jnp.where(oh2, row, 0.0), axis=0, keepdims=True).astype(I32)
    start_ref[...] = start_out.astype(I32)
    padded_ref[...] = padded_out.astype(I32)


def _router(xa, xb, l, mods, rw_t, lat_seq):
    n_a = xa.shape[0] // MOE_CHUNK
    n_chunks = n_a + xb.shape[0] // MOE_CHUNK
    tok = np.arange(ROUTER_BLOCK)
    before = jnp.asarray(tok[:, None] < tok[None, :], BF16)
    return pl.pallas_call(
        functools.partial(_router_kernel, n_a=n_a, lat_seq=lat_seq),
        grid=(n_chunks,),
        in_specs=_two_stream_specs(MOE_CHUNK, n_a, lambda c: c) + [
                  _layer_block(mods, l),
                  _full((N_EXPERTS, D_MODEL)), _full((ROUTER_BLOCK, ROUTER_BLOCK))],
        out_specs=[pl.BlockSpec((None, 2, MOE_CHUNK), lambda c: (c, 0, 0)),
                   pl.BlockSpec((None, 2, MOE_CHUNK), lambda c: (c, 0, 0)),
                   pl.BlockSpec((None, N_EXPERTS, LANES), lambda c: (c, 0, 0)),
                   pl.BlockSpec((None, N_EXPERTS, LANES), lambda c: (c, 0, 0))],
        out_shape=[jax.ShapeDtypeStruct((n_chunks, 2, MOE_CHUNK), I32),
                   jax.ShapeDtypeStruct((n_chunks, 2, MOE_CHUNK), F32),
                   jax.ShapeDtypeStruct((n_chunks, N_EXPERTS, LANES), I32),
                   jax.ShapeDtypeStruct((n_chunks, N_EXPERTS, LANES), I32)],
        compiler_params=pltpu.CompilerParams(dimension_semantics=("arbitrary",), vmem_limit_bytes=VMEM_LIMIT),
        name="moe_router",
    )(xa, xb, mods, rw_t, before)


def _token_rows(first_row):
    return pl.ds(pl.multiple_of(first_row, SUBLANES), SUBLANES)


def _store_token_major(ref, tok0, val):
    for cc in range(D_MODEL // LANES):
        ref[pl.ds(tok0 * SUBLANES + cc, val.shape[0], stride=SUBLANES), :] = val[:, cc * LANES:(cc + 1) * LANES]


def _load_token_major(ref, tok0, n):
    return jnp.concatenate(
        [ref[pl.ds(tok0 * SUBLANES + cc, n, stride=SUBLANES), :] for cc in range(D_MODEL // LANES)], axis=1)


def _moe_kernel(start_ref, padded_ref, xa_ref, xb_ref, mod_ref, dest_ref, gate_ref, w1_ref, w3_ref, w2_ref,
                g_ref, b_ref, oa_ref, ob_ref, tok_scr, rows_scr, *, n_a):
    c = pl.program_id(0)
    j = pl.program_id(1)
    is_a = c < n_a
    first_expert_step = MOE_TOK_STEPS
    first_combine_step = MOE_TOK_STEPS + N_EXPERTS

    @pl.when((c == 0) & (j == 0))
    def _init():
        rows_scr[...] = jnp.zeros(rows_scr.shape, F32)

    @pl.when(j < first_expert_step)
    def _dispatch():
        for r in range(MOE_TOK_BLOCK // ROW_BLOCK):
            rows = pl.ds(r * ROW_BLOCK, ROW_BLOCK)
            x = jnp.where(is_a, xa_ref[rows, :], xb_ref[rows, :])
            h = _standardize(x) * (1.0 + mod_ref[4:5, :]) + mod_ref[3:4, :]
            _store_token_major(tok_scr, r * ROW_BLOCK, h)
        t0 = j * MOE_TOK_BLOCK

        def body(t, carry):
            row = tok_scr[_token_rows(t * SUBLANES), :]
            rows_scr[_token_rows(dest_ref[t0 + t]), :] = row
            rows_scr[_token_rows(dest_ref[MOE_CHUNK + t0 + t]), :] = row
            return carry
        lax.fori_loop(0, MOE_TOK_BLOCK, body, 0, unroll=8)

    def experts(row0, m):
        xin = _load_token_major(rows_scr, row0, m).astype(BF16)
        a = _dot(xin, w1_ref[...])
        b = _dot(xin, w3_ref[...])
        y = _dot((_silu(a) * b).astype(BF16), w2_ref[...])
        _store_token_major(rows_scr, row0, y)

    @pl.when((j >= first_expert_step) & (j < first_combine_step))
    def _experts():
        region = c * N_EXPERTS + (j - first_expert_step)
        start = start_ref[region]
        padded = padded_ref[region]
        n_full = lax.shift_right_logical(padded, MOE_BLOCK.bit_length() - 1)

        def body(i, carry):
            experts(start + i * MOE_BLOCK, MOE_BLOCK)
            return carry
        lax.fori_loop(0, n_full, body, 0)
        for m in range(MOE_GRAN, MOE_BLOCK, MOE_GRAN):
            pl.when(padded - n_full * MOE_BLOCK == m)(
                functools.partial(experts, start + n_full * MOE_BLOCK, m))

    @pl.when(j >= first_combine_step)
    def _combine():
        t0 = (j - first_combine_step) * MOE_TOK_BLOCK

        def body(t, carry):
            y0 = rows_scr[_token_rows(dest_ref[t0 + t]), :]
            y1 = rows_scr[_token_rows(dest_ref[MOE_CHUNK + t0 + t]), :]
            tok_scr[_token_rows(t * SUBLANES), :] = gate_ref[t0 + t] * y0 + gate_ref[MOE_CHUNK + t0 + t] * y1
            return carry
        lax.fori_loop(0, MOE_TOK_BLOCK, body, 0, unroll=8)
        for r in range(MOE_TOK_BLOCK // ROW_BLOCK):
            rows = pl.ds(r * ROW_BLOCK, ROW_BLOCK)
            ffn = _load_token_major(tok_scr, r * ROW_BLOCK, ROW_BLOCK)
            x = jnp.where(is_a, xa_ref[rows, :], xb_ref[rows, :])
            y = DEEPNORM_ALPHA * x + mod_ref[5:6, :] * ffn
            out = _standardize(y) * g_ref[...] + b_ref[...]

            @pl.when(is_a)
            def _():
                oa_ref[rows, :] = out

            @pl.when(jnp.logical_not(is_a))
            def _():
                ob_ref[rows, :] = out


def _ffn_moe(xa, xb, l, mods, rw_t, w1, w3, w2, g, b, lat_seq):
    n_a = xa.shape[0] // MOE_CHUNK
    n_chunks = n_a + xb.shape[0] // MOE_CHUNK
    n_a_blocks = n_a * MOE_TOK_STEPS
    dest, gates, start, padded = _router(xa, xb, l, mods, rw_t, lat_seq)
    first_expert_step = MOE_TOK_STEPS
    first_combine_step = MOE_TOK_STEPS + N_EXPERTS

    def token_block(c, j, *_):
        blk = jnp.where(j < first_combine_step, jnp.minimum(j, MOE_TOK_STEPS - 1), j - first_combine_step)
        return c * MOE_TOK_STEPS + blk

    def out_token_block(c, j, *_):
        return c * MOE_TOK_STEPS + jnp.maximum(j - first_combine_step, 0)

    def mod_of(c, j, *_):
        return (l, _group_of_block(token_block(c, j), MOE_TOK_BLOCK, n_a_blocks, lat_seq), 0, 0)

    def expert_of(c, j, *_):
        return (jnp.clip(j - first_expert_step, 0, N_EXPERTS - 1), 0, 0)

    grid_spec = pltpu.PrefetchScalarGridSpec(
        num_scalar_prefetch=2,
        grid=(n_chunks, MOE_STEPS),
        in_specs=_two_stream_specs(MOE_TOK_BLOCK, n_a_blocks, token_block) + [
                  pl.BlockSpec((None, None, SUBLANES, D_MODEL), mod_of),
                  pl.BlockSpec((2 * MOE_CHUNK,), lambda c, j, *_: (c,), memory_space=pltpu.SMEM),
                  pl.BlockSpec((2 * MOE_CHUNK,), lambda c, j, *_: (c,), memory_space=pltpu.SMEM),
                  pl.BlockSpec((None, D_MODEL, D_FF_EXPERT), expert_of),
                  pl.BlockSpec((None, D_MODEL, D_FF_EXPERT), expert_of),
                  pl.BlockSpec((None, D_FF_EXPERT, D_MODEL), expert_of),
                  _layer_block(g, l), _layer_block(b, l)],
        out_specs=_two_stream_specs(MOE_TOK_BLOCK, n_a_blocks, out_token_block),
        scratch_shapes=[pltpu.VMEM((MOE_TOK_BLOCK * SUBLANES, LANES), F32),
                        pltpu.VMEM((MOE_ROWS * SUBLANES, LANES), F32)],
    )
    return pl.pallas_call(
        functools.partial(_moe_kernel, n_a=n_a),
        grid_spec=grid_spec,
        out_shape=[jax.ShapeDtypeStruct(xa.shape, F32), jax.ShapeDtypeStruct(xb.shape, F32)],
        compiler_params=pltpu.CompilerParams(
            dimension_semantics=("arbitrary", "arbitrary"), vmem_limit_bytes=VMEM_LIMIT),
        name="moe_experts",
    )(start[:, :, 0].reshape(-1), padded[:, :, 0].reshape(-1), xa, xb, mods,
      dest.reshape(-1), gates.reshape(-1), w1, w3, w2, g, b)


def _rope_tables(n_tokens):
    t = np.arange(n_tokens)
    row = (t // GRID_W).astype(np.float32)
    col = (t % GRID_W).astype(np.float32)
    inv_freq = (np.float32(ROPE_THETA) ** (-np.arange(0, AXIS_ROT, 2, dtype=np.float32) / AXIS_ROT)).astype(np.float32)
    ang_r = row[:, None] * inv_freq
    ang_c = col[:, None] * inv_freq
    cos = np.concatenate([np.cos(ang_r), np.cos(ang_r), np.cos(ang_c), np.cos(ang_c)], axis=1)
    sin = np.concatenate([-np.sin(ang_r), np.sin(ang_r), -np.sin(ang_c), np.sin(ang_c)], axis=1)
    return jnp.asarray(np.tile(cos, (1, 2)), F32), jnp.asarray(np.tile(sin, (1, 2)), F32)


def kernel(x_prompt, x_sample, cache_k, cache_v, c, c_ctx, ada_w, ada_b, w_in, q_norm_g, k_norm_g, conv_w, conv_b, sgu_norm_g, sgu_w, sgu_b, w_out, ln1_g, ln1_b, ln2_g, ln2_b, ffn_w1, ffn_w3, ffn_w2, router_w, moe_w1, moe_w3, moe_w2):
    batch, seq, _ = x_prompt.shape
    dec_batch, dec_seq, _ = x_sample.shape
    past_len = cache_k.shape[2]
    n_ctx = batch * seq
    n_lat = dec_batch * dec_seq
    assert DEPTH == 2 and 1 + dec_batch <= SUBLANES
    assert seq == ATT_BLOCK and CTX_SEQS_PER_STEP * seq == ROW_BLOCK and batch % CTX_SEQS_PER_STEP == 0
    assert dec_seq % ROW_BLOCK == 0 and dec_seq & (dec_seq - 1) == 0
    assert n_ctx % MOE_CHUNK == 0 and n_lat % MOE_CHUNK == 0
    assert dec_seq % ROUTER_BLOCK == 0 and dec_seq % MOE_TOK_BLOCK == 0 and dec_seq % FFN_TILE == 0

    cond = jnp.zeros((SUBLANES, D_MODEL), F32).at[0].set(c_ctx).at[1:1 + dec_batch].set(c)
    mod = _modulation(cond, ada_w, ada_b)
    mod = mod.reshape(DEPTH, SUBLANES, 6, D_MODEL)[:, :1 + dec_batch]
    mod = jnp.pad(mod, ((0, 0), (0, 0), (0, SUBLANES - 6), (0, 0)))

    lane_id = np.arange(GROUP_TILE) // HEAD_DIM
    ones_bd = jnp.asarray(lane_id[:, None] == lane_id[None, :], BF16)
    cos, sin = _rope_tables(dec_seq)
    small = (
        jnp.tile(q_norm_g, (1, N_Q_HEADS))[:, None, :], jnp.tile(k_norm_g, (1, N_KV_HEADS))[:, None, :],
        ones_bd,
        conv_w, conv_b[:, None, :], sgu_norm_g[:, None, :], sgu_w,
        jnp.repeat(jnp.swapaxes(sgu_b, 1, 2), HEAD_DIM, axis=2),
        ln1_g[:, None, :], ln1_b[:, None, :],
    )
    g2, b2 = ln2_g[:, None, :], ln2_b[:, None, :]
    kc = cache_k.reshape(dec_batch, DEPTH, past_len, KV_WIDTH)
    vc = cache_v.reshape(dec_batch, DEPTH, past_len, KV_WIDTH)

    xs = [x_prompt.reshape(n_ctx, D_MODEL), x_sample.reshape(n_lat, D_MODEL)]
    assert moe_w1.shape[0] == 1
    w1_rows = moe_w1.reshape(-1, D_FF_EXPERT)
    w3_rows = moe_w3.reshape(-1, D_FF_EXPERT)
    w2_rows = moe_w2.reshape(-1, D_MODEL)
    kv = None
    for l in range(DEPTH):
        side = [("cast", w1_rows), ("even", w2_rows)] if l == 0 else [("cast", w3_rows), ("merge", w2_rows, w2_even)]
        x_ctx, k_ctx, v_ctx, w_in_bf, w_out_bf, cast_a, cast_b = _mixer_ctx(
            xs[0], l, batch, seq, mod, w_in, w_out, small, kv, side)
        if l == 0:
            w1_bf, w2_even = cast_a, cast_b
        else:
            w3_bf, w2_bf = cast_a, cast_b
        kv = (k_ctx, v_ctx)
        x_lat = _mixer_lat(xs[1], l, dec_batch, dec_seq, mod, w_in_bf, w_out_bf, small, cos, sin, kc, vc)
        if l % 2 == 0:
            xs = _ffn_dense(x_ctx, x_lat, l, l // 2, mod, ffn_w1, ffn_w3, ffn_w2, g2, b2, dec_seq)
        else:
            ws = (w1_bf.reshape(N_EXPERTS, D_MODEL, D_FF_EXPERT), w3_bf.reshape(N_EXPERTS, D_MODEL, D_FF_EXPERT),
                  w2_bf.reshape(N_EXPERTS, D_FF_EXPERT, D_MODEL))
            xs = _ffn_moe(x_ctx, x_lat, l, mod, router_w[l // 2].T, *ws, g2, b2, dec_seq)
    y_p = xs[0].reshape(batch, seq, D_MODEL)
    y_s = xs[1].reshape(dec_batch, dec_seq, D_MODEL)
    new_k = kv[0].reshape(batch, DEPTH, seq, N_KV_HEADS, HEAD_DIM)
    new_v = kv[1].reshape(batch, DEPTH, seq, N_KV_HEADS, HEAD_DIM)
    return (y_p, y_s, new_k, new_v)
```

```python
import functools

import numpy as np
import jax
import jax.numpy as jnp
from jax import lax
from jax.experimental import pallas as pl
from jax.experimental.pallas import tpu as pltpu

F32 = jnp.float32
BF16 = jnp.bfloat16
I32 = jnp.int32

D_MODEL = 1024
DEPTH = 2
GRID_W = 64
HEAD_DIM = 64
N_Q_HEADS = 8
N_KV_HEADS = 2
ATTN_WIDTH = N_Q_HEADS * HEAD_DIM
KV_WIDTH = N_KV_HEADS * HEAD_DIM
ATTN_SCALE = HEAD_DIM ** -0.5
ROPE_THETA = 10000.0
AXIS_ROT = HEAD_DIM // 2
CONV_WIDTH = 256
SGU_WIDTH = 256
SGU_HEADS = 4
CHUNK = 128
IN_WIDTH = 2048
D_FF = 2816
N_EXPERTS = 8
D_FF_EXPERT = 1408
EPS = 1e-6
DEEPNORM_ALPHA = (2 * DEPTH) ** 0.25

LANES = 128
SUBLANES = 8
ROW_BLOCK = 512
GROUP_TILE = 256
ATT_BLOCK = 256
CTX_SEQS_PER_STEP = 2
FFN_TILE = 512
FFN_COLS = 256
MOE_CHUNK = 2048
ROUTER_BLOCK = 1024
MOE_GRAN = 128
MOE_BLOCK = 256
MOE_TOK_BLOCK = 512
MOE_TOK_STEPS = MOE_CHUNK // MOE_TOK_BLOCK
MOE_STEPS = 2 * MOE_TOK_STEPS + N_EXPERTS
MOE_ROWS = 2 * MOE_CHUNK + N_EXPERTS * MOE_GRAN
VMEM_LIMIT = 58 * 1024 * 1024

_Q0, _K0, _V0, _CI0, _CB0, _CC0, _SU0, _SV0 = 0, 512, 640, 768, 1024, 1280, 1536, 1792


def _dot(a, b):
    return jnp.dot(a, b, preferred_element_type=F32)


def _dot_nt(a, b):
    return lax.dot_general(a, b, (((1,), (1,)), ((), ())), preferred_element_type=F32)


def _split(x):
    hi = x.astype(BF16)
    lo = (x - hi.astype(F32)).astype(BF16)
    return hi, lo


def _group_sum(x, ones_bd):
    outs = []
    for c0 in range(0, x.shape[1], GROUP_TILE):
        width = min(GROUP_TILE, x.shape[1] - c0)
        outs.append(_dot(x[:, c0:c0 + width].astype(BF16), ones_bd[:width, :width]))
    return outs[0] if len(outs) == 1 else jnp.concatenate(outs, axis=1)


def _standardize(x):
    mu = jnp.mean(x, axis=-1, keepdims=True)
    d = x - mu
    return d * lax.rsqrt(jnp.mean(d * d, axis=-1, keepdims=True) + EPS)


def _silu(x):
    return x / (1.0 + jnp.exp(-x))


def _modulation_kernel(cond_ref, w_ref, b_ref, o_ref):
    s_hi, s_lo = _split(_silu(cond_ref[...]))
    w_hi, w_lo = _split(w_ref[...])
    o_ref[...] = _dot(s_hi, w_hi) + _dot(s_hi, w_lo) + _dot(s_lo, w_hi) + b_ref[...]


def _modulation(cond, ada_w, ada_b):
    n_out = ada_w.shape[-1]
    tn = 1536
    return pl.pallas_call(
        _modulation_kernel,
        grid=(DEPTH, n_out // tn),
        in_specs=[
            pl.BlockSpec((SUBLANES, D_MODEL), lambda l, j: (0, 0)),
            pl.BlockSpec((None, D_MODEL, tn), lambda l, j: (l, 0, j)),
            pl.BlockSpec((None, 1, tn), lambda l, j: (l, 0, j)),
        ],
        out_specs=pl.BlockSpec((None, SUBLANES, tn), lambda l, j: (l, 0, j)),
        out_shape=jax.ShapeDtypeStruct((DEPTH, SUBLANES, n_out), F32),
        compiler_params=pltpu.CompilerParams(
            dimension_semantics=("arbitrary", "arbitrary"), vmem_limit_bytes=VMEM_LIMIT),
        name="modulation",
    )(cond, ada_w, ada_b.reshape(DEPTH, 1, n_out))


def _rope(x, cos, sin_signed):
    w = x.shape[1]
    lane = lax.broadcasted_iota(I32, x.shape, 1)
    first_half = (lane & 31) < 16
    partner = jnp.where(first_half, pltpu.roll(x, w - 16, 1), pltpu.roll(x, 16, 1))
    return x * cos + partner * sin_signed


def _head_variants(x):
    lane = lax.broadcasted_iota(I32, x.shape, 1)
    lo = lane < HEAD_DIM
    xr = pltpu.roll(x, HEAD_DIM, 1)
    zero = jnp.zeros_like(x)
    return (jnp.where(lo, x, zero).astype(BF16), jnp.where(lo, zero, xr).astype(BF16),
            jnp.where(lo, xr, zero).astype(BF16), jnp.where(lo, zero, x).astype(BF16))


def _mixer_kernel(*refs, seq, n_seq, n_cache, rope, cast_weights, stack_kv, side_jobs=()):
    refs = list(refs)

    def take(n):
        out, refs[:] = refs[:n], refs[n:]
        return out

    x_ref, mod_ref, win_ref, wout_ref = take(4)
    qg_ref, kg_ref, ones_ref, convw_ref, convb_ref, sgug_ref, sguw_ref, sgub_ref, ln1g_ref, ln1b_ref = take(10)
    if rope:
        cos_ref, sin_ref, kc_ref, vc_ref = take(4)
    if stack_kv:
        kprev_ref, vprev_ref = take(2)
    side_in = [take(2 if job == "merge" else 1) for job in side_jobs]
    (x1_ref,) = take(1)
    if not rope:
        k_ref, v_ref = take(2)
    if cast_weights:
        winb_ref, woutb_ref = take(2)

        @pl.when(pl.program_id(0) == 0)
        def _cast():
            winb_ref[...] = win_ref[...].astype(BF16)
            woutb_ref[...] = wout_ref[...].astype(BF16)
        win_ref, wout_ref = winb_ref, woutb_ref
    for job, srcs, dst in zip(side_jobs, side_in, take(len(side_jobs))):
        if job == "merge":
            half = srcs[1].shape[0]
            dst[:half, :] = srcs[1][...]
            dst[half:, :] = srcs[0][...].astype(BF16)
        else:
            dst[...] = srcs[0][...].astype(BF16)
    q_scr, kvar_scr, vvar_scr, u_scr, cb_scr, su_scr, vn_scr, mix_scr = take(8)
    n_rows = n_seq * seq
    assert n_cache == 0 or n_seq == 1

    def loop(n, body):
        if n == 1:
            body(0)
        else:
            def step(r, carry):
                body(r)
                return carry
            lax.fori_loop(0, n, step, 0)

    def block(r, size):
        if isinstance(r, int):
            return pl.ds(r * size, size)
        return pl.ds(pl.multiple_of(r * size, size), size)

    if n_cache:
        for i, var in enumerate(_head_variants(kc_ref[...])):
            kvar_scr[i, pl.ds(seq, n_cache), :] = var
        for i, var in enumerate(_head_variants(vc_ref[...])):
            vvar_scr[i, pl.ds(seq, n_cache), :] = var

    def project(r):
        rows = block(r, ROW_BLOCK)
        x = x_ref[rows, :]
        h = _standardize(x) * (1.0 + mod_ref[1:2, :]) + mod_ref[0:1, :]
        z = _dot(h.astype(BF16), win_ref[...])
        ones_bd = ones_ref[...]
        zq = z[:, _Q0:_K0]
        q = zq * lax.rsqrt(_group_sum(zq * zq, ones_bd) * (1.0 / HEAD_DIM) + EPS) * qg_ref[...]
        zk = z[:, _K0:_V0]
        k = zk * lax.rsqrt(_group_sum(zk * zk, ones_bd) * (1.0 / HEAD_DIM) + EPS) * kg_ref[...]
        v = z[:, _V0:_CI0]
        if rope:
            cos = cos_ref[rows, :]
            sin = sin_ref[rows, :]
            q = _rope(q, jnp.concatenate([cos] * 4, axis=1), jnp.concatenate([sin] * 4, axis=1))
            k = _rope(k, cos, sin)
        elif stack_kv:
            for s in range(ROW_BLOCK // seq):
                sub = slice(s * seq, (s + 1) * seq)
                k_ref[s, 0] = kprev_ref[sub, :]
                v_ref[s, 0] = vprev_ref[sub, :]
                k_ref[s, 1] = k[sub, :]
                v_ref[s, 1] = v[sub, :]
        else:
            k_ref[rows, :] = k
            v_ref[rows, :] = v
        q_scr[rows, :] = (q * ATTN_SCALE).astype(BF16)
        for i, var in enumerate(_head_variants(k)):
            kvar_scr[i, rows, :] = var
        for i, var in enumerate(_head_variants(v)):
            vvar_scr[i, rows, :] = var
        u_scr[rows, :] = z[:, _CC0:_SU0] * z[:, _CI0:_CB0]
        cb_scr[rows, :] = z[:, _CB0:_CC0]
        su_scr[rows, :] = z[:, _SU0:_SV0]
        sv = z[:, _SV0:IN_WIDTH]
        d = sv - _group_sum(sv, ones_bd) * (1.0 / HEAD_DIM)
        vn = d * lax.rsqrt(_group_sum(d * d, ones_bd) * (1.0 / HEAD_DIM) + EPS) * sgug_ref[...]
        vn_scr[rows, :] = vn.astype(BF16)

    loop(n_rows // ROW_BLOCK, project)

    u = u_scr[...]
    pos = lax.broadcasted_iota(I32, u.shape, 0) & (seq - 1)
    up = jnp.where(pos == 0, 0.0, pltpu.roll(u, 1, 0))
    dn = jnp.where(pos == seq - 1, 0.0, pltpu.roll(u, n_rows - 1, 0))
    conv = up * convw_ref[0:1, :] + u * convw_ref[1:2, :] + dn * convw_ref[2:3, :] + convb_ref[...]
    mix_scr[:, ATTN_WIDTH:ATTN_WIDTH + CONV_WIDTH] = (cb_scr[...] * conv).astype(BF16)

    for n in range(n_rows // CHUNK):
        rows = pl.ds(n * CHUNK, CHUNK)
        vn = vn_scr[rows, :]
        lane = lax.broadcasted_iota(I32, vn.shape, 1)
        s = sgub_ref[...]
        for hd in range(SGU_HEADS):
            in_head = (lane >= hd * HEAD_DIM) & (lane < (hd + 1) * HEAD_DIM)
            masked = jnp.where(in_head, vn, jnp.zeros_like(vn))
            s = s + _dot(sguw_ref[hd].astype(BF16), masked)
        mix_scr[rows, ATTN_WIDTH + CONV_WIDTH:] = (su_scr[rows, :] * s).astype(BF16)

    def attend(s, r):
        rows = block(s * (seq // ATT_BLOCK) + r, ATT_BLOCK)
        keys = pl.ds(s * seq, seq + n_cache)
        for pair in range(N_Q_HEADS // 2):
            qp = q_scr[rows, pair * LANES:(pair + 1) * LANES]
            kv = pair // (N_Q_HEADS // N_KV_HEADS // 2)
            acc = jnp.zeros((ATT_BLOCK, LANES), F32)
            for parity in range(2):
                sc = _dot_nt(qp, kvar_scr[2 * kv + parity, keys, :])
                p = jnp.exp(sc - jnp.max(sc, axis=1, keepdims=True))
                denom = jnp.sum(p, axis=1, keepdims=True)
                acc = acc + _dot(p.astype(BF16), vvar_scr[2 * kv + parity, keys, :]) / denom
            mix_scr[rows, pair * LANES:(pair + 1) * LANES] = acc.astype(BF16)

    for s in range(n_seq):
        loop(seq // ATT_BLOCK, functools.partial(attend, s))

    def finish(r):
        rows = block(r, ROW_BLOCK)
        mix = _dot(mix_scr[rows, :], wout_ref[...])
        y = DEEPNORM_ALPHA * x_ref[rows, :] + mod_ref[2:3, :] * mix
        x1_ref[rows, :] = _standardize(y) * ln1g_ref[...] + ln1b_ref[...]

    loop(n_rows // ROW_BLOCK, finish)


def _full(shape):
    n = len(shape)
    return pl.BlockSpec(shape, lambda *_: (0,) * n)


def _resident(shape):
    n = len(shape)
    return pl.BlockSpec(shape, lambda *_: (0,) * n, pipeline_mode=pl.Buffered(1))


def _layer_block(arr, l, resident=False):
    shape = arr.shape[1:]
    kw = dict(pipeline_mode=pl.Buffered(1)) if resident else {}
    return pl.BlockSpec((None,) + shape, lambda *_: (l,) + (0,) * len(shape), **kw)


def _mixer_scratch(n_rows, n_cache):
    nk = n_rows + n_cache
    return [
        pltpu.VMEM((n_rows, ATTN_WIDTH), BF16),
        pltpu.VMEM((4, nk, LANES), BF16),
        pltpu.VMEM((4, nk, LANES), BF16),
        pltpu.VMEM((n_rows, CONV_WIDTH), F32),
        pltpu.VMEM((n_rows, CONV_WIDTH), F32),
        pltpu.VMEM((n_rows, SGU_WIDTH), F32),
        pltpu.VMEM((n_rows, SGU_WIDTH), BF16),
        pltpu.VMEM((n_rows, D_MODEL), BF16),
    ]


def _mixer_ctx(x, l, n_seq, seq, mod, w_in, w_out, small, kv_prev=None, side_casts=()):
    per_step = CTX_SEQS_PER_STEP
    rows = per_step * seq
    stack_kv = kv_prev is not None
    n_steps = n_seq // per_step
    kernel = functools.partial(_mixer_kernel, seq=seq, n_seq=per_step, n_cache=0, rope=False,
                               cast_weights=True, stack_kv=stack_kv,
                               side_jobs=tuple("merge" if job[0] == "merge" else "cast" for job in side_casts))
    in_specs = ([pl.BlockSpec((rows, D_MODEL), lambda i: (i, 0)),
                 pl.BlockSpec((None, None, SUBLANES, D_MODEL), lambda i: (l, 0, 0, 0)),
                 _layer_block(w_in, l, resident=True), _layer_block(w_out, l, resident=True)]
                + [_full(a.shape) if a.ndim == 2 else _layer_block(a, l) for a in small])
    args = [x, mod, w_in, w_out, *small]
    if stack_kv:
        in_specs += [pl.BlockSpec((rows, KV_WIDTH), lambda i: (i, 0))] * 2
        args += list(kv_prev)
        kv_spec = pl.BlockSpec((per_step, DEPTH, seq, KV_WIDTH), lambda i: (i, 0, 0, 0))
        kv_shape = jax.ShapeDtypeStruct((n_seq, DEPTH, seq, KV_WIDTH), F32)
    else:
        kv_spec = pl.BlockSpec((rows, KV_WIDTH), lambda i: (i, 0))
        kv_shape = jax.ShapeDtypeStruct((n_seq * seq, KV_WIDTH), F32)
    side_out_specs, side_out_shapes = [], []
    for kind, src, *earlier in side_casts:
        n_rows, n_cols = src.shape
        if kind == "cast":
            blk = n_rows // n_steps
            in_specs.append(pl.BlockSpec((blk, n_cols), lambda i: (i, 0)))
            args.append(src)
            out_rows, out_blk = n_rows, blk
        else:
            blk = n_rows // (2 * n_steps)
            odd = int(kind == "merge")
            in_specs.append(pl.BlockSpec((blk, n_cols), lambda i, odd=odd: (2 * i + odd, 0)))
            args.append(src)
            out_rows, out_blk = (n_rows, 2 * blk) if odd else (n_rows // 2, blk)
            if odd:
                in_specs.append(pl.BlockSpec((blk, n_cols), lambda i: (i, 0)))
                args.append(earlier[0])
        side_out_specs.append(pl.BlockSpec((out_blk, n_cols), lambda i: (i, 0)))
        side_out_shapes.append(jax.ShapeDtypeStruct((out_rows, n_cols), BF16))
    return pl.pallas_call(
        kernel,
        grid=(n_steps,),
        in_specs=in_specs,
        out_specs=[pl.BlockSpec((rows, D_MODEL), lambda i: (i, 0)), kv_spec, kv_spec,
                   _full((D_MODEL, IN_WIDTH)), _full((D_MODEL, D_MODEL))] + side_out_specs,
        out_shape=[jax.ShapeDtypeStruct((n_seq * seq, D_MODEL), F32), kv_shape, kv_shape,
                   jax.ShapeDtypeStruct((D_MODEL, IN_WIDTH), BF16), jax.ShapeDtypeStruct((D_MODEL, D_MODEL), BF16)]
        + side_out_shapes,
        scratch_shapes=_mixer_scratch(rows, 0),
        compiler_params=pltpu.CompilerParams(dimension_semantics=("arbitrary",), vmem_limit_bytes=VMEM_LIMIT),
        name="mixer_ctx",
    )(*args)


def _mixer_lat(x, l, n_seq, seq, mod, w_in_bf, w_out_bf, small, cos, sin, kc, vc):
    n_cache = kc.shape[2]
    kernel = functools.partial(_mixer_kernel, seq=seq, n_seq=1, n_cache=n_cache, rope=True,
                               cast_weights=False, stack_kv=False)
    cache_spec = pl.BlockSpec((None, None, n_cache, KV_WIDTH), lambda b: (b, l, 0, 0))
    return pl.pallas_call(
        kernel,
        grid=(n_seq,),
        in_specs=([pl.BlockSpec((seq, D_MODEL), lambda b: (b, 0)),
                   pl.BlockSpec((None, None, SUBLANES, D_MODEL), lambda b: (l, 1 + b, 0, 0)),
                   _resident((D_MODEL, IN_WIDTH)), _resident((D_MODEL, D_MODEL))]
                  + [_full(a.shape) if a.ndim == 2 else _layer_block(a, l) for a in small]
                  + [_full((seq, LANES)), _full((seq, LANES)), cache_spec, cache_spec]),
        out_specs=pl.BlockSpec((seq, D_MODEL), lambda b: (b, 0)),
        out_shape=jax.ShapeDtypeStruct((n_seq * seq, D_MODEL), F32),
        scratch_shapes=_mixer_scratch(seq, n_cache),
        compiler_params=pltpu.CompilerParams(dimension_semantics=("arbitrary",), vmem_limit_bytes=VMEM_LIMIT),
        name="mixer_lat",
    )(x, mod, w_in_bf, w_out_bf, *small, cos, sin, kc, vc)


def _two_stream_specs(block_rows, n_a, block_of, single_buffer_b=False):
    kw = dict(pipeline_mode=pl.Buffered(1)) if single_buffer_b else {}
    spec_a = pl.BlockSpec((block_rows, D_MODEL), lambda *ids: (jnp.minimum(block_of(*ids), n_a - 1), 0))
    spec_b = pl.BlockSpec((block_rows, D_MODEL), lambda *ids: (jnp.maximum(block_of(*ids) - n_a, 0), 0), **kw)
    return [spec_a, spec_b]


def _group_of_block(blk, block_rows, n_a, lat_seq):
    return jnp.where(blk < n_a, 0, 1 + (blk - n_a) // (lat_seq // block_rows))


def _ffn_kernel(xa_ref, xb_ref, mod_ref, w1_ref, w3_ref, w2_ref, g_ref, b_ref, oa_ref, ob_ref, *, n_a):
    is_a = pl.program_id(0) < n_a
    x = jnp.where(is_a, xa_ref[...], xb_ref[...])
    h = (_standardize(x) * (1.0 + mod_ref[4:5, :]) + mod_ref[3:4, :]).astype(BF16)
    acc = jnp.zeros(x.shape, F32)
    for c in range(D_FF // FFN_COLS):
        cols = slice(c * FFN_COLS, (c + 1) * FFN_COLS)
        a = _dot(h, w1_ref[:, cols])
        b = _dot(h, w3_ref[:, cols])
        acc = acc + _dot((_silu(a) * b).astype(BF16), w2_ref[cols, :])
    y = DEEPNORM_ALPHA * x + mod_ref[5:6, :] * acc
    out = _standardize(y) * g_ref[...] + b_ref[...]

    @pl.when(is_a)
    def _():
        oa_ref[...] = out

    @pl.when(jnp.logical_not(is_a))
    def _():
        ob_ref[...] = out


def _ffn_dense(xa, xb, l, mods, w1, w3, w2, g, b, lat_seq):
    n_a, n_b = xa.shape[0] // FFN_TILE, xb.shape[0] // FFN_TILE
    x_specs = _two_stream_specs(FFN_TILE, n_a, lambda t: t)
    return pl.pallas_call(
        functools.partial(_ffn_kernel, n_a=n_a),
        grid=(n_a + n_b,),
        in_specs=x_specs + [
            pl.BlockSpec((None, None, SUBLANES, D_MODEL),
                         lambda t: (l, _group_of_block(t, FFN_TILE, n_a, lat_seq), 0, 0)),
            _resident(w1.shape), _resident(w3.shape), _resident(w2.shape), _layer_block(g, l), _layer_block(b, l)],
        out_specs=x_specs,
        out_shape=[jax.ShapeDtypeStruct(xa.shape, F32), jax.ShapeDtypeStruct(xb.shape, F32)],
        compiler_params=pltpu.CompilerParams(dimension_semantics=("arbitrary",), vmem_limit_bytes=VMEM_LIMIT),
        name="ffn_dense",
    )(xa, xb, mods, w1, w3, w2, g, b)


def _router_kernel(xa_ref, xb_ref, mod_ref, rw_ref, before_ref, dest_ref, gate_ref, start_ref, padded_ref, *,
                   n_a, lat_seq):
    c = pl.program_id(0)
    w_hi, w_lo = _split(rw_ref[...])
    n_blocks = MOE_CHUNK // ROUTER_BLOCK
    parts = []
    for blk in range(n_blocks):
        rows = pl.ds(blk * ROUTER_BLOCK, ROUTER_BLOCK)
        mod = mod_ref[_group_of_block(c * n_blocks + blk, ROUTER_BLOCK, n_a * n_blocks, lat_seq)]
        x = jnp.where(c < n_a, xa_ref[rows, :], xb_ref[rows, :])
        h = _standardize(x) * (1.0 + mod[4:5, :]) + mod[3:4, :]
        h_hi, h_lo = _split(h)
        parts.append(_dot_nt(w_hi, h_hi) + _dot_nt(w_hi, h_lo) + _dot_nt(w_lo, h_hi))
    logits = jnp.concatenate(parts, axis=1)
    eid = lax.broadcasted_iota(I32, logits.shape, 0).astype(F32)
    m1 = jnp.max(logits, axis=0, keepdims=True)
    i1 = jnp.min(jnp.where(logits == m1, eid, float(N_EXPERTS)), axis=0, keepdims=True)
    oh1 = eid == i1
    rest = jnp.where(oh1, -jnp.inf, logits)
    m2 = jnp.max(rest, axis=0, keepdims=True)
    i2 = jnp.min(jnp.where(rest == m2, eid, float(N_EXPERTS)), axis=0, keepdims=True)
    oh2 = eid == i2
    e = jnp.exp(m2 - m1)
    gate_ref[0:1, :] = 1.0 / (1.0 + e)
    gate_ref[1:2, :] = e / (1.0 + e)
    sel = jnp.where(oh1 | oh2, 1.0, 0.0)
    ranks = []
    seen = jnp.zeros((N_EXPERTS, 1), F32)
    for blk in range(n_blocks):
        s_blk = sel[:, blk * ROUTER_BLOCK:(blk + 1) * ROUTER_BLOCK]
        ranks.append(_dot(s_blk.astype(BF16), before_ref[...]) + seen)
        seen = seen + jnp.sum(s_blk, axis=1, keepdims=True)
    rank = jnp.concatenate(ranks, axis=1)
    eid_out = lax.broadcasted_iota(I32, start_ref.shape, 0).astype(F32)
    start = jnp.zeros(sel.shape, F32)
    start_out = jnp.zeros(start_ref.shape, F32)
    padded_out = jnp.zeros(start_ref.shape, F32)
    for ex in range(N_EXPERTS):
        cnt = jnp.sum(sel[ex:ex + 1, :], axis=1, keepdims=True)
        padded = jnp.ceil(cnt * (1.0 / MOE_GRAN)) * MOE_GRAN
        start = start + jnp.where(eid > ex, padded, 0.0)
        start_out = start_out + jnp.where(eid_out > ex, padded, 0.0)
        padded_out = padded_out + jnp.where(eid_out == ex, padded, 0.0)
    row = (start + rank) * SUBLANES
    dest_ref[0:1, :] = jnp.sum(jnp.where(oh1, row, 0.0), axis=0, keepdims=True).astype(I32)
    dest_ref[1:2, :] = jnp.sum(jnp.where(oh2, row, 0.0), axis=0, keepdims=True).astype(I32)
    start_ref[...] = start_out.astype(I32)
    padded_ref[...] = padded_out.astype(I32)


def _router(xa, xb, l, mods, rw_t, lat_seq):
    n_a = xa.shape[0] // MOE_CHUNK
    n_chunks = n_a + xb.shape[0] // MOE_CHUNK
    tok = np.arange(ROUTER_BLOCK)
    before = jnp.asarray(tok[:, None] < tok[None, :], BF16)
    return pl.pallas_call(
        functools.partial(_router_kernel, n_a=n_a, lat_seq=lat_seq),
        grid=(n_chunks,),
        in_specs=_two_stream_specs(MOE_CHUNK, n_a, lambda c: c) + [
                  _layer_block(mods, l),
                  _full((N_EXPERTS, D_MODEL)), _full((ROUTER_BLOCK, ROUTER_BLOCK))],
        out_specs=[pl.BlockSpec((None, 2, MOE_CHUNK), lambda c: (c, 0, 0)),
                   pl.BlockSpec((None, 2, MOE_CHUNK), lambda c: (c, 0, 0)),
                   pl.BlockSpec((None, N_EXPERTS, LANES), lambda c: (c, 0, 0)),
                   pl.BlockSpec((None, N_EXPERTS, LANES), lambda c: (c, 0, 0))],
        out_shape=[jax.ShapeDtypeStruct((n_chunks, 2, MOE_CHUNK), I32),
                   jax.ShapeDtypeStruct((n_chunks, 2, MOE_CHUNK), F32),
                   jax.ShapeDtypeStruct((n_chunks, N_EXPERTS, LANES), I32),
                   jax.ShapeDtypeStruct((n_chunks, N_EXPERTS, LANES), I32)],
        compiler_params=pltpu.CompilerParams(dimension_semantics=("arbitrary",), vmem_limit_bytes=VMEM_LIMIT),
        name="moe_router",
    )(xa, xb, mods, rw_t, before)


def _token_rows(first_row):
    return pl.ds(pl.multiple_of(first_row, SUBLANES), SUBLANES)


def _store_token_major(ref, tok0, val):
    for cc in range(D_MODEL // LANES):
        ref[pl.ds(tok0 * SUBLANES + cc, val.shape[0], stride=SUBLANES), :] = val[:, cc * LANES:(cc + 1) * LANES]


def _load_token_major(ref, tok0, n):
    return jnp.concatenate(
        [ref[pl.ds(tok0 * SUBLANES + cc, n, stride=SUBLANES), :] for cc in range(D_MODEL // LANES)], axis=1)


def _moe_kernel(start_ref, padded_ref, xa_ref, xb_ref, mod_ref, dest_ref, gate_ref, w1_ref, w3_ref, w2_ref,
                g_ref, b_ref, oa_ref, ob_ref, tok_scr, rows_scr, *, n_a):
    c = pl.program_id(0)
    j = pl.program_id(1)
    is_a = c < n_a
    first_expert_step = MOE_TOK_STEPS
    first_combine_step = MOE_TOK_STEPS + N_EXPERTS

    @pl.when((c == 0) & (j == 0))
    def _init():
        rows_scr[...] = jnp.zeros(rows_scr.shape, F32)

    @pl.when(j < first_expert_step)
    def _dispatch():
        for r in range(MOE_TOK_BLOCK // ROW_BLOCK):
            rows = pl.ds(r * ROW_BLOCK, ROW_BLOCK)
            x = jnp.where(is_a, xa_ref[rows, :], xb_ref[rows, :])
            h = _standardize(x) * (1.0 + mod_ref[4:5, :]) + mod_ref[3:4, :]
            _store_token_major(tok_scr, r * ROW_BLOCK, h)
        t0 = j * MOE_TOK_BLOCK

        def body(t, carry):
            row = tok_scr[_token_rows(t * SUBLANES), :]
            rows_scr[_token_rows(dest_ref[t0 + t]), :] = row
            rows_scr[_token_rows(dest_ref[MOE_CHUNK + t0 + t]), :] = row
            return carry
        lax.fori_loop(0, MOE_TOK_BLOCK, body, 0, unroll=8)

    def experts(row0, m):
        xin = _load_token_major(rows_scr, row0, m).astype(BF16)
        a = _dot(xin, w1_ref[...])
        b = _dot(xin, w3_ref[...])
        y = _dot((_silu(a) * b).astype(BF16), w2_ref[...])
        _store_token_major(rows_scr, row0, y)

    @pl.when((j >= first_expert_step) & (j < first_combine_step))
    def _experts():
        region = c * N_EXPERTS + (j - first_expert_step)
        start = start_ref[region]
        padded = padded_ref[region]
        n_full = lax.shift_right_logical(padded, MOE_BLOCK.bit_length() - 1)

        def body(i, carry):
            experts(start + i * MOE_BLOCK, MOE_BLOCK)
            return carry
        lax.fori_loop(0, n_full, body, 0)
        for m in range(MOE_GRAN, MOE_BLOCK, MOE_GRAN):
            pl.when(padded - n_full * MOE_BLOCK == m)(
                functools.partial(experts, start + n_full * MOE_BLOCK, m))

    @pl.when(j >= first_combine_step)
    def _combine():
        t0 = (j - first_combine_step) * MOE_TOK_BLOCK

        def body(t, carry):
            y0 = rows_scr[_token_rows(dest_ref[t0 + t]), :]
            y1 = rows_scr[_token_rows(dest_ref[MOE_CHUNK + t0 + t]), :]
            tok_scr[_token_rows(t * SUBLANES), :] = gate_ref[t0 + t] * y0 + gate_ref[MOE_CHUNK + t0 + t] * y1
            return carry
        lax.fori_loop(0, MOE_TOK_BLOCK, body, 0, unroll=8)
        for r in range(MOE_TOK_BLOCK // ROW_BLOCK):
            rows = pl.ds(r * ROW_BLOCK, ROW_BLOCK)
            ffn = _load_token_major(tok_scr, r * ROW_BLOCK, ROW_BLOCK)
            x = jnp.where(is_a, xa_ref[rows, :], xb_ref[rows, :])
            y = DEEPNORM_ALPHA * x + mod_ref[5:6, :] * ffn
            out = _standardize(y) * g_ref[...] + b_ref[...]

            @pl.when(is_a)
            def _():
                oa_ref[rows, :] = out

            @pl.when(jnp.logical_not(is_a))
            def _():
                ob_ref[rows, :] = out


def _ffn_moe(xa, xb, l, mods, rw_t, w1, w3, w2, g, b, lat_seq):
    n_a = xa.shape[0] // MOE_CHUNK
    n_chunks = n_a + xb.shape[0] // MOE_CHUNK
    n_a_blocks = n_a * MOE_TOK_STEPS
    dest, gates, start, padded = _router(xa, xb, l, mods, rw_t, lat_seq)
    first_expert_step = MOE_TOK_STEPS
    first_combine_step = MOE_TOK_STEPS + N_EXPERTS

    def token_block(c, j, *_):
        blk = jnp.where(j < first_combine_step, jnp.minimum(j, MOE_TOK_STEPS - 1), j - first_combine_step)
        return c * MOE_TOK_STEPS + blk

    def out_token_block(c, j, *_):
        return c * MOE_TOK_STEPS + jnp.maximum(j - first_combine_step, 0)

    def mod_of(c, j, *_):
        return (l, _group_of_block(token_block(c, j), MOE_TOK_BLOCK, n_a_blocks, lat_seq), 0, 0)

    def expert_of(c, j, *_):
        return (jnp.clip(j - first_expert_step, 0, N_EXPERTS - 1), 0, 0)

    grid_spec = pltpu.PrefetchScalarGridSpec(
        num_scalar_prefetch=2,
        grid=(n_chunks, MOE_STEPS),
        in_specs=_two_stream_specs(MOE_TOK_BLOCK, n_a_blocks, token_block) + [
                  pl.BlockSpec((None, None, SUBLANES, D_MODEL), mod_of),
                  pl.BlockSpec((2 * MOE_CHUNK,), lambda c, j, *_: (c,), memory_space=pltpu.SMEM),
                  pl.BlockSpec((2 * MOE_CHUNK,), lambda c, j, *_: (c,), memory_space=pltpu.SMEM),
                  pl.BlockSpec((None, D_MODEL, D_FF_EXPERT), expert_of),
                  pl.BlockSpec((None, D_MODEL, D_FF_EXPERT), expert_of),
                  pl.BlockSpec((None, D_FF_EXPERT, D_MODEL), expert_of),
                  _layer_block(g, l), _layer_block(b, l)],
        out_specs=_two_stream_specs(MOE_TOK_BLOCK, n_a_blocks, out_token_block),
        scratch_shapes=[pltpu.VMEM((MOE_TOK_BLOCK * SUBLANES, LANES), F32),
                        pltpu.VMEM((MOE_ROWS * SUBLANES, LANES), F32)],
    )
    return pl.pallas_call(
        functools.partial(_moe_kernel, n_a=n_a),
        grid_spec=grid_spec,
        out_shape=[jax.ShapeDtypeStruct(xa.shape, F32), jax.ShapeDtypeStruct(xb.shape, F32)],
        compiler_params=pltpu.CompilerParams(
            dimension_semantics=("arbitrary", "arbitrary"), vmem_limit_bytes=VMEM_LIMIT),
        name="moe_experts",
    )(start[:, :, 0].reshape(-1), padded[:, :, 0].reshape(-1), xa, xb, mods,
      dest.reshape(-1), gates.reshape(-1), w1, w3, w2, g, b)


def _rope_tables(n_tokens):
    t = np.arange(n_tokens)
    row = (t // GRID_W).astype(np.float32)
    col = (t % GRID_W).astype(np.float32)
    inv_freq = (np.float32(ROPE_THETA) ** (-np.arange(0, AXIS_ROT, 2, dtype=np.float32) / AXIS_ROT)).astype(np.float32)
    ang_r = row[:, None] * inv_freq
    ang_c = col[:, None] * inv_freq
    cos = np.concatenate([np.cos(ang_r), np.cos(ang_r), np.cos(ang_c), np.cos(ang_c)], axis=1)
    sin = np.concatenate([-np.sin(ang_r), np.sin(ang_r), -np.sin(ang_c), np.sin(ang_c)], axis=1)
    return jnp.asarray(np.tile(cos, (1, 2)), F32), jnp.asarray(np.tile(sin, (1, 2)), F32)


def kernel(x_prompt, x_sample, cache_k, cache_v, c, c_ctx, ada_w, ada_b, w_in, q_norm_g, k_norm_g, conv_w, conv_b, sgu_norm_g, sgu_w, sgu_b, w_out, ln1_g, ln1_b, ln2_g, ln2_b, ffn_w1, ffn_w3, ffn_w2, router_w, moe_w1, moe_w3, moe_w2):
    batch, seq, _ = x_prompt.shape
    dec_batch, dec_seq, _ = x_sample.shape
    past_len = cache_k.shape[2]
    n_ctx = batch * seq
    n_lat = dec_batch * dec_seq
    assert DEPTH == 2 and 1 + dec_batch <= SUBLANES
    assert seq == ATT_BLOCK and CTX_SEQS_PER_STEP * seq == ROW_BLOCK and batch % CTX_SEQS_PER_STEP == 0
    assert dec_seq % ROW_BLOCK == 0 and dec_seq & (dec_seq - 1) == 0
    assert n_ctx % MOE_CHUNK == 0 and n_lat % MOE_CHUNK == 0
    assert dec_seq % ROUTER_BLOCK == 0 and dec_seq % MOE_TOK_BLOCK == 0 and dec_seq % FFN_TILE == 0

    cond = jnp.zeros((SUBLANES, D_MODEL), F32).at[0].set(c_ctx).at[1:1 + dec_batch].set(c)
    mod = _modulation(cond, ada_w, ada_b)
    mod = mod.reshape(DEPTH, SUBLANES, 6, D_MODEL)[:, :1 + dec_batch]
    mod = jnp.pad(mod, ((0, 0), (0, 0), (0, SUBLANES - 6), (0, 0)))

    lane_id = np.arange(GROUP_TILE) // HEAD_DIM
    ones_bd = jnp.asarray(lane_id[:, None] == lane_id[None, :], BF16)
    cos, sin = _rope_tables(dec_seq)
    small = (
        jnp.tile(q_norm_g, (1, N_Q_HEADS))[:, None, :], jnp.tile(k_norm_g, (1, N_KV_HEADS))[:, None, :],
        ones_bd,
        conv_w, conv_b[:, None, :], sgu_norm_g[:, None, :], sgu_w,
        jnp.repeat(jnp.swapaxes(sgu_b, 1, 2), HEAD_DIM, axis=2),
        ln1_g[:, None, :], ln1_b[:, None, :],
    )
    g2, b2 = ln2_g[:, None, :], ln2_b[:, None, :]
    kc = cache_k.reshape(dec_batch, DEPTH, past_len, KV_WIDTH)
    vc = cache_v.reshape(dec_batch, DEPTH, past_len, KV_WIDTH)

    xs = [x_prompt.reshape(n_ctx, D_MODEL), x_sample.reshape(n_lat, D_MODEL)]
    assert ffn_w1.shape[0] == 1 and moe_w1.shape[0] == 1
    w1_rows = moe_w1.reshape(-1, D_FF_EXPERT)
    w3_rows = moe_w3.reshape(-1, D_FF_EXPERT)
    w2_rows = moe_w2.reshape(-1, D_MODEL)
    kv = None
    for l in range(DEPTH):
        if l == 0:
            side = [("cast", ffn_w1[0]), ("cast", ffn_w3[0]), ("cast", ffn_w2[0]), ("cast", w1_rows), ("even", w2_rows)]
        else:
            side = [("cast", w3_rows), ("merge", w2_rows, w2_even)]
        x_ctx, k_ctx, v_ctx, w_in_bf, w_out_bf, *cast = _mixer_ctx(
            xs[0], l, batch, seq, mod, w_in, w_out, small, kv, side)
        if l == 0:
            ffn_bf, (w1_bf, w2_even) = cast[:3], cast[3:]
        else:
            w3_bf, w2_bf = cast
        kv = (k_ctx, v_ctx)
        x_lat = _mixer_lat(xs[1], l, dec_batch, dec_seq, mod, w_in_bf, w_out_bf, small, cos, sin, kc, vc)
        if l % 2 == 0:
            xs = _ffn_dense(x_ctx, x_lat, l, mod, *ffn_bf, g2, b2, dec_seq)
        else:
            ws = (w1_bf.reshape(N_EXPERTS, D_MODEL, D_FF_EXPERT), w3_bf.reshape(N_EXPERTS, D_MODEL, D_FF_EXPERT),
                  w2_bf.reshape(N_EXPERTS, D_FF_EXPERT, D_MODEL))
            xs = _ffn_moe(x_ctx, x_lat, l, mod, router_w[l // 2].T, *ws, g2, b2, dec_seq)
    y_p = xs[0].reshape(batch, seq, D_MODEL)
    y_s = xs[1].reshape(dec_batch, dec_seq, D_MODEL)
    new_k = kv[0].reshape(batch, DEPTH, seq, N_KV_HEADS, HEAD_DIM)
    new_v = kv[1].reshape(batch, DEPTH, seq, N_KV_HEADS, HEAD_DIM)
    return (y_p, y_s, new_k, new_v)
```

```python
import functools

import numpy as np
import jax
import jax.numpy as jnp
from jax import lax
from jax.experimental import pallas as pl
from jax.experimental.pallas import tpu as pltpu

F32 = jnp.float32
BF16 = jnp.bfloat16
I32 = jnp.int32

D_MODEL = 1024
DEPTH = 2
GRID_W = 64
HEAD_DIM = 64
N_Q_HEADS = 8
N_KV_HEADS = 2
ATTN_WIDTH = N_Q_HEADS * HEAD_DIM
KV_WIDTH = N_KV_HEADS * HEAD_DIM
ATTN_SCALE = HEAD_DIM ** -0.5
ROPE_THETA = 10000.0
AXIS_ROT = HEAD_DIM // 2
CONV_WIDTH = 256
SGU_WIDTH = 256
SGU_HEADS = 4
CHUNK = 128
IN_WIDTH = 2048
D_FF = 2816
N_EXPERTS = 8
D_FF_EXPERT = 1408
EPS = 1e-6
DEEPNORM_ALPHA = (2 * DEPTH) ** 0.25

LANES = 128
SUBLANES = 8
ROW_BLOCK = 512
GROUP_TILE = 256
ATT_BLOCK = 256
CTX_SEQS_PER_STEP = 2
FFN_TILE = 512
FFN_COLS = 256
MOE_CHUNK = 2048
ROUTER_BLOCK = 1024
MOE_GRAN = 128
MOE_BLOCK = 256
MOE_TOK_BLOCK = 512
MOE_TOK_STEPS = MOE_CHUNK // MOE_TOK_BLOCK
MOE_STEPS = 2 * MOE_TOK_STEPS + N_EXPERTS
MOE_ROWS = 2 * MOE_CHUNK + N_EXPERTS * MOE_GRAN
VMEM_LIMIT = 58 * 1024 * 1024

_Q0, _K0, _V0, _CI0, _CB0, _CC0, _SU0, _SV0 = 0, 512, 640, 768, 1024, 1280, 1536, 1792


def _dot(a, b):
    return jnp.dot(a, b, preferred_element_type=F32)


def _dot_nt(a, b):
    return lax.dot_general(a, b, (((1,), (1,)), ((), ())), preferred_element_type=F32)


def _split(x):
    hi = x.astype(BF16)
    lo = (x - hi.astype(F32)).astype(BF16)
    return hi, lo


def _group_sum(x, ones_bd):
    outs = []
    for c0 in range(0, x.shape[1], GROUP_TILE):
        width = min(GROUP_TILE, x.shape[1] - c0)
        outs.append(_dot(x[:, c0:c0 + width].astype(BF16), ones_bd[:width, :width]))
    return outs[0] if len(outs) == 1 else jnp.concatenate(outs, axis=1)


def _standardize(x):
    mu = jnp.mean(x, axis=-1, keepdims=True)
    d = x - mu
    return d * lax.rsqrt(jnp.mean(d * d, axis=-1, keepdims=True) + EPS)


def _silu(x):
    return x / (1.0 + jnp.exp(-x))


def _modulation_kernel(cond_ref, w_ref, b_ref, o_ref):
    s_hi, s_lo = _split(_silu(cond_ref[...]))
    w_hi, w_lo = _split(w_ref[...])
    o_ref[...] = _dot(s_hi, w_hi) + _dot(s_hi, w_lo) + _dot(s_lo, w_hi) + b_ref[...]


def _modulation(cond, ada_w, ada_b):
    n_out = ada_w.shape[-1]
    tn = 1536
    return pl.pallas_call(
        _modulation_kernel,
        grid=(DEPTH, n_out // tn),
        in_specs=[
            pl.BlockSpec((SUBLANES, D_MODEL), lambda l, j: (0, 0)),
            pl.BlockSpec((None, D_MODEL, tn), lambda l, j: (l, 0, j)),
            pl.BlockSpec((None, 1, tn), lambda l, j: (l, 0, j)),
        ],
        out_specs=pl.BlockSpec((None, SUBLANES, tn), lambda l, j: (l, 0, j)),
        out_shape=jax.ShapeDtypeStruct((DEPTH, SUBLANES, n_out), F32),
        compiler_params=pltpu.CompilerParams(
            dimension_semantics=("arbitrary", "arbitrary"), vmem_limit_bytes=VMEM_LIMIT),
        name="modulation",
    )(cond, ada_w, ada_b.reshape(DEPTH, 1, n_out))


def _rope(x, cos, sin_signed):
    w = x.shape[1]
    lane = lax.broadcasted_iota(I32, x.shape, 1)
    first_half = (lane & 31) < 16
    partner = jnp.where(first_half, pltpu.roll(x, w - 16, 1), pltpu.roll(x, 16, 1))
    return x * cos + partner * sin_signed


def _head_variants(x):
    lane = lax.broadcasted_iota(I32, x.shape, 1)
    lo = lane < HEAD_DIM
    xr = pltpu.roll(x, HEAD_DIM, 1)
    zero = jnp.zeros_like(x)
    return (jnp.where(lo, x, zero).astype(BF16), jnp.where(lo, zero, xr).astype(BF16),
            jnp.where(lo, xr, zero).astype(BF16), jnp.where(lo, zero, x).astype(BF16))


def _mixer_kernel(*refs, seq, n_seq, n_cache, rope, cast_weights, stack_kv, side_jobs=()):
    refs = list(refs)

    def take(n):
        out, refs[:] = refs[:n], refs[n:]
        return out

    x_ref, mod_ref, win_ref, wout_ref = take(4)
    qg_ref, kg_ref, ones_ref, convw_ref, convb_ref, sgug_ref, sguw_ref, sgub_ref, ln1g_ref, ln1b_ref = take(10)
    if rope:
        cos_ref, sin_ref, kc_ref, vc_ref = take(4)
    if stack_kv:
        kprev_ref, vprev_ref = take(2)
    side_in = [take(n_src) for n_src in side_jobs]
    (x1_ref,) = take(1)
    if not rope:
        k_ref, v_ref = take(2)
    if cast_weights:
        winb_ref, woutb_ref = take(2)

        @pl.when(pl.program_id(0) == 0)
        def _cast():
            winb_ref[...] = win_ref[...].astype(BF16)
            woutb_ref[...] = wout_ref[...].astype(BF16)
        win_ref, wout_ref = winb_ref, woutb_ref
    for srcs, dst in zip(side_in, take(len(side_jobs))):
        col = 0
        for src in srcs:
            dst[:, col:col + src.shape[1]] = src[...].astype(BF16)
            col += src.shape[1]
    q_scr, kvar_scr, vvar_scr, u_scr, cb_scr, su_scr, vn_scr, mix_scr = take(8)
    n_rows = n_seq * seq
    assert n_cache == 0 or n_seq == 1

    def loop(n, body):
        if n == 1:
            body(0)
        else:
            def step(r, carry):
                body(r)
                return carry
            lax.fori_loop(0, n, step, 0)

    def block(r, size):
        if isinstance(r, int):
            return pl.ds(r * size, size)
        return pl.ds(pl.multiple_of(r * size, size), size)

    if n_cache:
        for i, var in enumerate(_head_variants(kc_ref[...])):
            kvar_scr[i, pl.ds(seq, n_cache), :] = var
        for i, var in enumerate(_head_variants(vc_ref[...])):
            vvar_scr[i, pl.ds(seq, n_cache), :] = var

    def project(r):
        rows = block(r, ROW_BLOCK)
        x = x_ref[rows, :]
        h = _standardize(x) * (1.0 + mod_ref[1:2, :]) + mod_ref[0:1, :]
        z = _dot(h.astype(BF16), win_ref[...])
        ones_bd = ones_ref[...]
        zq = z[:, _Q0:_K0]
        q = zq * lax.rsqrt(_group_sum(zq * zq, ones_bd) * (1.0 / HEAD_DIM) + EPS) * qg_ref[...]
        zk = z[:, _K0:_V0]
        k = zk * lax.rsqrt(_group_sum(zk * zk, ones_bd) * (1.0 / HEAD_DIM) + EPS) * kg_ref[...]
        v = z[:, _V0:_CI0]
        if rope:
            cos = cos_ref[rows, :]
            sin = sin_ref[rows, :]
            q = _rope(q, jnp.concatenate([cos] * 4, axis=1), jnp.concatenate([sin] * 4, axis=1))
            k = _rope(k, cos, sin)
        elif stack_kv:
            for s in range(ROW_BLOCK // seq):
                sub = slice(s * seq, (s + 1) * seq)
                k_ref[s, 0] = kprev_ref[sub, :]
                v_ref[s, 0] = vprev_ref[sub, :]
                k_ref[s, 1] = k[sub, :]
                v_ref[s, 1] = v[sub, :]
        else:
            k_ref[rows, :] = k
            v_ref[rows, :] = v
        q_scr[rows, :] = (q * ATTN_SCALE).astype(BF16)
        for i, var in enumerate(_head_variants(k)):
            kvar_scr[i, rows, :] = var
        for i, var in enumerate(_head_variants(v)):
            vvar_scr[i, rows, :] = var
        u_scr[rows, :] = z[:, _CC0:_SU0] * z[:, _CI0:_CB0]
        cb_scr[rows, :] = z[:, _CB0:_CC0]
        su_scr[rows, :] = z[:, _SU0:_SV0]
        sv = z[:, _SV0:IN_WIDTH]
        d = sv - _group_sum(sv, ones_bd) * (1.0 / HEAD_DIM)
        vn = d * lax.rsqrt(_group_sum(d * d, ones_bd) * (1.0 / HEAD_DIM) + EPS) * sgug_ref[...]
        vn_scr[rows, :] = vn.astype(BF16)

    loop(n_rows // ROW_BLOCK, project)

    u = u_scr[...]
    pos = lax.broadcasted_iota(I32, u.shape, 0) & (seq - 1)
    up = jnp.where(pos == 0, 0.0, pltpu.roll(u, 1, 0))
    dn = jnp.where(pos == seq - 1, 0.0, pltpu.roll(u, n_rows - 1, 0))
    conv = up * convw_ref[0:1, :] + u * convw_ref[1:2, :] + dn * convw_ref[2:3, :] + convb_ref[...]
    mix_scr[:, ATTN_WIDTH:ATTN_WIDTH + CONV_WIDTH] = (cb_scr[...] * conv).astype(BF16)

    for n in range(n_rows // CHUNK):
        rows = pl.ds(n * CHUNK, CHUNK)
        vn = vn_scr[rows, :]
        lane = lax.broadcasted_iota(I32, vn.shape, 1)
        s = sgub_ref[...]
        for hd in range(SGU_HEADS):
            in_head = (lane >= hd * HEAD_DIM) & (lane < (hd + 1) * HEAD_DIM)
            masked = jnp.where(in_head, vn, jnp.zeros_like(vn))
            s = s + _dot(sguw_ref[hd].astype(BF16), masked)
        mix_scr[rows, ATTN_WIDTH + CONV_WIDTH:] = (su_scr[rows, :] * s).astype(BF16)

    def attend(s, r):
        rows = block(s * (seq // ATT_BLOCK) + r, ATT_BLOCK)
        keys = pl.ds(s * seq, seq + n_cache)
        for pair in range(N_Q_HEADS // 2):
            qp = q_scr[rows, pair * LANES:(pair + 1) * LANES]
            kv = pair // (N_Q_HEADS // N_KV_HEADS // 2)
            acc = jnp.zeros((ATT_BLOCK, LANES), F32)
            for parity in range(2):
                sc = _dot_nt(qp, kvar_scr[2 * kv + parity, keys, :])
                p = jnp.exp(sc - jnp.max(sc, axis=1, keepdims=True))
                denom = jnp.sum(p, axis=1, keepdims=True)
                acc = acc + _dot(p.astype(BF16), vvar_scr[2 * kv + parity, keys, :]) / denom
            mix_scr[rows, pair * LANES:(pair + 1) * LANES] = acc.astype(BF16)

    for s in range(n_seq):
        loop(seq // ATT_BLOCK, functools.partial(attend, s))

    def finish(r):
        rows = block(r, ROW_BLOCK)
        mix = _dot(mix_scr[rows, :], wout_ref[...])
        y = DEEPNORM_ALPHA * x_ref[rows, :] + mod_ref[2:3, :] * mix
        x1_ref[rows, :] = _standardize(y) * ln1g_ref[...] + ln1b_ref[...]

    loop(n_rows // ROW_BLOCK, finish)


def _full(shape):
    n = len(shape)
    return pl.BlockSpec(shape, lambda *_: (0,) * n)


def _resident(shape):
    n = len(shape)
    return pl.BlockSpec(shape, lambda *_: (0,) * n, pipeline_mode=pl.Buffered(1))


def _layer_block(arr, l, resident=False):
    shape = arr.shape[1:]
    kw = dict(pipeline_mode=pl.Buffered(1)) if resident else {}
    return pl.BlockSpec((None,) + shape, lambda *_: (l,) + (0,) * len(shape), **kw)


def _mixer_scratch(n_rows, n_cache):
    nk = n_rows + n_cache
    return [
        pltpu.VMEM((n_rows, ATTN_WIDTH), BF16),
        pltpu.VMEM((4, nk, LANES), BF16),
        pltpu.VMEM((4, nk, LANES), BF16),
        pltpu.VMEM((n_rows, CONV_WIDTH), F32),
        pltpu.VMEM((n_rows, CONV_WIDTH), F32),
        pltpu.VMEM((n_rows, SGU_WIDTH), F32),
        pltpu.VMEM((n_rows, SGU_WIDTH), BF16),
        pltpu.VMEM((n_rows, D_MODEL), BF16),
    ]


def _mixer_ctx(x, l, n_seq, seq, mod, w_in, w_out, small, kv_prev=None, side_casts=()):
    per_step = CTX_SEQS_PER_STEP
    rows = per_step * seq
    stack_kv = kv_prev is not None
    n_steps = n_seq // per_step
    kernel = functools.partial(_mixer_kernel, seq=seq, n_seq=per_step, n_cache=0, rope=False,
                               cast_weights=True, stack_kv=stack_kv,
                               side_jobs=tuple(len(job) for job in side_casts))
    in_specs = ([pl.BlockSpec((rows, D_MODEL), lambda i: (i, 0)),
                 pl.BlockSpec((None, None, SUBLANES, D_MODEL), lambda i: (l, 0, 0, 0)),
                 _layer_block(w_in, l, resident=True), _layer_block(w_out, l, resident=True)]
                + [_full(a.shape) if a.ndim == 2 else _layer_block(a, l) for a in small])
    args = [x, mod, w_in, w_out, *small]
    if stack_kv:
        in_specs += [pl.BlockSpec((rows, KV_WIDTH), lambda i: (i, 0))] * 2
        args += list(kv_prev)
        kv_spec = pl.BlockSpec((per_step, DEPTH, seq, KV_WIDTH), lambda i: (i, 0, 0, 0))
        kv_shape = jax.ShapeDtypeStruct((n_seq, DEPTH, seq, KV_WIDTH), F32)
    else:
        kv_spec = pl.BlockSpec((rows, KV_WIDTH), lambda i: (i, 0))
        kv_shape = jax.ShapeDtypeStruct((n_seq * seq, KV_WIDTH), F32)
    side_out_specs, side_out_shapes = [], []
    for job in side_casts:
        n_rows = job[0].shape[0]
        n_cols = sum(src.shape[1] for src in job)
        for src in job:
            in_specs.append(pl.BlockSpec((n_rows // n_steps, src.shape[1]), lambda i: (i, 0)))
            args.append(src)
        side_out_specs.append(pl.BlockSpec((n_rows // n_steps, n_cols), lambda i: (i, 0)))
        side_out_shapes.append(jax.ShapeDtypeStruct((n_rows, n_cols), BF16))
    return pl.pallas_call(
        kernel,
        grid=(n_steps,),
        in_specs=in_specs,
        out_specs=[pl.BlockSpec((rows, D_MODEL), lambda i: (i, 0)), kv_spec, kv_spec,
                   _full((D_MODEL, IN_WIDTH)), _full((D_MODEL, D_MODEL))] + side_out_specs,
        out_shape=[jax.ShapeDtypeStruct((n_seq * seq, D_MODEL), F32), kv_shape, kv_shape,
                   jax.ShapeDtypeStruct((D_MODEL, IN_WIDTH), BF16), jax.ShapeDtypeStruct((D_MODEL, D_MODEL), BF16)]
        + side_out_shapes,
        scratch_shapes=_mixer_scratch(rows, 0),
        compiler_params=pltpu.CompilerParams(dimension_semantics=("arbitrary",), vmem_limit_bytes=VMEM_LIMIT),
        name="mixer_ctx",
    )(*args)


def _mixer_lat(x, l, n_seq, seq, mod, w_in_bf, w_out_bf, small, cos, sin, kc, vc):
    n_cache = kc.shape[2]
    kernel = functools.partial(_mixer_kernel, seq=seq, n_seq=1, n_cache=n_cache, rope=True,
                               cast_weights=False, stack_kv=False)
    cache_spec = pl.BlockSpec((None, None, n_cache, KV_WIDTH), lambda b: (b, l, 0, 0))
    return pl.pallas_call(
        kernel,
        grid=(n_seq,),
        in_specs=([pl.BlockSpec((seq, D_MODEL), lambda b: (b, 0)),
                   pl.BlockSpec((None, None, SUBLANES, D_MODEL), lambda b: (l, 1 + b, 0, 0)),
                   _resident((D_MODEL, IN_WIDTH)), _resident((D_MODEL, D_MODEL))]
                  + [_full(a.shape) if a.ndim == 2 else _layer_block(a, l) for a in small]
                  + [_full((seq, LANES)), _full((seq, LANES)), cache_spec, cache_spec]),
        out_specs=pl.BlockSpec((seq, D_MODEL), lambda b: (b, 0)),
        out_shape=jax.ShapeDtypeStruct((n_seq * seq, D_MODEL), F32),
        scratch_shapes=_mixer_scratch(seq, n_cache),
        compiler_params=pltpu.CompilerParams(dimension_semantics=("arbitrary",), vmem_limit_bytes=VMEM_LIMIT),
        name="mixer_lat",
    )(x, mod, w_in_bf, w_out_bf, *small, cos, sin, kc, vc)


def _two_stream_specs(block_rows, n_a, block_of, single_buffer_b=False):
    kw = dict(pipeline_mode=pl.Buffered(1)) if single_buffer_b else {}
    spec_a = pl.BlockSpec((block_rows, D_MODEL), lambda *ids: (jnp.minimum(block_of(*ids), n_a - 1), 0))
    spec_b = pl.BlockSpec((block_rows, D_MODEL), lambda *ids: (jnp.maximum(block_of(*ids) - n_a, 0), 0), **kw)
    return [spec_a, spec_b]


def _group_of_block(blk, block_rows, n_a, lat_seq):
    return jnp.where(blk < n_a, 0, 1 + (blk - n_a) // (lat_seq // block_rows))


def _ffn_kernel(xa_ref, xb_ref, mod_ref, w1_ref, w3_ref, w2_ref, g_ref, b_ref, oa_ref, ob_ref, *, n_a):
    is_a = pl.program_id(0) < n_a
    x = jnp.where(is_a, xa_ref[...], xb_ref[...])
    h = (_standardize(x) * (1.0 + mod_ref[4:5, :]) + mod_ref[3:4, :]).astype(BF16)
    acc = jnp.zeros(x.shape, F32)
    for c in range(D_FF // FFN_COLS):
        cols = slice(c * FFN_COLS, (c + 1) * FFN_COLS)
        a = _dot(h, w1_ref[:, cols])
        b = _dot(h, w3_ref[:, cols])
        acc = acc + _dot((_silu(a) * b).astype(BF16), w2_ref[cols, :])
    y = DEEPNORM_ALPHA * x + mod_ref[5:6, :] * acc
    out = _standardize(y) * g_ref[...] + b_ref[...]

    @pl.when(is_a)
    def _():
        oa_ref[...] = out

    @pl.when(jnp.logical_not(is_a))
    def _():
        ob_ref[...] = out


def _ffn_dense(xa, xb, l, mods, w1, w3, w2, g, b, lat_seq):
    n_a, n_b = xa.shape[0] // FFN_TILE, xb.shape[0] // FFN_TILE
    x_specs = _two_stream_specs(FFN_TILE, n_a, lambda t: t)
    return pl.pallas_call(
        functools.partial(_ffn_kernel, n_a=n_a),
        grid=(n_a + n_b,),
        in_specs=x_specs + [
            pl.BlockSpec((None, None, SUBLANES, D_MODEL),
                         lambda t: (l, _group_of_block(t, FFN_TILE, n_a, lat_seq), 0, 0)),
            _resident(w1.shape), _resident(w3.shape), _resident(w2.shape), _layer_block(g, l), _layer_block(b, l)],
        out_specs=x_specs,
        out_shape=[jax.ShapeDtypeStruct(xa.shape, F32), jax.ShapeDtypeStruct(xb.shape, F32)],
        compiler_params=pltpu.CompilerParams(dimension_semantics=("arbitrary",), vmem_limit_bytes=VMEM_LIMIT),
        name="ffn_dense",
    )(xa, xb, mods, w1, w3, w2, g, b)


def _router_kernel(xa_ref, xb_ref, mod_ref, rw_ref, before_ref, dest_ref, gate_ref, start_ref, padded_ref, *,
                   n_a, lat_seq):
    c = pl.program_id(0)
    w_hi, w_lo = _split(rw_ref[...])
    n_blocks = MOE_CHUNK // ROUTER_BLOCK
    parts = []
    for blk in range(n_blocks):
        rows = pl.ds(blk * ROUTER_BLOCK, ROUTER_BLOCK)
        mod = mod_ref[_group_of_block(c * n_blocks + blk, ROUTER_BLOCK, n_a * n_blocks, lat_seq)]
        x = jnp.where(c < n_a, xa_ref[rows, :], xb_ref[rows, :])
        h = _standardize(x) * (1.0 + mod[4:5, :]) + mod[3:4, :]
        h_hi, h_lo = _split(h)
        parts.append(_dot_nt(w_hi, h_hi) + _dot_nt(w_hi, h_lo) + _dot_nt(w_lo, h_hi))
    logits = jnp.concatenate(parts, axis=1)
    eid = lax.broadcasted_iota(I32, logits.shape, 0).astype(F32)
    m1 = jnp.max(logits, axis=0, keepdims=True)
    i1 = jnp.min(jnp.where(logits == m1, eid, float(N_EXPERTS)), axis=0, keepdims=True)
    oh1 = eid == i1
    rest = jnp.where(oh1, -jnp.inf, logits)
    m2 = jnp.max(rest, axis=0, keepdims=True)
    i2 = jnp.min(jnp.where(rest == m2, eid, float(N_EXPERTS)), axis=0, keepdims=True)
    oh2 = eid == i2
    e = jnp.exp(m2 - m1)
    gate_ref[0:1, :] = 1.0 / (1.0 + e)
    gate_ref[1:2, :] = e / (1.0 + e)
    sel = jnp.where(oh1 | oh2, 1.0, 0.0)
    ranks = []
    seen = jnp.zeros((N_EXPERTS, 1), F32)
    for blk in range(n_blocks):
        s_blk = sel[:, blk * ROUTER_BLOCK:(blk + 1) * ROUTER_BLOCK]
        ranks.append(_dot(s_blk.astype(BF16), before_ref[...]) + seen)
        seen = seen + jnp.sum(s_blk, axis=1, keepdims=True)
    rank = jnp.concatenate(ranks, axis=1)
    eid_out = lax.broadcasted_iota(I32, start_ref.shape, 0).astype(F32)
    start = jnp.zeros(sel.shape, F32)
    start_out = jnp.zeros(start_ref.shape, F32)
    padded_out = jnp.zeros(start_ref.shape, F32)
    for ex in range(N_EXPERTS):
        cnt = jnp.sum(sel[ex:ex + 1, :], axis=1, keepdims=True)
        padded = jnp.ceil(cnt * (1.0 / MOE_GRAN)) * MOE_GRAN
        start = start + jnp.where(eid > ex, padded, 0.0)
        start_out = start_out + jnp.where(eid_out > ex, padded, 0.0)
        padded_out = padded_out + jnp.where(eid_out == ex, padded, 0.0)
    row = (start + rank) * SUBLANES
    dest_ref[0:1, :] = jnp.sum(jnp.where(oh1, row, 0.0), axis=0, keepdims=True).astype(I32)
    dest_ref[1:2, :] = jnp.sum(jnp.where(oh2, row, 0.0), axis=0, keepdims=True).astype(I32)
    start_ref[...] = start_out.astype(I32)
    padded_ref[...] = padded_out.astype(I32)


def _router(xa, xb, l, mods, rw_t, lat_seq):
    n_a = xa.shape[0] // MOE_CHUNK
    n_chunks = n_a + xb.shape[0] // MOE_CHUNK
    tok = np.arange(ROUTER_BLOCK)
    before = jnp.asarray(tok[:, None] < tok[None, :], BF16)
    return pl.pallas_call(
        functools.partial(_router_kernel, n_a=n_a, lat_seq=lat_seq),
        grid=(n_chunks,),
        in_specs=_two_stream_specs(MOE_CHUNK, n_a, lambda c: c) + [
                  _layer_block(mods, l),
                  _full((N_EXPERTS, D_MODEL)), _full((ROUTER_BLOCK, ROUTER_BLOCK))],
        out_specs=[pl.BlockSpec((None, 2, MOE_CHUNK), lambda c: (c, 0, 0)),
                   pl.BlockSpec((None, 2, MOE_CHUNK), lambda c: (c, 0, 0)),
                   pl.BlockSpec((None, N_EXPERTS, LANES), lambda c: (c, 0, 0)),
                   pl.BlockSpec((None, N_EXPERTS, LANES), lambda c: (c, 0, 0))],
        out_shape=[jax.ShapeDtypeStruct((n_chunks, 2, MOE_CHUNK), I32),
                   jax.ShapeDtypeStruct((n_chunks, 2, MOE_CHUNK), F32),
                   jax.ShapeDtypeStruct((n_chunks, N_EXPERTS, LANES), I32),
                   jax.ShapeDtypeStruct((n_chunks, N_EXPERTS, LANES), I32)],
        compiler_params=pltpu.CompilerParams(dimension_semantics=("arbitrary",), vmem_limit_bytes=VMEM_LIMIT),
        name="moe_router",
    )(xa, xb, mods, rw_t, before)


def _token_rows(first_row):
    return pl.ds(pl.multiple_of(first_row, SUBLANES), SUBLANES)


def _store_token_major(ref, tok0, val):
    for cc in range(D_MODEL // LANES):
        ref[pl.ds(tok0 * SUBLANES + cc, val.shape[0], stride=SUBLANES), :] = val[:, cc * LANES:(cc + 1) * LANES]


def _load_token_major(ref, tok0, n):
    return jnp.concatenate(
        [ref[pl.ds(tok0 * SUBLANES + cc, n, stride=SUBLANES), :] for cc in range(D_MODEL // LANES)], axis=1)


def _moe_kernel(start_ref, padded_ref, xa_ref, xb_ref, mod_ref, dest_ref, gate_ref, w13_ref, w2_ref,
                g_ref, b_ref, oa_ref, ob_ref, tok_scr, rows_scr, *, n_a):
    c = pl.program_id(0)
    j = pl.program_id(1)
    is_a = c < n_a
    first_expert_step = MOE_TOK_STEPS
    first_combine_step = MOE_TOK_STEPS + N_EXPERTS

    @pl.when((c == 0) & (j == 0))
    def _init():
        rows_scr[...] = jnp.zeros(rows_scr.shape, F32)

    @pl.when(j < first_expert_step)
    def _dispatch():
        for r in range(MOE_TOK_BLOCK // ROW_BLOCK):
            rows = pl.ds(r * ROW_BLOCK, ROW_BLOCK)
            x = jnp.where(is_a, xa_ref[rows, :], xb_ref[rows, :])
            h = _standardize(x) * (1.0 + mod_ref[4:5, :]) + mod_ref[3:4, :]
            _store_token_major(tok_scr, r * ROW_BLOCK, h)
        t0 = j * MOE_TOK_BLOCK

        def body(t, carry):
            row = tok_scr[_token_rows(t * SUBLANES), :]
            rows_scr[_token_rows(dest_ref[t0 + t]), :] = row
            rows_scr[_token_rows(dest_ref[MOE_CHUNK + t0 + t]), :] = row
            return carry
        lax.fori_loop(0, MOE_TOK_BLOCK, body, 0, unroll=8)

    def experts(row0, m):
        xin = _load_token_major(rows_scr, row0, m).astype(BF16)
        ab = _dot(xin, w13_ref[...])
        a, b = ab[:, :D_FF_EXPERT], ab[:, D_FF_EXPERT:]
        y = _dot((_silu(a) * b).astype(BF16), w2_ref[...])
        _store_token_major(rows_scr, row0, y)

    @pl.when((j >= first_expert_step) & (j < first_combine_step))
    def _experts():
        region = c * N_EXPERTS + (j - first_expert_step)
        start = start_ref[region]
        padded = padded_ref[region]
        n_full = lax.shift_right_logical(padded, MOE_BLOCK.bit_length() - 1)

        def body(i, carry):
            experts(start + i * MOE_BLOCK, MOE_BLOCK)
            return carry
        lax.fori_loop(0, n_full, body, 0)
        for m in range(MOE_GRAN, MOE_BLOCK, MOE_GRAN):
            pl.when(padded - n_full * MOE_BLOCK == m)(
                functools.partial(experts, start + n_full * MOE_BLOCK, m))

    @pl.when(j >= first_combine_step)
    def _combine():
        t0 = (j - first_combine_step) * MOE_TOK_BLOCK

        def body(t, carry):
            y0 = rows_scr[_token_rows(dest_ref[t0 + t]), :]
            y1 = rows_scr[_token_rows(dest_ref[MOE_CHUNK + t0 + t]), :]
            tok_scr[_token_rows(t * SUBLANES), :] = gate_ref[t0 + t] * y0 + gate_ref[MOE_CHUNK + t0 + t] * y1
            return carry
        lax.fori_loop(0, MOE_TOK_BLOCK, body, 0, unroll=8)
        for r in range(MOE_TOK_BLOCK // ROW_BLOCK):
            rows = pl.ds(r * ROW_BLOCK, ROW_BLOCK)
            ffn = _load_token_major(tok_scr, r * ROW_BLOCK, ROW_BLOCK)
            x = jnp.where(is_a, xa_ref[rows, :], xb_ref[rows, :])
            y = DEEPNORM_ALPHA * x + mod_ref[5:6, :] * ffn
            out = _standardize(y) * g_ref[...] + b_ref[...]

            @pl.when(is_a)
            def _():
                oa_ref[rows, :] = out

            @pl.when(jnp.logical_not(is_a))
            def _():
                ob_ref[rows, :] = out


def _ffn_moe(xa, xb, l, mods, rw_t, w13, w2, g, b, lat_seq):
    n_a = xa.shape[0] // MOE_CHUNK
    n_chunks = n_a + xb.shape[0] // MOE_CHUNK
    n_a_blocks = n_a * MOE_TOK_STEPS
    dest, gates, start, padded = _router(xa, xb, l, mods, rw_t, lat_seq)
    first_expert_step = MOE_TOK_STEPS
    first_combine_step = MOE_TOK_STEPS + N_EXPERTS

    def token_block(c, j, *_):
        blk = jnp.where(j < first_combine_step, jnp.minimum(j, MOE_TOK_STEPS - 1), j - first_combine_step)
        return c * MOE_TOK_STEPS + blk

    def out_token_block(c, j, *_):
        return c * MOE_TOK_STEPS + jnp.maximum(j - first_combine_step, 0)

    def mod_of(c, j, *_):
        return (l, _group_of_block(token_block(c, j), MOE_TOK_BLOCK, n_a_blocks, lat_seq), 0, 0)

    def expert_of(c, j, *_):
        return (jnp.clip(j - first_expert_step, 0, N_EXPERTS - 1), 0, 0)

    grid_spec = pltpu.PrefetchScalarGridSpec(
        num_scalar_prefetch=2,
        grid=(n_chunks, MOE_STEPS),
        in_specs=_two_stream_specs(MOE_TOK_BLOCK, n_a_blocks, token_block) + [
                  pl.BlockSpec((None, None, SUBLANES, D_MODEL), mod_of),
                  pl.BlockSpec((2 * MOE_CHUNK,), lambda c, j, *_: (c,), memory_space=pltpu.SMEM),
                  pl.BlockSpec((2 * MOE_CHUNK,), lambda c, j, *_: (c,), memory_space=pltpu.SMEM),
                  pl.BlockSpec((None, D_MODEL, 2 * D_FF_EXPERT), expert_of),
                  pl.BlockSpec((None, D_FF_EXPERT, D_MODEL), expert_of),
                  _layer_block(g, l), _layer_block(b, l)],
        out_specs=_two_stream_specs(MOE_TOK_BLOCK, n_a_blocks, out_token_block),
        scratch_shapes=[pltpu.VMEM((MOE_TOK_BLOCK * SUBLANES, LANES), F32),
                        pltpu.VMEM((MOE_ROWS * SUBLANES, LANES), F32)],
    )
    return pl.pallas_call(
        functools.partial(_moe_kernel, n_a=n_a),
        grid_spec=grid_spec,
        out_shape=[jax.ShapeDtypeStruct(xa.shape, F32), jax.ShapeDtypeStruct(xb.shape, F32)],
        compiler_params=pltpu.CompilerParams(
            dimension_semantics=("arbitrary", "arbitrary"), vmem_limit_bytes=VMEM_LIMIT),
        name="moe_experts",
    )(start[:, :, 0].reshape(-1), padded[:, :, 0].reshape(-1), xa, xb, mods,
      dest.reshape(-1), gates.reshape(-1), w13, w2, g, b)


def _rope_tables(n_tokens):
    t = np.arange(n_tokens)
    row = (t // GRID_W).astype(np.float32)
    col = (t % GRID_W).astype(np.float32)
    inv_freq = (np.float32(ROPE_THETA) ** (-np.arange(0, AXIS_ROT, 2, dtype=np.float32) / AXIS_ROT)).astype(np.float32)
    ang_r = row[:, None] * inv_freq
    ang_c = col[:, None] * inv_freq
    cos = np.concatenate([np.cos(ang_r), np.cos(ang_r), np.cos(ang_c), np.cos(ang_c)], axis=1)
    sin = np.concatenate([-np.sin(ang_r), np.sin(ang_r), -np.sin(ang_c), np.sin(ang_c)], axis=1)
    return jnp.asarray(np.tile(cos, (1, 2)), F32), jnp.asarray(np.tile(sin, (1, 2)), F32)


def kernel(x_prompt, x_sample, cache_k, cache_v, c, c_ctx, ada_w, ada_b, w_in, q_norm_g, k_norm_g, conv_w, conv_b, sgu_norm_g, sgu_w, sgu_b, w_out, ln1_g, ln1_b, ln2_g, ln2_b, ffn_w1, ffn_w3, ffn_w2, router_w, moe_w1, moe_w3, moe_w2):
    batch, seq, _ = x_prompt.shape
    dec_batch, dec_seq, _ = x_sample.shape
    past_len = cache_k.shape[2]
    n_ctx = batch * seq
    n_lat = dec_batch * dec_seq
    assert DEPTH == 2 and 1 + dec_batch <= SUBLANES
    assert seq == ATT_BLOCK and CTX_SEQS_PER_STEP * seq == ROW_BLOCK and batch % CTX_SEQS_PER_STEP == 0
    assert dec_seq % ROW_BLOCK == 0 and dec_seq & (dec_seq - 1) == 0
    assert n_ctx % MOE_CHUNK == 0 and n_lat % MOE_CHUNK == 0
    assert dec_seq % ROUTER_BLOCK == 0 and dec_seq % MOE_TOK_BLOCK == 0 and dec_seq % FFN_TILE == 0

    cond = jnp.zeros((SUBLANES, D_MODEL), F32).at[0].set(c_ctx).at[1:1 + dec_batch].set(c)
    mod = _modulation(cond, ada_w, ada_b)
    mod = mod.reshape(DEPTH, SUBLANES, 6, D_MODEL)[:, :1 + dec_batch]
    mod = jnp.pad(mod, ((0, 0), (0, 0), (0, SUBLANES - 6), (0, 0)))

    lane_id = np.arange(GROUP_TILE) // HEAD_DIM
    ones_bd = jnp.asarray(lane_id[:, None] == lane_id[None, :], BF16)
    cos, sin = _rope_tables(dec_seq)
    small = (
        jnp.tile(q_norm_g, (1, N_Q_HEADS))[:, None, :], jnp.tile(k_norm_g, (1, N_KV_HEADS))[:, None, :],
        ones_bd,
        conv_w, conv_b[:, None, :], sgu_norm_g[:, None, :], sgu_w,
        jnp.repeat(jnp.swapaxes(sgu_b, 1, 2), HEAD_DIM, axis=2),
        ln1_g[:, None, :], ln1_b[:, None, :],
    )
    g2, b2 = ln2_g[:, None, :], ln2_b[:, None, :]
    kc = cache_k.reshape(dec_batch, DEPTH, past_len, KV_WIDTH)
    vc = cache_v.reshape(dec_batch, DEPTH, past_len, KV_WIDTH)

    xs = [x_prompt.reshape(n_ctx, D_MODEL), x_sample.reshape(n_lat, D_MODEL)]
    assert ffn_w1.shape[0] == 1 and moe_w1.shape[0] == 1
    side_casts = [[(ffn_w1[0],), (ffn_w3[0],), (ffn_w2[0],), (moe_w2.reshape(-1, D_MODEL),)],
                  [(moe_w1.reshape(-1, D_FF_EXPERT), moe_w3.reshape(-1, D_FF_EXPERT))]]
    kv = None
    for l in range(DEPTH):
        x_ctx, k_ctx, v_ctx, w_in_bf, w_out_bf, *cast = _mixer_ctx(
            xs[0], l, batch, seq, mod, w_in, w_out, small, kv, side_casts[l])
        if l == 0:
            ffn_bf, w2_bf = cast[:3], cast[3]
        else:
            (w13_bf,) = cast
        kv = (k_ctx, v_ctx)
        x_lat = _mixer_lat(xs[1], l, dec_batch, dec_seq, mod, w_in_bf, w_out_bf, small, cos, sin, kc, vc)
        if l % 2 == 0:
            xs = _ffn_dense(x_ctx, x_lat, l, mod, *ffn_bf, g2, b2, dec_seq)
        else:
            ws = (w13_bf.reshape(N_EXPERTS, D_MODEL, 2 * D_FF_EXPERT), w2_bf.reshape(N_EXPERTS, D_FF_EXPERT, D_MODEL))
            xs = _ffn_moe(x_ctx, x_lat, l, mod, router_w[l // 2].T, *ws, g2, b2, dec_seq)
    y_p = xs[0].reshape(batch, seq, D_MODEL)
    y_s = xs[1].reshape(dec_batch, dec_seq, D_MODEL)
    new_k = kv[0].reshape(batch, DEPTH, seq, N_KV_HEADS, HEAD_DIM)
    new_v = kv[1].reshape(batch, DEPTH, seq, N_KV_HEADS, HEAD_DIM)
    return (y_p, y_s, new_k, new_v)
```

```python
import functools

import numpy as np
import jax
import jax.numpy as jnp
from jax import lax
from jax.experimental import pallas as pl
from jax.experimental.pallas import tpu as pltpu

F32 = jnp.float32
BF16 = jnp.bfloat16
I32 = jnp.int32

D_MODEL = 1024
DEPTH = 2
GRID_W = 64
HEAD_DIM = 64
N_Q_HEADS = 8
N_KV_HEADS = 2
ATTN_WIDTH = N_Q_HEADS * HEAD_DIM
KV_WIDTH = N_KV_HEADS * HEAD_DIM
ATTN_SCALE = HEAD_DIM ** -0.5
ROPE_THETA = 10000.0
AXIS_ROT = HEAD_DIM // 2
CONV_WIDTH = 256
SGU_WIDTH = 256
SGU_HEADS = 4
CHUNK = 128
IN_WIDTH = 2048
D_FF = 2816
N_EXPERTS = 8
D_FF_EXPERT = 1408
EPS = 1e-6
DEEPNORM_ALPHA = (2 * DEPTH) ** 0.25

LANES = 128
SUBLANES = 8
ROW_BLOCK = 512
GROUP_TILE = 256
ATT_BLOCK = 256
CTX_SEQS_PER_STEP = 2
FFN_TILE = 512
FFN_COLS = 256
MOE_CHUNK = 2048
ROUTER_BLOCK = 1024
MOE_GRAN = 128
MOE_BLOCK = 256
MOE_TOK_BLOCK = 512
MOE_TOK_STEPS = MOE_CHUNK // MOE_TOK_BLOCK
MOE_STEPS = 2 * MOE_TOK_STEPS + N_EXPERTS
MOE_ROWS = 2 * MOE_CHUNK + N_EXPERTS * MOE_GRAN
VMEM_LIMIT = 58 * 1024 * 1024

_Q0, _K0, _V0, _CI0, _CB0, _CC0, _SU0, _SV0 = 0, 512, 640, 768, 1024, 1280, 1536, 1792


def _dot(a, b):
    return jnp.dot(a, b, preferred_element_type=F32)


def _dot_nt(a, b):
    return lax.dot_general(a, b, (((1,), (1,)), ((), ())), preferred_element_type=F32)


def _split(x):
    hi = x.astype(BF16)
    lo = (x - hi.astype(F32)).astype(BF16)
    return hi, lo


def _group_sum(x, ones_bd):
    outs = []
    for c0 in range(0, x.shape[1], GROUP_TILE):
        width = min(GROUP_TILE, x.shape[1] - c0)
        outs.append(_dot(x[:, c0:c0 + width].astype(BF16), ones_bd[:width, :width]))
    return outs[0] if len(outs) == 1 else jnp.concatenate(outs, axis=1)


def _standardize(x):
    mu = jnp.mean(x, axis=-1, keepdims=True)
    d = x - mu
    return d * lax.rsqrt(jnp.mean(d * d, axis=-1, keepdims=True) + EPS)


def _silu(x):
    return x / (1.0 + jnp.exp(-x))


def _modulation_kernel(cond_ref, w_ref, b_ref, o_ref):
    s_hi, s_lo = _split(_silu(cond_ref[...]))
    w_hi, w_lo = _split(w_ref[...])
    o_ref[...] = _dot(s_hi, w_hi) + _dot(s_hi, w_lo) + _dot(s_lo, w_hi) + b_ref[...]


def _modulation(cond, ada_w, ada_b):
    n_out = ada_w.shape[-1]
    tn = 1536
    return pl.pallas_call(
        _modulation_kernel,
        grid=(DEPTH, n_out // tn),
        in_specs=[
            pl.BlockSpec((SUBLANES, D_MODEL), lambda l, j: (0, 0)),
            pl.BlockSpec((None, D_MODEL, tn), lambda l, j: (l, 0, j)),
            pl.BlockSpec((None, 1, tn), lambda l, j: (l, 0, j)),
        ],
        out_specs=pl.BlockSpec((None, SUBLANES, tn), lambda l, j: (l, 0, j)),
        out_shape=jax.ShapeDtypeStruct((DEPTH, SUBLANES, n_out), F32),
        compiler_params=pltpu.CompilerParams(
            dimension_semantics=("arbitrary", "arbitrary"), vmem_limit_bytes=VMEM_LIMIT),
        name="modulation",
    )(cond, ada_w, ada_b.reshape(DEPTH, 1, n_out))


def _rope(x, cos, sin_signed):
    w = x.shape[1]
    lane = lax.broadcasted_iota(I32, x.shape, 1)
    first_half = (lane & 31) < 16
    partner = jnp.where(first_half, pltpu.roll(x, w - 16, 1), pltpu.roll(x, 16, 1))
    return x * cos + partner * sin_signed


def _head_variants(x):
    lane = lax.broadcasted_iota(I32, x.shape, 1)
    lo = lane < HEAD_DIM
    xr = pltpu.roll(x, HEAD_DIM, 1)
    zero = jnp.zeros_like(x)
    return (jnp.where(lo, x, zero).astype(BF16), jnp.where(lo, zero, xr).astype(BF16),
            jnp.where(lo, xr, zero).astype(BF16), jnp.where(lo, zero, x).astype(BF16))


def _mixer_kernel(*refs, seq, n_seq, n_cache, rope, cast_weights, stack_kv, side_jobs=()):
    refs = list(refs)

    def take(n):
        out, refs[:] = refs[:n], refs[n:]
        return out

    x_ref, mod_ref, win_ref, wout_ref = take(4)
    qg_ref, kg_ref, ones_ref, convw_ref, convb_ref, sgug_ref, sguw_ref, sgub_ref, ln1g_ref, ln1b_ref = take(10)
    if rope:
        cos_ref, sin_ref, kc_ref, vc_ref = take(4)
    if stack_kv:
        kprev_ref, vprev_ref = take(2)
    side_in = [take(n_src) for n_src in side_jobs]
    (x1_ref,) = take(1)
    if not rope:
        k_ref, v_ref = take(2)
    if cast_weights:
        winb_ref, woutb_ref = take(2)

        @pl.when(pl.program_id(0) == 0)
        def _cast():
            winb_ref[...] = win_ref[...].astype(BF16)
            woutb_ref[...] = wout_ref[...].astype(BF16)
        win_ref, wout_ref = winb_ref, woutb_ref
    for srcs, dst in zip(side_in, take(len(side_jobs))):
        col = 0
        for src in srcs:
            dst[:, col:col + src.shape[1]] = src[...].astype(BF16)
            col += src.shape[1]
    q_scr, kvar_scr, vvar_scr, u_scr, cb_scr, su_scr, vn_scr, mix_scr = take(8)
    n_rows = n_seq * seq
    assert n_cache == 0 or n_seq == 1

    def loop(n, body):
        if n == 1:
            body(0)
        else:
            def step(r, carry):
                body(r)
                return carry
            lax.fori_loop(0, n, step, 0, unroll=2)

    def block(r, size):
        if isinstance(r, int):
            return pl.ds(r * size, size)
        return pl.ds(pl.multiple_of(r * size, size), size)

    if n_cache:
        for i, var in enumerate(_head_variants(kc_ref[...])):
            kvar_scr[i, pl.ds(seq, n_cache), :] = var
        for i, var in enumerate(_head_variants(vc_ref[...])):
            vvar_scr[i, pl.ds(seq, n_cache), :] = var

    def project(r):
        rows = block(r, ROW_BLOCK)
        x = x_ref[rows, :]
        h = _standardize(x) * (1.0 + mod_ref[1:2, :]) + mod_ref[0:1, :]
        z = _dot(h.astype(BF16), win_ref[...])
        ones_bd = ones_ref[...]
        zq = z[:, _Q0:_K0]
        q = zq * lax.rsqrt(_group_sum(zq * zq, ones_bd) * (1.0 / HEAD_DIM) + EPS) * qg_ref[...]
        zk = z[:, _K0:_V0]
        k = zk * lax.rsqrt(_group_sum(zk * zk, ones_bd) * (1.0 / HEAD_DIM) + EPS) * kg_ref[...]
        v = z[:, _V0:_CI0]
        if rope:
            cos = cos_ref[rows, :]
            sin = sin_ref[rows, :]
            q = _rope(q, jnp.concatenate([cos] * 4, axis=1), jnp.concatenate([sin] * 4, axis=1))
            k = _rope(k, cos, sin)
        elif stack_kv:
            for s in range(ROW_BLOCK // seq):
                sub = slice(s * seq, (s + 1) * seq)
                k_ref[s, 0] = kprev_ref[sub, :]
                v_ref[s, 0] = vprev_ref[sub, :]
                k_ref[s, 1] = k[sub, :]
                v_ref[s, 1] = v[sub, :]
        else:
            k_ref[rows, :] = k
            v_ref[rows, :] = v
        q_scr[rows, :] = (q * ATTN_SCALE).astype(BF16)
        for i, var in enumerate(_head_variants(k)):
            kvar_scr[i, rows, :] = var
        for i, var in enumerate(_head_variants(v)):
            vvar_scr[i, rows, :] = var
        u_scr[rows, :] = z[:, _CC0:_SU0] * z[:, _CI0:_CB0]
        cb_scr[rows, :] = z[:, _CB0:_CC0]
        su_scr[rows, :] = z[:, _SU0:_SV0]
        sv = z[:, _SV0:IN_WIDTH]
        d = sv - _group_sum(sv, ones_bd) * (1.0 / HEAD_DIM)
        vn = d * lax.rsqrt(_group_sum(d * d, ones_bd) * (1.0 / HEAD_DIM) + EPS) * sgug_ref[...]
        vn_scr[rows, :] = vn.astype(BF16)

    loop(n_rows // ROW_BLOCK, project)

    u = u_scr[...]
    pos = lax.broadcasted_iota(I32, u.shape, 0) & (seq - 1)
    up = jnp.where(pos == 0, 0.0, pltpu.roll(u, 1, 0))
    dn = jnp.where(pos == seq - 1, 0.0, pltpu.roll(u, n_rows - 1, 0))
    conv = up * convw_ref[0:1, :] + u * convw_ref[1:2, :] + dn * convw_ref[2:3, :] + convb_ref[...]
    mix_scr[:, ATTN_WIDTH:ATTN_WIDTH + CONV_WIDTH] = (cb_scr[...] * conv).astype(BF16)

    for n in range(n_rows // CHUNK):
        rows = pl.ds(n * CHUNK, CHUNK)
        vn = vn_scr[rows, :]
        lane = lax.broadcasted_iota(I32, vn.shape, 1)
        per_head = [jnp.where((lane >= hd * HEAD_DIM) & (lane < (hd + 1) * HEAD_DIM), vn, jnp.zeros_like(vn))
                    for hd in range(SGU_HEADS)]
        s = sgub_ref[...] + _dot(sguw_ref[...], jnp.concatenate(per_head, axis=0))
        mix_scr[rows, ATTN_WIDTH + CONV_WIDTH:] = (su_scr[rows, :] * s).astype(BF16)

    def attend(s, r):
        rows = block(s * (seq // ATT_BLOCK) + r, ATT_BLOCK)
        keys = pl.ds(s * seq, seq + n_cache)
        for pair in range(N_Q_HEADS // 2):
            qp = q_scr[rows, pair * LANES:(pair + 1) * LANES]
            kv = pair // (N_Q_HEADS // N_KV_HEADS // 2)
            acc = jnp.zeros((ATT_BLOCK, LANES), F32)
            for parity in range(2):
                sc = _dot_nt(qp, kvar_scr[2 * kv + parity, keys, :])
                p = jnp.exp(sc - jnp.max(sc, axis=1, keepdims=True))
                denom = jnp.sum(p, axis=1, keepdims=True)
                acc = acc + _dot(p.astype(BF16), vvar_scr[2 * kv + parity, keys, :]) / denom
            mix_scr[rows, pair * LANES:(pair + 1) * LANES] = acc.astype(BF16)

    for s in range(n_seq):
        loop(seq // ATT_BLOCK, functools.partial(attend, s))

    def finish(r):
        rows = block(r, ROW_BLOCK)
        mix = _dot(mix_scr[rows, :], wout_ref[...])
        y = DEEPNORM_ALPHA * x_ref[rows, :] + mod_ref[2:3, :] * mix
        x1_ref[rows, :] = _standardize(y) * ln1g_ref[...] + ln1b_ref[...]

    loop(n_rows // ROW_BLOCK, finish)


def _full(shape):
    n = len(shape)
    return pl.BlockSpec(shape, lambda *_: (0,) * n)


def _resident(shape):
    n = len(shape)
    return pl.BlockSpec(shape, lambda *_: (0,) * n, pipeline_mode=pl.Buffered(1))


def _layer_block(arr, l, resident=False):
    shape = arr.shape[1:]
    kw = dict(pipeline_mode=pl.Buffered(1)) if resident else {}
    return pl.BlockSpec((None,) + shape, lambda *_: (l,) + (0,) * len(shape), **kw)


def _mixer_scratch(n_rows, n_cache):
    nk = n_rows + n_cache
    return [
        pltpu.VMEM((n_rows, ATTN_WIDTH), BF16),
        pltpu.VMEM((4, nk, LANES), BF16),
        pltpu.VMEM((4, nk, LANES), BF16),
        pltpu.VMEM((n_rows, CONV_WIDTH), F32),
        pltpu.VMEM((n_rows, CONV_WIDTH), F32),
        pltpu.VMEM((n_rows, SGU_WIDTH), F32),
        pltpu.VMEM((n_rows, SGU_WIDTH), BF16),
        pltpu.VMEM((n_rows, D_MODEL), BF16),
    ]


def _mixer_ctx(x, l, n_seq, seq, mod, w_in, w_out, small, kv_prev=None, side_casts=()):
    per_step = CTX_SEQS_PER_STEP
    rows = per_step * seq
    stack_kv = kv_prev is not None
    n_steps = n_seq // per_step
    kernel = functools.partial(_mixer_kernel, seq=seq, n_seq=per_step, n_cache=0, rope=False,
                               cast_weights=True, stack_kv=stack_kv,
                               side_jobs=tuple(len(job) for job in side_casts))
    in_specs = ([pl.BlockSpec((rows, D_MODEL), lambda i: (i, 0)),
                 pl.BlockSpec((None, None, SUBLANES, D_MODEL), lambda i: (l, 0, 0, 0)),
                 _layer_block(w_in, l, resident=True), _layer_block(w_out, l, resident=True)]
                + [_full(a.shape) if a.ndim == 2 else _layer_block(a, l) for a in small])
    args = [x, mod, w_in, w_out, *small]
    if stack_kv:
        in_specs += [pl.BlockSpec((rows, KV_WIDTH), lambda i: (i, 0))] * 2
        args += list(kv_prev)
        kv_spec = pl.BlockSpec((per_step, DEPTH, seq, KV_WIDTH), lambda i: (i, 0, 0, 0))
        kv_shape = jax.ShapeDtypeStruct((n_seq, DEPTH, seq, KV_WIDTH), F32)
    else:
        kv_spec = pl.BlockSpec((rows, KV_WIDTH), lambda i: (i, 0))
        kv_shape = jax.ShapeDtypeStruct((n_seq * seq, KV_WIDTH), F32)
    side_out_specs, side_out_shapes = [], []
    for job in side_casts:
        n_rows = job[0].shape[0]
        n_cols = sum(src.shape[1] for src in job)
        for src in job:
            in_specs.append(pl.BlockSpec((n_rows // n_steps, src.shape[1]), lambda i: (i, 0)))
            args.append(src)
        side_out_specs.append(pl.BlockSpec((n_rows // n_steps, n_cols), lambda i: (i, 0)))
        side_out_shapes.append(jax.ShapeDtypeStruct((n_rows, n_cols), BF16))
    return pl.pallas_call(
        kernel,
        grid=(n_steps,),
        in_specs=in_specs,
        out_specs=[pl.BlockSpec((rows, D_MODEL), lambda i: (i, 0)), kv_spec, kv_spec,
                   _full((D_MODEL, IN_WIDTH)), _full((D_MODEL, D_MODEL))] + side_out_specs,
        out_shape=[jax.ShapeDtypeStruct((n_seq * seq, D_MODEL), F32), kv_shape, kv_shape,
                   jax.ShapeDtypeStruct((D_MODEL, IN_WIDTH), BF16), jax.ShapeDtypeStruct((D_MODEL, D_MODEL), BF16)]
        + side_out_shapes,
        scratch_shapes=_mixer_scratch(rows, 0),
        compiler_params=pltpu.CompilerParams(dimension_semantics=("arbitrary",), vmem_limit_bytes=VMEM_LIMIT),
        name="mixer_ctx",
    )(*args)


def _mixer_lat(x, l, n_seq, seq, mod, w_in_bf, w_out_bf, small, cos, sin, kc, vc):
    n_cache = kc.shape[2]
    kernel = functools.partial(_mixer_kernel, seq=seq, n_seq=1, n_cache=n_cache, rope=True,
                               cast_weights=False, stack_kv=False)
    cache_spec = pl.BlockSpec((None, None, n_cache, KV_WIDTH), lambda b: (b, l, 0, 0))
    return pl.pallas_call(
        kernel,
        grid=(n_seq,),
        in_specs=([pl.BlockSpec((seq, D_MODEL), lambda b: (b, 0)),
                   pl.BlockSpec((None, None, SUBLANES, D_MODEL), lambda b: (l, 1 + b, 0, 0)),
                   _resident((D_MODEL, IN_WIDTH)), _resident((D_MODEL, D_MODEL))]
                  + [_full(a.shape) if a.ndim == 2 else _layer_block(a, l) for a in small]
                  + [_full((seq, LANES)), _full((seq, LANES)), cache_spec, cache_spec]),
        out_specs=pl.BlockSpec((seq, D_MODEL), lambda b: (b, 0)),
        out_shape=jax.ShapeDtypeStruct((n_seq * seq, D_MODEL), F32),
        scratch_shapes=_mixer_scratch(seq, n_cache),
        compiler_params=pltpu.CompilerParams(dimension_semantics=("arbitrary",), vmem_limit_bytes=VMEM_LIMIT),
        name="mixer_lat",
    )(x, mod, w_in_bf, w_out_bf, *small, cos, sin, kc, vc)


def _two_stream_specs(block_rows, n_a, block_of, single_buffer_b=False):
    kw = dict(pipeline_mode=pl.Buffered(1)) if single_buffer_b else {}
    spec_a = pl.BlockSpec((block_rows, D_MODEL), lambda *ids: (jnp.minimum(block_of(*ids), n_a - 1), 0))
    spec_b = pl.BlockSpec((block_rows, D_MODEL), lambda *ids: (jnp.maximum(block_of(*ids) - n_a, 0), 0), **kw)
    return [spec_a, spec_b]


def _group_of_block(blk, block_rows, n_a, lat_seq):
    return jnp.where(blk < n_a, 0, 1 + (blk - n_a) // (lat_seq // block_rows))


def _ffn_kernel(xa_ref, xb_ref, mod_ref, w1_ref, w3_ref, w2_ref, g_ref, b_ref, oa_ref, ob_ref, *, n_a):
    is_a = pl.program_id(0) < n_a
    x = jnp.where(is_a, xa_ref[...], xb_ref[...])
    h = (_standardize(x) * (1.0 + mod_ref[4:5, :]) + mod_ref[3:4, :]).astype(BF16)
    acc = jnp.zeros(x.shape, F32)
    for c in range(D_FF // FFN_COLS):
        cols = slice(c * FFN_COLS, (c + 1) * FFN_COLS)
        a = _dot(h, w1_ref[:, cols])
        b = _dot(h, w3_ref[:, cols])
        acc = acc + _dot((_silu(a) * b).astype(BF16), w2_ref[cols, :])
    y = DEEPNORM_ALPHA * x + mod_ref[5:6, :] * acc
    out = _standardize(y) * g_ref[...] + b_ref[...]

    @pl.when(is_a)
    def _():
        oa_ref[...] = out

    @pl.when(jnp.logical_not(is_a))
    def _():
        ob_ref[...] = out


def _ffn_dense(xa, xb, l, mods, w1, w3, w2, g, b, lat_seq):
    n_a, n_b = xa.shape[0] // FFN_TILE, xb.shape[0] // FFN_TILE
    x_specs = _two_stream_specs(FFN_TILE, n_a, lambda t: t)
    return pl.pallas_call(
        functools.partial(_ffn_kernel, n_a=n_a),
        grid=(n_a + n_b,),
        in_specs=x_specs + [
            pl.BlockSpec((None, None, SUBLANES, D_MODEL),
                         lambda t: (l, _group_of_block(t, FFN_TILE, n_a, lat_seq), 0, 0)),
            _resident(w1.shape), _resident(w3.shape), _resident(w2.shape), _layer_block(g, l), _layer_block(b, l)],
        out_specs=x_specs,
        out_shape=[jax.ShapeDtypeStruct(xa.shape, F32), jax.ShapeDtypeStruct(xb.shape, F32)],
        compiler_params=pltpu.CompilerParams(dimension_semantics=("arbitrary",), vmem_limit_bytes=VMEM_LIMIT),
        name="ffn_dense",
    )(xa, xb, mods, w1, w3, w2, g, b)


def _router_kernel(xa_ref, xb_ref, mod_ref, rw_ref, before_ref, dest_ref, gate_ref, start_ref, padded_ref, *,
                   n_a, lat_seq):
    c = pl.program_id(0)
    w_hi, w_lo = _split(rw_ref[...])
    n_blocks = MOE_CHUNK // ROUTER_BLOCK
    parts = []
    for blk in range(n_blocks):
        rows = pl.ds(blk * ROUTER_BLOCK, ROUTER_BLOCK)
        mod = mod_ref[_group_of_block(c * n_blocks + blk, ROUTER_BLOCK, n_a * n_blocks, lat_seq)]
        x = jnp.where(c < n_a, xa_ref[rows, :], xb_ref[rows, :])
        h = _standardize(x) * (1.0 + mod[4:5, :]) + mod[3:4, :]
        h_hi, h_lo = _split(h)
        parts.append(_dot_nt(w_hi, h_hi) + _dot_nt(w_hi, h_lo) + _dot_nt(w_lo, h_hi))
    logits = jnp.concatenate(parts, axis=1)
    eid = lax.broadcasted_iota(I32, logits.shape, 0).astype(F32)
    m1 = jnp.max(logits, axis=0, keepdims=True)
    i1 = jnp.min(jnp.where(logits == m1, eid, float(N_EXPERTS)), axis=0, keepdims=True)
    oh1 = eid == i1
    rest = jnp.where(oh1, -jnp.inf, logits)
    m2 = jnp.max(rest, axis=0, keepdims=True)
    i2 = jnp.min(jnp.where(rest == m2, eid, float(N_EXPERTS)), axis=0, keepdims=True)
    oh2 = eid == i2
    e = jnp.exp(m2 - m1)
    gate_ref[0:1, :] = 1.0 / (1.0 + e)
    gate_ref[1:2, :] = e / (1.0 + e)
    sel = jnp.where(oh1 | oh2, 1.0, 0.0)
    ranks = []
    seen = jnp.zeros((N_EXPERTS, 1), F32)
    for blk in range(n_blocks):
        s_blk = sel[:, blk * ROUTER_BLOCK:(blk + 1) * ROUTER_BLOCK]
        ranks.append(_dot(s_blk.astype(BF16), before_ref[...]) + seen)
        seen = seen + jnp.sum(s_blk, axis=1, keepdims=True)
    rank = jnp.concatenate(ranks, axis=1)
    eid_out = lax.broadcasted_iota(I32, start_ref.shape, 0).astype(F32)
    start = jnp.zeros(sel.shape, F32)
    start_out = jnp.zeros(start_ref.shape, F32)
    padded_out = jnp.zeros(start_ref.shape, F32)
    for ex in range(N_EXPERTS):
        cnt = jnp.sum(sel[ex:ex + 1, :], axis=1, keepdims=True)
        padded = jnp.ceil(cnt * (1.0 / MOE_GRAN)) * MOE_GRAN
        start = start + jnp.where(eid > ex, padded, 0.0)
        start_out = start_out + jnp.where(eid_out > ex, padded, 0.0)
        padded_out = padded_out + jnp.where(eid_out == ex, padded, 0.0)
    row = (start + rank) * SUBLANES
    dest_ref[0:1, :] = jnp.sum(jnp.where(oh1, row, 0.0), axis=0, keepdims=True).astype(I32)
    dest_ref[1:2, :] = jnp.sum(jnp.where(oh2, row, 0.0), axis=0, keepdims=True).astype(I32)
    start_ref[...] = start_out.astype(I32)
    padded_ref[...] = padded_out.astype(I32)


def _router(xa, xb, l, mods, rw_t, lat_seq):
    n_a = xa.shape[0] // MOE_CHUNK
    n_chunks = n_a + xb.shape[0] // MOE_CHUNK
    tok = np.arange(ROUTER_BLOCK)
    before = jnp.asarray(tok[:, None] < tok[None, :], BF16)
    return pl.pallas_call(
        functools.partial(_router_kernel, n_a=n_a, lat_seq=lat_seq),
        grid=(n_chunks,),
        in_specs=_two_stream_specs(MOE_CHUNK, n_a, lambda c: c) + [
                  _layer_block(mods, l),
                  _full((N_EXPERTS, D_MODEL)), _full((ROUTER_BLOCK, ROUTER_BLOCK))],
        out_specs=[pl.BlockSpec((None, 2, MOE_CHUNK), lambda c: (c, 0, 0)),
                   pl.BlockSpec((None, 2, MOE_CHUNK), lambda c: (c, 0, 0)),
                   pl.BlockSpec((None, N_EXPERTS, LANES), lambda c: (c, 0, 0)),
                   pl.BlockSpec((None, N_EXPERTS, LANES), lambda c: (c, 0, 0))],
        out_shape=[jax.ShapeDtypeStruct((n_chunks, 2, MOE_CHUNK), I32),
                   jax.ShapeDtypeStruct((n_chunks, 2, MOE_CHUNK), F32),
                   jax.ShapeDtypeStruct((n_chunks, N_EXPERTS, LANES), I32),
                   jax.ShapeDtypeStruct((n_chunks, N_EXPERTS, LANES), I32)],
        compiler_params=pltpu.CompilerParams(dimension_semantics=("arbitrary",), vmem_limit_bytes=VMEM_LIMIT),
        name="moe_router",
    )(xa, xb, mods, rw_t, before)


def _token_rows(first_row):
    return pl.ds(pl.multiple_of(first_row, SUBLANES), SUBLANES)


def _store_token_major(ref, tok0, val):
    for cc in range(D_MODEL // LANES):
        ref[pl.ds(tok0 * SUBLANES + cc, val.shape[0], stride=SUBLANES), :] = val[:, cc * LANES:(cc + 1) * LANES]


def _load_token_major(ref, tok0, n):
    return jnp.concatenate(
        [ref[pl.ds(tok0 * SUBLANES + cc, n, stride=SUBLANES), :] for cc in range(D_MODEL // LANES)], axis=1)


def _moe_kernel(start_ref, padded_ref, xa_ref, xb_ref, mod_ref, dest_ref, gate_ref, w13_ref, w2_ref,
                g_ref, b_ref, oa_ref, ob_ref, tok_scr, rows_scr, *, n_a):
    c = pl.program_id(0)
    j = pl.program_id(1)
    is_a = c < n_a
    first_expert_step = MOE_TOK_STEPS
    first_combine_step = MOE_TOK_STEPS + N_EXPERTS

    @pl.when((c == 0) & (j == 0))
    def _init():
        rows_scr[...] = jnp.zeros(rows_scr.shape, F32)

    @pl.when(j < first_expert_step)
    def _dispatch():
        for r in range(MOE_TOK_BLOCK // ROW_BLOCK):
            rows = pl.ds(r * ROW_BLOCK, ROW_BLOCK)
            x = jnp.where(is_a, xa_ref[rows, :], xb_ref[rows, :])
            h = _standardize(x) * (1.0 + mod_ref[4:5, :]) + mod_ref[3:4, :]
            _store_token_major(tok_scr, r * ROW_BLOCK, h)
        t0 = j * MOE_TOK_BLOCK

        def body(t, carry):
            row = tok_scr[_token_rows(t * SUBLANES), :]
            rows_scr[_token_rows(dest_ref[t0 + t]), :] = row
            rows_scr[_token_rows(dest_ref[MOE_CHUNK + t0 + t]), :] = row
            return carry
        lax.fori_loop(0, MOE_TOK_BLOCK, body, 0, unroll=8)

    def experts(row0, m):
        xin = _load_token_major(rows_scr, row0, m).astype(BF16)
        ab = _dot(xin, w13_ref[...])
        a, b = ab[:, :D_FF_EXPERT], ab[:, D_FF_EXPERT:]
        y = _dot((_silu(a) * b).astype(BF16), w2_ref[...])
        _store_token_major(rows_scr, row0, y)

    @pl.when((j >= first_expert_step) & (j < first_combine_step))
    def _experts():
        region = c * N_EXPERTS + (j - first_expert_step)
        start = start_ref[region]
        padded = padded_ref[region]
        n_full = lax.shift_right_logical(padded, MOE_BLOCK.bit_length() - 1)

        def body(i, carry):
            experts(start + i * MOE_BLOCK, MOE_BLOCK)
            return carry
        lax.fori_loop(0, n_full, body, 0)
        for m in range(MOE_GRAN, MOE_BLOCK, MOE_GRAN):
            pl.when(padded - n_full * MOE_BLOCK == m)(
                functools.partial(experts, start + n_full * MOE_BLOCK, m))

    @pl.when(j >= first_combine_step)
    def _combine():
        t0 = (j - first_combine_step) * MOE_TOK_BLOCK

        def body(t, carry):
            y0 = rows_scr[_token_rows(dest_ref[t0 + t]), :]
            y1 = rows_scr[_token_rows(dest_ref[MOE_CHUNK + t0 + t]), :]
            tok_scr[_token_rows(t * SUBLANES), :] = gate_ref[t0 + t] * y0 + gate_ref[MOE_CHUNK + t0 + t] * y1
            return carry
        lax.fori_loop(0, MOE_TOK_BLOCK, body, 0, unroll=8)
        for r in range(MOE_TOK_BLOCK // ROW_BLOCK):
            rows = pl.ds(r * ROW_BLOCK, ROW_BLOCK)
            ffn = _load_token_major(tok_scr, r * ROW_BLOCK, ROW_BLOCK)
            x = jnp.where(is_a, xa_ref[rows, :], xb_ref[rows, :])
            y = DEEPNORM_ALPHA * x + mod_ref[5:6, :] * ffn
            out = _standardize(y) * g_ref[...] + b_ref[...]

            @pl.when(is_a)
            def _():
                oa_ref[rows, :] = out

            @pl.when(jnp.logical_not(is_a))
            def _():
                ob_ref[rows, :] = out


def _ffn_moe(xa, xb, l, mods, rw_t, w13, w2, g, b, lat_seq):
    n_a = xa.shape[0] // MOE_CHUNK
    n_chunks = n_a + xb.shape[0] // MOE_CHUNK
    n_a_blocks = n_a * MOE_TOK_STEPS
    dest, gates, start, padded = _router(xa, xb, l, mods, rw_t, lat_seq)
    first_expert_step = MOE_TOK_STEPS
    first_combine_step = MOE_TOK_STEPS + N_EXPERTS

    def token_block(c, j, *_):
        blk = jnp.where(j < first_combine_step, jnp.minimum(j, MOE_TOK_STEPS - 1), j - first_combine_step)
        return c * MOE_TOK_STEPS + blk

    def out_token_block(c, j, *_):
        return c * MOE_TOK_STEPS + jnp.maximum(j - first_combine_step, 0)

    def mod_of(c, j, *_):
        return (l, _group_of_block(token_block(c, j), MOE_TOK_BLOCK, n_a_blocks, lat_seq), 0, 0)

    def expert_of(c, j, *_):
        return (jnp.clip(j - first_expert_step, 0, N_EXPERTS - 1), 0, 0)

    grid_spec = pltpu.PrefetchScalarGridSpec(
        num_scalar_prefetch=2,
        grid=(n_chunks, MOE_STEPS),
        in_specs=_two_stream_specs(MOE_TOK_BLOCK, n_a_blocks, token_block) + [
                  pl.BlockSpec((None, None, SUBLANES, D_MODEL), mod_of),
                  pl.BlockSpec((2 * MOE_CHUNK,), lambda c, j, *_: (c,), memory_space=pltpu.SMEM),
                  pl.BlockSpec((2 * MOE_CHUNK,), lambda c, j, *_: (c,), memory_space=pltpu.SMEM),
                  pl.BlockSpec((None, D_MODEL, 2 * D_FF_EXPERT), expert_of),
                  pl.BlockSpec((None, D_FF_EXPERT, D_MODEL), expert_of),
                  _layer_block(g, l), _layer_block(b, l)],
        out_specs=_two_stream_specs(MOE_TOK_BLOCK, n_a_blocks, out_token_block),
        scratch_shapes=[pltpu.VMEM((MOE_TOK_BLOCK * SUBLANES, LANES), F32),
                        pltpu.VMEM((MOE_ROWS * SUBLANES, LANES), F32)],
    )
    return pl.pallas_call(
        functools.partial(_moe_kernel, n_a=n_a),
        grid_spec=grid_spec,
        out_shape=[jax.ShapeDtypeStruct(xa.shape, F32), jax.ShapeDtypeStruct(xb.shape, F32)],
        compiler_params=pltpu.CompilerParams(
            dimension_semantics=("arbitrary", "arbitrary"), vmem_limit_bytes=VMEM_LIMIT),
        name="moe_experts",
    )(start[:, :, 0].reshape(-1), padded[:, :, 0].reshape(-1), xa, xb, mods,
      dest.reshape(-1), gates.reshape(-1), w13, w2, g, b)


def _rope_tables(n_tokens):
    t = np.arange(n_tokens)
    row = (t // GRID_W).astype(np.float32)
    col = (t % GRID_W).astype(np.float32)
    inv_freq = (np.float32(ROPE_THETA) ** (-np.arange(0, AXIS_ROT, 2, dtype=np.float32) / AXIS_ROT)).astype(np.float32)
    ang_r = row[:, None] * inv_freq
    ang_c = col[:, None] * inv_freq
    cos = np.concatenate([np.cos(ang_r), np.cos(ang_r), np.cos(ang_c), np.cos(ang_c)], axis=1)
    sin = np.concatenate([-np.sin(ang_r), np.sin(ang_r), -np.sin(ang_c), np.sin(ang_c)], axis=1)
    return jnp.asarray(np.tile(cos, (1, 2)), F32), jnp.asarray(np.tile(sin, (1, 2)), F32)


def kernel(x_prompt, x_sample, cache_k, cache_v, c, c_ctx, ada_w, ada_b, w_in, q_norm_g, k_norm_g, conv_w, conv_b, sgu_norm_g, sgu_w, sgu_b, w_out, ln1_g, ln1_b, ln2_g, ln2_b, ffn_w1, ffn_w3, ffn_w2, router_w, moe_w1, moe_w3, moe_w2):
    batch, seq, _ = x_prompt.shape
    dec_batch, dec_seq, _ = x_sample.shape
    past_len = cache_k.shape[2]
    n_ctx = batch * seq
    n_lat = dec_batch * dec_seq
    assert DEPTH == 2 and 1 + dec_batch <= SUBLANES
    assert seq == ATT_BLOCK and CTX_SEQS_PER_STEP * seq == ROW_BLOCK and batch % CTX_SEQS_PER_STEP == 0
    assert dec_seq % ROW_BLOCK == 0 and dec_seq & (dec_seq - 1) == 0
    assert n_ctx % MOE_CHUNK == 0 and n_lat % MOE_CHUNK == 0
    assert dec_seq % ROUTER_BLOCK == 0 and dec_seq % MOE_TOK_BLOCK == 0 and dec_seq % FFN_TILE == 0

    cond = jnp.zeros((SUBLANES, D_MODEL), F32).at[0].set(c_ctx).at[1:1 + dec_batch].set(c)
    mod = _modulation(cond, ada_w, ada_b)
    mod = mod.reshape(DEPTH, SUBLANES, 6, D_MODEL)[:, :1 + dec_batch]
    mod = jnp.pad(mod, ((0, 0), (0, 0), (0, SUBLANES - 6), (0, 0)))

    lane_id = np.arange(GROUP_TILE) // HEAD_DIM
    ones_bd = jnp.asarray(lane_id[:, None] == lane_id[None, :], BF16)
    cos, sin = _rope_tables(dec_seq)
    small = (
        jnp.tile(q_norm_g, (1, N_Q_HEADS))[:, None, :], jnp.tile(k_norm_g, (1, N_KV_HEADS))[:, None, :],
        ones_bd,
        conv_w, conv_b[:, None, :], sgu_norm_g[:, None, :],
        jnp.swapaxes(sgu_w, 1, 2).reshape(DEPTH, CHUNK, SGU_HEADS * CHUNK).astype(BF16),
        jnp.repeat(jnp.swapaxes(sgu_b, 1, 2), HEAD_DIM, axis=2),
        ln1_g[:, None, :], ln1_b[:, None, :],
    )
    g2, b2 = ln2_g[:, None, :], ln2_b[:, None, :]
    kc = cache_k.reshape(dec_batch, DEPTH, past_len, KV_WIDTH)
    vc = cache_v.reshape(dec_batch, DEPTH, past_len, KV_WIDTH)

    xs = [x_prompt.reshape(n_ctx, D_MODEL), x_sample.reshape(n_lat, D_MODEL)]
    assert ffn_w1.shape[0] == 1 and moe_w1.shape[0] == 1
    side_casts = [[(ffn_w1[0],), (ffn_w3[0],), (ffn_w2[0],), (moe_w2.reshape(-1, D_MODEL),)],
                  [(moe_w1.reshape(-1, D_FF_EXPERT), moe_w3.reshape(-1, D_FF_EXPERT))]]
    kv = None
    for l in range(DEPTH):
        x_ctx, k_ctx, v_ctx, w_in_bf, w_out_bf, *cast = _mixer_ctx(
            xs[0], l, batch, seq, mod, w_in, w_out, small, kv, side_casts[l])
        if l == 0:
            ffn_bf, w2_bf = cast[:3], cast[3]
        else:
            (w13_bf,) = cast
        kv = (k_ctx, v_ctx)
        x_lat = _mixer_lat(xs[1], l, dec_batch, dec_seq, mod, w_in_bf, w_out_bf, small, cos, sin, kc, vc)
        if l % 2 == 0:
            xs = _ffn_dense(x_ctx, x_lat, l, mod, *ffn_bf, g2, b2, dec_seq)
        else:
            ws = (w13_bf.reshape(N_EXPERTS, D_MODEL, 2 * D_FF_EXPERT), w2_bf.reshape(N_EXPERTS, D_FF_EXPERT, D_MODEL))
            xs = _ffn_moe(x_ctx, x_lat, l, mod, router_w[l // 2].T, *ws, g2, b2, dec_seq)
    y_p = xs[0].reshape(batch, seq, D_MODEL)
    y_s = xs[1].reshape(dec_batch, dec_seq, D_MODEL)
    new_k = kv[0].reshape(batch, DEPTH, seq, N_KV_HEADS, HEAD_DIM)
    new_v = kv[1].reshape(batch, DEPTH, seq, N_KV_HEADS, HEAD_DIM)
    return (y_p, y_s, new_k, new_v)
```

```python
import functools

import numpy as np
import jax
import jax.numpy as jnp
from jax import lax
from jax.experimental import pallas as pl
from jax.experimental.pallas import tpu as pltpu

F32 = jnp.float32
BF16 = jnp.bfloat16
I32 = jnp.int32

D_MODEL = 1024
DEPTH = 2
GRID_W = 64
HEAD_DIM = 64
N_Q_HEADS = 8
N_KV_HEADS = 2
ATTN_WIDTH = N_Q_HEADS * HEAD_DIM
KV_WIDTH = N_KV_HEADS * HEAD_DIM
ATTN_SCALE = HEAD_DIM ** -0.5
ROPE_THETA = 10000.0
AXIS_ROT = HEAD_DIM // 2
CONV_WIDTH = 256
SGU_WIDTH = 256
SGU_HEADS = 4
CHUNK = 128
IN_WIDTH = 2048
D_FF = 2816
N_EXPERTS = 8
D_FF_EXPERT = 1408
EPS = 1e-6
DEEPNORM_ALPHA = (2 * DEPTH) ** 0.25

LANES = 128
SUBLANES = 8
ROW_BLOCK = 512
GROUP_TILE = 256
ATT_BLOCK = 256
CTX_SEQS_PER_STEP = 2
FFN_TILE = 512
FFN_COLS = 256
MOE_CHUNK = 2048
ROUTER_BLOCK = 1024
MOE_GRAN = 128
MOE_BLOCK = 256
MOE_TOK_BLOCK = 512
MOE_TOK_STEPS = MOE_CHUNK // MOE_TOK_BLOCK
MOE_STEPS = 2 * MOE_TOK_STEPS + N_EXPERTS
MOE_ROWS = 2 * MOE_CHUNK + N_EXPERTS * MOE_GRAN
VMEM_LIMIT = 58 * 1024 * 1024

_Q0, _K0, _V0, _CI0, _CB0, _CC0, _SU0, _SV0 = 0, 512, 640, 768, 1024, 1280, 1536, 1792


def _dot(a, b):
    return jnp.dot(a, b, preferred_element_type=F32)


def _dot_nt(a, b):
    return lax.dot_general(a, b, (((1,), (1,)), ((), ())), preferred_element_type=F32)


def _split(x):
    hi = x.astype(BF16)
    lo = (x - hi.astype(F32)).astype(BF16)
    return hi, lo


def _group_sum(x, ones_bd):
    outs = []
    for c0 in range(0, x.shape[1], GROUP_TILE):
        width = min(GROUP_TILE, x.shape[1] - c0)
        outs.append(_dot(x[:, c0:c0 + width].astype(BF16), ones_bd[:width, :width]))
    return outs[0] if len(outs) == 1 else jnp.concatenate(outs, axis=1)


def _standardize(x):
    mu = jnp.mean(x, axis=-1, keepdims=True)
    d = x - mu
    return d * lax.rsqrt(jnp.mean(d * d, axis=-1, keepdims=True) + EPS)


def _silu(x):
    return x / (1.0 + jnp.exp(-x))


def _modulation_kernel(cond_ref, w_ref, b_ref, o_ref):
    s_hi, s_lo = _split(_silu(cond_ref[...]))
    w_hi, w_lo = _split(w_ref[...])
    o_ref[...] = _dot(s_hi, w_hi) + _dot(s_hi, w_lo) + _dot(s_lo, w_hi) + b_ref[...]


def _modulation(cond, ada_w, ada_b):
    n_out = ada_w.shape[-1]
    tn = 1536
    return pl.pallas_call(
        _modulation_kernel,
        grid=(DEPTH, n_out // tn),
        in_specs=[
            pl.BlockSpec((SUBLANES, D_MODEL), lambda l, j: (0, 0)),
            pl.BlockSpec((None, D_MODEL, tn), lambda l, j: (l, 0, j)),
            pl.BlockSpec((None, 1, tn), lambda l, j: (l, 0, j)),
        ],
        out_specs=pl.BlockSpec((None, SUBLANES, tn), lambda l, j: (l, 0, j)),
        out_shape=jax.ShapeDtypeStruct((DEPTH, SUBLANES, n_out), F32),
        compiler_params=pltpu.CompilerParams(
            dimension_semantics=("arbitrary", "arbitrary"), vmem_limit_bytes=VMEM_LIMIT),
        name="modulation",
    )(cond, ada_w, ada_b.reshape(DEPTH, 1, n_out))


def _rope(x, cos, sin_signed):
    w = x.shape[1]
    lane = lax.broadcasted_iota(I32, x.shape, 1)
    first_half = (lane & 31) < 16
    partner = jnp.where(first_half, pltpu.roll(x, w - 16, 1), pltpu.roll(x, 16, 1))
    return x * cos + partner * sin_signed


def _head_variants(x):
    lane = lax.broadcasted_iota(I32, x.shape, 1)
    lo = lane < HEAD_DIM
    xr = pltpu.roll(x, HEAD_DIM, 1)
    zero = jnp.zeros_like(x)
    return (jnp.where(lo, x, zero).astype(BF16), jnp.where(lo, zero, xr).astype(BF16),
            jnp.where(lo, xr, zero).astype(BF16), jnp.where(lo, zero, x).astype(BF16))


def _mixer_kernel(*refs, seq, n_seq, n_cache, rope, cast_weights, stack_kv, side_jobs=(), n_steps=None):
    refs = list(refs)
    skew = n_steps is not None

    def take(n):
        out, refs[:] = refs[:n], refs[n:]
        return out

    (x_ref,) = take(1)
    xres_ref = take(1)[0] if skew else x_ref
    mod_ref, win_ref, wout_ref = take(3)
    qg_ref, kg_ref, ones_ref, convw_ref, convb_ref, sgug_ref, sguw_ref, sgub_ref, ln1g_ref, ln1b_ref = take(10)
    if rope:
        cos_ref, sin_ref, kc_ref, vc_ref = take(4)
    if stack_kv:
        kprev_ref, vprev_ref = take(2)
    side_in = [take(n_src) for n_src in side_jobs]
    (x1_ref,) = take(1)
    if not rope:
        k_ref, v_ref = take(2)
    if cast_weights:
        winb_ref, woutb_ref = take(2)

        @pl.when(pl.program_id(0) == 0)
        def _cast():
            winb_ref[...] = win_ref[...].astype(BF16)
            woutb_ref[...] = wout_ref[...].astype(BF16)
        win_ref, wout_ref = winb_ref, woutb_ref
    side_out = take(len(side_jobs))
    scratch = take(7)
    (mix_scr,) = take(1)
    n_rows = n_seq * seq
    assert n_cache == 0 or n_seq == 1

    def loop(n, body):
        if n == 1:
            body(0)
        else:
            def step(r, carry):
                body(r)
                return carry
            lax.fori_loop(0, n, step, 0, unroll=2)

    def block(r, size):
        if isinstance(r, int):
            return pl.ds(r * size, size)
        return pl.ds(pl.multiple_of(r * size, size), size)

    def project_phase(scr):
        q_scr, kvar_scr, vvar_scr, u_scr, cb_scr, su_scr, vn_scr = scr
        for srcs, dst in zip(side_in, side_out):
            col = 0
            for src in srcs:
                dst[:, col:col + src.shape[1]] = src[...].astype(BF16)
                col += src.shape[1]
        if n_cache:
            for i, var in enumerate(_head_variants(kc_ref[...])):
                kvar_scr[i, pl.ds(seq, n_cache), :] = var
            for i, var in enumerate(_head_variants(vc_ref[...])):
                vvar_scr[i, pl.ds(seq, n_cache), :] = var

        def project(r):
            rows = block(r, ROW_BLOCK)
            x = x_ref[rows, :]
            h = _standardize(x) * (1.0 + mod_ref[1:2, :]) + mod_ref[0:1, :]
            z = _dot(h.astype(BF16), win_ref[...])
            ones_bd = ones_ref[...]
            zq = z[:, _Q0:_K0]
            q = zq * lax.rsqrt(_group_sum(zq * zq, ones_bd) * (1.0 / HEAD_DIM) + EPS) * qg_ref[...]
            zk = z[:, _K0:_V0]
            k = zk * lax.rsqrt(_group_sum(zk * zk, ones_bd) * (1.0 / HEAD_DIM) + EPS) * kg_ref[...]
            v = z[:, _V0:_CI0]
            if rope:
                cos = cos_ref[rows, :]
                sin = sin_ref[rows, :]
                q = _rope(q, jnp.concatenate([cos] * 4, axis=1), jnp.concatenate([sin] * 4, axis=1))
                k = _rope(k, cos, sin)
            elif stack_kv:
                for s in range(ROW_BLOCK // seq):
                    sub = slice(s * seq, (s + 1) * seq)
                    k_ref[s, 0] = kprev_ref[sub, :]
                    v_ref[s, 0] = vprev_ref[sub, :]
                    k_ref[s, 1] = k[sub, :]
                    v_ref[s, 1] = v[sub, :]
            else:
                k_ref[rows, :] = k
                v_ref[rows, :] = v
            q_scr[rows, :] = (q * ATTN_SCALE).astype(BF16)
            for i, var in enumerate(_head_variants(k)):
                kvar_scr[i, rows, :] = var
            for i, var in enumerate(_head_variants(v)):
                vvar_scr[i, rows, :] = var
            u_scr[rows, :] = z[:, _CC0:_SU0] * z[:, _CI0:_CB0]
            cb_scr[rows, :] = z[:, _CB0:_CC0]
            su_scr[rows, :] = z[:, _SU0:_SV0]
            sv = z[:, _SV0:IN_WIDTH]
            d = sv - _group_sum(sv, ones_bd) * (1.0 / HEAD_DIM)
            vn = d * lax.rsqrt(_group_sum(d * d, ones_bd) * (1.0 / HEAD_DIM) + EPS) * sgug_ref[...]
            vn_scr[rows, :] = vn.astype(BF16)

        loop(n_rows // ROW_BLOCK, project)

    def consume_phase(scr):
        q_scr, kvar_scr, vvar_scr, u_scr, cb_scr, su_scr, vn_scr = scr
        u = u_scr[...]
        pos = lax.broadcasted_iota(I32, u.shape, 0) & (seq - 1)
        up = jnp.where(pos == 0, 0.0, pltpu.roll(u, 1, 0))
        dn = jnp.where(pos == seq - 1, 0.0, pltpu.roll(u, n_rows - 1, 0))
        conv = up * convw_ref[0:1, :] + u * convw_ref[1:2, :] + dn * convw_ref[2:3, :] + convb_ref[...]
        mix_scr[:, ATTN_WIDTH:ATTN_WIDTH + CONV_WIDTH] = (cb_scr[...] * conv).astype(BF16)

        for n in range(n_rows // CHUNK):
            rows = pl.ds(n * CHUNK, CHUNK)
            vn = vn_scr[rows, :]
            lane = lax.broadcasted_iota(I32, vn.shape, 1)
            per_head = [jnp.where((lane >= hd * HEAD_DIM) & (lane < (hd + 1) * HEAD_DIM), vn, jnp.zeros_like(vn))
                        for hd in range(SGU_HEADS)]
            s = sgub_ref[...] + _dot(sguw_ref[...], jnp.concatenate(per_head, axis=0))
            mix_scr[rows, ATTN_WIDTH + CONV_WIDTH:] = (su_scr[rows, :] * s).astype(BF16)

        def attend(s, r):
            rows = block(s * (seq // ATT_BLOCK) + r, ATT_BLOCK)
            keys = pl.ds(s * seq, seq + n_cache)
            for pair in range(N_Q_HEADS // 2):
                qp = q_scr[rows, pair * LANES:(pair + 1) * LANES]
                kv = pair // (N_Q_HEADS // N_KV_HEADS // 2)
                acc = jnp.zeros((ATT_BLOCK, LANES), F32)
                for parity in range(2):
                    sc = _dot_nt(qp, kvar_scr[2 * kv + parity, keys, :])
                    p = jnp.exp(sc - jnp.max(sc, axis=1, keepdims=True))
                    denom = jnp.sum(p, axis=1, keepdims=True)
                    acc = acc + _dot(p.astype(BF16), vvar_scr[2 * kv + parity, keys, :]) / denom
                mix_scr[rows, pair * LANES:(pair + 1) * LANES] = acc.astype(BF16)

        for s in range(n_seq):
            loop(seq // ATT_BLOCK, functools.partial(attend, s))

        def finish(r):
            rows = block(r, ROW_BLOCK)
            mix = _dot(mix_scr[rows, :], wout_ref[...])
            y = DEEPNORM_ALPHA * xres_ref[rows, :] + mod_ref[2:3, :] * mix
            x1_ref[rows, :] = _standardize(y) * ln1g_ref[...] + ln1b_ref[...]

        loop(n_rows // ROW_BLOCK, finish)

    if not skew:
        project_phase(scratch)
        consume_phase(scratch)
        return

    i = pl.program_id(0)
    slot = lax.rem(i, 2)
    mine = [ref.at[slot] for ref in scratch]
    other = [ref.at[1 - slot] for ref in scratch]
    pl.when(i == 0)(functools.partial(project_phase, mine))

    @pl.when((i > 0) & (i < n_steps))
    def _():
        consume_phase(other)
        project_phase(mine)

    pl.when(i == n_steps)(functools.partial(consume_phase, other))


def _full(shape):
    n = len(shape)
    return pl.BlockSpec(shape, lambda *_: (0,) * n)


def _resident(shape):
    n = len(shape)
    return pl.BlockSpec(shape, lambda *_: (0,) * n, pipeline_mode=pl.Buffered(1))


def _layer_block(arr, l, resident=False):
    shape = arr.shape[1:]
    kw = dict(pipeline_mode=pl.Buffered(1)) if resident else {}
    return pl.BlockSpec((None,) + shape, lambda *_: (l,) + (0,) * len(shape), **kw)


def _mixer_scratch(n_rows, n_cache, slots=()):
    nk = n_rows + n_cache
    return [
        pltpu.VMEM(slots + (n_rows, ATTN_WIDTH), BF16),
        pltpu.VMEM(slots + (4, nk, LANES), BF16),
        pltpu.VMEM(slots + (4, nk, LANES), BF16),
        pltpu.VMEM(slots + (n_rows, CONV_WIDTH), F32),
        pltpu.VMEM(slots + (n_rows, CONV_WIDTH), F32),
        pltpu.VMEM(slots + (n_rows, SGU_WIDTH), F32),
        pltpu.VMEM(slots + (n_rows, SGU_WIDTH), BF16),
        pltpu.VMEM((n_rows, D_MODEL), BF16),
    ]


def _side_source(src):
    return src if isinstance(src, tuple) else (src, None)


def _mixer_ctx(x, l, n_seq, seq, mod, w_in, w_out, small, kv_prev=None, side_casts=()):
    per_step = CTX_SEQS_PER_STEP
    rows = per_step * seq
    stack_kv = kv_prev is not None
    cast_weights = w_in.dtype != BF16
    n_steps = n_seq // per_step
    kernel = functools.partial(_mixer_kernel, seq=seq, n_seq=per_step, n_cache=0, rope=False,
                               cast_weights=cast_weights, stack_kv=stack_kv,
                               side_jobs=tuple(len(job) for job in side_casts), n_steps=n_steps)
    cur = lambda i: (jnp.minimum(i, n_steps - 1), 0)
    prev = lambda i: (jnp.maximum(i - 1, 0), 0)
    w_specs = ([_layer_block(w_in, l, resident=True), _layer_block(w_out, l, resident=True)] if cast_weights
               else [_resident(w_in.shape), _resident(w_out.shape)])
    in_specs = ([pl.BlockSpec((rows, D_MODEL), cur), pl.BlockSpec((rows, D_MODEL), prev),
                 pl.BlockSpec((None, None, SUBLANES, D_MODEL), lambda i: (l, 0, 0, 0))] + w_specs
                + [_full(a.shape) if a.ndim == 2 else _layer_block(a, l) for a in small])
    args = [x, x, mod, w_in, w_out, *small]
    if stack_kv:
        in_specs += [pl.BlockSpec((rows, KV_WIDTH), cur)] * 2
        args += list(kv_prev)
        kv_spec = pl.BlockSpec((per_step, DEPTH, seq, KV_WIDTH), lambda i: cur(i) + (0, 0))
        kv_shape = jax.ShapeDtypeStruct((n_seq, DEPTH, seq, KV_WIDTH), F32)
    else:
        kv_spec = pl.BlockSpec((rows, KV_WIDTH), cur)
        kv_shape = jax.ShapeDtypeStruct((n_seq * seq, KV_WIDTH), F32)
    side_out_specs, side_out_shapes = [], []
    for job in side_casts:
        sources = [_side_source(src) for src in job]
        n_rows = sources[0][0].shape[-2]
        n_cols = sum(arr.shape[-1] for arr, _ in sources)
        for arr, layer in sources:
            blk = (n_rows // n_steps, arr.shape[-1])
            if layer is None:
                in_specs.append(pl.BlockSpec(blk, cur))
            else:
                in_specs.append(pl.BlockSpec((None,) + blk, lambda i, layer=layer: (layer,) + cur(i)))
            args.append(arr)
        side_out_specs.append(pl.BlockSpec((n_rows // n_steps, n_cols), cur))
        side_out_shapes.append(jax.ShapeDtypeStruct((n_rows, n_cols), BF16))
    w_out_specs, w_out_shapes = [], []
    if cast_weights:
        w_out_specs = [_full((D_MODEL, IN_WIDTH)), _full((D_MODEL, D_MODEL))]
        w_out_shapes = [jax.ShapeDtypeStruct((D_MODEL, IN_WIDTH), BF16), jax.ShapeDtypeStruct((D_MODEL, D_MODEL), BF16)]
    return pl.pallas_call(
        kernel,
        grid=(n_steps + 1,),
        in_specs=in_specs,
        out_specs=[pl.BlockSpec((rows, D_MODEL), prev), kv_spec, kv_spec] + w_out_specs + side_out_specs,
        out_shape=[jax.ShapeDtypeStruct((n_seq * seq, D_MODEL), F32), kv_shape, kv_shape] + w_out_shapes
        + side_out_shapes,
        scratch_shapes=_mixer_scratch(rows, 0, slots=(2,)),
        compiler_params=pltpu.CompilerParams(dimension_semantics=("arbitrary",), vmem_limit_bytes=VMEM_LIMIT),
        name="mixer_ctx",
    )(*args)


def _mixer_lat(x, l, n_seq, seq, mod, w_in_bf, w_out_bf, small, cos, sin, kc, vc):
    n_cache = kc.shape[2]
    kernel = functools.partial(_mixer_kernel, seq=seq, n_seq=1, n_cache=n_cache, rope=True,
                               cast_weights=False, stack_kv=False)
    cache_spec = pl.BlockSpec((None, None, n_cache, KV_WIDTH), lambda b: (b, l, 0, 0))
    return pl.pallas_call(
        kernel,
        grid=(n_seq,),
        in_specs=([pl.BlockSpec((seq, D_MODEL), lambda b: (b, 0)),
                   pl.BlockSpec((None, None, SUBLANES, D_MODEL), lambda b: (l, 1 + b, 0, 0)),
                   _resident((D_MODEL, IN_WIDTH)), _resident((D_MODEL, D_MODEL))]
                  + [_full(a.shape) if a.ndim == 2 else _layer_block(a, l) for a in small]
                  + [_full((seq, LANES)), _full((seq, LANES)), cache_spec, cache_spec]),
        out_specs=pl.BlockSpec((seq, D_MODEL), lambda b: (b, 0)),
        out_shape=jax.ShapeDtypeStruct((n_seq * seq, D_MODEL), F32),
        scratch_shapes=_mixer_scratch(seq, n_cache),
        compiler_params=pltpu.CompilerParams(dimension_semantics=("arbitrary",), vmem_limit_bytes=VMEM_LIMIT),
        name="mixer_lat",
    )(x, mod, w_in_bf, w_out_bf, *small, cos, sin, kc, vc)


def _two_stream_specs(block_rows, n_a, block_of):
    spec_a = pl.BlockSpec((block_rows, D_MODEL), lambda *ids: (jnp.minimum(block_of(*ids), n_a - 1), 0))
    spec_b = pl.BlockSpec((block_rows, D_MODEL), lambda *ids: (jnp.maximum(block_of(*ids) - n_a, 0), 0))
    return [spec_a, spec_b]


def _group_of_block(blk, block_rows, n_a, lat_seq):
    return jnp.where(blk < n_a, 0, 1 + (blk - n_a) // (lat_seq // block_rows))


def _ffn_kernel(xa_ref, xb_ref, mod_ref, w1_ref, w3_ref, w2_ref, g_ref, b_ref, oa_ref, ob_ref, *, n_a):
    is_a = pl.program_id(0) < n_a
    x = jnp.where(is_a, xa_ref[...], xb_ref[...])
    h = (_standardize(x) * (1.0 + mod_ref[4:5, :]) + mod_ref[3:4, :]).astype(BF16)
    acc = jnp.zeros(x.shape, F32)
    for c in range(D_FF // FFN_COLS):
        cols = slice(c * FFN_COLS, (c + 1) * FFN_COLS)
        a = _dot(h, w1_ref[:, cols])
        b = _dot(h, w3_ref[:, cols])
        acc = acc + _dot((_silu(a) * b).astype(BF16), w2_ref[cols, :])
    y = DEEPNORM_ALPHA * x + mod_ref[5:6, :] * acc
    out = _standardize(y) * g_ref[...] + b_ref[...]

    @pl.when(is_a)
    def _():
        oa_ref[...] = out

    @pl.when(jnp.logical_not(is_a))
    def _():
        ob_ref[...] = out


def _ffn_dense(xa, xb, l, mods, w1, w3, w2, g, b, lat_seq):
    n_a, n_b = xa.shape[0] // FFN_TILE, xb.shape[0] // FFN_TILE
    x_specs = _two_stream_specs(FFN_TILE, n_a, lambda t: t)
    return pl.pallas_call(
        functools.partial(_ffn_kernel, n_a=n_a),
        grid=(n_a + n_b,),
        in_specs=x_specs + [
            pl.BlockSpec((None, None, SUBLANES, D_MODEL),
                         lambda t: (l, _group_of_block(t, FFN_TILE, n_a, lat_seq), 0, 0)),
            _resident(w1.shape), _resident(w3.shape), _resident(w2.shape), _layer_block(g, l), _layer_block(b, l)],
        out_specs=x_specs,
        out_shape=[jax.ShapeDtypeStruct(xa.shape, F32), jax.ShapeDtypeStruct(xb.shape, F32)],
        compiler_params=pltpu.CompilerParams(dimension_semantics=("arbitrary",), vmem_limit_bytes=VMEM_LIMIT),
        name="ffn_dense",
    )(xa, xb, mods, w1, w3, w2, g, b)


def _router_kernel(xa_ref, xb_ref, mod_ref, rw_ref, before_ref, dest_ref, gate_ref, start_ref, padded_ref, *,
                   n_a, lat_seq):
    c = pl.program_id(0)
    w_hi, w_lo = _split(rw_ref[...])
    n_blocks = MOE_CHUNK // ROUTER_BLOCK
    parts = []
    for blk in range(n_blocks):
        rows = pl.ds(blk * ROUTER_BLOCK, ROUTER_BLOCK)
        mod = mod_ref[_group_of_block(c * n_blocks + blk, ROUTER_BLOCK, n_a * n_blocks, lat_seq)]
        x = jnp.where(c < n_a, xa_ref[rows, :], xb_ref[rows, :])
        h = _standardize(x) * (1.0 + mod[4:5, :]) + mod[3:4, :]
        h_hi, h_lo = _split(h)
        parts.append(_dot_nt(w_hi, h_hi) + _dot_nt(w_hi, h_lo) + _dot_nt(w_lo, h_hi))
    logits = jnp.concatenate(parts, axis=1)
    eid = lax.broadcasted_iota(I32, logits.shape, 0).astype(F32)
    m1 = jnp.max(logits, axis=0, keepdims=True)
    i1 = jnp.min(jnp.where(logits == m1, eid, float(N_EXPERTS)), axis=0, keepdims=True)
    oh1 = eid == i1
    rest = jnp.where(oh1, -jnp.inf, logits)
    m2 = jnp.max(rest, axis=0, keepdims=True)
    i2 = jnp.min(jnp.where(rest == m2, eid, float(N_EXPERTS)), axis=0, keepdims=True)
    oh2 = eid == i2
    e = jnp.exp(m2 - m1)
    gate_ref[0:1, :] = 1.0 / (1.0 + e)
    gate_ref[1:2, :] = e / (1.0 + e)
    sel = jnp.where(oh1 | oh2, 1.0, 0.0)
    ranks = []
    seen = jnp.zeros((N_EXPERTS, 1), F32)
    for blk in range(n_blocks):
        s_blk = sel[:, blk * ROUTER_BLOCK:(blk + 1) * ROUTER_BLOCK]
        ranks.append(_dot(s_blk.astype(BF16), before_ref[...]) + seen)
        seen = seen + jnp.sum(s_blk, axis=1, keepdims=True)
    rank = jnp.concatenate(ranks, axis=1)
    eid_out = lax.broadcasted_iota(I32, start_ref.shape, 0).astype(F32)
    start = jnp.zeros(sel.shape, F32)
    start_out = jnp.zeros(start_ref.shape, F32)
    padded_out = jnp.zeros(start_ref.shape, F32)
    for ex in range(N_EXPERTS):
        cnt = jnp.sum(sel[ex:ex + 1, :], axis=1, keepdims=True)
        padded = jnp.ceil(cnt * (1.0 / MOE_GRAN)) * MOE_GRAN
        start = start + jnp.where(eid > ex, padded, 0.0)
        start_out = start_out + jnp.where(eid_out > ex, padded, 0.0)
        padded_out = padded_out + jnp.where(eid_out == ex, padded, 0.0)
    row = (start + rank) * SUBLANES
    dest_ref[0:1, :] = jnp.sum(jnp.where(oh1, row, 0.0), axis=0, keepdims=True).astype(I32)
    dest_ref[1:2, :] = jnp.sum(jnp.where(oh2, row, 0.0), axis=0, keepdims=True).astype(I32)
    start_ref[...] = start_out.astype(I32)
    padded_ref[...] = padded_out.astype(I32)


def _router(xa, xb, l, mods, rw_t, lat_seq):
    n_a = xa.shape[0] // MOE_CHUNK
    n_chunks = n_a + xb.shape[0] // MOE_CHUNK
    tok = np.arange(ROUTER_BLOCK)
    before = jnp.asarray(tok[:, None] < tok[None, :], BF16)
    return pl.pallas_call(
        functools.partial(_router_kernel, n_a=n_a, lat_seq=lat_seq),
        grid=(n_chunks,),
        in_specs=_two_stream_specs(MOE_CHUNK, n_a, lambda c: c) + [
                  _layer_block(mods, l),
                  _full((N_EXPERTS, D_MODEL)), _full((ROUTER_BLOCK, ROUTER_BLOCK))],
        out_specs=[pl.BlockSpec((None, 2, MOE_CHUNK), lambda c: (c, 0, 0)),
                   pl.BlockSpec((None, 2, MOE_CHUNK), lambda c: (c, 0, 0)),
                   pl.BlockSpec((None, N_EXPERTS, LANES), lambda c: (c, 0, 0)),
                   pl.BlockSpec((None, N_EXPERTS, LANES), lambda c: (c, 0, 0))],
        out_shape=[jax.ShapeDtypeStruct((n_chunks, 2, MOE_CHUNK), I32),
                   jax.ShapeDtypeStruct((n_chunks, 2, MOE_CHUNK), F32),
                   jax.ShapeDtypeStruct((n_chunks, N_EXPERTS, LANES), I32),
                   jax.ShapeDtypeStruct((n_chunks, N_EXPERTS, LANES), I32)],
        compiler_params=pltpu.CompilerParams(dimension_semantics=("arbitrary",), vmem_limit_bytes=VMEM_LIMIT),
        name="moe_router",
    )(xa, xb, mods, rw_t, before)


def _token_rows(first_row):
    return pl.ds(pl.multiple_of(first_row, SUBLANES), SUBLANES)


def _store_token_major(ref, tok0, val):
    for cc in range(D_MODEL // LANES):
        ref[pl.ds(tok0 * SUBLANES + cc, val.shape[0], stride=SUBLANES), :] = val[:, cc * LANES:(cc + 1) * LANES]


def _load_token_major(ref, tok0, n):
    return jnp.concatenate(
        [ref[pl.ds(tok0 * SUBLANES + cc, n, stride=SUBLANES), :] for cc in range(D_MODEL // LANES)], axis=1)


def _moe_kernel(start_ref, padded_ref, xa_ref, xb_ref, mod_ref, dest_ref, gate_ref, w13_ref, w2_ref,
                g_ref, b_ref, oa_ref, ob_ref, tok_scr, rows_scr, *, n_a):
    c = pl.program_id(0)
    j = pl.program_id(1)
    is_a = c < n_a
    first_expert_step = MOE_TOK_STEPS
    first_combine_step = MOE_TOK_STEPS + N_EXPERTS

    @pl.when((c == 0) & (j == 0))
    def _init():
        rows_scr[...] = jnp.zeros(rows_scr.shape, F32)

    @pl.when(j < first_expert_step)
    def _dispatch():
        for r in range(MOE_TOK_BLOCK // ROW_BLOCK):
            rows = pl.ds(r * ROW_BLOCK, ROW_BLOCK)
            x = jnp.where(is_a, xa_ref[rows, :], xb_ref[rows, :])
            h = _standardize(x) * (1.0 + mod_ref[4:5, :]) + mod_ref[3:4, :]
            _store_token_major(tok_scr, r * ROW_BLOCK, h)
        t0 = j * MOE_TOK_BLOCK

        def body(t, carry):
            row = tok_scr[_token_rows(t * SUBLANES), :]
            rows_scr[_token_rows(dest_ref[t0 + t]), :] = row
            rows_scr[_token_rows(dest_ref[MOE_CHUNK + t0 + t]), :] = row
            return carry
        lax.fori_loop(0, MOE_TOK_BLOCK, body, 0, unroll=8)

    def experts(row0, m):
        xin = _load_token_major(rows_scr, row0, m).astype(BF16)
        ab = _dot(xin, w13_ref[...])
        a, b = ab[:, :D_FF_EXPERT], ab[:, D_FF_EXPERT:]
        y = _dot((_silu(a) * b).astype(BF16), w2_ref[...])
        _store_token_major(rows_scr, row0, y)

    @pl.when((j >= first_expert_step) & (j < first_combine_step))
    def _experts():
        region = c * N_EXPERTS + (j - first_expert_step)
        start = start_ref[region]
        padded = padded_ref[region]
        n_full = lax.shift_right_logical(padded, MOE_BLOCK.bit_length() - 1)

        def body(i, carry):
            experts(start + i * MOE_BLOCK, MOE_BLOCK)
            return carry
        lax.fori_loop(0, n_full, body, 0)
        for m in range(MOE_GRAN, MOE_BLOCK, MOE_GRAN):
            pl.when(padded - n_full * MOE_BLOCK == m)(
                functools.partial(experts, start + n_full * MOE_BLOCK, m))

    @pl.when(j >= first_combine_step)
    def _combine():
        t0 = (j - first_combine_step) * MOE_TOK_BLOCK

        def body(t, carry):
            y0 = rows_scr[_token_rows(dest_ref[t0 + t]), :]
            y1 = rows_scr[_token_rows(dest_ref[MOE_CHUNK + t0 + t]), :]
            tok_scr[_token_rows(t * SUBLANES), :] = gate_ref[t0 + t] * y0 + gate_ref[MOE_CHUNK + t0 + t] * y1
            return carry
        lax.fori_loop(0, MOE_TOK_BLOCK, body, 0, unroll=8)
        for r in range(MOE_TOK_BLOCK // ROW_BLOCK):
            rows = pl.ds(r * ROW_BLOCK, ROW_BLOCK)
            ffn = _load_token_major(tok_scr, r * ROW_BLOCK, ROW_BLOCK)
            x = jnp.where(is_a, xa_ref[rows, :], xb_ref[rows, :])
            y = DEEPNORM_ALPHA * x + mod_ref[5:6, :] * ffn
            out = _standardize(y) * g_ref[...] + b_ref[...]

            @pl.when(is_a)
            def _():
                oa_ref[rows, :] = out

            @pl.when(jnp.logical_not(is_a))
            def _():
                ob_ref[rows, :] = out


def _ffn_moe(xa, xb, l, mods, rw_t, w13, w2, g, b, lat_seq):
    n_a = xa.shape[0] // MOE_CHUNK
    n_chunks = n_a + xb.shape[0] // MOE_CHUNK
    n_a_blocks = n_a * MOE_TOK_STEPS
    dest, gates, start, padded = _router(xa, xb, l, mods, rw_t, lat_seq)
    first_expert_step = MOE_TOK_STEPS
    first_combine_step = MOE_TOK_STEPS + N_EXPERTS

    def token_block(c, j, *_):
        blk = jnp.where(j < first_combine_step, jnp.minimum(j, MOE_TOK_STEPS - 1), j - first_combine_step)
        return c * MOE_TOK_STEPS + blk

    def out_token_block(c, j, *_):
        return c * MOE_TOK_STEPS + jnp.maximum(j - first_combine_step, 0)

    def mod_of(c, j, *_):
        return (l, _group_of_block(token_block(c, j), MOE_TOK_BLOCK, n_a_blocks, lat_seq), 0, 0)

    def expert_of(c, j, *_):
        return (jnp.clip(j - first_expert_step, 0, N_EXPERTS - 1), 0, 0)

    grid_spec = pltpu.PrefetchScalarGridSpec(
        num_scalar_prefetch=2,
        grid=(n_chunks, MOE_STEPS),
        in_specs=_two_stream_specs(MOE_TOK_BLOCK, n_a_blocks, token_block) + [
                  pl.BlockSpec((None, None, SUBLANES, D_MODEL), mod_of),
                  pl.BlockSpec((2 * MOE_CHUNK,), lambda c, j, *_: (c,), memory_space=pltpu.SMEM),
                  pl.BlockSpec((2 * MOE_CHUNK,), lambda c, j, *_: (c,), memory_space=pltpu.SMEM),
                  pl.BlockSpec((None, D_MODEL, 2 * D_FF_EXPERT), expert_of),
                  pl.BlockSpec((None, D_FF_EXPERT, D_MODEL), expert_of),
                  _layer_block(g, l), _layer_block(b, l)],
        out_specs=_two_stream_specs(MOE_TOK_BLOCK, n_a_blocks, out_token_block),
        scratch_shapes=[pltpu.VMEM((MOE_TOK_BLOCK * SUBLANES, LANES), F32),
                        pltpu.VMEM((MOE_ROWS * SUBLANES, LANES), F32)],
    )
    return pl.pallas_call(
        functools.partial(_moe_kernel, n_a=n_a),
        grid_spec=grid_spec,
        out_shape=[jax.ShapeDtypeStruct(xa.shape, F32), jax.ShapeDtypeStruct(xb.shape, F32)],
        compiler_params=pltpu.CompilerParams(
            dimension_semantics=("arbitrary", "arbitrary"), vmem_limit_bytes=VMEM_LIMIT),
        name="moe_experts",
    )(start[:, :, 0].reshape(-1), padded[:, :, 0].reshape(-1), xa, xb, mods,
      dest.reshape(-1), gates.reshape(-1), w13, w2, g, b)


def _rope_tables(n_tokens):
    t = np.arange(n_tokens)
    row = (t // GRID_W).astype(np.float32)
    col = (t % GRID_W).astype(np.float32)
    inv_freq = (np.float32(ROPE_THETA) ** (-np.arange(0, AXIS_ROT, 2, dtype=np.float32) / AXIS_ROT)).astype(np.float32)
    ang_r = row[:, None] * inv_freq
    ang_c = col[:, None] * inv_freq
    cos = np.concatenate([np.cos(ang_r), np.cos(ang_r), np.cos(ang_c), np.cos(ang_c)], axis=1)
    sin = np.concatenate([-np.sin(ang_r), np.sin(ang_r), -np.sin(ang_c), np.sin(ang_c)], axis=1)
    return jnp.asarray(np.tile(cos, (1, 2)), F32), jnp.asarray(np.tile(sin, (1, 2)), F32)


def kernel(x_prompt, x_sample, cache_k, cache_v, c, c_ctx, ada_w, ada_b, w_in, q_norm_g, k_norm_g, conv_w, conv_b, sgu_norm_g, sgu_w, sgu_b, w_out, ln1_g, ln1_b, ln2_g, ln2_b, ffn_w1, ffn_w3, ffn_w2, router_w, moe_w1, moe_w3, moe_w2):
    batch, seq, _ = x_prompt.shape
    dec_batch, dec_seq, _ = x_sample.shape
    past_len = cache_k.shape[2]
    n_ctx = batch * seq
    n_lat = dec_batch * dec_seq
    assert DEPTH == 2 and 1 + dec_batch <= SUBLANES
    assert seq == ATT_BLOCK and CTX_SEQS_PER_STEP * seq == ROW_BLOCK and batch % CTX_SEQS_PER_STEP == 0
    assert dec_seq % ROW_BLOCK == 0 and dec_seq & (dec_seq - 1) == 0
    assert n_ctx % MOE_CHUNK == 0 and n_lat % MOE_CHUNK == 0
    assert dec_seq % ROUTER_BLOCK == 0 and dec_seq % MOE_TOK_BLOCK == 0 and dec_seq % FFN_TILE == 0

    cond = jnp.zeros((SUBLANES, D_MODEL), F32).at[0].set(c_ctx).at[1:1 + dec_batch].set(c)
    mod = _modulation(cond, ada_w, ada_b)
    mod = mod.reshape(DEPTH, SUBLANES, 6, D_MODEL)[:, :1 + dec_batch]
    mod = jnp.pad(mod, ((0, 0), (0, 0), (0, SUBLANES - 6), (0, 0)))

    lane_id = np.arange(GROUP_TILE) // HEAD_DIM
    ones_bd = jnp.asarray(lane_id[:, None] == lane_id[None, :], BF16)
    cos, sin = _rope_tables(dec_seq)
    small = (
        jnp.tile(q_norm_g, (1, N_Q_HEADS))[:, None, :], jnp.tile(k_norm_g, (1, N_KV_HEADS))[:, None, :],
        ones_bd,
        conv_w, conv_b[:, None, :], sgu_norm_g[:, None, :],
        jnp.swapaxes(sgu_w, 1, 2).reshape(DEPTH, CHUNK, SGU_HEADS * CHUNK).astype(BF16),
        jnp.repeat(jnp.swapaxes(sgu_b, 1, 2), HEAD_DIM, axis=2),
        ln1_g[:, None, :], ln1_b[:, None, :],
    )
    g2, b2 = ln2_g[:, None, :], ln2_b[:, None, :]
    kc = cache_k.reshape(dec_batch, DEPTH, past_len, KV_WIDTH)
    vc = cache_v.reshape(dec_batch, DEPTH, past_len, KV_WIDTH)

    xs = [x_prompt.reshape(n_ctx, D_MODEL), x_sample.reshape(n_lat, D_MODEL)]
    assert DEPTH == 2 and ffn_w1.shape[0] == 1 and moe_w1.shape[0] == 1
    side_casts = [[(ffn_w1[0],), (ffn_w3[0],), (ffn_w2[0],), ((w_in, 1),), ((w_out, 1),)],
                  [(moe_w1.reshape(-1, D_FF_EXPERT), moe_w3.reshape(-1, D_FF_EXPERT)), (moe_w2.reshape(-1, D_MODEL),)]]
    kv = None
    for l in range(DEPTH):
        if l == 0:
            x_ctx, k_ctx, v_ctx, w_in_bf, w_out_bf, *ffn_bf, w_in_next, w_out_next = _mixer_ctx(
                xs[0], l, batch, seq, mod, w_in, w_out, small, kv, side_casts[l])
        else:
            w_in_bf, w_out_bf = w_in_next, w_out_next
            x_ctx, k_ctx, v_ctx, w13_bf, w2_bf = _mixer_ctx(
                xs[0], l, batch, seq, mod, w_in_bf, w_out_bf, small, kv, side_casts[l])
        kv = (k_ctx, v_ctx)
        x_lat = _mixer_lat(xs[1], l, dec_batch, dec_seq, mod, w_in_bf, w_out_bf, small, cos, sin, kc, vc)
        if l % 2 == 0:
            xs = _ffn_dense(x_ctx, x_lat, l, mod, *ffn_bf, g2, b2, dec_seq)
        else:
            ws = (w13_bf.reshape(N_EXPERTS, D_MODEL, 2 * D_FF_EXPERT), w2_bf.reshape(N_EXPERTS, D_FF_EXPERT, D_MODEL))
            xs = _ffn_moe(x_ctx, x_lat, l, mod, router_w[l // 2].T, *ws, g2, b2, dec_seq)
    y_p = xs[0].reshape(batch, seq, D_MODEL)
    y_s = xs[1].reshape(dec_batch, dec_seq, D_MODEL)
    new_k = kv[0].reshape(batch, DEPTH, seq, N_KV_HEADS, HEAD_DIM)
    new_v = kv[1].reshape(batch, DEPTH, seq, N_KV_HEADS, HEAD_DIM)
    return (y_p, y_s, new_k, new_v)
```

```python
import functools

import numpy as np
import jax
import jax.numpy as jnp
from jax import lax
from jax.experimental import pallas as pl
from jax.experimental.pallas import tpu as pltpu

F32 = jnp.float32
BF16 = jnp.bfloat16
I32 = jnp.int32

D_MODEL = 1024
DEPTH = 2
GRID_W = 64
HEAD_DIM = 64
N_Q_HEADS = 8
N_KV_HEADS = 2
ATTN_WIDTH = N_Q_HEADS * HEAD_DIM
KV_WIDTH = N_KV_HEADS * HEAD_DIM
ATTN_SCALE = HEAD_DIM ** -0.5
ROPE_THETA = 10000.0
AXIS_ROT = HEAD_DIM // 2
CONV_WIDTH = 256
SGU_WIDTH = 256
SGU_HEADS = 4
CHUNK = 128
IN_WIDTH = 2048
D_FF = 2816
N_EXPERTS = 8
D_FF_EXPERT = 1408
EPS = 1e-6
DEEPNORM_ALPHA = (2 * DEPTH) ** 0.25

LANES = 128
SUBLANES = 8
ROW_BLOCK = 512
GROUP_TILE = 256
ATT_BLOCK = 256
CTX_SEQS_PER_STEP = 2
FFN_TILE = 512
FFN_COLS = 256
MOE_CHUNK = 2048
ROUTER_BLOCK = 1024
MOE_GRAN = 128
MOE_BLOCK = 256
MOE_TOK_BLOCK = 512
MOE_TOK_STEPS = MOE_CHUNK // MOE_TOK_BLOCK
MOE_STEPS = 2 * MOE_TOK_STEPS + N_EXPERTS
MOE_ROWS = 2 * MOE_CHUNK + N_EXPERTS * MOE_GRAN
VMEM_LIMIT = 60 * 1024 * 1024

_Q0, _K0, _V0, _CI0, _CB0, _CC0, _SU0, _SV0 = 0, 512, 640, 768, 1024, 1280, 1536, 1792


def _dot(a, b):
    return jnp.dot(a, b, preferred_element_type=F32)


def _dot_nt(a, b):
    return lax.dot_general(a, b, (((1,), (1,)), ((), ())), preferred_element_type=F32)


def _split(x):
    hi = x.astype(BF16)
    lo = (x - hi.astype(F32)).astype(BF16)
    return hi, lo


def _group_sum(x, ones_bd):
    outs = []
    for c0 in range(0, x.shape[1], GROUP_TILE):
        width = min(GROUP_TILE, x.shape[1] - c0)
        outs.append(_dot(x[:, c0:c0 + width].astype(BF16), ones_bd[:width, :width]))
    return outs[0] if len(outs) == 1 else jnp.concatenate(outs, axis=1)


def _standardize(x):
    mu = jnp.mean(x, axis=-1, keepdims=True)
    d = x - mu
    return d * lax.rsqrt(jnp.mean(d * d, axis=-1, keepdims=True) + EPS)


def _silu(x):
    return x / (1.0 + jnp.exp(-x))


def _modulation_kernel(cond_ref, w_ref, b_ref, o_ref):
    s_hi, s_lo = _split(_silu(cond_ref[...]))
    w_hi, w_lo = _split(w_ref[...])
    o_ref[...] = _dot(s_hi, w_hi) + _dot(s_hi, w_lo) + _dot(s_lo, w_hi) + b_ref[...]


def _modulation(cond, ada_w, ada_b):
    n_out = ada_w.shape[-1]
    tn = 1536
    return pl.pallas_call(
        _modulation_kernel,
        grid=(DEPTH, n_out // tn),
        in_specs=[
            pl.BlockSpec((SUBLANES, D_MODEL), lambda l, j: (0, 0)),
            pl.BlockSpec((None, D_MODEL, tn), lambda l, j: (l, 0, j)),
            pl.BlockSpec((None, 1, tn), lambda l, j: (l, 0, j)),
        ],
        out_specs=pl.BlockSpec((None, SUBLANES, tn), lambda l, j: (l, 0, j)),
        out_shape=jax.ShapeDtypeStruct((DEPTH, SUBLANES, n_out), F32),
        compiler_params=pltpu.CompilerParams(
            dimension_semantics=("arbitrary", "arbitrary"), vmem_limit_bytes=VMEM_LIMIT),
        name="modulation",
    )(cond, ada_w, ada_b.reshape(DEPTH, 1, n_out))


def _rope(x, cos, sin_signed):
    w = x.shape[1]
    lane = lax.broadcasted_iota(I32, x.shape, 1)
    first_half = (lane & 31) < 16
    partner = jnp.where(first_half, pltpu.roll(x, w - 16, 1), pltpu.roll(x, 16, 1))
    return x * cos + partner * sin_signed


def _head_variants(x):
    lane = lax.broadcasted_iota(I32, x.shape, 1)
    lo = lane < HEAD_DIM
    xr = pltpu.roll(x, HEAD_DIM, 1)
    zero = jnp.zeros_like(x)
    return (jnp.where(lo, x, zero).astype(BF16), jnp.where(lo, zero, xr).astype(BF16),
            jnp.where(lo, xr, zero).astype(BF16), jnp.where(lo, zero, x).astype(BF16))


def _mixer_kernel(*refs, seq, n_seq, n_cache, rope, cast_weights, stack_kv, side_jobs=(), n_steps=None):
    refs = list(refs)
    skew = n_steps is not None

    def take(n):
        out, refs[:] = refs[:n], refs[n:]
        return out

    (x_ref,) = take(1)
    xres_ref = take(1)[0] if skew else x_ref
    mod_ref, win_ref, wout_ref = take(3)
    qg_ref, kg_ref, ones_ref, convw_ref, convb_ref, sgug_ref, sguw_ref, sgub_ref, ln1g_ref, ln1b_ref = take(10)
    if rope:
        cos_ref, sin_ref, kc_ref, vc_ref = take(4)
    if stack_kv:
        kprev_ref, vprev_ref = take(2)
    side_in = [take(n_src) for n_src in side_jobs]
    (x1_ref,) = take(1)
    if not rope:
        k_ref, v_ref = take(2)
    if cast_weights:
        winb_ref, woutb_ref = take(2)

        @pl.when(pl.program_id(0) == 0)
        def _cast():
            winb_ref[...] = win_ref[...].astype(BF16)
            woutb_ref[...] = wout_ref[...].astype(BF16)
        win_ref, wout_ref = winb_ref, woutb_ref
    side_out = take(len(side_jobs))
    scratch = take(7)
    (mix_scr,) = take(1)
    n_rows = n_seq * seq
    assert n_cache == 0 or n_seq == 1

    def loop(n, body):
        if n == 1:
            body(0)
        else:
            def step(r, carry):
                body(r)
                return carry
            lax.fori_loop(0, n, step, 0, unroll=2)

    def block(r, size):
        if isinstance(r, int):
            return pl.ds(r * size, size)
        return pl.ds(pl.multiple_of(r * size, size), size)

    def project_phase(scr):
        q_scr, kvar_scr, vvar_scr, u_scr, cb_scr, su_scr, vn_scr = scr
        for srcs, dst in zip(side_in, side_out):
            col = 0
            for src in srcs:
                dst[:, col:col + src.shape[1]] = src[...].astype(BF16)
                col += src.shape[1]
        if n_cache:
            for i, var in enumerate(_head_variants(kc_ref[...])):
                kvar_scr[i, pl.ds(seq, n_cache), :] = var
            for i, var in enumerate(_head_variants(vc_ref[...])):
                vvar_scr[i, pl.ds(seq, n_cache), :] = var

        def project(r):
            rows = block(r, ROW_BLOCK)
            x = x_ref[rows, :]
            h = _standardize(x) * (1.0 + mod_ref[1:2, :]) + mod_ref[0:1, :]
            z = _dot(h.astype(BF16), win_ref[...])
            ones_bd = ones_ref[...]
            zq = z[:, _Q0:_K0]
            q = zq * lax.rsqrt(_group_sum(zq * zq, ones_bd) * (1.0 / HEAD_DIM) + EPS) * qg_ref[...]
            zk = z[:, _K0:_V0]
            k = zk * lax.rsqrt(_group_sum(zk * zk, ones_bd) * (1.0 / HEAD_DIM) + EPS) * kg_ref[...]
            v = z[:, _V0:_CI0]
            if rope:
                cos = cos_ref[rows, :]
                sin = sin_ref[rows, :]
                q = _rope(q, jnp.concatenate([cos] * 4, axis=1), jnp.concatenate([sin] * 4, axis=1))
                k = _rope(k, cos, sin)
            elif stack_kv:
                for s in range(ROW_BLOCK // seq):
                    sub = slice(s * seq, (s + 1) * seq)
                    k_ref[s, 0] = kprev_ref[sub, :]
                    v_ref[s, 0] = vprev_ref[sub, :]
                    k_ref[s, 1] = k[sub, :]
                    v_ref[s, 1] = v[sub, :]
            else:
                k_ref[rows, :] = k
                v_ref[rows, :] = v
            q_scr[rows, :] = (q * ATTN_SCALE).astype(BF16)
            for i, var in enumerate(_head_variants(k)):
                kvar_scr[i, rows, :] = var
            for i, var in enumerate(_head_variants(v)):
                vvar_scr[i, rows, :] = var
            u_scr[rows, :] = z[:, _CC0:_SU0] * z[:, _CI0:_CB0]
            cb_scr[rows, :] = z[:, _CB0:_CC0]
            su_scr[rows, :] = z[:, _SU0:_SV0]
            sv = z[:, _SV0:IN_WIDTH]
            d = sv - _group_sum(sv, ones_bd) * (1.0 / HEAD_DIM)
            vn = d * lax.rsqrt(_group_sum(d * d, ones_bd) * (1.0 / HEAD_DIM) + EPS) * sgug_ref[...]
            vn_scr[rows, :] = vn.astype(BF16)

        loop(n_rows // ROW_BLOCK, project)

    def consume_phase(scr):
        q_scr, kvar_scr, vvar_scr, u_scr, cb_scr, su_scr, vn_scr = scr
        u = u_scr[...]
        pos = lax.broadcasted_iota(I32, u.shape, 0) & (seq - 1)
        up = jnp.where(pos == 0, 0.0, pltpu.roll(u, 1, 0))
        dn = jnp.where(pos == seq - 1, 0.0, pltpu.roll(u, n_rows - 1, 0))
        conv = up * convw_ref[0:1, :] + u * convw_ref[1:2, :] + dn * convw_ref[2:3, :] + convb_ref[...]
        mix_scr[:, ATTN_WIDTH:ATTN_WIDTH + CONV_WIDTH] = (cb_scr[...] * conv).astype(BF16)

        for n in range(n_rows // CHUNK):
            rows = pl.ds(n * CHUNK, CHUNK)
            vn = vn_scr[rows, :]
            lane = lax.broadcasted_iota(I32, vn.shape, 1)
            per_head = [jnp.where((lane >= hd * HEAD_DIM) & (lane < (hd + 1) * HEAD_DIM), vn, jnp.zeros_like(vn))
                        for hd in range(SGU_HEADS)]
            s = sgub_ref[...] + _dot(sguw_ref[...], jnp.concatenate(per_head, axis=0))
            mix_scr[rows, ATTN_WIDTH + CONV_WIDTH:] = (su_scr[rows, :] * s).astype(BF16)

        def attend(s, r):
            rows = block(s * (seq // ATT_BLOCK) + r, ATT_BLOCK)
            keys = pl.ds(s * seq, seq + n_cache)
            for pair in range(N_Q_HEADS // 2):
                qp = q_scr[rows, pair * LANES:(pair + 1) * LANES]
                kv = pair // (N_Q_HEADS // N_KV_HEADS // 2)
                acc = jnp.zeros((ATT_BLOCK, LANES), F32)
                for parity in range(2):
                    sc = _dot_nt(qp, kvar_scr[2 * kv + parity, keys, :])
                    p = jnp.exp(sc - jnp.max(sc, axis=1, keepdims=True))
                    denom = jnp.sum(p, axis=1, keepdims=True)
                    acc = acc + _dot(p.astype(BF16), vvar_scr[2 * kv + parity, keys, :]) / denom
                mix_scr[rows, pair * LANES:(pair + 1) * LANES] = acc.astype(BF16)

        for s in range(n_seq):
            loop(seq // ATT_BLOCK, functools.partial(attend, s))

        def finish(r):
            rows = block(r, ROW_BLOCK)
            mix = _dot(mix_scr[rows, :], wout_ref[...])
            y = DEEPNORM_ALPHA * xres_ref[rows, :] + mod_ref[2:3, :] * mix
            x1_ref[rows, :] = _standardize(y) * ln1g_ref[...] + ln1b_ref[...]

        loop(n_rows // ROW_BLOCK, finish)

    if not skew:
        project_phase(scratch)
        consume_phase(scratch)
        return

    i = pl.program_id(0)
    slot = lax.rem(i, 2)
    mine = [ref.at[slot] for ref in scratch]
    other = [ref.at[1 - slot] for ref in scratch]
    pl.when(i == 0)(functools.partial(project_phase, mine))

    @pl.when((i > 0) & (i < n_steps))
    def _():
        consume_phase(other)
        project_phase(mine)

    pl.when(i == n_steps)(functools.partial(consume_phase, other))


def _full(shape):
    n = len(shape)
    return pl.BlockSpec(shape, lambda *_: (0,) * n)


def _resident(shape):
    n = len(shape)
    return pl.BlockSpec(shape, lambda *_: (0,) * n, pipeline_mode=pl.Buffered(1))


def _layer_block(arr, l, resident=False):
    shape = arr.shape[1:]
    kw = dict(pipeline_mode=pl.Buffered(1)) if resident else {}
    return pl.BlockSpec((None,) + shape, lambda *_: (l,) + (0,) * len(shape), **kw)


def _mixer_scratch(n_rows, n_cache, slots=()):
    nk = n_rows + n_cache
    return [
        pltpu.VMEM(slots + (n_rows, ATTN_WIDTH), BF16),
        pltpu.VMEM(slots + (4, nk, LANES), BF16),
        pltpu.VMEM(slots + (4, nk, LANES), BF16),
        pltpu.VMEM(slots + (n_rows, CONV_WIDTH), F32),
        pltpu.VMEM(slots + (n_rows, CONV_WIDTH), F32),
        pltpu.VMEM(slots + (n_rows, SGU_WIDTH), F32),
        pltpu.VMEM(slots + (n_rows, SGU_WIDTH), BF16),
        pltpu.VMEM((n_rows, D_MODEL), BF16),
    ]


def _side_source(src):
    return src if isinstance(src, tuple) else (src, None)


def _mixer_ctx(x, l, n_seq, seq, mod, w_in, w_out, small, kv_prev=None, side_casts=()):
    per_step = CTX_SEQS_PER_STEP
    rows = per_step * seq
    stack_kv = kv_prev is not None
    cast_weights = w_in.dtype != BF16
    n_steps = n_seq // per_step
    kernel = functools.partial(_mixer_kernel, seq=seq, n_seq=per_step, n_cache=0, rope=False,
                               cast_weights=cast_weights, stack_kv=stack_kv,
                               side_jobs=tuple(len(job) for job in side_casts), n_steps=n_steps)
    cur = lambda i: (jnp.minimum(i, n_steps - 1), 0)
    prev = lambda i: (jnp.maximum(i - 1, 0), 0)
    w_specs = ([_layer_block(w_in, l, resident=True), _layer_block(w_out, l, resident=True)] if cast_weights
               else [_resident(w_in.shape), _resident(w_out.shape)])
    in_specs = ([pl.BlockSpec((rows, D_MODEL), cur), pl.BlockSpec((rows, D_MODEL), prev),
                 pl.BlockSpec((None, None, SUBLANES, D_MODEL), lambda i: (l, 0, 0, 0))] + w_specs
                + [_full(a.shape) if a.ndim == 2 else _layer_block(a, l) for a in small])
    args = [x, x, mod, w_in, w_out, *small]
    if stack_kv:
        in_specs += [pl.BlockSpec((rows, KV_WIDTH), cur)] * 2
        args += list(kv_prev)
        kv_spec = pl.BlockSpec((per_step, DEPTH, seq, KV_WIDTH), lambda i: cur(i) + (0, 0))
        kv_shape = jax.ShapeDtypeStruct((n_seq, DEPTH, seq, KV_WIDTH), F32)
    else:
        kv_spec = pl.BlockSpec((rows, KV_WIDTH), cur)
        kv_shape = jax.ShapeDtypeStruct((n_seq * seq, KV_WIDTH), F32)
    side_out_specs, side_out_shapes = [], []
    for job in side_casts:
        sources = [_side_source(src) for src in job]
        n_rows = sources[0][0].shape[-2]
        n_cols = sum(arr.shape[-1] for arr, _ in sources)
        for arr, layer in sources:
            blk = (n_rows // n_steps, arr.shape[-1])
            if layer is None:
                in_specs.append(pl.BlockSpec(blk, cur))
            else:
                in_specs.append(pl.BlockSpec((None,) + blk, lambda i, layer=layer: (layer,) + cur(i)))
            args.append(arr)
        side_out_specs.append(pl.BlockSpec((n_rows // n_steps, n_cols), cur))
        side_out_shapes.append(jax.ShapeDtypeStruct((n_rows, n_cols), BF16))
    w_out_specs, w_out_shapes = [], []
    if cast_weights:
        w_out_specs = [_full((D_MODEL, IN_WIDTH)), _full((D_MODEL, D_MODEL))]
        w_out_shapes = [jax.ShapeDtypeStruct((D_MODEL, IN_WIDTH), BF16), jax.ShapeDtypeStruct((D_MODEL, D_MODEL), BF16)]
    return pl.pallas_call(
        kernel,
        grid=(n_steps + 1,),
        in_specs=in_specs,
        out_specs=[pl.BlockSpec((rows, D_MODEL), prev), kv_spec, kv_spec] + w_out_specs + side_out_specs,
        out_shape=[jax.ShapeDtypeStruct((n_seq * seq, D_MODEL), F32), kv_shape, kv_shape] + w_out_shapes
        + side_out_shapes,
        scratch_shapes=_mixer_scratch(rows, 0, slots=(2,)),
        compiler_params=pltpu.CompilerParams(dimension_semantics=("arbitrary",), vmem_limit_bytes=VMEM_LIMIT),
        name="mixer_ctx",
    )(*args)


def _mixer_lat(x, l, n_seq, seq, mod, w_in_bf, w_out_bf, small, cos, sin, kc, vc):
    n_cache = kc.shape[2]
    kernel = functools.partial(_mixer_kernel, seq=seq, n_seq=1, n_cache=n_cache, rope=True,
                               cast_weights=False, stack_kv=False)
    cache_spec = pl.BlockSpec((None, None, n_cache, KV_WIDTH), lambda b: (b, l, 0, 0))
    return pl.pallas_call(
        kernel,
        grid=(n_seq,),
        in_specs=([pl.BlockSpec((seq, D_MODEL), lambda b: (b, 0)),
                   pl.BlockSpec((None, None, SUBLANES, D_MODEL), lambda b: (l, 1 + b, 0, 0)),
                   _resident((D_MODEL, IN_WIDTH)), _resident((D_MODEL, D_MODEL))]
                  + [_full(a.shape) if a.ndim == 2 else _layer_block(a, l) for a in small]
                  + [_full((seq, LANES)), _full((seq, LANES)), cache_spec, cache_spec]),
        out_specs=pl.BlockSpec((seq, D_MODEL), lambda b: (b, 0)),
        out_shape=jax.ShapeDtypeStruct((n_seq * seq, D_MODEL), F32),
        scratch_shapes=_mixer_scratch(seq, n_cache),
        compiler_params=pltpu.CompilerParams(dimension_semantics=("arbitrary",), vmem_limit_bytes=VMEM_LIMIT),
        name="mixer_lat",
    )(x, mod, w_in_bf, w_out_bf, *small, cos, sin, kc, vc)


def _two_stream_specs(block_rows, n_a, block_of):
    spec_a = pl.BlockSpec((block_rows, D_MODEL), lambda *ids: (jnp.minimum(block_of(*ids), n_a - 1), 0))
    spec_b = pl.BlockSpec((block_rows, D_MODEL), lambda *ids: (jnp.maximum(block_of(*ids) - n_a, 0), 0))
    return [spec_a, spec_b]


def _group_of_block(blk, block_rows, n_a, lat_seq):
    return jnp.where(blk < n_a, 0, 1 + (blk - n_a) // (lat_seq // block_rows))


def _ffn_kernel(xa_ref, xb_ref, mod_ref, w1_ref, w3_ref, w2_ref, g_ref, b_ref, oa_ref, ob_ref, *, n_a):
    is_a = pl.program_id(0) < n_a
    x = jnp.where(is_a, xa_ref[...], xb_ref[...])
    h = (_standardize(x) * (1.0 + mod_ref[4:5, :]) + mod_ref[3:4, :]).astype(BF16)
    acc = jnp.zeros(x.shape, F32)
    for c in range(D_FF // FFN_COLS):
        cols = slice(c * FFN_COLS, (c + 1) * FFN_COLS)
        a = _dot(h, w1_ref[:, cols])
        b = _dot(h, w3_ref[:, cols])
        acc = acc + _dot((_silu(a) * b).astype(BF16), w2_ref[cols, :])
    y = DEEPNORM_ALPHA * x + mod_ref[5:6, :] * acc
    out = _standardize(y) * g_ref[...] + b_ref[...]

    @pl.when(is_a)
    def _():
        oa_ref[...] = out

    @pl.when(jnp.logical_not(is_a))
    def _():
        ob_ref[...] = out


def _ffn_dense(xa, xb, l, mods, w1, w3, w2, g, b, lat_seq):
    n_a, n_b = xa.shape[0] // FFN_TILE, xb.shape[0] // FFN_TILE
    x_specs = _two_stream_specs(FFN_TILE, n_a, lambda t: t)
    return pl.pallas_call(
        functools.partial(_ffn_kernel, n_a=n_a),
        grid=(n_a + n_b,),
        in_specs=x_specs + [
            pl.BlockSpec((None, None, SUBLANES, D_MODEL),
                         lambda t: (l, _group_of_block(t, FFN_TILE, n_a, lat_seq), 0, 0)),
            _resident(w1.shape), _resident(w3.shape), _resident(w2.shape), _layer_block(g, l), _layer_block(b, l)],
        out_specs=x_specs,
        out_shape=[jax.ShapeDtypeStruct(xa.shape, F32), jax.ShapeDtypeStruct(xb.shape, F32)],
        compiler_params=pltpu.CompilerParams(dimension_semantics=("arbitrary",), vmem_limit_bytes=VMEM_LIMIT),
        name="ffn_dense",
    )(xa, xb, mods, w1, w3, w2, g, b)


def _router_kernel(xa_ref, xb_ref, mod_ref, rw_ref, before_ref, dest_ref, gate_ref, start_ref, padded_ref, *,
                   n_a, lat_seq):
    c = pl.program_id(0)
    w_hi, w_lo = _split(rw_ref[...])
    n_blocks = MOE_CHUNK // ROUTER_BLOCK
    parts = []
    for blk in range(n_blocks):
        rows = pl.ds(blk * ROUTER_BLOCK, ROUTER_BLOCK)
        mod = mod_ref[_group_of_block(c * n_blocks + blk, ROUTER_BLOCK, n_a * n_blocks, lat_seq)]
        x = jnp.where(c < n_a, xa_ref[rows, :], xb_ref[rows, :])
        h = _standardize(x) * (1.0 + mod[4:5, :]) + mod[3:4, :]
        h_hi, h_lo = _split(h)
        parts.append(_dot_nt(w_hi, h_hi) + _dot_nt(w_hi, h_lo) + _dot_nt(w_lo, h_hi))
    logits = jnp.concatenate(parts, axis=1)
    eid = lax.broadcasted_iota(I32, logits.shape, 0).astype(F32)
    m1 = jnp.max(logits, axis=0, keepdims=True)
    i1 = jnp.min(jnp.where(logits == m1, eid, float(N_EXPERTS)), axis=0, keepdims=True)
    oh1 = eid == i1
    rest = jnp.where(oh1, -jnp.inf, logits)
    m2 = jnp.max(rest, axis=0, keepdims=True)
    i2 = jnp.min(jnp.where(rest == m2, eid, float(N_EXPERTS)), axis=0, keepdims=True)
    oh2 = eid == i2
    e = jnp.exp(m2 - m1)
    gate_ref[0:1, :] = 1.0 / (1.0 + e)
    gate_ref[1:2, :] = e / (1.0 + e)
    sel = jnp.where(oh1 | oh2, 1.0, 0.0)
    ranks = []
    seen = jnp.zeros((N_EXPERTS, 1), F32)
    for blk in range(n_blocks):
        s_blk = sel[:, blk * ROUTER_BLOCK:(blk + 1) * ROUTER_BLOCK]
        ranks.append(_dot(s_blk.astype(BF16), before_ref[...]) + seen)
        seen = seen + jnp.sum(s_blk, axis=1, keepdims=True)
    rank = jnp.concatenate(ranks, axis=1)
    eid_out = lax.broadcasted_iota(I32, start_ref.shape, 0).astype(F32)
    start = jnp.zeros(sel.shape, F32)
    start_out = jnp.zeros(start_ref.shape, F32)
    padded_out = jnp.zeros(start_ref.shape, F32)
    for ex in range(N_EXPERTS):
        cnt = jnp.sum(sel[ex:ex + 1, :], axis=1, keepdims=True)
        padded = jnp.ceil(cnt * (1.0 / MOE_GRAN)) * MOE_GRAN
        start = start + jnp.where(eid > ex, padded, 0.0)
        start_out = start_out + jnp.where(eid_out > ex, padded, 0.0)
        padded_out = padded_out + jnp.where(eid_out == ex, padded, 0.0)
    row = (start + rank) * SUBLANES
    dest_ref[0:1, :] = jnp.sum(jnp.where(oh1, row, 0.0), axis=0, keepdims=True).astype(I32)
    dest_ref[1:2, :] = jnp.sum(jnp.where(oh2, row, 0.0), axis=0, keepdims=True).astype(I32)
    start_ref[...] = start_out.astype(I32)
    padded_ref[...] = padded_out.astype(I32)


def _router(xa, xb, l, mods, rw_t, lat_seq):
    n_a = xa.shape[0] // MOE_CHUNK
    n_chunks = n_a + xb.shape[0] // MOE_CHUNK
    tok = np.arange(ROUTER_BLOCK)
    before = jnp.asarray(tok[:, None] < tok[None, :], BF16)
    return pl.pallas_call(
        functools.partial(_router_kernel, n_a=n_a, lat_seq=lat_seq),
        grid=(n_chunks,),
        in_specs=_two_stream_specs(MOE_CHUNK, n_a, lambda c: c) + [
                  _layer_block(mods, l),
                  _full((N_EXPERTS, D_MODEL)), _full((ROUTER_BLOCK, ROUTER_BLOCK))],
        out_specs=[pl.BlockSpec((None, 2, MOE_CHUNK), lambda c: (c, 0, 0)),
                   pl.BlockSpec((None, 2, MOE_CHUNK), lambda c: (c, 0, 0)),
                   pl.BlockSpec((None, N_EXPERTS, LANES), lambda c: (c, 0, 0)),
                   pl.BlockSpec((None, N_EXPERTS, LANES), lambda c: (c, 0, 0))],
        out_shape=[jax.ShapeDtypeStruct((n_chunks, 2, MOE_CHUNK), I32),
                   jax.ShapeDtypeStruct((n_chunks, 2, MOE_CHUNK), F32),
                   jax.ShapeDtypeStruct((n_chunks, N_EXPERTS, LANES), I32),
                   jax.ShapeDtypeStruct((n_chunks, N_EXPERTS, LANES), I32)],
        compiler_params=pltpu.CompilerParams(dimension_semantics=("arbitrary",), vmem_limit_bytes=VMEM_LIMIT),
        name="moe_router",
    )(xa, xb, mods, rw_t, before)


def _token_rows(first_row):
    return pl.ds(pl.multiple_of(first_row, SUBLANES), SUBLANES)


def _store_token_major(ref, tok0, val):
    for cc in range(D_MODEL // LANES):
        ref[pl.ds(tok0 * SUBLANES + cc, val.shape[0], stride=SUBLANES), :] = val[:, cc * LANES:(cc + 1) * LANES]


def _load_token_major(ref, tok0, n):
    return jnp.concatenate(
        [ref[pl.ds(tok0 * SUBLANES + cc, n, stride=SUBLANES), :] for cc in range(D_MODEL // LANES)], axis=1)


def _moe_kernel(start_ref, padded_ref, xa_ref, xb_ref, mod_ref, dest_ref, gate_ref, w13_ref, w2_ref,
                g_ref, b_ref, oa_ref, ob_ref, tok_scr, rows_scr, *, n_a):
    c = pl.program_id(0)
    j = pl.program_id(1)
    is_a = c < n_a
    first_expert_step = MOE_TOK_STEPS
    first_combine_step = MOE_TOK_STEPS + N_EXPERTS

    @pl.when((c == 0) & (j == 0))
    def _init():
        rows_scr[...] = jnp.zeros(rows_scr.shape, F32)

    @pl.when(j < first_expert_step)
    def _dispatch():
        for r in range(MOE_TOK_BLOCK // ROW_BLOCK):
            rows = pl.ds(r * ROW_BLOCK, ROW_BLOCK)
            x = jnp.where(is_a, xa_ref[rows, :], xb_ref[rows, :])
            h = _standardize(x) * (1.0 + mod_ref[4:5, :]) + mod_ref[3:4, :]
            _store_token_major(tok_scr, r * ROW_BLOCK, h)
        t0 = j * MOE_TOK_BLOCK

        def body(t, carry):
            row = tok_scr[_token_rows(t * SUBLANES), :]
            rows_scr[_token_rows(dest_ref[t0 + t]), :] = row
            rows_scr[_token_rows(dest_ref[MOE_CHUNK + t0 + t]), :] = row
            return carry
        lax.fori_loop(0, MOE_TOK_BLOCK, body, 0, unroll=8)

    def experts(row0, m):
        xin = _load_token_major(rows_scr, row0, m).astype(BF16)
        ab = _dot(xin, w13_ref[...])
        a, b = ab[:, :D_FF_EXPERT], ab[:, D_FF_EXPERT:]
        y = _dot((_silu(a) * b).astype(BF16), w2_ref[...])
        _store_token_major(rows_scr, row0, y)

    @pl.when((j >= first_expert_step) & (j < first_combine_step))
    def _experts():
        region = c * N_EXPERTS + (j - first_expert_step)
        start = start_ref[region]
        padded = padded_ref[region]
        n_full = lax.shift_right_logical(padded, MOE_BLOCK.bit_length() - 1)

        def body(i, carry):
            experts(start + i * MOE_BLOCK, MOE_BLOCK)
            return carry
        lax.fori_loop(0, n_full, body, 0)
        for m in range(MOE_GRAN, MOE_BLOCK, MOE_GRAN):
            pl.when(padded - n_full * MOE_BLOCK == m)(
                functools.partial(experts, start + n_full * MOE_BLOCK, m))

    @pl.when(j >= first_combine_step)
    def _combine():
        t0 = (j - first_combine_step) * MOE_TOK_BLOCK

        def body(t, carry):
            y0 = rows_scr[_token_rows(dest_ref[t0 + t]), :]
            y1 = rows_scr[_token_rows(dest_ref[MOE_CHUNK + t0 + t]), :]
            tok_scr[_token_rows(t * SUBLANES), :] = gate_ref[t0 + t] * y0 + gate_ref[MOE_CHUNK + t0 + t] * y1
            return carry
        lax.fori_loop(0, MOE_TOK_BLOCK, body, 0, unroll=8)
        for r in range(MOE_TOK_BLOCK // ROW_BLOCK):
            rows = pl.ds(r * ROW_BLOCK, ROW_BLOCK)
            ffn = _load_token_major(tok_scr, r * ROW_BLOCK, ROW_BLOCK)
            x = jnp.where(is_a, xa_ref[rows, :], xb_ref[rows, :])
            y = DEEPNORM_ALPHA * x + mod_ref[5:6, :] * ffn
            out = _standardize(y) * g_ref[...] + b_ref[...]

            @pl.when(is_a)
            def _():
                oa_ref[rows, :] = out

            @pl.when(jnp.logical_not(is_a))
            def _():
                ob_ref[rows, :] = out


def _ffn_moe(xa, xb, l, mods, rw_t, w13, w2, g, b, lat_seq):
    n_a = xa.shape[0] // MOE_CHUNK
    n_chunks = n_a + xb.shape[0] // MOE_CHUNK
    n_a_blocks = n_a * MOE_TOK_STEPS
    dest, gates, start, padded = _router(xa, xb, l, mods, rw_t, lat_seq)
    first_expert_step = MOE_TOK_STEPS
    first_combine_step = MOE_TOK_STEPS + N_EXPERTS

    def token_block(c, j, *_):
        blk = jnp.where(j < first_combine_step, jnp.minimum(j, MOE_TOK_STEPS - 1), j - first_combine_step)
        return c * MOE_TOK_STEPS + blk

    def out_token_block(c, j, *_):
        return c * MOE_TOK_STEPS + jnp.maximum(j - first_combine_step, 0)

    def mod_of(c, j, *_):
        return (l, _group_of_block(token_block(c, j), MOE_TOK_BLOCK, n_a_blocks, lat_seq), 0, 0)

    def expert_of(c, j, *_):
        return (jnp.clip(j - first_expert_step, 0, N_EXPERTS - 1), 0, 0)

    grid_spec = pltpu.PrefetchScalarGridSpec(
        num_scalar_prefetch=2,
        grid=(n_chunks, MOE_STEPS),
        in_specs=_two_stream_specs(MOE_TOK_BLOCK, n_a_blocks, token_block) + [
                  pl.BlockSpec((None, None, SUBLANES, D_MODEL), mod_of),
                  pl.BlockSpec((2 * MOE_CHUNK,), lambda c, j, *_: (c,), memory_space=pltpu.SMEM),
                  pl.BlockSpec((2 * MOE_CHUNK,), lambda c, j, *_: (c,), memory_space=pltpu.SMEM),
                  pl.BlockSpec((None, D_MODEL, 2 * D_FF_EXPERT), expert_of),
                  pl.BlockSpec((None, D_FF_EXPERT, D_MODEL), expert_of),
                  _layer_block(g, l), _layer_block(b, l)],
        out_specs=_two_stream_specs(MOE_TOK_BLOCK, n_a_blocks, out_token_block),
        scratch_shapes=[pltpu.VMEM((MOE_TOK_BLOCK * SUBLANES, LANES), F32),
                        pltpu.VMEM((MOE_ROWS * SUBLANES, LANES), F32)],
    )
    return pl.pallas_call(
        functools.partial(_moe_kernel, n_a=n_a),
        grid_spec=grid_spec,
        out_shape=[jax.ShapeDtypeStruct(xa.shape, F32), jax.ShapeDtypeStruct(xb.shape, F32)],
        compiler_params=pltpu.CompilerParams(
            dimension_semantics=("arbitrary", "arbitrary"), vmem_limit_bytes=VMEM_LIMIT),
        name="moe_experts",
    )(start[:, :, 0].reshape(-1), padded[:, :, 0].reshape(-1), xa, xb, mods,
      dest.reshape(-1), gates.reshape(-1), w13, w2, g, b)


def _rope_tables(n_tokens):
    t = np.arange(n_tokens)
    row = (t // GRID_W).astype(np.float32)
    col = (t % GRID_W).astype(np.float32)
    inv_freq = (np.float32(ROPE_THETA) ** (-np.arange(0, AXIS_ROT, 2, dtype=np.float32) / AXIS_ROT)).astype(np.float32)
    ang_r = row[:, None] * inv_freq
    ang_c = col[:, None] * inv_freq
    cos = np.concatenate([np.cos(ang_r), np.cos(ang_r), np.cos(ang_c), np.cos(ang_c)], axis=1)
    sin = np.concatenate([-np.sin(ang_r), np.sin(ang_r), -np.sin(ang_c), np.sin(ang_c)], axis=1)
    return jnp.asarray(np.tile(cos, (1, 2)), F32), jnp.asarray(np.tile(sin, (1, 2)), F32)


def kernel(x_prompt, x_sample, cache_k, cache_v, c, c_ctx, ada_w, ada_b, w_in, q_norm_g, k_norm_g, conv_w, conv_b, sgu_norm_g, sgu_w, sgu_b, w_out, ln1_g, ln1_b, ln2_g, ln2_b, ffn_w1, ffn_w3, ffn_w2, router_w, moe_w1, moe_w3, moe_w2):
    batch, seq, _ = x_prompt.shape
    dec_batch, dec_seq, _ = x_sample.shape
    past_len = cache_k.shape[2]
    n_ctx = batch * seq
    n_lat = dec_batch * dec_seq
    assert DEPTH == 2 and 1 + dec_batch <= SUBLANES
    assert seq == ATT_BLOCK and CTX_SEQS_PER_STEP * seq == ROW_BLOCK and batch % CTX_SEQS_PER_STEP == 0
    assert dec_seq % ROW_BLOCK == 0 and dec_seq & (dec_seq - 1) == 0
    assert n_ctx % MOE_CHUNK == 0 and n_lat % MOE_CHUNK == 0
    assert dec_seq % ROUTER_BLOCK == 0 and dec_seq % MOE_TOK_BLOCK == 0 and dec_seq % FFN_TILE == 0

    cond = jnp.zeros((SUBLANES, D_MODEL), F32).at[0].set(c_ctx).at[1:1 + dec_batch].set(c)
    mod = _modulation(cond, ada_w, ada_b)
    mod = mod.reshape(DEPTH, SUBLANES, 6, D_MODEL)[:, :1 + dec_batch]
    mod = jnp.pad(mod, ((0, 0), (0, 0), (0, SUBLANES - 6), (0, 0)))

    lane_id = np.arange(GROUP_TILE) // HEAD_DIM
    ones_bd = jnp.asarray(lane_id[:, None] == lane_id[None, :], BF16)
    cos, sin = _rope_tables(dec_seq)
    small = (
        jnp.tile(q_norm_g, (1, N_Q_HEADS))[:, None, :], jnp.tile(k_norm_g, (1, N_KV_HEADS))[:, None, :],
        ones_bd,
        conv_w, conv_b[:, None, :], sgu_norm_g[:, None, :],
        jnp.swapaxes(sgu_w, 1, 2).reshape(DEPTH, CHUNK, SGU_HEADS * CHUNK).astype(BF16),
        jnp.repeat(jnp.swapaxes(sgu_b, 1, 2), HEAD_DIM, axis=2),
        ln1_g[:, None, :], ln1_b[:, None, :],
    )
    g2, b2 = ln2_g[:, None, :], ln2_b[:, None, :]
    kc = cache_k.reshape(dec_batch, DEPTH, past_len, KV_WIDTH)
    vc = cache_v.reshape(dec_batch, DEPTH, past_len, KV_WIDTH)

    xs = [x_prompt.reshape(n_ctx, D_MODEL), x_sample.reshape(n_lat, D_MODEL)]
    assert DEPTH == 2 and ffn_w1.shape[0] == 1 and moe_w1.shape[0] == 1
    side_casts = [[(ffn_w1[0],), (ffn_w3[0],), (ffn_w2[0],), ((w_in, 1),), ((w_out, 1),), (moe_w2.reshape(-1, D_MODEL),)],
                  [(moe_w1.reshape(-1, D_FF_EXPERT), moe_w3.reshape(-1, D_FF_EXPERT))]]
    kv = None
    for l in range(DEPTH):
        if l == 0:
            x_ctx, k_ctx, v_ctx, w_in_bf, w_out_bf, *ffn_bf, w_in_next, w_out_next, w2_bf = _mixer_ctx(
                xs[0], l, batch, seq, mod, w_in, w_out, small, kv, side_casts[l])
        else:
            w_in_bf, w_out_bf = w_in_next, w_out_next
            x_ctx, k_ctx, v_ctx, w13_bf = _mixer_ctx(
                xs[0], l, batch, seq, mod, w_in_bf, w_out_bf, small, kv, side_casts[l])
        kv = (k_ctx, v_ctx)
        x_lat = _mixer_lat(xs[1], l, dec_batch, dec_seq, mod, w_in_bf, w_out_bf, small, cos, sin, kc, vc)
        if l % 2 == 0:
            xs = _ffn_dense(x_ctx, x_lat, l, mod, *ffn_bf, g2, b2, dec_seq)
        else:
            ws = (w13_bf.reshape(N_EXPERTS, D_MODEL, 2 * D_FF_EXPERT), w2_bf.reshape(N_EXPERTS, D_FF_EXPERT, D_MODEL))
            xs = _ffn_moe(x_ctx, x_lat, l, mod, router_w[l // 2].T, *ws, g2, b2, dec_seq)
    y_p = xs[0].reshape(batch, seq, D_MODEL)
    y_s = xs[1].reshape(dec_batch, dec_seq, D_MODEL)
    new_k = kv[0].reshape(batch, DEPTH, seq, N_KV_HEADS, HEAD_DIM)
    new_v = kv[1].reshape(batch, DEPTH, seq, N_KV_HEADS, HEAD_DIM)
    return (y_p, y_s, new_k, new_v)
```

```python
import functools

import numpy as np
import jax
import jax.numpy as jnp
from jax import lax
from jax.experimental import pallas as pl
from jax.experimental.pallas import tpu as pltpu

F32 = jnp.float32
BF16 = jnp.bfloat16
I32 = jnp.int32

D_MODEL = 1024
DEPTH = 2
GRID_W = 64
HEAD_DIM = 64
N_Q_HEADS = 8
N_KV_HEADS = 2
ATTN_WIDTH = N_Q_HEADS * HEAD_DIM
KV_WIDTH = N_KV_HEADS * HEAD_DIM
ATTN_SCALE = HEAD_DIM ** -0.5
ROPE_THETA = 10000.0
AXIS_ROT = HEAD_DIM // 2
CONV_WIDTH = 256
SGU_WIDTH = 256
SGU_HEADS = 4
CHUNK = 128
IN_WIDTH = 2048
D_FF = 2816
N_EXPERTS = 8
D_FF_EXPERT = 1408
EPS = 1e-6
DEEPNORM_ALPHA = (2 * DEPTH) ** 0.25

LANES = 128
SUBLANES = 8
ROW_BLOCK = 512
GROUP_TILE = 256
ATT_BLOCK = 256
CTX_SEQS_PER_STEP = 2
FFN_TILE = 1024
FFN_COLS = 256
MOE_CHUNK = 2048
ROUTER_BLOCK = 1024
MOE_GRAN = 128
MOE_BLOCK = 256
MOE_TOK_BLOCK = 512
MOE_TOK_STEPS = MOE_CHUNK // MOE_TOK_BLOCK
MOE_STEPS = 2 * MOE_TOK_STEPS + N_EXPERTS
MOE_ROWS = 2 * MOE_CHUNK + N_EXPERTS * MOE_GRAN
VMEM_LIMIT = 60 * 1024 * 1024

_Q0, _K0, _V0, _CI0, _CB0, _CC0, _SU0, _SV0 = 0, 512, 640, 768, 1024, 1280, 1536, 1792


def _dot(a, b):
    return jnp.dot(a, b, preferred_element_type=F32)


def _dot_nt(a, b):
    return lax.dot_general(a, b, (((1,), (1,)), ((), ())), preferred_element_type=F32)


def _split(x):
    hi = x.astype(BF16)
    lo = (x - hi.astype(F32)).astype(BF16)
    return hi, lo


def _group_sum(x, ones_bd):
    outs = []
    for c0 in range(0, x.shape[1], GROUP_TILE):
        width = min(GROUP_TILE, x.shape[1] - c0)
        outs.append(_dot(x[:, c0:c0 + width].astype(BF16), ones_bd[:width, :width]))
    return outs[0] if len(outs) == 1 else jnp.concatenate(outs, axis=1)


def _standardize(x):
    mu = jnp.mean(x, axis=-1, keepdims=True)
    d = x - mu
    return d * lax.rsqrt(jnp.mean(d * d, axis=-1, keepdims=True) + EPS)


def _silu(x):
    return x / (1.0 + jnp.exp(-x))


def _modulation_kernel(cond_ref, w_ref, b_ref, o_ref):
    s_hi, s_lo = _split(_silu(cond_ref[...]))
    w_hi, w_lo = _split(w_ref[...])
    o_ref[...] = _dot(s_hi, w_hi) + _dot(s_hi, w_lo) + _dot(s_lo, w_hi) + b_ref[...]


def _modulation(cond, ada_w, ada_b):
    n_out = ada_w.shape[-1]
    tn = 1536
    return pl.pallas_call(
        _modulation_kernel,
        grid=(DEPTH, n_out // tn),
        in_specs=[
            pl.BlockSpec((SUBLANES, D_MODEL), lambda l, j: (0, 0)),
            pl.BlockSpec((None, D_MODEL, tn), lambda l, j: (l, 0, j)),
            pl.BlockSpec((None, 1, tn), lambda l, j: (l, 0, j)),
        ],
        out_specs=pl.BlockSpec((None, SUBLANES, tn), lambda l, j: (l, 0, j)),
        out_shape=jax.ShapeDtypeStruct((DEPTH, SUBLANES, n_out), F32),
        compiler_params=pltpu.CompilerParams(
            dimension_semantics=("arbitrary", "arbitrary"), vmem_limit_bytes=VMEM_LIMIT),
        name="modulation",
    )(cond, ada_w, ada_b.reshape(DEPTH, 1, n_out))


def _rope(x, cos, sin_signed):
    w = x.shape[1]
    lane = lax.broadcasted_iota(I32, x.shape, 1)
    first_half = (lane & 31) < 16
    partner = jnp.where(first_half, pltpu.roll(x, w - 16, 1), pltpu.roll(x, 16, 1))
    return x * cos + partner * sin_signed


def _head_variants(x):
    lane = lax.broadcasted_iota(I32, x.shape, 1)
    lo = lane < HEAD_DIM
    xr = pltpu.roll(x, HEAD_DIM, 1)
    zero = jnp.zeros_like(x)
    return (jnp.where(lo, x, zero).astype(BF16), jnp.where(lo, zero, xr).astype(BF16),
            jnp.where(lo, xr, zero).astype(BF16), jnp.where(lo, zero, x).astype(BF16))


def _mixer_kernel(*refs, seq, n_seq, n_cache, rope, cast_weights, stack_kv, side_jobs=(), n_steps=None):
    refs = list(refs)
    skew = n_steps is not None

    def take(n):
        out, refs[:] = refs[:n], refs[n:]
        return out

    (x_ref,) = take(1)
    xres_ref = take(1)[0] if skew else x_ref
    mod_ref, win_ref, wout_ref = take(3)
    qg_ref, kg_ref, ones_ref, convw_ref, convb_ref, sgug_ref, sguw_ref, sgub_ref, ln1g_ref, ln1b_ref = take(10)
    if rope:
        cos_ref, sin_ref, kc_ref, vc_ref = take(4)
    if stack_kv:
        kprev_ref, vprev_ref = take(2)
    side_in = [take(n_src) for n_src in side_jobs]
    (x1_ref,) = take(1)
    if not rope:
        k_ref, v_ref = take(2)
    if cast_weights:
        winb_ref, woutb_ref = take(2)

        @pl.when(pl.program_id(0) == 0)
        def _cast():
            winb_ref[...] = win_ref[...].astype(BF16)
            woutb_ref[...] = wout_ref[...].astype(BF16)
        win_ref, wout_ref = winb_ref, woutb_ref
    side_out = take(len(side_jobs))
    scratch = take(7)
    (mix_scr,) = take(1)
    n_rows = n_seq * seq
    assert n_cache == 0 or n_seq == 1

    def loop(n, body):
        if n == 1:
            body(0)
        else:
            def step(r, carry):
                body(r)
                return carry
            lax.fori_loop(0, n, step, 0, unroll=2)

    def block(r, size):
        if isinstance(r, int):
            return pl.ds(r * size, size)
        return pl.ds(pl.multiple_of(r * size, size), size)

    def project_phase(scr):
        q_scr, kvar_scr, vvar_scr, u_scr, cb_scr, su_scr, vn_scr = scr
        for srcs, dst in zip(side_in, side_out):
            col = 0
            for src in srcs:
                dst[:, col:col + src.shape[1]] = src[...].astype(BF16)
                col += src.shape[1]
        if n_cache:
            for i, var in enumerate(_head_variants(kc_ref[...])):
                kvar_scr[i, pl.ds(seq, n_cache), :] = var
            for i, var in enumerate(_head_variants(vc_ref[...])):
                vvar_scr[i, pl.ds(seq, n_cache), :] = var

        def project(r):
            rows = block(r, ROW_BLOCK)
            x = x_ref[rows, :]
            h = _standardize(x) * (1.0 + mod_ref[1:2, :]) + mod_ref[0:1, :]
            z = _dot(h.astype(BF16), win_ref[...])
            ones_bd = ones_ref[...]
            zq = z[:, _Q0:_K0]
            q = zq * lax.rsqrt(_group_sum(zq * zq, ones_bd) * (1.0 / HEAD_DIM) + EPS) * qg_ref[...]
            zk = z[:, _K0:_V0]
            k = zk * lax.rsqrt(_group_sum(zk * zk, ones_bd) * (1.0 / HEAD_DIM) + EPS) * kg_ref[...]
            v = z[:, _V0:_CI0]
            if rope:
                cos = cos_ref[rows, :]
                sin = sin_ref[rows, :]
                q = _rope(q, jnp.concatenate([cos] * 4, axis=1), jnp.concatenate([sin] * 4, axis=1))
                k = _rope(k, cos, sin)
            elif stack_kv:
                for s in range(ROW_BLOCK // seq):
                    sub = slice(s * seq, (s + 1) * seq)
                    k_ref[s, 0] = kprev_ref[sub, :]
                    v_ref[s, 0] = vprev_ref[sub, :]
                    k_ref[s, 1] = k[sub, :]
                    v_ref[s, 1] = v[sub, :]
            else:
                k_ref[rows, :] = k
                v_ref[rows, :] = v
            q_scr[rows, :] = (q * ATTN_SCALE).astype(BF16)
            for i, var in enumerate(_head_variants(k)):
                kvar_scr[i, rows, :] = var
            for i, var in enumerate(_head_variants(v)):
                vvar_scr[i, rows, :] = var
            u_scr[rows, :] = z[:, _CC0:_SU0] * z[:, _CI0:_CB0]
            cb_scr[rows, :] = z[:, _CB0:_CC0]
            su_scr[rows, :] = z[:, _SU0:_SV0]
            sv = z[:, _SV0:IN_WIDTH]
            d = sv - _group_sum(sv, ones_bd) * (1.0 / HEAD_DIM)
            vn = d * lax.rsqrt(_group_sum(d * d, ones_bd) * (1.0 / HEAD_DIM) + EPS) * sgug_ref[...]
            vn_scr[rows, :] = vn.astype(BF16)

        loop(n_rows // ROW_BLOCK, project)

    def consume_phase(scr):
        q_scr, kvar_scr, vvar_scr, u_scr, cb_scr, su_scr, vn_scr = scr
        u = u_scr[...]
        pos = lax.broadcasted_iota(I32, u.shape, 0) & (seq - 1)
        up = jnp.where(pos == 0, 0.0, pltpu.roll(u, 1, 0))
        dn = jnp.where(pos == seq - 1, 0.0, pltpu.roll(u, n_rows - 1, 0))
        conv = up * convw_ref[0:1, :] + u * convw_ref[1:2, :] + dn * convw_ref[2:3, :] + convb_ref[...]
        mix_scr[:, ATTN_WIDTH:ATTN_WIDTH + CONV_WIDTH] = (cb_scr[...] * conv).astype(BF16)

        for n in range(n_rows // CHUNK):
            rows = pl.ds(n * CHUNK, CHUNK)
            vn = vn_scr[rows, :]
            lane = lax.broadcasted_iota(I32, vn.shape, 1)
            per_head = [jnp.where((lane >= hd * HEAD_DIM) & (lane < (hd + 1) * HEAD_DIM), vn, jnp.zeros_like(vn))
                        for hd in range(SGU_HEADS)]
            s = sgub_ref[...] + _dot(sguw_ref[...], jnp.concatenate(per_head, axis=0))
            mix_scr[rows, ATTN_WIDTH + CONV_WIDTH:] = (su_scr[rows, :] * s).astype(BF16)

        def attend(s, r):
            rows = block(s * (seq // ATT_BLOCK) + r, ATT_BLOCK)
            keys = pl.ds(s * seq, seq + n_cache)
            for pair in range(N_Q_HEADS // 2):
                qp = q_scr[rows, pair * LANES:(pair + 1) * LANES]
                kv = pair // (N_Q_HEADS // N_KV_HEADS // 2)
                acc = jnp.zeros((ATT_BLOCK, LANES), F32)
                for parity in range(2):
                    sc = _dot_nt(qp, kvar_scr[2 * kv + parity, keys, :])
                    p = jnp.exp(sc - jnp.max(sc, axis=1, keepdims=True))
                    denom = jnp.sum(p, axis=1, keepdims=True)
                    acc = acc + _dot(p.astype(BF16), vvar_scr[2 * kv + parity, keys, :]) / denom
                mix_scr[rows, pair * LANES:(pair + 1) * LANES] = acc.astype(BF16)

        for s in range(n_seq):
            loop(seq // ATT_BLOCK, functools.partial(attend, s))

        def finish(r):
            rows = block(r, ROW_BLOCK)
            mix = _dot(mix_scr[rows, :], wout_ref[...])
            y = DEEPNORM_ALPHA * xres_ref[rows, :] + mod_ref[2:3, :] * mix
            x1_ref[rows, :] = _standardize(y) * ln1g_ref[...] + ln1b_ref[...]

        loop(n_rows // ROW_BLOCK, finish)

    if not skew:
        project_phase(scratch)
        consume_phase(scratch)
        return

    i = pl.program_id(0)
    slot = lax.rem(i, 2)
    mine = [ref.at[slot] for ref in scratch]
    other = [ref.at[1 - slot] for ref in scratch]
    pl.when(i == 0)(functools.partial(project_phase, mine))

    @pl.when((i > 0) & (i < n_steps))
    def _():
        consume_phase(other)
        project_phase(mine)

    pl.when(i == n_steps)(functools.partial(consume_phase, other))


def _full(shape):
    n = len(shape)
    return pl.BlockSpec(shape, lambda *_: (0,) * n)


def _resident(shape):
    n = len(shape)
    return pl.BlockSpec(shape, lambda *_: (0,) * n, pipeline_mode=pl.Buffered(1))


def _layer_block(arr, l, resident=False):
    shape = arr.shape[1:]
    kw = dict(pipeline_mode=pl.Buffered(1)) if resident else {}
    return pl.BlockSpec((None,) + shape, lambda *_: (l,) + (0,) * len(shape), **kw)


def _mixer_scratch(n_rows, n_cache, slots=()):
    nk = n_rows + n_cache
    return [
        pltpu.VMEM(slots + (n_rows, ATTN_WIDTH), BF16),
        pltpu.VMEM(slots + (4, nk, LANES), BF16),
        pltpu.VMEM(slots + (4, nk, LANES), BF16),
        pltpu.VMEM(slots + (n_rows, CONV_WIDTH), F32),
        pltpu.VMEM(slots + (n_rows, CONV_WIDTH), F32),
        pltpu.VMEM(slots + (n_rows, SGU_WIDTH), F32),
        pltpu.VMEM(slots + (n_rows, SGU_WIDTH), BF16),
        pltpu.VMEM((n_rows, D_MODEL), BF16),
    ]


def _side_source(src):
    return src if isinstance(src, tuple) else (src, None)


def _mixer_ctx(x, l, n_seq, seq, mod, w_in, w_out, small, kv_prev=None, side_casts=()):
    per_step = CTX_SEQS_PER_STEP
    rows = per_step * seq
    stack_kv = kv_prev is not None
    cast_weights = w_in.dtype != BF16
    n_steps = n_seq // per_step
    kernel = functools.partial(_mixer_kernel, seq=seq, n_seq=per_step, n_cache=0, rope=False,
                               cast_weights=cast_weights, stack_kv=stack_kv,
                               side_jobs=tuple(len(job) for job in side_casts), n_steps=n_steps)
    cur = lambda i: (jnp.minimum(i, n_steps - 1), 0)
    prev = lambda i: (jnp.maximum(i - 1, 0), 0)
    w_specs = ([_layer_block(w_in, l, resident=True), _layer_block(w_out, l, resident=True)] if cast_weights
               else [_resident(w_in.shape), _resident(w_out.shape)])
    in_specs = ([pl.BlockSpec((rows, D_MODEL), cur), pl.BlockSpec((rows, D_MODEL), prev),
                 pl.BlockSpec((None, None, SUBLANES, D_MODEL), lambda i: (l, 0, 0, 0))] + w_specs
                + [_full(a.shape) if a.ndim == 2 else _layer_block(a, l) for a in small])
    args = [x, x, mod, w_in, w_out, *small]
    if stack_kv:
        in_specs += [pl.BlockSpec((rows, KV_WIDTH), cur)] * 2
        args += list(kv_prev)
        kv_spec = pl.BlockSpec((per_step, DEPTH, seq, KV_WIDTH), lambda i: cur(i) + (0, 0))
        kv_shape = jax.ShapeDtypeStruct((n_seq, DEPTH, seq, KV_WIDTH), F32)
    else:
        kv_spec = pl.BlockSpec((rows, KV_WIDTH), cur)
        kv_shape = jax.ShapeDtypeStruct((n_seq * seq, KV_WIDTH), F32)
    side_out_specs, side_out_shapes = [], []
    for job in side_casts:
        sources = [_side_source(src) for src in job]
        n_rows = sources[0][0].shape[-2]
        n_cols = sum(arr.shape[-1] for arr, _ in sources)
        for arr, layer in sources:
            blk = (n_rows // n_steps, arr.shape[-1])
            if layer is None:
                in_specs.append(pl.BlockSpec(blk, cur))
            else:
                in_specs.append(pl.BlockSpec((None,) + blk, lambda i, layer=layer: (layer,) + cur(i)))
            args.append(arr)
        side_out_specs.append(pl.BlockSpec((n_rows // n_steps, n_cols), cur))
        side_out_shapes.append(jax.ShapeDtypeStruct((n_rows, n_cols), BF16))
    w_out_specs, w_out_shapes = [], []
    if cast_weights:
        w_out_specs = [_full((D_MODEL, IN_WIDTH)), _full((D_MODEL, D_MODEL))]
        w_out_shapes = [jax.ShapeDtypeStruct((D_MODEL, IN_WIDTH), BF16), jax.ShapeDtypeStruct((D_MODEL, D_MODEL), BF16)]
    return pl.pallas_call(
        kernel,
        grid=(n_steps + 1,),
        in_specs=in_specs,
        out_specs=[pl.BlockSpec((rows, D_MODEL), prev), kv_spec, kv_spec] + w_out_specs + side_out_specs,
        out_shape=[jax.ShapeDtypeStruct((n_seq * seq, D_MODEL), F32), kv_shape, kv_shape] + w_out_shapes
        + side_out_shapes,
        scratch_shapes=_mixer_scratch(rows, 0, slots=(2,)),
        compiler_params=pltpu.CompilerParams(dimension_semantics=("arbitrary",), vmem_limit_bytes=VMEM_LIMIT),
        name="mixer_ctx",
    )(*args)


def _mixer_lat(x, l, n_seq, seq, mod, w_in_bf, w_out_bf, small, cos, sin, kc, vc):
    n_cache = kc.shape[2]
    kernel = functools.partial(_mixer_kernel, seq=seq, n_seq=1, n_cache=n_cache, rope=True,
                               cast_weights=False, stack_kv=False)
    cache_spec = pl.BlockSpec((None, None, n_cache, KV_WIDTH), lambda b: (b, l, 0, 0))
    return pl.pallas_call(
        kernel,
        grid=(n_seq,),
        in_specs=([pl.BlockSpec((seq, D_MODEL), lambda b: (b, 0)),
                   pl.BlockSpec((None, None, SUBLANES, D_MODEL), lambda b: (l, 1 + b, 0, 0)),
                   _resident((D_MODEL, IN_WIDTH)), _resident((D_MODEL, D_MODEL))]
                  + [_full(a.shape) if a.ndim == 2 else _layer_block(a, l) for a in small]
                  + [_full((seq, LANES)), _full((seq, LANES)), cache_spec, cache_spec]),
        out_specs=pl.BlockSpec((seq, D_MODEL), lambda b: (b, 0)),
        out_shape=jax.ShapeDtypeStruct((n_seq * seq, D_MODEL), F32),
        scratch_shapes=_mixer_scratch(seq, n_cache),
        compiler_params=pltpu.CompilerParams(dimension_semantics=("arbitrary",), vmem_limit_bytes=VMEM_LIMIT),
        name="mixer_lat",
    )(x, mod, w_in_bf, w_out_bf, *small, cos, sin, kc, vc)


def _two_stream_specs(block_rows, n_a, block_of):
    spec_a = pl.BlockSpec((block_rows, D_MODEL), lambda *ids: (jnp.minimum(block_of(*ids), n_a - 1), 0))
    spec_b = pl.BlockSpec((block_rows, D_MODEL), lambda *ids: (jnp.maximum(block_of(*ids) - n_a, 0), 0))
    return [spec_a, spec_b]


def _group_of_block(blk, block_rows, n_a, lat_seq):
    return jnp.where(blk < n_a, 0, 1 + (blk - n_a) // (lat_seq // block_rows))


def _ffn_kernel(xa_ref, xb_ref, mod_ref, w1_ref, w3_ref, w2_ref, g_ref, b_ref, oa_ref, ob_ref, *, n_a):
    is_a = pl.program_id(0) < n_a
    x = jnp.where(is_a, xa_ref[...], xb_ref[...])
    h = (_standardize(x) * (1.0 + mod_ref[4:5, :]) + mod_ref[3:4, :]).astype(BF16)
    acc = jnp.zeros(x.shape, F32)
    for c in range(D_FF // FFN_COLS):
        cols = slice(c * FFN_COLS, (c + 1) * FFN_COLS)
        a = _dot(h, w1_ref[:, cols])
        b = _dot(h, w3_ref[:, cols])
        acc = acc + _dot((_silu(a) * b).astype(BF16), w2_ref[cols, :])
    y = DEEPNORM_ALPHA * x + mod_ref[5:6, :] * acc
    out = _standardize(y) * g_ref[...] + b_ref[...]

    @pl.when(is_a)
    def _():
        oa_ref[...] = out

    @pl.when(jnp.logical_not(is_a))
    def _():
        ob_ref[...] = out


def _ffn_dense(xa, xb, l, mods, w1, w3, w2, g, b, lat_seq):
    n_a, n_b = xa.shape[0] // FFN_TILE, xb.shape[0] // FFN_TILE
    x_specs = _two_stream_specs(FFN_TILE, n_a, lambda t: t)
    return pl.pallas_call(
        functools.partial(_ffn_kernel, n_a=n_a),
        grid=(n_a + n_b,),
        in_specs=x_specs + [
            pl.BlockSpec((None, None, SUBLANES, D_MODEL),
                         lambda t: (l, _group_of_block(t, FFN_TILE, n_a, lat_seq), 0, 0)),
            _resident(w1.shape), _resident(w3.shape), _resident(w2.shape), _layer_block(g, l), _layer_block(b, l)],
        out_specs=x_specs,
        out_shape=[jax.ShapeDtypeStruct(xa.shape, F32), jax.ShapeDtypeStruct(xb.shape, F32)],
        compiler_params=pltpu.CompilerParams(dimension_semantics=("arbitrary",), vmem_limit_bytes=VMEM_LIMIT),
        name="ffn_dense",
    )(xa, xb, mods, w1, w3, w2, g, b)


def _router_kernel(xa_ref, xb_ref, mod_ref, rw_ref, before_ref, dest_ref, gate_ref, start_ref, padded_ref, *,
                   n_a, lat_seq):
    c = pl.program_id(0)
    w_hi, w_lo = _split(rw_ref[...])
    n_blocks = MOE_CHUNK // ROUTER_BLOCK
    parts = []
    for blk in range(n_blocks):
        rows = pl.ds(blk * ROUTER_BLOCK, ROUTER_BLOCK)
        mod = mod_ref[_group_of_block(c * n_blocks + blk, ROUTER_BLOCK, n_a * n_blocks, lat_seq)]
        x = jnp.where(c < n_a, xa_ref[rows, :], xb_ref[rows, :])
        h = _standardize(x) * (1.0 + mod[4:5, :]) + mod[3:4, :]
        h_hi, h_lo = _split(h)
        parts.append(_dot_nt(w_hi, h_hi) + _dot_nt(w_hi, h_lo) + _dot_nt(w_lo, h_hi))
    logits = jnp.concatenate(parts, axis=1)
    eid = lax.broadcasted_iota(I32, logits.shape, 0).astype(F32)
    m1 = jnp.max(logits, axis=0, keepdims=True)
    i1 = jnp.min(jnp.where(logits == m1, eid, float(N_EXPERTS)), axis=0, keepdims=True)
    oh1 = eid == i1
    rest = jnp.where(oh1, -jnp.inf, logits)
    m2 = jnp.max(rest, axis=0, keepdims=True)
    i2 = jnp.min(jnp.where(rest == m2, eid, float(N_EXPERTS)), axis=0, keepdims=True)
    oh2 = eid == i2
    e = jnp.exp(m2 - m1)
    gate_ref[0:1, :] = 1.0 / (1.0 + e)
    gate_ref[1:2, :] = e / (1.0 + e)
    sel = jnp.where(oh1 | oh2, 1.0, 0.0)
    ranks = []
    seen = jnp.zeros((N_EXPERTS, 1), F32)
    for blk in range(n_blocks):
        s_blk = sel[:, blk * ROUTER_BLOCK:(blk + 1) * ROUTER_BLOCK]
        ranks.append(_dot(s_blk.astype(BF16), before_ref[...]) + seen)
        seen = seen + jnp.sum(s_blk, axis=1, keepdims=True)
    rank = jnp.concatenate(ranks, axis=1)
    eid_out = lax.broadcasted_iota(I32, start_ref.shape, 0).astype(F32)
    start = jnp.zeros(sel.shape, F32)
    start_out = jnp.zeros(start_ref.shape, F32)
    padded_out = jnp.zeros(start_ref.shape, F32)
    for ex in range(N_EXPERTS):
        cnt = jnp.sum(sel[ex:ex + 1, :], axis=1, keepdims=True)
        padded = jnp.ceil(cnt * (1.0 / MOE_GRAN)) * MOE_GRAN
        start = start + jnp.where(eid > ex, padded, 0.0)
        start_out = start_out + jnp.where(eid_out > ex, padded, 0.0)
        padded_out = padded_out + jnp.where(eid_out == ex, padded, 0.0)
    row = (start + rank) * SUBLANES
    dest_ref[0:1, :] = jnp.sum(jnp.where(oh1, row, 0.0), axis=0, keepdims=True).astype(I32)
    dest_ref[1:2, :] = jnp.sum(jnp.where(oh2, row, 0.0), axis=0, keepdims=True).astype(I32)
    start_ref[...] = start_out.astype(I32)
    padded_ref[...] = padded_out.astype(I32)


def _router(xa, xb, l, mods, rw_t, lat_seq):
    n_a = xa.shape[0] // MOE_CHUNK
    n_chunks = n_a + xb.shape[0] // MOE_CHUNK
    tok = np.arange(ROUTER_BLOCK)
    before = jnp.asarray(tok[:, None] < tok[None, :], BF16)
    return pl.pallas_call(
        functools.partial(_router_kernel, n_a=n_a, lat_seq=lat_seq),
        grid=(n_chunks,),
        in_specs=_two_stream_specs(MOE_CHUNK, n_a, lambda c: c) + [
                  _layer_block(mods, l),
                  _full((N_EXPERTS, D_MODEL)), _full((ROUTER_BLOCK, ROUTER_BLOCK))],
        out_specs=[pl.BlockSpec((None, 2, MOE_CHUNK), lambda c: (c, 0, 0)),
                   pl.BlockSpec((None, 2, MOE_CHUNK), lambda c: (c, 0, 0)),
                   pl.BlockSpec((None, N_EXPERTS, LANES), lambda c: (c, 0, 0)),
                   pl.BlockSpec((None, N_EXPERTS, LANES), lambda c: (c, 0, 0))],
        out_shape=[jax.ShapeDtypeStruct((n_chunks, 2, MOE_CHUNK), I32),
                   jax.ShapeDtypeStruct((n_chunks, 2, MOE_CHUNK), F32),
                   jax.ShapeDtypeStruct((n_chunks, N_EXPERTS, LANES), I32),
                   jax.ShapeDtypeStruct((n_chunks, N_EXPERTS, LANES), I32)],
        compiler_params=pltpu.CompilerParams(dimension_semantics=("arbitrary",), vmem_limit_bytes=VMEM_LIMIT),
        name="moe_router",
    )(xa, xb, mods, rw_t, before)


def _token_rows(first_row):
    return pl.ds(pl.multiple_of(first_row, SUBLANES), SUBLANES)


def _store_token_major(ref, tok0, val):
    for cc in range(D_MODEL // LANES):
        ref[pl.ds(tok0 * SUBLANES + cc, val.shape[0], stride=SUBLANES), :] = val[:, cc * LANES:(cc + 1) * LANES]


def _load_token_major(ref, tok0, n):
    return jnp.concatenate(
        [ref[pl.ds(tok0 * SUBLANES + cc, n, stride=SUBLANES), :] for cc in range(D_MODEL // LANES)], axis=1)


def _moe_kernel(start_ref, padded_ref, xa_ref, xb_ref, mod_ref, dest_ref, gate_ref, w13_ref, w2_ref,
                g_ref, b_ref, oa_ref, ob_ref, tok_scr, rows_scr, *, n_a):
    c = pl.program_id(0)
    j = pl.program_id(1)
    is_a = c < n_a
    first_expert_step = MOE_TOK_STEPS
    first_combine_step = MOE_TOK_STEPS + N_EXPERTS

    @pl.when((c == 0) & (j == 0))
    def _init():
        rows_scr[...] = jnp.zeros(rows_scr.shape, F32)

    @pl.when(j < first_expert_step)
    def _dispatch():
        for r in range(MOE_TOK_BLOCK // ROW_BLOCK):
            rows = pl.ds(r * ROW_BLOCK, ROW_BLOCK)
            x = jnp.where(is_a, xa_ref[rows, :], xb_ref[rows, :])
            h = _standardize(x) * (1.0 + mod_ref[4:5, :]) + mod_ref[3:4, :]
            _store_token_major(tok_scr, r * ROW_BLOCK, h)
        t0 = j * MOE_TOK_BLOCK

        def body(t, carry):
            row = tok_scr[_token_rows(t * SUBLANES), :]
            rows_scr[_token_rows(dest_ref[t0 + t]), :] = row
            rows_scr[_token_rows(dest_ref[MOE_CHUNK + t0 + t]), :] = row
            return carry
        lax.fori_loop(0, MOE_TOK_BLOCK, body, 0, unroll=8)

    def experts(row0, m):
        xin = _load_token_major(rows_scr, row0, m).astype(BF16)
        ab = _dot(xin, w13_ref[...])
        a, b = ab[:, :D_FF_EXPERT], ab[:, D_FF_EXPERT:]
        y = _dot((_silu(a) * b).astype(BF16), w2_ref[...])
        _store_token_major(rows_scr, row0, y)

    @pl.when((j >= first_expert_step) & (j < first_combine_step))
    def _experts():
        region = c * N_EXPERTS + (j - first_expert_step)
        start = start_ref[region]
        padded = padded_ref[region]
        n_full = lax.shift_right_logical(padded, MOE_BLOCK.bit_length() - 1)

        def body(i, carry):
            experts(start + i * MOE_BLOCK, MOE_BLOCK)
            return carry
        lax.fori_loop(0, n_full, body, 0)
        for m in range(MOE_GRAN, MOE_BLOCK, MOE_GRAN):
            pl.when(padded - n_full * MOE_BLOCK == m)(
                functools.partial(experts, start + n_full * MOE_BLOCK, m))

    @pl.when(j >= first_combine_step)
    def _combine():
        t0 = (j - first_combine_step) * MOE_TOK_BLOCK

        def body(t, carry):
            y0 = rows_scr[_token_rows(dest_ref[t0 + t]), :]
            y1 = rows_scr[_token_rows(dest_ref[MOE_CHUNK + t0 + t]), :]
            tok_scr[_token_rows(t * SUBLANES), :] = gate_ref[t0 + t] * y0 + gate_ref[MOE_CHUNK + t0 + t] * y1
            return carry
        lax.fori_loop(0, MOE_TOK_BLOCK, body, 0, unroll=8)
        for r in range(MOE_TOK_BLOCK // ROW_BLOCK):
            rows = pl.ds(r * ROW_BLOCK, ROW_BLOCK)
            ffn = _load_token_major(tok_scr, r * ROW_BLOCK, ROW_BLOCK)
            x = jnp.where(is_a, xa_ref[rows, :], xb_ref[rows, :])
            y = DEEPNORM_ALPHA * x + mod_ref[5:6, :] * ffn
            out = _standardize(y) * g_ref[...] + b_ref[...]

            @pl.when(is_a)
            def _():
                oa_ref[rows, :] = out

            @pl.when(jnp.logical_not(is_a))
            def _():
                ob_ref[rows, :] = out


def _ffn_moe(xa, xb, l, mods, rw_t, w13, w2, g, b, lat_seq):
    n_a = xa.shape[0] // MOE_CHUNK
    n_chunks = n_a + xb.shape[0] // MOE_CHUNK
    n_a_blocks = n_a * MOE_TOK_STEPS
    dest, gates, start, padded = _router(xa, xb, l, mods, rw_t, lat_seq)
    first_expert_step = MOE_TOK_STEPS
    first_combine_step = MOE_TOK_STEPS + N_EXPERTS

    def token_block(c, j, *_):
        blk = jnp.where(j < first_combine_step, jnp.minimum(j, MOE_TOK_STEPS - 1), j - first_combine_step)
        return c * MOE_TOK_STEPS + blk

    def out_token_block(c, j, *_):
        return c * MOE_TOK_STEPS + jnp.maximum(j - first_combine_step, 0)

    def mod_of(c, j, *_):
        return (l, _group_of_block(token_block(c, j), MOE_TOK_BLOCK, n_a_blocks, lat_seq), 0, 0)

    def expert_of(c, j, *_):
        return (jnp.clip(j - first_expert_step, 0, N_EXPERTS - 1), 0, 0)

    grid_spec = pltpu.PrefetchScalarGridSpec(
        num_scalar_prefetch=2,
        grid=(n_chunks, MOE_STEPS),
        in_specs=_two_stream_specs(MOE_TOK_BLOCK, n_a_blocks, token_block) + [
                  pl.BlockSpec((None, None, SUBLANES, D_MODEL), mod_of),
                  pl.BlockSpec((2 * MOE_CHUNK,), lambda c, j, *_: (c,), memory_space=pltpu.SMEM),
                  pl.BlockSpec((2 * MOE_CHUNK,), lambda c, j, *_: (c,), memory_space=pltpu.SMEM),
                  pl.BlockSpec((None, D_MODEL, 2 * D_FF_EXPERT), expert_of),
                  pl.BlockSpec((None, D_FF_EXPERT, D_MODEL), expert_of),
                  _layer_block(g, l), _layer_block(b, l)],
        out_specs=_two_stream_specs(MOE_TOK_BLOCK, n_a_blocks, out_token_block),
        scratch_shapes=[pltpu.VMEM((MOE_TOK_BLOCK * SUBLANES, LANES), F32),
                        pltpu.VMEM((MOE_ROWS * SUBLANES, LANES), F32)],
    )
    return pl.pallas_call(
        functools.partial(_moe_kernel, n_a=n_a),
        grid_spec=grid_spec,
        out_shape=[jax.ShapeDtypeStruct(xa.shape, F32), jax.ShapeDtypeStruct(xb.shape, F32)],
        compiler_params=pltpu.CompilerParams(
            dimension_semantics=("arbitrary", "arbitrary"), vmem_limit_bytes=VMEM_LIMIT),
        name="moe_experts",
    )(start[:, :, 0].reshape(-1), padded[:, :, 0].reshape(-1), xa, xb, mods,
      dest.reshape(-1), gates.reshape(-1), w13, w2, g, b)


def _rope_tables(n_tokens):
    t = np.arange(n_tokens)
    row = (t // GRID_W).astype(np.float32)
    col = (t % GRID_W).astype(np.float32)
    inv_freq = (np.float32(ROPE_THETA) ** (-np.arange(0, AXIS_ROT, 2, dtype=np.float32) / AXIS_ROT)).astype(np.float32)
    ang_r = row[:, None] * inv_freq
    ang_c = col[:, None] * inv_freq
    cos = np.concatenate([np.cos(ang_r), np.cos(ang_r), np.cos(ang_c), np.cos(ang_c)], axis=1)
    sin = np.concatenate([-np.sin(ang_r), np.sin(ang_r), -np.sin(ang_c), np.sin(ang_c)], axis=1)
    return jnp.asarray(np.tile(cos, (1, 2)), F32), jnp.asarray(np.tile(sin, (1, 2)), F32)


def kernel(x_prompt, x_sample, cache_k, cache_v, c, c_ctx, ada_w, ada_b, w_in, q_norm_g, k_norm_g, conv_w, conv_b, sgu_norm_g, sgu_w, sgu_b, w_out, ln1_g, ln1_b, ln2_g, ln2_b, ffn_w1, ffn_w3, ffn_w2, router_w, moe_w1, moe_w3, moe_w2):
    batch, seq, _ = x_prompt.shape
    dec_batch, dec_seq, _ = x_sample.shape
    past_len = cache_k.shape[2]
    n_ctx = batch * seq
    n_lat = dec_batch * dec_seq
    assert DEPTH == 2 and 1 + dec_batch <= SUBLANES
    assert seq == ATT_BLOCK and CTX_SEQS_PER_STEP * seq == ROW_BLOCK and batch % CTX_SEQS_PER_STEP == 0
    assert dec_seq % ROW_BLOCK == 0 and dec_seq & (dec_seq - 1) == 0
    assert n_ctx % MOE_CHUNK == 0 and n_lat % MOE_CHUNK == 0
    assert dec_seq % ROUTER_BLOCK == 0 and dec_seq % MOE_TOK_BLOCK == 0 and dec_seq % FFN_TILE == 0

    cond = jnp.zeros((SUBLANES, D_MODEL), F32).at[0].set(c_ctx).at[1:1 + dec_batch].set(c)
    mod = _modulation(cond, ada_w, ada_b)
    mod = mod.reshape(DEPTH, SUBLANES, 6, D_MODEL)[:, :1 + dec_batch]
    mod = jnp.pad(mod, ((0, 0), (0, 0), (0, SUBLANES - 6), (0, 0)))

    lane_id = np.arange(GROUP_TILE) // HEAD_DIM
    ones_bd = jnp.asarray(lane_id[:, None] == lane_id[None, :], BF16)
    cos, sin = _rope_tables(dec_seq)
    small = (
        jnp.tile(q_norm_g, (1, N_Q_HEADS))[:, None, :], jnp.tile(k_norm_g, (1, N_KV_HEADS))[:, None, :],
        ones_bd,
        conv_w, conv_b[:, None, :], sgu_norm_g[:, None, :],
        jnp.swapaxes(sgu_w, 1, 2).reshape(DEPTH, CHUNK, SGU_HEADS * CHUNK).astype(BF16),
        jnp.repeat(jnp.swapaxes(sgu_b, 1, 2), HEAD_DIM, axis=2),
        ln1_g[:, None, :], ln1_b[:, None, :],
    )
    g2, b2 = ln2_g[:, None, :], ln2_b[:, None, :]
    kc = cache_k.reshape(dec_batch, DEPTH, past_len, KV_WIDTH)
    vc = cache_v.reshape(dec_batch, DEPTH, past_len, KV_WIDTH)

    xs = [x_prompt.reshape(n_ctx, D_MODEL), x_sample.reshape(n_lat, D_MODEL)]
    assert DEPTH == 2 and ffn_w1.shape[0] == 1 and moe_w1.shape[0] == 1
    side_casts = [[(ffn_w1[0],), (ffn_w3[0],), (ffn_w2[0],), ((w_in, 1),), ((w_out, 1),), (moe_w2.reshape(-1, D_MODEL),)],
                  [(moe_w1.reshape(-1, D_FF_EXPERT), moe_w3.reshape(-1, D_FF_EXPERT))]]
    kv = None
    for l in range(DEPTH):
        if l == 0:
            x_ctx, k_ctx, v_ctx, w_in_bf, w_out_bf, *ffn_bf, w_in_next, w_out_next, w2_bf = _mixer_ctx(
                xs[0], l, batch, seq, mod, w_in, w_out, small, kv, side_casts[l])
        else:
            w_in_bf, w_out_bf = w_in_next, w_out_next
            x_ctx, k_ctx, v_ctx, w13_bf = _mixer_ctx(
                xs[0], l, batch, seq, mod, w_in_bf, w_out_bf, small, kv, side_casts[l])
        kv = (k_ctx, v_ctx)
        x_lat = _mixer_lat(xs[1], l, dec_batch, dec_seq, mod, w_in_bf, w_out_bf, small, cos, sin, kc, vc)
        if l % 2 == 0:
            xs = _ffn_dense(x_ctx, x_lat, l, mod, *ffn_bf, g2, b2, dec_seq)
        else:
            ws = (w13_bf.reshape(N_EXPERTS, D_MODEL, 2 * D_FF_EXPERT), w2_bf.reshape(N_EXPERTS, D_FF_EXPERT, D_MODEL))
            xs = _ffn_moe(x_ctx, x_lat, l, mod, router_w[l // 2].T, *ws, g2, b2, dec_seq)
    y_p = xs[0].reshape(batch, seq, D_MODEL)
    y_s = xs[1].reshape(dec_batch, dec_seq, D_MODEL)
    new_k = kv[0].reshape(batch, DEPTH, seq, N_KV_HEADS, HEAD_DIM)
    new_v = kv[1].reshape(batch, DEPTH, seq, N_KV_HEADS, HEAD_DIM)
    return (y_p, y_s, new_k, new_v)
```

```python
import functools

import numpy as np
import jax
import jax.numpy as jnp
from jax import lax
from jax.experimental import pallas as pl
from jax.experimental.pallas import tpu as pltpu

F32 = jnp.float32
BF16 = jnp.bfloat16
I32 = jnp.int32

D_MODEL = 1024
DEPTH = 2
GRID_W = 64
HEAD_DIM = 64
N_Q_HEADS = 8
N_KV_HEADS = 2
ATTN_WIDTH = N_Q_HEADS * HEAD_DIM
KV_WIDTH = N_KV_HEADS * HEAD_DIM
ATTN_SCALE = HEAD_DIM ** -0.5
ROPE_THETA = 10000.0
AXIS_ROT = HEAD_DIM // 2
CONV_WIDTH = 256
SGU_WIDTH = 256
SGU_HEADS = 4
CHUNK = 128
IN_WIDTH = 2048
D_FF = 2816
N_EXPERTS = 8
D_FF_EXPERT = 1408
EPS = 1e-6
DEEPNORM_ALPHA = (2 * DEPTH) ** 0.25

LANES = 128
SUBLANES = 8
ROW_BLOCK = 512
GROUP_TILE = 256
ATT_BLOCK = 256
CTX_SEQS_PER_STEP = 2
FFN_TILE = 512
FFN_COLS = 256
MOE_CHUNK = 2048
ROUTER_BLOCK = 1024
MOE_GRAN = 128
MOE_BLOCK = 256
MOE_TOK_BLOCK = 512
MOE_TOK_STEPS = MOE_CHUNK // MOE_TOK_BLOCK
MOE_STEPS = 2 * MOE_TOK_STEPS + N_EXPERTS
MOE_ROWS = 2 * MOE_CHUNK + N_EXPERTS * MOE_GRAN
VMEM_LIMIT = 60 * 1024 * 1024

_Q0, _K0, _V0, _CI0, _CB0, _CC0, _SU0, _SV0 = 0, 512, 640, 768, 1024, 1280, 1536, 1792


def _dot(a, b):
    return jnp.dot(a, b, preferred_element_type=F32)


def _dot_nt(a, b):
    return lax.dot_general(a, b, (((1,), (1,)), ((), ())), preferred_element_type=F32)


def _split(x):
    hi = x.astype(BF16)
    lo = (x - hi.astype(F32)).astype(BF16)
    return hi, lo


def _group_sum(x, ones_bd):
    outs = []
    for c0 in range(0, x.shape[1], GROUP_TILE):
        width = min(GROUP_TILE, x.shape[1] - c0)
        outs.append(_dot(x[:, c0:c0 + width].astype(BF16), ones_bd[:width, :width]))
    return outs[0] if len(outs) == 1 else jnp.concatenate(outs, axis=1)


def _standardize(x):
    mu = jnp.mean(x, axis=-1, keepdims=True)
    d = x - mu
    return d * lax.rsqrt(jnp.mean(d * d, axis=-1, keepdims=True) + EPS)


def _silu(x):
    return x / (1.0 + jnp.exp(-x))


def _modulation_kernel(cond_ref, w_ref, b_ref, o_ref):
    s_hi, s_lo = _split(_silu(cond_ref[...]))
    w_hi, w_lo = _split(w_ref[...])
    o_ref[...] = _dot(s_hi, w_hi) + _dot(s_hi, w_lo) + _dot(s_lo, w_hi) + b_ref[...]


def _modulation(cond, ada_w, ada_b):
    n_out = ada_w.shape[-1]
    tn = 1536
    return pl.pallas_call(
        _modulation_kernel,
        grid=(DEPTH, n_out // tn),
        in_specs=[
            pl.BlockSpec((SUBLANES, D_MODEL), lambda l, j: (0, 0)),
            pl.BlockSpec((None, D_MODEL, tn), lambda l, j: (l, 0, j)),
            pl.BlockSpec((None, 1, tn), lambda l, j: (l, 0, j)),
        ],
        out_specs=pl.BlockSpec((None, SUBLANES, tn), lambda l, j: (l, 0, j)),
        out_shape=jax.ShapeDtypeStruct((DEPTH, SUBLANES, n_out), F32),
        compiler_params=pltpu.CompilerParams(
            dimension_semantics=("arbitrary", "arbitrary"), vmem_limit_bytes=VMEM_LIMIT),
        name="modulation",
    )(cond, ada_w, ada_b.reshape(DEPTH, 1, n_out))


def _rope(x, cos, sin_signed):
    w = x.shape[1]
    lane = lax.broadcasted_iota(I32, x.shape, 1)
    first_half = (lane & 31) < 16
    partner = jnp.where(first_half, pltpu.roll(x, w - 16, 1), pltpu.roll(x, 16, 1))
    return x * cos + partner * sin_signed


def _head_variants(x):
    lane = lax.broadcasted_iota(I32, x.shape, 1)
    lo = lane < HEAD_DIM
    xr = pltpu.roll(x, HEAD_DIM, 1)
    zero = jnp.zeros_like(x)
    return (jnp.where(lo, x, zero).astype(BF16), jnp.where(lo, zero, xr).astype(BF16),
            jnp.where(lo, xr, zero).astype(BF16), jnp.where(lo, zero, x).astype(BF16))


def _mixer_kernel(*refs, seq, n_seq, n_cache, rope, cast_weights, stack_kv, side_jobs=(), n_steps=None):
    refs = list(refs)
    skew = n_steps is not None

    def take(n):
        out, refs[:] = refs[:n], refs[n:]
        return out

    (x_ref,) = take(1)
    xres_ref = take(1)[0] if skew else x_ref
    mod_ref, win_ref, wout_ref = take(3)
    qg_ref, kg_ref, ones_ref, convw_ref, convb_ref, sgug_ref, sguw_ref, sgub_ref, ln1g_ref, ln1b_ref = take(10)
    if rope:
        cos_ref, sin_ref, kc_ref, vc_ref = take(4)
    if stack_kv:
        kprev_ref, vprev_ref = take(2)
    side_in = [take(n_src) for n_src in side_jobs]
    (x1_ref,) = take(1)
    if not rope:
        k_ref, v_ref = take(2)
    if cast_weights:
        winb_ref, woutb_ref = take(2)

        @pl.when(pl.program_id(0) == 0)
        def _cast():
            winb_ref[...] = win_ref[...].astype(BF16)
            woutb_ref[...] = wout_ref[...].astype(BF16)
        win_ref, wout_ref = winb_ref, woutb_ref
    side_out = take(len(side_jobs))
    scratch = take(7)
    (mix_scr,) = take(1)
    n_rows = n_seq * seq
    assert n_cache == 0 or n_seq == 1

    def loop(n, body):
        if n == 1:
            body(0)
        else:
            def step(r, carry):
                body(r)
                return carry
            lax.fori_loop(0, n, step, 0, unroll=2)

    def block(r, size):
        if isinstance(r, int):
            return pl.ds(r * size, size)
        return pl.ds(pl.multiple_of(r * size, size), size)

    def project_phase(scr):
        q_scr, kvar_scr, vvar_scr, u_scr, cb_scr, su_scr, vn_scr = scr
        for srcs, dst in zip(side_in, side_out):
            col = 0
            for src in srcs:
                dst[:, col:col + src.shape[1]] = src[...].astype(BF16)
                col += src.shape[1]
        if n_cache:
            for i, var in enumerate(_head_variants(kc_ref[...])):
                kvar_scr[i, pl.ds(seq, n_cache), :] = var
            for i, var in enumerate(_head_variants(vc_ref[...])):
                vvar_scr[i, pl.ds(seq, n_cache), :] = var

        def project(r):
            rows = block(r, ROW_BLOCK)
            x = x_ref[rows, :]
            h = _standardize(x) * (1.0 + mod_ref[1:2, :]) + mod_ref[0:1, :]
            z = _dot(h.astype(BF16), win_ref[...])
            ones_bd = ones_ref[...]
            zq = z[:, _Q0:_K0]
            q = zq * lax.rsqrt(_group_sum(zq * zq, ones_bd) * (1.0 / HEAD_DIM) + EPS) * qg_ref[...]
            zk = z[:, _K0:_V0]
            k = zk * lax.rsqrt(_group_sum(zk * zk, ones_bd) * (1.0 / HEAD_DIM) + EPS) * kg_ref[...]
            v = z[:, _V0:_CI0]
            if rope:
                cos = cos_ref[rows, :]
                sin = sin_ref[rows, :]
                q = _rope(q, jnp.concatenate([cos] * 4, axis=1), jnp.concatenate([sin] * 4, axis=1))
                k = _rope(k, cos, sin)
            elif stack_kv:
                for s in range(ROW_BLOCK // seq):
                    sub = slice(s * seq, (s + 1) * seq)
                    k_ref[s, 0] = kprev_ref[sub, :]
                    v_ref[s, 0] = vprev_ref[sub, :]
                    k_ref[s, 1] = k[sub, :]
                    v_ref[s, 1] = v[sub, :]
            else:
                k_ref[rows, :] = k
                v_ref[rows, :] = v
            q_scr[rows, :] = (q * ATTN_SCALE).astype(BF16)
            for i, var in enumerate(_head_variants(k)):
                kvar_scr[i, rows, :] = var
            for i, var in enumerate(_head_variants(v)):
                vvar_scr[i, rows, :] = var
            u_scr[rows, :] = z[:, _CC0:_SU0] * z[:, _CI0:_CB0]
            cb_scr[rows, :] = z[:, _CB0:_CC0]
            su_scr[rows, :] = z[:, _SU0:_SV0]
            sv = z[:, _SV0:IN_WIDTH]
            d = sv - _group_sum(sv, ones_bd) * (1.0 / HEAD_DIM)
            vn = d * lax.rsqrt(_group_sum(d * d, ones_bd) * (1.0 / HEAD_DIM) + EPS) * sgug_ref[...]
            vn_scr[rows, :] = vn.astype(BF16)

        loop(n_rows // ROW_BLOCK, project)

    def consume_phase(scr):
        q_scr, kvar_scr, vvar_scr, u_scr, cb_scr, su_scr, vn_scr = scr
        u = u_scr[...]
        pos = lax.broadcasted_iota(I32, u.shape, 0) & (seq - 1)
        up = jnp.where(pos == 0, 0.0, pltpu.roll(u, 1, 0))
        dn = jnp.where(pos == seq - 1, 0.0, pltpu.roll(u, n_rows - 1, 0))
        conv = up * convw_ref[0:1, :] + u * convw_ref[1:2, :] + dn * convw_ref[2:3, :] + convb_ref[...]
        mix_scr[:, ATTN_WIDTH:ATTN_WIDTH + CONV_WIDTH] = (cb_scr[...] * conv).astype(BF16)

        for n in range(n_rows // CHUNK):
            rows = pl.ds(n * CHUNK, CHUNK)
            vn = vn_scr[rows, :]
            lane = lax.broadcasted_iota(I32, vn.shape, 1)
            per_head = [jnp.where((lane >= hd * HEAD_DIM) & (lane < (hd + 1) * HEAD_DIM), vn, jnp.zeros_like(vn))
                        for hd in range(SGU_HEADS)]
            s = sgub_ref[...] + _dot(sguw_ref[...], jnp.concatenate(per_head, axis=0))
            mix_scr[rows, ATTN_WIDTH + CONV_WIDTH:] = (su_scr[rows, :] * s).astype(BF16)

        def attend(s, r):
            rows = block(s * (seq // ATT_BLOCK) + r, ATT_BLOCK)
            keys = pl.ds(s * seq, seq + n_cache)
            for pair in range(N_Q_HEADS // 2):
                qp = q_scr[rows, pair * LANES:(pair + 1) * LANES]
                kv = pair // (N_Q_HEADS // N_KV_HEADS // 2)
                acc = jnp.zeros((ATT_BLOCK, LANES), F32)
                for parity in range(2):
                    sc = _dot_nt(qp, kvar_scr[2 * kv + parity, keys, :])
                    p = jnp.exp(sc - jnp.max(sc, axis=1, keepdims=True))
                    denom = jnp.sum(p, axis=1, keepdims=True)
                    acc = acc + _dot(p.astype(BF16), vvar_scr[2 * kv + parity, keys, :]) / denom
                mix_scr[rows, pair * LANES:(pair + 1) * LANES] = acc.astype(BF16)

        for s in range(n_seq):
            loop(seq // ATT_BLOCK, functools.partial(attend, s))

        def finish(r):
            rows = block(r, ROW_BLOCK)
            mix = _dot(mix_scr[rows, :], wout_ref[...])
            y = DEEPNORM_ALPHA * xres_ref[rows, :] + mod_ref[2:3, :] * mix
            x1_ref[rows, :] = _standardize(y) * ln1g_ref[...] + ln1b_ref[...]

        loop(n_rows // ROW_BLOCK, finish)

    if not skew:
        project_phase(scratch)
        consume_phase(scratch)
        return

    i = pl.program_id(0)
    slot = lax.rem(i, 2)
    mine = [ref.at[slot] for ref in scratch]
    other = [ref.at[1 - slot] for ref in scratch]
    pl.when(i == 0)(functools.partial(project_phase, mine))

    @pl.when((i > 0) & (i < n_steps))
    def _():
        consume_phase(other)
        project_phase(mine)

    pl.when(i == n_steps)(functools.partial(consume_phase, other))


def _full(shape):
    n = len(shape)
    return pl.BlockSpec(shape, lambda *_: (0,) * n)


def _resident(shape):
    n = len(shape)
    return pl.BlockSpec(shape, lambda *_: (0,) * n, pipeline_mode=pl.Buffered(1))


def _layer_block(arr, l, resident=False):
    shape = arr.shape[1:]
    kw = dict(pipeline_mode=pl.Buffered(1)) if resident else {}
    return pl.BlockSpec((None,) + shape, lambda *_: (l,) + (0,) * len(shape), **kw)


def _mixer_scratch(n_rows, n_cache, slots=()):
    nk = n_rows + n_cache
    return [
        pltpu.VMEM(slots + (n_rows, ATTN_WIDTH), BF16),
        pltpu.VMEM(slots + (4, nk, LANES), BF16),
        pltpu.VMEM(slots + (4, nk, LANES), BF16),
        pltpu.VMEM(slots + (n_rows, CONV_WIDTH), F32),
        pltpu.VMEM(slots + (n_rows, CONV_WIDTH), F32),
        pltpu.VMEM(slots + (n_rows, SGU_WIDTH), F32),
        pltpu.VMEM(slots + (n_rows, SGU_WIDTH), BF16),
        pltpu.VMEM((n_rows, D_MODEL), BF16),
    ]


def _side_source(src):
    return src if isinstance(src, tuple) else (src, None)


def _mixer_ctx(x, l, n_seq, seq, mod, w_in, w_out, small, kv_prev=None, side_casts=()):
    per_step = CTX_SEQS_PER_STEP
    rows = per_step * seq
    stack_kv = kv_prev is not None
    cast_weights = w_in.dtype != BF16
    n_steps = n_seq // per_step
    kernel = functools.partial(_mixer_kernel, seq=seq, n_seq=per_step, n_cache=0, rope=False,
                               cast_weights=cast_weights, stack_kv=stack_kv,
                               side_jobs=tuple(len(job) for job in side_casts), n_steps=n_steps)
    cur = lambda i: (jnp.minimum(i, n_steps - 1), 0)
    prev = lambda i: (jnp.maximum(i - 1, 0), 0)
    w_specs = ([_layer_block(w_in, l, resident=True), _layer_block(w_out, l, resident=True)] if cast_weights
               else [_resident(w_in.shape), _resident(w_out.shape)])
    in_specs = ([pl.BlockSpec((rows, D_MODEL), cur), pl.BlockSpec((rows, D_MODEL), prev),
                 pl.BlockSpec((None, None, SUBLANES, D_MODEL), lambda i: (l, 0, 0, 0))] + w_specs
                + [_full(a.shape) if a.ndim == 2 else _layer_block(a, l) for a in small])
    args = [x, x, mod, w_in, w_out, *small]
    if stack_kv:
        in_specs += [pl.BlockSpec((rows, KV_WIDTH), cur)] * 2
        args += list(kv_prev)
        kv_spec = pl.BlockSpec((per_step, DEPTH, seq, KV_WIDTH), lambda i: cur(i) + (0, 0))
        kv_shape = jax.ShapeDtypeStruct((n_seq, DEPTH, seq, KV_WIDTH), F32)
    else:
        kv_spec = pl.BlockSpec((rows, KV_WIDTH), cur)
        kv_shape = jax.ShapeDtypeStruct((n_seq * seq, KV_WIDTH), F32)
    side_out_specs, side_out_shapes = [], []
    for job in side_casts:
        sources = [_side_source(src) for src in job]
        n_rows = sources[0][0].shape[-2]
        n_cols = sum(arr.shape[-1] for arr, _ in sources)
        for arr, layer in sources:
            blk = (n_rows // n_steps, arr.shape[-1])
            if layer is None:
                in_specs.append(pl.BlockSpec(blk, cur))
            else:
                in_specs.append(pl.BlockSpec((None,) + blk, lambda i, layer=layer: (layer,) + cur(i)))
            args.append(arr)
        side_out_specs.append(pl.BlockSpec((n_rows // n_steps, n_cols), cur))
        side_out_shapes.append(jax.ShapeDtypeStruct((n_rows, n_cols), BF16))
    w_out_specs, w_out_shapes = [], []
    if cast_weights:
        w_out_specs = [_full((D_MODEL, IN_WIDTH)), _full((D_MODEL, D_MODEL))]
        w_out_shapes = [jax.ShapeDtypeStruct((D_MODEL, IN_WIDTH), BF16), jax.ShapeDtypeStruct((D_MODEL, D_MODEL), BF16)]
    return pl.pallas_call(
        kernel,
        grid=(n_steps + 1,),
        in_specs=in_specs,
        out_specs=[pl.BlockSpec((rows, D_MODEL), prev), kv_spec, kv_spec] + w_out_specs + side_out_specs,
        out_shape=[jax.ShapeDtypeStruct((n_seq * seq, D_MODEL), F32), kv_shape, kv_shape] + w_out_shapes
        + side_out_shapes,
        scratch_shapes=_mixer_scratch(rows, 0, slots=(2,)),
        compiler_params=pltpu.CompilerParams(dimension_semantics=("arbitrary",), vmem_limit_bytes=VMEM_LIMIT),
        name="mixer_ctx",
    )(*args)


def _mixer_lat(x, l, n_seq, seq, mod, w_in_bf, w_out_bf, small, cos, sin, kc, vc):
    n_cache = kc.shape[2]
    kernel = functools.partial(_mixer_kernel, seq=seq, n_seq=1, n_cache=n_cache, rope=True,
                               cast_weights=False, stack_kv=False)
    cache_spec = pl.BlockSpec((None, None, n_cache, KV_WIDTH), lambda b: (b, l, 0, 0))
    return pl.pallas_call(
        kernel,
        grid=(n_seq,),
        in_specs=([pl.BlockSpec((seq, D_MODEL), lambda b: (b, 0)),
                   pl.BlockSpec((None, None, SUBLANES, D_MODEL), lambda b: (l, 1 + b, 0, 0)),
                   _resident((D_MODEL, IN_WIDTH)), _resident((D_MODEL, D_MODEL))]
                  + [_full(a.shape) if a.ndim == 2 else _layer_block(a, l) for a in small]
                  + [_full((seq, LANES)), _full((seq, LANES)), cache_spec, cache_spec]),
        out_specs=pl.BlockSpec((seq, D_MODEL), lambda b: (b, 0)),
        out_shape=jax.ShapeDtypeStruct((n_seq * seq, D_MODEL), F32),
        scratch_shapes=_mixer_scratch(seq, n_cache),
        compiler_params=pltpu.CompilerParams(dimension_semantics=("arbitrary",), vmem_limit_bytes=VMEM_LIMIT),
        name="mixer_lat",
    )(x, mod, w_in_bf, w_out_bf, *small, cos, sin, kc, vc)


def _two_stream_specs(block_rows, n_a, block_of):
    spec_a = pl.BlockSpec((block_rows, D_MODEL), lambda *ids: (jnp.minimum(block_of(*ids), n_a - 1), 0))
    spec_b = pl.BlockSpec((block_rows, D_MODEL), lambda *ids: (jnp.maximum(block_of(*ids) - n_a, 0), 0))
    return [spec_a, spec_b]


def _group_of_block(blk, block_rows, n_a, lat_seq):
    return jnp.where(blk < n_a, 0, 1 + (blk - n_a) // (lat_seq // block_rows))


def _ffn_kernel(xa_ref, xb_ref, mod_ref, w1_ref, w3_ref, w2_ref, g_ref, b_ref, oa_ref, ob_ref, *, n_a):
    is_a = pl.program_id(0) < n_a
    x = jnp.where(is_a, xa_ref[...], xb_ref[...])
    h = (_standardize(x) * (1.0 + mod_ref[4:5, :]) + mod_ref[3:4, :]).astype(BF16)
    acc = jnp.zeros(x.shape, F32)
    for c in range(D_FF // FFN_COLS):
        cols = slice(c * FFN_COLS, (c + 1) * FFN_COLS)
        a = _dot(h, w1_ref[:, cols])
        b = _dot(h, w3_ref[:, cols])
        acc = acc + _dot((_silu(a) * b).astype(BF16), w2_ref[cols, :])
    y = DEEPNORM_ALPHA * x + mod_ref[5:6, :] * acc
    out = _standardize(y) * g_ref[...] + b_ref[...]

    @pl.when(is_a)
    def _():
        oa_ref[...] = out

    @pl.when(jnp.logical_not(is_a))
    def _():
        ob_ref[...] = out


def _ffn_dense(xa, xb, l, mods, w1, w3, w2, g, b, lat_seq):
    n_a, n_b = xa.shape[0] // FFN_TILE, xb.shape[0] // FFN_TILE
    x_specs = _two_stream_specs(FFN_TILE, n_a, lambda t: t)
    return pl.pallas_call(
        functools.partial(_ffn_kernel, n_a=n_a),
        grid=(n_a + n_b,),
        in_specs=x_specs + [
            pl.BlockSpec((None, None, SUBLANES, D_MODEL),
                         lambda t: (l, _group_of_block(t, FFN_TILE, n_a, lat_seq), 0, 0)),
            _resident(w1.shape), _resident(w3.shape), _resident(w2.shape), _layer_block(g, l), _layer_block(b, l)],
        out_specs=x_specs,
        out_shape=[jax.ShapeDtypeStruct(xa.shape, F32), jax.ShapeDtypeStruct(xb.shape, F32)],
        compiler_params=pltpu.CompilerParams(dimension_semantics=("arbitrary",), vmem_limit_bytes=VMEM_LIMIT),
        name="ffn_dense",
    )(xa, xb, mods, w1, w3, w2, g, b)


def _router_kernel(xa_ref, xb_ref, mod_ref, rw_ref, before_ref, dest_ref, gate_ref, start_ref, padded_ref, *,
                   n_a, lat_seq):
    c = pl.program_id(0)
    w_hi, w_lo = _split(rw_ref[...])
    n_blocks = MOE_CHUNK // ROUTER_BLOCK
    parts = []
    for blk in range(n_blocks):
        rows = pl.ds(blk * ROUTER_BLOCK, ROUTER_BLOCK)
        mod = mod_ref[_group_of_block(c * n_blocks + blk, ROUTER_BLOCK, n_a * n_blocks, lat_seq)]
        x = jnp.where(c < n_a, xa_ref[rows, :], xb_ref[rows, :])
        h = _standardize(x) * (1.0 + mod[4:5, :]) + mod[3:4, :]
        h_hi, h_lo = _split(h)
        parts.append(_dot_nt(w_hi, h_hi) + _dot_nt(w_hi, h_lo) + _dot_nt(w_lo, h_hi))
    logits = jnp.concatenate(parts, axis=1)
    eid = lax.broadcasted_iota(I32, logits.shape, 0).astype(F32)
    m1 = jnp.max(logits, axis=0, keepdims=True)
    i1 = jnp.min(jnp.where(logits == m1, eid, float(N_EXPERTS)), axis=0, keepdims=True)
    oh1 = eid == i1
    rest = jnp.where(oh1, -jnp.inf, logits)
    m2 = jnp.max(rest, axis=0, keepdims=True)
    i2 = jnp.min(jnp.where(rest == m2, eid, float(N_EXPERTS)), axis=0, keepdims=True)
    oh2 = eid == i2
    e = jnp.exp(m2 - m1)
    gate_ref[0:1, :] = 1.0 / (1.0 + e)
    gate_ref[1:2, :] = e / (1.0 + e)
    sel = jnp.where(oh1 | oh2, 1.0, 0.0)
    ranks = []
    seen = jnp.zeros((N_EXPERTS, 1), F32)
    for blk in range(n_blocks):
        s_blk = sel[:, blk * ROUTER_BLOCK:(blk + 1) * ROUTER_BLOCK]
        ranks.append(_dot(s_blk.astype(BF16), before_ref[...]) + seen)
        seen = seen + jnp.sum(s_blk, axis=1, keepdims=True)
    rank = jnp.concatenate(ranks, axis=1)
    eid_out = lax.broadcasted_iota(I32, start_ref.shape, 0).astype(F32)
    start = jnp.zeros(sel.shape, F32)
    start_out = jnp.zeros(start_ref.shape, F32)
    padded_out = jnp.zeros(start_ref.shape, F32)
    for ex in range(N_EXPERTS):
        cnt = jnp.sum(sel[ex:ex + 1, :], axis=1, keepdims=True)
        padded = jnp.ceil(cnt * (1.0 / MOE_GRAN)) * MOE_GRAN
        start = start + jnp.where(eid > ex, padded, 0.0)
        start_out = start_out + jnp.where(eid_out > ex, padded, 0.0)
        padded_out = padded_out + jnp.where(eid_out == ex, padded, 0.0)
    row = (start + rank) * SUBLANES
    dest_ref[0:1, :] = jnp.sum(jnp.where(oh1, row, 0.0), axis=0, keepdims=True).astype(I32)
    dest_ref[1:2, :] = jnp.sum(jnp.where(oh2, row, 0.0), axis=0, keepdims=True).astype(I32)
    start_ref[...] = start_out.astype(I32)
    padded_ref[...] = padded_out.astype(I32)


def _router(xa, xb, l, mods, rw_t, lat_seq):
    n_a = xa.shape[0] // MOE_CHUNK
    n_chunks = n_a + xb.shape[0] // MOE_CHUNK
    tok = np.arange(ROUTER_BLOCK)
    before = jnp.asarray(tok[:, None] < tok[None, :], BF16)
    return pl.pallas_call(
        functools.partial(_router_kernel, n_a=n_a, lat_seq=lat_seq),
        grid=(n_chunks,),
        in_specs=_two_stream_specs(MOE_CHUNK, n_a, lambda c: c) + [
                  _layer_block(mods, l),
                  _full((N_EXPERTS, D_MODEL)), _full((ROUTER_BLOCK, ROUTER_BLOCK))],
        out_specs=[pl.BlockSpec((None, 2, MOE_CHUNK), lambda c: (c, 0, 0)),
                   pl.BlockSpec((None, 2, MOE_CHUNK), lambda c: (c, 0, 0)),
                   pl.BlockSpec((None, N_EXPERTS, LANES), lambda c: (c, 0, 0)),
                   pl.BlockSpec((None, N_EXPERTS, LANES), lambda c: (c, 0, 0))],
        out_shape=[jax.ShapeDtypeStruct((n_chunks, 2, MOE_CHUNK), I32),
                   jax.ShapeDtypeStruct((n_chunks, 2, MOE_CHUNK), F32),
                   jax.ShapeDtypeStruct((n_chunks, N_EXPERTS, LANES), I32),
                   jax.ShapeDtypeStruct((n_chunks, N_EXPERTS, LANES), I32)],
        compiler_params=pltpu.CompilerParams(dimension_semantics=("arbitrary",), vmem_limit_bytes=VMEM_LIMIT),
        name="moe_router",
    )(xa, xb, mods, rw_t, before)


def _token_rows(first_row):
    return pl.ds(pl.multiple_of(first_row, SUBLANES), SUBLANES)


def _store_token_major(ref, tok0, val):
    for cc in range(D_MODEL // LANES):
        ref[pl.ds(tok0 * SUBLANES + cc, val.shape[0], stride=SUBLANES), :] = val[:, cc * LANES:(cc + 1) * LANES]


def _load_token_major(ref, tok0, n):
    return jnp.concatenate(
        [ref[pl.ds(tok0 * SUBLANES + cc, n, stride=SUBLANES), :] for cc in range(D_MODEL // LANES)], axis=1)


def _moe_kernel(start_ref, padded_ref, xa_ref, xb_ref, mod_ref, dest_ref, gate_ref, w13_ref, w2_ref,
                g_ref, b_ref, oa_ref, ob_ref, tok_scr, rows_scr, *, n_a):
    c = pl.program_id(0)
    j = pl.program_id(1)
    is_a = c < n_a
    first_expert_step = MOE_TOK_STEPS
    first_combine_step = MOE_TOK_STEPS + N_EXPERTS

    @pl.when((c == 0) & (j == 0))
    def _init():
        rows_scr[...] = jnp.zeros(rows_scr.shape, F32)

    @pl.when(j < first_expert_step)
    def _dispatch():
        for r in range(MOE_TOK_BLOCK // ROW_BLOCK):
            rows = pl.ds(r * ROW_BLOCK, ROW_BLOCK)
            x = jnp.where(is_a, xa_ref[rows, :], xb_ref[rows, :])
            h = _standardize(x) * (1.0 + mod_ref[4:5, :]) + mod_ref[3:4, :]
            _store_token_major(tok_scr, r * ROW_BLOCK, h)
        t0 = j * MOE_TOK_BLOCK

        def body(t, carry):
            row = tok_scr[_token_rows(t * SUBLANES), :]
            rows_scr[_token_rows(dest_ref[t0 + t]), :] = row
            rows_scr[_token_rows(dest_ref[MOE_CHUNK + t0 + t]), :] = row
            return carry
        lax.fori_loop(0, MOE_TOK_BLOCK, body, 0, unroll=8)

    def experts(row0, m):
        xin = _load_token_major(rows_scr, row0, m).astype(BF16)
        ab = _dot(xin, w13_ref[...])
        a, b = ab[:, :D_FF_EXPERT], ab[:, D_FF_EXPERT:]
        y = _dot((_silu(a) * b).astype(BF16), w2_ref[...])
        _store_token_major(rows_scr, row0, y)

    @pl.when((j >= first_expert_step) & (j < first_combine_step))
    def _experts():
        region = c * N_EXPERTS + (j - first_expert_step)
        start = start_ref[region]
        padded = padded_ref[region]
        n_full = lax.shift_right_logical(padded, MOE_BLOCK.bit_length() - 1)

        def body(i, carry):
            experts(start + 2 * i * MOE_BLOCK, MOE_BLOCK)
            experts(start + (2 * i + 1) * MOE_BLOCK, MOE_BLOCK)
            return carry
        lax.fori_loop(0, lax.shift_right_logical(n_full, 1), body, 0)
        pl.when((n_full & 1) == 1)(functools.partial(experts, start + (n_full - 1) * MOE_BLOCK, MOE_BLOCK))
        for m in range(MOE_GRAN, MOE_BLOCK, MOE_GRAN):
            pl.when(padded - n_full * MOE_BLOCK == m)(
                functools.partial(experts, start + n_full * MOE_BLOCK, m))

    @pl.when(j >= first_combine_step)
    def _combine():
        t0 = (j - first_combine_step) * MOE_TOK_BLOCK

        def body(t, carry):
            y0 = rows_scr[_token_rows(dest_ref[t0 + t]), :]
            y1 = rows_scr[_token_rows(dest_ref[MOE_CHUNK + t0 + t]), :]
            tok_scr[_token_rows(t * SUBLANES), :] = gate_ref[t0 + t] * y0 + gate_ref[MOE_CHUNK + t0 + t] * y1
            return carry
        lax.fori_loop(0, MOE_TOK_BLOCK, body, 0, unroll=8)
        for r in range(MOE_TOK_BLOCK // ROW_BLOCK):
            rows = pl.ds(r * ROW_BLOCK, ROW_BLOCK)
            ffn = _load_token_major(tok_scr, r * ROW_BLOCK, ROW_BLOCK)
            x = jnp.where(is_a, xa_ref[rows, :], xb_ref[rows, :])
            y = DEEPNORM_ALPHA * x + mod_ref[5:6, :] * ffn
            out = _standardize(y) * g_ref[...] + b_ref[...]

            @pl.when(is_a)
            def _():
                oa_ref[rows, :] = out

            @pl.when(jnp.logical_not(is_a))
            def _():
                ob_ref[rows, :] = out


def _ffn_moe(xa, xb, l, mods, rw_t, w13, w2, g, b, lat_seq):
    n_a = xa.shape[0] // MOE_CHUNK
    n_chunks = n_a + xb.shape[0] // MOE_CHUNK
    n_a_blocks = n_a * MOE_TOK_STEPS
    dest, gates, start, padded = _router(xa, xb, l, mods, rw_t, lat_seq)
    first_expert_step = MOE_TOK_STEPS
    first_combine_step = MOE_TOK_STEPS + N_EXPERTS

    def token_block(c, j, *_):
        blk = jnp.where(j < first_combine_step, jnp.minimum(j, MOE_TOK_STEPS - 1), j - first_combine_step)
        return c * MOE_TOK_STEPS + blk

    def out_token_block(c, j, *_):
        return c * MOE_TOK_STEPS + jnp.maximum(j - first_combine_step, 0)

    def mod_of(c, j, *_):
        return (l, _group_of_block(token_block(c, j), MOE_TOK_BLOCK, n_a_blocks, lat_seq), 0, 0)

    def expert_of(c, j, *_):
        return (jnp.clip(j - first_expert_step, 0, N_EXPERTS - 1), 0, 0)

    grid_spec = pltpu.PrefetchScalarGridSpec(
        num_scalar_prefetch=2,
        grid=(n_chunks, MOE_STEPS),
        in_specs=_two_stream_specs(MOE_TOK_BLOCK, n_a_blocks, token_block) + [
                  pl.BlockSpec((None, None, SUBLANES, D_MODEL), mod_of),
                  pl.BlockSpec((2 * MOE_CHUNK,), lambda c, j, *_: (c,), memory_space=pltpu.SMEM),
                  pl.BlockSpec((2 * MOE_CHUNK,), lambda c, j, *_: (c,), memory_space=pltpu.SMEM),
                  pl.BlockSpec((None, D_MODEL, 2 * D_FF_EXPERT), expert_of),
                  pl.BlockSpec((None, D_FF_EXPERT, D_MODEL), expert_of),
                  _layer_block(g, l), _layer_block(b, l)],
        out_specs=_two_stream_specs(MOE_TOK_BLOCK, n_a_blocks, out_token_block),
        scratch_shapes=[pltpu.VMEM((MOE_TOK_BLOCK * SUBLANES, LANES), F32),
                        pltpu.VMEM((MOE_ROWS * SUBLANES, LANES), F32)],
    )
    return pl.pallas_call(
        functools.partial(_moe_kernel, n_a=n_a),
        grid_spec=grid_spec,
        out_shape=[jax.ShapeDtypeStruct(xa.shape, F32), jax.ShapeDtypeStruct(xb.shape, F32)],
        compiler_params=pltpu.CompilerParams(
            dimension_semantics=("arbitrary", "arbitrary"), vmem_limit_bytes=VMEM_LIMIT),
        name="moe_experts",
    )(start[:, :, 0].reshape(-1), padded[:, :, 0].reshape(-1), xa, xb, mods,
      dest.reshape(-1), gates.reshape(-1), w13, w2, g, b)


def _rope_tables(n_tokens):
    t = np.arange(n_tokens)
    row = (t // GRID_W).astype(np.float32)
    col = (t % GRID_W).astype(np.float32)
    inv_freq = (np.float32(ROPE_THETA) ** (-np.arange(0, AXIS_ROT, 2, dtype=np.float32) / AXIS_ROT)).astype(np.float32)
    ang_r = row[:, None] * inv_freq
    ang_c = col[:, None] * inv_freq
    cos = np.concatenate([np.cos(ang_r), np.cos(ang_r), np.cos(ang_c), np.cos(ang_c)], axis=1)
    sin = np.concatenate([-np.sin(ang_r), np.sin(ang_r), -np.sin(ang_c), np.sin(ang_c)], axis=1)
    return jnp.asarray(np.tile(cos, (1, 2)), F32), jnp.asarray(np.tile(sin, (1, 2)), F32)


def kernel(x_prompt, x_sample, cache_k, cache_v, c, c_ctx, ada_w, ada_b, w_in, q_norm_g, k_norm_g, conv_w, conv_b, sgu_norm_g, sgu_w, sgu_b, w_out, ln1_g, ln1_b, ln2_g, ln2_b, ffn_w1, ffn_w3, ffn_w2, router_w, moe_w1, moe_w3, moe_w2):
    batch, seq, _ = x_prompt.shape
    dec_batch, dec_seq, _ = x_sample.shape
    past_len = cache_k.shape[2]
    n_ctx = batch * seq
    n_lat = dec_batch * dec_seq
    assert DEPTH == 2 and 1 + dec_batch <= SUBLANES
    assert seq == ATT_BLOCK and CTX_SEQS_PER_STEP * seq == ROW_BLOCK and batch % CTX_SEQS_PER_STEP == 0
    assert dec_seq % ROW_BLOCK == 0 and dec_seq & (dec_seq - 1) == 0
    assert n_ctx % MOE_CHUNK == 0 and n_lat % MOE_CHUNK == 0
    assert dec_seq % ROUTER_BLOCK == 0 and dec_seq % MOE_TOK_BLOCK == 0 and dec_seq % FFN_TILE == 0

    cond = jnp.zeros((SUBLANES, D_MODEL), F32).at[0].set(c_ctx).at[1:1 + dec_batch].set(c)
    mod = _modulation(cond, ada_w, ada_b)
    mod = mod.reshape(DEPTH, SUBLANES, 6, D_MODEL)[:, :1 + dec_batch]
    mod = jnp.pad(mod, ((0, 0), (0, 0), (0, SUBLANES - 6), (0, 0)))

    lane_id = np.arange(GROUP_TILE) // HEAD_DIM
    ones_bd = jnp.asarray(lane_id[:, None] == lane_id[None, :], BF16)
    cos, sin = _rope_tables(dec_seq)
    small = (
        jnp.tile(q_norm_g, (1, N_Q_HEADS))[:, None, :], jnp.tile(k_norm_g, (1, N_KV_HEADS))[:, None, :],
        ones_bd,
        conv_w, conv_b[:, None, :], sgu_norm_g[:, None, :],
        jnp.swapaxes(sgu_w, 1, 2).reshape(DEPTH, CHUNK, SGU_HEADS * CHUNK).astype(BF16),
        jnp.repeat(jnp.swapaxes(sgu_b, 1, 2), HEAD_DIM, axis=2),
        ln1_g[:, None, :], ln1_b[:, None, :],
    )
    g2, b2 = ln2_g[:, None, :], ln2_b[:, None, :]
    kc = cache_k.reshape(dec_batch, DEPTH, past_len, KV_WIDTH)
    vc = cache_v.reshape(dec_batch, DEPTH, past_len, KV_WIDTH)

    xs = [x_prompt.reshape(n_ctx, D_MODEL), x_sample.reshape(n_lat, D_MODEL)]
    assert DEPTH == 2 and ffn_w1.shape[0] == 1 and moe_w1.shape[0] == 1
    side_casts = [[(ffn_w1[0],), (ffn_w3[0],), (ffn_w2[0],), ((w_in, 1),), ((w_out, 1),), (moe_w2.reshape(-1, D_MODEL),)],
                  [(moe_w1.reshape(-1, D_FF_EXPERT), moe_w3.reshape(-1, D_FF_EXPERT))]]
    kv = None
    for l in range(DEPTH):
        if l == 0:
            x_ctx, k_ctx, v_ctx, w_in_bf, w_out_bf, *ffn_bf, w_in_next, w_out_next, w2_bf = _mixer_ctx(
                xs[0], l, batch, seq, mod, w_in, w_out, small, kv, side_casts[l])
        else:
            w_in_bf, w_out_bf = w_in_next, w_out_next
            x_ctx, k_ctx, v_ctx, w13_bf = _mixer_ctx(
                xs[0], l, batch, seq, mod, w_in_bf, w_out_bf, small, kv, side_casts[l])
        kv = (k_ctx, v_ctx)
        x_lat = _mixer_lat(xs[1], l, dec_batch, dec_seq, mod, w_in_bf, w_out_bf, small, cos, sin, kc, vc)
        if l % 2 == 0:
            xs = _ffn_dense(x_ctx, x_lat, l, mod, *ffn_bf, g2, b2, dec_seq)
        else:
            ws = (w13_bf.reshape(N_EXPERTS, D_MODEL, 2 * D_FF_EXPERT), w2_bf.reshape(N_EXPERTS, D_FF_EXPERT, D_MODEL))
            xs = _ffn_moe(x_ctx, x_lat, l, mod, router_w[l // 2].T, *ws, g2, b2, dec_seq)
    y_p = xs[0].reshape(batch, seq, D_MODEL)
    y_s = xs[1].reshape(dec_batch, dec_seq, D_MODEL)
    new_k = kv[0].reshape(batch, DEPTH, seq, N_KV_HEADS, HEAD_DIM)
    new_v = kv[1].reshape(batch, DEPTH, seq, N_KV_HEADS, HEAD_DIM)
    return (y_p, y_s, new_k, new_v)
```

```python
import functools

import numpy as np
import jax
import jax.numpy as jnp
from jax import lax
from jax.experimental import pallas as pl
from jax.experimental.pallas import tpu as pltpu

F32 = jnp.float32
BF16 = jnp.bfloat16
I32 = jnp.int32

D_MODEL = 1024
DEPTH = 2
GRID_W = 64
HEAD_DIM = 64
N_Q_HEADS = 8
N_KV_HEADS = 2
ATTN_WIDTH = N_Q_HEADS * HEAD_DIM
KV_WIDTH = N_KV_HEADS * HEAD_DIM
ATTN_SCALE = HEAD_DIM ** -0.5
ROPE_THETA = 10000.0
AXIS_ROT = HEAD_DIM // 2
CONV_WIDTH = 256
SGU_WIDTH = 256
SGU_HEADS = 4
CHUNK = 128
IN_WIDTH = 2048
D_FF = 2816
N_EXPERTS = 8
D_FF_EXPERT = 1408
EPS = 1e-6
DEEPNORM_ALPHA = (2 * DEPTH) ** 0.25

LANES = 128
SUBLANES = 8
ROW_BLOCK = 512
GROUP_TILE = 256
ATT_BLOCK = 256
CTX_SEQS_PER_STEP = 2
FFN_TILE = 512
FFN_COLS = 256
MOE_CHUNK = 2048
ROUTER_BLOCK = 1024
MOE_GRAN = 128
MOE_BLOCK = 256
MOE_TOK_BLOCK = 512
MOE_SUB_BLOCK = 256
MOE_TOK_STEPS = MOE_CHUNK // MOE_TOK_BLOCK
MOE_STEPS = 2 * MOE_TOK_STEPS + N_EXPERTS
MOE_ROWS = 2 * MOE_CHUNK + N_EXPERTS * MOE_GRAN
VMEM_LIMIT = 60 * 1024 * 1024

_Q0, _K0, _V0, _CI0, _CB0, _CC0, _SU0, _SV0 = 0, 512, 640, 768, 1024, 1280, 1536, 1792


def _dot(a, b):
    return jnp.dot(a, b, preferred_element_type=F32)


def _dot_nt(a, b):
    return lax.dot_general(a, b, (((1,), (1,)), ((), ())), preferred_element_type=F32)


def _split(x):
    hi = x.astype(BF16)
    lo = (x - hi.astype(F32)).astype(BF16)
    return hi, lo


def _group_sum(x, ones_bd):
    outs = []
    for c0 in range(0, x.shape[1], GROUP_TILE):
        width = min(GROUP_TILE, x.shape[1] - c0)
        outs.append(_dot(x[:, c0:c0 + width].astype(BF16), ones_bd[:width, :width]))
    return outs[0] if len(outs) == 1 else jnp.concatenate(outs, axis=1)


def _standardize(x):
    mu = jnp.mean(x, axis=-1, keepdims=True)
    d = x - mu
    return d * lax.rsqrt(jnp.mean(d * d, axis=-1, keepdims=True) + EPS)


def _silu(x):
    return x / (1.0 + jnp.exp(-x))


def _modulation_kernel(cond_ref, w_ref, b_ref, o_ref):
    s_hi, s_lo = _split(_silu(cond_ref[...]))
    w_hi, w_lo = _split(w_ref[...])
    o_ref[...] = _dot(s_hi, w_hi) + _dot(s_hi, w_lo) + _dot(s_lo, w_hi) + b_ref[...]


def _modulation(cond, ada_w, ada_b):
    n_out = ada_w.shape[-1]
    tn = 1536
    return pl.pallas_call(
        _modulation_kernel,
        grid=(DEPTH, n_out // tn),
        in_specs=[
            pl.BlockSpec((SUBLANES, D_MODEL), lambda l, j: (0, 0)),
            pl.BlockSpec((None, D_MODEL, tn), lambda l, j: (l, 0, j)),
            pl.BlockSpec((None, 1, tn), lambda l, j: (l, 0, j)),
        ],
        out_specs=pl.BlockSpec((None, SUBLANES, tn), lambda l, j: (l, 0, j)),
        out_shape=jax.ShapeDtypeStruct((DEPTH, SUBLANES, n_out), F32),
        compiler_params=pltpu.CompilerParams(
            dimension_semantics=("arbitrary", "arbitrary"), vmem_limit_bytes=VMEM_LIMIT),
        name="modulation",
    )(cond, ada_w, ada_b.reshape(DEPTH, 1, n_out))


def _rope(x, cos, sin_signed):
    w = x.shape[1]
    lane = lax.broadcasted_iota(I32, x.shape, 1)
    first_half = (lane & 31) < 16
    partner = jnp.where(first_half, pltpu.roll(x, w - 16, 1), pltpu.roll(x, 16, 1))
    return x * cos + partner * sin_signed


def _head_variants(x):
    lane = lax.broadcasted_iota(I32, x.shape, 1)
    lo = lane < HEAD_DIM
    xr = pltpu.roll(x, HEAD_DIM, 1)
    zero = jnp.zeros_like(x)
    return (jnp.where(lo, x, zero).astype(BF16), jnp.where(lo, zero, xr).astype(BF16),
            jnp.where(lo, xr, zero).astype(BF16), jnp.where(lo, zero, x).astype(BF16))


def _mixer_kernel(*refs, seq, n_seq, n_cache, rope, cast_weights, stack_kv, side_jobs=(), n_steps=None):
    refs = list(refs)
    skew = n_steps is not None

    def take(n):
        out, refs[:] = refs[:n], refs[n:]
        return out

    (x_ref,) = take(1)
    xres_ref = take(1)[0] if skew else x_ref
    mod_ref, win_ref, wout_ref = take(3)
    qg_ref, kg_ref, ones_ref, convw_ref, convb_ref, sgug_ref, sguw_ref, sgub_ref, ln1g_ref, ln1b_ref = take(10)
    if rope:
        cos_ref, sin_ref, kc_ref, vc_ref = take(4)
    if stack_kv:
        kprev_ref, vprev_ref = take(2)
    side_in = [take(n_src) for n_src in side_jobs]
    (x1_ref,) = take(1)
    if not rope:
        k_ref, v_ref = take(2)
    if cast_weights:
        winb_ref, woutb_ref = take(2)

        @pl.when(pl.program_id(0) == 0)
        def _cast():
            winb_ref[...] = win_ref[...].astype(BF16)
            woutb_ref[...] = wout_ref[...].astype(BF16)
        win_ref, wout_ref = winb_ref, woutb_ref
    side_out = take(len(side_jobs))
    scratch = take(7)
    (mix_scr,) = take(1)
    n_rows = n_seq * seq
    assert n_cache == 0 or n_seq == 1

    def loop(n, body):
        if n == 1:
            body(0)
        else:
            def step(r, carry):
                body(r)
                return carry
            lax.fori_loop(0, n, step, 0, unroll=2)

    def block(r, size):
        if isinstance(r, int):
            return pl.ds(r * size, size)
        return pl.ds(pl.multiple_of(r * size, size), size)

    def project_phase(scr):
        q_scr, kvar_scr, vvar_scr, u_scr, cb_scr, su_scr, vn_scr = scr
        for srcs, dst in zip(side_in, side_out):
            col = 0
            for src in srcs:
                dst[:, col:col + src.shape[1]] = src[...].astype(BF16)
                col += src.shape[1]
        if n_cache:
            for i, var in enumerate(_head_variants(kc_ref[...])):
                kvar_scr[i, pl.ds(seq, n_cache), :] = var
            for i, var in enumerate(_head_variants(vc_ref[...])):
                vvar_scr[i, pl.ds(seq, n_cache), :] = var

        def project(r):
            rows = block(r, ROW_BLOCK)
            x = x_ref[rows, :]
            h = _standardize(x) * (1.0 + mod_ref[1:2, :]) + mod_ref[0:1, :]
            z = _dot(h.astype(BF16), win_ref[...])
            ones_bd = ones_ref[...]
            zq = z[:, _Q0:_K0]
            q = zq * lax.rsqrt(_group_sum(zq * zq, ones_bd) * (1.0 / HEAD_DIM) + EPS) * qg_ref[...]
            zk = z[:, _K0:_V0]
            k = zk * lax.rsqrt(_group_sum(zk * zk, ones_bd) * (1.0 / HEAD_DIM) + EPS) * kg_ref[...]
            v = z[:, _V0:_CI0]
            if rope:
                cos = cos_ref[rows, :]
                sin = sin_ref[rows, :]
                q = _rope(q, jnp.concatenate([cos] * 4, axis=1), jnp.concatenate([sin] * 4, axis=1))
                k = _rope(k, cos, sin)
            elif stack_kv:
                for s in range(ROW_BLOCK // seq):
                    sub = slice(s * seq, (s + 1) * seq)
                    k_ref[s, 0] = kprev_ref[sub, :]
                    v_ref[s, 0] = vprev_ref[sub, :]
                    k_ref[s, 1] = k[sub, :]
                    v_ref[s, 1] = v[sub, :]
            else:
                k_ref[rows, :] = k
                v_ref[rows, :] = v
            q_scr[rows, :] = (q * ATTN_SCALE).astype(BF16)
            for i, var in enumerate(_head_variants(k)):
                kvar_scr[i, rows, :] = var
            for i, var in enumerate(_head_variants(v)):
                vvar_scr[i, rows, :] = var
            u_scr[rows, :] = z[:, _CC0:_SU0] * z[:, _CI0:_CB0]
            cb_scr[rows, :] = z[:, _CB0:_CC0]
            su_scr[rows, :] = z[:, _SU0:_SV0]
            sv = z[:, _SV0:IN_WIDTH]
            d = sv - _group_sum(sv, ones_bd) * (1.0 / HEAD_DIM)
            vn = d * lax.rsqrt(_group_sum(d * d, ones_bd) * (1.0 / HEAD_DIM) + EPS) * sgug_ref[...]
            vn_scr[rows, :] = vn.astype(BF16)

        loop(n_rows // ROW_BLOCK, project)

    def consume_phase(scr):
        q_scr, kvar_scr, vvar_scr, u_scr, cb_scr, su_scr, vn_scr = scr
        u = u_scr[...]
        pos = lax.broadcasted_iota(I32, u.shape, 0) & (seq - 1)
        up = jnp.where(pos == 0, 0.0, pltpu.roll(u, 1, 0))
        dn = jnp.where(pos == seq - 1, 0.0, pltpu.roll(u, n_rows - 1, 0))
        conv = up * convw_ref[0:1, :] + u * convw_ref[1:2, :] + dn * convw_ref[2:3, :] + convb_ref[...]
        mix_scr[:, ATTN_WIDTH:ATTN_WIDTH + CONV_WIDTH] = (cb_scr[...] * conv).astype(BF16)

        for n in range(n_rows // CHUNK):
            rows = pl.ds(n * CHUNK, CHUNK)
            vn = vn_scr[rows, :]
            lane = lax.broadcasted_iota(I32, vn.shape, 1)
            per_head = [jnp.where((lane >= hd * HEAD_DIM) & (lane < (hd + 1) * HEAD_DIM), vn, jnp.zeros_like(vn))
                        for hd in range(SGU_HEADS)]
            s = sgub_ref[...] + _dot(sguw_ref[...], jnp.concatenate(per_head, axis=0))
            mix_scr[rows, ATTN_WIDTH + CONV_WIDTH:] = (su_scr[rows, :] * s).astype(BF16)

        def attend(s, r):
            rows = block(s * (seq // ATT_BLOCK) + r, ATT_BLOCK)
            keys = pl.ds(s * seq, seq + n_cache)
            for pair in range(N_Q_HEADS // 2):
                qp = q_scr[rows, pair * LANES:(pair + 1) * LANES]
                kv = pair // (N_Q_HEADS // N_KV_HEADS // 2)
                acc = jnp.zeros((ATT_BLOCK, LANES), F32)
                for parity in range(2):
                    sc = _dot_nt(qp, kvar_scr[2 * kv + parity, keys, :])
                    p = jnp.exp(sc - jnp.max(sc, axis=1, keepdims=True))
                    denom = jnp.sum(p, axis=1, keepdims=True)
                    acc = acc + _dot(p.astype(BF16), vvar_scr[2 * kv + parity, keys, :]) / denom
                mix_scr[rows, pair * LANES:(pair + 1) * LANES] = acc.astype(BF16)

        for s in range(n_seq):
            loop(seq // ATT_BLOCK, functools.partial(attend, s))

        def finish(r):
            rows = block(r, ROW_BLOCK)
            mix = _dot(mix_scr[rows, :], wout_ref[...])
            y = DEEPNORM_ALPHA * xres_ref[rows, :] + mod_ref[2:3, :] * mix
            x1_ref[rows, :] = _standardize(y) * ln1g_ref[...] + ln1b_ref[...]

        loop(n_rows // ROW_BLOCK, finish)

    if not skew:
        project_phase(scratch)
        consume_phase(scratch)
        return

    i = pl.program_id(0)
    slot = lax.rem(i, 2)
    mine = [ref.at[slot] for ref in scratch]
    other = [ref.at[1 - slot] for ref in scratch]
    pl.when(i == 0)(functools.partial(project_phase, mine))

    @pl.when((i > 0) & (i < n_steps))
    def _():
        consume_phase(other)
        project_phase(mine)

    pl.when(i == n_steps)(functools.partial(consume_phase, other))


def _full(shape):
    n = len(shape)
    return pl.BlockSpec(shape, lambda *_: (0,) * n)


def _resident(shape):
    n = len(shape)
    return pl.BlockSpec(shape, lambda *_: (0,) * n, pipeline_mode=pl.Buffered(1))


def _layer_block(arr, l, resident=False):
    shape = arr.shape[1:]
    kw = dict(pipeline_mode=pl.Buffered(1)) if resident else {}
    return pl.BlockSpec((None,) + shape, lambda *_: (l,) + (0,) * len(shape), **kw)


def _mixer_scratch(n_rows, n_cache, slots=()):
    nk = n_rows + n_cache
    return [
        pltpu.VMEM(slots + (n_rows, ATTN_WIDTH), BF16),
        pltpu.VMEM(slots + (4, nk, LANES), BF16),
        pltpu.VMEM(slots + (4, nk, LANES), BF16),
        pltpu.VMEM(slots + (n_rows, CONV_WIDTH), F32),
        pltpu.VMEM(slots + (n_rows, CONV_WIDTH), F32),
        pltpu.VMEM(slots + (n_rows, SGU_WIDTH), F32),
        pltpu.VMEM(slots + (n_rows, SGU_WIDTH), BF16),
        pltpu.VMEM((n_rows, D_MODEL), BF16),
    ]


def _side_source(src):
    return src if isinstance(src, tuple) else (src, None)


def _mixer_ctx(x, l, n_seq, seq, mod, w_in, w_out, small, kv_prev=None, side_casts=()):
    per_step = CTX_SEQS_PER_STEP
    rows = per_step * seq
    stack_kv = kv_prev is not None
    cast_weights = w_in.dtype != BF16
    n_steps = n_seq // per_step
    kernel = functools.partial(_mixer_kernel, seq=seq, n_seq=per_step, n_cache=0, rope=False,
                               cast_weights=cast_weights, stack_kv=stack_kv,
                               side_jobs=tuple(len(job) for job in side_casts), n_steps=n_steps)
    cur = lambda i: (jnp.minimum(i, n_steps - 1), 0)
    prev = lambda i: (jnp.maximum(i - 1, 0), 0)
    w_specs = ([_layer_block(w_in, l, resident=True), _layer_block(w_out, l, resident=True)] if cast_weights
               else [_resident(w_in.shape), _resident(w_out.shape)])
    in_specs = ([pl.BlockSpec((rows, D_MODEL), cur), pl.BlockSpec((rows, D_MODEL), prev),
                 pl.BlockSpec((None, None, SUBLANES, D_MODEL), lambda i: (l, 0, 0, 0))] + w_specs
                + [_full(a.shape) if a.ndim == 2 else _layer_block(a, l) for a in small])
    args = [x, x, mod, w_in, w_out, *small]
    if stack_kv:
        in_specs += [pl.BlockSpec((rows, KV_WIDTH), cur)] * 2
        args += list(kv_prev)
        kv_spec = pl.BlockSpec((per_step, DEPTH, seq, KV_WIDTH), lambda i: cur(i) + (0, 0))
        kv_shape = jax.ShapeDtypeStruct((n_seq, DEPTH, seq, KV_WIDTH), F32)
    else:
        kv_spec = pl.BlockSpec((rows, KV_WIDTH), cur)
        kv_shape = jax.ShapeDtypeStruct((n_seq * seq, KV_WIDTH), F32)
    side_out_specs, side_out_shapes = [], []
    for job in side_casts:
        sources = [_side_source(src) for src in job]
        n_rows = sources[0][0].shape[-2]
        n_cols = sum(arr.shape[-1] for arr, _ in sources)
        for arr, layer in sources:
            blk = (n_rows // n_steps, arr.shape[-1])
            if layer is None:
                in_specs.append(pl.BlockSpec(blk, cur))
            else:
                in_specs.append(pl.BlockSpec((None,) + blk, lambda i, layer=layer: (layer,) + cur(i)))
            args.append(arr)
        side_out_specs.append(pl.BlockSpec((n_rows // n_steps, n_cols), cur))
        side_out_shapes.append(jax.ShapeDtypeStruct((n_rows, n_cols), BF16))
    w_out_specs, w_out_shapes = [], []
    if cast_weights:
        w_out_specs = [_full((D_MODEL, IN_WIDTH)), _full((D_MODEL, D_MODEL))]
        w_out_shapes = [jax.ShapeDtypeStruct((D_MODEL, IN_WIDTH), BF16), jax.ShapeDtypeStruct((D_MODEL, D_MODEL), BF16)]
    return pl.pallas_call(
        kernel,
        grid=(n_steps + 1,),
        in_specs=in_specs,
        out_specs=[pl.BlockSpec((rows, D_MODEL), prev), kv_spec, kv_spec] + w_out_specs + side_out_specs,
        out_shape=[jax.ShapeDtypeStruct((n_seq * seq, D_MODEL), F32), kv_shape, kv_shape] + w_out_shapes
        + side_out_shapes,
        scratch_shapes=_mixer_scratch(rows, 0, slots=(2,)),
        compiler_params=pltpu.CompilerParams(dimension_semantics=("arbitrary",), vmem_limit_bytes=VMEM_LIMIT),
        name="mixer_ctx",
    )(*args)


def _mixer_lat(x, l, n_seq, seq, mod, w_in_bf, w_out_bf, small, cos, sin, kc, vc):
    n_cache = kc.shape[2]
    kernel = functools.partial(_mixer_kernel, seq=seq, n_seq=1, n_cache=n_cache, rope=True,
                               cast_weights=False, stack_kv=False)
    cache_spec = pl.BlockSpec((None, None, n_cache, KV_WIDTH), lambda b: (b, l, 0, 0))
    return pl.pallas_call(
        kernel,
        grid=(n_seq,),
        in_specs=([pl.BlockSpec((seq, D_MODEL), lambda b: (b, 0)),
                   pl.BlockSpec((None, None, SUBLANES, D_MODEL), lambda b: (l, 1 + b, 0, 0)),
                   _resident((D_MODEL, IN_WIDTH)), _resident((D_MODEL, D_MODEL))]
                  + [_full(a.shape) if a.ndim == 2 else _layer_block(a, l) for a in small]
                  + [_full((seq, LANES)), _full((seq, LANES)), cache_spec, cache_spec]),
        out_specs=pl.BlockSpec((seq, D_MODEL), lambda b: (b, 0)),
        out_shape=jax.ShapeDtypeStruct((n_seq * seq, D_MODEL), F32),
        scratch_shapes=_mixer_scratch(seq, n_cache),
        compiler_params=pltpu.CompilerParams(dimension_semantics=("arbitrary",), vmem_limit_bytes=VMEM_LIMIT),
        name="mixer_lat",
    )(x, mod, w_in_bf, w_out_bf, *small, cos, sin, kc, vc)


def _two_stream_specs(block_rows, n_a, block_of):
    spec_a = pl.BlockSpec((block_rows, D_MODEL), lambda *ids: (jnp.minimum(block_of(*ids), n_a - 1), 0))
    spec_b = pl.BlockSpec((block_rows, D_MODEL), lambda *ids: (jnp.maximum(block_of(*ids) - n_a, 0), 0))
    return [spec_a, spec_b]


def _group_of_block(blk, block_rows, n_a, lat_seq):
    return jnp.where(blk < n_a, 0, 1 + (blk - n_a) // (lat_seq // block_rows))


def _ffn_kernel(xa_ref, xb_ref, mod_ref, w1_ref, w3_ref, w2_ref, g_ref, b_ref, oa_ref, ob_ref, *, n_a):
    is_a = pl.program_id(0) < n_a
    x = jnp.where(is_a, xa_ref[...], xb_ref[...])
    h = (_standardize(x) * (1.0 + mod_ref[4:5, :]) + mod_ref[3:4, :]).astype(BF16)
    acc = jnp.zeros(x.shape, F32)
    for c in range(D_FF // FFN_COLS):
        cols = slice(c * FFN_COLS, (c + 1) * FFN_COLS)
        a = _dot(h, w1_ref[:, cols])
        b = _dot(h, w3_ref[:, cols])
        acc = acc + _dot((_silu(a) * b).astype(BF16), w2_ref[cols, :])
    y = DEEPNORM_ALPHA * x + mod_ref[5:6, :] * acc
    out = _standardize(y) * g_ref[...] + b_ref[...]

    @pl.when(is_a)
    def _():
        oa_ref[...] = out

    @pl.when(jnp.logical_not(is_a))
    def _():
        ob_ref[...] = out


def _ffn_dense(xa, xb, l, mods, w1, w3, w2, g, b, lat_seq):
    n_a, n_b = xa.shape[0] // FFN_TILE, xb.shape[0] // FFN_TILE
    x_specs = _two_stream_specs(FFN_TILE, n_a, lambda t: t)
    return pl.pallas_call(
        functools.partial(_ffn_kernel, n_a=n_a),
        grid=(n_a + n_b,),
        in_specs=x_specs + [
            pl.BlockSpec((None, None, SUBLANES, D_MODEL),
                         lambda t: (l, _group_of_block(t, FFN_TILE, n_a, lat_seq), 0, 0)),
            _resident(w1.shape), _resident(w3.shape), _resident(w2.shape), _layer_block(g, l), _layer_block(b, l)],
        out_specs=x_specs,
        out_shape=[jax.ShapeDtypeStruct(xa.shape, F32), jax.ShapeDtypeStruct(xb.shape, F32)],
        compiler_params=pltpu.CompilerParams(dimension_semantics=("arbitrary",), vmem_limit_bytes=VMEM_LIMIT),
        name="ffn_dense",
    )(xa, xb, mods, w1, w3, w2, g, b)


def _router_kernel(xa_ref, xb_ref, mod_ref, rw_ref, before_ref, dest_ref, gate_ref, start_ref, padded_ref, *,
                   n_a, lat_seq):
    c = pl.program_id(0)
    w_hi, w_lo = _split(rw_ref[...])
    n_blocks = MOE_CHUNK // ROUTER_BLOCK
    parts = []
    for blk in range(n_blocks):
        rows = pl.ds(blk * ROUTER_BLOCK, ROUTER_BLOCK)
        mod = mod_ref[_group_of_block(c * n_blocks + blk, ROUTER_BLOCK, n_a * n_blocks, lat_seq)]
        x = jnp.where(c < n_a, xa_ref[rows, :], xb_ref[rows, :])
        h = _standardize(x) * (1.0 + mod[4:5, :]) + mod[3:4, :]
        h_hi, h_lo = _split(h)
        parts.append(_dot_nt(w_hi, h_hi) + _dot_nt(w_hi, h_lo) + _dot_nt(w_lo, h_hi))
    logits = jnp.concatenate(parts, axis=1)
    eid = lax.broadcasted_iota(I32, logits.shape, 0).astype(F32)
    m1 = jnp.max(logits, axis=0, keepdims=True)
    i1 = jnp.min(jnp.where(logits == m1, eid, float(N_EXPERTS)), axis=0, keepdims=True)
    oh1 = eid == i1
    rest = jnp.where(oh1, -jnp.inf, logits)
    m2 = jnp.max(rest, axis=0, keepdims=True)
    i2 = jnp.min(jnp.where(rest == m2, eid, float(N_EXPERTS)), axis=0, keepdims=True)
    oh2 = eid == i2
    e = jnp.exp(m2 - m1)
    gate_ref[0:1, :] = 1.0 / (1.0 + e)
    gate_ref[1:2, :] = e / (1.0 + e)
    sel = jnp.where(oh1 | oh2, 1.0, 0.0)
    ranks = []
    seen = jnp.zeros((N_EXPERTS, 1), F32)
    for blk in range(n_blocks):
        s_blk = sel[:, blk * ROUTER_BLOCK:(blk + 1) * ROUTER_BLOCK]
        ranks.append(_dot(s_blk.astype(BF16), before_ref[...]) + seen)
        seen = seen + jnp.sum(s_blk, axis=1, keepdims=True)
    rank = jnp.concatenate(ranks, axis=1)
    eid_out = lax.broadcasted_iota(I32, start_ref.shape, 0).astype(F32)
    start = jnp.zeros(sel.shape, F32)
    start_out = jnp.zeros(start_ref.shape, F32)
    padded_out = jnp.zeros(start_ref.shape, F32)
    for ex in range(N_EXPERTS):
        cnt = jnp.sum(sel[ex:ex + 1, :], axis=1, keepdims=True)
        padded = jnp.ceil(cnt * (1.0 / MOE_GRAN)) * MOE_GRAN
        start = start + jnp.where(eid > ex, padded, 0.0)
        start_out = start_out + jnp.where(eid_out > ex, padded, 0.0)
        padded_out = padded_out + jnp.where(eid_out == ex, padded, 0.0)
    row = (start + rank) * SUBLANES
    dest_ref[0:1, :] = jnp.sum(jnp.where(oh1, row, 0.0), axis=0, keepdims=True).astype(I32)
    dest_ref[1:2, :] = jnp.sum(jnp.where(oh2, row, 0.0), axis=0, keepdims=True).astype(I32)
    start_ref[...] = start_out.astype(I32)
    padded_ref[...] = padded_out.astype(I32)


def _router(xa, xb, l, mods, rw_t, lat_seq):
    n_a = xa.shape[0] // MOE_CHUNK
    n_chunks = n_a + xb.shape[0] // MOE_CHUNK
    tok = np.arange(ROUTER_BLOCK)
    before = jnp.asarray(tok[:, None] < tok[None, :], BF16)
    return pl.pallas_call(
        functools.partial(_router_kernel, n_a=n_a, lat_seq=lat_seq),
        grid=(n_chunks,),
        in_specs=_two_stream_specs(MOE_CHUNK, n_a, lambda c: c) + [
                  _layer_block(mods, l),
                  _full((N_EXPERTS, D_MODEL)), _full((ROUTER_BLOCK, ROUTER_BLOCK))],
        out_specs=[pl.BlockSpec((None, 2, MOE_CHUNK), lambda c: (c, 0, 0)),
                   pl.BlockSpec((None, 2, MOE_CHUNK), lambda c: (c, 0, 0)),
                   pl.BlockSpec((None, N_EXPERTS, LANES), lambda c: (c, 0, 0)),
                   pl.BlockSpec((None, N_EXPERTS, LANES), lambda c: (c, 0, 0))],
        out_shape=[jax.ShapeDtypeStruct((n_chunks, 2, MOE_CHUNK), I32),
                   jax.ShapeDtypeStruct((n_chunks, 2, MOE_CHUNK), F32),
                   jax.ShapeDtypeStruct((n_chunks, N_EXPERTS, LANES), I32),
                   jax.ShapeDtypeStruct((n_chunks, N_EXPERTS, LANES), I32)],
        compiler_params=pltpu.CompilerParams(dimension_semantics=("arbitrary",), vmem_limit_bytes=VMEM_LIMIT),
        name="moe_router",
    )(xa, xb, mods, rw_t, before)


def _token_rows(first_row):
    return pl.ds(pl.multiple_of(first_row, SUBLANES), SUBLANES)


def _store_token_major(ref, tok0, val):
    for cc in range(D_MODEL // LANES):
        ref[pl.ds(tok0 * SUBLANES + cc, val.shape[0], stride=SUBLANES), :] = val[:, cc * LANES:(cc + 1) * LANES]


def _load_token_major(ref, tok0, n):
    return jnp.concatenate(
        [ref[pl.ds(tok0 * SUBLANES + cc, n, stride=SUBLANES), :] for cc in range(D_MODEL // LANES)], axis=1)


def _moe_kernel(start_ref, padded_ref, xa_ref, xb_ref, mod_ref, dest_ref, gate_ref, w13_ref, w2_ref,
                g_ref, b_ref, oa_ref, ob_ref, tok_scr, rows_scr, *, n_a):
    c = pl.program_id(0)
    j = pl.program_id(1)
    is_a = c < n_a
    first_expert_step = MOE_TOK_STEPS
    first_combine_step = MOE_TOK_STEPS + N_EXPERTS

    @pl.when((c == 0) & (j == 0))
    def _init():
        rows_scr[...] = jnp.zeros(rows_scr.shape, F32)

    @pl.when(j < first_expert_step)
    def _dispatch():
        t0 = j * MOE_TOK_BLOCK
        for r in range(MOE_TOK_BLOCK // MOE_SUB_BLOCK):
            rows = pl.ds(r * MOE_SUB_BLOCK, MOE_SUB_BLOCK)
            x = jnp.where(is_a, xa_ref[rows, :], xb_ref[rows, :])
            h = _standardize(x) * (1.0 + mod_ref[4:5, :]) + mod_ref[3:4, :]
            _store_token_major(tok_scr, r * MOE_SUB_BLOCK, h)
            for t in range(r * MOE_SUB_BLOCK, (r + 1) * MOE_SUB_BLOCK):
                row = tok_scr[pl.ds(t * SUBLANES, SUBLANES), :]
                rows_scr[_token_rows(dest_ref[t0 + t]), :] = row
                rows_scr[_token_rows(dest_ref[MOE_CHUNK + t0 + t]), :] = row

    def experts(row0, m):
        xin = _load_token_major(rows_scr, row0, m).astype(BF16)
        ab = _dot(xin, w13_ref[...])
        a, b = ab[:, :D_FF_EXPERT], ab[:, D_FF_EXPERT:]
        y = _dot((_silu(a) * b).astype(BF16), w2_ref[...])
        _store_token_major(rows_scr, row0, y)

    @pl.when((j >= first_expert_step) & (j < first_combine_step))
    def _experts():
        region = c * N_EXPERTS + (j - first_expert_step)
        start = start_ref[region]
        padded = padded_ref[region]
        n_full = lax.shift_right_logical(padded, MOE_BLOCK.bit_length() - 1)

        def body(i, carry):
            experts(start + 2 * i * MOE_BLOCK, MOE_BLOCK)
            experts(start + (2 * i + 1) * MOE_BLOCK, MOE_BLOCK)
            return carry
        lax.fori_loop(0, lax.shift_right_logical(n_full, 1), body, 0)
        pl.when((n_full & 1) == 1)(functools.partial(experts, start + (n_full - 1) * MOE_BLOCK, MOE_BLOCK))
        for m in range(MOE_GRAN, MOE_BLOCK, MOE_GRAN):
            pl.when(padded - n_full * MOE_BLOCK == m)(
                functools.partial(experts, start + n_full * MOE_BLOCK, m))

    @pl.when(j >= first_combine_step)
    def _combine():
        t0 = (j - first_combine_step) * MOE_TOK_BLOCK
        outs = []
        for r in range(MOE_TOK_BLOCK // MOE_SUB_BLOCK):
            for t in range(r * MOE_SUB_BLOCK, (r + 1) * MOE_SUB_BLOCK):
                y0 = rows_scr[_token_rows(dest_ref[t0 + t]), :]
                y1 = rows_scr[_token_rows(dest_ref[MOE_CHUNK + t0 + t]), :]
                tok_scr[pl.ds(t * SUBLANES, SUBLANES), :] = (gate_ref[t0 + t] * y0
                                                             + gate_ref[MOE_CHUNK + t0 + t] * y1)
            rows = pl.ds(r * MOE_SUB_BLOCK, MOE_SUB_BLOCK)
            ffn = _load_token_major(tok_scr, r * MOE_SUB_BLOCK, MOE_SUB_BLOCK)
            x = jnp.where(is_a, xa_ref[rows, :], xb_ref[rows, :])
            y = DEEPNORM_ALPHA * x + mod_ref[5:6, :] * ffn
            outs.append(_standardize(y) * g_ref[...] + b_ref[...])

        @pl.when(is_a)
        def _():
            for r, out in enumerate(outs):
                oa_ref[pl.ds(r * MOE_SUB_BLOCK, MOE_SUB_BLOCK), :] = out

        @pl.when(jnp.logical_not(is_a))
        def _():
            for r, out in enumerate(outs):
                ob_ref[pl.ds(r * MOE_SUB_BLOCK, MOE_SUB_BLOCK), :] = out


def _ffn_moe(xa, xb, l, mods, rw_t, w13, w2, g, b, lat_seq):
    n_a = xa.shape[0] // MOE_CHUNK
    n_chunks = n_a + xb.shape[0] // MOE_CHUNK
    n_a_blocks = n_a * MOE_TOK_STEPS
    dest, gates, start, padded = _router(xa, xb, l, mods, rw_t, lat_seq)
    first_expert_step = MOE_TOK_STEPS
    first_combine_step = MOE_TOK_STEPS + N_EXPERTS

    def token_block(c, j, *_):
        blk = jnp.where(j < first_combine_step, jnp.minimum(j, MOE_TOK_STEPS - 1), j - first_combine_step)
        return c * MOE_TOK_STEPS + blk

    def out_token_block(c, j, *_):
        return c * MOE_TOK_STEPS + jnp.maximum(j - first_combine_step, 0)

    def mod_of(c, j, *_):
        return (l, _group_of_block(token_block(c, j), MOE_TOK_BLOCK, n_a_blocks, lat_seq), 0, 0)

    def expert_of(c, j, *_):
        return (jnp.clip(j - first_expert_step, 0, N_EXPERTS - 1), 0, 0)

    grid_spec = pltpu.PrefetchScalarGridSpec(
        num_scalar_prefetch=2,
        grid=(n_chunks, MOE_STEPS),
        in_specs=_two_stream_specs(MOE_TOK_BLOCK, n_a_blocks, token_block) + [
                  pl.BlockSpec((None, None, SUBLANES, D_MODEL), mod_of),
                  pl.BlockSpec((2 * MOE_CHUNK,), lambda c, j, *_: (c,), memory_space=pltpu.SMEM),
                  pl.BlockSpec((2 * MOE_CHUNK,), lambda c, j, *_: (c,), memory_space=pltpu.SMEM),
                  pl.BlockSpec((None, D_MODEL, 2 * D_FF_EXPERT), expert_of),
                  pl.BlockSpec((None, D_FF_EXPERT, D_MODEL), expert_of),
                  _layer_block(g, l), _layer_block(b, l)],
        out_specs=_two_stream_specs(MOE_TOK_BLOCK, n_a_blocks, out_token_block),
        scratch_shapes=[pltpu.VMEM((MOE_TOK_BLOCK * SUBLANES, LANES), F32),
                        pltpu.VMEM((MOE_ROWS * SUBLANES, LANES), F32)],
    )
    return pl.pallas_call(
        functools.partial(_moe_kernel, n_a=n_a),
        grid_spec=grid_spec,
        out_shape=[jax.ShapeDtypeStruct(xa.shape, F32), jax.ShapeDtypeStruct(xb.shape, F32)],
        compiler_params=pltpu.CompilerParams(
            dimension_semantics=("arbitrary", "arbitrary"), vmem_limit_bytes=VMEM_LIMIT),
        name="moe_experts",
    )(start[:, :, 0].reshape(-1), padded[:, :, 0].reshape(-1), xa, xb, mods,
      dest.reshape(-1), gates.reshape(-1), w13, w2, g, b)


def _rope_tables(n_tokens):
    t = np.arange(n_tokens)
    row = (t // GRID_W).astype(np.float32)
    col = (t % GRID_W).astype(np.float32)
    inv_freq = (np.float32(ROPE_THETA) ** (-np.arange(0, AXIS_ROT, 2, dtype=np.float32) / AXIS_ROT)).astype(np.float32)
    ang_r = row[:, None] * inv_freq
    ang_c = col[:, None] * inv_freq
    cos = np.concatenate([np.cos(ang_r), np.cos(ang_r), np.cos(ang_c), np.cos(ang_c)], axis=1)
    sin = np.concatenate([-np.sin(ang_r), np.sin(ang_r), -np.sin(ang_c), np.sin(ang_c)], axis=1)
    return jnp.asarray(np.tile(cos, (1, 2)), F32), jnp.asarray(np.tile(sin, (1, 2)), F32)


def kernel(x_prompt, x_sample, cache_k, cache_v, c, c_ctx, ada_w, ada_b, w_in, q_norm_g, k_norm_g, conv_w, conv_b, sgu_norm_g, sgu_w, sgu_b, w_out, ln1_g, ln1_b, ln2_g, ln2_b, ffn_w1, ffn_w3, ffn_w2, router_w, moe_w1, moe_w3, moe_w2):
    batch, seq, _ = x_prompt.shape
    dec_batch, dec_seq, _ = x_sample.shape
    past_len = cache_k.shape[2]
    n_ctx = batch * seq
    n_lat = dec_batch * dec_seq
    assert DEPTH == 2 and 1 + dec_batch <= SUBLANES
    assert seq == ATT_BLOCK and CTX_SEQS_PER_STEP * seq == ROW_BLOCK and batch % CTX_SEQS_PER_STEP == 0
    assert dec_seq % ROW_BLOCK == 0 and dec_seq & (dec_seq - 1) == 0
    assert n_ctx % MOE_CHUNK == 0 and n_lat % MOE_CHUNK == 0
    assert dec_seq % ROUTER_BLOCK == 0 and dec_seq % MOE_TOK_BLOCK == 0 and dec_seq % FFN_TILE == 0

    cond = jnp.zeros((SUBLANES, D_MODEL), F32).at[0].set(c_ctx).at[1:1 + dec_batch].set(c)
    mod = _modulation(cond, ada_w, ada_b)
    mod = mod.reshape(DEPTH, SUBLANES, 6, D_MODEL)[:, :1 + dec_batch]
    mod = jnp.pad(mod, ((0, 0), (0, 0), (0, SUBLANES - 6), (0, 0)))

    lane_id = np.arange(GROUP_TILE) // HEAD_DIM
    ones_bd = jnp.asarray(lane_id[:, None] == lane_id[None, :], BF16)
    cos, sin = _rope_tables(dec_seq)
    small = (
        jnp.tile(q_norm_g, (1, N_Q_HEADS))[:, None, :], jnp.tile(k_norm_g, (1, N_KV_HEADS))[:, None, :],
        ones_bd,
        conv_w, conv_b[:, None, :], sgu_norm_g[:, None, :],
        jnp.swapaxes(sgu_w, 1, 2).reshape(DEPTH, CHUNK, SGU_HEADS * CHUNK).astype(BF16),
        jnp.repeat(jnp.swapaxes(sgu_b, 1, 2), HEAD_DIM, axis=2),
        ln1_g[:, None, :], ln1_b[:, None, :],
    )
    g2, b2 = ln2_g[:, None, :], ln2_b[:, None, :]
    kc = cache_k.reshape(dec_batch, DEPTH, past_len, KV_WIDTH)
    vc = cache_v.reshape(dec_batch, DEPTH, past_len, KV_WIDTH)

    xs = [x_prompt.reshape(n_ctx, D_MODEL), x_sample.reshape(n_lat, D_MODEL)]
    assert DEPTH == 2 and ffn_w1.shape[0] == 1 and moe_w1.shape[0] == 1
    side_casts = [[(ffn_w1[0],), (ffn_w3[0],), (ffn_w2[0],), ((w_in, 1),), ((w_out, 1),), (moe_w2.reshape(-1, D_MODEL),)],
                  [(moe_w1.reshape(-1, D_FF_EXPERT), moe_w3.reshape(-1, D_FF_EXPERT))]]
    kv = None
    for l in range(DEPTH):
        if l == 0:
            x_ctx, k_ctx, v_ctx, w_in_bf, w_out_bf, *ffn_bf, w_in_next, w_out_next, w2_bf = _mixer_ctx(
                xs[0], l, batch, seq, mod, w_in, w_out, small, kv, side_casts[l])
        else:
            w_in_bf, w_out_bf = w_in_next, w_out_next
            x_ctx, k_ctx, v_ctx, w13_bf = _mixer_ctx(
                xs[0], l, batch, seq, mod, w_in_bf, w_out_bf, small, kv, side_casts[l])
        kv = (k_ctx, v_ctx)
        x_lat = _mixer_lat(xs[1], l, dec_batch, dec_seq, mod, w_in_bf, w_out_bf, small, cos, sin, kc, vc)
        if l % 2 == 0:
            xs = _ffn_dense(x_ctx, x_lat, l, mod, *ffn_bf, g2, b2, dec_seq)
        else:
            ws = (w13_bf.reshape(N_EXPERTS, D_MODEL, 2 * D_FF_EXPERT), w2_bf.reshape(N_EXPERTS, D_FF_EXPERT, D_MODEL))
            xs = _ffn_moe(x_ctx, x_lat, l, mod, router_w[l // 2].T, *ws, g2, b2, dec_seq)
    y_p = xs[0].reshape(batch, seq, D_MODEL)
    y_s = xs[1].reshape(dec_batch, dec_seq, D_MODEL)
    new_k = kv[0].reshape(batch, DEPTH, seq, N_KV_HEADS, HEAD_DIM)
    new_v = kv[1].reshape(batch, DEPTH, seq, N_KV_HEADS, HEAD_DIM)
    return (y_p, y_s, new_k, new_v)
```

```python
import functools

import numpy as np
import jax
import jax.numpy as jnp
from jax import lax
from jax.experimental import pallas as pl
from jax.experimental.pallas import tpu as pltpu

F32 = jnp.float32
BF16 = jnp.bfloat16
I32 = jnp.int32

D_MODEL = 1024
DEPTH = 2
GRID_W = 64
HEAD_DIM = 64
N_Q_HEADS = 8
N_KV_HEADS = 2
ATTN_WIDTH = N_Q_HEADS * HEAD_DIM
KV_WIDTH = N_KV_HEADS * HEAD_DIM
ATTN_SCALE = HEAD_DIM ** -0.5
ROPE_THETA = 10000.0
AXIS_ROT = HEAD_DIM // 2
CONV_WIDTH = 256
SGU_WIDTH = 256
SGU_HEADS = 4
CHUNK = 128
IN_WIDTH = 2048
D_FF = 2816
N_EXPERTS = 8
D_FF_EXPERT = 1408
EPS = 1e-6
DEEPNORM_ALPHA = (2 * DEPTH) ** 0.25

LANES = 128
SUBLANES = 8
ROW_BLOCK = 512
GROUP_TILE = 256
ATT_BLOCK = 256
CTX_SEQS_PER_STEP = 2
FFN_TILE = 1024
FFN_ROWS = 512
FFN_COLS = 256
MOE_CHUNK = 2048
ROUTER_BLOCK = 1024
MOE_GRAN = 128
MOE_BLOCK = 256
MOE_TOK_BLOCK = 512
MOE_SUB_BLOCK = 256
MOE_TOK_STEPS = MOE_CHUNK // MOE_TOK_BLOCK
MOE_STEPS = 2 * MOE_TOK_STEPS + N_EXPERTS
MOE_ROWS = 2 * MOE_CHUNK + N_EXPERTS * MOE_GRAN
VMEM_LIMIT = 60 * 1024 * 1024

_Q0, _K0, _V0, _CI0, _CB0, _CC0, _SU0, _SV0 = 0, 512, 640, 768, 1024, 1280, 1536, 1792


def _dot(a, b):
    return jnp.dot(a, b, preferred_element_type=F32)


def _dot_nt(a, b):
    return lax.dot_general(a, b, (((1,), (1,)), ((), ())), preferred_element_type=F32)


def _split(x):
    hi = x.astype(BF16)
    lo = (x - hi.astype(F32)).astype(BF16)
    return hi, lo


def _group_sum(x, ones_bd):
    outs = []
    for c0 in range(0, x.shape[1], GROUP_TILE):
        width = min(GROUP_TILE, x.shape[1] - c0)
        outs.append(_dot(x[:, c0:c0 + width].astype(BF16), ones_bd[:width, :width]))
    return outs[0] if len(outs) == 1 else jnp.concatenate(outs, axis=1)


def _standardize(x):
    mu = jnp.mean(x, axis=-1, keepdims=True)
    d = x - mu
    return d * lax.rsqrt(jnp.mean(d * d, axis=-1, keepdims=True) + EPS)


def _silu(x):
    return x / (1.0 + jnp.exp(-x))


def _modulation_kernel(cond_ref, w_ref, b_ref, o_ref):
    s_hi, s_lo = _split(_silu(cond_ref[...]))
    w_hi, w_lo = _split(w_ref[...])
    o_ref[...] = _dot(s_hi, w_hi) + _dot(s_hi, w_lo) + _dot(s_lo, w_hi) + b_ref[...]


def _modulation(cond, ada_w, ada_b):
    n_out = ada_w.shape[-1]
    tn = 1536
    return pl.pallas_call(
        _modulation_kernel,
        grid=(DEPTH, n_out // tn),
        in_specs=[
            pl.BlockSpec((SUBLANES, D_MODEL), lambda l, j: (0, 0)),
            pl.BlockSpec((None, D_MODEL, tn), lambda l, j: (l, 0, j)),
            pl.BlockSpec((None, 1, tn), lambda l, j: (l, 0, j)),
        ],
        out_specs=pl.BlockSpec((None, SUBLANES, tn), lambda l, j: (l, 0, j)),
        out_shape=jax.ShapeDtypeStruct((DEPTH, SUBLANES, n_out), F32),
        compiler_params=pltpu.CompilerParams(
            dimension_semantics=("arbitrary", "arbitrary"), vmem_limit_bytes=VMEM_LIMIT),
        name="modulation",
    )(cond, ada_w, ada_b.reshape(DEPTH, 1, n_out))


def _rope(x, cos, sin_signed):
    w = x.shape[1]
    lane = lax.broadcasted_iota(I32, x.shape, 1)
    first_half = (lane & 31) < 16
    partner = jnp.where(first_half, pltpu.roll(x, w - 16, 1), pltpu.roll(x, 16, 1))
    return x * cos + partner * sin_signed


def _head_variants(x):
    lane = lax.broadcasted_iota(I32, x.shape, 1)
    lo = lane < HEAD_DIM
    xr = pltpu.roll(x, HEAD_DIM, 1)
    zero = jnp.zeros_like(x)
    return (jnp.where(lo, x, zero).astype(BF16), jnp.where(lo, zero, xr).astype(BF16),
            jnp.where(lo, xr, zero).astype(BF16), jnp.where(lo, zero, x).astype(BF16))


def _mixer_kernel(*refs, seq, n_seq, n_cache, rope, cast_weights, stack_kv, side_jobs=(), n_steps=None):
    refs = list(refs)
    skew = n_steps is not None

    def take(n):
        out, refs[:] = refs[:n], refs[n:]
        return out

    (x_ref,) = take(1)
    xres_ref = take(1)[0] if skew else x_ref
    mod_ref, win_ref, wout_ref = take(3)
    qg_ref, kg_ref, ones_ref, convw_ref, convb_ref, sgug_ref, sguw_ref, sgub_ref, ln1g_ref, ln1b_ref = take(10)
    if rope:
        cos_ref, sin_ref, kc_ref, vc_ref = take(4)
    if stack_kv:
        kprev_ref, vprev_ref = take(2)
    side_in = [take(n_src) for n_src in side_jobs]
    (x1_ref,) = take(1)
    if not rope:
        k_ref, v_ref = take(2)
    if cast_weights:
        winb_ref, woutb_ref = take(2)

        @pl.when(pl.program_id(0) == 0)
        def _cast():
            winb_ref[...] = win_ref[...].astype(BF16)
            woutb_ref[...] = wout_ref[...].astype(BF16)
        win_ref, wout_ref = winb_ref, woutb_ref
    side_out = take(len(side_jobs))
    scratch = take(7)
    (mix_scr,) = take(1)
    n_rows = n_seq * seq
    assert n_cache == 0 or n_seq == 1

    def loop(n, body):
        if n == 1:
            body(0)
        else:
            def step(r, carry):
                body(r)
                return carry
            lax.fori_loop(0, n, step, 0, unroll=2)

    def block(r, size):
        if isinstance(r, int):
            return pl.ds(r * size, size)
        return pl.ds(pl.multiple_of(r * size, size), size)

    def project_phase(scr):
        q_scr, kvar_scr, vvar_scr, u_scr, cb_scr, su_scr, vn_scr = scr
        for srcs, dst in zip(side_in, side_out):
            col = 0
            for src in srcs:
                dst[:, col:col + src.shape[1]] = src[...].astype(BF16)
                col += src.shape[1]
        if n_cache:
            for i, var in enumerate(_head_variants(kc_ref[...])):
                kvar_scr[i, pl.ds(seq, n_cache), :] = var
            for i, var in enumerate(_head_variants(vc_ref[...])):
                vvar_scr[i, pl.ds(seq, n_cache), :] = var

        def project(r):
            rows = block(r, ROW_BLOCK)
            x = x_ref[rows, :]
            h = _standardize(x) * (1.0 + mod_ref[1:2, :]) + mod_ref[0:1, :]
            z = _dot(h.astype(BF16), win_ref[...])
            ones_bd = ones_ref[...]
            zq = z[:, _Q0:_K0]
            q = zq * lax.rsqrt(_group_sum(zq * zq, ones_bd) * (1.0 / HEAD_DIM) + EPS) * qg_ref[...]
            zk = z[:, _K0:_V0]
            k = zk * lax.rsqrt(_group_sum(zk * zk, ones_bd) * (1.0 / HEAD_DIM) + EPS) * kg_ref[...]
            v = z[:, _V0:_CI0]
            if rope:
                cos = cos_ref[rows, :]
                sin = sin_ref[rows, :]
                q = _rope(q, jnp.concatenate([cos] * 4, axis=1), jnp.concatenate([sin] * 4, axis=1))
                k = _rope(k, cos, sin)
            elif stack_kv:
                for s in range(ROW_BLOCK // seq):
                    sub = slice(s * seq, (s + 1) * seq)
                    k_ref[s, 0] = kprev_ref[sub, :]
                    v_ref[s, 0] = vprev_ref[sub, :]
                    k_ref[s, 1] = k[sub, :]
                    v_ref[s, 1] = v[sub, :]
            else:
                k_ref[rows, :] = k
                v_ref[rows, :] = v
            q_scr[rows, :] = (q * ATTN_SCALE).astype(BF16)
            for i, var in enumerate(_head_variants(k)):
                kvar_scr[i, rows, :] = var
            for i, var in enumerate(_head_variants(v)):
                vvar_scr[i, rows, :] = var
            u_scr[rows, :] = z[:, _CC0:_SU0] * z[:, _CI0:_CB0]
            cb_scr[rows, :] = z[:, _CB0:_CC0]
            su_scr[rows, :] = z[:, _SU0:_SV0]
            sv = z[:, _SV0:IN_WIDTH]
            d = sv - _group_sum(sv, ones_bd) * (1.0 / HEAD_DIM)
            vn = d * lax.rsqrt(_group_sum(d * d, ones_bd) * (1.0 / HEAD_DIM) + EPS) * sgug_ref[...]
            vn_scr[rows, :] = vn.astype(BF16)

        loop(n_rows // ROW_BLOCK, project)

    def consume_phase(scr):
        q_scr, kvar_scr, vvar_scr, u_scr, cb_scr, su_scr, vn_scr = scr
        u = u_scr[...]
        pos = lax.broadcasted_iota(I32, u.shape, 0) & (seq - 1)
        up = jnp.where(pos == 0, 0.0, pltpu.roll(u, 1, 0))
        dn = jnp.where(pos == seq - 1, 0.0, pltpu.roll(u, n_rows - 1, 0))
        conv = up * convw_ref[0:1, :] + u * convw_ref[1:2, :] + dn * convw_ref[2:3, :] + convb_ref[...]
        mix_scr[:, ATTN_WIDTH:ATTN_WIDTH + CONV_WIDTH] = (cb_scr[...] * conv).astype(BF16)

        for n in range(n_rows // CHUNK):
            rows = pl.ds(n * CHUNK, CHUNK)
            vn = vn_scr[rows, :]
            lane = lax.broadcasted_iota(I32, vn.shape, 1)
            per_head = [jnp.where((lane >= hd * HEAD_DIM) & (lane < (hd + 1) * HEAD_DIM), vn, jnp.zeros_like(vn))
                        for hd in range(SGU_HEADS)]
            s = sgub_ref[...] + _dot(sguw_ref[...], jnp.concatenate(per_head, axis=0))
            mix_scr[rows, ATTN_WIDTH + CONV_WIDTH:] = (su_scr[rows, :] * s).astype(BF16)

        def attend(s, r):
            rows = block(s * (seq // ATT_BLOCK) + r, ATT_BLOCK)
            keys = pl.ds(s * seq, seq + n_cache)
            for pair in range(N_Q_HEADS // 2):
                qp = q_scr[rows, pair * LANES:(pair + 1) * LANES]
                kv = pair // (N_Q_HEADS // N_KV_HEADS // 2)
                acc = jnp.zeros((ATT_BLOCK, LANES), F32)
                for parity in range(2):
                    sc = _dot_nt(qp, kvar_scr[2 * kv + parity, keys, :])
                    p = jnp.exp(sc - jnp.max(sc, axis=1, keepdims=True))
                    denom = jnp.sum(p, axis=1, keepdims=True)
                    acc = acc + _dot(p.astype(BF16), vvar_scr[2 * kv + parity, keys, :]) / denom
                mix_scr[rows, pair * LANES:(pair + 1) * LANES] = acc.astype(BF16)

        for s in range(n_seq):
            loop(seq // ATT_BLOCK, functools.partial(attend, s))

        def finish(r):
            rows = block(r, ROW_BLOCK)
            mix = _dot(mix_scr[rows, :], wout_ref[...])
            y = DEEPNORM_ALPHA * xres_ref[rows, :] + mod_ref[2:3, :] * mix
            x1_ref[rows, :] = _standardize(y) * ln1g_ref[...] + ln1b_ref[...]

        loop(n_rows // ROW_BLOCK, finish)

    if not skew:
        project_phase(scratch)
        consume_phase(scratch)
        return

    i = pl.program_id(0)
    slot = lax.rem(i, 2)
    mine = [ref.at[slot] for ref in scratch]
    other = [ref.at[1 - slot] for ref in scratch]
    pl.when(i == 0)(functools.partial(project_phase, mine))

    @pl.when((i > 0) & (i < n_steps))
    def _():
        consume_phase(other)
        project_phase(mine)

    pl.when(i == n_steps)(functools.partial(consume_phase, other))


def _full(shape):
    n = len(shape)
    return pl.BlockSpec(shape, lambda *_: (0,) * n)


def _resident(shape):
    n = len(shape)
    return pl.BlockSpec(shape, lambda *_: (0,) * n, pipeline_mode=pl.Buffered(1))


def _layer_block(arr, l, resident=False):
    shape = arr.shape[1:]
    kw = dict(pipeline_mode=pl.Buffered(1)) if resident else {}
    return pl.BlockSpec((None,) + shape, lambda *_: (l,) + (0,) * len(shape), **kw)


def _mixer_scratch(n_rows, n_cache, slots=()):
    nk = n_rows + n_cache
    return [
        pltpu.VMEM(slots + (n_rows, ATTN_WIDTH), BF16),
        pltpu.VMEM(slots + (4, nk, LANES), BF16),
        pltpu.VMEM(slots + (4, nk, LANES), BF16),
        pltpu.VMEM(slots + (n_rows, CONV_WIDTH), F32),
        pltpu.VMEM(slots + (n_rows, CONV_WIDTH), F32),
        pltpu.VMEM(slots + (n_rows, SGU_WIDTH), F32),
        pltpu.VMEM(slots + (n_rows, SGU_WIDTH), BF16),
        pltpu.VMEM((n_rows, D_MODEL), BF16),
    ]


def _side_source(src):
    return src if isinstance(src, tuple) else (src, None)


def _mixer_ctx(x, l, n_seq, seq, mod, w_in, w_out, small, kv_prev=None, side_casts=()):
    per_step = CTX_SEQS_PER_STEP
    rows = per_step * seq
    stack_kv = kv_prev is not None
    cast_weights = w_in.dtype != BF16
    n_steps = n_seq // per_step
    kernel = functools.partial(_mixer_kernel, seq=seq, n_seq=per_step, n_cache=0, rope=False,
                               cast_weights=cast_weights, stack_kv=stack_kv,
                               side_jobs=tuple(len(job) for job in side_casts), n_steps=n_steps)
    cur = lambda i: (jnp.minimum(i, n_steps - 1), 0)
    prev = lambda i: (jnp.maximum(i - 1, 0), 0)
    w_specs = ([_layer_block(w_in, l, resident=True), _layer_block(w_out, l, resident=True)] if cast_weights
               else [_resident(w_in.shape), _resident(w_out.shape)])
    in_specs = ([pl.BlockSpec((rows, D_MODEL), cur), pl.BlockSpec((rows, D_MODEL), prev),
                 pl.BlockSpec((None, None, SUBLANES, D_MODEL), lambda i: (l, 0, 0, 0))] + w_specs
                + [_full(a.shape) if a.ndim == 2 else _layer_block(a, l) for a in small])
    args = [x, x, mod, w_in, w_out, *small]
    if stack_kv:
        in_specs += [pl.BlockSpec((rows, KV_WIDTH), cur)] * 2
        args += list(kv_prev)
        kv_spec = pl.BlockSpec((per_step, DEPTH, seq, KV_WIDTH), lambda i: cur(i) + (0, 0))
        kv_shape = jax.ShapeDtypeStruct((n_seq, DEPTH, seq, KV_WIDTH), F32)
    else:
        kv_spec = pl.BlockSpec((rows, KV_WIDTH), cur)
        kv_shape = jax.ShapeDtypeStruct((n_seq * seq, KV_WIDTH), F32)
    side_out_specs, side_out_shapes = [], []
    for job in side_casts:
        sources = [_side_source(src) for src in job]
        n_rows = sources[0][0].shape[-2]
        n_cols = sum(arr.shape[-1] for arr, _ in sources)
        for arr, layer in sources:
            blk = (n_rows // n_steps, arr.shape[-1])
            if layer is None:
                in_specs.append(pl.BlockSpec(blk, cur))
            else:
                in_specs.append(pl.BlockSpec((None,) + blk, lambda i, layer=layer: (layer,) + cur(i)))
            args.append(arr)
        side_out_specs.append(pl.BlockSpec((n_rows // n_steps, n_cols), cur))
        side_out_shapes.append(jax.ShapeDtypeStruct((n_rows, n_cols), BF16))
    w_out_specs, w_out_shapes = [], []
    if cast_weights:
        w_out_specs = [_full((D_MODEL, IN_WIDTH)), _full((D_MODEL, D_MODEL))]
        w_out_shapes = [jax.ShapeDtypeStruct((D_MODEL, IN_WIDTH), BF16), jax.ShapeDtypeStruct((D_MODEL, D_MODEL), BF16)]
    return pl.pallas_call(
        kernel,
        grid=(n_steps + 1,),
        in_specs=in_specs,
        out_specs=[pl.BlockSpec((rows, D_MODEL), prev), kv_spec, kv_spec] + w_out_specs + side_out_specs,
        out_shape=[jax.ShapeDtypeStruct((n_seq * seq, D_MODEL), F32), kv_shape, kv_shape] + w_out_shapes
        + side_out_shapes,
        scratch_shapes=_mixer_scratch(rows, 0, slots=(2,)),
        compiler_params=pltpu.CompilerParams(dimension_semantics=("arbitrary",), vmem_limit_bytes=VMEM_LIMIT),
        name="mixer_ctx",
    )(*args)


def _mixer_lat(x, l, n_seq, seq, mod, w_in_bf, w_out_bf, small, cos, sin, kc, vc):
    n_cache = kc.shape[2]
    kernel = functools.partial(_mixer_kernel, seq=seq, n_seq=1, n_cache=n_cache, rope=True,
                               cast_weights=False, stack_kv=False)
    cache_spec = pl.BlockSpec((None, None, n_cache, KV_WIDTH), lambda b: (b, l, 0, 0))
    return pl.pallas_call(
        kernel,
        grid=(n_seq,),
        in_specs=([pl.BlockSpec((seq, D_MODEL), lambda b: (b, 0)),
                   pl.BlockSpec((None, None, SUBLANES, D_MODEL), lambda b: (l, 1 + b, 0, 0)),
                   _resident((D_MODEL, IN_WIDTH)), _resident((D_MODEL, D_MODEL))]
                  + [_full(a.shape) if a.ndim == 2 else _layer_block(a, l) for a in small]
                  + [_full((seq, LANES)), _full((seq, LANES)), cache_spec, cache_spec]),
        out_specs=pl.BlockSpec((seq, D_MODEL), lambda b: (b, 0)),
        out_shape=jax.ShapeDtypeStruct((n_seq * seq, D_MODEL), F32),
        scratch_shapes=_mixer_scratch(seq, n_cache),
        compiler_params=pltpu.CompilerParams(dimension_semantics=("arbitrary",), vmem_limit_bytes=VMEM_LIMIT),
        name="mixer_lat",
    )(x, mod, w_in_bf, w_out_bf, *small, cos, sin, kc, vc)


def _two_stream_specs(block_rows, n_a, block_of, single_buffer_b=False):
    kw = dict(pipeline_mode=pl.Buffered(1)) if single_buffer_b else {}
    spec_a = pl.BlockSpec((block_rows, D_MODEL), lambda *ids: (jnp.minimum(block_of(*ids), n_a - 1), 0))
    spec_b = pl.BlockSpec((block_rows, D_MODEL), lambda *ids: (jnp.maximum(block_of(*ids) - n_a, 0), 0), **kw)
    return [spec_a, spec_b]


def _group_of_block(blk, block_rows, n_a, lat_seq):
    return jnp.where(blk < n_a, 0, 1 + (blk - n_a) // (lat_seq // block_rows))


def _ffn_kernel(xa_ref, xb_ref, mod_ref, w1_ref, w3_ref, w2_ref, g_ref, b_ref, oa_ref, ob_ref, *, n_a):
    is_a = pl.program_id(0) < n_a
    outs = []
    for r0 in range(0, FFN_TILE, FFN_ROWS):
        rows = slice(r0, r0 + FFN_ROWS)
        x = jnp.where(is_a, xa_ref[rows, :], xb_ref[rows, :])
        h = (_standardize(x) * (1.0 + mod_ref[4:5, :]) + mod_ref[3:4, :]).astype(BF16)
        acc = jnp.zeros(x.shape, F32)
        for c in range(D_FF // FFN_COLS):
            cols = slice(c * FFN_COLS, (c + 1) * FFN_COLS)
            a = _dot(h, w1_ref[:, cols])
            b = _dot(h, w3_ref[:, cols])
            acc = acc + _dot((_silu(a) * b).astype(BF16), w2_ref[cols, :])
        y = DEEPNORM_ALPHA * x + mod_ref[5:6, :] * acc
        outs.append(_standardize(y) * g_ref[...] + b_ref[...])

    @pl.when(is_a)
    def _():
        for i, out in enumerate(outs):
            oa_ref[i * FFN_ROWS:(i + 1) * FFN_ROWS, :] = out

    @pl.when(jnp.logical_not(is_a))
    def _():
        for i, out in enumerate(outs):
            ob_ref[i * FFN_ROWS:(i + 1) * FFN_ROWS, :] = out


def _ffn_dense(xa, xb, l, mods, w1, w3, w2, g, b, lat_seq):
    n_a, n_b = xa.shape[0] // FFN_TILE, xb.shape[0] // FFN_TILE
    x_specs = _two_stream_specs(FFN_TILE, n_a, lambda t: t)
    return pl.pallas_call(
        functools.partial(_ffn_kernel, n_a=n_a),
        grid=(n_a + n_b,),
        in_specs=_two_stream_specs(FFN_TILE, n_a, lambda t: t, single_buffer_b=True) + [
            pl.BlockSpec((None, None, SUBLANES, D_MODEL),
                         lambda t: (l, _group_of_block(t, FFN_TILE, n_a, lat_seq), 0, 0)),
            _resident(w1.shape), _resident(w3.shape), _resident(w2.shape), _layer_block(g, l), _layer_block(b, l)],
        out_specs=x_specs,
        out_shape=[jax.ShapeDtypeStruct(xa.shape, F32), jax.ShapeDtypeStruct(xb.shape, F32)],
        compiler_params=pltpu.CompilerParams(dimension_semantics=("arbitrary",), vmem_limit_bytes=VMEM_LIMIT),
        name="ffn_dense",
    )(xa, xb, mods, w1, w3, w2, g, b)


def _router_kernel(xa_ref, xb_ref, mod_ref, rw_ref, before_ref, dest_ref, gate_ref, start_ref, padded_ref, *,
                   n_a, lat_seq):
    c = pl.program_id(0)
    w_hi, w_lo = _split(rw_ref[...])
    n_blocks = MOE_CHUNK // ROUTER_BLOCK
    parts = []
    for blk in range(n_blocks):
        rows = pl.ds(blk * ROUTER_BLOCK, ROUTER_BLOCK)
        mod = mod_ref[_group_of_block(c * n_blocks + blk, ROUTER_BLOCK, n_a * n_blocks, lat_seq)]
        x = jnp.where(c < n_a, xa_ref[rows, :], xb_ref[rows, :])
        h = _standardize(x) * (1.0 + mod[4:5, :]) + mod[3:4, :]
        h_hi, h_lo = _split(h)
        parts.append(_dot_nt(w_hi, h_hi) + _dot_nt(w_hi, h_lo) + _dot_nt(w_lo, h_hi))
    logits = jnp.concatenate(parts, axis=1)
    eid = lax.broadcasted_iota(I32, logits.shape, 0).astype(F32)
    m1 = jnp.max(logits, axis=0, keepdims=True)
    i1 = jnp.min(jnp.where(logits == m1, eid, float(N_EXPERTS)), axis=0, keepdims=True)
    oh1 = eid == i1
    rest = jnp.where(oh1, -jnp.inf, logits)
    m2 = jnp.max(rest, axis=0, keepdims=True)
    i2 = jnp.min(jnp.where(rest == m2, eid, float(N_EXPERTS)), axis=0, keepdims=True)
    oh2 = eid == i2
    e = jnp.exp(m2 - m1)
    gate_ref[0:1, :] = 1.0 / (1.0 + e)
    gate_ref[1:2, :] = e / (1.0 + e)
    sel = jnp.where(oh1 | oh2, 1.0, 0.0)
    ranks = []
    seen = jnp.zeros((N_EXPERTS, 1), F32)
    for blk in range(n_blocks):
        s_blk = sel[:, blk * ROUTER_BLOCK:(blk + 1) * ROUTER_BLOCK]
        ranks.append(_dot(s_blk.astype(BF16), before_ref[...]) + seen)
        seen = seen + jnp.sum(s_blk, axis=1, keepdims=True)
    rank = jnp.concatenate(ranks, axis=1)
    eid_out = lax.broadcasted_iota(I32, start_ref.shape, 0).astype(F32)
    start = jnp.zeros(sel.shape, F32)
    start_out = jnp.zeros(start_ref.shape, F32)
    padded_out = jnp.zeros(start_ref.shape, F32)
    for ex in range(N_EXPERTS):
        cnt = jnp.sum(sel[ex:ex + 1, :], axis=1, keepdims=True)
        padded = jnp.ceil(cnt * (1.0 / MOE_GRAN)) * MOE_GRAN
        start = start + jnp.where(eid > ex, padded, 0.0)
        start_out = start_out + jnp.where(eid_out > ex, padded, 0.0)
        padded_out = padded_out + jnp.where(eid_out == ex, padded, 0.0)
    row = (start + rank) * SUBLANES
    dest_ref[0:1, :] = jnp.sum(jnp.where(oh1, row, 0.0), axis=0, keepdims=True).astype(I32)
    dest_ref[1:2, :] = jnp.sum(jnp.where(oh2, row, 0.0), axis=0, keepdims=True).astype(I32)
    start_ref[...] = start_out.astype(I32)
    padded_ref[...] = padded_out.astype(I32)


def _router(xa, xb, l, mods, rw_t, lat_seq):
    n_a = xa.shape[0] // MOE_CHUNK
    n_chunks = n_a + xb.shape[0] // MOE_CHUNK
    tok = np.arange(ROUTER_BLOCK)
    before = jnp.asarray(tok[:, None] < tok[None, :], BF16)
    return pl.pallas_call(
        functools.partial(_router_kernel, n_a=n_a, lat_seq=lat_seq),
        grid=(n_chunks,),
        in_specs=_two_stream_specs(MOE_CHUNK, n_a, lambda c: c) + [
                  _layer_block(mods, l),
                  _full((N_EXPERTS, D_MODEL)), _full((ROUTER_BLOCK, ROUTER_BLOCK))],
        out_specs=[pl.BlockSpec((None, 2, MOE_CHUNK), lambda c: (c, 0, 0)),
                   pl.BlockSpec((None, 2, MOE_CHUNK), lambda c: (c, 0, 0)),
                   pl.BlockSpec((None, N_EXPERTS, LANES), lambda c: (c, 0, 0)),
                   pl.BlockSpec((None, N_EXPERTS, LANES), lambda c: (c, 0, 0))],
        out_shape=[jax.ShapeDtypeStruct((n_chunks, 2, MOE_CHUNK), I32),
                   jax.ShapeDtypeStruct((n_chunks, 2, MOE_CHUNK), F32),
                   jax.ShapeDtypeStruct((n_chunks, N_EXPERTS, LANES), I32),
                   jax.ShapeDtypeStruct((n_chunks, N_EXPERTS, LANES), I32)],
        compiler_params=pltpu.CompilerParams(dimension_semantics=("arbitrary",), vmem_limit_bytes=VMEM_LIMIT),
        name="moe_router",
    )(xa, xb, mods, rw_t, before)


def _token_rows(first_row):
    return pl.ds(pl.multiple_of(first_row, SUBLANES), SUBLANES)


def _store_token_major(ref, tok0, val):
    for cc in range(D_MODEL // LANES):
        ref[pl.ds(tok0 * SUBLANES + cc, val.shape[0], stride=SUBLANES), :] = val[:, cc * LANES:(cc + 1) * LANES]


def _load_token_major(ref, tok0, n):
    return jnp.concatenate(
        [ref[pl.ds(tok0 * SUBLANES + cc, n, stride=SUBLANES), :] for cc in range(D_MODEL // LANES)], axis=1)


def _moe_kernel(start_ref, padded_ref, xa_ref, xb_ref, mod_ref, dest_ref, gate_ref, w13_ref, w2_ref,
                g_ref, b_ref, oa_ref, ob_ref, tok_scr, rows_scr, *, n_a):
    c = pl.program_id(0)
    j = pl.program_id(1)
    is_a = c < n_a
    first_expert_step = MOE_TOK_STEPS
    first_combine_step = MOE_TOK_STEPS + N_EXPERTS

    @pl.when((c == 0) & (j == 0))
    def _init():
        rows_scr[...] = jnp.zeros(rows_scr.shape, F32)

    @pl.when(j < first_expert_step)
    def _dispatch():
        t0 = j * MOE_TOK_BLOCK
        for r in range(MOE_TOK_BLOCK // MOE_SUB_BLOCK):
            rows = pl.ds(r * MOE_SUB_BLOCK, MOE_SUB_BLOCK)
            x = jnp.where(is_a, xa_ref[rows, :], xb_ref[rows, :])
            h = _standardize(x) * (1.0 + mod_ref[4:5, :]) + mod_ref[3:4, :]
            _store_token_major(tok_scr, r * MOE_SUB_BLOCK, h)
            for t in range(r * MOE_SUB_BLOCK, (r + 1) * MOE_SUB_BLOCK):
                row = tok_scr[pl.ds(t * SUBLANES, SUBLANES), :]
                rows_scr[_token_rows(dest_ref[t0 + t]), :] = row
                rows_scr[_token_rows(dest_ref[MOE_CHUNK + t0 + t]), :] = row

    def experts(row0, m):
        xin = _load_token_major(rows_scr, row0, m).astype(BF16)
        ab = _dot(xin, w13_ref[...])
        a, b = ab[:, :D_FF_EXPERT], ab[:, D_FF_EXPERT:]
        y = _dot((_silu(a) * b).astype(BF16), w2_ref[...])
        _store_token_major(rows_scr, row0, y)

    @pl.when((j >= first_expert_step) & (j < first_combine_step))
    def _experts():
        region = c * N_EXPERTS + (j - first_expert_step)
        start = start_ref[region]
        padded = padded_ref[region]
        n_full = lax.shift_right_logical(padded, MOE_BLOCK.bit_length() - 1)

        def body(i, carry):
            experts(start + 2 * i * MOE_BLOCK, MOE_BLOCK)
            experts(start + (2 * i + 1) * MOE_BLOCK, MOE_BLOCK)
            return carry
        lax.fori_loop(0, lax.shift_right_logical(n_full, 1), body, 0)
        pl.when((n_full & 1) == 1)(functools.partial(experts, start + (n_full - 1) * MOE_BLOCK, MOE_BLOCK))
        for m in range(MOE_GRAN, MOE_BLOCK, MOE_GRAN):
            pl.when(padded - n_full * MOE_BLOCK == m)(
                functools.partial(experts, start + n_full * MOE_BLOCK, m))

    @pl.when(j >= first_combine_step)
    def _combine():
        t0 = (j - first_combine_step) * MOE_TOK_BLOCK
        outs = []
        for r in range(MOE_TOK_BLOCK // MOE_SUB_BLOCK):
            for t in range(r * MOE_SUB_BLOCK, (r + 1) * MOE_SUB_BLOCK):
                y0 = rows_scr[_token_rows(dest_ref[t0 + t]), :]
                y1 = rows_scr[_token_rows(dest_ref[MOE_CHUNK + t0 + t]), :]
                tok_scr[pl.ds(t * SUBLANES, SUBLANES), :] = (gate_ref[t0 + t] * y0
                                                             + gate_ref[MOE_CHUNK + t0 + t] * y1)
            rows = pl.ds(r * MOE_SUB_BLOCK, MOE_SUB_BLOCK)
            ffn = _load_token_major(tok_scr, r * MOE_SUB_BLOCK, MOE_SUB_BLOCK)
            x = jnp.where(is_a, xa_ref[rows, :], xb_ref[rows, :])
            y = DEEPNORM_ALPHA * x + mod_ref[5:6, :] * ffn
            outs.append(_standardize(y) * g_ref[...] + b_ref[...])

        @pl.when(is_a)
        def _():
            for r, out in enumerate(outs):
                oa_ref[pl.ds(r * MOE_SUB_BLOCK, MOE_SUB_BLOCK), :] = out

        @pl.when(jnp.logical_not(is_a))
        def _():
            for r, out in enumerate(outs):
                ob_ref[pl.ds(r * MOE_SUB_BLOCK, MOE_SUB_BLOCK), :] = out


def _ffn_moe(xa, xb, l, mods, rw_t, w13, w2, g, b, lat_seq):
    n_a = xa.shape[0] // MOE_CHUNK
    n_chunks = n_a + xb.shape[0] // MOE_CHUNK
    n_a_blocks = n_a * MOE_TOK_STEPS
    dest, gates, start, padded = _router(xa, xb, l, mods, rw_t, lat_seq)
    first_expert_step = MOE_TOK_STEPS
    first_combine_step = MOE_TOK_STEPS + N_EXPERTS

    def token_block(c, j, *_):
        blk = jnp.where(j < first_combine_step, jnp.minimum(j, MOE_TOK_STEPS - 1), j - first_combine_step)
        return c * MOE_TOK_STEPS + blk

    def out_token_block(c, j, *_):
        return c * MOE_TOK_STEPS + jnp.maximum(j - first_combine_step, 0)

    def mod_of(c, j, *_):
        return (l, _group_of_block(token_block(c, j), MOE_TOK_BLOCK, n_a_blocks, lat_seq), 0, 0)

    def expert_of(c, j, *_):
        return (jnp.clip(j - first_expert_step, 0, N_EXPERTS - 1), 0, 0)

    grid_spec = pltpu.PrefetchScalarGridSpec(
        num_scalar_prefetch=2,
        grid=(n_chunks, MOE_STEPS),
        in_specs=_two_stream_specs(MOE_TOK_BLOCK, n_a_blocks, token_block) + [
                  pl.BlockSpec((None, None, SUBLANES, D_MODEL), mod_of),
                  pl.BlockSpec((2 * MOE_CHUNK,), lambda c, j, *_: (c,), memory_space=pltpu.SMEM),
                  pl.BlockSpec((2 * MOE_CHUNK,), lambda c, j, *_: (c,), memory_space=pltpu.SMEM),
                  pl.BlockSpec((None, D_MODEL, 2 * D_FF_EXPERT), expert_of),
                  pl.BlockSpec((None, D_FF_EXPERT, D_MODEL), expert_of),
                  _layer_block(g, l), _layer_block(b, l)],
        out_specs=_two_stream_specs(MOE_TOK_BLOCK, n_a_blocks, out_token_block),
        scratch_shapes=[pltpu.VMEM((MOE_TOK_BLOCK * SUBLANES, LANES), F32),
                        pltpu.VMEM((MOE_ROWS * SUBLANES, LANES), F32)],
    )
    return pl.pallas_call(
        functools.partial(_moe_kernel, n_a=n_a),
        grid_spec=grid_spec,
        out_shape=[jax.ShapeDtypeStruct(xa.shape, F32), jax.ShapeDtypeStruct(xb.shape, F32)],
        compiler_params=pltpu.CompilerParams(
            dimension_semantics=("arbitrary", "arbitrary"), vmem_limit_bytes=VMEM_LIMIT),
        name="moe_experts",
    )(start[:, :, 0].reshape(-1), padded[:, :, 0].reshape(-1), xa, xb, mods,
      dest.reshape(-1), gates.reshape(-1), w13, w2, g, b)


def _rope_tables(n_tokens):
    t = np.arange(n_tokens)
    row = (t // GRID_W).astype(np.float32)
    col = (t % GRID_W).astype(np.float32)
    inv_freq = (np.float32(ROPE_THETA) ** (-np.arange(0, AXIS_ROT, 2, dtype=np.float32) / AXIS_ROT)).astype(np.float32)
    ang_r = row[:, None] * inv_freq
    ang_c = col[:, None] * inv_freq
    cos = np.concatenate([np.cos(ang_r), np.cos(ang_r), np.cos(ang_c), np.cos(ang_c)], axis=1)
    sin = np.concatenate([-np.sin(ang_r), np.sin(ang_r), -np.sin(ang_c), np.sin(ang_c)], axis=1)
    return jnp.asarray(np.tile(cos, (1, 2)), F32), jnp.asarray(np.tile(sin, (1, 2)), F32)


def kernel(x_prompt, x_sample, cache_k, cache_v, c, c_ctx, ada_w, ada_b, w_in, q_norm_g, k_norm_g, conv_w, conv_b, sgu_norm_g, sgu_w, sgu_b, w_out, ln1_g, ln1_b, ln2_g, ln2_b, ffn_w1, ffn_w3, ffn_w2, router_w, moe_w1, moe_w3, moe_w2):
    batch, seq, _ = x_prompt.shape
    dec_batch, dec_seq, _ = x_sample.shape
    past_len = cache_k.shape[2]
    n_ctx = batch * seq
    n_lat = dec_batch * dec_seq
    assert DEPTH == 2 and 1 + dec_batch <= SUBLANES
    assert seq == ATT_BLOCK and CTX_SEQS_PER_STEP * seq == ROW_BLOCK and batch % CTX_SEQS_PER_STEP == 0
    assert dec_seq % ROW_BLOCK == 0 and dec_seq & (dec_seq - 1) == 0
    assert n_ctx % MOE_CHUNK == 0 and n_lat % MOE_CHUNK == 0
    assert dec_seq % ROUTER_BLOCK == 0 and dec_seq % MOE_TOK_BLOCK == 0 and dec_seq % FFN_TILE == 0

    cond = jnp.zeros((SUBLANES, D_MODEL), F32).at[0].set(c_ctx).at[1:1 + dec_batch].set(c)
    mod = _modulation(cond, ada_w, ada_b)
    mod = mod.reshape(DEPTH, SUBLANES, 6, D_MODEL)[:, :1 + dec_batch]
    mod = jnp.pad(mod, ((0, 0), (0, 0), (0, SUBLANES - 6), (0, 0)))

    lane_id = np.arange(GROUP_TILE) // HEAD_DIM
    ones_bd = jnp.asarray(lane_id[:, None] == lane_id[None, :], BF16)
    cos, sin = _rope_tables(dec_seq)
    small = (
        jnp.tile(q_norm_g, (1, N_Q_HEADS))[:, None, :], jnp.tile(k_norm_g, (1, N_KV_HEADS))[:, None, :],
        ones_bd,
        conv_w, conv_b[:, None, :], sgu_norm_g[:, None, :],
        jnp.swapaxes(sgu_w, 1, 2).reshape(DEPTH, CHUNK, SGU_HEADS * CHUNK).astype(BF16),
        jnp.repeat(jnp.swapaxes(sgu_b, 1, 2), HEAD_DIM, axis=2),
        ln1_g[:, None, :], ln1_b[:, None, :],
    )
    g2, b2 = ln2_g[:, None, :], ln2_b[:, None, :]
    kc = cache_k.reshape(dec_batch, DEPTH, past_len, KV_WIDTH)
    vc = cache_v.reshape(dec_batch, DEPTH, past_len, KV_WIDTH)

    xs = [x_prompt.reshape(n_ctx, D_MODEL), x_sample.reshape(n_lat, D_MODEL)]
    assert DEPTH == 2 and ffn_w1.shape[0] == 1 and moe_w1.shape[0] == 1
    side_casts = [[(ffn_w1[0],), (ffn_w3[0],), (ffn_w2[0],), ((w_in, 1),), ((w_out, 1),), (moe_w2.reshape(-1, D_MODEL),)],
                  [(moe_w1.reshape(-1, D_FF_EXPERT), moe_w3.reshape(-1, D_FF_EXPERT))]]
    kv = None
    for l in range(DEPTH):
        if l == 0:
            x_ctx, k_ctx, v_ctx, w_in_bf, w_out_bf, *ffn_bf, w_in_next, w_out_next, w2_bf = _mixer_ctx(
                xs[0], l, batch, seq, mod, w_in, w_out, small, kv, side_casts[l])
        else:
            w_in_bf, w_out_bf = w_in_next, w_out_next
            x_ctx, k_ctx, v_ctx, w13_bf = _mixer_ctx(
                xs[0], l, batch, seq, mod, w_in_bf, w_out_bf, small, kv, side_casts[l])
        kv = (k_ctx, v_ctx)
        x_lat = _mixer_lat(xs[1], l, dec_batch, dec_seq, mod, w_in_bf, w_out_bf, small, cos, sin, kc, vc)
        if l % 2 == 0:
            xs = _ffn_dense(x_ctx, x_lat, l, mod, *ffn_bf, g2, b2, dec_seq)
        else:
            ws = (w13_bf.reshape(N_EXPERTS, D_MODEL, 2 * D_FF_EXPERT), w2_bf.reshape(N_EXPERTS, D_FF_EXPERT, D_MODEL))
            xs = _ffn_moe(x_ctx, x_lat, l, mod, router_w[l // 2].T, *ws, g2, b2, dec_seq)
    y_p = xs[0].reshape(batch, seq, D_MODEL)
    y_s = xs[1].reshape(dec_batch, dec_seq, D_MODEL)
    new_k = kv[0].reshape(batch, DEPTH, seq, N_KV_HEADS, HEAD_DIM)
    new_v = kv[1].reshape(batch, DEPTH, seq, N_KV_HEADS, HEAD_DIM)
    return (y_p, y_s, new_k, new_v)
```

```python
import functools

import numpy as np
import jax
import jax.numpy as jnp
from jax import lax
from jax.experimental import pallas as pl
from jax.experimental.pallas import tpu as pltpu

F32 = jnp.float32
BF16 = jnp.bfloat16
I32 = jnp.int32

D_MODEL = 1024
DEPTH = 2
GRID_W = 64
HEAD_DIM = 64
N_Q_HEADS = 8
N_KV_HEADS = 2
ATTN_WIDTH = N_Q_HEADS * HEAD_DIM
KV_WIDTH = N_KV_HEADS * HEAD_DIM
ATTN_SCALE = HEAD_DIM ** -0.5
ROPE_THETA = 10000.0
AXIS_ROT = HEAD_DIM // 2
CONV_WIDTH = 256
SGU_WIDTH = 256
SGU_HEADS = 4
CHUNK = 128
IN_WIDTH = 2048
D_FF = 2816
N_EXPERTS = 8
D_FF_EXPERT = 1408
EPS = 1e-6
DEEPNORM_ALPHA = (2 * DEPTH) ** 0.25

LANES = 128
SUBLANES = 8
ROW_BLOCK = 512
GROUP_TILE = 256
ATT_BLOCK = 256
CTX_SEQS_PER_STEP = 2
FFN_TILE = 512
FFN_COLS = 256
MOE_CHUNK = 2048
ROUTER_BLOCK = 1024
MOE_GRAN = 128
MOE_BLOCK = 256
MOE_TOK_BLOCK = 512
MOE_SUB_BLOCK = 256
MOE_TOK_STEPS = MOE_CHUNK // MOE_TOK_BLOCK
MOE_STEPS = 2 * MOE_TOK_STEPS + N_EXPERTS
MOE_ROWS = 2 * MOE_CHUNK + N_EXPERTS * MOE_GRAN
VMEM_LIMIT = 60 * 1024 * 1024

_Q0, _K0, _V0, _CI0, _CB0, _CC0, _SU0, _SV0 = 0, 512, 640, 768, 1024, 1280, 1536, 1792
_QG, _KG, _CONVB, _SGUG, _LN1G, _LN1B, _LN2G, _LN2B = range(8)


def _dot(a, b):
    return jnp.dot(a, b, preferred_element_type=F32)


def _dot_nt(a, b):
    return lax.dot_general(a, b, (((1,), (1,)), ((), ())), preferred_element_type=F32)


def _split(x):
    hi = x.astype(BF16)
    lo = (x - hi.astype(F32)).astype(BF16)
    return hi, lo


def _group_sum(x, ones_bd):
    outs = []
    for c0 in range(0, x.shape[1], GROUP_TILE):
        width = min(GROUP_TILE, x.shape[1] - c0)
        outs.append(_dot(x[:, c0:c0 + width].astype(BF16), ones_bd[:width, :width]))
    return outs[0] if len(outs) == 1 else jnp.concatenate(outs, axis=1)


def _standardize(x):
    mu = jnp.mean(x, axis=-1, keepdims=True)
    d = x - mu
    return d * lax.rsqrt(jnp.mean(d * d, axis=-1, keepdims=True) + EPS)


def _silu(x):
    return x / (1.0 + jnp.exp(-x))


def _modulation_kernel(cond_ref, w_ref, b_ref, o_ref):
    s_hi, s_lo = _split(_silu(cond_ref[...]))
    w_hi, w_lo = _split(w_ref[...])
    o_ref[...] = _dot(s_hi, w_hi) + _dot(s_hi, w_lo) + _dot(s_lo, w_hi) + b_ref[...]


def _modulation(cond, ada_w, ada_b):
    n_out = ada_w.shape[-1]
    tn = 1536
    return pl.pallas_call(
        _modulation_kernel,
        grid=(DEPTH, n_out // tn),
        in_specs=[
            pl.BlockSpec((SUBLANES, D_MODEL), lambda l, j: (0, 0)),
            pl.BlockSpec((None, D_MODEL, tn), lambda l, j: (l, 0, j)),
            pl.BlockSpec((None, 1, tn), lambda l, j: (l, 0, j)),
        ],
        out_specs=pl.BlockSpec((None, SUBLANES, tn), lambda l, j: (l, 0, j)),
        out_shape=jax.ShapeDtypeStruct((DEPTH, SUBLANES, n_out), F32),
        compiler_params=pltpu.CompilerParams(
            dimension_semantics=("arbitrary", "arbitrary"), vmem_limit_bytes=VMEM_LIMIT),
        name="modulation",
    )(cond, ada_w, ada_b.reshape(DEPTH, 1, n_out))


def _rope(x, cos, sin_signed):
    w = x.shape[1]
    lane = lax.broadcasted_iota(I32, x.shape, 1)
    first_half = (lane & 31) < 16
    partner = jnp.where(first_half, pltpu.roll(x, w - 16, 1), pltpu.roll(x, 16, 1))
    return x * cos + partner * sin_signed


def _head_variants(x):
    lane = lax.broadcasted_iota(I32, x.shape, 1)
    lo = lane < HEAD_DIM
    xr = pltpu.roll(x, HEAD_DIM, 1)
    zero = jnp.zeros_like(x)
    return (jnp.where(lo, x, zero).astype(BF16), jnp.where(lo, zero, xr).astype(BF16),
            jnp.where(lo, xr, zero).astype(BF16), jnp.where(lo, zero, x).astype(BF16))


def _mixer_kernel(*refs, seq, n_seq, n_cache, rope, cast_weights, stack_kv, side_jobs=(), n_steps=None):
    refs = list(refs)
    skew = n_steps is not None

    def take(n):
        out, refs[:] = refs[:n], refs[n:]
        return out

    (x_ref,) = take(1)
    xres_ref = take(1)[0] if skew else x_ref
    mod_ref, win_ref, wout_ref = take(3)
    vec_ref, ones_ref, convw_ref, sguw_ref, sgub_ref = take(5)
    if rope:
        cos_ref, sin_ref, kc_ref, vc_ref = take(4)
    if stack_kv:
        kprev_ref, vprev_ref = take(2)
    side_in = [take(n_src) for n_src in side_jobs]
    (x1_ref,) = take(1)
    if not rope:
        k_ref, v_ref = take(2)
    if cast_weights:
        winb_ref, woutb_ref = take(2)

        @pl.when(pl.program_id(0) == 0)
        def _cast():
            winb_ref[...] = win_ref[...].astype(BF16)
            woutb_ref[...] = wout_ref[...].astype(BF16)
        win_ref, wout_ref = winb_ref, woutb_ref
    side_out = take(len(side_jobs))
    scratch = take(7)
    (mix_scr,) = take(1)
    n_rows = n_seq * seq
    assert n_cache == 0 or n_seq == 1

    def loop(n, body):
        if n == 1:
            body(0)
        else:
            def step(r, carry):
                body(r)
                return carry
            lax.fori_loop(0, n, step, 0, unroll=2)

    def block(r, size):
        if isinstance(r, int):
            return pl.ds(r * size, size)
        return pl.ds(pl.multiple_of(r * size, size), size)

    def project_phase(scr):
        q_scr, kvar_scr, vvar_scr, u_scr, cb_scr, su_scr, vn_scr = scr
        for srcs, dst in zip(side_in, side_out):
            col = 0
            for src in srcs:
                dst[:, col:col + src.shape[1]] = src[...].astype(BF16)
                col += src.shape[1]
        if n_cache:
            for i, var in enumerate(_head_variants(kc_ref[...])):
                kvar_scr[i, pl.ds(seq, n_cache), :] = var
            for i, var in enumerate(_head_variants(vc_ref[...])):
                vvar_scr[i, pl.ds(seq, n_cache), :] = var

        def project(r):
            rows = block(r, ROW_BLOCK)
            x = x_ref[rows, :]
            h = _standardize(x) * (1.0 + mod_ref[1:2, :]) + mod_ref[0:1, :]
            z = _dot(h.astype(BF16), win_ref[...])
            ones_bd = ones_ref[...]
            zq = z[:, _Q0:_K0]
            q = zq * lax.rsqrt(_group_sum(zq * zq, ones_bd) * (1.0 / HEAD_DIM) + EPS) * vec_ref[_QG:_QG + 1, :ATTN_WIDTH]
            zk = z[:, _K0:_V0]
            k = zk * lax.rsqrt(_group_sum(zk * zk, ones_bd) * (1.0 / HEAD_DIM) + EPS) * vec_ref[_KG:_KG + 1, :KV_WIDTH]
            v = z[:, _V0:_CI0]
            if rope:
                cos = cos_ref[rows, :]
                sin = sin_ref[rows, :]
                q = _rope(q, jnp.concatenate([cos] * 4, axis=1), jnp.concatenate([sin] * 4, axis=1))
                k = _rope(k, cos, sin)
            elif stack_kv:
                for s in range(ROW_BLOCK // seq):
                    sub = slice(s * seq, (s + 1) * seq)
                    k_ref[s, 0] = kprev_ref[sub, :]
                    v_ref[s, 0] = vprev_ref[sub, :]
                    k_ref[s, 1] = k[sub, :]
                    v_ref[s, 1] = v[sub, :]
            else:
                k_ref[rows, :] = k
                v_ref[rows, :] = v
            q_scr[rows, :] = (q * ATTN_SCALE).astype(BF16)
            for i, var in enumerate(_head_variants(k)):
                kvar_scr[i, rows, :] = var
            for i, var in enumerate(_head_variants(v)):
                vvar_scr[i, rows, :] = var
            u_scr[rows, :] = z[:, _CC0:_SU0] * z[:, _CI0:_CB0]
            cb_scr[rows, :] = z[:, _CB0:_CC0]
            su_scr[rows, :] = z[:, _SU0:_SV0]
            sv = z[:, _SV0:IN_WIDTH]
            d = sv - _group_sum(sv, ones_bd) * (1.0 / HEAD_DIM)
            vn = d * lax.rsqrt(_group_sum(d * d, ones_bd) * (1.0 / HEAD_DIM) + EPS) * vec_ref[_SGUG:_SGUG + 1, :SGU_WIDTH]
            vn_scr[rows, :] = vn.astype(BF16)

        loop(n_rows // ROW_BLOCK, project)

    def consume_phase(scr):
        q_scr, kvar_scr, vvar_scr, u_scr, cb_scr, su_scr, vn_scr = scr
        u = u_scr[...]
        pos = lax.broadcasted_iota(I32, u.shape, 0) & (seq - 1)
        up = jnp.where(pos == 0, 0.0, pltpu.roll(u, 1, 0))
        dn = jnp.where(pos == seq - 1, 0.0, pltpu.roll(u, n_rows - 1, 0))
        conv = up * convw_ref[0:1, :] + u * convw_ref[1:2, :] + dn * convw_ref[2:3, :] + vec_ref[_CONVB:_CONVB + 1, :CONV_WIDTH]
        mix_scr[:, ATTN_WIDTH:ATTN_WIDTH + CONV_WIDTH] = (cb_scr[...] * conv).astype(BF16)

        for n in range(n_rows // CHUNK):
            rows = pl.ds(n * CHUNK, CHUNK)
            vn = vn_scr[rows, :]
            lane = lax.broadcasted_iota(I32, vn.shape, 1)
            per_head = [jnp.where((lane >= hd * HEAD_DIM) & (lane < (hd + 1) * HEAD_DIM), vn, jnp.zeros_like(vn))
                        for hd in range(SGU_HEADS)]
            s = sgub_ref[...] + _dot(sguw_ref[...], jnp.concatenate(per_head, axis=0))
            mix_scr[rows, ATTN_WIDTH + CONV_WIDTH:] = (su_scr[rows, :] * s).astype(BF16)

        def attend(s, r):
            rows = block(s * (seq // ATT_BLOCK) + r, ATT_BLOCK)
            keys = pl.ds(s * seq, seq + n_cache)
            for pair in range(N_Q_HEADS // 2):
                qp = q_scr[rows, pair * LANES:(pair + 1) * LANES]
                kv = pair // (N_Q_HEADS // N_KV_HEADS // 2)
                acc = jnp.zeros((ATT_BLOCK, LANES), F32)
                for parity in range(2):
                    sc = _dot_nt(qp, kvar_scr[2 * kv + parity, keys, :])
                    p = jnp.exp(sc - jnp.max(sc, axis=1, keepdims=True))
                    denom = jnp.sum(p, axis=1, keepdims=True)
                    acc = acc + _dot(p.astype(BF16), vvar_scr[2 * kv + parity, keys, :]) / denom
                mix_scr[rows, pair * LANES:(pair + 1) * LANES] = acc.astype(BF16)

        for s in range(n_seq):
            loop(seq // ATT_BLOCK, functools.partial(attend, s))

        def finish(r):
            rows = block(r, ROW_BLOCK)
            mix = _dot(mix_scr[rows, :], wout_ref[...])
            y = DEEPNORM_ALPHA * xres_ref[rows, :] + mod_ref[2:3, :] * mix
            x1_ref[rows, :] = _standardize(y) * vec_ref[_LN1G:_LN1G + 1, :] + vec_ref[_LN1B:_LN1B + 1, :]

        loop(n_rows // ROW_BLOCK, finish)

    if not skew:
        project_phase(scratch)
        consume_phase(scratch)
        return

    i = pl.program_id(0)
    slot = lax.rem(i, 2)
    mine = [ref.at[slot] for ref in scratch]
    other = [ref.at[1 - slot] for ref in scratch]
    pl.when(i == 0)(functools.partial(project_phase, mine))

    @pl.when((i > 0) & (i < n_steps))
    def _():
        consume_phase(other)
        project_phase(mine)

    pl.when(i == n_steps)(functools.partial(consume_phase, other))


def _full(shape):
    n = len(shape)
    return pl.BlockSpec(shape, lambda *_: (0,) * n)


def _resident(shape):
    n = len(shape)
    return pl.BlockSpec(shape, lambda *_: (0,) * n, pipeline_mode=pl.Buffered(1))


def _layer_block(arr, l, resident=False):
    shape = arr.shape[1:]
    kw = dict(pipeline_mode=pl.Buffered(1)) if resident else {}
    return pl.BlockSpec((None,) + shape, lambda *_: (l,) + (0,) * len(shape), **kw)


def _mixer_scratch(n_rows, n_cache, slots=()):
    nk = n_rows + n_cache
    return [
        pltpu.VMEM(slots + (n_rows, ATTN_WIDTH), BF16),
        pltpu.VMEM(slots + (4, nk, LANES), BF16),
        pltpu.VMEM(slots + (4, nk, LANES), BF16),
        pltpu.VMEM(slots + (n_rows, CONV_WIDTH), F32),
        pltpu.VMEM(slots + (n_rows, CONV_WIDTH), F32),
        pltpu.VMEM(slots + (n_rows, SGU_WIDTH), F32),
        pltpu.VMEM(slots + (n_rows, SGU_WIDTH), BF16),
        pltpu.VMEM((n_rows, D_MODEL), BF16),
    ]


def _side_source(src):
    return src if isinstance(src, tuple) else (src, None)


def _mixer_ctx(x, l, n_seq, seq, mod, w_in, w_out, small, kv_prev=None, side_casts=()):
    per_step = CTX_SEQS_PER_STEP
    rows = per_step * seq
    stack_kv = kv_prev is not None
    cast_weights = w_in.dtype != BF16
    n_steps = n_seq // per_step
    kernel = functools.partial(_mixer_kernel, seq=seq, n_seq=per_step, n_cache=0, rope=False,
                               cast_weights=cast_weights, stack_kv=stack_kv,
                               side_jobs=tuple(len(job) for job in side_casts), n_steps=n_steps)
    cur = lambda i: (jnp.minimum(i, n_steps - 1), 0)
    prev = lambda i: (jnp.maximum(i - 1, 0), 0)
    w_specs = ([_layer_block(w_in, l, resident=True), _layer_block(w_out, l, resident=True)] if cast_weights
               else [_resident(w_in.shape), _resident(w_out.shape)])
    in_specs = ([pl.BlockSpec((rows, D_MODEL), cur), pl.BlockSpec((rows, D_MODEL), prev),
                 pl.BlockSpec((None, None, SUBLANES, D_MODEL), lambda i: (l, 0, 0, 0))] + w_specs
                + [_full(a.shape) if a.ndim == 2 else _layer_block(a, l) for a in small])
    args = [x, x, mod, w_in, w_out, *small]
    if stack_kv:
        in_specs += [pl.BlockSpec((rows, KV_WIDTH), cur)] * 2
        args += list(kv_prev)
        kv_spec = pl.BlockSpec((per_step, DEPTH, seq, KV_WIDTH), lambda i: cur(i) + (0, 0))
        kv_shape = jax.ShapeDtypeStruct((n_seq, DEPTH, seq, KV_WIDTH), F32)
    else:
        kv_spec = pl.BlockSpec((rows, KV_WIDTH), cur)
        kv_shape = jax.ShapeDtypeStruct((n_seq * seq, KV_WIDTH), F32)
    side_out_specs, side_out_shapes = [], []
    for job in side_casts:
        sources = [_side_source(src) for src in job]
        n_rows = sources[0][0].shape[-2]
        n_cols = sum(arr.shape[-1] for arr, _ in sources)
        for arr, layer in sources:
            blk = (n_rows // n_steps, arr.shape[-1])
            if layer is None:
                in_specs.append(pl.BlockSpec(blk, cur))
            else:
                in_specs.append(pl.BlockSpec((None,) + blk, lambda i, layer=layer: (layer,) + cur(i)))
            args.append(arr)
        side_out_specs.append(pl.BlockSpec((n_rows // n_steps, n_cols), cur))
        side_out_shapes.append(jax.ShapeDtypeStruct((n_rows, n_cols), BF16))
    w_out_specs, w_out_shapes = [], []
    if cast_weights:
        w_out_specs = [_full((D_MODEL, IN_WIDTH)), _full((D_MODEL, D_MODEL))]
        w_out_shapes = [jax.ShapeDtypeStruct((D_MODEL, IN_WIDTH), BF16), jax.ShapeDtypeStruct((D_MODEL, D_MODEL), BF16)]
    return pl.pallas_call(
        kernel,
        grid=(n_steps + 1,),
        in_specs=in_specs,
        out_specs=[pl.BlockSpec((rows, D_MODEL), prev), kv_spec, kv_spec] + w_out_specs + side_out_specs,
        out_shape=[jax.ShapeDtypeStruct((n_seq * seq, D_MODEL), F32), kv_shape, kv_shape] + w_out_shapes
        + side_out_shapes,
        scratch_shapes=_mixer_scratch(rows, 0, slots=(2,)),
        compiler_params=pltpu.CompilerParams(dimension_semantics=("arbitrary",), vmem_limit_bytes=VMEM_LIMIT),
        name="mixer_ctx",
    )(*args)


def _mixer_lat(x, l, n_seq, seq, mod, w_in_bf, w_out_bf, small, cos, sin, kc, vc):
    n_cache = kc.shape[2]
    kernel = functools.partial(_mixer_kernel, seq=seq, n_seq=1, n_cache=n_cache, rope=True,
                               cast_weights=False, stack_kv=False)
    cache_spec = pl.BlockSpec((None, None, n_cache, KV_WIDTH), lambda b: (b, l, 0, 0))
    return pl.pallas_call(
        kernel,
        grid=(n_seq,),
        in_specs=([pl.BlockSpec((seq, D_MODEL), lambda b: (b, 0)),
                   pl.BlockSpec((None, None, SUBLANES, D_MODEL), lambda b: (l, 1 + b, 0, 0)),
                   _resident((D_MODEL, IN_WIDTH)), _resident((D_MODEL, D_MODEL))]
                  + [_full(a.shape) if a.ndim == 2 else _layer_block(a, l) for a in small]
                  + [_full((seq, LANES)), _full((seq, LANES)), cache_spec, cache_spec]),
        out_specs=pl.BlockSpec((seq, D_MODEL), lambda b: (b, 0)),
        out_shape=jax.ShapeDtypeStruct((n_seq * seq, D_MODEL), F32),
        scratch_shapes=_mixer_scratch(seq, n_cache),
        compiler_params=pltpu.CompilerParams(dimension_semantics=("arbitrary",), vmem_limit_bytes=VMEM_LIMIT),
        name="mixer_lat",
    )(x, mod, w_in_bf, w_out_bf, *small, cos, sin, kc, vc)


def _two_stream_specs(block_rows, n_a, block_of):
    spec_a = pl.BlockSpec((block_rows, D_MODEL), lambda *ids: (jnp.minimum(block_of(*ids), n_a - 1), 0))
    spec_b = pl.BlockSpec((block_rows, D_MODEL), lambda *ids: (jnp.maximum(block_of(*ids) - n_a, 0), 0))
    return [spec_a, spec_b]


def _group_of_block(blk, block_rows, n_a, lat_seq):
    return jnp.where(blk < n_a, 0, 1 + (blk - n_a) // (lat_seq // block_rows))


def _ffn_kernel(xa_ref, xb_ref, mod_ref, w1_ref, w3_ref, w2_ref, vec_ref, oa_ref, ob_ref, *, n_a):
    is_a = pl.program_id(0) < n_a
    x = jnp.where(is_a, xa_ref[...], xb_ref[...])
    h = (_standardize(x) * (1.0 + mod_ref[4:5, :]) + mod_ref[3:4, :]).astype(BF16)
    acc = jnp.zeros(x.shape, F32)
    for c in range(D_FF // FFN_COLS):
        cols = slice(c * FFN_COLS, (c + 1) * FFN_COLS)
        a = _dot(h, w1_ref[:, cols])
        b = _dot(h, w3_ref[:, cols])
        acc = acc + _dot((_silu(a) * b).astype(BF16), w2_ref[cols, :])
    y = DEEPNORM_ALPHA * x + mod_ref[5:6, :] * acc
    out = _standardize(y) * vec_ref[_LN2G:_LN2G + 1, :] + vec_ref[_LN2B:_LN2B + 1, :]

    @pl.when(is_a)
    def _():
        oa_ref[...] = out

    @pl.when(jnp.logical_not(is_a))
    def _():
        ob_ref[...] = out


def _ffn_dense(xa, xb, l, mods, w1, w3, w2, vecs, lat_seq):
    n_a, n_b = xa.shape[0] // FFN_TILE, xb.shape[0] // FFN_TILE
    x_specs = _two_stream_specs(FFN_TILE, n_a, lambda t: t)
    return pl.pallas_call(
        functools.partial(_ffn_kernel, n_a=n_a),
        grid=(n_a + n_b,),
        in_specs=x_specs + [
            pl.BlockSpec((None, None, SUBLANES, D_MODEL),
                         lambda t: (l, _group_of_block(t, FFN_TILE, n_a, lat_seq), 0, 0)),
            _resident(w1.shape), _resident(w3.shape), _resident(w2.shape), _layer_block(vecs, l)],
        out_specs=x_specs,
        out_shape=[jax.ShapeDtypeStruct(xa.shape, F32), jax.ShapeDtypeStruct(xb.shape, F32)],
        compiler_params=pltpu.CompilerParams(dimension_semantics=("arbitrary",), vmem_limit_bytes=VMEM_LIMIT),
        name="ffn_dense",
    )(xa, xb, mods, w1, w3, w2, vecs)


def _router_kernel(xa_ref, xb_ref, mod_ref, rw_ref, before_ref, dest_ref, gate_ref, start_ref, padded_ref, *,
                   n_a, lat_seq):
    c = pl.program_id(0)
    w_hi, w_lo = _split(rw_ref[...])
    n_blocks = MOE_CHUNK // ROUTER_BLOCK
    parts = []
    for blk in range(n_blocks):
        rows = pl.ds(blk * ROUTER_BLOCK, ROUTER_BLOCK)
        mod = mod_ref[_group_of_block(c * n_blocks + blk, ROUTER_BLOCK, n_a * n_blocks, lat_seq)]
        x = jnp.where(c < n_a, xa_ref[rows, :], xb_ref[rows, :])
        h = _standardize(x) * (1.0 + mod[4:5, :]) + mod[3:4, :]
        h_hi, h_lo = _split(h)
        parts.append(_dot_nt(w_hi, h_hi) + _dot_nt(w_hi, h_lo) + _dot_nt(w_lo, h_hi))
    logits = jnp.concatenate(parts, axis=1)
    eid = lax.broadcasted_iota(I32, logits.shape, 0).astype(F32)
    m1 = jnp.max(logits, axis=0, keepdims=True)
    i1 = jnp.min(jnp.where(logits == m1, eid, float(N_EXPERTS)), axis=0, keepdims=True)
    oh1 = eid == i1
    rest = jnp.where(oh1, -jnp.inf, logits)
    m2 = jnp.max(rest, axis=0, keepdims=True)
    i2 = jnp.min(jnp.where(rest == m2, eid, float(N_EXPERTS)), axis=0, keepdims=True)
    oh2 = eid == i2
    e = jnp.exp(m2 - m1)
    gate_ref[0:1, :] = 1.0 / (1.0 + e)
    gate_ref[1:2, :] = e / (1.0 + e)
    sel = jnp.where(oh1 | oh2, 1.0, 0.0)
    ranks = []
    seen = jnp.zeros((N_EXPERTS, 1), F32)
    for blk in range(n_blocks):
        s_blk = sel[:, blk * ROUTER_BLOCK:(blk + 1) * ROUTER_BLOCK]
        ranks.append(_dot(s_blk.astype(BF16), before_ref[...]) + seen)
        seen = seen + jnp.sum(s_blk, axis=1, keepdims=True)
    rank = jnp.concatenate(ranks, axis=1)
    eid_out = lax.broadcasted_iota(I32, start_ref.shape, 0).astype(F32)
    start = jnp.zeros(sel.shape, F32)
    start_out = jnp.zeros(start_ref.shape, F32)
    padded_out = jnp.zeros(start_ref.shape, F32)
    for ex in range(N_EXPERTS):
        cnt = jnp.sum(sel[ex:ex + 1, :], axis=1, keepdims=True)
        padded = jnp.ceil(cnt * (1.0 / MOE_GRAN)) * MOE_GRAN
        start = start + jnp.where(eid > ex, padded, 0.0)
        start_out = start_out + jnp.where(eid_out > ex, padded, 0.0)
        padded_out = padded_out + jnp.where(eid_out == ex, padded, 0.0)
    row = (start + rank) * SUBLANES
    dest_ref[0:1, :] = jnp.sum(jnp.where(oh1, row, 0.0), axis=0, keepdims=True).astype(I32)
    dest_ref[1:2, :] = jnp.sum(jnp.where(oh2, row, 0.0), axis=0, keepdims=True).astype(I32)
    start_ref[...] = start_out.astype(I32)
    padded_ref[...] = padded_out.astype(I32)


def _router(xa, xb, l, mods, rw_t, lat_seq):
    n_a = xa.shape[0] // MOE_CHUNK
    n_chunks = n_a + xb.shape[0] // MOE_CHUNK
    tok = np.arange(ROUTER_BLOCK)
    before = jnp.asarray(tok[:, None] < tok[None, :], BF16)
    return pl.pallas_call(
        functools.partial(_router_kernel, n_a=n_a, lat_seq=lat_seq),
        grid=(n_chunks,),
        in_specs=_two_stream_specs(MOE_CHUNK, n_a, lambda c: c) + [
                  _layer_block(mods, l),
                  _full((N_EXPERTS, D_MODEL)), _full((ROUTER_BLOCK, ROUTER_BLOCK))],
        out_specs=[pl.BlockSpec((None, 2, MOE_CHUNK), lambda c: (c, 0, 0)),
                   pl.BlockSpec((None, 2, MOE_CHUNK), lambda c: (c, 0, 0)),
                   pl.BlockSpec((None, N_EXPERTS, LANES), lambda c: (c, 0, 0)),
                   pl.BlockSpec((None, N_EXPERTS, LANES), lambda c: (c, 0, 0))],
        out_shape=[jax.ShapeDtypeStruct((n_chunks, 2, MOE_CHUNK), I32),
                   jax.ShapeDtypeStruct((n_chunks, 2, MOE_CHUNK), F32),
                   jax.ShapeDtypeStruct((n_chunks, N_EXPERTS, LANES), I32),
                   jax.ShapeDtypeStruct((n_chunks, N_EXPERTS, LANES), I32)],
        compiler_params=pltpu.CompilerParams(dimension_semantics=("arbitrary",), vmem_limit_bytes=VMEM_LIMIT),
        name="moe_router",
    )(xa, xb, mods, rw_t, before)


def _token_rows(first_row):
    return pl.ds(pl.multiple_of(first_row, SUBLANES), SUBLANES)


def _store_token_major(ref, tok0, val):
    for cc in range(D_MODEL // LANES):
        ref[pl.ds(tok0 * SUBLANES + cc, val.shape[0], stride=SUBLANES), :] = val[:, cc * LANES:(cc + 1) * LANES]


def _load_token_major(ref, tok0, n):
    return jnp.concatenate(
        [ref[pl.ds(tok0 * SUBLANES + cc, n, stride=SUBLANES), :] for cc in range(D_MODEL // LANES)], axis=1)


def _moe_kernel(start_ref, padded_ref, xa_ref, xb_ref, mod_ref, dest_ref, gate_ref, w13_ref, w2_ref,
                vec_ref, oa_ref, ob_ref, tok_scr, rows_scr, *, n_a):
    c = pl.program_id(0)
    j = pl.program_id(1)
    is_a = c < n_a
    first_expert_step = MOE_TOK_STEPS
    first_combine_step = MOE_TOK_STEPS + N_EXPERTS

    @pl.when((c == 0) & (j == 0))
    def _init():
        rows_scr[...] = jnp.zeros(rows_scr.shape, F32)

    @pl.when(j < first_expert_step)
    def _dispatch():
        t0 = j * MOE_TOK_BLOCK
        for r in range(MOE_TOK_BLOCK // MOE_SUB_BLOCK):
            rows = pl.ds(r * MOE_SUB_BLOCK, MOE_SUB_BLOCK)
            x = jnp.where(is_a, xa_ref[rows, :], xb_ref[rows, :])
            h = _standardize(x) * (1.0 + mod_ref[4:5, :]) + mod_ref[3:4, :]
            _store_token_major(tok_scr, r * MOE_SUB_BLOCK, h)
            for t in range(r * MOE_SUB_BLOCK, (r + 1) * MOE_SUB_BLOCK):
                row = tok_scr[pl.ds(t * SUBLANES, SUBLANES), :]
                rows_scr[_token_rows(dest_ref[t0 + t]), :] = row
                rows_scr[_token_rows(dest_ref[MOE_CHUNK + t0 + t]), :] = row

    def experts(row0, m):
        xin = _load_token_major(rows_scr, row0, m).astype(BF16)
        ab = _dot(xin, w13_ref[...])
        a, b = ab[:, :D_FF_EXPERT], ab[:, D_FF_EXPERT:]
        y = _dot((_silu(a) * b).astype(BF16), w2_ref[...])
        _store_token_major(rows_scr, row0, y)

    @pl.when((j >= first_expert_step) & (j < first_combine_step))
    def _experts():
        region = c * N_EXPERTS + (j - first_expert_step)
        start = start_ref[region]
        padded = padded_ref[region]
        n_full = lax.shift_right_logical(padded, MOE_BLOCK.bit_length() - 1)

        def body(i, carry):
            experts(start + 2 * i * MOE_BLOCK, MOE_BLOCK)
            experts(start + (2 * i + 1) * MOE_BLOCK, MOE_BLOCK)
            return carry
        lax.fori_loop(0, lax.shift_right_logical(n_full, 1), body, 0)
        pl.when((n_full & 1) == 1)(functools.partial(experts, start + (n_full - 1) * MOE_BLOCK, MOE_BLOCK))
        for m in range(MOE_GRAN, MOE_BLOCK, MOE_GRAN):
            pl.when(padded - n_full * MOE_BLOCK == m)(
                functools.partial(experts, start + n_full * MOE_BLOCK, m))

    @pl.when(j >= first_combine_step)
    def _combine():
        t0 = (j - first_combine_step) * MOE_TOK_BLOCK
        outs = []
        for r in range(MOE_TOK_BLOCK // MOE_SUB_BLOCK):
            for t in range(r * MOE_SUB_BLOCK, (r + 1) * MOE_SUB_BLOCK):
                y0 = rows_scr[_token_rows(dest_ref[t0 + t]), :]
                y1 = rows_scr[_token_rows(dest_ref[MOE_CHUNK + t0 + t]), :]
                tok_scr[pl.ds(t * SUBLANES, SUBLANES), :] = (gate_ref[t0 + t] * y0
                                                             + gate_ref[MOE_CHUNK + t0 + t] * y1)
            rows = pl.ds(r * MOE_SUB_BLOCK, MOE_SUB_BLOCK)
            ffn = _load_token_major(tok_scr, r * MOE_SUB_BLOCK, MOE_SUB_BLOCK)
            x = jnp.where(is_a, xa_ref[rows, :], xb_ref[rows, :])
            y = DEEPNORM_ALPHA * x + mod_ref[5:6, :] * ffn
            outs.append(_standardize(y) * vec_ref[_LN2G:_LN2G + 1, :] + vec_ref[_LN2B:_LN2B + 1, :])

        @pl.when(is_a)
        def _():
            for r, out in enumerate(outs):
                oa_ref[pl.ds(r * MOE_SUB_BLOCK, MOE_SUB_BLOCK), :] = out

        @pl.when(jnp.logical_not(is_a))
        def _():
            for r, out in enumerate(outs):
                ob_ref[pl.ds(r * MOE_SUB_BLOCK, MOE_SUB_BLOCK), :] = out


def _ffn_moe(xa, xb, l, mods, rw_t, w13, w2, vecs, lat_seq):
    n_a = xa.shape[0] // MOE_CHUNK
    n_chunks = n_a + xb.shape[0] // MOE_CHUNK
    n_a_blocks = n_a * MOE_TOK_STEPS
    dest, gates, start, padded = _router(xa, xb, l, mods, rw_t, lat_seq)
    first_expert_step = MOE_TOK_STEPS
    first_combine_step = MOE_TOK_STEPS + N_EXPERTS

    def token_block(c, j, *_):
        blk = jnp.where(j < first_combine_step, jnp.minimum(j, MOE_TOK_STEPS - 1), j - first_combine_step)
        return c * MOE_TOK_STEPS + blk

    def out_token_block(c, j, *_):
        return c * MOE_TOK_STEPS + jnp.maximum(j - first_combine_step, 0)

    def mod_of(c, j, *_):
        return (l, _group_of_block(token_block(c, j), MOE_TOK_BLOCK, n_a_blocks, lat_seq), 0, 0)

    def expert_of(c, j, *_):
        return (jnp.clip(j - first_expert_step, 0, N_EXPERTS - 1), 0, 0)

    grid_spec = pltpu.PrefetchScalarGridSpec(
        num_scalar_prefetch=2,
        grid=(n_chunks, MOE_STEPS),
        in_specs=_two_stream_specs(MOE_TOK_BLOCK, n_a_blocks, token_block) + [
                  pl.BlockSpec((None, None, SUBLANES, D_MODEL), mod_of),
                  pl.BlockSpec((2 * MOE_CHUNK,), lambda c, j, *_: (c,), memory_space=pltpu.SMEM),
                  pl.BlockSpec((2 * MOE_CHUNK,), lambda c, j, *_: (c,), memory_space=pltpu.SMEM),
                  pl.BlockSpec((None, D_MODEL, 2 * D_FF_EXPERT), expert_of),
                  pl.BlockSpec((None, D_FF_EXPERT, D_MODEL), expert_of),
                  _layer_block(vecs, l)],
        out_specs=_two_stream_specs(MOE_TOK_BLOCK, n_a_blocks, out_token_block),
        scratch_shapes=[pltpu.VMEM((MOE_TOK_BLOCK * SUBLANES, LANES), F32),
                        pltpu.VMEM((MOE_ROWS * SUBLANES, LANES), F32)],
    )
    return pl.pallas_call(
        functools.partial(_moe_kernel, n_a=n_a),
        grid_spec=grid_spec,
        out_shape=[jax.ShapeDtypeStruct(xa.shape, F32), jax.ShapeDtypeStruct(xb.shape, F32)],
        compiler_params=pltpu.CompilerParams(
            dimension_semantics=("arbitrary", "arbitrary"), vmem_limit_bytes=VMEM_LIMIT),
        name="moe_experts",
    )(start[:, :, 0].reshape(-1), padded[:, :, 0].reshape(-1), xa, xb, mods,
      dest.reshape(-1), gates.reshape(-1), w13, w2, vecs)


def _rope_tables(n_tokens):
    t = np.arange(n_tokens)
    row = (t // GRID_W).astype(np.float32)
    col = (t % GRID_W).astype(np.float32)
    inv_freq = (np.float32(ROPE_THETA) ** (-np.arange(0, AXIS_ROT, 2, dtype=np.float32) / AXIS_ROT)).astype(np.float32)
    ang_r = row[:, None] * inv_freq
    ang_c = col[:, None] * inv_freq
    cos = np.concatenate([np.cos(ang_r), np.cos(ang_r), np.cos(ang_c), np.cos(ang_c)], axis=1)
    sin = np.concatenate([-np.sin(ang_r), np.sin(ang_r), -np.sin(ang_c), np.sin(ang_c)], axis=1)
    return jnp.asarray(np.tile(cos, (1, 2)), F32), jnp.asarray(np.tile(sin, (1, 2)), F32)


def kernel(x_prompt, x_sample, cache_k, cache_v, c, c_ctx, ada_w, ada_b, w_in, q_norm_g, k_norm_g, conv_w, conv_b, sgu_norm_g, sgu_w, sgu_b, w_out, ln1_g, ln1_b, ln2_g, ln2_b, ffn_w1, ffn_w3, ffn_w2, router_w, moe_w1, moe_w3, moe_w2):
    batch, seq, _ = x_prompt.shape
    dec_batch, dec_seq, _ = x_sample.shape
    past_len = cache_k.shape[2]
    n_ctx = batch * seq
    n_lat = dec_batch * dec_seq
    assert DEPTH == 2 and 1 + dec_batch <= SUBLANES
    assert seq == ATT_BLOCK and CTX_SEQS_PER_STEP * seq == ROW_BLOCK and batch % CTX_SEQS_PER_STEP == 0
    assert dec_seq % ROW_BLOCK == 0 and dec_seq & (dec_seq - 1) == 0
    assert n_ctx % MOE_CHUNK == 0 and n_lat % MOE_CHUNK == 0
    assert dec_seq % ROUTER_BLOCK == 0 and dec_seq % MOE_TOK_BLOCK == 0 and dec_seq % FFN_TILE == 0

    cond = jnp.concatenate([c_ctx[None, :], c, jnp.zeros((SUBLANES - 1 - dec_batch, D_MODEL), F32)], axis=0)
    mod = _modulation(cond, ada_w, ada_b)
    mod = mod.reshape(DEPTH, SUBLANES, 6, D_MODEL)[:, :1 + dec_batch]
    mod = jnp.pad(mod, ((0, 0), (0, 0), (0, SUBLANES - 6), (0, 0)))

    lane_id = np.arange(GROUP_TILE) // HEAD_DIM
    ones_bd = jnp.asarray(lane_id[:, None] == lane_id[None, :], BF16)
    cos, sin = _rope_tables(dec_seq)
    vec_rows = {_QG: jnp.tile(q_norm_g, (1, N_Q_HEADS)), _KG: jnp.tile(k_norm_g, (1, N_KV_HEADS)), _CONVB: conv_b,
                _SGUG: sgu_norm_g, _LN1G: ln1_g, _LN1B: ln1_b, _LN2G: ln2_g, _LN2B: ln2_b}
    vecs = jnp.stack([jnp.pad(vec_rows[i], ((0, 0), (0, D_MODEL - vec_rows[i].shape[1])))
                      for i in range(len(vec_rows))], axis=1)
    small = (
        vecs, ones_bd, conv_w,
        jnp.swapaxes(sgu_w, 1, 2).reshape(DEPTH, CHUNK, SGU_HEADS * CHUNK).astype(BF16),
        jnp.repeat(jnp.swapaxes(sgu_b, 1, 2), HEAD_DIM, axis=2),
    )
    kc = cache_k.reshape(dec_batch, DEPTH, past_len, KV_WIDTH)
    vc = cache_v.reshape(dec_batch, DEPTH, past_len, KV_WIDTH)

    xs = [x_prompt.reshape(n_ctx, D_MODEL), x_sample.reshape(n_lat, D_MODEL)]
    assert DEPTH == 2 and ffn_w1.shape[0] == 1 and moe_w1.shape[0] == 1
    side_casts = [[(ffn_w1[0],), (ffn_w3[0],), (ffn_w2[0],), ((w_in, 1),), ((w_out, 1),), (moe_w2.reshape(-1, D_MODEL),)],
                  [(moe_w1.reshape(-1, D_FF_EXPERT), moe_w3.reshape(-1, D_FF_EXPERT))]]
    kv = None
    for l in range(DEPTH):
        if l == 0:
            x_ctx, k_ctx, v_ctx, w_in_bf, w_out_bf, *ffn_bf, w_in_next, w_out_next, w2_bf = _mixer_ctx(
                xs[0], l, batch, seq, mod, w_in, w_out, small, kv, side_casts[l])
        else:
            w_in_bf, w_out_bf = w_in_next, w_out_next
            x_ctx, k_ctx, v_ctx, w13_bf = _mixer_ctx(
                xs[0], l, batch, seq, mod, w_in_bf, w_out_bf, small, kv, side_casts[l])
        kv = (k_ctx, v_ctx)
        x_lat = _mixer_lat(xs[1], l, dec_batch, dec_seq, mod, w_in_bf, w_out_bf, small, cos, sin, kc, vc)
        if l % 2 == 0:
            xs = _ffn_dense(x_ctx, x_lat, l, mod, *ffn_bf, vecs, dec_seq)
        else:
            ws = (w13_bf.reshape(N_EXPERTS, D_MODEL, 2 * D_FF_EXPERT), w2_bf.reshape(N_EXPERTS, D_FF_EXPERT, D_MODEL))
            xs = _ffn_moe(x_ctx, x_lat, l, mod, router_w[l // 2].T, *ws, vecs, dec_seq)
    y_p = xs[0].reshape(batch, seq, D_MODEL)
    y_s = xs[1].reshape(dec_batch, dec_seq, D_MODEL)
    new_k = kv[0].reshape(batch, DEPTH, seq, N_KV_HEADS, HEAD_DIM)
    new_v = kv[1].reshape(batch, DEPTH, seq, N_KV_HEADS, HEAD_DIM)
    return (y_p, y_s, new_k, new_v)
```

```python
import functools

import numpy as np
import jax
import jax.numpy as jnp
from jax import lax
from jax.experimental import pallas as pl
from jax.experimental.pallas import tpu as pltpu

F32 = jnp.float32
BF16 = jnp.bfloat16
I32 = jnp.int32

D_MODEL = 1024
DEPTH = 2
GRID_W = 64
HEAD_DIM = 64
N_Q_HEADS = 8
N_KV_HEADS = 2
ATTN_WIDTH = N_Q_HEADS * HEAD_DIM
KV_WIDTH = N_KV_HEADS * HEAD_DIM
ATTN_SCALE = HEAD_DIM ** -0.5
ROPE_THETA = 10000.0
AXIS_ROT = HEAD_DIM // 2
CONV_WIDTH = 256
SGU_WIDTH = 256
SGU_HEADS = 4
CHUNK = 128
IN_WIDTH = 2048
D_FF = 2816
N_EXPERTS = 8
D_FF_EXPERT = 1408
EPS = 1e-6
DEEPNORM_ALPHA = (2 * DEPTH) ** 0.25

LANES = 128
SUBLANES = 8
ROW_BLOCK = 512
GROUP_TILE = 256
ATT_BLOCK = 256
CTX_SEQS_PER_STEP = 2
FFN_TILE = 512
FFN_COLS = 256
MOE_CHUNK = 2048
ROUTER_BLOCK = 1024
MOE_GRAN = 128
MOE_BLOCK = 256
MOE_TOK_BLOCK = 512
MOE_SUB_BLOCK = 256
MOE_TOK_STEPS = MOE_CHUNK // MOE_TOK_BLOCK
MOE_STEPS = 2 * MOE_TOK_STEPS + N_EXPERTS
MOE_ROWS = 2 * MOE_CHUNK + N_EXPERTS * MOE_GRAN
VMEM_LIMIT = 60 * 1024 * 1024

_Q0, _K0, _V0, _CI0, _CB0, _CC0, _SU0, _SV0 = 0, 512, 640, 768, 1024, 1280, 1536, 1792


def _dot(a, b):
    return jnp.dot(a, b, preferred_element_type=F32)


def _dot_nt(a, b):
    return lax.dot_general(a, b, (((1,), (1,)), ((), ())), preferred_element_type=F32)


def _split(x):
    hi = x.astype(BF16)
    lo = (x - hi.astype(F32)).astype(BF16)
    return hi, lo


def _group_sum(x, ones_bd):
    outs = []
    for c0 in range(0, x.shape[1], GROUP_TILE):
        width = min(GROUP_TILE, x.shape[1] - c0)
        outs.append(_dot(x[:, c0:c0 + width].astype(BF16), ones_bd[:width, :width]))
    return outs[0] if len(outs) == 1 else jnp.concatenate(outs, axis=1)


def _standardize(x):
    mu = jnp.mean(x, axis=-1, keepdims=True)
    d = x - mu
    return d * lax.rsqrt(jnp.mean(d * d, axis=-1, keepdims=True) + EPS)


def _silu(x):
    return x / (1.0 + jnp.exp(-x))


def _modulation_kernel(cond_ref, w_ref, b_ref, o_ref):
    s_hi, s_lo = _split(_silu(cond_ref[...]))
    w_hi, w_lo = _split(w_ref[...])
    o_ref[...] = _dot(s_hi, w_hi) + _dot(s_hi, w_lo) + _dot(s_lo, w_hi) + b_ref[...]


def _modulation(cond, ada_w, ada_b):
    n_out = ada_w.shape[-1]
    tn = 1536
    return pl.pallas_call(
        _modulation_kernel,
        grid=(DEPTH, n_out // tn),
        in_specs=[
            pl.BlockSpec((SUBLANES, D_MODEL), lambda l, j: (0, 0)),
            pl.BlockSpec((None, D_MODEL, tn), lambda l, j: (l, 0, j)),
            pl.BlockSpec((None, 1, tn), lambda l, j: (l, 0, j)),
        ],
        out_specs=pl.BlockSpec((None, SUBLANES, tn), lambda l, j: (l, 0, j)),
        out_shape=jax.ShapeDtypeStruct((DEPTH, SUBLANES, n_out), F32),
        compiler_params=pltpu.CompilerParams(
            dimension_semantics=("arbitrary", "arbitrary"), vmem_limit_bytes=VMEM_LIMIT),
        name="modulation",
    )(cond, ada_w, ada_b.reshape(DEPTH, 1, n_out))


def _rope(x, cos, sin_signed):
    w = x.shape[1]
    lane = lax.broadcasted_iota(I32, x.shape, 1)
    first_half = (lane & 31) < 16
    partner = jnp.where(first_half, pltpu.roll(x, w - 16, 1), pltpu.roll(x, 16, 1))
    return x * cos + partner * sin_signed


def _head_variants(x):
    lane = lax.broadcasted_iota(I32, x.shape, 1)
    lo = lane < HEAD_DIM
    xr = pltpu.roll(x, HEAD_DIM, 1)
    zero = jnp.zeros_like(x)
    return (jnp.where(lo, x, zero).astype(BF16), jnp.where(lo, zero, xr).astype(BF16),
            jnp.where(lo, xr, zero).astype(BF16), jnp.where(lo, zero, x).astype(BF16))


def _mixer_kernel(*refs, layer, seq, n_seq, n_cache, rope, cast_weights, stack_kv, side_jobs=(), n_steps=None):
    refs = list(refs)
    skew = n_steps is not None

    def take(n):
        out, refs[:] = refs[:n], refs[n:]
        return out

    (x_ref,) = take(1)
    xres_ref = take(1)[0] if skew else x_ref
    mod_ref, win_ref, wout_ref = take(3)
    qkg_ref, convb_ref, sgug_ref, ln1g_ref, ln1b_ref = take(5)
    ones_ref, convw_ref, sguw_ref, sgub_ref = take(4)
    this = slice(layer, layer + 1)
    if rope:
        cos_ref, sin_ref, kc_ref, vc_ref = take(4)
    if stack_kv:
        kprev_ref, vprev_ref = take(2)
    side_in = [take(n_src) for n_src in side_jobs]
    (x1_ref,) = take(1)
    if not rope:
        k_ref, v_ref = take(2)
    if cast_weights:
        winb_ref, woutb_ref = take(2)

        @pl.when(pl.program_id(0) == 0)
        def _cast():
            winb_ref[...] = win_ref[...].astype(BF16)
            woutb_ref[...] = wout_ref[...].astype(BF16)
        win_ref, wout_ref = winb_ref, woutb_ref
    side_out = take(len(side_jobs))
    scratch = take(7)
    (mix_scr,) = take(1)
    n_rows = n_seq * seq
    assert n_cache == 0 or n_seq == 1

    def loop(n, body):
        if n == 1:
            body(0)
        else:
            def step(r, carry):
                body(r)
                return carry
            lax.fori_loop(0, n, step, 0, unroll=2)

    def block(r, size):
        if isinstance(r, int):
            return pl.ds(r * size, size)
        return pl.ds(pl.multiple_of(r * size, size), size)

    def project_phase(scr):
        q_scr, kvar_scr, vvar_scr, u_scr, cb_scr, su_scr, vn_scr = scr
        for srcs, dst in zip(side_in, side_out):
            col = 0
            for src in srcs:
                dst[:, col:col + src.shape[1]] = src[...].astype(BF16)
                col += src.shape[1]
        if n_cache:
            for i, var in enumerate(_head_variants(kc_ref[...])):
                kvar_scr[i, pl.ds(seq, n_cache), :] = var
            for i, var in enumerate(_head_variants(vc_ref[...])):
                vvar_scr[i, pl.ds(seq, n_cache), :] = var

        def project(r):
            rows = block(r, ROW_BLOCK)
            x = x_ref[rows, :]
            h = _standardize(x) * (1.0 + mod_ref[1:2, :]) + mod_ref[0:1, :]
            z = _dot(h.astype(BF16), win_ref[...])
            ones_bd = ones_ref[...]
            zq = z[:, _Q0:_K0]
            q = zq * lax.rsqrt(_group_sum(zq * zq, ones_bd) * (1.0 / HEAD_DIM) + EPS) * qkg_ref[this, :ATTN_WIDTH]
            zk = z[:, _K0:_V0]
            k = zk * lax.rsqrt(_group_sum(zk * zk, ones_bd) * (1.0 / HEAD_DIM) + EPS) * qkg_ref[this, ATTN_WIDTH:]
            v = z[:, _V0:_CI0]
            if rope:
                cos = cos_ref[rows, :]
                sin = sin_ref[rows, :]
                q = _rope(q, jnp.concatenate([cos] * 4, axis=1), jnp.concatenate([sin] * 4, axis=1))
                k = _rope(k, cos, sin)
            elif stack_kv:
                for s in range(ROW_BLOCK // seq):
                    sub = slice(s * seq, (s + 1) * seq)
                    k_ref[s, 0] = kprev_ref[sub, :]
                    v_ref[s, 0] = vprev_ref[sub, :]
                    k_ref[s, 1] = k[sub, :]
                    v_ref[s, 1] = v[sub, :]
            else:
                k_ref[rows, :] = k
                v_ref[rows, :] = v
            q_scr[rows, :] = (q * ATTN_SCALE).astype(BF16)
            for i, var in enumerate(_head_variants(k)):
                kvar_scr[i, rows, :] = var
            for i, var in enumerate(_head_variants(v)):
                vvar_scr[i, rows, :] = var
            u_scr[rows, :] = z[:, _CC0:_SU0] * z[:, _CI0:_CB0]
            cb_scr[rows, :] = z[:, _CB0:_CC0]
            su_scr[rows, :] = z[:, _SU0:_SV0]
            sv = z[:, _SV0:IN_WIDTH]
            d = sv - _group_sum(sv, ones_bd) * (1.0 / HEAD_DIM)
            vn = d * lax.rsqrt(_group_sum(d * d, ones_bd) * (1.0 / HEAD_DIM) + EPS) * sgug_ref[this, :]
            vn_scr[rows, :] = vn.astype(BF16)

        loop(n_rows // ROW_BLOCK, project)

    def consume_phase(scr):
        q_scr, kvar_scr, vvar_scr, u_scr, cb_scr, su_scr, vn_scr = scr
        u = u_scr[...]
        pos = lax.broadcasted_iota(I32, u.shape, 0) & (seq - 1)
        up = jnp.where(pos == 0, 0.0, pltpu.roll(u, 1, 0))
        dn = jnp.where(pos == seq - 1, 0.0, pltpu.roll(u, n_rows - 1, 0))
        conv = up * convw_ref[0:1, :] + u * convw_ref[1:2, :] + dn * convw_ref[2:3, :] + convb_ref[this, :]
        mix_scr[:, ATTN_WIDTH:ATTN_WIDTH + CONV_WIDTH] = (cb_scr[...] * conv).astype(BF16)

        for n in range(n_rows // CHUNK):
            rows = pl.ds(n * CHUNK, CHUNK)
            vn = vn_scr[rows, :]
            lane = lax.broadcasted_iota(I32, vn.shape, 1)
            per_head = [jnp.where((lane >= hd * HEAD_DIM) & (lane < (hd + 1) * HEAD_DIM), vn, jnp.zeros_like(vn))
                        for hd in range(SGU_HEADS)]
            s = sgub_ref[...] + _dot(sguw_ref[...], jnp.concatenate(per_head, axis=0))
            mix_scr[rows, ATTN_WIDTH + CONV_WIDTH:] = (su_scr[rows, :] * s).astype(BF16)

        def attend(s, r):
            rows = block(s * (seq // ATT_BLOCK) + r, ATT_BLOCK)
            keys = pl.ds(s * seq, seq + n_cache)
            for pair in range(N_Q_HEADS // 2):
                qp = q_scr[rows, pair * LANES:(pair + 1) * LANES]
                kv = pair // (N_Q_HEADS // N_KV_HEADS // 2)
                acc = jnp.zeros((ATT_BLOCK, LANES), F32)
                for parity in range(2):
                    sc = _dot_nt(qp, kvar_scr[2 * kv + parity, keys, :])
                    p = jnp.exp(sc - jnp.max(sc, axis=1, keepdims=True))
                    denom = jnp.sum(p, axis=1, keepdims=True)
                    acc = acc + _dot(p.astype(BF16), vvar_scr[2 * kv + parity, keys, :]) / denom
                mix_scr[rows, pair * LANES:(pair + 1) * LANES] = acc.astype(BF16)

        for s in range(n_seq):
            loop(seq // ATT_BLOCK, functools.partial(attend, s))

        def finish(r):
            rows = block(r, ROW_BLOCK)
            mix = _dot(mix_scr[rows, :], wout_ref[...])
            y = DEEPNORM_ALPHA * xres_ref[rows, :] + mod_ref[2:3, :] * mix
            x1_ref[rows, :] = _standardize(y) * ln1g_ref[this, :] + ln1b_ref[this, :]

        loop(n_rows // ROW_BLOCK, finish)

    if not skew:
        project_phase(scratch)
        consume_phase(scratch)
        return

    i = pl.program_id(0)
    slot = lax.rem(i, 2)
    mine = [ref.at[slot] for ref in scratch]
    other = [ref.at[1 - slot] for ref in scratch]
    pl.when(i == 0)(functools.partial(project_phase, mine))

    @pl.when((i > 0) & (i < n_steps))
    def _():
        consume_phase(other)
        project_phase(mine)

    pl.when(i == n_steps)(functools.partial(consume_phase, other))


def _full(shape):
    n = len(shape)
    return pl.BlockSpec(shape, lambda *_: (0,) * n)


def _resident(shape):
    n = len(shape)
    return pl.BlockSpec(shape, lambda *_: (0,) * n, pipeline_mode=pl.Buffered(1))


def _layer_block(arr, l, resident=False):
    shape = arr.shape[1:]
    kw = dict(pipeline_mode=pl.Buffered(1)) if resident else {}
    return pl.BlockSpec((None,) + shape, lambda *_: (l,) + (0,) * len(shape), **kw)


def _mixer_scratch(n_rows, n_cache, slots=()):
    nk = n_rows + n_cache
    return [
        pltpu.VMEM(slots + (n_rows, ATTN_WIDTH), BF16),
        pltpu.VMEM(slots + (4, nk, LANES), BF16),
        pltpu.VMEM(slots + (4, nk, LANES), BF16),
        pltpu.VMEM(slots + (n_rows, CONV_WIDTH), F32),
        pltpu.VMEM(slots + (n_rows, CONV_WIDTH), F32),
        pltpu.VMEM(slots + (n_rows, SGU_WIDTH), F32),
        pltpu.VMEM(slots + (n_rows, SGU_WIDTH), BF16),
        pltpu.VMEM((n_rows, D_MODEL), BF16),
    ]


def _side_source(src):
    return src if isinstance(src, tuple) else (src, None)


def _mixer_ctx(x, l, n_seq, seq, mod, w_in, w_out, small, kv_prev=None, side_casts=()):
    per_step = CTX_SEQS_PER_STEP
    rows = per_step * seq
    stack_kv = kv_prev is not None
    cast_weights = w_in.dtype != BF16
    n_steps = n_seq // per_step
    kernel = functools.partial(_mixer_kernel, layer=l, seq=seq, n_seq=per_step, n_cache=0, rope=False,
                               cast_weights=cast_weights, stack_kv=stack_kv,
                               side_jobs=tuple(len(job) for job in side_casts), n_steps=n_steps)
    cur = lambda i: (jnp.minimum(i, n_steps - 1), 0)
    prev = lambda i: (jnp.maximum(i - 1, 0), 0)
    w_specs = ([_layer_block(w_in, l, resident=True), _layer_block(w_out, l, resident=True)] if cast_weights
               else [_resident(w_in.shape), _resident(w_out.shape)])
    in_specs = ([pl.BlockSpec((rows, D_MODEL), cur), pl.BlockSpec((rows, D_MODEL), prev),
                 pl.BlockSpec((None, None, SUBLANES, D_MODEL), lambda i: (l, 0, 0, 0))] + w_specs
                + [_full(a.shape) if a.ndim == 2 else _layer_block(a, l) for a in small])
    args = [x, x, mod, w_in, w_out, *small]
    if stack_kv:
        in_specs += [pl.BlockSpec((rows, KV_WIDTH), cur)] * 2
        args += list(kv_prev)
        kv_spec = pl.BlockSpec((per_step, DEPTH, seq, KV_WIDTH), lambda i: cur(i) + (0, 0))
        kv_shape = jax.ShapeDtypeStruct((n_seq, DEPTH, seq, KV_WIDTH), F32)
    else:
        kv_spec = pl.BlockSpec((rows, KV_WIDTH), cur)
        kv_shape = jax.ShapeDtypeStruct((n_seq * seq, KV_WIDTH), F32)
    side_out_specs, side_out_shapes = [], []
    for job in side_casts:
        sources = [_side_source(src) for src in job]
        n_rows = sources[0][0].shape[-2]
        n_cols = sum(arr.shape[-1] for arr, _ in sources)
        for arr, layer in sources:
            blk = (n_rows // n_steps, arr.shape[-1])
            if layer is None:
                in_specs.append(pl.BlockSpec(blk, cur))
            else:
                in_specs.append(pl.BlockSpec((None,) + blk, lambda i, layer=layer: (layer,) + cur(i)))
            args.append(arr)
        side_out_specs.append(pl.BlockSpec((n_rows // n_steps, n_cols), cur))
        side_out_shapes.append(jax.ShapeDtypeStruct((n_rows, n_cols), BF16))
    w_out_specs, w_out_shapes = [], []
    if cast_weights:
        w_out_specs = [_full((D_MODEL, IN_WIDTH)), _full((D_MODEL, D_MODEL))]
        w_out_shapes = [jax.ShapeDtypeStruct((D_MODEL, IN_WIDTH), BF16), jax.ShapeDtypeStruct((D_MODEL, D_MODEL), BF16)]
    return pl.pallas_call(
        kernel,
        grid=(n_steps + 1,),
        in_specs=in_specs,
        out_specs=[pl.BlockSpec((rows, D_MODEL), prev), kv_spec, kv_spec] + w_out_specs + side_out_specs,
        out_shape=[jax.ShapeDtypeStruct((n_seq * seq, D_MODEL), F32), kv_shape, kv_shape] + w_out_shapes
        + side_out_shapes,
        scratch_shapes=_mixer_scratch(rows, 0, slots=(2,)),
        compiler_params=pltpu.CompilerParams(dimension_semantics=("arbitrary",), vmem_limit_bytes=VMEM_LIMIT),
        name="mixer_ctx",
    )(*args)


def _mixer_lat(x, l, n_seq, seq, mod, w_in_bf, w_out_bf, small, cos, sin, kc, vc):
    n_cache = kc.shape[2]
    kernel = functools.partial(_mixer_kernel, layer=l, seq=seq, n_seq=1, n_cache=n_cache, rope=True,
                               cast_weights=False, stack_kv=False)
    cache_spec = pl.BlockSpec((None, None, n_cache, KV_WIDTH), lambda b: (b, l, 0, 0))
    return pl.pallas_call(
        kernel,
        grid=(n_seq,),
        in_specs=([pl.BlockSpec((seq, D_MODEL), lambda b: (b, 0)),
                   pl.BlockSpec((None, None, SUBLANES, D_MODEL), lambda b: (l, 1 + b, 0, 0)),
                   _resident((D_MODEL, IN_WIDTH)), _resident((D_MODEL, D_MODEL))]
                  + [_full(a.shape) if a.ndim == 2 else _layer_block(a, l) for a in small]
                  + [_full((seq, LANES)), _full((seq, LANES)), cache_spec, cache_spec]),
        out_specs=pl.BlockSpec((seq, D_MODEL), lambda b: (b, 0)),
        out_shape=jax.ShapeDtypeStruct((n_seq * seq, D_MODEL), F32),
        scratch_shapes=_mixer_scratch(seq, n_cache),
        compiler_params=pltpu.CompilerParams(dimension_semantics=("arbitrary",), vmem_limit_bytes=VMEM_LIMIT),
        name="mixer_lat",
    )(x, mod, w_in_bf, w_out_bf, *small, cos, sin, kc, vc)


def _two_stream_specs(block_rows, n_a, block_of):
    spec_a = pl.BlockSpec((block_rows, D_MODEL), lambda *ids: (jnp.minimum(block_of(*ids), n_a - 1), 0))
    spec_b = pl.BlockSpec((block_rows, D_MODEL), lambda *ids: (jnp.maximum(block_of(*ids) - n_a, 0), 0))
    return [spec_a, spec_b]


def _group_of_block(blk, block_rows, n_a, lat_seq):
    return jnp.where(blk < n_a, 0, 1 + (blk - n_a) // (lat_seq // block_rows))


def _ffn_kernel(xa_ref, xb_ref, mod_ref, w1_ref, w3_ref, w2_ref, g_ref, b_ref, oa_ref, ob_ref, *, n_a, layer):
    is_a = pl.program_id(0) < n_a
    x = jnp.where(is_a, xa_ref[...], xb_ref[...])
    h = (_standardize(x) * (1.0 + mod_ref[4:5, :]) + mod_ref[3:4, :]).astype(BF16)
    acc = jnp.zeros(x.shape, F32)
    for c in range(D_FF // FFN_COLS):
        cols = slice(c * FFN_COLS, (c + 1) * FFN_COLS)
        a = _dot(h, w1_ref[:, cols])
        b = _dot(h, w3_ref[:, cols])
        acc = acc + _dot((_silu(a) * b).astype(BF16), w2_ref[cols, :])
    y = DEEPNORM_ALPHA * x + mod_ref[5:6, :] * acc
    out = _standardize(y) * g_ref[layer:layer + 1, :] + b_ref[layer:layer + 1, :]

    @pl.when(is_a)
    def _():
        oa_ref[...] = out

    @pl.when(jnp.logical_not(is_a))
    def _():
        ob_ref[...] = out


def _ffn_dense(xa, xb, l, mods, w1, w3, w2, g, b, lat_seq):
    n_a, n_b = xa.shape[0] // FFN_TILE, xb.shape[0] // FFN_TILE
    x_specs = _two_stream_specs(FFN_TILE, n_a, lambda t: t)
    return pl.pallas_call(
        functools.partial(_ffn_kernel, n_a=n_a, layer=l),
        grid=(n_a + n_b,),
        in_specs=x_specs + [
            pl.BlockSpec((None, None, SUBLANES, D_MODEL),
                         lambda t: (l, _group_of_block(t, FFN_TILE, n_a, lat_seq), 0, 0)),
            _resident(w1.shape), _resident(w3.shape), _resident(w2.shape), _full(g.shape), _full(b.shape)],
        out_specs=x_specs,
        out_shape=[jax.ShapeDtypeStruct(xa.shape, F32), jax.ShapeDtypeStruct(xb.shape, F32)],
        compiler_params=pltpu.CompilerParams(dimension_semantics=("arbitrary",), vmem_limit_bytes=VMEM_LIMIT),
        name="ffn_dense",
    )(xa, xb, mods, w1, w3, w2, g, b)


def _router_kernel(xa_ref, xb_ref, mod_ref, rw_ref, before_ref, dest_ref, gate_ref, start_ref, padded_ref, *,
                   n_a, lat_seq):
    c = pl.program_id(0)
    w_hi, w_lo = _split(rw_ref[...])
    n_blocks = MOE_CHUNK // ROUTER_BLOCK

    def store_per_token(ref, slot, val):
        for j in range(MOE_CHUNK // LANES):
            ref[slot, j:j + 1, :] = val[:, j * LANES:(j + 1) * LANES]

    parts = []
    for blk in range(n_blocks):
        rows = pl.ds(blk * ROUTER_BLOCK, ROUTER_BLOCK)
        mod = mod_ref[_group_of_block(c * n_blocks + blk, ROUTER_BLOCK, n_a * n_blocks, lat_seq)]
        x = jnp.where(c < n_a, xa_ref[rows, :], xb_ref[rows, :])
        h = _standardize(x) * (1.0 + mod[4:5, :]) + mod[3:4, :]
        h_hi, h_lo = _split(h)
        parts.append(_dot_nt(w_hi, h_hi) + _dot_nt(w_hi, h_lo) + _dot_nt(w_lo, h_hi))
    logits = jnp.concatenate(parts, axis=1)
    eid = lax.broadcasted_iota(I32, logits.shape, 0).astype(F32)
    m1 = jnp.max(logits, axis=0, keepdims=True)
    i1 = jnp.min(jnp.where(logits == m1, eid, float(N_EXPERTS)), axis=0, keepdims=True)
    oh1 = eid == i1
    rest = jnp.where(oh1, -jnp.inf, logits)
    m2 = jnp.max(rest, axis=0, keepdims=True)
    i2 = jnp.min(jnp.where(rest == m2, eid, float(N_EXPERTS)), axis=0, keepdims=True)
    oh2 = eid == i2
    e = jnp.exp(m2 - m1)
    store_per_token(gate_ref, 0, 1.0 / (1.0 + e))
    store_per_token(gate_ref, 1, e / (1.0 + e))
    sel = jnp.where(oh1 | oh2, 1.0, 0.0)
    ranks = []
    seen = jnp.zeros((N_EXPERTS, 1), F32)
    for blk in range(n_blocks):
        s_blk = sel[:, blk * ROUTER_BLOCK:(blk + 1) * ROUTER_BLOCK]
        ranks.append(_dot(s_blk.astype(BF16), before_ref[...]) + seen)
        seen = seen + jnp.sum(s_blk, axis=1, keepdims=True)
    rank = jnp.concatenate(ranks, axis=1)
    eid_out = lax.broadcasted_iota(I32, start_ref.shape, 0).astype(F32)
    start = jnp.zeros(sel.shape, F32)
    start_out = jnp.zeros(start_ref.shape, F32)
    padded_out = jnp.zeros(start_ref.shape, F32)
    for ex in range(N_EXPERTS):
        cnt = jnp.sum(sel[ex:ex + 1, :], axis=1, keepdims=True)
        padded = jnp.ceil(cnt * (1.0 / MOE_GRAN)) * MOE_GRAN
        start = start + jnp.where(eid > ex, padded, 0.0)
        start_out = start_out + jnp.where(eid_out > ex, padded, 0.0)
        padded_out = padded_out + jnp.where(eid_out == ex, padded, 0.0)
    row = (start + rank) * SUBLANES
    store_per_token(dest_ref, 0, jnp.sum(jnp.where(oh1, row, 0.0), axis=0, keepdims=True).astype(I32))
    store_per_token(dest_ref, 1, jnp.sum(jnp.where(oh2, row, 0.0), axis=0, keepdims=True).astype(I32))
    start_ref[...] = start_out.astype(I32)
    padded_ref[...] = padded_out.astype(I32)


def _router(xa, xb, l, mods, rw_t, lat_seq):
    n_a = xa.shape[0] // MOE_CHUNK
    n_chunks = n_a + xb.shape[0] // MOE_CHUNK
    tok = np.arange(ROUTER_BLOCK)
    before = jnp.asarray(tok[:, None] < tok[None, :], BF16)
    return pl.pallas_call(
        functools.partial(_router_kernel, n_a=n_a, lat_seq=lat_seq),
        grid=(n_chunks,),
        in_specs=_two_stream_specs(MOE_CHUNK, n_a, lambda c: c) + [
                  _layer_block(mods, l),
                  _full((N_EXPERTS, D_MODEL)), _full((ROUTER_BLOCK, ROUTER_BLOCK))],
        out_specs=[pl.BlockSpec((None, 2, MOE_CHUNK // LANES, LANES), lambda c: (c, 0, 0, 0)),
                   pl.BlockSpec((None, 2, MOE_CHUNK // LANES, LANES), lambda c: (c, 0, 0, 0)),
                   pl.BlockSpec((None, N_EXPERTS, LANES), lambda c: (c, 0, 0)),
                   pl.BlockSpec((None, N_EXPERTS, LANES), lambda c: (c, 0, 0))],
        out_shape=[jax.ShapeDtypeStruct((n_chunks, 2, MOE_CHUNK // LANES, LANES), I32),
                   jax.ShapeDtypeStruct((n_chunks, 2, MOE_CHUNK // LANES, LANES), F32),
                   jax.ShapeDtypeStruct((n_chunks, N_EXPERTS, LANES), I32),
                   jax.ShapeDtypeStruct((n_chunks, N_EXPERTS, LANES), I32)],
        compiler_params=pltpu.CompilerParams(dimension_semantics=("arbitrary",), vmem_limit_bytes=VMEM_LIMIT),
        name="moe_router",
    )(xa, xb, mods, rw_t, before)


def _token_rows(first_row):
    return pl.ds(pl.multiple_of(first_row, SUBLANES), SUBLANES)


def _store_token_major(ref, tok0, val):
    for cc in range(D_MODEL // LANES):
        ref[pl.ds(tok0 * SUBLANES + cc, val.shape[0], stride=SUBLANES), :] = val[:, cc * LANES:(cc + 1) * LANES]


def _load_token_major(ref, tok0, n):
    return jnp.concatenate(
        [ref[pl.ds(tok0 * SUBLANES + cc, n, stride=SUBLANES), :] for cc in range(D_MODEL // LANES)], axis=1)


def _moe_kernel(start_ref, padded_ref, xa_ref, xb_ref, mod_ref, dest_ref, gate_ref, w13_ref, w2_ref,
                g_ref, b_ref, oa_ref, ob_ref, tok_scr, rows_scr, *, n_a, layer):
    c = pl.program_id(0)
    j = pl.program_id(1)
    is_a = c < n_a
    first_expert_step = MOE_TOK_STEPS
    first_combine_step = MOE_TOK_STEPS + N_EXPERTS

    @pl.when((c == 0) & (j == 0))
    def _init():
        rows_scr[...] = jnp.zeros(rows_scr.shape, F32)

    @pl.when(j < first_expert_step)
    def _dispatch():
        t0 = j * MOE_TOK_BLOCK
        for r in range(MOE_TOK_BLOCK // MOE_SUB_BLOCK):
            rows = pl.ds(r * MOE_SUB_BLOCK, MOE_SUB_BLOCK)
            x = jnp.where(is_a, xa_ref[rows, :], xb_ref[rows, :])
            h = _standardize(x) * (1.0 + mod_ref[4:5, :]) + mod_ref[3:4, :]
            _store_token_major(tok_scr, r * MOE_SUB_BLOCK, h)
            for t in range(r * MOE_SUB_BLOCK, (r + 1) * MOE_SUB_BLOCK):
                row = tok_scr[pl.ds(t * SUBLANES, SUBLANES), :]
                rows_scr[_token_rows(dest_ref[t0 + t]), :] = row
                rows_scr[_token_rows(dest_ref[MOE_CHUNK + t0 + t]), :] = row

    def experts(row0, m):
        xin = _load_token_major(rows_scr, row0, m).astype(BF16)
        ab = _dot(xin, w13_ref[...])
        a, b = ab[:, :D_FF_EXPERT], ab[:, D_FF_EXPERT:]
        y = _dot((_silu(a) * b).astype(BF16), w2_ref[...])
        _store_token_major(rows_scr, row0, y)

    @pl.when((j >= first_expert_step) & (j < first_combine_step))
    def _experts():
        region = c * N_EXPERTS + (j - first_expert_step)
        start = start_ref[region]
        padded = padded_ref[region]
        n_full = lax.shift_right_logical(padded, MOE_BLOCK.bit_length() - 1)

        def body(i, carry):
            experts(start + 2 * i * MOE_BLOCK, MOE_BLOCK)
            experts(start + (2 * i + 1) * MOE_BLOCK, MOE_BLOCK)
            return carry
        lax.fori_loop(0, lax.shift_right_logical(n_full, 1), body, 0)
        pl.when((n_full & 1) == 1)(functools.partial(experts, start + (n_full - 1) * MOE_BLOCK, MOE_BLOCK))
        for m in range(MOE_GRAN, MOE_BLOCK, MOE_GRAN):
            pl.when(padded - n_full * MOE_BLOCK == m)(
                functools.partial(experts, start + n_full * MOE_BLOCK, m))

    @pl.when(j >= first_combine_step)
    def _combine():
        t0 = (j - first_combine_step) * MOE_TOK_BLOCK
        outs = []
        for r in range(MOE_TOK_BLOCK // MOE_SUB_BLOCK):
            for t in range(r * MOE_SUB_BLOCK, (r + 1) * MOE_SUB_BLOCK):
                y0 = rows_scr[_token_rows(dest_ref[t0 + t]), :]
                y1 = rows_scr[_token_rows(dest_ref[MOE_CHUNK + t0 + t]), :]
                tok_scr[pl.ds(t * SUBLANES, SUBLANES), :] = (gate_ref[t0 + t] * y0
                                                             + gate_ref[MOE_CHUNK + t0 + t] * y1)
            rows = pl.ds(r * MOE_SUB_BLOCK, MOE_SUB_BLOCK)
            ffn = _load_token_major(tok_scr, r * MOE_SUB_BLOCK, MOE_SUB_BLOCK)
            x = jnp.where(is_a, xa_ref[rows, :], xb_ref[rows, :])
            y = DEEPNORM_ALPHA * x + mod_ref[5:6, :] * ffn
            outs.append(_standardize(y) * g_ref[layer:layer + 1, :] + b_ref[layer:layer + 1, :])

        @pl.when(is_a)
        def _():
            for r, out in enumerate(outs):
                oa_ref[pl.ds(r * MOE_SUB_BLOCK, MOE_SUB_BLOCK), :] = out

        @pl.when(jnp.logical_not(is_a))
        def _():
            for r, out in enumerate(outs):
                ob_ref[pl.ds(r * MOE_SUB_BLOCK, MOE_SUB_BLOCK), :] = out


def _ffn_moe(xa, xb, l, mods, rw_t, w13, w2, g, b, lat_seq):
    n_a = xa.shape[0] // MOE_CHUNK
    n_chunks = n_a + xb.shape[0] // MOE_CHUNK
    n_a_blocks = n_a * MOE_TOK_STEPS
    dest, gates, start, padded = _router(xa, xb, l, mods, rw_t, lat_seq)
    first_expert_step = MOE_TOK_STEPS
    first_combine_step = MOE_TOK_STEPS + N_EXPERTS

    def token_block(c, j, *_):
        blk = jnp.where(j < first_combine_step, jnp.minimum(j, MOE_TOK_STEPS - 1), j - first_combine_step)
        return c * MOE_TOK_STEPS + blk

    def out_token_block(c, j, *_):
        return c * MOE_TOK_STEPS + jnp.maximum(j - first_combine_step, 0)

    def mod_of(c, j, *_):
        return (l, _group_of_block(token_block(c, j), MOE_TOK_BLOCK, n_a_blocks, lat_seq), 0, 0)

    def expert_of(c, j, *_):
        return (jnp.clip(j - first_expert_step, 0, N_EXPERTS - 1), 0, 0)

    grid_spec = pltpu.PrefetchScalarGridSpec(
        num_scalar_prefetch=2,
        grid=(n_chunks, MOE_STEPS),
        in_specs=_two_stream_specs(MOE_TOK_BLOCK, n_a_blocks, token_block) + [
                  pl.BlockSpec((None, None, SUBLANES, D_MODEL), mod_of),
                  pl.BlockSpec((2 * MOE_CHUNK,), lambda c, j, *_: (c,), memory_space=pltpu.SMEM),
                  pl.BlockSpec((2 * MOE_CHUNK,), lambda c, j, *_: (c,), memory_space=pltpu.SMEM),
                  pl.BlockSpec((None, D_MODEL, 2 * D_FF_EXPERT), expert_of),
                  pl.BlockSpec((None, D_FF_EXPERT, D_MODEL), expert_of),
                  pl.BlockSpec(g.shape, lambda c, j, *_: (0, 0)), pl.BlockSpec(b.shape, lambda c, j, *_: (0, 0))],
        out_specs=_two_stream_specs(MOE_TOK_BLOCK, n_a_blocks, out_token_block),
        scratch_shapes=[pltpu.VMEM((MOE_TOK_BLOCK * SUBLANES, LANES), F32),
                        pltpu.VMEM((MOE_ROWS * SUBLANES, LANES), F32)],
    )
    return pl.pallas_call(
        functools.partial(_moe_kernel, n_a=n_a, layer=l),
        grid_spec=grid_spec,
        out_shape=[jax.ShapeDtypeStruct(xa.shape, F32), jax.ShapeDtypeStruct(xb.shape, F32)],
        compiler_params=pltpu.CompilerParams(
            dimension_semantics=("arbitrary", "arbitrary"), vmem_limit_bytes=VMEM_LIMIT),
        name="moe_experts",
    )(start[:, :, 0].reshape(-1), padded[:, :, 0].reshape(-1), xa, xb, mods,
      dest.reshape(-1), gates.reshape(-1), w13, w2, g, b)


def _rope_tables(n_tokens):
    t = np.arange(n_tokens)
    row = (t // GRID_W).astype(np.float32)
    col = (t % GRID_W).astype(np.float32)
    inv_freq = (np.float32(ROPE_THETA) ** (-np.arange(0, AXIS_ROT, 2, dtype=np.float32) / AXIS_ROT)).astype(np.float32)
    ang_r = row[:, None] * inv_freq
    ang_c = col[:, None] * inv_freq
    cos = np.concatenate([np.cos(ang_r), np.cos(ang_r), np.cos(ang_c), np.cos(ang_c)], axis=1)
    sin = np.concatenate([-np.sin(ang_r), np.sin(ang_r), -np.sin(ang_c), np.sin(ang_c)], axis=1)
    return jnp.asarray(np.tile(cos, (1, 2)), F32), jnp.asarray(np.tile(sin, (1, 2)), F32)


def kernel(x_prompt, x_sample, cache_k, cache_v, c, c_ctx, ada_w, ada_b, w_in, q_norm_g, k_norm_g, conv_w, conv_b, sgu_norm_g, sgu_w, sgu_b, w_out, ln1_g, ln1_b, ln2_g, ln2_b, ffn_w1, ffn_w3, ffn_w2, router_w, moe_w1, moe_w3, moe_w2):
    batch, seq, _ = x_prompt.shape
    dec_batch, dec_seq, _ = x_sample.shape
    past_len = cache_k.shape[2]
    n_ctx = batch * seq
    n_lat = dec_batch * dec_seq
    assert DEPTH == 2 and 1 + dec_batch <= SUBLANES
    assert seq == ATT_BLOCK and CTX_SEQS_PER_STEP * seq == ROW_BLOCK and batch % CTX_SEQS_PER_STEP == 0
    assert dec_seq % ROW_BLOCK == 0 and dec_seq & (dec_seq - 1) == 0
    assert n_ctx % MOE_CHUNK == 0 and n_lat % MOE_CHUNK == 0
    assert dec_seq % ROUTER_BLOCK == 0 and dec_seq % MOE_TOK_BLOCK == 0 and dec_seq % FFN_TILE == 0

    cond = jnp.concatenate([c_ctx[None, :], c, jnp.zeros((SUBLANES - 1 - dec_batch, D_MODEL), F32)], axis=0)
    mod = _modulation(cond, ada_w, ada_b)
    mod = mod.reshape(DEPTH, SUBLANES, 6, D_MODEL)[:, :1 + dec_batch]
    mod = jnp.pad(mod, ((0, 0), (0, 0), (0, SUBLANES - 6), (0, 0)))

    lane_id = np.arange(GROUP_TILE) // HEAD_DIM
    ones_bd = jnp.asarray(lane_id[:, None] == lane_id[None, :], BF16)
    cos, sin = _rope_tables(dec_seq)
    small = (
        jnp.concatenate([jnp.tile(q_norm_g, (1, N_Q_HEADS)), jnp.tile(k_norm_g, (1, N_KV_HEADS))], axis=1),
        conv_b, sgu_norm_g, ln1_g, ln1_b,
        ones_bd, conv_w,
        jnp.swapaxes(sgu_w, 1, 2).reshape(DEPTH, CHUNK, SGU_HEADS * CHUNK).astype(BF16),
        jnp.repeat(jnp.swapaxes(sgu_b, 1, 2), HEAD_DIM, axis=2),
    )
    kc = cache_k.reshape(dec_batch, DEPTH, past_len, KV_WIDTH)
    vc = cache_v.reshape(dec_batch, DEPTH, past_len, KV_WIDTH)

    xs = [x_prompt.reshape(n_ctx, D_MODEL), x_sample.reshape(n_lat, D_MODEL)]
    assert DEPTH == 2 and ffn_w1.shape[0] == 1 and moe_w1.shape[0] == 1
    side_casts = [[(ffn_w1[0],), (ffn_w3[0],), (ffn_w2[0],), ((w_in, 1),), ((w_out, 1),), (moe_w2.reshape(-1, D_MODEL),)],
                  [(moe_w1.reshape(-1, D_FF_EXPERT), moe_w3.reshape(-1, D_FF_EXPERT))]]
    kv = None
    for l in range(DEPTH):
        if l == 0:
            x_ctx, k_ctx, v_ctx, w_in_bf, w_out_bf, *ffn_bf, w_in_next, w_out_next, w2_bf = _mixer_ctx(
                xs[0], l, batch, seq, mod, w_in, w_out, small, kv, side_casts[l])
        else:
            w_in_bf, w_out_bf = w_in_next, w_out_next
            x_ctx, k_ctx, v_ctx, w13_bf = _mixer_ctx(
                xs[0], l, batch, seq, mod, w_in_bf, w_out_bf, small, kv, side_casts[l])
        kv = (k_ctx, v_ctx)
        x_lat = _mixer_lat(xs[1], l, dec_batch, dec_seq, mod, w_in_bf, w_out_bf, small, cos, sin, kc, vc)
        if l % 2 == 0:
            xs = _ffn_dense(x_ctx, x_lat, l, mod, *ffn_bf, ln2_g, ln2_b, dec_seq)
        else:
            ws = (w13_bf.reshape(N_EXPERTS, D_MODEL, 2 * D_FF_EXPERT), w2_bf.reshape(N_EXPERTS, D_FF_EXPERT, D_MODEL))
            xs = _ffn_moe(x_ctx, x_lat, l, mod, router_w[l // 2].T, *ws, ln2_g, ln2_b, dec_seq)
    y_p = xs[0].reshape(batch, seq, D_MODEL)
    y_s = xs[1].reshape(dec_batch, dec_seq, D_MODEL)
    new_k = kv[0].reshape(batch, DEPTH, seq, N_KV_HEADS, HEAD_DIM)
    new_v = kv[1].reshape(batch, DEPTH, seq, N_KV_HEADS, HEAD_DIM)
    return (y_p, y_s, new_k, new_v)
```

```python
import functools

import numpy as np
import jax
import jax.numpy as jnp
from jax import lax
from jax.experimental import pallas as pl
from jax.experimental.pallas import tpu as pltpu

F32 = jnp.float32
BF16 = jnp.bfloat16
I32 = jnp.int32

D_MODEL = 1024
DEPTH = 2
GRID_W = 64
HEAD_DIM = 64
N_Q_HEADS = 8
N_KV_HEADS = 2
ATTN_WIDTH = N_Q_HEADS * HEAD_DIM
KV_WIDTH = N_KV_HEADS * HEAD_DIM
ATTN_SCALE = HEAD_DIM ** -0.5
ROPE_THETA = 10000.0
AXIS_ROT = HEAD_DIM // 2
CONV_WIDTH = 256
SGU_WIDTH = 256
SGU_HEADS = 4
CHUNK = 128
IN_WIDTH = 2048
D_FF = 2816
N_EXPERTS = 8
D_FF_EXPERT = 1408
EPS = 1e-6
DEEPNORM_ALPHA = (2 * DEPTH) ** 0.25

LANES = 128
SUBLANES = 8
ROW_BLOCK = 512
GROUP_TILE = 256
ATT_BLOCK = 256
CTX_SEQS_PER_STEP = 2
FFN_TILE = 512
FFN_COLS = 256
MOE_CHUNK = 2048
ROUTER_BLOCK = 1024
MOE_GRAN = 128
MOE_BLOCK = 256
MOE_TOK_BLOCK = 512
MOE_SUB_BLOCK = 256
MOE_TOK_STEPS = MOE_CHUNK // MOE_TOK_BLOCK
MOE_STEPS = 2 * MOE_TOK_STEPS + N_EXPERTS
MOE_ROWS = 2 * MOE_CHUNK + N_EXPERTS * MOE_GRAN
VMEM_LIMIT = 60 * 1024 * 1024

_Q0, _K0, _V0, _CI0, _CB0, _CC0, _SU0, _SV0 = 0, 512, 640, 768, 1024, 1280, 1536, 1792


def _dot(a, b):
    return jnp.dot(a, b, preferred_element_type=F32)


def _dot_nt(a, b):
    return lax.dot_general(a, b, (((1,), (1,)), ((), ())), preferred_element_type=F32)


def _split(x):
    hi = x.astype(BF16)
    lo = (x - hi.astype(F32)).astype(BF16)
    return hi, lo


def _group_sum(x, ones_bd):
    outs = []
    for c0 in range(0, x.shape[1], GROUP_TILE):
        width = min(GROUP_TILE, x.shape[1] - c0)
        outs.append(_dot(x[:, c0:c0 + width].astype(BF16), ones_bd[:width, :width]))
    return outs[0] if len(outs) == 1 else jnp.concatenate(outs, axis=1)


def _standardize(x):
    mu = jnp.mean(x, axis=-1, keepdims=True)
    d = x - mu
    return d * lax.rsqrt(jnp.mean(d * d, axis=-1, keepdims=True) + EPS)


def _silu(x):
    return x / (1.0 + jnp.exp(-x))


def _modulation_kernel(cctx_ref, c_ref, w_ref, b_ref, o_ref):
    n_pad = SUBLANES - 1 - c_ref.shape[0]
    cond = jnp.concatenate([cctx_ref[...], c_ref[...], jnp.zeros((n_pad, D_MODEL), F32)], axis=0)
    s_hi, s_lo = _split(_silu(cond))
    w_hi, w_lo = _split(w_ref[...])
    bias = b_ref[pl.ds(pl.program_id(0), 1), :]
    o_ref[...] = _dot(s_hi, w_hi) + _dot(s_hi, w_lo) + _dot(s_lo, w_hi) + bias


def _modulation(c_ctx, c, ada_w, ada_b):
    n_out = ada_w.shape[-1]
    tn = 1536
    return pl.pallas_call(
        _modulation_kernel,
        grid=(DEPTH, n_out // tn),
        in_specs=[
            pl.BlockSpec(c_ctx.shape, lambda l, j: (0, 0)),
            pl.BlockSpec(c.shape, lambda l, j: (0, 0)),
            pl.BlockSpec((None, D_MODEL, tn), lambda l, j: (l, 0, j)),
            pl.BlockSpec((DEPTH, tn), lambda l, j: (0, j)),
        ],
        out_specs=pl.BlockSpec((None, SUBLANES, tn), lambda l, j: (l, 0, j)),
        out_shape=jax.ShapeDtypeStruct((DEPTH, SUBLANES, n_out), F32),
        compiler_params=pltpu.CompilerParams(
            dimension_semantics=("arbitrary", "arbitrary"), vmem_limit_bytes=VMEM_LIMIT),
        name="modulation",
    )(c_ctx, c, ada_w, ada_b)


def _rope(x, cos, sin_signed):
    w = x.shape[1]
    lane = lax.broadcasted_iota(I32, x.shape, 1)
    first_half = (lane & 31) < 16
    partner = jnp.where(first_half, pltpu.roll(x, w - 16, 1), pltpu.roll(x, 16, 1))
    return x * cos + partner * sin_signed


def _head_variants(x):
    lane = lax.broadcasted_iota(I32, x.shape, 1)
    lo = lane < HEAD_DIM
    xr = pltpu.roll(x, HEAD_DIM, 1)
    zero = jnp.zeros_like(x)
    return (jnp.where(lo, x, zero).astype(BF16), jnp.where(lo, zero, xr).astype(BF16),
            jnp.where(lo, xr, zero).astype(BF16), jnp.where(lo, zero, x).astype(BF16))


def _mixer_kernel(*refs, layer, seq, n_seq, n_cache, rope, cast_weights, stack_kv, side_jobs=(), n_steps=None):
    refs = list(refs)
    skew = n_steps is not None

    def take(n):
        out, refs[:] = refs[:n], refs[n:]
        return out

    (x_ref,) = take(1)
    xres_ref = take(1)[0] if skew else x_ref
    mod_ref, win_ref, wout_ref = take(3)
    qkg_ref, convb_ref, sgug_ref, ln1g_ref, ln1b_ref = take(5)
    ones_ref, convw_ref, sguw_ref, sgub_ref = take(4)
    this = slice(layer, layer + 1)
    if rope:
        cos_ref, sin_ref, kc_ref, vc_ref = take(4)
    if stack_kv:
        kprev_ref, vprev_ref = take(2)
    side_in = [take(n_src) for n_src in side_jobs]
    (x1_ref,) = take(1)
    if not rope:
        k_ref, v_ref = take(2)
    if cast_weights:
        winb_ref, woutb_ref = take(2)

        @pl.when(pl.program_id(0) == 0)
        def _cast():
            winb_ref[...] = win_ref[...].astype(BF16)
            woutb_ref[...] = wout_ref[...].astype(BF16)
        win_ref, wout_ref = winb_ref, woutb_ref
    side_out = take(len(side_jobs))
    scratch = take(7)
    (mix_scr,) = take(1)
    n_rows = n_seq * seq
    assert n_cache == 0 or n_seq == 1

    def loop(n, body):
        if n == 1:
            body(0)
        else:
            def step(r, carry):
                body(r)
                return carry
            lax.fori_loop(0, n, step, 0, unroll=2)

    def block(r, size):
        if isinstance(r, int):
            return pl.ds(r * size, size)
        return pl.ds(pl.multiple_of(r * size, size), size)

    def project_phase(scr):
        q_scr, kvar_scr, vvar_scr, u_scr, cb_scr, su_scr, vn_scr = scr
        for srcs, dst in zip(side_in, side_out):
            col = 0
            for src in srcs:
                dst[:, col:col + src.shape[1]] = src[...].astype(BF16)
                col += src.shape[1]
        if n_cache:
            for i, var in enumerate(_head_variants(kc_ref[...])):
                kvar_scr[i, pl.ds(seq, n_cache), :] = var
            for i, var in enumerate(_head_variants(vc_ref[...])):
                vvar_scr[i, pl.ds(seq, n_cache), :] = var

        def project(r):
            rows = block(r, ROW_BLOCK)
            x = x_ref[rows, :]
            h = _standardize(x) * (1.0 + mod_ref[1:2, :]) + mod_ref[0:1, :]
            z = _dot(h.astype(BF16), win_ref[...])
            ones_bd = ones_ref[...]
            zq = z[:, _Q0:_K0]
            q = zq * lax.rsqrt(_group_sum(zq * zq, ones_bd) * (1.0 / HEAD_DIM) + EPS) * qkg_ref[this, :ATTN_WIDTH]
            zk = z[:, _K0:_V0]
            k = zk * lax.rsqrt(_group_sum(zk * zk, ones_bd) * (1.0 / HEAD_DIM) + EPS) * qkg_ref[this, ATTN_WIDTH:]
            v = z[:, _V0:_CI0]
            if rope:
                cos = cos_ref[rows, :]
                sin = sin_ref[rows, :]
                q = _rope(q, jnp.concatenate([cos] * 4, axis=1), jnp.concatenate([sin] * 4, axis=1))
                k = _rope(k, cos, sin)
            elif stack_kv:
                for s in range(ROW_BLOCK // seq):
                    sub = slice(s * seq, (s + 1) * seq)
                    k_ref[s, 0] = kprev_ref[sub, :]
                    v_ref[s, 0] = vprev_ref[sub, :]
                    k_ref[s, 1] = k[sub, :]
                    v_ref[s, 1] = v[sub, :]
            else:
                k_ref[rows, :] = k
                v_ref[rows, :] = v
            q_scr[rows, :] = (q * ATTN_SCALE).astype(BF16)
            for i, var in enumerate(_head_variants(k)):
                kvar_scr[i, rows, :] = var
            for i, var in enumerate(_head_variants(v)):
                vvar_scr[i, rows, :] = var
            u_scr[rows, :] = z[:, _CC0:_SU0] * z[:, _CI0:_CB0]
            cb_scr[rows, :] = z[:, _CB0:_CC0]
            su_scr[rows, :] = z[:, _SU0:_SV0]
            sv = z[:, _SV0:IN_WIDTH]
            d = sv - _group_sum(sv, ones_bd) * (1.0 / HEAD_DIM)
            vn = d * lax.rsqrt(_group_sum(d * d, ones_bd) * (1.0 / HEAD_DIM) + EPS) * sgug_ref[this, :]
            vn_scr[rows, :] = vn.astype(BF16)

        loop(n_rows // ROW_BLOCK, project)

    def consume_phase(scr):
        q_scr, kvar_scr, vvar_scr, u_scr, cb_scr, su_scr, vn_scr = scr
        u = u_scr[...]
        pos = lax.broadcasted_iota(I32, u.shape, 0) & (seq - 1)
        up = jnp.where(pos == 0, 0.0, pltpu.roll(u, 1, 0))
        dn = jnp.where(pos == seq - 1, 0.0, pltpu.roll(u, n_rows - 1, 0))
        conv = up * convw_ref[0:1, :] + u * convw_ref[1:2, :] + dn * convw_ref[2:3, :] + convb_ref[this, :]
        mix_scr[:, ATTN_WIDTH:ATTN_WIDTH + CONV_WIDTH] = (cb_scr[...] * conv).astype(BF16)

        for n in range(n_rows // CHUNK):
            rows = pl.ds(n * CHUNK, CHUNK)
            vn = vn_scr[rows, :]
            lane = lax.broadcasted_iota(I32, vn.shape, 1)
            per_head = [jnp.where((lane >= hd * HEAD_DIM) & (lane < (hd + 1) * HEAD_DIM), vn, jnp.zeros_like(vn))
                        for hd in range(SGU_HEADS)]
            s = sgub_ref[...] + _dot(sguw_ref[...], jnp.concatenate(per_head, axis=0))
            mix_scr[rows, ATTN_WIDTH + CONV_WIDTH:] = (su_scr[rows, :] * s).astype(BF16)

        def attend(s, r):
            rows = block(s * (seq // ATT_BLOCK) + r, ATT_BLOCK)
            keys = pl.ds(s * seq, seq + n_cache)
            for pair in range(N_Q_HEADS // 2):
                qp = q_scr[rows, pair * LANES:(pair + 1) * LANES]
                kv = pair // (N_Q_HEADS // N_KV_HEADS // 2)
                acc = jnp.zeros((ATT_BLOCK, LANES), F32)
                for parity in range(2):
                    sc = _dot_nt(qp, kvar_scr[2 * kv + parity, keys, :])
                    p = jnp.exp(sc - jnp.max(sc, axis=1, keepdims=True))
                    denom = jnp.sum(p, axis=1, keepdims=True)
                    acc = acc + _dot(p.astype(BF16), vvar_scr[2 * kv + parity, keys, :]) / denom
                mix_scr[rows, pair * LANES:(pair + 1) * LANES] = acc.astype(BF16)

        for s in range(n_seq):
            loop(seq // ATT_BLOCK, functools.partial(attend, s))

        def finish(r):
            rows = block(r, ROW_BLOCK)
            mix = _dot(mix_scr[rows, :], wout_ref[...])
            y = DEEPNORM_ALPHA * xres_ref[rows, :] + mod_ref[2:3, :] * mix
            x1_ref[rows, :] = _standardize(y) * ln1g_ref[this, :] + ln1b_ref[this, :]

        loop(n_rows // ROW_BLOCK, finish)

    if not skew:
        project_phase(scratch)
        consume_phase(scratch)
        return

    i = pl.program_id(0)
    slot = lax.rem(i, 2)
    mine = [ref.at[slot] for ref in scratch]
    other = [ref.at[1 - slot] for ref in scratch]
    pl.when(i == 0)(functools.partial(project_phase, mine))

    @pl.when((i > 0) & (i < n_steps))
    def _():
        consume_phase(other)
        project_phase(mine)

    pl.when(i == n_steps)(functools.partial(consume_phase, other))


def _full(shape):
    n = len(shape)
    return pl.BlockSpec(shape, lambda *_: (0,) * n)


def _resident(shape):
    n = len(shape)
    return pl.BlockSpec(shape, lambda *_: (0,) * n, pipeline_mode=pl.Buffered(1))


def _layer_block(arr, l, resident=False):
    shape = arr.shape[1:]
    kw = dict(pipeline_mode=pl.Buffered(1)) if resident else {}
    return pl.BlockSpec((None,) + shape, lambda *_: (l,) + (0,) * len(shape), **kw)


def _mixer_scratch(n_rows, n_cache, slots=()):
    nk = n_rows + n_cache
    return [
        pltpu.VMEM(slots + (n_rows, ATTN_WIDTH), BF16),
        pltpu.VMEM(slots + (4, nk, LANES), BF16),
        pltpu.VMEM(slots + (4, nk, LANES), BF16),
        pltpu.VMEM(slots + (n_rows, CONV_WIDTH), F32),
        pltpu.VMEM(slots + (n_rows, CONV_WIDTH), F32),
        pltpu.VMEM(slots + (n_rows, SGU_WIDTH), F32),
        pltpu.VMEM(slots + (n_rows, SGU_WIDTH), BF16),
        pltpu.VMEM((n_rows, D_MODEL), BF16),
    ]


def _side_source(src):
    return src if isinstance(src, tuple) else (src, None)


def _mixer_ctx(x, l, n_seq, seq, mod, w_in, w_out, small, kv_prev=None, side_casts=()):
    per_step = CTX_SEQS_PER_STEP
    rows = per_step * seq
    stack_kv = kv_prev is not None
    cast_weights = w_in.dtype != BF16
    n_steps = n_seq // per_step
    kernel = functools.partial(_mixer_kernel, layer=l, seq=seq, n_seq=per_step, n_cache=0, rope=False,
                               cast_weights=cast_weights, stack_kv=stack_kv,
                               side_jobs=tuple(len(job) for job in side_casts), n_steps=n_steps)
    cur = lambda i: (jnp.minimum(i, n_steps - 1), 0)
    prev = lambda i: (jnp.maximum(i - 1, 0), 0)
    w_specs = ([_layer_block(w_in, l, resident=True), _layer_block(w_out, l, resident=True)] if cast_weights
               else [_resident(w_in.shape), _resident(w_out.shape)])
    in_specs = ([pl.BlockSpec((rows, D_MODEL), cur), pl.BlockSpec((rows, D_MODEL), prev),
                 pl.BlockSpec((None, None, SUBLANES, D_MODEL), lambda i: (l, 0, 0, 0))] + w_specs
                + [_full(a.shape) if a.ndim == 2 else _layer_block(a, l) for a in small])
    args = [x, x, mod, w_in, w_out, *small]
    if stack_kv:
        in_specs += [pl.BlockSpec((rows, KV_WIDTH), cur)] * 2
        args += list(kv_prev)
        kv_spec = pl.BlockSpec((per_step, DEPTH, seq, KV_WIDTH), lambda i: cur(i) + (0, 0))
        kv_shape = jax.ShapeDtypeStruct((n_seq, DEPTH, seq, KV_WIDTH), F32)
    else:
        kv_spec = pl.BlockSpec((rows, KV_WIDTH), cur)
        kv_shape = jax.ShapeDtypeStruct((n_seq * seq, KV_WIDTH), F32)
    side_out_specs, side_out_shapes = [], []
    for job in side_casts:
        sources = [_side_source(src) for src in job]
        n_rows = sources[0][0].shape[-2]
        n_cols = sum(arr.shape[-1] for arr, _ in sources)
        for arr, layer in sources:
            blk = (n_rows // n_steps, arr.shape[-1])
            if layer is None:
                in_specs.append(pl.BlockSpec(blk, cur))
            else:
                in_specs.append(pl.BlockSpec((None,) + blk, lambda i, layer=layer: (layer,) + cur(i)))
            args.append(arr)
        side_out_specs.append(pl.BlockSpec((n_rows // n_steps, n_cols), cur))
        side_out_shapes.append(jax.ShapeDtypeStruct((n_rows, n_cols), BF16))
    w_out_specs, w_out_shapes = [], []
    if cast_weights:
        w_out_specs = [_full((D_MODEL, IN_WIDTH)), _full((D_MODEL, D_MODEL))]
        w_out_shapes = [jax.ShapeDtypeStruct((D_MODEL, IN_WIDTH), BF16), jax.ShapeDtypeStruct((D_MODEL, D_MODEL), BF16)]
    return pl.pallas_call(
        kernel,
        grid=(n_steps + 1,),
        in_specs=in_specs,
        out_specs=[pl.BlockSpec((rows, D_MODEL), prev), kv_spec, kv_spec] + w_out_specs + side_out_specs,
        out_shape=[jax.ShapeDtypeStruct((n_seq * seq, D_MODEL), F32), kv_shape, kv_shape] + w_out_shapes
        + side_out_shapes,
        scratch_shapes=_mixer_scratch(rows, 0, slots=(2,)),
        compiler_params=pltpu.CompilerParams(dimension_semantics=("arbitrary",), vmem_limit_bytes=VMEM_LIMIT),
        name="mixer_ctx",
    )(*args)


def _mixer_lat(x, l, n_seq, seq, mod, w_in_bf, w_out_bf, small, cos, sin, kc, vc):
    n_cache = kc.shape[2]
    kernel = functools.partial(_mixer_kernel, layer=l, seq=seq, n_seq=1, n_cache=n_cache, rope=True,
                               cast_weights=False, stack_kv=False)
    cache_spec = pl.BlockSpec((None, None, n_cache, KV_WIDTH), lambda b: (b, l, 0, 0))
    return pl.pallas_call(
        kernel,
        grid=(n_seq,),
        in_specs=([pl.BlockSpec((seq, D_MODEL), lambda b: (b, 0)),
                   pl.BlockSpec((None, None, SUBLANES, D_MODEL), lambda b: (l, 1 + b, 0, 0)),
                   _resident((D_MODEL, IN_WIDTH)), _resident((D_MODEL, D_MODEL))]
                  + [_full(a.shape) if a.ndim == 2 else _layer_block(a, l) for a in small]
                  + [_full((seq, LANES)), _full((seq, LANES)), cache_spec, cache_spec]),
        out_specs=pl.BlockSpec((seq, D_MODEL), lambda b: (b, 0)),
        out_shape=jax.ShapeDtypeStruct((n_seq * seq, D_MODEL), F32),
        scratch_shapes=_mixer_scratch(seq, n_cache),
        compiler_params=pltpu.CompilerParams(dimension_semantics=("arbitrary",), vmem_limit_bytes=VMEM_LIMIT),
        name="mixer_lat",
    )(x, mod, w_in_bf, w_out_bf, *small, cos, sin, kc, vc)


def _two_stream_specs(block_rows, n_a, block_of):
    spec_a = pl.BlockSpec((block_rows, D_MODEL), lambda *ids: (jnp.minimum(block_of(*ids), n_a - 1), 0))
    spec_b = pl.BlockSpec((block_rows, D_MODEL), lambda *ids: (jnp.maximum(block_of(*ids) - n_a, 0), 0))
    return [spec_a, spec_b]


def _group_of_block(blk, block_rows, n_a, lat_seq):
    return jnp.where(blk < n_a, 0, 1 + (blk - n_a) // (lat_seq // block_rows))


def _ffn_kernel(xa_ref, xb_ref, mod_ref, w1_ref, w3_ref, w2_ref, g_ref, b_ref, oa_ref, ob_ref, *, n_a, layer):
    is_a = pl.program_id(0) < n_a
    x = jnp.where(is_a, xa_ref[...], xb_ref[...])
    h = (_standardize(x) * (1.0 + mod_ref[4:5, :]) + mod_ref[3:4, :]).astype(BF16)
    acc = jnp.zeros(x.shape, F32)
    for c in range(D_FF // FFN_COLS):
        cols = slice(c * FFN_COLS, (c + 1) * FFN_COLS)
        a = _dot(h, w1_ref[:, cols])
        b = _dot(h, w3_ref[:, cols])
        acc = acc + _dot((_silu(a) * b).astype(BF16), w2_ref[cols, :])
    y = DEEPNORM_ALPHA * x + mod_ref[5:6, :] * acc
    out = _standardize(y) * g_ref[layer:layer + 1, :] + b_ref[layer:layer + 1, :]

    @pl.when(is_a)
    def _():
        oa_ref[...] = out

    @pl.when(jnp.logical_not(is_a))
    def _():
        ob_ref[...] = out


def _ffn_dense(xa, xb, l, mods, w1, w3, w2, g, b, lat_seq):
    n_a, n_b = xa.shape[0] // FFN_TILE, xb.shape[0] // FFN_TILE
    x_specs = _two_stream_specs(FFN_TILE, n_a, lambda t: t)
    return pl.pallas_call(
        functools.partial(_ffn_kernel, n_a=n_a, layer=l),
        grid=(n_a + n_b,),
        in_specs=x_specs + [
            pl.BlockSpec((None, None, SUBLANES, D_MODEL),
                         lambda t: (l, _group_of_block(t, FFN_TILE, n_a, lat_seq), 0, 0)),
            _resident(w1.shape), _resident(w3.shape), _resident(w2.shape), _full(g.shape), _full(b.shape)],
        out_specs=x_specs,
        out_shape=[jax.ShapeDtypeStruct(xa.shape, F32), jax.ShapeDtypeStruct(xb.shape, F32)],
        compiler_params=pltpu.CompilerParams(dimension_semantics=("arbitrary",), vmem_limit_bytes=VMEM_LIMIT),
        name="ffn_dense",
    )(xa, xb, mods, w1, w3, w2, g, b)


def _router_kernel(xa_ref, xb_ref, mod_ref, rw_ref, before_ref, dest_ref, gate_ref, start_ref, padded_ref, *,
                   n_a, lat_seq):
    c = pl.program_id(0)
    w_hi, w_lo = _split(rw_ref[...])
    n_blocks = MOE_CHUNK // ROUTER_BLOCK

    def store_per_token(ref, slot, val):
        for j in range(MOE_CHUNK // LANES):
            ref[slot, j:j + 1, :] = val[:, j * LANES:(j + 1) * LANES]

    parts = []
    for blk in range(n_blocks):
        rows = pl.ds(blk * ROUTER_BLOCK, ROUTER_BLOCK)
        mod = mod_ref[_group_of_block(c * n_blocks + blk, ROUTER_BLOCK, n_a * n_blocks, lat_seq)]
        x = jnp.where(c < n_a, xa_ref[rows, :], xb_ref[rows, :])
        h = _standardize(x) * (1.0 + mod[4:5, :]) + mod[3:4, :]
        h_hi, h_lo = _split(h)
        parts.append(_dot_nt(w_hi, h_hi) + _dot_nt(w_hi, h_lo) + _dot_nt(w_lo, h_hi))
    logits = jnp.concatenate(parts, axis=1)
    eid = lax.broadcasted_iota(I32, logits.shape, 0).astype(F32)
    m1 = jnp.max(logits, axis=0, keepdims=True)
    i1 = jnp.min(jnp.where(logits == m1, eid, float(N_EXPERTS)), axis=0, keepdims=True)
    oh1 = eid == i1
    rest = jnp.where(oh1, -jnp.inf, logits)
    m2 = jnp.max(rest, axis=0, keepdims=True)
    i2 = jnp.min(jnp.where(rest == m2, eid, float(N_EXPERTS)), axis=0, keepdims=True)
    oh2 = eid == i2
    e = jnp.exp(m2 - m1)
    store_per_token(gate_ref, 0, 1.0 / (1.0 + e))
    store_per_token(gate_ref, 1, e / (1.0 + e))
    sel = jnp.where(oh1 | oh2, 1.0, 0.0)
    ranks = []
    seen = jnp.zeros((N_EXPERTS, 1), F32)
    for blk in range(n_blocks):
        s_blk = sel[:, blk * ROUTER_BLOCK:(blk + 1) * ROUTER_BLOCK]
        ranks.append(_dot(s_blk.astype(BF16), before_ref[...]) + seen)
        seen = seen + jnp.sum(s_blk, axis=1, keepdims=True)
    rank = jnp.concatenate(ranks, axis=1)
    eid_out = lax.broadcasted_iota(I32, (1, LANES), 1).astype(F32)
    start = jnp.zeros(sel.shape, F32)
    start_out = jnp.zeros((1, LANES), F32)
    padded_out = jnp.zeros((1, LANES), F32)
    for ex in range(N_EXPERTS):
        cnt = jnp.sum(sel[ex:ex + 1, :], axis=1, keepdims=True)
        padded = jnp.ceil(cnt * (1.0 / MOE_GRAN)) * MOE_GRAN
        start = start + jnp.where(eid > ex, padded, 0.0)
        start_out = start_out + jnp.where(eid_out > ex, padded, 0.0)
        padded_out = padded_out + jnp.where(eid_out == ex, padded, 0.0)
    row = (start + rank) * SUBLANES
    store_per_token(dest_ref, 0, jnp.sum(jnp.where(oh1, row, 0.0), axis=0, keepdims=True).astype(I32))
    store_per_token(dest_ref, 1, jnp.sum(jnp.where(oh2, row, 0.0), axis=0, keepdims=True).astype(I32))

    @pl.when(c == 0)
    def _():
        start_ref[...] = jnp.zeros(start_ref.shape, I32)
        padded_ref[...] = jnp.zeros(padded_ref.shape, I32)

    start_ref[pl.ds(c, 1), :] = start_out.astype(I32)
    padded_ref[pl.ds(c, 1), :] = padded_out.astype(I32)


def _router(xa, xb, l, mods, rw_t, lat_seq):
    n_a = xa.shape[0] // MOE_CHUNK
    n_chunks = n_a + xb.shape[0] // MOE_CHUNK
    tok = np.arange(ROUTER_BLOCK)
    before = jnp.asarray(tok[:, None] < tok[None, :], BF16)
    table_rows = -(-n_chunks // SUBLANES) * SUBLANES
    return pl.pallas_call(
        functools.partial(_router_kernel, n_a=n_a, lat_seq=lat_seq),
        grid=(n_chunks,),
        in_specs=_two_stream_specs(MOE_CHUNK, n_a, lambda c: c) + [
                  _layer_block(mods, l),
                  _full((N_EXPERTS, D_MODEL)), _full((ROUTER_BLOCK, ROUTER_BLOCK))],
        out_specs=[pl.BlockSpec((None, 2, MOE_CHUNK // LANES, LANES), lambda c: (c, 0, 0, 0)),
                   pl.BlockSpec((None, 2, MOE_CHUNK // LANES, LANES), lambda c: (c, 0, 0, 0)),
                   _full((table_rows, LANES)), _full((table_rows, LANES))],
        out_shape=[jax.ShapeDtypeStruct((n_chunks, 2, MOE_CHUNK // LANES, LANES), I32),
                   jax.ShapeDtypeStruct((n_chunks, 2, MOE_CHUNK // LANES, LANES), F32),
                   jax.ShapeDtypeStruct((table_rows, LANES), I32),
                   jax.ShapeDtypeStruct((table_rows, LANES), I32)],
        compiler_params=pltpu.CompilerParams(dimension_semantics=("arbitrary",), vmem_limit_bytes=VMEM_LIMIT),
        name="moe_router",
    )(xa, xb, mods, rw_t, before)


def _token_rows(first_row):
    return pl.ds(pl.multiple_of(first_row, SUBLANES), SUBLANES)


def _store_token_major(ref, tok0, val):
    for cc in range(D_MODEL // LANES):
        ref[pl.ds(tok0 * SUBLANES + cc, val.shape[0], stride=SUBLANES), :] = val[:, cc * LANES:(cc + 1) * LANES]


def _load_token_major(ref, tok0, n):
    return jnp.concatenate(
        [ref[pl.ds(tok0 * SUBLANES + cc, n, stride=SUBLANES), :] for cc in range(D_MODEL // LANES)], axis=1)


def _moe_kernel(start_ref, padded_ref, xa_ref, xb_ref, mod_ref, dest_ref, gate_ref, w13_ref, w2_ref,
                g_ref, b_ref, oa_ref, ob_ref, tok_scr, rows_scr, *, n_a, layer):
    c = pl.program_id(0)
    j = pl.program_id(1)
    is_a = c < n_a
    first_expert_step = MOE_TOK_STEPS
    first_combine_step = MOE_TOK_STEPS + N_EXPERTS

    @pl.when((c == 0) & (j == 0))
    def _init():
        rows_scr[...] = jnp.zeros(rows_scr.shape, F32)

    @pl.when(j < first_expert_step)
    def _dispatch():
        t0 = j * MOE_TOK_BLOCK
        for r in range(MOE_TOK_BLOCK // MOE_SUB_BLOCK):
            rows = pl.ds(r * MOE_SUB_BLOCK, MOE_SUB_BLOCK)
            x = jnp.where(is_a, xa_ref[rows, :], xb_ref[rows, :])
            h = _standardize(x) * (1.0 + mod_ref[4:5, :]) + mod_ref[3:4, :]
            _store_token_major(tok_scr, r * MOE_SUB_BLOCK, h)
            for t in range(r * MOE_SUB_BLOCK, (r + 1) * MOE_SUB_BLOCK):
                row = tok_scr[pl.ds(t * SUBLANES, SUBLANES), :]
                rows_scr[_token_rows(dest_ref[t0 + t]), :] = row
                rows_scr[_token_rows(dest_ref[MOE_CHUNK + t0 + t]), :] = row

    def experts(row0, m):
        xin = _load_token_major(rows_scr, row0, m).astype(BF16)
        ab = _dot(xin, w13_ref[...])
        a, b = ab[:, :D_FF_EXPERT], ab[:, D_FF_EXPERT:]
        y = _dot((_silu(a) * b).astype(BF16), w2_ref[...])
        _store_token_major(rows_scr, row0, y)

    @pl.when((j >= first_expert_step) & (j < first_combine_step))
    def _experts():
        region = c * LANES + (j - first_expert_step)
        start = start_ref[region]
        padded = padded_ref[region]
        n_full = lax.shift_right_logical(padded, MOE_BLOCK.bit_length() - 1)

        def body(i, carry):
            experts(start + 2 * i * MOE_BLOCK, MOE_BLOCK)
            experts(start + (2 * i + 1) * MOE_BLOCK, MOE_BLOCK)
            return carry
        lax.fori_loop(0, lax.shift_right_logical(n_full, 1), body, 0)
        pl.when((n_full & 1) == 1)(functools.partial(experts, start + (n_full - 1) * MOE_BLOCK, MOE_BLOCK))
        for m in range(MOE_GRAN, MOE_BLOCK, MOE_GRAN):
            pl.when(padded - n_full * MOE_BLOCK == m)(
                functools.partial(experts, start + n_full * MOE_BLOCK, m))

    @pl.when(j >= first_combine_step)
    def _combine():
        t0 = (j - first_combine_step) * MOE_TOK_BLOCK
        outs = []
        for r in range(MOE_TOK_BLOCK // MOE_SUB_BLOCK):
            for t in range(r * MOE_SUB_BLOCK, (r + 1) * MOE_SUB_BLOCK):
                y0 = rows_scr[_token_rows(dest_ref[t0 + t]), :]
                y1 = rows_scr[_token_rows(dest_ref[MOE_CHUNK + t0 + t]), :]
                tok_scr[pl.ds(t * SUBLANES, SUBLANES), :] = (gate_ref[t0 + t] * y0
                                                             + gate_ref[MOE_CHUNK + t0 + t] * y1)
            rows = pl.ds(r * MOE_SUB_BLOCK, MOE_SUB_BLOCK)
            ffn = _load_token_major(tok_scr, r * MOE_SUB_BLOCK, MOE_SUB_BLOCK)
            x = jnp.where(is_a, xa_ref[rows, :], xb_ref[rows, :])
            y = DEEPNORM_ALPHA * x + mod_ref[5:6, :] * ffn
            outs.append(_standardize(y) * g_ref[layer:layer + 1, :] + b_ref[layer:layer + 1, :])

        @pl.when(is_a)
        def _():
            for r, out in enumerate(outs):
                oa_ref[pl.ds(r * MOE_SUB_BLOCK, MOE_SUB_BLOCK), :] = out

        @pl.when(jnp.logical_not(is_a))
        def _():
            for r, out in enumerate(outs):
                ob_ref[pl.ds(r * MOE_SUB_BLOCK, MOE_SUB_BLOCK), :] = out


def _ffn_moe(xa, xb, l, mods, rw_t, w13, w2, g, b, lat_seq):
    n_a = xa.shape[0] // MOE_CHUNK
    n_chunks = n_a + xb.shape[0] // MOE_CHUNK
    n_a_blocks = n_a * MOE_TOK_STEPS
    dest, gates, start, padded = _router(xa, xb, l, mods, rw_t, lat_seq)
    first_expert_step = MOE_TOK_STEPS
    first_combine_step = MOE_TOK_STEPS + N_EXPERTS

    def token_block(c, j, *_):
        blk = jnp.where(j < first_combine_step, jnp.minimum(j, MOE_TOK_STEPS - 1), j - first_combine_step)
        return c * MOE_TOK_STEPS + blk

    def out_token_block(c, j, *_):
        return c * MOE_TOK_STEPS + jnp.maximum(j - first_combine_step, 0)

    def mod_of(c, j, *_):
        return (l, _group_of_block(token_block(c, j), MOE_TOK_BLOCK, n_a_blocks, lat_seq), 0, 0)

    def expert_of(c, j, *_):
        return (jnp.clip(j - first_expert_step, 0, N_EXPERTS - 1), 0, 0)

    grid_spec = pltpu.PrefetchScalarGridSpec(
        num_scalar_prefetch=2,
        grid=(n_chunks, MOE_STEPS),
        in_specs=_two_stream_specs(MOE_TOK_BLOCK, n_a_blocks, token_block) + [
                  pl.BlockSpec((None, None, SUBLANES, D_MODEL), mod_of),
                  pl.BlockSpec((2 * MOE_CHUNK,), lambda c, j, *_: (c,), memory_space=pltpu.SMEM),
                  pl.BlockSpec((2 * MOE_CHUNK,), lambda c, j, *_: (c,), memory_space=pltpu.SMEM),
                  pl.BlockSpec((None, D_MODEL, 2 * D_FF_EXPERT), expert_of),
                  pl.BlockSpec((None, D_FF_EXPERT, D_MODEL), expert_of),
                  pl.BlockSpec(g.shape, lambda c, j, *_: (0, 0)), pl.BlockSpec(b.shape, lambda c, j, *_: (0, 0))],
        out_specs=_two_stream_specs(MOE_TOK_BLOCK, n_a_blocks, out_token_block),
        scratch_shapes=[pltpu.VMEM((MOE_TOK_BLOCK * SUBLANES, LANES), F32),
                        pltpu.VMEM((MOE_ROWS * SUBLANES, LANES), F32)],
    )
    return pl.pallas_call(
        functools.partial(_moe_kernel, n_a=n_a, layer=l),
        grid_spec=grid_spec,
        out_shape=[jax.ShapeDtypeStruct(xa.shape, F32), jax.ShapeDtypeStruct(xb.shape, F32)],
        compiler_params=pltpu.CompilerParams(
            dimension_semantics=("arbitrary", "arbitrary"), vmem_limit_bytes=VMEM_LIMIT),
        name="moe_experts",
    )(start.reshape(-1), padded.reshape(-1), xa, xb, mods,
      dest.reshape(-1), gates.reshape(-1), w13, w2, g, b)


def _rope_tables(n_tokens):
    t = np.arange(n_tokens)
    row = (t // GRID_W).astype(np.float32)
    col = (t % GRID_W).astype(np.float32)
    inv_freq = (np.float32(ROPE_THETA) ** (-np.arange(0, AXIS_ROT, 2, dtype=np.float32) / AXIS_ROT)).astype(np.float32)
    ang_r = row[:, None] * inv_freq
    ang_c = col[:, None] * inv_freq
    cos = np.concatenate([np.cos(ang_r), np.cos(ang_r), np.cos(ang_c), np.cos(ang_c)], axis=1)
    sin = np.concatenate([-np.sin(ang_r), np.sin(ang_r), -np.sin(ang_c), np.sin(ang_c)], axis=1)
    return jnp.asarray(np.tile(cos, (1, 2)), F32), jnp.asarray(np.tile(sin, (1, 2)), F32)


def kernel(x_prompt, x_sample, cache_k, cache_v, c, c_ctx, ada_w, ada_b, w_in, q_norm_g, k_norm_g, conv_w, conv_b, sgu_norm_g, sgu_w, sgu_b, w_out, ln1_g, ln1_b, ln2_g, ln2_b, ffn_w1, ffn_w3, ffn_w2, router_w, moe_w1, moe_w3, moe_w2):
    batch, seq, _ = x_prompt.shape
    dec_batch, dec_seq, _ = x_sample.shape
    past_len = cache_k.shape[2]
    n_ctx = batch * seq
    n_lat = dec_batch * dec_seq
    assert DEPTH == 2 and 1 + dec_batch <= SUBLANES
    assert seq == ATT_BLOCK and CTX_SEQS_PER_STEP * seq == ROW_BLOCK and batch % CTX_SEQS_PER_STEP == 0
    assert dec_seq % ROW_BLOCK == 0 and dec_seq & (dec_seq - 1) == 0
    assert n_ctx % MOE_CHUNK == 0 and n_lat % MOE_CHUNK == 0
    assert dec_seq % ROUTER_BLOCK == 0 and dec_seq % MOE_TOK_BLOCK == 0 and dec_seq % FFN_TILE == 0

    mod = _modulation(c_ctx[None, :], c, ada_w, ada_b)
    mod = mod.reshape(DEPTH, SUBLANES, 6, D_MODEL)[:, :1 + dec_batch]
    mod = jnp.pad(mod, ((0, 0), (0, 0), (0, SUBLANES - 6), (0, 0)))

    lane_id = np.arange(GROUP_TILE) // HEAD_DIM
    ones_bd = jnp.asarray(lane_id[:, None] == lane_id[None, :], BF16)
    cos, sin = _rope_tables(dec_seq)
    small = (
        jnp.concatenate([jnp.tile(q_norm_g, (1, N_Q_HEADS)), jnp.tile(k_norm_g, (1, N_KV_HEADS))], axis=1),
        conv_b, sgu_norm_g, ln1_g, ln1_b,
        ones_bd, conv_w,
        jnp.swapaxes(sgu_w, 1, 2).reshape(DEPTH, CHUNK, SGU_HEADS * CHUNK).astype(BF16),
        jnp.repeat(jnp.swapaxes(sgu_b, 1, 2), HEAD_DIM, axis=2),
    )
    kc = cache_k.reshape(dec_batch, DEPTH, past_len, KV_WIDTH)
    vc = cache_v.reshape(dec_batch, DEPTH, past_len, KV_WIDTH)

    xs = [x_prompt.reshape(n_ctx, D_MODEL), x_sample.reshape(n_lat, D_MODEL)]
    assert DEPTH == 2 and ffn_w1.shape[0] == 1 and moe_w1.shape[0] == 1
    side_casts = [[(ffn_w1[0],), (ffn_w3[0],), (ffn_w2[0],), ((w_in, 1),), ((w_out, 1),), (moe_w2.reshape(-1, D_MODEL),)],
                  [(moe_w1.reshape(-1, D_FF_EXPERT), moe_w3.reshape(-1, D_FF_EXPERT))]]
    kv = None
    for l in range(DEPTH):
        if l == 0:
            x_ctx, k_ctx, v_ctx, w_in_bf, w_out_bf, *ffn_bf, w_in_next, w_out_next, w2_bf = _mixer_ctx(
                xs[0], l, batch, seq, mod, w_in, w_out, small, kv, side_casts[l])
        else:
            w_in_bf, w_out_bf = w_in_next, w_out_next
            x_ctx, k_ctx, v_ctx, w13_bf = _mixer_ctx(
                xs[0], l, batch, seq, mod, w_in_bf, w_out_bf, small, kv, side_casts[l])
        kv = (k_ctx, v_ctx)
        x_lat = _mixer_lat(xs[1], l, dec_batch, dec_seq, mod, w_in_bf, w_out_bf, small, cos, sin, kc, vc)
        if l % 2 == 0:
            xs = _ffn_dense(x_ctx, x_lat, l, mod, *ffn_bf, ln2_g, ln2_b, dec_seq)
        else:
            ws = (w13_bf.reshape(N_EXPERTS, D_MODEL, 2 * D_FF_EXPERT), w2_bf.reshape(N_EXPERTS, D_FF_EXPERT, D_MODEL))
            xs = _ffn_moe(x_ctx, x_lat, l, mod, router_w[l // 2].T, *ws, ln2_g, ln2_b, dec_seq)
    y_p = xs[0].reshape(batch, seq, D_MODEL)
    y_s = xs[1].reshape(dec_batch, dec_seq, D_MODEL)
    new_k = kv[0].reshape(batch, DEPTH, seq, N_KV_HEADS, HEAD_DIM)
    new_v = kv[1].reshape(batch, DEPTH, seq, N_KV_HEADS, HEAD_DIM)
    return (y_p, y_s, new_k, new_v)
```

```python
import functools

import numpy as np
import jax
import jax.numpy as jnp
from jax import lax
from jax.experimental import pallas as pl
from jax.experimental.pallas import tpu as pltpu

F32 = jnp.float32
BF16 = jnp.bfloat16
I32 = jnp.int32

D_MODEL = 1024
DEPTH = 2
GRID_W = 64
HEAD_DIM = 64
N_Q_HEADS = 8
N_KV_HEADS = 2
ATTN_WIDTH = N_Q_HEADS * HEAD_DIM
KV_WIDTH = N_KV_HEADS * HEAD_DIM
ATTN_SCALE = HEAD_DIM ** -0.5
ROPE_THETA = 10000.0
AXIS_ROT = HEAD_DIM // 2
CONV_WIDTH = 256
SGU_WIDTH = 256
SGU_HEADS = 4
CHUNK = 128
IN_WIDTH = 2048
D_FF = 2816
N_EXPERTS = 8
D_FF_EXPERT = 1408
EPS = 1e-6
DEEPNORM_ALPHA = (2 * DEPTH) ** 0.25

LANES = 128
SUBLANES = 8
ROW_BLOCK = 512
GROUP_TILE = 256
ATT_BLOCK = 256
CTX_SEQS_PER_STEP = 2
FFN_TILE = 512
FFN_COLS = 256
MOE_CHUNK = 2048
ROUTER_BLOCK = 1024
MOE_GRAN = 128
MOE_BLOCK = 256
MOE_TOK_BLOCK = 512
MOE_SUB_BLOCK = 256
MOE_TOK_STEPS = MOE_CHUNK // MOE_TOK_BLOCK
MOE_STEPS = 2 * MOE_TOK_STEPS + N_EXPERTS
MOE_ROWS = 2 * MOE_CHUNK + N_EXPERTS * MOE_GRAN
VMEM_LIMIT = 60 * 1024 * 1024
N_MOD = 6

_Q0, _K0, _V0, _CI0, _CB0, _CC0, _SU0, _SV0 = 0, 512, 640, 768, 1024, 1280, 1536, 1792


def _dot(a, b):
    return jnp.dot(a, b, preferred_element_type=F32)


def _dot_nt(a, b):
    return lax.dot_general(a, b, (((1,), (1,)), ((), ())), preferred_element_type=F32)


def _split(x):
    hi = x.astype(BF16)
    lo = (x - hi.astype(F32)).astype(BF16)
    return hi, lo


def _group_sum(x, ones_bd):
    outs = []
    for c0 in range(0, x.shape[1], GROUP_TILE):
        width = min(GROUP_TILE, x.shape[1] - c0)
        outs.append(_dot(x[:, c0:c0 + width].astype(BF16), ones_bd[:width, :width]))
    return outs[0] if len(outs) == 1 else jnp.concatenate(outs, axis=1)


def _standardize(x):
    mu = jnp.mean(x, axis=-1, keepdims=True)
    d = x - mu
    return d * lax.rsqrt(jnp.mean(d * d, axis=-1, keepdims=True) + EPS)


def _silu(x):
    return x / (1.0 + jnp.exp(-x))


def _modulation_kernel(cctx_ref, c_ref, w_ref, b_ref, o_ref):
    n_pad = SUBLANES - 1 - c_ref.shape[0]
    cond = jnp.concatenate([cctx_ref[...], c_ref[...], jnp.zeros((n_pad, D_MODEL), F32)], axis=0)
    s_hi, s_lo = _split(_silu(cond))
    w_hi, w_lo = _split(w_ref[...])
    bias = b_ref[pl.ds(pl.program_id(0), 1), :]
    o_ref[...] = _dot(s_hi, w_hi) + _dot(s_hi, w_lo) + _dot(s_lo, w_hi) + bias


def _modulation(c_ctx, c, ada_w, ada_b):
    n_out = ada_w.shape[-1]
    tn = 1536
    return pl.pallas_call(
        _modulation_kernel,
        grid=(DEPTH, n_out // tn),
        in_specs=[
            pl.BlockSpec(c_ctx.shape, lambda l, j: (0, 0)),
            pl.BlockSpec(c.shape, lambda l, j: (0, 0)),
            pl.BlockSpec((None, D_MODEL, tn), lambda l, j: (l, 0, j)),
            pl.BlockSpec((DEPTH, tn), lambda l, j: (0, j)),
        ],
        out_specs=pl.BlockSpec((None, SUBLANES, tn), lambda l, j: (l, 0, j)),
        out_shape=jax.ShapeDtypeStruct((DEPTH, SUBLANES, n_out), F32),
        compiler_params=pltpu.CompilerParams(
            dimension_semantics=("arbitrary", "arbitrary"), vmem_limit_bytes=VMEM_LIMIT),
        name="modulation",
    )(c_ctx, c, ada_w, ada_b)


def _rope(x, cos, sin_signed):
    w = x.shape[1]
    lane = lax.broadcasted_iota(I32, x.shape, 1)
    first_half = (lane & 31) < 16
    partner = jnp.where(first_half, pltpu.roll(x, w - 16, 1), pltpu.roll(x, 16, 1))
    return x * cos + partner * sin_signed


def _head_variants(x):
    lane = lax.broadcasted_iota(I32, x.shape, 1)
    lo = lane < HEAD_DIM
    xr = pltpu.roll(x, HEAD_DIM, 1)
    zero = jnp.zeros_like(x)
    return (jnp.where(lo, x, zero).astype(BF16), jnp.where(lo, zero, xr).astype(BF16),
            jnp.where(lo, xr, zero).astype(BF16), jnp.where(lo, zero, x).astype(BF16))


def _mixer_kernel(*refs, layer, seq, n_seq, n_cache, rope, cast_weights, stack_kv, side_jobs=(), n_steps=None):
    refs = list(refs)
    skew = n_steps is not None

    def take(n):
        out, refs[:] = refs[:n], refs[n:]
        return out

    (x_ref,) = take(1)
    xres_ref = take(1)[0] if skew else x_ref
    mod_ref, win_ref, wout_ref = take(3)
    qg_ref, kg_ref, convb_ref, sgug_ref, ln1g_ref, ln1b_ref = take(6)
    ones_ref, convw_ref, sguw_ref, sgub_ref = take(4)
    this = slice(layer, layer + 1)
    if rope:
        cos_ref, sin_ref, kc_ref, vc_ref = take(4)
    if stack_kv:
        kprev_ref, vprev_ref = take(2)
    side_in = [take(n_src) for n_src in side_jobs]
    (x1_ref,) = take(1)
    if not rope:
        k_ref, v_ref = take(2)
    if cast_weights:
        winb_ref, woutb_ref = take(2)

        @pl.when(pl.program_id(0) == 0)
        def _cast():
            winb_ref[...] = win_ref[...].astype(BF16)
            woutb_ref[...] = wout_ref[...].astype(BF16)
        win_ref, wout_ref = winb_ref, woutb_ref
    side_out = take(len(side_jobs))
    scratch = take(7)
    (mix_scr,) = take(1)
    n_rows = n_seq * seq
    assert n_cache == 0 or n_seq == 1

    def loop(n, body):
        if n == 1:
            body(0)
        else:
            def step(r, carry):
                body(r)
                return carry
            lax.fori_loop(0, n, step, 0, unroll=2)

    def block(r, size):
        if isinstance(r, int):
            return pl.ds(r * size, size)
        return pl.ds(pl.multiple_of(r * size, size), size)

    def project_phase(scr):
        q_scr, kvar_scr, vvar_scr, u_scr, cb_scr, su_scr, vn_scr = scr
        for srcs, dst in zip(side_in, side_out):
            col = 0
            for src in srcs:
                dst[:, col:col + src.shape[1]] = src[...].astype(BF16)
                col += src.shape[1]
        if n_cache:
            for i, var in enumerate(_head_variants(kc_ref[...])):
                kvar_scr[i, pl.ds(seq, n_cache), :] = var
            for i, var in enumerate(_head_variants(vc_ref[...])):
                vvar_scr[i, pl.ds(seq, n_cache), :] = var

        def project(r):
            rows = block(r, ROW_BLOCK)
            x = x_ref[rows, :]
            h = _standardize(x) * (1.0 + mod_ref[1:2, :]) + mod_ref[0:1, :]
            z = _dot(h.astype(BF16), win_ref[...])
            ones_bd = ones_ref[...]
            zq = z[:, _Q0:_K0]
            q = zq * lax.rsqrt(_group_sum(zq * zq, ones_bd) * (1.0 / HEAD_DIM) + EPS) * jnp.tile(qg_ref[this, :], (1, N_Q_HEADS))
            zk = z[:, _K0:_V0]
            k = zk * lax.rsqrt(_group_sum(zk * zk, ones_bd) * (1.0 / HEAD_DIM) + EPS) * jnp.tile(kg_ref[this, :], (1, N_KV_HEADS))
            v = z[:, _V0:_CI0]
            if rope:
                cos = cos_ref[rows, :]
                sin = sin_ref[rows, :]
                q = _rope(q, jnp.concatenate([cos] * 4, axis=1), jnp.concatenate([sin] * 4, axis=1))
                k = _rope(k, cos, sin)
            elif stack_kv:
                for s in range(ROW_BLOCK // seq):
                    sub = slice(s * seq, (s + 1) * seq)
                    k_ref[s, 0] = kprev_ref[sub, :]
                    v_ref[s, 0] = vprev_ref[sub, :]
                    k_ref[s, 1] = k[sub, :]
                    v_ref[s, 1] = v[sub, :]
            else:
                k_ref[rows, :] = k
                v_ref[rows, :] = v
            q_scr[rows, :] = (q * ATTN_SCALE).astype(BF16)
            for i, var in enumerate(_head_variants(k)):
                kvar_scr[i, rows, :] = var
            for i, var in enumerate(_head_variants(v)):
                vvar_scr[i, rows, :] = var
            u_scr[rows, :] = z[:, _CC0:_SU0] * z[:, _CI0:_CB0]
            cb_scr[rows, :] = z[:, _CB0:_CC0]
            su_scr[rows, :] = z[:, _SU0:_SV0]
            sv = z[:, _SV0:IN_WIDTH]
            d = sv - _group_sum(sv, ones_bd) * (1.0 / HEAD_DIM)
            vn = d * lax.rsqrt(_group_sum(d * d, ones_bd) * (1.0 / HEAD_DIM) + EPS) * sgug_ref[this, :]
            vn_scr[rows, :] = vn.astype(BF16)

        loop(n_rows // ROW_BLOCK, project)

    def consume_phase(scr):
        q_scr, kvar_scr, vvar_scr, u_scr, cb_scr, su_scr, vn_scr = scr
        u = u_scr[...]
        pos = lax.broadcasted_iota(I32, u.shape, 0) & (seq - 1)
        up = jnp.where(pos == 0, 0.0, pltpu.roll(u, 1, 0))
        dn = jnp.where(pos == seq - 1, 0.0, pltpu.roll(u, n_rows - 1, 0))
        conv = up * convw_ref[0:1, :] + u * convw_ref[1:2, :] + dn * convw_ref[2:3, :] + convb_ref[this, :]
        mix_scr[:, ATTN_WIDTH:ATTN_WIDTH + CONV_WIDTH] = (cb_scr[...] * conv).astype(BF16)

        for n in range(n_rows // CHUNK):
            rows = pl.ds(n * CHUNK, CHUNK)
            vn = vn_scr[rows, :]
            lane = lax.broadcasted_iota(I32, vn.shape, 1)
            per_head = [jnp.where((lane >= hd * HEAD_DIM) & (lane < (hd + 1) * HEAD_DIM), vn, jnp.zeros_like(vn))
                        for hd in range(SGU_HEADS)]
            s = sgub_ref[...] + _dot(sguw_ref[...], jnp.concatenate(per_head, axis=0))
            mix_scr[rows, ATTN_WIDTH + CONV_WIDTH:] = (su_scr[rows, :] * s).astype(BF16)

        def attend(s, r):
            rows = block(s * (seq // ATT_BLOCK) + r, ATT_BLOCK)
            keys = pl.ds(s * seq, seq + n_cache)
            for pair in range(N_Q_HEADS // 2):
                qp = q_scr[rows, pair * LANES:(pair + 1) * LANES]
                kv = pair // (N_Q_HEADS // N_KV_HEADS // 2)
                acc = jnp.zeros((ATT_BLOCK, LANES), F32)
                for parity in range(2):
                    sc = _dot_nt(qp, kvar_scr[2 * kv + parity, keys, :])
                    p = jnp.exp(sc - jnp.max(sc, axis=1, keepdims=True))
                    denom = jnp.sum(p, axis=1, keepdims=True)
                    acc = acc + _dot(p.astype(BF16), vvar_scr[2 * kv + parity, keys, :]) / denom
                mix_scr[rows, pair * LANES:(pair + 1) * LANES] = acc.astype(BF16)

        for s in range(n_seq):
            loop(seq // ATT_BLOCK, functools.partial(attend, s))

        def finish(r):
            rows = block(r, ROW_BLOCK)
            mix = _dot(mix_scr[rows, :], wout_ref[...])
            y = DEEPNORM_ALPHA * xres_ref[rows, :] + mod_ref[2:3, :] * mix
            x1_ref[rows, :] = _standardize(y) * ln1g_ref[this, :] + ln1b_ref[this, :]

        loop(n_rows // ROW_BLOCK, finish)

    if not skew:
        project_phase(scratch)
        consume_phase(scratch)
        return

    i = pl.program_id(0)
    slot = lax.rem(i, 2)
    mine = [ref.at[slot] for ref in scratch]
    other = [ref.at[1 - slot] for ref in scratch]
    pl.when(i == 0)(functools.partial(project_phase, mine))

    @pl.when((i > 0) & (i < n_steps))
    def _():
        consume_phase(other)
        project_phase(mine)

    pl.when(i == n_steps)(functools.partial(consume_phase, other))


def _full(shape):
    n = len(shape)
    return pl.BlockSpec(shape, lambda *_: (0,) * n)


def _resident(shape):
    n = len(shape)
    return pl.BlockSpec(shape, lambda *_: (0,) * n, pipeline_mode=pl.Buffered(1))


def _layer_block(arr, l, resident=False):
    shape = arr.shape[1:]
    kw = dict(pipeline_mode=pl.Buffered(1)) if resident else {}
    return pl.BlockSpec((None,) + shape, lambda *_: (l,) + (0,) * len(shape), **kw)


def _mixer_scratch(n_rows, n_cache, slots=()):
    nk = n_rows + n_cache
    return [
        pltpu.VMEM(slots + (n_rows, ATTN_WIDTH), BF16),
        pltpu.VMEM(slots + (4, nk, LANES), BF16),
        pltpu.VMEM(slots + (4, nk, LANES), BF16),
        pltpu.VMEM(slots + (n_rows, CONV_WIDTH), F32),
        pltpu.VMEM(slots + (n_rows, CONV_WIDTH), F32),
        pltpu.VMEM(slots + (n_rows, SGU_WIDTH), F32),
        pltpu.VMEM(slots + (n_rows, SGU_WIDTH), BF16),
        pltpu.VMEM((n_rows, D_MODEL), BF16),
    ]


def _side_source(src):
    return src if isinstance(src, tuple) else (src, None)


def _mixer_ctx(x, l, n_seq, seq, mod, w_in, w_out, small, kv_prev=None, side_casts=()):
    per_step = CTX_SEQS_PER_STEP
    rows = per_step * seq
    stack_kv = kv_prev is not None
    cast_weights = w_in.dtype != BF16
    n_steps = n_seq // per_step
    kernel = functools.partial(_mixer_kernel, layer=l, seq=seq, n_seq=per_step, n_cache=0, rope=False,
                               cast_weights=cast_weights, stack_kv=stack_kv,
                               side_jobs=tuple(len(job) for job in side_casts), n_steps=n_steps)
    cur = lambda i: (jnp.minimum(i, n_steps - 1), 0)
    prev = lambda i: (jnp.maximum(i - 1, 0), 0)
    w_specs = ([_layer_block(w_in, l, resident=True), _layer_block(w_out, l, resident=True)] if cast_weights
               else [_resident(w_in.shape), _resident(w_out.shape)])
    in_specs = ([pl.BlockSpec((rows, D_MODEL), cur), pl.BlockSpec((rows, D_MODEL), prev),
                 pl.BlockSpec((None, None, N_MOD, D_MODEL), lambda i: (l, 0, 0, 0))] + w_specs
                + [_full(a.shape) if a.ndim == 2 else _layer_block(a, l) for a in small])
    args = [x, x, mod, w_in, w_out, *small]
    if stack_kv:
        in_specs += [pl.BlockSpec((rows, KV_WIDTH), cur)] * 2
        args += list(kv_prev)
        kv_spec = pl.BlockSpec((per_step, DEPTH, seq, KV_WIDTH), lambda i: cur(i) + (0, 0))
        kv_shape = jax.ShapeDtypeStruct((n_seq, DEPTH, seq, KV_WIDTH), F32)
    else:
        kv_spec = pl.BlockSpec((rows, KV_WIDTH), cur)
        kv_shape = jax.ShapeDtypeStruct((n_seq * seq, KV_WIDTH), F32)
    side_out_specs, side_out_shapes = [], []
    for job in side_casts:
        sources = [_side_source(src) for src in job]
        n_rows = sources[0][0].shape[-2]
        n_cols = sum(arr.shape[-1] for arr, _ in sources)
        for arr, layer in sources:
            blk = (n_rows // n_steps, arr.shape[-1])
            if layer is None:
                in_specs.append(pl.BlockSpec(blk, cur))
            else:
                in_specs.append(pl.BlockSpec((None,) + blk, lambda i, layer=layer: (layer,) + cur(i)))
            args.append(arr)
        side_out_specs.append(pl.BlockSpec((n_rows // n_steps, n_cols), cur))
        side_out_shapes.append(jax.ShapeDtypeStruct((n_rows, n_cols), BF16))
    w_out_specs, w_out_shapes = [], []
    if cast_weights:
        w_out_specs = [_full((D_MODEL, IN_WIDTH)), _full((D_MODEL, D_MODEL))]
        w_out_shapes = [jax.ShapeDtypeStruct((D_MODEL, IN_WIDTH), BF16), jax.ShapeDtypeStruct((D_MODEL, D_MODEL), BF16)]
    return pl.pallas_call(
        kernel,
        grid=(n_steps + 1,),
        in_specs=in_specs,
        out_specs=[pl.BlockSpec((rows, D_MODEL), prev), kv_spec, kv_spec] + w_out_specs + side_out_specs,
        out_shape=[jax.ShapeDtypeStruct((n_seq * seq, D_MODEL), F32), kv_shape, kv_shape] + w_out_shapes
        + side_out_shapes,
        scratch_shapes=_mixer_scratch(rows, 0, slots=(2,)),
        compiler_params=pltpu.CompilerParams(dimension_semantics=("arbitrary",), vmem_limit_bytes=VMEM_LIMIT),
        name="mixer_ctx",
    )(*args)


def _mixer_lat(x, l, n_seq, seq, mod, w_in_bf, w_out_bf, small, cos, sin, kc, vc):
    n_cache = kc.shape[2]
    kernel = functools.partial(_mixer_kernel, layer=l, seq=seq, n_seq=1, n_cache=n_cache, rope=True,
                               cast_weights=False, stack_kv=False)
    cache_spec = pl.BlockSpec((None, None, n_cache, KV_WIDTH), lambda b: (b, l, 0, 0))
    return pl.pallas_call(
        kernel,
        grid=(n_seq,),
        in_specs=([pl.BlockSpec((seq, D_MODEL), lambda b: (b, 0)),
                   pl.BlockSpec((None, None, N_MOD, D_MODEL), lambda b: (l, 1 + b, 0, 0)),
                   _resident((D_MODEL, IN_WIDTH)), _resident((D_MODEL, D_MODEL))]
                  + [_full(a.shape) if a.ndim == 2 else _layer_block(a, l) for a in small]
                  + [_full((seq, LANES)), _full((seq, LANES)), cache_spec, cache_spec]),
        out_specs=pl.BlockSpec((seq, D_MODEL), lambda b: (b, 0)),
        out_shape=jax.ShapeDtypeStruct((n_seq * seq, D_MODEL), F32),
        scratch_shapes=_mixer_scratch(seq, n_cache),
        compiler_params=pltpu.CompilerParams(dimension_semantics=("arbitrary",), vmem_limit_bytes=VMEM_LIMIT),
        name="mixer_lat",
    )(x, mod, w_in_bf, w_out_bf, *small, cos, sin, kc, vc)


def _two_stream_specs(block_rows, n_a, block_of):
    spec_a = pl.BlockSpec((block_rows, D_MODEL), lambda *ids: (jnp.minimum(block_of(*ids), n_a - 1), 0))
    spec_b = pl.BlockSpec((block_rows, D_MODEL), lambda *ids: (jnp.maximum(block_of(*ids) - n_a, 0), 0))
    return [spec_a, spec_b]


def _group_of_block(blk, block_rows, n_a, lat_seq):
    return jnp.where(blk < n_a, 0, 1 + (blk - n_a) // (lat_seq // block_rows))


def _ffn_kernel(xa_ref, xb_ref, mod_ref, w1_ref, w3_ref, w2_ref, g_ref, b_ref, oa_ref, ob_ref, *, n_a, layer):
    is_a = pl.program_id(0) < n_a
    x = jnp.where(is_a, xa_ref[...], xb_ref[...])
    h = (_standardize(x) * (1.0 + mod_ref[4:5, :]) + mod_ref[3:4, :]).astype(BF16)
    acc = jnp.zeros(x.shape, F32)
    for c in range(D_FF // FFN_COLS):
        cols = slice(c * FFN_COLS, (c + 1) * FFN_COLS)
        a = _dot(h, w1_ref[:, cols])
        b = _dot(h, w3_ref[:, cols])
        acc = acc + _dot((_silu(a) * b).astype(BF16), w2_ref[cols, :])
    y = DEEPNORM_ALPHA * x + mod_ref[5:6, :] * acc
    out = _standardize(y) * g_ref[layer:layer + 1, :] + b_ref[layer:layer + 1, :]

    @pl.when(is_a)
    def _():
        oa_ref[...] = out

    @pl.when(jnp.logical_not(is_a))
    def _():
        ob_ref[...] = out


def _ffn_dense(xa, xb, l, mods, w1, w3, w2, g, b, lat_seq):
    n_a, n_b = xa.shape[0] // FFN_TILE, xb.shape[0] // FFN_TILE
    x_specs = _two_stream_specs(FFN_TILE, n_a, lambda t: t)
    return pl.pallas_call(
        functools.partial(_ffn_kernel, n_a=n_a, layer=l),
        grid=(n_a + n_b,),
        in_specs=x_specs + [
            pl.BlockSpec((None, None, N_MOD, D_MODEL),
                         lambda t: (l, _group_of_block(t, FFN_TILE, n_a, lat_seq), 0, 0)),
            _resident(w1.shape), _resident(w3.shape), _resident(w2.shape), _full(g.shape), _full(b.shape)],
        out_specs=x_specs,
        out_shape=[jax.ShapeDtypeStruct(xa.shape, F32), jax.ShapeDtypeStruct(xb.shape, F32)],
        compiler_params=pltpu.CompilerParams(dimension_semantics=("arbitrary",), vmem_limit_bytes=VMEM_LIMIT),
        name="ffn_dense",
    )(xa, xb, mods, w1, w3, w2, g, b)


def _router_kernel(xa_ref, xb_ref, mod_ref, rw_ref, before_ref, dest_ref, gate_ref, start_ref, padded_ref, *,
                   n_a, lat_seq):
    c = pl.program_id(0)
    w_hi, w_lo = _split(rw_ref[...])
    n_blocks = MOE_CHUNK // ROUTER_BLOCK

    def store_per_token(ref, slot, val):
        for j in range(MOE_CHUNK // LANES):
            ref[slot, j:j + 1, :] = val[:, j * LANES:(j + 1) * LANES]

    parts = []
    for blk in range(n_blocks):
        rows = pl.ds(blk * ROUTER_BLOCK, ROUTER_BLOCK)
        mod = mod_ref[_group_of_block(c * n_blocks + blk, ROUTER_BLOCK, n_a * n_blocks, lat_seq)]
        x = jnp.where(c < n_a, xa_ref[rows, :], xb_ref[rows, :])
        h = _standardize(x) * (1.0 + mod[4:5, :]) + mod[3:4, :]
        h_hi, h_lo = _split(h)
        parts.append(_dot_nt(w_hi, h_hi) + _dot_nt(w_hi, h_lo) + _dot_nt(w_lo, h_hi))
    logits = jnp.concatenate(parts, axis=1)
    eid = lax.broadcasted_iota(I32, logits.shape, 0).astype(F32)
    m1 = jnp.max(logits, axis=0, keepdims=True)
    i1 = jnp.min(jnp.where(logits == m1, eid, float(N_EXPERTS)), axis=0, keepdims=True)
    oh1 = eid == i1
    rest = jnp.where(oh1, -jnp.inf, logits)
    m2 = jnp.max(rest, axis=0, keepdims=True)
    i2 = jnp.min(jnp.where(rest == m2, eid, float(N_EXPERTS)), axis=0, keepdims=True)
    oh2 = eid == i2
    e = jnp.exp(m2 - m1)
    store_per_token(gate_ref, 0, 1.0 / (1.0 + e))
    store_per_token(gate_ref, 1, e / (1.0 + e))
    sel = jnp.where(oh1 | oh2, 1.0, 0.0)
    ranks = []
    seen = jnp.zeros((N_EXPERTS, 1), F32)
    for blk in range(n_blocks):
        s_blk = sel[:, blk * ROUTER_BLOCK:(blk + 1) * ROUTER_BLOCK]
        ranks.append(_dot(s_blk.astype(BF16), before_ref[...]) + seen)
        seen = seen + jnp.sum(s_blk, axis=1, keepdims=True)
    rank = jnp.concatenate(ranks, axis=1)
    eid_out = lax.broadcasted_iota(I32, (1, LANES), 1).astype(F32)
    start = jnp.zeros(sel.shape, F32)
    start_out = jnp.zeros((1, LANES), F32)
    padded_out = jnp.zeros((1, LANES), F32)
    for ex in range(N_EXPERTS):
        cnt = jnp.sum(sel[ex:ex + 1, :], axis=1, keepdims=True)
        padded = jnp.ceil(cnt * (1.0 / MOE_GRAN)) * MOE_GRAN
        start = start + jnp.where(eid > ex, padded, 0.0)
        start_out = start_out + jnp.where(eid_out > ex, padded, 0.0)
        padded_out = padded_out + jnp.where(eid_out == ex, padded, 0.0)
    row = (start + rank) * SUBLANES
    store_per_token(dest_ref, 0, jnp.sum(jnp.where(oh1, row, 0.0), axis=0, keepdims=True).astype(I32))
    store_per_token(dest_ref, 1, jnp.sum(jnp.where(oh2, row, 0.0), axis=0, keepdims=True).astype(I32))

    @pl.when(c == 0)
    def _():
        start_ref[...] = jnp.zeros(start_ref.shape, I32)
        padded_ref[...] = jnp.zeros(padded_ref.shape, I32)

    start_ref[pl.ds(c, 1), :] = start_out.astype(I32)
    padded_ref[pl.ds(c, 1), :] = padded_out.astype(I32)


def _router(xa, xb, l, mods, rw_t, lat_seq):
    n_a = xa.shape[0] // MOE_CHUNK
    n_chunks = n_a + xb.shape[0] // MOE_CHUNK
    tok = np.arange(ROUTER_BLOCK)
    before = jnp.asarray(tok[:, None] < tok[None, :], BF16)
    table_rows = -(-n_chunks // SUBLANES) * SUBLANES
    return pl.pallas_call(
        functools.partial(_router_kernel, n_a=n_a, lat_seq=lat_seq),
        grid=(n_chunks,),
        in_specs=_two_stream_specs(MOE_CHUNK, n_a, lambda c: c) + [
                  _layer_block(mods, l),
                  _full((N_EXPERTS, D_MODEL)), _full((ROUTER_BLOCK, ROUTER_BLOCK))],
        out_specs=[pl.BlockSpec((None, 2, MOE_CHUNK // LANES, LANES), lambda c: (c, 0, 0, 0)),
                   pl.BlockSpec((None, 2, MOE_CHUNK // LANES, LANES), lambda c: (c, 0, 0, 0)),
                   _full((table_rows, LANES)), _full((table_rows, LANES))],
        out_shape=[jax.ShapeDtypeStruct((n_chunks, 2, MOE_CHUNK // LANES, LANES), I32),
                   jax.ShapeDtypeStruct((n_chunks, 2, MOE_CHUNK // LANES, LANES), F32),
                   jax.ShapeDtypeStruct((table_rows, LANES), I32),
                   jax.ShapeDtypeStruct((table_rows, LANES), I32)],
        compiler_params=pltpu.CompilerParams(dimension_semantics=("arbitrary",), vmem_limit_bytes=VMEM_LIMIT),
        name="moe_router",
    )(xa, xb, mods, rw_t, before)


def _token_rows(first_row):
    return pl.ds(pl.multiple_of(first_row, SUBLANES), SUBLANES)


def _store_token_major(ref, tok0, val):
    for cc in range(D_MODEL // LANES):
        ref[pl.ds(tok0 * SUBLANES + cc, val.shape[0], stride=SUBLANES), :] = val[:, cc * LANES:(cc + 1) * LANES]


def _load_token_major(ref, tok0, n):
    return jnp.concatenate(
        [ref[pl.ds(tok0 * SUBLANES + cc, n, stride=SUBLANES), :] for cc in range(D_MODEL // LANES)], axis=1)


def _moe_kernel(start_ref, padded_ref, xa_ref, xb_ref, mod_ref, dest_ref, gate_ref, w13_ref, w2_ref,
                g_ref, b_ref, oa_ref, ob_ref, tok_scr, rows_scr, *, n_a, layer):
    c = pl.program_id(0)
    j = pl.program_id(1)
    is_a = c < n_a
    first_expert_step = MOE_TOK_STEPS
    first_combine_step = MOE_TOK_STEPS + N_EXPERTS

    @pl.when((c == 0) & (j == 0))
    def _init():
        rows_scr[...] = jnp.zeros(rows_scr.shape, F32)

    @pl.when(j < first_expert_step)
    def _dispatch():
        t0 = j * MOE_TOK_BLOCK
        for r in range(MOE_TOK_BLOCK // MOE_SUB_BLOCK):
            rows = pl.ds(r * MOE_SUB_BLOCK, MOE_SUB_BLOCK)
            x = jnp.where(is_a, xa_ref[rows, :], xb_ref[rows, :])
            h = _standardize(x) * (1.0 + mod_ref[4:5, :]) + mod_ref[3:4, :]
            _store_token_major(tok_scr, r * MOE_SUB_BLOCK, h)
            for t in range(r * MOE_SUB_BLOCK, (r + 1) * MOE_SUB_BLOCK):
                row = tok_scr[pl.ds(t * SUBLANES, SUBLANES), :]
                rows_scr[_token_rows(dest_ref[t0 + t]), :] = row
                rows_scr[_token_rows(dest_ref[MOE_CHUNK + t0 + t]), :] = row

    def experts(row0, m):
        xin = _load_token_major(rows_scr, row0, m).astype(BF16)
        ab = _dot(xin, w13_ref[...])
        a, b = ab[:, :D_FF_EXPERT], ab[:, D_FF_EXPERT:]
        y = _dot((_silu(a) * b).astype(BF16), w2_ref[...])
        _store_token_major(rows_scr, row0, y)

    @pl.when((j >= first_expert_step) & (j < first_combine_step))
    def _experts():
        region = c * LANES + (j - first_expert_step)
        start = start_ref[region]
        padded = padded_ref[region]
        n_full = lax.shift_right_logical(padded, MOE_BLOCK.bit_length() - 1)

        def body(i, carry):
            experts(start + 2 * i * MOE_BLOCK, MOE_BLOCK)
            experts(start + (2 * i + 1) * MOE_BLOCK, MOE_BLOCK)
            return carry
        lax.fori_loop(0, lax.shift_right_logical(n_full, 1), body, 0)
        pl.when((n_full & 1) == 1)(functools.partial(experts, start + (n_full - 1) * MOE_BLOCK, MOE_BLOCK))
        for m in range(MOE_GRAN, MOE_BLOCK, MOE_GRAN):
            pl.when(padded - n_full * MOE_BLOCK == m)(
                functools.partial(experts, start + n_full * MOE_BLOCK, m))

    @pl.when(j >= first_combine_step)
    def _combine():
        t0 = (j - first_combine_step) * MOE_TOK_BLOCK
        outs = []
        for r in range(MOE_TOK_BLOCK // MOE_SUB_BLOCK):
            for t in range(r * MOE_SUB_BLOCK, (r + 1) * MOE_SUB_BLOCK):
                y0 = rows_scr[_token_rows(dest_ref[t0 + t]), :]
                y1 = rows_scr[_token_rows(dest_ref[MOE_CHUNK + t0 + t]), :]
                tok_scr[pl.ds(t * SUBLANES, SUBLANES), :] = (gate_ref[t0 + t] * y0
                                                             + gate_ref[MOE_CHUNK + t0 + t] * y1)
            rows = pl.ds(r * MOE_SUB_BLOCK, MOE_SUB_BLOCK)
            ffn = _load_token_major(tok_scr, r * MOE_SUB_BLOCK, MOE_SUB_BLOCK)
            x = jnp.where(is_a, xa_ref[rows, :], xb_ref[rows, :])
            y = DEEPNORM_ALPHA * x + mod_ref[5:6, :] * ffn
            outs.append(_standardize(y) * g_ref[layer:layer + 1, :] + b_ref[layer:layer + 1, :])

        @pl.when(is_a)
        def _():
            for r, out in enumerate(outs):
                oa_ref[pl.ds(r * MOE_SUB_BLOCK, MOE_SUB_BLOCK), :] = out

        @pl.when(jnp.logical_not(is_a))
        def _():
            for r, out in enumerate(outs):
                ob_ref[pl.ds(r * MOE_SUB_BLOCK, MOE_SUB_BLOCK), :] = out


def _ffn_moe(xa, xb, l, mods, rw_t, w13, w2, g, b, lat_seq):
    n_a = xa.shape[0] // MOE_CHUNK
    n_chunks = n_a + xb.shape[0] // MOE_CHUNK
    n_a_blocks = n_a * MOE_TOK_STEPS
    dest, gates, start, padded = _router(xa, xb, l, mods, rw_t, lat_seq)
    first_expert_step = MOE_TOK_STEPS
    first_combine_step = MOE_TOK_STEPS + N_EXPERTS

    def token_block(c, j, *_):
        blk = jnp.where(j < first_combine_step, jnp.minimum(j, MOE_TOK_STEPS - 1), j - first_combine_step)
        return c * MOE_TOK_STEPS + blk

    def out_token_block(c, j, *_):
        return c * MOE_TOK_STEPS + jnp.maximum(j - first_combine_step, 0)

    def mod_of(c, j, *_):
        return (l, _group_of_block(token_block(c, j), MOE_TOK_BLOCK, n_a_blocks, lat_seq), 0, 0)

    def expert_of(c, j, *_):
        return (jnp.clip(j - first_expert_step, 0, N_EXPERTS - 1), 0, 0)

    grid_spec = pltpu.PrefetchScalarGridSpec(
        num_scalar_prefetch=2,
        grid=(n_chunks, MOE_STEPS),
        in_specs=_two_stream_specs(MOE_TOK_BLOCK, n_a_blocks, token_block) + [
                  pl.BlockSpec((None, None, N_MOD, D_MODEL), mod_of),
                  pl.BlockSpec((2 * MOE_CHUNK,), lambda c, j, *_: (c,), memory_space=pltpu.SMEM),
                  pl.BlockSpec((2 * MOE_CHUNK,), lambda c, j, *_: (c,), memory_space=pltpu.SMEM),
                  pl.BlockSpec((None, D_MODEL, 2 * D_FF_EXPERT), expert_of),
                  pl.BlockSpec((None, D_FF_EXPERT, D_MODEL), expert_of),
                  pl.BlockSpec(g.shape, lambda c, j, *_: (0, 0)), pl.BlockSpec(b.shape, lambda c, j, *_: (0, 0))],
        out_specs=_two_stream_specs(MOE_TOK_BLOCK, n_a_blocks, out_token_block),
        scratch_shapes=[pltpu.VMEM((MOE_TOK_BLOCK * SUBLANES, LANES), F32),
                        pltpu.VMEM((MOE_ROWS * SUBLANES, LANES), F32)],
    )
    return pl.pallas_call(
        functools.partial(_moe_kernel, n_a=n_a, layer=l),
        grid_spec=grid_spec,
        out_shape=[jax.ShapeDtypeStruct(xa.shape, F32), jax.ShapeDtypeStruct(xb.shape, F32)],
        compiler_params=pltpu.CompilerParams(
            dimension_semantics=("arbitrary", "arbitrary"), vmem_limit_bytes=VMEM_LIMIT),
        name="moe_experts",
    )(start.reshape(-1), padded.reshape(-1), xa, xb, mods,
      dest.reshape(-1), gates.reshape(-1), w13, w2, g, b)


def _rope_tables(n_tokens):
    t = np.arange(n_tokens)
    row = (t // GRID_W).astype(np.float32)
    col = (t % GRID_W).astype(np.float32)
    inv_freq = (np.float32(ROPE_THETA) ** (-np.arange(0, AXIS_ROT, 2, dtype=np.float32) / AXIS_ROT)).astype(np.float32)
    ang_r = row[:, None] * inv_freq
    ang_c = col[:, None] * inv_freq
    cos = np.concatenate([np.cos(ang_r), np.cos(ang_r), np.cos(ang_c), np.cos(ang_c)], axis=1)
    sin = np.concatenate([-np.sin(ang_r), np.sin(ang_r), -np.sin(ang_c), np.sin(ang_c)], axis=1)
    return jnp.asarray(np.tile(cos, (1, 2)), F32), jnp.asarray(np.tile(sin, (1, 2)), F32)


def kernel(x_prompt, x_sample, cache_k, cache_v, c, c_ctx, ada_w, ada_b, w_in, q_norm_g, k_norm_g, conv_w, conv_b, sgu_norm_g, sgu_w, sgu_b, w_out, ln1_g, ln1_b, ln2_g, ln2_b, ffn_w1, ffn_w3, ffn_w2, router_w, moe_w1, moe_w3, moe_w2):
    batch, seq, _ = x_prompt.shape
    dec_batch, dec_seq, _ = x_sample.shape
    past_len = cache_k.shape[2]
    n_ctx = batch * seq
    n_lat = dec_batch * dec_seq
    assert DEPTH == 2 and 1 + dec_batch <= SUBLANES
    assert seq == ATT_BLOCK and CTX_SEQS_PER_STEP * seq == ROW_BLOCK and batch % CTX_SEQS_PER_STEP == 0
    assert dec_seq % ROW_BLOCK == 0 and dec_seq & (dec_seq - 1) == 0
    assert n_ctx % MOE_CHUNK == 0 and n_lat % MOE_CHUNK == 0
    assert dec_seq % ROUTER_BLOCK == 0 and dec_seq % MOE_TOK_BLOCK == 0 and dec_seq % FFN_TILE == 0

    mod = _modulation(c_ctx[None, :], c, ada_w, ada_b)
    mod = mod.reshape(DEPTH, SUBLANES, N_MOD, D_MODEL)

    lane_id = np.arange(GROUP_TILE) // HEAD_DIM
    ones_bd = jnp.asarray(lane_id[:, None] == lane_id[None, :], BF16)
    cos, sin = _rope_tables(dec_seq)
    small = (
        q_norm_g, k_norm_g,
        conv_b, sgu_norm_g, ln1_g, ln1_b,
        ones_bd, conv_w,
        jnp.swapaxes(sgu_w, 1, 2).reshape(DEPTH, CHUNK, SGU_HEADS * CHUNK).astype(BF16),
        jnp.repeat(jnp.swapaxes(sgu_b, 1, 2), HEAD_DIM, axis=2),
    )
    kc = cache_k.reshape(dec_batch, DEPTH, past_len, KV_WIDTH)
    vc = cache_v.reshape(dec_batch, DEPTH, past_len, KV_WIDTH)

    xs = [x_prompt.reshape(n_ctx, D_MODEL), x_sample.reshape(n_lat, D_MODEL)]
    assert DEPTH == 2 and ffn_w1.shape[0] == 1 and moe_w1.shape[0] == 1
    side_casts = [[(ffn_w1[0],), (ffn_w3[0],), (ffn_w2[0],), ((w_in, 1),), ((w_out, 1),), (moe_w2.reshape(-1, D_MODEL),)],
                  [(moe_w1.reshape(-1, D_FF_EXPERT), moe_w3.reshape(-1, D_FF_EXPERT))]]
    kv = None
    for l in range(DEPTH):
        if l == 0:
            x_ctx, k_ctx, v_ctx, w_in_bf, w_out_bf, *ffn_bf, w_in_next, w_out_next, w2_bf = _mixer_ctx(
                xs[0], l, batch, seq, mod, w_in, w_out, small, kv, side_casts[l])
        else:
            w_in_bf, w_out_bf = w_in_next, w_out_next
            x_ctx, k_ctx, v_ctx, w13_bf = _mixer_ctx(
                xs[0], l, batch, seq, mod, w_in_bf, w_out_bf, small, kv, side_casts[l])
        kv = (k_ctx, v_ctx)
        x_lat = _mixer_lat(xs[1], l, dec_batch, dec_seq, mod, w_in_bf, w_out_bf, small, cos, sin, kc, vc)
        if l % 2 == 0:
            xs = _ffn_dense(x_ctx, x_lat, l, mod, *ffn_bf, ln2_g, ln2_b, dec_seq)
        else:
            ws = (w13_bf.reshape(N_EXPERTS, D_MODEL, 2 * D_FF_EXPERT), w2_bf.reshape(N_EXPERTS, D_FF_EXPERT, D_MODEL))
            xs = _ffn_moe(x_ctx, x_lat, l, mod, router_w[l // 2].T, *ws, ln2_g, ln2_b, dec_seq)
    y_p = xs[0].reshape(batch, seq, D_MODEL)
    y_s = xs[1].reshape(dec_batch, dec_seq, D_MODEL)
    new_k = kv[0].reshape(batch, DEPTH, seq, N_KV_HEADS, HEAD_DIM)
    new_v = kv[1].reshape(batch, DEPTH, seq, N_KV_HEADS, HEAD_DIM)
    return (y_p, y_s, new_k, new_v)
```

```python
import functools

import numpy as np
import jax
import jax.numpy as jnp
from jax import lax
from jax.experimental import pallas as pl
from jax.experimental.pallas import tpu as pltpu

F32 = jnp.float32
BF16 = jnp.bfloat16
I32 = jnp.int32

D_MODEL = 1024
DEPTH = 2
GRID_W = 64
HEAD_DIM = 64
N_Q_HEADS = 8
N_KV_HEADS = 2
ATTN_WIDTH = N_Q_HEADS * HEAD_DIM
KV_WIDTH = N_KV_HEADS * HEAD_DIM
ATTN_SCALE = HEAD_DIM ** -0.5
ROPE_THETA = 10000.0
AXIS_ROT = HEAD_DIM // 2
CONV_WIDTH = 256
SGU_WIDTH = 256
SGU_HEADS = 4
CHUNK = 128
IN_WIDTH = 2048
D_FF = 2816
N_EXPERTS = 8
D_FF_EXPERT = 1408
EPS = 1e-6
DEEPNORM_ALPHA = (2 * DEPTH) ** 0.25

LANES = 128
SUBLANES = 8
ROW_BLOCK = 512
GROUP_TILE = 256
ATT_BLOCK = 256
CTX_SEQS_PER_STEP = 2
FFN_TILE = 512
FFN_COLS = 256
MOE_CHUNK = 2048
ROUTER_BLOCK = 1024
MOE_GRAN = 128
MOE_BLOCK = 256
MOE_TOK_BLOCK = 512
MOE_SUB_BLOCK = 256
MOE_TOK_STEPS = MOE_CHUNK // MOE_TOK_BLOCK
MOE_STEPS = 2 * MOE_TOK_STEPS + N_EXPERTS
MOE_ROWS = 2 * MOE_CHUNK + N_EXPERTS * MOE_GRAN
VMEM_LIMIT = 60 * 1024 * 1024

_Q0, _K0, _V0, _CI0, _CB0, _CC0, _SU0, _SV0 = 0, 512, 640, 768, 1024, 1280, 1536, 1792


def _dot(a, b):
    return jnp.dot(a, b, preferred_element_type=F32)


def _dot_nt(a, b):
    return lax.dot_general(a, b, (((1,), (1,)), ((), ())), preferred_element_type=F32)


def _split(x):
    hi = x.astype(BF16)
    lo = (x - hi.astype(F32)).astype(BF16)
    return hi, lo


def _group_sum(x, ones_bd):
    outs = []
    for c0 in range(0, x.shape[1], GROUP_TILE):
        width = min(GROUP_TILE, x.shape[1] - c0)
        outs.append(_dot(x[:, c0:c0 + width].astype(BF16), ones_bd[:width, :width]))
    return outs[0] if len(outs) == 1 else jnp.concatenate(outs, axis=1)


def _standardize(x):
    mu = jnp.mean(x, axis=-1, keepdims=True)
    d = x - mu
    return d * lax.rsqrt(jnp.mean(d * d, axis=-1, keepdims=True) + EPS)


def _silu(x):
    return x / (1.0 + jnp.exp(-x))


def _modulation_kernel(cctx_ref, c_ref, w_ref, b_ref, o_ref):
    n_pad = SUBLANES - 1 - c_ref.shape[0]
    cond = jnp.concatenate([cctx_ref[...], c_ref[...], jnp.zeros((n_pad, D_MODEL), F32)], axis=0)
    s_hi, s_lo = _split(_silu(cond))
    w_hi, w_lo = _split(w_ref[...])
    bias = b_ref[pl.ds(pl.program_id(0), 1), :]
    o_ref[...] = _dot(s_hi, w_hi) + _dot(s_hi, w_lo) + _dot(s_lo, w_hi) + bias


def _modulation(c_ctx, c, ada_w, ada_b):
    n_out = ada_w.shape[-1]
    tn = 1536
    return pl.pallas_call(
        _modulation_kernel,
        grid=(DEPTH, n_out // tn),
        in_specs=[
            pl.BlockSpec(c_ctx.shape, lambda l, j: (0, 0)),
            pl.BlockSpec(c.shape, lambda l, j: (0, 0)),
            pl.BlockSpec((None, D_MODEL, tn), lambda l, j: (l, 0, j)),
            pl.BlockSpec((DEPTH, tn), lambda l, j: (0, j)),
        ],
        out_specs=pl.BlockSpec((None, SUBLANES, tn), lambda l, j: (l, 0, j)),
        out_shape=jax.ShapeDtypeStruct((DEPTH, SUBLANES, n_out), F32),
        compiler_params=pltpu.CompilerParams(
            dimension_semantics=("arbitrary", "arbitrary"), vmem_limit_bytes=VMEM_LIMIT),
        name="modulation",
    )(c_ctx, c, ada_w, ada_b)


def _rope(x, cos, sin_signed):
    w = x.shape[1]
    lane = lax.broadcasted_iota(I32, x.shape, 1)
    first_half = (lane & 31) < 16
    partner = jnp.where(first_half, pltpu.roll(x, w - 16, 1), pltpu.roll(x, 16, 1))
    return x * cos + partner * sin_signed


def _head_variants(x):
    lane = lax.broadcasted_iota(I32, x.shape, 1)
    lo = lane < HEAD_DIM
    xr = pltpu.roll(x, HEAD_DIM, 1)
    zero = jnp.zeros_like(x)
    return (jnp.where(lo, x, zero).astype(BF16), jnp.where(lo, zero, xr).astype(BF16),
            jnp.where(lo, xr, zero).astype(BF16), jnp.where(lo, zero, x).astype(BF16))


def _mixer_kernel(*refs, layer, seq, n_seq, n_cache, rope, cast_weights, stack_kv, side_jobs=(), n_steps=None):
    refs = list(refs)
    skew = n_steps is not None

    def take(n):
        out, refs[:] = refs[:n], refs[n:]
        return out

    (x_ref,) = take(1)
    xres_ref = take(1)[0] if skew else x_ref
    mod_ref, win_ref, wout_ref = take(3)
    qkg_ref, convb_ref, sgug_ref, ln1g_ref, ln1b_ref = take(5)
    ones_ref, convw_ref, sguw_ref, sgub_ref = take(4)
    this = slice(layer, layer + 1)
    if rope:
        cos_ref, sin_ref, kc_ref, vc_ref = take(4)
    if stack_kv:
        kprev_ref, vprev_ref = take(2)
    side_in = [take(n_src) for n_src in side_jobs]
    (x1_ref,) = take(1)
    if not rope:
        k_ref, v_ref = take(2)
    if cast_weights:
        winb_ref, woutb_ref = take(2)

        @pl.when(pl.program_id(0) == 0)
        def _cast():
            winb_ref[...] = win_ref[...].astype(BF16)
            woutb_ref[...] = wout_ref[...].astype(BF16)
        win_ref, wout_ref = winb_ref, woutb_ref
    side_out = take(len(side_jobs))
    scratch = take(7)
    (mix_scr,) = take(1)
    n_rows = n_seq * seq
    assert n_cache == 0 or n_seq == 1

    def loop(n, body):
        if n == 1:
            body(0)
        else:
            def step(r, carry):
                body(r)
                return carry
            lax.fori_loop(0, n, step, 0, unroll=2)

    def block(r, size):
        if isinstance(r, int):
            return pl.ds(r * size, size)
        return pl.ds(pl.multiple_of(r * size, size), size)

    def project_phase(scr):
        q_scr, kvar_scr, vvar_scr, u_scr, cb_scr, su_scr, vn_scr = scr
        for srcs, dst in zip(side_in, side_out):
            col = 0
            for src in srcs:
                dst[:, col:col + src.shape[1]] = src[...].astype(BF16)
                col += src.shape[1]
        if n_cache:
            for i, var in enumerate(_head_variants(kc_ref[...])):
                kvar_scr[i, pl.ds(seq, n_cache), :] = var
            for i, var in enumerate(_head_variants(vc_ref[...])):
                vvar_scr[i, pl.ds(seq, n_cache), :] = var

        def project(r):
            rows = block(r, ROW_BLOCK)
            x = x_ref[rows, :]
            h = _standardize(x) * (1.0 + mod_ref[1:2, :]) + mod_ref[0:1, :]
            z = _dot(h.astype(BF16), win_ref[...])
            ones_bd = ones_ref[...]
            zq = z[:, _Q0:_K0]
            q = zq * lax.rsqrt(_group_sum(zq * zq, ones_bd) * (1.0 / HEAD_DIM) + EPS) * qkg_ref[this, :ATTN_WIDTH]
            zk = z[:, _K0:_V0]
            k = zk * lax.rsqrt(_group_sum(zk * zk, ones_bd) * (1.0 / HEAD_DIM) + EPS) * qkg_ref[this, ATTN_WIDTH:]
            v = z[:, _V0:_CI0]
            if rope:
                cos = cos_ref[rows, :]
                sin = sin_ref[rows, :]
                q = _rope(q, jnp.concatenate([cos] * 4, axis=1), jnp.concatenate([sin] * 4, axis=1))
                k = _rope(k, cos, sin)
            elif stack_kv:
                for s in range(ROW_BLOCK // seq):
                    sub = slice(s * seq, (s + 1) * seq)
                    k_ref[s, 0] = kprev_ref[sub, :]
                    v_ref[s, 0] = vprev_ref[sub, :]
                    k_ref[s, 1] = k[sub, :]
                    v_ref[s, 1] = v[sub, :]
            else:
                k_ref[rows, :] = k
                v_ref[rows, :] = v
            q_scr[rows, :] = (q * ATTN_SCALE).astype(BF16)
            for i, var in enumerate(_head_variants(k)):
                kvar_scr[i, rows, :] = var
            for i, var in enumerate(_head_variants(v)):
                vvar_scr[i, rows, :] = var
            u_scr[rows, :] = z[:, _CC0:_SU0] * z[:, _CI0:_CB0]
            cb_scr[rows, :] = z[:, _CB0:_CC0]
            su_scr[rows, :] = z[:, _SU0:_SV0]
            sv = z[:, _SV0:IN_WIDTH]
            d = sv - _group_sum(sv, ones_bd) * (1.0 / HEAD_DIM)
            vn = d * lax.rsqrt(_group_sum(d * d, ones_bd) * (1.0 / HEAD_DIM) + EPS) * sgug_ref[this, :]
            vn_scr[rows, :] = vn.astype(BF16)

        loop(n_rows // ROW_BLOCK, project)

    def consume_phase(scr):
        q_scr, kvar_scr, vvar_scr, u_scr, cb_scr, su_scr, vn_scr = scr
        u = u_scr[...]
        pos = lax.broadcasted_iota(I32, u.shape, 0) & (seq - 1)
        up = jnp.where(pos == 0, 0.0, pltpu.roll(u, 1, 0))
        dn = jnp.where(pos == seq - 1, 0.0, pltpu.roll(u, n_rows - 1, 0))
        conv = up * convw_ref[0:1, :] + u * convw_ref[1:2, :] + dn * convw_ref[2:3, :] + convb_ref[this, :]
        mix_scr[:, ATTN_WIDTH:ATTN_WIDTH + CONV_WIDTH] = (cb_scr[...] * conv).astype(BF16)

        for n in range(n_rows // CHUNK):
            rows = pl.ds(n * CHUNK, CHUNK)
            vn = vn_scr[rows, :]
            lane = lax.broadcasted_iota(I32, vn.shape, 1)
            per_head = [jnp.where((lane >= hd * HEAD_DIM) & (lane < (hd + 1) * HEAD_DIM), vn, jnp.zeros_like(vn))
                        for hd in range(SGU_HEADS)]
            s = sgub_ref[...] + _dot(sguw_ref[...], jnp.concatenate(per_head, axis=0))
            mix_scr[rows, ATTN_WIDTH + CONV_WIDTH:] = (su_scr[rows, :] * s).astype(BF16)

        def attend(s, r):
            rows = block(s * (seq // ATT_BLOCK) + r, ATT_BLOCK)
            keys = pl.ds(s * seq, seq + n_cache)
            for pair in range(N_Q_HEADS // 2):
                qp = q_scr[rows, pair * LANES:(pair + 1) * LANES]
                kv = pair // (N_Q_HEADS // N_KV_HEADS // 2)
                acc = jnp.zeros((ATT_BLOCK, LANES), F32)
                for parity in range(2):
                    sc = _dot_nt(qp, kvar_scr[2 * kv + parity, keys, :])
                    p = jnp.exp(sc - jnp.max(sc, axis=1, keepdims=True))
                    denom = jnp.sum(p, axis=1, keepdims=True)
                    acc = acc + _dot(p.astype(BF16), vvar_scr[2 * kv + parity, keys, :]) / denom
                mix_scr[rows, pair * LANES:(pair + 1) * LANES] = acc.astype(BF16)

        for s in range(n_seq):
            loop(seq // ATT_BLOCK, functools.partial(attend, s))

        def finish(r):
            rows = block(r, ROW_BLOCK)
            mix = _dot(mix_scr[rows, :], wout_ref[...])
            y = DEEPNORM_ALPHA * xres_ref[rows, :] + mod_ref[2:3, :] * mix
            x1_ref[rows, :] = _standardize(y) * ln1g_ref[this, :] + ln1b_ref[this, :]

        loop(n_rows // ROW_BLOCK, finish)

    if not skew:
        project_phase(scratch)
        consume_phase(scratch)
        return

    i = pl.program_id(0)
    slot = lax.rem(i, 2)
    mine = [ref.at[slot] for ref in scratch]
    other = [ref.at[1 - slot] for ref in scratch]
    pl.when(i == 0)(functools.partial(project_phase, mine))

    @pl.when((i > 0) & (i < n_steps))
    def _():
        consume_phase(other)
        project_phase(mine)

    pl.when(i == n_steps)(functools.partial(consume_phase, other))


def _full(shape):
    n = len(shape)
    return pl.BlockSpec(shape, lambda *_: (0,) * n)


def _resident(shape):
    n = len(shape)
    return pl.BlockSpec(shape, lambda *_: (0,) * n, pipeline_mode=pl.Buffered(1))


def _layer_block(arr, l, resident=False):
    shape = arr.shape[1:]
    kw = dict(pipeline_mode=pl.Buffered(1)) if resident else {}
    return pl.BlockSpec((None,) + shape, lambda *_: (l,) + (0,) * len(shape), **kw)


def _mixer_scratch(n_rows, n_cache, slots=()):
    nk = n_rows + n_cache
    return [
        pltpu.VMEM(slots + (n_rows, ATTN_WIDTH), BF16),
        pltpu.VMEM(slots + (4, nk, LANES), BF16),
        pltpu.VMEM(slots + (4, nk, LANES), BF16),
        pltpu.VMEM(slots + (n_rows, CONV_WIDTH), F32),
        pltpu.VMEM(slots + (n_rows, CONV_WIDTH), F32),
        pltpu.VMEM(slots + (n_rows, SGU_WIDTH), F32),
        pltpu.VMEM(slots + (n_rows, SGU_WIDTH), BF16),
        pltpu.VMEM((n_rows, D_MODEL), BF16),
    ]


def _side_source(src):
    return src if isinstance(src, tuple) else (src, None)


def _mixer_ctx(x, l, n_seq, seq, mod, w_in, w_out, small, kv_prev=None, side_casts=()):
    per_step = CTX_SEQS_PER_STEP
    rows = per_step * seq
    stack_kv = kv_prev is not None
    cast_weights = w_in.dtype != BF16
    n_steps = n_seq // per_step
    kernel = functools.partial(_mixer_kernel, layer=l, seq=seq, n_seq=per_step, n_cache=0, rope=False,
                               cast_weights=cast_weights, stack_kv=stack_kv,
                               side_jobs=tuple(len(job) for job in side_casts), n_steps=n_steps)
    cur = lambda i: (jnp.minimum(i, n_steps - 1), 0)
    prev = lambda i: (jnp.maximum(i - 1, 0), 0)
    w_specs = ([_layer_block(w_in, l, resident=True), _layer_block(w_out, l, resident=True)] if cast_weights
               else [_resident(w_in.shape), _resident(w_out.shape)])
    in_specs = ([pl.BlockSpec((rows, D_MODEL), cur), pl.BlockSpec((rows, D_MODEL), prev),
                 pl.BlockSpec((None, None, SUBLANES, D_MODEL), lambda i: (l, 0, 0, 0))] + w_specs
                + [_full(a.shape) if a.ndim == 2 else _layer_block(a, l) for a in small])
    args = [x, x, mod, w_in, w_out, *small]
    if stack_kv:
        in_specs += [pl.BlockSpec((rows, KV_WIDTH), cur)] * 2
        args += list(kv_prev)
        kv_spec = pl.BlockSpec((per_step, DEPTH, seq, KV_WIDTH), lambda i: cur(i) + (0, 0))
        kv_shape = jax.ShapeDtypeStruct((n_seq, DEPTH, seq, KV_WIDTH), F32)
    else:
        kv_spec = pl.BlockSpec((rows, KV_WIDTH), cur)
        kv_shape = jax.ShapeDtypeStruct((n_seq * seq, KV_WIDTH), F32)
    side_out_specs, side_out_shapes = [], []
    for job in side_casts:
        sources = [_side_source(src) for src in job]
        n_rows = sources[0][0].shape[-2]
        n_cols = sum(arr.shape[-1] for arr, _ in sources)
        for arr, layer in sources:
            blk = (n_rows // n_steps, arr.shape[-1])
            if layer is None:
                in_specs.append(pl.BlockSpec(blk, cur))
            else:
                in_specs.append(pl.BlockSpec((None,) + blk, lambda i, layer=layer: (layer,) + cur(i)))
            args.append(arr)
        side_out_specs.append(pl.BlockSpec((n_rows // n_steps, n_cols), cur))
        side_out_shapes.append(jax.ShapeDtypeStruct((n_rows, n_cols), BF16))
    w_out_specs, w_out_shapes = [], []
    if cast_weights:
        w_out_specs = [_full((D_MODEL, IN_WIDTH)), _full((D_MODEL, D_MODEL))]
        w_out_shapes = [jax.ShapeDtypeStruct((D_MODEL, IN_WIDTH), BF16), jax.ShapeDtypeStruct((D_MODEL, D_MODEL), BF16)]
    return pl.pallas_call(
        kernel,
        grid=(n_steps + 1,),
        in_specs=in_specs,
        out_specs=[pl.BlockSpec((rows, D_MODEL), prev), kv_spec, kv_spec] + w_out_specs + side_out_specs,
        out_shape=[jax.ShapeDtypeStruct((n_seq * seq, D_MODEL), F32), kv_shape, kv_shape] + w_out_shapes
        + side_out_shapes,
        scratch_shapes=_mixer_scratch(rows, 0, slots=(2,)),
        compiler_params=pltpu.CompilerParams(dimension_semantics=("arbitrary",), vmem_limit_bytes=VMEM_LIMIT),
        name="mixer_ctx",
    )(*args)


def _mixer_lat(x, l, n_seq, seq, mod, w_in_bf, w_out_bf, small, cos, sin, kc, vc):
    n_cache = kc.shape[2]
    kernel = functools.partial(_mixer_kernel, layer=l, seq=seq, n_seq=1, n_cache=n_cache, rope=True,
                               cast_weights=False, stack_kv=False)
    cache_spec = pl.BlockSpec((None, None, n_cache, KV_WIDTH), lambda b: (b, l, 0, 0))
    return pl.pallas_call(
        kernel,
        grid=(n_seq,),
        in_specs=([pl.BlockSpec((seq, D_MODEL), lambda b: (b, 0)),
                   pl.BlockSpec((None, None, SUBLANES, D_MODEL), lambda b: (l, 1 + b, 0, 0)),
                   _resident((D_MODEL, IN_WIDTH)), _resident((D_MODEL, D_MODEL))]
                  + [_full(a.shape) if a.ndim == 2 else _layer_block(a, l) for a in small]
                  + [_full((seq, LANES)), _full((seq, LANES)), cache_spec, cache_spec]),
        out_specs=pl.BlockSpec((seq, D_MODEL), lambda b: (b, 0)),
        out_shape=jax.ShapeDtypeStruct((n_seq * seq, D_MODEL), F32),
        scratch_shapes=_mixer_scratch(seq, n_cache),
        compiler_params=pltpu.CompilerParams(dimension_semantics=("arbitrary",), vmem_limit_bytes=VMEM_LIMIT),
        name="mixer_lat",
    )(x, mod, w_in_bf, w_out_bf, *small, cos, sin, kc, vc)


def _two_stream_specs(block_rows, n_a, block_of):
    spec_a = pl.BlockSpec((block_rows, D_MODEL), lambda *ids: (jnp.minimum(block_of(*ids), n_a - 1), 0))
    spec_b = pl.BlockSpec((block_rows, D_MODEL), lambda *ids: (jnp.maximum(block_of(*ids) - n_a, 0), 0))
    return [spec_a, spec_b]


def _group_of_block(blk, block_rows, n_a, lat_seq):
    return jnp.where(blk < n_a, 0, 1 + (blk - n_a) // (lat_seq // block_rows))


def _ffn_kernel(xa_ref, xb_ref, mod_ref, w1_hbm, w3_hbm, w2_hbm, g_ref, b_ref, oa_ref, ob_ref,
                w1_ref, w3_ref, w2_ref, sem, *, n_a, layer):
    t = pl.program_id(0)
    is_a = t < n_a
    n_col_blocks = D_FF // FFN_COLS

    def weight_copies(c):
        cols = pl.ds(c * FFN_COLS, FFN_COLS)
        return (pltpu.make_async_copy(w1_hbm.at[:, cols], w1_ref.at[:, cols], sem.at[0, c]),
                pltpu.make_async_copy(w3_hbm.at[:, cols], w3_ref.at[:, cols], sem.at[1, c]),
                pltpu.make_async_copy(w2_hbm.at[cols, :], w2_ref.at[cols, :], sem.at[2, c]))

    def compute(first_step):
        if first_step:
            for c in range(n_col_blocks):
                for copy in weight_copies(c):
                    copy.start()
        x = jnp.where(is_a, xa_ref[...], xb_ref[...])
        h = (_standardize(x) * (1.0 + mod_ref[4:5, :]) + mod_ref[3:4, :]).astype(BF16)
        acc = jnp.zeros(x.shape, F32)
        for c in range(n_col_blocks):
            cols = slice(c * FFN_COLS, (c + 1) * FFN_COLS)
            if first_step:
                for copy in weight_copies(c):
                    copy.wait()
            a = _dot(h, w1_ref[:, cols])
            b = _dot(h, w3_ref[:, cols])
            acc = acc + _dot((_silu(a) * b).astype(BF16), w2_ref[cols, :])
        y = DEEPNORM_ALPHA * x + mod_ref[5:6, :] * acc
        return _standardize(y) * g_ref[layer:layer + 1, :] + b_ref[layer:layer + 1, :]

    @pl.when(t == 0)
    def _():
        oa_ref[...] = compute(True)

    @pl.when(t > 0)
    def _():
        out = compute(False)

        @pl.when(is_a)
        def _():
            oa_ref[...] = out

        @pl.when(jnp.logical_not(is_a))
        def _():
            ob_ref[...] = out


def _ffn_dense(xa, xb, l, mods, w1, w3, w2, g, b, lat_seq):
    n_a, n_b = xa.shape[0] // FFN_TILE, xb.shape[0] // FFN_TILE
    assert n_a >= 1
    x_specs = _two_stream_specs(FFN_TILE, n_a, lambda t: t)
    in_hbm = pl.BlockSpec(memory_space=pltpu.HBM)
    return pl.pallas_call(
        functools.partial(_ffn_kernel, n_a=n_a, layer=l),
        grid=(n_a + n_b,),
        in_specs=x_specs + [
            pl.BlockSpec((None, None, SUBLANES, D_MODEL),
                         lambda t: (l, _group_of_block(t, FFN_TILE, n_a, lat_seq), 0, 0)),
            in_hbm, in_hbm, in_hbm, _full(g.shape), _full(b.shape)],
        out_specs=x_specs,
        scratch_shapes=[pltpu.VMEM(w1.shape, BF16), pltpu.VMEM(w3.shape, BF16), pltpu.VMEM(w2.shape, BF16),
                        pltpu.SemaphoreType.DMA((3, D_FF // FFN_COLS))],
        out_shape=[jax.ShapeDtypeStruct(xa.shape, F32), jax.ShapeDtypeStruct(xb.shape, F32)],
        compiler_params=pltpu.CompilerParams(dimension_semantics=("arbitrary",), vmem_limit_bytes=VMEM_LIMIT),
        name="ffn_dense",
    )(xa, xb, mods, w1, w3, w2, g, b)


def _router_kernel(xa_ref, xb_ref, mod_ref, rw_ref, before_ref, dest_ref, gate_ref, start_ref, padded_ref, *,
                   n_a, lat_seq):
    c = pl.program_id(0)
    w_hi, w_lo = _split(rw_ref[...])
    n_blocks = MOE_CHUNK // ROUTER_BLOCK

    def store_per_token(ref, slot, val):
        for j in range(MOE_CHUNK // LANES):
            ref[slot, j:j + 1, :] = val[:, j * LANES:(j + 1) * LANES]

    parts = []
    for blk in range(n_blocks):
        rows = pl.ds(blk * ROUTER_BLOCK, ROUTER_BLOCK)
        mod = mod_ref[_group_of_block(c * n_blocks + blk, ROUTER_BLOCK, n_a * n_blocks, lat_seq)]
        x = jnp.where(c < n_a, xa_ref[rows, :], xb_ref[rows, :])
        h = _standardize(x) * (1.0 + mod[4:5, :]) + mod[3:4, :]
        h_hi, h_lo = _split(h)
        parts.append(_dot_nt(w_hi, h_hi) + _dot_nt(w_hi, h_lo) + _dot_nt(w_lo, h_hi))
    logits = jnp.concatenate(parts, axis=1)
    eid = lax.broadcasted_iota(I32, logits.shape, 0).astype(F32)
    m1 = jnp.max(logits, axis=0, keepdims=True)
    i1 = jnp.min(jnp.where(logits == m1, eid, float(N_EXPERTS)), axis=0, keepdims=True)
    oh1 = eid == i1
    rest = jnp.where(oh1, -jnp.inf, logits)
    m2 = jnp.max(rest, axis=0, keepdims=True)
    i2 = jnp.min(jnp.where(rest == m2, eid, float(N_EXPERTS)), axis=0, keepdims=True)
    oh2 = eid == i2
    e = jnp.exp(m2 - m1)
    store_per_token(gate_ref, 0, 1.0 / (1.0 + e))
    store_per_token(gate_ref, 1, e / (1.0 + e))
    sel = jnp.where(oh1 | oh2, 1.0, 0.0)
    ranks = []
    seen = jnp.zeros((N_EXPERTS, 1), F32)
    for blk in range(n_blocks):
        s_blk = sel[:, blk * ROUTER_BLOCK:(blk + 1) * ROUTER_BLOCK]
        ranks.append(_dot(s_blk.astype(BF16), before_ref[...]) + seen)
        seen = seen + jnp.sum(s_blk, axis=1, keepdims=True)
    rank = jnp.concatenate(ranks, axis=1)
    eid_out = lax.broadcasted_iota(I32, (1, LANES), 1).astype(F32)
    start = jnp.zeros(sel.shape, F32)
    start_out = jnp.zeros((1, LANES), F32)
    padded_out = jnp.zeros((1, LANES), F32)
    for ex in range(N_EXPERTS):
        cnt = jnp.sum(sel[ex:ex + 1, :], axis=1, keepdims=True)
        padded = jnp.ceil(cnt * (1.0 / MOE_GRAN)) * MOE_GRAN
        start = start + jnp.where(eid > ex, padded, 0.0)
        start_out = start_out + jnp.where(eid_out > ex, padded, 0.0)
        padded_out = padded_out + jnp.where(eid_out == ex, padded, 0.0)
    row = (start + rank) * SUBLANES
    store_per_token(dest_ref, 0, jnp.sum(jnp.where(oh1, row, 0.0), axis=0, keepdims=True).astype(I32))
    store_per_token(dest_ref, 1, jnp.sum(jnp.where(oh2, row, 0.0), axis=0, keepdims=True).astype(I32))

    @pl.when(c == 0)
    def _():
        start_ref[...] = jnp.zeros(start_ref.shape, I32)
        padded_ref[...] = jnp.zeros(padded_ref.shape, I32)

    start_ref[pl.ds(c, 1), :] = start_out.astype(I32)
    padded_ref[pl.ds(c, 1), :] = padded_out.astype(I32)


def _router(xa, xb, l, mods, rw_t, lat_seq):
    n_a = xa.shape[0] // MOE_CHUNK
    n_chunks = n_a + xb.shape[0] // MOE_CHUNK
    tok = np.arange(ROUTER_BLOCK)
    before = jnp.asarray(tok[:, None] < tok[None, :], BF16)
    table_rows = -(-n_chunks // SUBLANES) * SUBLANES
    return pl.pallas_call(
        functools.partial(_router_kernel, n_a=n_a, lat_seq=lat_seq),
        grid=(n_chunks,),
        in_specs=_two_stream_specs(MOE_CHUNK, n_a, lambda c: c) + [
                  _layer_block(mods, l),
                  _full((N_EXPERTS, D_MODEL)), _full((ROUTER_BLOCK, ROUTER_BLOCK))],
        out_specs=[pl.BlockSpec((None, 2, MOE_CHUNK // LANES, LANES), lambda c: (c, 0, 0, 0)),
                   pl.BlockSpec((None, 2, MOE_CHUNK // LANES, LANES), lambda c: (c, 0, 0, 0)),
                   _full((table_rows, LANES)), _full((table_rows, LANES))],
        out_shape=[jax.ShapeDtypeStruct((n_chunks, 2, MOE_CHUNK // LANES, LANES), I32),
                   jax.ShapeDtypeStruct((n_chunks, 2, MOE_CHUNK // LANES, LANES), F32),
                   jax.ShapeDtypeStruct((table_rows, LANES), I32),
                   jax.ShapeDtypeStruct((table_rows, LANES), I32)],
        compiler_params=pltpu.CompilerParams(dimension_semantics=("arbitrary",), vmem_limit_bytes=VMEM_LIMIT),
        name="moe_router",
    )(xa, xb, mods, rw_t, before)


def _token_rows(first_row):
    return pl.ds(pl.multiple_of(first_row, SUBLANES), SUBLANES)


def _store_token_major(ref, tok0, val):
    for cc in range(D_MODEL // LANES):
        ref[pl.ds(tok0 * SUBLANES + cc, val.shape[0], stride=SUBLANES), :] = val[:, cc * LANES:(cc + 1) * LANES]


def _load_token_major(ref, tok0, n):
    return jnp.concatenate(
        [ref[pl.ds(tok0 * SUBLANES + cc, n, stride=SUBLANES), :] for cc in range(D_MODEL // LANES)], axis=1)


def _moe_kernel(start_ref, padded_ref, xa_ref, xb_ref, mod_ref, dest_ref, gate_ref, w13_ref, w2_ref,
                g_ref, b_ref, oa_ref, ob_ref, tok_scr, rows_scr, *, n_a, layer):
    c = pl.program_id(0)
    j = pl.program_id(1)
    is_a = c < n_a
    first_expert_step = MOE_TOK_STEPS
    first_combine_step = MOE_TOK_STEPS + N_EXPERTS

    @pl.when((c == 0) & (j == 0))
    def _init():
        rows_scr[...] = jnp.zeros(rows_scr.shape, F32)

    @pl.when(j < first_expert_step)
    def _dispatch():
        t0 = j * MOE_TOK_BLOCK
        for r in range(MOE_TOK_BLOCK // MOE_SUB_BLOCK):
            rows = pl.ds(r * MOE_SUB_BLOCK, MOE_SUB_BLOCK)
            x = jnp.where(is_a, xa_ref[rows, :], xb_ref[rows, :])
            h = _standardize(x) * (1.0 + mod_ref[4:5, :]) + mod_ref[3:4, :]
            _store_token_major(tok_scr, r * MOE_SUB_BLOCK, h)
            for t in range(r * MOE_SUB_BLOCK, (r + 1) * MOE_SUB_BLOCK):
                row = tok_scr[pl.ds(t * SUBLANES, SUBLANES), :]
                rows_scr[_token_rows(dest_ref[t0 + t]), :] = row
                rows_scr[_token_rows(dest_ref[MOE_CHUNK + t0 + t]), :] = row

    def experts(row0, m):
        xin = _load_token_major(rows_scr, row0, m).astype(BF16)
        ab = _dot(xin, w13_ref[...])
        a, b = ab[:, :D_FF_EXPERT], ab[:, D_FF_EXPERT:]
        y = _dot((_silu(a) * b).astype(BF16), w2_ref[...])
        _store_token_major(rows_scr, row0, y)

    @pl.when((j >= first_expert_step) & (j < first_combine_step))
    def _experts():
        region = c * LANES + (j - first_expert_step)
        start = start_ref[region]
        padded = padded_ref[region]
        n_full = lax.shift_right_logical(padded, MOE_BLOCK.bit_length() - 1)

        def body(i, carry):
            experts(start + 2 * i * MOE_BLOCK, MOE_BLOCK)
            experts(start + (2 * i + 1) * MOE_BLOCK, MOE_BLOCK)
            return carry
        lax.fori_loop(0, lax.shift_right_logical(n_full, 1), body, 0)
        pl.when((n_full & 1) == 1)(functools.partial(experts, start + (n_full - 1) * MOE_BLOCK, MOE_BLOCK))
        for m in range(MOE_GRAN, MOE_BLOCK, MOE_GRAN):
            pl.when(padded - n_full * MOE_BLOCK == m)(
                functools.partial(experts, start + n_full * MOE_BLOCK, m))

    @pl.when(j >= first_combine_step)
    def _combine():
        t0 = (j - first_combine_step) * MOE_TOK_BLOCK
        outs = []
        for r in range(MOE_TOK_BLOCK // MOE_SUB_BLOCK):
            for t in range(r * MOE_SUB_BLOCK, (r + 1) * MOE_SUB_BLOCK):
                y0 = rows_scr[_token_rows(dest_ref[t0 + t]), :]
                y1 = rows_scr[_token_rows(dest_ref[MOE_CHUNK + t0 + t]), :]
                tok_scr[pl.ds(t * SUBLANES, SUBLANES), :] = (gate_ref[t0 + t] * y0
                                                             + gate_ref[MOE_CHUNK + t0 + t] * y1)
            rows = pl.ds(r * MOE_SUB_BLOCK, MOE_SUB_BLOCK)
            ffn = _load_token_major(tok_scr, r * MOE_SUB_BLOCK, MOE_SUB_BLOCK)
            x = jnp.where(is_a, xa_ref[rows, :], xb_ref[rows, :])
            y = DEEPNORM_ALPHA * x + mod_ref[5:6, :] * ffn
            outs.append(_standardize(y) * g_ref[layer:layer + 1, :] + b_ref[layer:layer + 1, :])

        @pl.when(is_a)
        def _():
            for r, out in enumerate(outs):
                oa_ref[pl.ds(r * MOE_SUB_BLOCK, MOE_SUB_BLOCK), :] = out

        @pl.when(jnp.logical_not(is_a))
        def _():
            for r, out in enumerate(outs):
                ob_ref[pl.ds(r * MOE_SUB_BLOCK, MOE_SUB_BLOCK), :] = out


def _ffn_moe(xa, xb, l, mods, rw_t, w13, w2, g, b, lat_seq):
    n_a = xa.shape[0] // MOE_CHUNK
    n_chunks = n_a + xb.shape[0] // MOE_CHUNK
    n_a_blocks = n_a * MOE_TOK_STEPS
    dest, gates, start, padded = _router(xa, xb, l, mods, rw_t, lat_seq)
    first_expert_step = MOE_TOK_STEPS
    first_combine_step = MOE_TOK_STEPS + N_EXPERTS

    def token_block(c, j, *_):
        blk = jnp.where(j < first_combine_step, jnp.minimum(j, MOE_TOK_STEPS - 1), j - first_combine_step)
        return c * MOE_TOK_STEPS + blk

    def out_token_block(c, j, *_):
        return c * MOE_TOK_STEPS + jnp.maximum(j - first_combine_step, 0)

    def mod_of(c, j, *_):
        return (l, _group_of_block(token_block(c, j), MOE_TOK_BLOCK, n_a_blocks, lat_seq), 0, 0)

    def expert_of(c, j, *_):
        return (jnp.clip(j - first_expert_step, 0, N_EXPERTS - 1), 0, 0)

    grid_spec = pltpu.PrefetchScalarGridSpec(
        num_scalar_prefetch=2,
        grid=(n_chunks, MOE_STEPS),
        in_specs=_two_stream_specs(MOE_TOK_BLOCK, n_a_blocks, token_block) + [
                  pl.BlockSpec((None, None, SUBLANES, D_MODEL), mod_of),
                  pl.BlockSpec((2 * MOE_CHUNK,), lambda c, j, *_: (c,), memory_space=pltpu.SMEM),
                  pl.BlockSpec((2 * MOE_CHUNK,), lambda c, j, *_: (c,), memory_space=pltpu.SMEM),
                  pl.BlockSpec((None, D_MODEL, 2 * D_FF_EXPERT), expert_of),
                  pl.BlockSpec((None, D_FF_EXPERT, D_MODEL), expert_of),
                  pl.BlockSpec(g.shape, lambda c, j, *_: (0, 0)), pl.BlockSpec(b.shape, lambda c, j, *_: (0, 0))],
        out_specs=_two_stream_specs(MOE_TOK_BLOCK, n_a_blocks, out_token_block),
        scratch_shapes=[pltpu.VMEM((MOE_TOK_BLOCK * SUBLANES, LANES), F32),
                        pltpu.VMEM((MOE_ROWS * SUBLANES, LANES), F32)],
    )
    return pl.pallas_call(
        functools.partial(_moe_kernel, n_a=n_a, layer=l),
        grid_spec=grid_spec,
        out_shape=[jax.ShapeDtypeStruct(xa.shape, F32), jax.ShapeDtypeStruct(xb.shape, F32)],
        compiler_params=pltpu.CompilerParams(
            dimension_semantics=("arbitrary", "arbitrary"), vmem_limit_bytes=VMEM_LIMIT),
        name="moe_experts",
    )(start.reshape(-1), padded.reshape(-1), xa, xb, mods,
      dest.reshape(-1), gates.reshape(-1), w13, w2, g, b)


def _rope_tables(n_tokens):
    t = np.arange(n_tokens)
    row = (t // GRID_W).astype(np.float32)
    col = (t % GRID_W).astype(np.float32)
    inv_freq = (np.float32(ROPE_THETA) ** (-np.arange(0, AXIS_ROT, 2, dtype=np.float32) / AXIS_ROT)).astype(np.float32)
    ang_r = row[:, None] * inv_freq
    ang_c = col[:, None] * inv_freq
    cos = np.concatenate([np.cos(ang_r), np.cos(ang_r), np.cos(ang_c), np.cos(ang_c)], axis=1)
    sin = np.concatenate([-np.sin(ang_r), np.sin(ang_r), -np.sin(ang_c), np.sin(ang_c)], axis=1)
    return jnp.asarray(np.tile(cos, (1, 2)), F32), jnp.asarray(np.tile(sin, (1, 2)), F32)


def kernel(x_prompt, x_sample, cache_k, cache_v, c, c_ctx, ada_w, ada_b, w_in, q_norm_g, k_norm_g, conv_w, conv_b, sgu_norm_g, sgu_w, sgu_b, w_out, ln1_g, ln1_b, ln2_g, ln2_b, ffn_w1, ffn_w3, ffn_w2, router_w, moe_w1, moe_w3, moe_w2):
    batch, seq, _ = x_prompt.shape
    dec_batch, dec_seq, _ = x_sample.shape
    past_len = cache_k.shape[2]
    n_ctx = batch * seq
    n_lat = dec_batch * dec_seq
    assert DEPTH == 2 and 1 + dec_batch <= SUBLANES
    assert seq == ATT_BLOCK and CTX_SEQS_PER_STEP * seq == ROW_BLOCK and batch % CTX_SEQS_PER_STEP == 0
    assert dec_seq % ROW_BLOCK == 0 and dec_seq & (dec_seq - 1) == 0
    assert n_ctx % MOE_CHUNK == 0 and n_lat % MOE_CHUNK == 0
    assert dec_seq % ROUTER_BLOCK == 0 and dec_seq % MOE_TOK_BLOCK == 0 and dec_seq % FFN_TILE == 0

    mod = _modulation(c_ctx[None, :], c, ada_w, ada_b)
    mod = mod.reshape(DEPTH, SUBLANES, 6, D_MODEL)[:, :1 + dec_batch]
    mod = jnp.pad(mod, ((0, 0), (0, 0), (0, SUBLANES - 6), (0, 0)))

    lane_id = np.arange(GROUP_TILE) // HEAD_DIM
    ones_bd = jnp.asarray(lane_id[:, None] == lane_id[None, :], BF16)
    cos, sin = _rope_tables(dec_seq)
    small = (
        jnp.concatenate([jnp.tile(q_norm_g, (1, N_Q_HEADS)), jnp.tile(k_norm_g, (1, N_KV_HEADS))], axis=1),
        conv_b, sgu_norm_g, ln1_g, ln1_b,
        ones_bd, conv_w,
        jnp.swapaxes(sgu_w, 1, 2).reshape(DEPTH, CHUNK, SGU_HEADS * CHUNK).astype(BF16),
        jnp.repeat(jnp.swapaxes(sgu_b, 1, 2), HEAD_DIM, axis=2),
    )
    kc = cache_k.reshape(dec_batch, DEPTH, past_len, KV_WIDTH)
    vc = cache_v.reshape(dec_batch, DEPTH, past_len, KV_WIDTH)

    xs = [x_prompt.reshape(n_ctx, D_MODEL), x_sample.reshape(n_lat, D_MODEL)]
    assert DEPTH == 2 and ffn_w1.shape[0] == 1 and moe_w1.shape[0] == 1
    side_casts = [[(ffn_w1[0],), (ffn_w3[0],), (ffn_w2[0],), ((w_in, 1),), ((w_out, 1),), (moe_w2.reshape(-1, D_MODEL),)],
                  [(moe_w1.reshape(-1, D_FF_EXPERT), moe_w3.reshape(-1, D_FF_EXPERT))]]
    kv = None
    for l in range(DEPTH):
        if l == 0:
            x_ctx, k_ctx, v_ctx, w_in_bf, w_out_bf, *ffn_bf, w_in_next, w_out_next, w2_bf = _mixer_ctx(
                xs[0], l, batch, seq, mod, w_in, w_out, small, kv, side_casts[l])
        else:
            w_in_bf, w_out_bf = w_in_next, w_out_next
            x_ctx, k_ctx, v_ctx, w13_bf = _mixer_ctx(
                xs[0], l, batch, seq, mod, w_in_bf, w_out_bf, small, kv, side_casts[l])
        kv = (k_ctx, v_ctx)
        x_lat = _mixer_lat(xs[1], l, dec_batch, dec_seq, mod, w_in_bf, w_out_bf, small, cos, sin, kc, vc)
        if l % 2 == 0:
            xs = _ffn_dense(x_ctx, x_lat, l, mod, *ffn_bf, ln2_g, ln2_b, dec_seq)
        else:
            ws = (w13_bf.reshape(N_EXPERTS, D_MODEL, 2 * D_FF_EXPERT), w2_bf.reshape(N_EXPERTS, D_FF_EXPERT, D_MODEL))
            xs = _ffn_moe(x_ctx, x_lat, l, mod, router_w[l // 2].T, *ws, ln2_g, ln2_b, dec_seq)
    y_p = xs[0].reshape(batch, seq, D_MODEL)
    y_s = xs[1].reshape(dec_batch, dec_seq, D_MODEL)
    new_k = kv[0].reshape(batch, DEPTH, seq, N_KV_HEADS, HEAD_DIM)
    new_v = kv[1].reshape(batch, DEPTH, seq, N_KV_HEADS, HEAD_DIM)
    return (y_p, y_s, new_k, new_v)
```

```python
import functools

import numpy as np
import jax
import jax.numpy as jnp
from jax import lax
from jax.experimental import pallas as pl
from jax.experimental.pallas import tpu as pltpu

F32 = jnp.float32
BF16 = jnp.bfloat16
I32 = jnp.int32

D_MODEL = 1024
DEPTH = 2
GRID_W = 64
HEAD_DIM = 64
N_Q_HEADS = 8
N_KV_HEADS = 2
ATTN_WIDTH = N_Q_HEADS * HEAD_DIM
KV_WIDTH = N_KV_HEADS * HEAD_DIM
ATTN_SCALE = HEAD_DIM ** -0.5
ROPE_THETA = 10000.0
AXIS_ROT = HEAD_DIM // 2
CONV_WIDTH = 256
SGU_WIDTH = 256
SGU_HEADS = 4
CHUNK = 128
IN_WIDTH = 2048
D_FF = 2816
N_EXPERTS = 8
D_FF_EXPERT = 1408
EPS = 1e-6
DEEPNORM_ALPHA = (2 * DEPTH) ** 0.25

LANES = 128
SUBLANES = 8
ROW_BLOCK = 512
GROUP_TILE = 256
ATT_BLOCK = 256
LAT_ATT_BLOCK = 128
CTX_SEQS_PER_STEP = 2
FFN_TILE = 512
FFN_COLS = 256
MOE_CHUNK = 2048
ROUTER_BLOCK = 1024
MOE_GRAN = 128
MOE_BLOCK = 256
MOE_TOK_BLOCK = 512
MOE_SUB_BLOCK = 256
MOE_TOK_STEPS = MOE_CHUNK // MOE_TOK_BLOCK
MOE_STEPS = 2 * MOE_TOK_STEPS + N_EXPERTS
MOE_ROWS = 2 * MOE_CHUNK + N_EXPERTS * MOE_GRAN
VMEM_LIMIT = 60 * 1024 * 1024

_Q0, _K0, _V0, _CI0, _CB0, _CC0, _SU0, _SV0 = 0, 512, 640, 768, 1024, 1280, 1536, 1792


def _dot(a, b):
    return jnp.dot(a, b, preferred_element_type=F32)


def _dot_nt(a, b):
    return lax.dot_general(a, b, (((1,), (1,)), ((), ())), preferred_element_type=F32)


def _split(x):
    hi = x.astype(BF16)
    lo = (x - hi.astype(F32)).astype(BF16)
    return hi, lo


def _group_sum(x, ones_bd):
    outs = []
    for c0 in range(0, x.shape[1], GROUP_TILE):
        width = min(GROUP_TILE, x.shape[1] - c0)
        outs.append(_dot(x[:, c0:c0 + width].astype(BF16), ones_bd[:width, :width]))
    return outs[0] if len(outs) == 1 else jnp.concatenate(outs, axis=1)


def _standardize(x):
    mu = jnp.mean(x, axis=-1, keepdims=True)
    d = x - mu
    return d * lax.rsqrt(jnp.mean(d * d, axis=-1, keepdims=True) + EPS)


def _silu(x):
    return x / (1.0 + jnp.exp(-x))


def _modulation_kernel(cctx_ref, c_ref, w_ref, b_ref, o_ref):
    n_pad = SUBLANES - 1 - c_ref.shape[0]
    cond = jnp.concatenate([cctx_ref[...], c_ref[...], jnp.zeros((n_pad, D_MODEL), F32)], axis=0)
    s_hi, s_lo = _split(_silu(cond))
    w_hi, w_lo = _split(w_ref[...])
    bias = b_ref[pl.ds(pl.program_id(0), 1), :]
    o_ref[...] = _dot(s_hi, w_hi) + _dot(s_hi, w_lo) + _dot(s_lo, w_hi) + bias


def _modulation(c_ctx, c, ada_w, ada_b):
    n_out = ada_w.shape[-1]
    tn = 1536
    return pl.pallas_call(
        _modulation_kernel,
        grid=(DEPTH, n_out // tn),
        in_specs=[
            pl.BlockSpec(c_ctx.shape, lambda l, j: (0, 0)),
            pl.BlockSpec(c.shape, lambda l, j: (0, 0)),
            pl.BlockSpec((None, D_MODEL, tn), lambda l, j: (l, 0, j)),
            pl.BlockSpec((DEPTH, tn), lambda l, j: (0, j)),
        ],
        out_specs=pl.BlockSpec((None, SUBLANES, tn), lambda l, j: (l, 0, j)),
        out_shape=jax.ShapeDtypeStruct((DEPTH, SUBLANES, n_out), F32),
        compiler_params=pltpu.CompilerParams(
            dimension_semantics=("arbitrary", "arbitrary"), vmem_limit_bytes=VMEM_LIMIT),
        name="modulation",
    )(c_ctx, c, ada_w, ada_b)


def _rope(x, cos, sin_signed):
    w = x.shape[1]
    lane = lax.broadcasted_iota(I32, x.shape, 1)
    first_half = (lane & 31) < 16
    partner = jnp.where(first_half, pltpu.roll(x, w - 16, 1), pltpu.roll(x, 16, 1))
    return x * cos + partner * sin_signed


def _head_variants(x):
    lane = lax.broadcasted_iota(I32, x.shape, 1)
    lo = lane < HEAD_DIM
    xr = pltpu.roll(x, HEAD_DIM, 1)
    zero = jnp.zeros_like(x)
    return (jnp.where(lo, x, zero).astype(BF16), jnp.where(lo, zero, xr).astype(BF16),
            jnp.where(lo, xr, zero).astype(BF16), jnp.where(lo, zero, x).astype(BF16))


def _mixer_kernel(*refs, layer, seq, n_seq, n_cache, rope, cast_weights, stack_kv, side_jobs=(), n_steps=None):
    refs = list(refs)
    skew = n_steps is not None

    def take(n):
        out, refs[:] = refs[:n], refs[n:]
        return out

    (x_ref,) = take(1)
    xres_ref = take(1)[0] if skew else x_ref
    mod_ref, win_ref, wout_ref = take(3)
    qkg_ref, convb_ref, sgug_ref, ln1g_ref, ln1b_ref = take(5)
    ones_ref, convw_ref, sguw_ref, sgub_ref = take(4)
    this = slice(layer, layer + 1)
    if rope:
        cos_ref, sin_ref, kc_ref, vc_ref = take(4)
    if stack_kv:
        kprev_ref, vprev_ref = take(2)
    side_in = [take(n_src) for n_src in side_jobs]
    (x1_ref,) = take(1)
    if not rope:
        k_ref, v_ref = take(2)
    if cast_weights:
        winb_ref, woutb_ref = take(2)

        @pl.when(pl.program_id(0) == 0)
        def _cast():
            winb_ref[...] = win_ref[...].astype(BF16)
            woutb_ref[...] = wout_ref[...].astype(BF16)
        win_ref, wout_ref = winb_ref, woutb_ref
    side_out = take(len(side_jobs))
    scratch = take(7)
    (mix_scr,) = take(1)
    n_rows = n_seq * seq
    assert n_cache == 0 or n_seq == 1

    def loop(n, body):
        if n == 1:
            body(0)
        else:
            def step(r, carry):
                body(r)
                return carry
            lax.fori_loop(0, n, step, 0, unroll=2)

    def block(r, size):
        if isinstance(r, int):
            return pl.ds(r * size, size)
        return pl.ds(pl.multiple_of(r * size, size), size)

    def project_phase(scr):
        q_scr, kvar_scr, vvar_scr, u_scr, cb_scr, su_scr, vn_scr = scr
        for srcs, dst in zip(side_in, side_out):
            col = 0
            for src in srcs:
                dst[:, col:col + src.shape[1]] = src[...].astype(BF16)
                col += src.shape[1]
        if n_cache:
            for i, var in enumerate(_head_variants(kc_ref[...])):
                kvar_scr[i, pl.ds(seq, n_cache), :] = var
            for i, var in enumerate(_head_variants(vc_ref[...])):
                vvar_scr[i, pl.ds(seq, n_cache), :] = var

        def project(r):
            rows = block(r, ROW_BLOCK)
            x = x_ref[rows, :]
            h = _standardize(x) * (1.0 + mod_ref[1:2, :]) + mod_ref[0:1, :]
            z = _dot(h.astype(BF16), win_ref[...])
            ones_bd = ones_ref[...]
            zq = z[:, _Q0:_K0]
            q = zq * lax.rsqrt(_group_sum(zq * zq, ones_bd) * (1.0 / HEAD_DIM) + EPS) * qkg_ref[this, :ATTN_WIDTH]
            zk = z[:, _K0:_V0]
            k = zk * lax.rsqrt(_group_sum(zk * zk, ones_bd) * (1.0 / HEAD_DIM) + EPS) * qkg_ref[this, ATTN_WIDTH:]
            v = z[:, _V0:_CI0]
            if rope:
                cos = cos_ref[rows, :]
                sin = sin_ref[rows, :]
                q = _rope(q, jnp.concatenate([cos] * 4, axis=1), jnp.concatenate([sin] * 4, axis=1))
                k = _rope(k, cos, sin)
            elif stack_kv:
                for s in range(ROW_BLOCK // seq):
                    sub = slice(s * seq, (s + 1) * seq)
                    k_ref[s, 0] = kprev_ref[sub, :]
                    v_ref[s, 0] = vprev_ref[sub, :]
                    k_ref[s, 1] = k[sub, :]
                    v_ref[s, 1] = v[sub, :]
            else:
                k_ref[rows, :] = k
                v_ref[rows, :] = v
            q_scr[rows, :] = (q * ATTN_SCALE).astype(BF16)
            for i, var in enumerate(_head_variants(k)):
                kvar_scr[i, rows, :] = var
            for i, var in enumerate(_head_variants(v)):
                vvar_scr[i, rows, :] = var
            u_scr[rows, :] = z[:, _CC0:_SU0] * z[:, _CI0:_CB0]
            cb_scr[rows, :] = z[:, _CB0:_CC0]
            su_scr[rows, :] = z[:, _SU0:_SV0]
            sv = z[:, _SV0:IN_WIDTH]
            d = sv - _group_sum(sv, ones_bd) * (1.0 / HEAD_DIM)
            vn = d * lax.rsqrt(_group_sum(d * d, ones_bd) * (1.0 / HEAD_DIM) + EPS) * sgug_ref[this, :]
            vn_scr[rows, :] = vn.astype(BF16)

        loop(n_rows // ROW_BLOCK, project)

    def consume_phase(scr):
        q_scr, kvar_scr, vvar_scr, u_scr, cb_scr, su_scr, vn_scr = scr
        u = u_scr[...]
        pos = lax.broadcasted_iota(I32, u.shape, 0) & (seq - 1)
        up = jnp.where(pos == 0, 0.0, pltpu.roll(u, 1, 0))
        dn = jnp.where(pos == seq - 1, 0.0, pltpu.roll(u, n_rows - 1, 0))
        conv = up * convw_ref[0:1, :] + u * convw_ref[1:2, :] + dn * convw_ref[2:3, :] + convb_ref[this, :]
        mix_scr[:, ATTN_WIDTH:ATTN_WIDTH + CONV_WIDTH] = (cb_scr[...] * conv).astype(BF16)

        for n in range(n_rows // CHUNK):
            rows = pl.ds(n * CHUNK, CHUNK)
            vn = vn_scr[rows, :]
            lane = lax.broadcasted_iota(I32, vn.shape, 1)
            per_head = [jnp.where((lane >= hd * HEAD_DIM) & (lane < (hd + 1) * HEAD_DIM), vn, jnp.zeros_like(vn))
                        for hd in range(SGU_HEADS)]
            s = sgub_ref[...] + _dot(sguw_ref[...], jnp.concatenate(per_head, axis=0))
            mix_scr[rows, ATTN_WIDTH + CONV_WIDTH:] = (su_scr[rows, :] * s).astype(BF16)

        att_rows = LAT_ATT_BLOCK if n_cache else ATT_BLOCK

        def attend(s, r):
            rows = block(s * (seq // att_rows) + r, att_rows)
            keys = pl.ds(s * seq, seq + n_cache)
            for pair in range(N_Q_HEADS // 2):
                qp = q_scr[rows, pair * LANES:(pair + 1) * LANES]
                kv = pair // (N_Q_HEADS // N_KV_HEADS // 2)
                acc = jnp.zeros((att_rows, LANES), F32)
                for parity in range(2):
                    sc = _dot_nt(qp, kvar_scr[2 * kv + parity, keys, :])
                    p = jnp.exp(sc - jnp.max(sc, axis=1, keepdims=True))
                    denom = jnp.sum(p, axis=1, keepdims=True)
                    acc = acc + _dot(p.astype(BF16), vvar_scr[2 * kv + parity, keys, :]) / denom
                mix_scr[rows, pair * LANES:(pair + 1) * LANES] = acc.astype(BF16)

        for s in range(n_seq):
            loop(seq // att_rows, functools.partial(attend, s))

        def finish(r):
            rows = block(r, ROW_BLOCK)
            mix = _dot(mix_scr[rows, :], wout_ref[...])
            y = DEEPNORM_ALPHA * xres_ref[rows, :] + mod_ref[2:3, :] * mix
            x1_ref[rows, :] = _standardize(y) * ln1g_ref[this, :] + ln1b_ref[this, :]

        loop(n_rows // ROW_BLOCK, finish)

    if not skew:
        project_phase(scratch)
        consume_phase(scratch)
        return

    i = pl.program_id(0)
    slot = lax.rem(i, 2)
    mine = [ref.at[slot] for ref in scratch]
    other = [ref.at[1 - slot] for ref in scratch]
    pl.when(i == 0)(functools.partial(project_phase, mine))

    @pl.when((i > 0) & (i < n_steps))
    def _():
        consume_phase(other)
        project_phase(mine)

    pl.when(i == n_steps)(functools.partial(consume_phase, other))


def _full(shape):
    n = len(shape)
    return pl.BlockSpec(shape, lambda *_: (0,) * n)


def _resident(shape):
    n = len(shape)
    return pl.BlockSpec(shape, lambda *_: (0,) * n, pipeline_mode=pl.Buffered(1))


def _layer_block(arr, l, resident=False):
    shape = arr.shape[1:]
    kw = dict(pipeline_mode=pl.Buffered(1)) if resident else {}
    return pl.BlockSpec((None,) + shape, lambda *_: (l,) + (0,) * len(shape), **kw)


def _mixer_scratch(n_rows, n_cache, slots=()):
    nk = n_rows + n_cache
    return [
        pltpu.VMEM(slots + (n_rows, ATTN_WIDTH), BF16),
        pltpu.VMEM(slots + (4, nk, LANES), BF16),
        pltpu.VMEM(slots + (4, nk, LANES), BF16),
        pltpu.VMEM(slots + (n_rows, CONV_WIDTH), F32),
        pltpu.VMEM(slots + (n_rows, CONV_WIDTH), F32),
        pltpu.VMEM(slots + (n_rows, SGU_WIDTH), F32),
        pltpu.VMEM(slots + (n_rows, SGU_WIDTH), BF16),
        pltpu.VMEM((n_rows, D_MODEL), BF16),
    ]


def _side_source(src):
    return src if isinstance(src, tuple) else (src, None)


def _mixer_ctx(x, l, n_seq, seq, mod, w_in, w_out, small, kv_prev=None, side_casts=()):
    per_step = CTX_SEQS_PER_STEP
    rows = per_step * seq
    stack_kv = kv_prev is not None
    cast_weights = w_in.dtype != BF16
    n_steps = n_seq // per_step
    kernel = functools.partial(_mixer_kernel, layer=l, seq=seq, n_seq=per_step, n_cache=0, rope=False,
                               cast_weights=cast_weights, stack_kv=stack_kv,
                               side_jobs=tuple(len(job) for job in side_casts), n_steps=n_steps)
    cur = lambda i: (jnp.minimum(i, n_steps - 1), 0)
    prev = lambda i: (jnp.maximum(i - 1, 0), 0)
    w_specs = ([_layer_block(w_in, l, resident=True), _layer_block(w_out, l, resident=True)] if cast_weights
               else [_resident(w_in.shape), _resident(w_out.shape)])
    in_specs = ([pl.BlockSpec((rows, D_MODEL), cur), pl.BlockSpec((rows, D_MODEL), prev),
                 pl.BlockSpec((None, None, SUBLANES, D_MODEL), lambda i: (l, 0, 0, 0))] + w_specs
                + [_full(a.shape) if a.ndim == 2 else _layer_block(a, l) for a in small])
    args = [x, x, mod, w_in, w_out, *small]
    if stack_kv:
        in_specs += [pl.BlockSpec((rows, KV_WIDTH), cur)] * 2
        args += list(kv_prev)
        kv_spec = pl.BlockSpec((per_step, DEPTH, seq, KV_WIDTH), lambda i: cur(i) + (0, 0))
        kv_shape = jax.ShapeDtypeStruct((n_seq, DEPTH, seq, KV_WIDTH), F32)
    else:
        kv_spec = pl.BlockSpec((rows, KV_WIDTH), cur)
        kv_shape = jax.ShapeDtypeStruct((n_seq * seq, KV_WIDTH), F32)
    side_out_specs, side_out_shapes = [], []
    for job in side_casts:
        sources = [_side_source(src) for src in job]
        n_rows = sources[0][0].shape[-2]
        n_cols = sum(arr.shape[-1] for arr, _ in sources)
        for arr, layer in sources:
            blk = (n_rows // n_steps, arr.shape[-1])
            if layer is None:
                in_specs.append(pl.BlockSpec(blk, cur))
            else:
                in_specs.append(pl.BlockSpec((None,) + blk, lambda i, layer=layer: (layer,) + cur(i)))
            args.append(arr)
        side_out_specs.append(pl.BlockSpec((n_rows // n_steps, n_cols), cur))
        side_out_shapes.append(jax.ShapeDtypeStruct((n_rows, n_cols), BF16))
    w_out_specs, w_out_shapes = [], []
    if cast_weights:
        w_out_specs = [_full((D_MODEL, IN_WIDTH)), _full((D_MODEL, D_MODEL))]
        w_out_shapes = [jax.ShapeDtypeStruct((D_MODEL, IN_WIDTH), BF16), jax.ShapeDtypeStruct((D_MODEL, D_MODEL), BF16)]
    return pl.pallas_call(
        kernel,
        grid=(n_steps + 1,),
        in_specs=in_specs,
        out_specs=[pl.BlockSpec((rows, D_MODEL), prev), kv_spec, kv_spec] + w_out_specs + side_out_specs,
        out_shape=[jax.ShapeDtypeStruct((n_seq * seq, D_MODEL), F32), kv_shape, kv_shape] + w_out_shapes
        + side_out_shapes,
        scratch_shapes=_mixer_scratch(rows, 0, slots=(2,)),
        compiler_params=pltpu.CompilerParams(dimension_semantics=("arbitrary",), vmem_limit_bytes=VMEM_LIMIT),
        name="mixer_ctx",
    )(*args)


def _mixer_lat(x, l, n_seq, seq, mod, w_in_bf, w_out_bf, small, cos, sin, kc, vc):
    n_cache = kc.shape[2]
    kernel = functools.partial(_mixer_kernel, layer=l, seq=seq, n_seq=1, n_cache=n_cache, rope=True,
                               cast_weights=False, stack_kv=False)
    cache_spec = pl.BlockSpec((None, None, n_cache, KV_WIDTH), lambda b: (b, l, 0, 0))
    return pl.pallas_call(
        kernel,
        grid=(n_seq,),
        in_specs=([pl.BlockSpec((seq, D_MODEL), lambda b: (b, 0)),
                   pl.BlockSpec((None, None, SUBLANES, D_MODEL), lambda b: (l, 1 + b, 0, 0)),
                   _resident((D_MODEL, IN_WIDTH)), _resident((D_MODEL, D_MODEL))]
                  + [_full(a.shape) if a.ndim == 2 else _layer_block(a, l) for a in small]
                  + [_full((seq, LANES)), _full((seq, LANES)), cache_spec, cache_spec]),
        out_specs=pl.BlockSpec((seq, D_MODEL), lambda b: (b, 0)),
        out_shape=jax.ShapeDtypeStruct((n_seq * seq, D_MODEL), F32),
        scratch_shapes=_mixer_scratch(seq, n_cache),
        compiler_params=pltpu.CompilerParams(dimension_semantics=("arbitrary",), vmem_limit_bytes=VMEM_LIMIT),
        name="mixer_lat",
    )(x, mod, w_in_bf, w_out_bf, *small, cos, sin, kc, vc)


def _two_stream_specs(block_rows, n_a, block_of):
    spec_a = pl.BlockSpec((block_rows, D_MODEL), lambda *ids: (jnp.minimum(block_of(*ids), n_a - 1), 0))
    spec_b = pl.BlockSpec((block_rows, D_MODEL), lambda *ids: (jnp.maximum(block_of(*ids) - n_a, 0), 0))
    return [spec_a, spec_b]


def _group_of_block(blk, block_rows, n_a, lat_seq):
    return jnp.where(blk < n_a, 0, 1 + (blk - n_a) // (lat_seq // block_rows))


def _ffn_kernel(xa_ref, xb_ref, mod_ref, w1_hbm, w3_hbm, w2_hbm, g_ref, b_ref, oa_ref, ob_ref,
                w1_ref, w3_ref, w2_ref, sem, *, n_a, layer):
    t = pl.program_id(0)
    is_a = t < n_a
    n_col_blocks = D_FF // FFN_COLS

    def weight_copies(c):
        cols = pl.ds(c * FFN_COLS, FFN_COLS)
        return (pltpu.make_async_copy(w1_hbm.at[:, cols], w1_ref.at[:, cols], sem.at[0, c]),
                pltpu.make_async_copy(w3_hbm.at[:, cols], w3_ref.at[:, cols], sem.at[1, c]),
                pltpu.make_async_copy(w2_hbm.at[cols, :], w2_ref.at[cols, :], sem.at[2, c]))

    def compute(first_step):
        if first_step:
            for c in range(n_col_blocks):
                for copy in weight_copies(c):
                    copy.start()
        x = jnp.where(is_a, xa_ref[...], xb_ref[...])
        h = (_standardize(x) * (1.0 + mod_ref[4:5, :]) + mod_ref[3:4, :]).astype(BF16)
        acc = jnp.zeros(x.shape, F32)
        for c in range(n_col_blocks):
            cols = slice(c * FFN_COLS, (c + 1) * FFN_COLS)
            if first_step:
                for copy in weight_copies(c):
                    copy.wait()
            a = _dot(h, w1_ref[:, cols])
            b = _dot(h, w3_ref[:, cols])
            acc = acc + _dot((_silu(a) * b).astype(BF16), w2_ref[cols, :])
        y = DEEPNORM_ALPHA * x + mod_ref[5:6, :] * acc
        return _standardize(y) * g_ref[layer:layer + 1, :] + b_ref[layer:layer + 1, :]

    @pl.when(t == 0)
    def _():
        oa_ref[...] = compute(True)

    @pl.when(t > 0)
    def _():
        out = compute(False)

        @pl.when(is_a)
        def _():
            oa_ref[...] = out

        @pl.when(jnp.logical_not(is_a))
        def _():
            ob_ref[...] = out


def _ffn_dense(xa, xb, l, mods, w1, w3, w2, g, b, lat_seq):
    n_a, n_b = xa.shape[0] // FFN_TILE, xb.shape[0] // FFN_TILE
    assert n_a >= 1
    x_specs = _two_stream_specs(FFN_TILE, n_a, lambda t: t)
    in_hbm = pl.BlockSpec(memory_space=pltpu.HBM)
    return pl.pallas_call(
        functools.partial(_ffn_kernel, n_a=n_a, layer=l),
        grid=(n_a + n_b,),
        in_specs=x_specs + [
            pl.BlockSpec((None, None, SUBLANES, D_MODEL),
                         lambda t: (l, _group_of_block(t, FFN_TILE, n_a, lat_seq), 0, 0)),
            in_hbm, in_hbm, in_hbm, _full(g.shape), _full(b.shape)],
        out_specs=x_specs,
        scratch_shapes=[pltpu.VMEM(w1.shape, BF16), pltpu.VMEM(w3.shape, BF16), pltpu.VMEM(w2.shape, BF16),
                        pltpu.SemaphoreType.DMA((3, D_FF // FFN_COLS))],
        out_shape=[jax.ShapeDtypeStruct(xa.shape, F32), jax.ShapeDtypeStruct(xb.shape, F32)],
        compiler_params=pltpu.CompilerParams(dimension_semantics=("arbitrary",), vmem_limit_bytes=VMEM_LIMIT),
        name="ffn_dense",
    )(xa, xb, mods, w1, w3, w2, g, b)


def _router_kernel(xa_ref, xb_ref, mod_ref, rw_ref, before_ref, dest_ref, gate_ref, start_ref, padded_ref, *,
                   n_a, lat_seq):
    c = pl.program_id(0)
    w_hi, w_lo = _split(rw_ref[...])
    n_blocks = MOE_CHUNK // ROUTER_BLOCK

    def store_per_token(ref, slot, val):
        for j in range(MOE_CHUNK // LANES):
            ref[slot, j:j + 1, :] = val[:, j * LANES:(j + 1) * LANES]

    parts = []
    for blk in range(n_blocks):
        rows = pl.ds(blk * ROUTER_BLOCK, ROUTER_BLOCK)
        mod = mod_ref[_group_of_block(c * n_blocks + blk, ROUTER_BLOCK, n_a * n_blocks, lat_seq)]
        x = jnp.where(c < n_a, xa_ref[rows, :], xb_ref[rows, :])
        h = _standardize(x) * (1.0 + mod[4:5, :]) + mod[3:4, :]
        h_hi, h_lo = _split(h)
        parts.append(_dot_nt(w_hi, h_hi) + _dot_nt(w_hi, h_lo) + _dot_nt(w_lo, h_hi))
    logits = jnp.concatenate(parts, axis=1)
    eid = lax.broadcasted_iota(I32, logits.shape, 0).astype(F32)
    m1 = jnp.max(logits, axis=0, keepdims=True)
    i1 = jnp.min(jnp.where(logits == m1, eid, float(N_EXPERTS)), axis=0, keepdims=True)
    oh1 = eid == i1
    rest = jnp.where(oh1, -jnp.inf, logits)
    m2 = jnp.max(rest, axis=0, keepdims=True)
    i2 = jnp.min(jnp.where(rest == m2, eid, float(N_EXPERTS)), axis=0, keepdims=True)
    oh2 = eid == i2
    e = jnp.exp(m2 - m1)
    store_per_token(gate_ref, 0, 1.0 / (1.0 + e))
    store_per_token(gate_ref, 1, e / (1.0 + e))
    sel = jnp.where(oh1 | oh2, 1.0, 0.0)
    ranks = []
    seen = jnp.zeros((N_EXPERTS, 1), F32)
    for blk in range(n_blocks):
        s_blk = sel[:, blk * ROUTER_BLOCK:(blk + 1) * ROUTER_BLOCK]
        ranks.append(_dot(s_blk.astype(BF16), before_ref[...]) + seen)
        seen = seen + jnp.sum(s_blk, axis=1, keepdims=True)
    rank = jnp.concatenate(ranks, axis=1)
    eid_out = lax.broadcasted_iota(I32, (1, LANES), 1).astype(F32)
    start = jnp.zeros(sel.shape, F32)
    start_out = jnp.zeros((1, LANES), F32)
    padded_out = jnp.zeros((1, LANES), F32)
    for ex in range(N_EXPERTS):
        cnt = jnp.sum(sel[ex:ex + 1, :], axis=1, keepdims=True)
        padded = jnp.ceil(cnt * (1.0 / MOE_GRAN)) * MOE_GRAN
        start = start + jnp.where(eid > ex, padded, 0.0)
        start_out = start_out + jnp.where(eid_out > ex, padded, 0.0)
        padded_out = padded_out + jnp.where(eid_out == ex, padded, 0.0)
    row = (start + rank) * SUBLANES
    store_per_token(dest_ref, 0, jnp.sum(jnp.where(oh1, row, 0.0), axis=0, keepdims=True).astype(I32))
    store_per_token(dest_ref, 1, jnp.sum(jnp.where(oh2, row, 0.0), axis=0, keepdims=True).astype(I32))

    @pl.when(c == 0)
    def _():
        start_ref[...] = jnp.zeros(start_ref.shape, I32)
        padded_ref[...] = jnp.zeros(padded_ref.shape, I32)

    start_ref[pl.ds(c, 1), :] = start_out.astype(I32)
    padded_ref[pl.ds(c, 1), :] = padded_out.astype(I32)


def _router(xa, xb, l, mods, rw_t, lat_seq):
    n_a = xa.shape[0] // MOE_CHUNK
    n_chunks = n_a + xb.shape[0] // MOE_CHUNK
    tok = np.arange(ROUTER_BLOCK)
    before = jnp.asarray(tok[:, None] < tok[None, :], BF16)
    table_rows = -(-n_chunks // SUBLANES) * SUBLANES
    return pl.pallas_call(
        functools.partial(_router_kernel, n_a=n_a, lat_seq=lat_seq),
        grid=(n_chunks,),
        in_specs=_two_stream_specs(MOE_CHUNK, n_a, lambda c: c) + [
                  _layer_block(mods, l),
                  _full((N_EXPERTS, D_MODEL)), _full((ROUTER_BLOCK, ROUTER_BLOCK))],
        out_specs=[pl.BlockSpec((None, 2, MOE_CHUNK // LANES, LANES), lambda c: (c, 0, 0, 0)),
                   pl.BlockSpec((None, 2, MOE_CHUNK // LANES, LANES), lambda c: (c, 0, 0, 0)),
                   _full((table_rows, LANES)), _full((table_rows, LANES))],
        out_shape=[jax.ShapeDtypeStruct((n_chunks, 2, MOE_CHUNK // LANES, LANES), I32),
                   jax.ShapeDtypeStruct((n_chunks, 2, MOE_CHUNK // LANES, LANES), F32),
                   jax.ShapeDtypeStruct((table_rows, LANES), I32),
                   jax.ShapeDtypeStruct((table_rows, LANES), I32)],
        compiler_params=pltpu.CompilerParams(dimension_semantics=("arbitrary",), vmem_limit_bytes=VMEM_LIMIT),
        name="moe_router",
    )(xa, xb, mods, rw_t, before)


def _token_rows(first_row):
    return pl.ds(pl.multiple_of(first_row, SUBLANES), SUBLANES)


def _store_token_major(ref, tok0, val):
    for cc in range(D_MODEL // LANES):
        ref[pl.ds(tok0 * SUBLANES + cc, val.shape[0], stride=SUBLANES), :] = val[:, cc * LANES:(cc + 1) * LANES]


def _load_token_major(ref, tok0, n):
    return jnp.concatenate(
        [ref[pl.ds(tok0 * SUBLANES + cc, n, stride=SUBLANES), :] for cc in range(D_MODEL // LANES)], axis=1)


def _moe_kernel(start_ref, padded_ref, xa_ref, xb_ref, mod_ref, dest_ref, gate_ref, w13_ref, w2_ref,
                g_ref, b_ref, oa_ref, ob_ref, tok_scr, rows_scr, *, n_a, layer):
    c = pl.program_id(0)
    j = pl.program_id(1)
    is_a = c < n_a
    first_expert_step = MOE_TOK_STEPS
    first_combine_step = MOE_TOK_STEPS + N_EXPERTS

    @pl.when((c == 0) & (j == 0))
    def _init():
        rows_scr[...] = jnp.zeros(rows_scr.shape, F32)

    @pl.when(j < first_expert_step)
    def _dispatch():
        t0 = j * MOE_TOK_BLOCK
        for r in range(MOE_TOK_BLOCK // MOE_SUB_BLOCK):
            rows = pl.ds(r * MOE_SUB_BLOCK, MOE_SUB_BLOCK)
            x = jnp.where(is_a, xa_ref[rows, :], xb_ref[rows, :])
            h = _standardize(x) * (1.0 + mod_ref[4:5, :]) + mod_ref[3:4, :]
            _store_token_major(tok_scr, r * MOE_SUB_BLOCK, h)
            for t in range(r * MOE_SUB_BLOCK, (r + 1) * MOE_SUB_BLOCK):
                row = tok_scr[pl.ds(t * SUBLANES, SUBLANES), :]
                rows_scr[_token_rows(dest_ref[t0 + t]), :] = row
                rows_scr[_token_rows(dest_ref[MOE_CHUNK + t0 + t]), :] = row

    def experts(row0, m):
        xin = _load_token_major(rows_scr, row0, m).astype(BF16)
        ab = _dot(xin, w13_ref[...])
        a, b = ab[:, :D_FF_EXPERT], ab[:, D_FF_EXPERT:]
        y = _dot((_silu(a) * b).astype(BF16), w2_ref[...])
        _store_token_major(rows_scr, row0, y)

    @pl.when((j >= first_expert_step) & (j < first_combine_step))
    def _experts():
        region = c * LANES + (j - first_expert_step)
        start = start_ref[region]
        padded = padded_ref[region]
        n_full = lax.shift_right_logical(padded, MOE_BLOCK.bit_length() - 1)

        def body(i, carry):
            experts(start + 2 * i * MOE_BLOCK, MOE_BLOCK)
            experts(start + (2 * i + 1) * MOE_BLOCK, MOE_BLOCK)
            return carry
        lax.fori_loop(0, lax.shift_right_logical(n_full, 1), body, 0)
        pl.when((n_full & 1) == 1)(functools.partial(experts, start + (n_full - 1) * MOE_BLOCK, MOE_BLOCK))
        for m in range(MOE_GRAN, MOE_BLOCK, MOE_GRAN):
            pl.when(padded - n_full * MOE_BLOCK == m)(
                functools.partial(experts, start + n_full * MOE_BLOCK, m))

    @pl.when(j >= first_combine_step)
    def _combine():
        t0 = (j - first_combine_step) * MOE_TOK_BLOCK
        outs = []
        for r in range(MOE_TOK_BLOCK // MOE_SUB_BLOCK):
            for t in range(r * MOE_SUB_BLOCK, (r + 1) * MOE_SUB_BLOCK):
                y0 = rows_scr[_token_rows(dest_ref[t0 + t]), :]
                y1 = rows_scr[_token_rows(dest_ref[MOE_CHUNK + t0 + t]), :]
                tok_scr[pl.ds(t * SUBLANES, SUBLANES), :] = (gate_ref[t0 + t] * y0
                                                             + gate_ref[MOE_CHUNK + t0 + t] * y1)
            rows = pl.ds(r * MOE_SUB_BLOCK, MOE_SUB_BLOCK)
            ffn = _load_token_major(tok_scr, r * MOE_SUB_BLOCK, MOE_SUB_BLOCK)
            x = jnp.where(is_a, xa_ref[rows, :], xb_ref[rows, :])
            y = DEEPNORM_ALPHA * x + mod_ref[5:6, :] * ffn
            outs.append(_standardize(y) * g_ref[layer:layer + 1, :] + b_ref[layer:layer + 1, :])

        @pl.when(is_a)
        def _():
            for r, out in enumerate(outs):
                oa_ref[pl.ds(r * MOE_SUB_BLOCK, MOE_SUB_BLOCK), :] = out

        @pl.when(jnp.logical_not(is_a))
        def _():
            for r, out in enumerate(outs):
                ob_ref[pl.ds(r * MOE_SUB_BLOCK, MOE_SUB_BLOCK), :] = out


def _ffn_moe(xa, xb, l, mods, rw_t, w13, w2, g, b, lat_seq):
    n_a = xa.shape[0] // MOE_CHUNK
    n_chunks = n_a + xb.shape[0] // MOE_CHUNK
    n_a_blocks = n_a * MOE_TOK_STEPS
    dest, gates, start, padded = _router(xa, xb, l, mods, rw_t, lat_seq)
    first_expert_step = MOE_TOK_STEPS
    first_combine_step = MOE_TOK_STEPS + N_EXPERTS

    def token_block(c, j, *_):
        blk = jnp.where(j < first_combine_step, jnp.minimum(j, MOE_TOK_STEPS - 1), j - first_combine_step)
        return c * MOE_TOK_STEPS + blk

    def out_token_block(c, j, *_):
        return c * MOE_TOK_STEPS + jnp.maximum(j - first_combine_step, 0)

    def mod_of(c, j, *_):
        return (l, _group_of_block(token_block(c, j), MOE_TOK_BLOCK, n_a_blocks, lat_seq), 0, 0)

    def expert_of(c, j, *_):
        return (jnp.clip(j - first_expert_step, 0, N_EXPERTS - 1), 0, 0)

    grid_spec = pltpu.PrefetchScalarGridSpec(
        num_scalar_prefetch=2,
        grid=(n_chunks, MOE_STEPS),
        in_specs=_two_stream_specs(MOE_TOK_BLOCK, n_a_blocks, token_block) + [
                  pl.BlockSpec((None, None, SUBLANES, D_MODEL), mod_of),
                  pl.BlockSpec((2 * MOE_CHUNK,), lambda c, j, *_: (c,), memory_space=pltpu.SMEM),
                  pl.BlockSpec((2 * MOE_CHUNK,), lambda c, j, *_: (c,), memory_space=pltpu.SMEM),
                  pl.BlockSpec((None, D_MODEL, 2 * D_FF_EXPERT), expert_of),
                  pl.BlockSpec((None, D_FF_EXPERT, D_MODEL), expert_of),
                  pl.BlockSpec(g.shape, lambda c, j, *_: (0, 0)), pl.BlockSpec(b.shape, lambda c, j, *_: (0, 0))],
        out_specs=_two_stream_specs(MOE_TOK_BLOCK, n_a_blocks, out_token_block),
        scratch_shapes=[pltpu.VMEM((MOE_TOK_BLOCK * SUBLANES, LANES), F32),
                        pltpu.VMEM((MOE_ROWS * SUBLANES, LANES), F32)],
    )
    return pl.pallas_call(
        functools.partial(_moe_kernel, n_a=n_a, layer=l),
        grid_spec=grid_spec,
        out_shape=[jax.ShapeDtypeStruct(xa.shape, F32), jax.ShapeDtypeStruct(xb.shape, F32)],
        compiler_params=pltpu.CompilerParams(
            dimension_semantics=("arbitrary", "arbitrary"), vmem_limit_bytes=VMEM_LIMIT),
        name="moe_experts",
    )(start.reshape(-1), padded.reshape(-1), xa, xb, mods,
      dest.reshape(-1), gates.reshape(-1), w13, w2, g, b)


def _rope_tables(n_tokens):
    t = np.arange(n_tokens)
    row = (t // GRID_W).astype(np.float32)
    col = (t % GRID_W).astype(np.float32)
    inv_freq = (np.float32(ROPE_THETA) ** (-np.arange(0, AXIS_ROT, 2, dtype=np.float32) / AXIS_ROT)).astype(np.float32)
    ang_r = row[:, None] * inv_freq
    ang_c = col[:, None] * inv_freq
    cos = np.concatenate([np.cos(ang_r), np.cos(ang_r), np.cos(ang_c), np.cos(ang_c)], axis=1)
    sin = np.concatenate([-np.sin(ang_r), np.sin(ang_r), -np.sin(ang_c), np.sin(ang_c)], axis=1)
    return jnp.asarray(np.tile(cos, (1, 2)), F32), jnp.asarray(np.tile(sin, (1, 2)), F32)


def kernel(x_prompt, x_sample, cache_k, cache_v, c, c_ctx, ada_w, ada_b, w_in, q_norm_g, k_norm_g, conv_w, conv_b, sgu_norm_g, sgu_w, sgu_b, w_out, ln1_g, ln1_b, ln2_g, ln2_b, ffn_w1, ffn_w3, ffn_w2, router_w, moe_w1, moe_w3, moe_w2):
    batch, seq, _ = x_prompt.shape
    dec_batch, dec_seq, _ = x_sample.shape
    past_len = cache_k.shape[2]
    n_ctx = batch * seq
    n_lat = dec_batch * dec_seq
    assert DEPTH == 2 and 1 + dec_batch <= SUBLANES
    assert seq == ATT_BLOCK and CTX_SEQS_PER_STEP * seq == ROW_BLOCK and batch % CTX_SEQS_PER_STEP == 0
    assert dec_seq % ROW_BLOCK == 0 and dec_seq & (dec_seq - 1) == 0
    assert n_ctx % MOE_CHUNK == 0 and n_lat % MOE_CHUNK == 0
    assert dec_seq % ROUTER_BLOCK == 0 and dec_seq % MOE_TOK_BLOCK == 0 and dec_seq % FFN_TILE == 0

    mod = _modulation(c_ctx[None, :], c, ada_w, ada_b)
    mod = mod.reshape(DEPTH, SUBLANES, 6, D_MODEL)[:, :1 + dec_batch]
    mod = jnp.pad(mod, ((0, 0), (0, 0), (0, SUBLANES - 6), (0, 0)))

    lane_id = np.arange(GROUP_TILE) // HEAD_DIM
    ones_bd = jnp.asarray(lane_id[:, None] == lane_id[None, :], BF16)
    cos, sin = _rope_tables(dec_seq)
    small = (
        jnp.concatenate([jnp.tile(q_norm_g, (1, N_Q_HEADS)), jnp.tile(k_norm_g, (1, N_KV_HEADS))], axis=1),
        conv_b, sgu_norm_g, ln1_g, ln1_b,
        ones_bd, conv_w,
        jnp.swapaxes(sgu_w, 1, 2).reshape(DEPTH, CHUNK, SGU_HEADS * CHUNK).astype(BF16),
        jnp.repeat(jnp.swapaxes(sgu_b, 1, 2), HEAD_DIM, axis=2),
    )
    kc = cache_k.reshape(dec_batch, DEPTH, past_len, KV_WIDTH)
    vc = cache_v.reshape(dec_batch, DEPTH, past_len, KV_WIDTH)

    xs = [x_prompt.reshape(n_ctx, D_MODEL), x_sample.reshape(n_lat, D_MODEL)]
    assert DEPTH == 2 and ffn_w1.shape[0] == 1 and moe_w1.shape[0] == 1
    side_casts = [[(ffn_w1[0],), (ffn_w3[0],), (ffn_w2[0],), ((w_in, 1),), ((w_out, 1),), (moe_w2.reshape(-1, D_MODEL),)],
                  [(moe_w1.reshape(-1, D_FF_EXPERT), moe_w3.reshape(-1, D_FF_EXPERT))]]
    kv = None
    for l in range(DEPTH):
        if l == 0:
            x_ctx, k_ctx, v_ctx, w_in_bf, w_out_bf, *ffn_bf, w_in_next, w_out_next, w2_bf = _mixer_ctx(
                xs[0], l, batch, seq, mod, w_in, w_out, small, kv, side_casts[l])
        else:
            w_in_bf, w_out_bf = w_in_next, w_out_next
            x_ctx, k_ctx, v_ctx, w13_bf = _mixer_ctx(
                xs[0], l, batch, seq, mod, w_in_bf, w_out_bf, small, kv, side_casts[l])
        kv = (k_ctx, v_ctx)
        x_lat = _mixer_lat(xs[1], l, dec_batch, dec_seq, mod, w_in_bf, w_out_bf, small, cos, sin, kc, vc)
        if l % 2 == 0:
            xs = _ffn_dense(x_ctx, x_lat, l, mod, *ffn_bf, ln2_g, ln2_b, dec_seq)
        else:
            ws = (w13_bf.reshape(N_EXPERTS, D_MODEL, 2 * D_FF_EXPERT), w2_bf.reshape(N_EXPERTS, D_FF_EXPERT, D_MODEL))
            xs = _ffn_moe(x_ctx, x_lat, l, mod, router_w[l // 2].T, *ws, ln2_g, ln2_b, dec_seq)
    y_p = xs[0].reshape(batch, seq, D_MODEL)
    y_s = xs[1].reshape(dec_batch, dec_seq, D_MODEL)
    new_k = kv[0].reshape(batch, DEPTH, seq, N_KV_HEADS, HEAD_DIM)
    new_v = kv[1].reshape(batch, DEPTH, seq, N_KV_HEADS, HEAD_DIM)
    return (y_p, y_s, new_k, new_v)
```

```python
import functools

import numpy as np
import jax
import jax.numpy as jnp
from jax import lax
from jax.experimental import pallas as pl
from jax.experimental.pallas import tpu as pltpu

F32 = jnp.float32
BF16 = jnp.bfloat16
I32 = jnp.int32

D_MODEL = 1024
DEPTH = 2
GRID_W = 64
HEAD_DIM = 64
N_Q_HEADS = 8
N_KV_HEADS = 2
ATTN_WIDTH = N_Q_HEADS * HEAD_DIM
KV_WIDTH = N_KV_HEADS * HEAD_DIM
ATTN_SCALE = HEAD_DIM ** -0.5
ROPE_THETA = 10000.0
AXIS_ROT = HEAD_DIM // 2
CONV_WIDTH = 256
SGU_WIDTH = 256
SGU_HEADS = 4
CHUNK = 128
IN_WIDTH = 2048
D_FF = 2816
N_EXPERTS = 8
D_FF_EXPERT = 1408
EPS = 1e-6
DEEPNORM_ALPHA = (2 * DEPTH) ** 0.25

LANES = 128
SUBLANES = 8
ROW_BLOCK = 512
GROUP_TILE = 256
ATT_BLOCK = 256
CTX_SEQS_PER_STEP = 2
FFN_TILE = 512
FFN_COLS = 256
MOE_CHUNK = 2048
ROUTER_BLOCK = 1024
MOE_GRAN = 128
MOE_BLOCK = 256
MOE_TOK_BLOCK = 512
MOE_SUB_BLOCK = 256
MOE_TOK_STEPS = MOE_CHUNK // MOE_TOK_BLOCK
MOE_STEPS = 2 * MOE_TOK_STEPS + N_EXPERTS
MOE_ROWS = 2 * MOE_CHUNK + N_EXPERTS * MOE_GRAN
VMEM_LIMIT = 60 * 1024 * 1024

_Q0, _K0, _V0, _CI0, _CB0, _CC0, _SU0, _SV0 = 0, 512, 640, 768, 1024, 1280, 1536, 1792


def _dot(a, b):
    return jnp.dot(a, b, preferred_element_type=F32)


def _dot_nt(a, b):
    return lax.dot_general(a, b, (((1,), (1,)), ((), ())), preferred_element_type=F32)


def _split(x):
    hi = x.astype(BF16)
    lo = (x - hi.astype(F32)).astype(BF16)
    return hi, lo


def _group_sum(x, ones_bd):
    outs = []
    for c0 in range(0, x.shape[1], GROUP_TILE):
        width = min(GROUP_TILE, x.shape[1] - c0)
        outs.append(_dot(x[:, c0:c0 + width].astype(BF16), ones_bd[:width, :width]))
    return outs[0] if len(outs) == 1 else jnp.concatenate(outs, axis=1)


def _standardize(x):
    mu = jnp.mean(x, axis=-1, keepdims=True)
    d = x - mu
    return d * lax.rsqrt(jnp.mean(d * d, axis=-1, keepdims=True) + EPS)


def _silu(x):
    return x / (1.0 + jnp.exp(-x))


def _modulation_kernel(cctx_ref, c_ref, w_ref, b_ref, o_ref):
    n_pad = SUBLANES - 1 - c_ref.shape[0]
    cond = jnp.concatenate([cctx_ref[...], c_ref[...], jnp.zeros((n_pad, D_MODEL), F32)], axis=0)
    s_hi, s_lo = _split(_silu(cond))
    w_hi, w_lo = _split(w_ref[...])
    bias = b_ref[pl.ds(pl.program_id(0), 1), :]
    o_ref[...] = _dot(s_hi, w_hi) + _dot(s_hi, w_lo) + _dot(s_lo, w_hi) + bias


def _modulation(c_ctx, c, ada_w, ada_b):
    n_out = ada_w.shape[-1]
    tn = 1536
    return pl.pallas_call(
        _modulation_kernel,
        grid=(DEPTH, n_out // tn),
        in_specs=[
            pl.BlockSpec(c_ctx.shape, lambda l, j: (0, 0)),
            pl.BlockSpec(c.shape, lambda l, j: (0, 0)),
            pl.BlockSpec((None, D_MODEL, tn), lambda l, j: (l, 0, j)),
            pl.BlockSpec((DEPTH, tn), lambda l, j: (0, j)),
        ],
        out_specs=pl.BlockSpec((None, SUBLANES, tn), lambda l, j: (l, 0, j)),
        out_shape=jax.ShapeDtypeStruct((DEPTH, SUBLANES, n_out), F32),
        compiler_params=pltpu.CompilerParams(
            dimension_semantics=("arbitrary", "arbitrary"), vmem_limit_bytes=VMEM_LIMIT),
        name="modulation",
    )(c_ctx, c, ada_w, ada_b)


def _rope(x, cos, sin_signed):
    w = x.shape[1]
    lane = lax.broadcasted_iota(I32, x.shape, 1)
    first_half = (lane & 31) < 16
    partner = jnp.where(first_half, pltpu.roll(x, w - 16, 1), pltpu.roll(x, 16, 1))
    return x * cos + partner * sin_signed


def _head_variants(x):
    lane = lax.broadcasted_iota(I32, x.shape, 1)
    lo = lane < HEAD_DIM
    xr = pltpu.roll(x, HEAD_DIM, 1)
    zero = jnp.zeros_like(x)
    return (jnp.where(lo, x, zero).astype(BF16), jnp.where(lo, zero, xr).astype(BF16),
            jnp.where(lo, xr, zero).astype(BF16), jnp.where(lo, zero, x).astype(BF16))


def _mixer_kernel(*refs, layer, seq, n_seq, n_cache, rope, cast_weights, stack_kv, side_jobs=(), n_steps=None):
    refs = list(refs)
    skew = n_steps is not None

    def take(n):
        out, refs[:] = refs[:n], refs[n:]
        return out

    (x_ref,) = take(1)
    xres_ref = take(1)[0] if skew else x_ref
    mod_ref, win_ref, wout_ref = take(3)
    qkg_ref, convb_ref, sgug_ref, ln1g_ref, ln1b_ref = take(5)
    ones_ref, convw_ref, sguw_ref, sgub_ref = take(4)
    this = slice(layer, layer + 1)
    if rope:
        cos_ref, sin_ref, kc_ref, vc_ref = take(4)
    if stack_kv:
        kprev_ref, vprev_ref = take(2)
    side_in = [take(n_src) for n_src in side_jobs]
    (x1_ref,) = take(1)
    if not rope:
        k_ref, v_ref = take(2)
    if cast_weights:
        winb_ref, woutb_ref = take(2)

        @pl.when(pl.program_id(0) == 0)
        def _cast():
            winb_ref[...] = win_ref[...].astype(BF16)
            woutb_ref[...] = wout_ref[...].astype(BF16)
        win_ref, wout_ref = winb_ref, woutb_ref
    side_out = take(len(side_jobs))
    scratch = take(7)
    (mix_scr,) = take(1)
    n_rows = n_seq * seq
    assert n_cache == 0 or n_seq == 1

    def loop(n, body):
        if n == 1:
            body(0)
        else:
            def step(r, carry):
                body(r)
                return carry
            lax.fori_loop(0, n, step, 0, unroll=2)

    def block(r, size):
        if isinstance(r, int):
            return pl.ds(r * size, size)
        return pl.ds(pl.multiple_of(r * size, size), size)

    def project_phase(scr):
        q_scr, kvar_scr, vvar_scr, u_scr, cb_scr, su_scr, vn_scr = scr
        for srcs, dst in zip(side_in, side_out):
            col = 0
            for src in srcs:
                dst[:, col:col + src.shape[1]] = src[...].astype(BF16)
                col += src.shape[1]
        if n_cache:
            for i, var in enumerate(_head_variants(kc_ref[...])):
                kvar_scr[i, pl.ds(seq, n_cache), :] = var
            for i, var in enumerate(_head_variants(vc_ref[...])):
                vvar_scr[i, pl.ds(seq, n_cache), :] = var

        def project(r):
            rows = block(r, ROW_BLOCK)
            x = x_ref[rows, :]
            h = _standardize(x) * (1.0 + mod_ref[1:2, :]) + mod_ref[0:1, :]
            z = _dot(h.astype(BF16), win_ref[...])
            ones_bd = ones_ref[...]
            zq = z[:, _Q0:_K0]
            q = zq * lax.rsqrt(_group_sum(zq * zq, ones_bd) * (1.0 / HEAD_DIM) + EPS) * qkg_ref[this, :ATTN_WIDTH]
            zk = z[:, _K0:_V0]
            k = zk * lax.rsqrt(_group_sum(zk * zk, ones_bd) * (1.0 / HEAD_DIM) + EPS) * qkg_ref[this, ATTN_WIDTH:]
            v = z[:, _V0:_CI0]
            if rope:
                cos = cos_ref[rows, :]
                sin = sin_ref[rows, :]
                q = _rope(q, jnp.concatenate([cos] * 4, axis=1), jnp.concatenate([sin] * 4, axis=1))
                k = _rope(k, cos, sin)
            elif stack_kv:
                for s in range(ROW_BLOCK // seq):
                    sub = slice(s * seq, (s + 1) * seq)
                    k_ref[s, 0] = kprev_ref[sub, :]
                    v_ref[s, 0] = vprev_ref[sub, :]
                    k_ref[s, 1] = k[sub, :]
                    v_ref[s, 1] = v[sub, :]
            else:
                k_ref[rows, :] = k
                v_ref[rows, :] = v
            q_scr[rows, :] = (q * ATTN_SCALE).astype(BF16)
            for i, var in enumerate(_head_variants(k)):
                kvar_scr[i, rows, :] = var
            for i, var in enumerate(_head_variants(v)):
                vvar_scr[i, rows, :] = var
            u_scr[rows, :] = z[:, _CC0:_SU0] * z[:, _CI0:_CB0]
            cb_scr[rows, :] = z[:, _CB0:_CC0]
            su_scr[rows, :] = z[:, _SU0:_SV0]
            sv = z[:, _SV0:IN_WIDTH]
            d = sv - _group_sum(sv, ones_bd) * (1.0 / HEAD_DIM)
            vn = d * lax.rsqrt(_group_sum(d * d, ones_bd) * (1.0 / HEAD_DIM) + EPS) * sgug_ref[this, :]
            vn_scr[rows, :] = vn.astype(BF16)

        loop(n_rows // ROW_BLOCK, project)

    def consume_phase(scr):
        q_scr, kvar_scr, vvar_scr, u_scr, cb_scr, su_scr, vn_scr = scr
        u = u_scr[...]
        pos = lax.broadcasted_iota(I32, u.shape, 0) & (seq - 1)
        up = jnp.where(pos == 0, 0.0, pltpu.roll(u, 1, 0))
        dn = jnp.where(pos == seq - 1, 0.0, pltpu.roll(u, n_rows - 1, 0))
        conv = up * convw_ref[0:1, :] + u * convw_ref[1:2, :] + dn * convw_ref[2:3, :] + convb_ref[this, :]
        mix_scr[:, ATTN_WIDTH:ATTN_WIDTH + CONV_WIDTH] = (cb_scr[...] * conv).astype(BF16)

        for n in range(n_rows // CHUNK):
            rows = pl.ds(n * CHUNK, CHUNK)
            vn = vn_scr[rows, :]
            lane = lax.broadcasted_iota(I32, vn.shape, 1)
            per_head = [jnp.where((lane >= hd * HEAD_DIM) & (lane < (hd + 1) * HEAD_DIM), vn, jnp.zeros_like(vn))
                        for hd in range(SGU_HEADS)]
            s = sgub_ref[...] + _dot(sguw_ref[...], jnp.concatenate(per_head, axis=0))
            mix_scr[rows, ATTN_WIDTH + CONV_WIDTH:] = (su_scr[rows, :] * s).astype(BF16)

        def attend(s, r):
            rows = block(s * (seq // ATT_BLOCK) + r, ATT_BLOCK)
            chunks = [pl.ds(s * seq, seq)] + ([pl.ds(s * seq + seq, n_cache)] if n_cache else [])
            for pair in range(N_Q_HEADS // 2):
                qp = q_scr[rows, pair * LANES:(pair + 1) * LANES]
                kv = pair // (N_Q_HEADS // N_KV_HEADS // 2)
                acc = jnp.zeros((ATT_BLOCK, LANES), F32)
                for parity in range(2):
                    scs = [_dot_nt(qp, kvar_scr[2 * kv + parity, keys, :]) for keys in chunks]
                    top = functools.reduce(jnp.maximum, [jnp.max(sc, axis=1, keepdims=True) for sc in scs])
                    ps = [jnp.exp(sc - top) for sc in scs]
                    denom = sum(jnp.sum(p, axis=1, keepdims=True) for p in ps)
                    pv = sum(_dot(p.astype(BF16), vvar_scr[2 * kv + parity, keys, :]) for p, keys in zip(ps, chunks))
                    acc = acc + pv / denom
                mix_scr[rows, pair * LANES:(pair + 1) * LANES] = acc.astype(BF16)

        for s in range(n_seq):
            loop(seq // ATT_BLOCK, functools.partial(attend, s))

        def finish(r):
            rows = block(r, ROW_BLOCK)
            mix = _dot(mix_scr[rows, :], wout_ref[...])
            y = DEEPNORM_ALPHA * xres_ref[rows, :] + mod_ref[2:3, :] * mix
            x1_ref[rows, :] = _standardize(y) * ln1g_ref[this, :] + ln1b_ref[this, :]

        loop(n_rows // ROW_BLOCK, finish)

    if not skew:
        project_phase(scratch)
        consume_phase(scratch)
        return

    i = pl.program_id(0)
    slot = lax.rem(i, 2)
    mine = [ref.at[slot] for ref in scratch]
    other = [ref.at[1 - slot] for ref in scratch]
    pl.when(i == 0)(functools.partial(project_phase, mine))

    @pl.when((i > 0) & (i < n_steps))
    def _():
        consume_phase(other)
        project_phase(mine)

    pl.when(i == n_steps)(functools.partial(consume_phase, other))


def _full(shape):
    n = len(shape)
    return pl.BlockSpec(shape, lambda *_: (0,) * n)


def _resident(shape):
    n = len(shape)
    return pl.BlockSpec(shape, lambda *_: (0,) * n, pipeline_mode=pl.Buffered(1))


def _layer_block(arr, l, resident=False):
    shape = arr.shape[1:]
    kw = dict(pipeline_mode=pl.Buffered(1)) if resident else {}
    return pl.BlockSpec((None,) + shape, lambda *_: (l,) + (0,) * len(shape), **kw)


def _mixer_scratch(n_rows, n_cache, slots=()):
    nk = n_rows + n_cache
    return [
        pltpu.VMEM(slots + (n_rows, ATTN_WIDTH), BF16),
        pltpu.VMEM(slots + (4, nk, LANES), BF16),
        pltpu.VMEM(slots + (4, nk, LANES), BF16),
        pltpu.VMEM(slots + (n_rows, CONV_WIDTH), F32),
        pltpu.VMEM(slots + (n_rows, CONV_WIDTH), F32),
        pltpu.VMEM(slots + (n_rows, SGU_WIDTH), F32),
        pltpu.VMEM(slots + (n_rows, SGU_WIDTH), BF16),
        pltpu.VMEM((n_rows, D_MODEL), BF16),
    ]


def _side_source(src):
    return src if isinstance(src, tuple) else (src, None)


def _mixer_ctx(x, l, n_seq, seq, mod, w_in, w_out, small, kv_prev=None, side_casts=()):
    per_step = CTX_SEQS_PER_STEP
    rows = per_step * seq
    stack_kv = kv_prev is not None
    cast_weights = w_in.dtype != BF16
    n_steps = n_seq // per_step
    kernel = functools.partial(_mixer_kernel, layer=l, seq=seq, n_seq=per_step, n_cache=0, rope=False,
                               cast_weights=cast_weights, stack_kv=stack_kv,
                               side_jobs=tuple(len(job) for job in side_casts), n_steps=n_steps)
    cur = lambda i: (jnp.minimum(i, n_steps - 1), 0)
    prev = lambda i: (jnp.maximum(i - 1, 0), 0)
    w_specs = ([_layer_block(w_in, l, resident=True), _layer_block(w_out, l, resident=True)] if cast_weights
               else [_resident(w_in.shape), _resident(w_out.shape)])
    in_specs = ([pl.BlockSpec((rows, D_MODEL), cur), pl.BlockSpec((rows, D_MODEL), prev),
                 pl.BlockSpec((None, None, SUBLANES, D_MODEL), lambda i: (l, 0, 0, 0))] + w_specs
                + [_full(a.shape) if a.ndim == 2 else _layer_block(a, l) for a in small])
    args = [x, x, mod, w_in, w_out, *small]
    if stack_kv:
        in_specs += [pl.BlockSpec((rows, KV_WIDTH), cur)] * 2
        args += list(kv_prev)
        kv_spec = pl.BlockSpec((per_step, DEPTH, seq, KV_WIDTH), lambda i: cur(i) + (0, 0))
        kv_shape = jax.ShapeDtypeStruct((n_seq, DEPTH, seq, KV_WIDTH), F32)
    else:
        kv_spec = pl.BlockSpec((rows, KV_WIDTH), cur)
        kv_shape = jax.ShapeDtypeStruct((n_seq * seq, KV_WIDTH), F32)
    side_out_specs, side_out_shapes = [], []
    for job in side_casts:
        sources = [_side_source(src) for src in job]
        n_rows = sources[0][0].shape[-2]
        n_cols = sum(arr.shape[-1] for arr, _ in sources)
        for arr, layer in sources:
            blk = (n_rows // n_steps, arr.shape[-1])
            if layer is None:
                in_specs.append(pl.BlockSpec(blk, cur))
            else:
                in_specs.append(pl.BlockSpec((None,) + blk, lambda i, layer=layer: (layer,) + cur(i)))
            args.append(arr)
        side_out_specs.append(pl.BlockSpec((n_rows // n_steps, n_cols), cur))
        side_out_shapes.append(jax.ShapeDtypeStruct((n_rows, n_cols), BF16))
    w_out_specs, w_out_shapes = [], []
    if cast_weights:
        w_out_specs = [_full((D_MODEL, IN_WIDTH)), _full((D_MODEL, D_MODEL))]
        w_out_shapes = [jax.ShapeDtypeStruct((D_MODEL, IN_WIDTH), BF16), jax.ShapeDtypeStruct((D_MODEL, D_MODEL), BF16)]
    return pl.pallas_call(
        kernel,
        grid=(n_steps + 1,),
        in_specs=in_specs,
        out_specs=[pl.BlockSpec((rows, D_MODEL), prev), kv_spec, kv_spec] + w_out_specs + side_out_specs,
        out_shape=[jax.ShapeDtypeStruct((n_seq * seq, D_MODEL), F32), kv_shape, kv_shape] + w_out_shapes
        + side_out_shapes,
        scratch_shapes=_mixer_scratch(rows, 0, slots=(2,)),
        compiler_params=pltpu.CompilerParams(dimension_semantics=("arbitrary",), vmem_limit_bytes=VMEM_LIMIT),
        name="mixer_ctx",
    )(*args)


def _mixer_lat(x, l, n_seq, seq, mod, w_in_bf, w_out_bf, small, cos, sin, kc, vc):
    n_cache = kc.shape[2]
    kernel = functools.partial(_mixer_kernel, layer=l, seq=seq, n_seq=1, n_cache=n_cache, rope=True,
                               cast_weights=False, stack_kv=False)
    cache_spec = pl.BlockSpec((None, None, n_cache, KV_WIDTH), lambda b: (b, l, 0, 0))
    return pl.pallas_call(
        kernel,
        grid=(n_seq,),
        in_specs=([pl.BlockSpec((seq, D_MODEL), lambda b: (b, 0)),
                   pl.BlockSpec((None, None, SUBLANES, D_MODEL), lambda b: (l, 1 + b, 0, 0)),
                   _resident((D_MODEL, IN_WIDTH)), _resident((D_MODEL, D_MODEL))]
                  + [_full(a.shape) if a.ndim == 2 else _layer_block(a, l) for a in small]
                  + [_full((seq, LANES)), _full((seq, LANES)), cache_spec, cache_spec]),
        out_specs=pl.BlockSpec((seq, D_MODEL), lambda b: (b, 0)),
        out_shape=jax.ShapeDtypeStruct((n_seq * seq, D_MODEL), F32),
        scratch_shapes=_mixer_scratch(seq, n_cache),
        compiler_params=pltpu.CompilerParams(dimension_semantics=("arbitrary",), vmem_limit_bytes=VMEM_LIMIT),
        name="mixer_lat",
    )(x, mod, w_in_bf, w_out_bf, *small, cos, sin, kc, vc)


def _two_stream_specs(block_rows, n_a, block_of):
    spec_a = pl.BlockSpec((block_rows, D_MODEL), lambda *ids: (jnp.minimum(block_of(*ids), n_a - 1), 0))
    spec_b = pl.BlockSpec((block_rows, D_MODEL), lambda *ids: (jnp.maximum(block_of(*ids) - n_a, 0), 0))
    return [spec_a, spec_b]


def _group_of_block(blk, block_rows, n_a, lat_seq):
    return jnp.where(blk < n_a, 0, 1 + (blk - n_a) // (lat_seq // block_rows))


def _ffn_kernel(xa_ref, xb_ref, mod_ref, w1_hbm, w3_hbm, w2_hbm, g_ref, b_ref, oa_ref, ob_ref,
                w1_ref, w3_ref, w2_ref, sem, *, n_a, layer):
    t = pl.program_id(0)
    is_a = t < n_a
    n_col_blocks = D_FF // FFN_COLS

    def weight_copies(c):
        cols = pl.ds(c * FFN_COLS, FFN_COLS)
        return (pltpu.make_async_copy(w1_hbm.at[:, cols], w1_ref.at[:, cols], sem.at[0, c]),
                pltpu.make_async_copy(w3_hbm.at[:, cols], w3_ref.at[:, cols], sem.at[1, c]),
                pltpu.make_async_copy(w2_hbm.at[cols, :], w2_ref.at[cols, :], sem.at[2, c]))

    def compute(first_step):
        if first_step:
            for c in range(n_col_blocks):
                for copy in weight_copies(c):
                    copy.start()
        x = jnp.where(is_a, xa_ref[...], xb_ref[...])
        h = (_standardize(x) * (1.0 + mod_ref[4:5, :]) + mod_ref[3:4, :]).astype(BF16)
        acc = jnp.zeros(x.shape, F32)
        for c in range(n_col_blocks):
            cols = slice(c * FFN_COLS, (c + 1) * FFN_COLS)
            if first_step:
                for copy in weight_copies(c):
                    copy.wait()
            a = _dot(h, w1_ref[:, cols])
            b = _dot(h, w3_ref[:, cols])
            acc = acc + _dot((_silu(a) * b).astype(BF16), w2_ref[cols, :])
        y = DEEPNORM_ALPHA * x + mod_ref[5:6, :] * acc
        return _standardize(y) * g_ref[layer:layer + 1, :] + b_ref[layer:layer + 1, :]

    @pl.when(t == 0)
    def _():
        oa_ref[...] = compute(True)

    @pl.when(t > 0)
    def _():
        out = compute(False)

        @pl.when(is_a)
        def _():
            oa_ref[...] = out

        @pl.when(jnp.logical_not(is_a))
        def _():
            ob_ref[...] = out


def _ffn_dense(xa, xb, l, mods, w1, w3, w2, g, b, lat_seq):
    n_a, n_b = xa.shape[0] // FFN_TILE, xb.shape[0] // FFN_TILE
    assert n_a >= 1
    x_specs = _two_stream_specs(FFN_TILE, n_a, lambda t: t)
    in_hbm = pl.BlockSpec(memory_space=pltpu.HBM)
    return pl.pallas_call(
        functools.partial(_ffn_kernel, n_a=n_a, layer=l),
        grid=(n_a + n_b,),
        in_specs=x_specs + [
            pl.BlockSpec((None, None, SUBLANES, D_MODEL),
                         lambda t: (l, _group_of_block(t, FFN_TILE, n_a, lat_seq), 0, 0)),
            in_hbm, in_hbm, in_hbm, _full(g.shape), _full(b.shape)],
        out_specs=x_specs,
        scratch_shapes=[pltpu.VMEM(w1.shape, BF16), pltpu.VMEM(w3.shape, BF16), pltpu.VMEM(w2.shape, BF16),
                        pltpu.SemaphoreType.DMA((3, D_FF // FFN_COLS))],
        out_shape=[jax.ShapeDtypeStruct(xa.shape, F32), jax.ShapeDtypeStruct(xb.shape, F32)],
        compiler_params=pltpu.CompilerParams(dimension_semantics=("arbitrary",), vmem_limit_bytes=VMEM_LIMIT),
        name="ffn_dense",
    )(xa, xb, mods, w1, w3, w2, g, b)


def _router_kernel(xa_ref, xb_ref, mod_ref, rw_ref, before_ref, dest_ref, gate_ref, start_ref, padded_ref, *,
                   n_a, lat_seq):
    c = pl.program_id(0)
    w_hi, w_lo = _split(rw_ref[...])
    n_blocks = MOE_CHUNK // ROUTER_BLOCK

    def store_per_token(ref, slot, val):
        for j in range(MOE_CHUNK // LANES):
            ref[slot, j:j + 1, :] = val[:, j * LANES:(j + 1) * LANES]

    parts = []
    for blk in range(n_blocks):
        rows = pl.ds(blk * ROUTER_BLOCK, ROUTER_BLOCK)
        mod = mod_ref[_group_of_block(c * n_blocks + blk, ROUTER_BLOCK, n_a * n_blocks, lat_seq)]
        x = jnp.where(c < n_a, xa_ref[rows, :], xb_ref[rows, :])
        h = _standardize(x) * (1.0 + mod[4:5, :]) + mod[3:4, :]
        h_hi, h_lo = _split(h)
        parts.append(_dot_nt(w_hi, h_hi) + _dot_nt(w_hi, h_lo) + _dot_nt(w_lo, h_hi))
    logits = jnp.concatenate(parts, axis=1)
    eid = lax.broadcasted_iota(I32, logits.shape, 0).astype(F32)
    m1 = jnp.max(logits, axis=0, keepdims=True)
    i1 = jnp.min(jnp.where(logits == m1, eid, float(N_EXPERTS)), axis=0, keepdims=True)
    oh1 = eid == i1
    rest = jnp.where(oh1, -jnp.inf, logits)
    m2 = jnp.max(rest, axis=0, keepdims=True)
    i2 = jnp.min(jnp.where(rest == m2, eid, float(N_EXPERTS)), axis=0, keepdims=True)
    oh2 = eid == i2
    e = jnp.exp(m2 - m1)
    store_per_token(gate_ref, 0, 1.0 / (1.0 + e))
    store_per_token(gate_ref, 1, e / (1.0 + e))
    sel = jnp.where(oh1 | oh2, 1.0, 0.0)
    ranks = []
    seen = jnp.zeros((N_EXPERTS, 1), F32)
    for blk in range(n_blocks):
        s_blk = sel[:, blk * ROUTER_BLOCK:(blk + 1) * ROUTER_BLOCK]
        ranks.append(_dot(s_blk.astype(BF16), before_ref[...]) + seen)
        seen = seen + jnp.sum(s_blk, axis=1, keepdims=True)
    rank = jnp.concatenate(ranks, axis=1)
    eid_out = lax.broadcasted_iota(I32, (1, LANES), 1).astype(F32)
    start = jnp.zeros(sel.shape, F32)
    start_out = jnp.zeros((1, LANES), F32)
    padded_out = jnp.zeros((1, LANES), F32)
    for ex in range(N_EXPERTS):
        cnt = jnp.sum(sel[ex:ex + 1, :], axis=1, keepdims=True)
        padded = jnp.ceil(cnt * (1.0 / MOE_GRAN)) * MOE_GRAN
        start = start + jnp.where(eid > ex, padded, 0.0)
        start_out = start_out + jnp.where(eid_out > ex, padded, 0.0)
        padded_out = padded_out + jnp.where(eid_out == ex, padded, 0.0)
    row = (start + rank) * SUBLANES
    store_per_token(dest_ref, 0, jnp.sum(jnp.where(oh1, row, 0.0), axis=0, keepdims=True).astype(I32))
    store_per_token(dest_ref, 1, jnp.sum(jnp.where(oh2, row, 0.0), axis=0, keepdims=True).astype(I32))

    @pl.when(c == 0)
    def _():
        start_ref[...] = jnp.zeros(start_ref.shape, I32)
        padded_ref[...] = jnp.zeros(padded_ref.shape, I32)

    start_ref[pl.ds(c, 1), :] = start_out.astype(I32)
    padded_ref[pl.ds(c, 1), :] = padded_out.astype(I32)


def _router(xa, xb, l, mods, rw_t, lat_seq):
    n_a = xa.shape[0] // MOE_CHUNK
    n_chunks = n_a + xb.shape[0] // MOE_CHUNK
    tok = np.arange(ROUTER_BLOCK)
    before = jnp.asarray(tok[:, None] < tok[None, :], BF16)
    table_rows = -(-n_chunks // SUBLANES) * SUBLANES
    return pl.pallas_call(
        functools.partial(_router_kernel, n_a=n_a, lat_seq=lat_seq),
        grid=(n_chunks,),
        in_specs=_two_stream_specs(MOE_CHUNK, n_a, lambda c: c) + [
                  _layer_block(mods, l),
                  _full((N_EXPERTS, D_MODEL)), _full((ROUTER_BLOCK, ROUTER_BLOCK))],
        out_specs=[pl.BlockSpec((None, 2, MOE_CHUNK // LANES, LANES), lambda c: (c, 0, 0, 0)),
                   pl.BlockSpec((None, 2, MOE_CHUNK // LANES, LANES), lambda c: (c, 0, 0, 0)),
                   _full((table_rows, LANES)), _full((table_rows, LANES))],
        out_shape=[jax.ShapeDtypeStruct((n_chunks, 2, MOE_CHUNK // LANES, LANES), I32),
                   jax.ShapeDtypeStruct((n_chunks, 2, MOE_CHUNK // LANES, LANES), F32),
                   jax.ShapeDtypeStruct((table_rows, LANES), I32),
                   jax.ShapeDtypeStruct((table_rows, LANES), I32)],
        compiler_params=pltpu.CompilerParams(dimension_semantics=("arbitrary",), vmem_limit_bytes=VMEM_LIMIT),
        name="moe_router",
    )(xa, xb, mods, rw_t, before)


def _token_rows(first_row):
    return pl.ds(pl.multiple_of(first_row, SUBLANES), SUBLANES)


def _store_token_major(ref, tok0, val):
    for cc in range(D_MODEL // LANES):
        ref[pl.ds(tok0 * SUBLANES + cc, val.shape[0], stride=SUBLANES), :] = val[:, cc * LANES:(cc + 1) * LANES]


def _load_token_major(ref, tok0, n):
    return jnp.concatenate(
        [ref[pl.ds(tok0 * SUBLANES + cc, n, stride=SUBLANES), :] for cc in range(D_MODEL // LANES)], axis=1)


def _moe_kernel(start_ref, padded_ref, xa_ref, xb_ref, mod_ref, dest_ref, gate_ref, w13_ref, w2_ref,
                g_ref, b_ref, oa_ref, ob_ref, tok_scr, rows_scr, *, n_a, layer):
    c = pl.program_id(0)
    j = pl.program_id(1)
    is_a = c < n_a
    first_expert_step = MOE_TOK_STEPS
    first_combine_step = MOE_TOK_STEPS + N_EXPERTS

    @pl.when((c == 0) & (j == 0))
    def _init():
        rows_scr[...] = jnp.zeros(rows_scr.shape, F32)

    @pl.when(j < first_expert_step)
    def _dispatch():
        t0 = j * MOE_TOK_BLOCK
        for r in range(MOE_TOK_BLOCK // MOE_SUB_BLOCK):
            rows = pl.ds(r * MOE_SUB_BLOCK, MOE_SUB_BLOCK)
            x = jnp.where(is_a, xa_ref[rows, :], xb_ref[rows, :])
            h = _standardize(x) * (1.0 + mod_ref[4:5, :]) + mod_ref[3:4, :]
            _store_token_major(tok_scr, r * MOE_SUB_BLOCK, h)
            for t in range(r * MOE_SUB_BLOCK, (r + 1) * MOE_SUB_BLOCK):
                row = tok_scr[pl.ds(t * SUBLANES, SUBLANES), :]
                rows_scr[_token_rows(dest_ref[t0 + t]), :] = row
                rows_scr[_token_rows(dest_ref[MOE_CHUNK + t0 + t]), :] = row

    def experts(row0, m):
        xin = _load_token_major(rows_scr, row0, m).astype(BF16)
        ab = _dot(xin, w13_ref[...])
        a, b = ab[:, :D_FF_EXPERT], ab[:, D_FF_EXPERT:]
        y = _dot((_silu(a) * b).astype(BF16), w2_ref[...])
        _store_token_major(rows_scr, row0, y)

    @pl.when((j >= first_expert_step) & (j < first_combine_step))
    def _experts():
        region = c * LANES + (j - first_expert_step)
        start = start_ref[region]
        padded = padded_ref[region]
        n_full = lax.shift_right_logical(padded, MOE_BLOCK.bit_length() - 1)

        def body(i, carry):
            experts(start + 2 * i * MOE_BLOCK, MOE_BLOCK)
            experts(start + (2 * i + 1) * MOE_BLOCK, MOE_BLOCK)
            return carry
        lax.fori_loop(0, lax.shift_right_logical(n_full, 1), body, 0)
        pl.when((n_full & 1) == 1)(functools.partial(experts, start + (n_full - 1) * MOE_BLOCK, MOE_BLOCK))
        for m in range(MOE_GRAN, MOE_BLOCK, MOE_GRAN):
            pl.when(padded - n_full * MOE_BLOCK == m)(
                functools.partial(experts, start + n_full * MOE_BLOCK, m))

    @pl.when(j >= first_combine_step)
    def _combine():
        t0 = (j - first_combine_step) * MOE_TOK_BLOCK
        outs = []
        for r in range(MOE_TOK_BLOCK // MOE_SUB_BLOCK):
            for t in range(r * MOE_SUB_BLOCK, (r + 1) * MOE_SUB_BLOCK):
                y0 = rows_scr[_token_rows(dest_ref[t0 + t]), :]
                y1 = rows_scr[_token_rows(dest_ref[MOE_CHUNK + t0 + t]), :]
                tok_scr[pl.ds(t * SUBLANES, SUBLANES), :] = (gate_ref[t0 + t] * y0
                                                             + gate_ref[MOE_CHUNK + t0 + t] * y1)
            rows = pl.ds(r * MOE_SUB_BLOCK, MOE_SUB_BLOCK)
            ffn = _load_token_major(tok_scr, r * MOE_SUB_BLOCK, MOE_SUB_BLOCK)
            x = jnp.where(is_a, xa_ref[rows, :], xb_ref[rows, :])
            y = DEEPNORM_ALPHA * x + mod_ref[5:6, :] * ffn
            outs.append(_standardize(y) * g_ref[layer:layer + 1, :] + b_ref[layer:layer + 1, :])

        @pl.when(is_a)
        def _():
            for r, out in enumerate(outs):
                oa_ref[pl.ds(r * MOE_SUB_BLOCK, MOE_SUB_BLOCK), :] = out

        @pl.when(jnp.logical_not(is_a))
        def _():
            for r, out in enumerate(outs):
                ob_ref[pl.ds(r * MOE_SUB_BLOCK, MOE_SUB_BLOCK), :] = out


def _ffn_moe(xa, xb, l, mods, rw_t, w13, w2, g, b, lat_seq):
    n_a = xa.shape[0] // MOE_CHUNK
    n_chunks = n_a + xb.shape[0] // MOE_CHUNK
    n_a_blocks = n_a * MOE_TOK_STEPS
    dest, gates, start, padded = _router(xa, xb, l, mods, rw_t, lat_seq)
    first_expert_step = MOE_TOK_STEPS
    first_combine_step = MOE_TOK_STEPS + N_EXPERTS

    def token_block(c, j, *_):
        blk = jnp.where(j < first_combine_step, jnp.minimum(j, MOE_TOK_STEPS - 1), j - first_combine_step)
        return c * MOE_TOK_STEPS + blk

    def out_token_block(c, j, *_):
        return c * MOE_TOK_STEPS + jnp.maximum(j - first_combine_step, 0)

    def mod_of(c, j, *_):
        return (l, _group_of_block(token_block(c, j), MOE_TOK_BLOCK, n_a_blocks, lat_seq), 0, 0)

    def expert_of(c, j, *_):
        return (jnp.clip(j - first_expert_step, 0, N_EXPERTS - 1), 0, 0)

    grid_spec = pltpu.PrefetchScalarGridSpec(
        num_scalar_prefetch=2,
        grid=(n_chunks, MOE_STEPS),
        in_specs=_two_stream_specs(MOE_TOK_BLOCK, n_a_blocks, token_block) + [
                  pl.BlockSpec((None, None, SUBLANES, D_MODEL), mod_of),
                  pl.BlockSpec((2 * MOE_CHUNK,), lambda c, j, *_: (c,), memory_space=pltpu.SMEM),
                  pl.BlockSpec((2 * MOE_CHUNK,), lambda c, j, *_: (c,), memory_space=pltpu.SMEM),
                  pl.BlockSpec((None, D_MODEL, 2 * D_FF_EXPERT), expert_of),
                  pl.BlockSpec((None, D_FF_EXPERT, D_MODEL), expert_of),
                  pl.BlockSpec(g.shape, lambda c, j, *_: (0, 0)), pl.BlockSpec(b.shape, lambda c, j, *_: (0, 0))],
        out_specs=_two_stream_specs(MOE_TOK_BLOCK, n_a_blocks, out_token_block),
        scratch_shapes=[pltpu.VMEM((MOE_TOK_BLOCK * SUBLANES, LANES), F32),
                        pltpu.VMEM((MOE_ROWS * SUBLANES, LANES), F32)],
    )
    return pl.pallas_call(
        functools.partial(_moe_kernel, n_a=n_a, layer=l),
        grid_spec=grid_spec,
        out_shape=[jax.ShapeDtypeStruct(xa.shape, F32), jax.ShapeDtypeStruct(xb.shape, F32)],
        compiler_params=pltpu.CompilerParams(
            dimension_semantics=("arbitrary", "arbitrary"), vmem_limit_bytes=VMEM_LIMIT),
        name="moe_experts",
    )(start.reshape(-1), padded.reshape(-1), xa, xb, mods,
      dest.reshape(-1), gates.reshape(-1), w13, w2, g, b)


def _rope_tables(n_tokens):
    t = np.arange(n_tokens)
    row = (t // GRID_W).astype(np.float32)
    col = (t % GRID_W).astype(np.float32)
    inv_freq = (np.float32(ROPE_THETA) ** (-np.arange(0, AXIS_ROT, 2, dtype=np.float32) / AXIS_ROT)).astype(np.float32)
    ang_r = row[:, None] * inv_freq
    ang_c = col[:, None] * inv_freq
    cos = np.concatenate([np.cos(ang_r), np.cos(ang_r), np.cos(ang_c), np.cos(ang_c)], axis=1)
    sin = np.concatenate([-np.sin(ang_r), np.sin(ang_r), -np.sin(ang_c), np.sin(ang_c)], axis=1)
    return jnp.asarray(np.tile(cos, (1, 2)), F32), jnp.asarray(np.tile(sin, (1, 2)), F32)


def kernel(x_prompt, x_sample, cache_k, cache_v, c, c_ctx, ada_w, ada_b, w_in, q_norm_g, k_norm_g, conv_w, conv_b, sgu_norm_g, sgu_w, sgu_b, w_out, ln1_g, ln1_b, ln2_g, ln2_b, ffn_w1, ffn_w3, ffn_w2, router_w, moe_w1, moe_w3, moe_w2):
    batch, seq, _ = x_prompt.shape
    dec_batch, dec_seq, _ = x_sample.shape
    past_len = cache_k.shape[2]
    n_ctx = batch * seq
    n_lat = dec_batch * dec_seq
    assert DEPTH == 2 and 1 + dec_batch <= SUBLANES
    assert seq == ATT_BLOCK and CTX_SEQS_PER_STEP * seq == ROW_BLOCK and batch % CTX_SEQS_PER_STEP == 0
    assert dec_seq % ROW_BLOCK == 0 and dec_seq & (dec_seq - 1) == 0
    assert n_ctx % MOE_CHUNK == 0 and n_lat % MOE_CHUNK == 0
    assert dec_seq % ROUTER_BLOCK == 0 and dec_seq % MOE_TOK_BLOCK == 0 and dec_seq % FFN_TILE == 0

    mod = _modulation(c_ctx[None, :], c, ada_w, ada_b)
    mod = mod.reshape(DEPTH, SUBLANES, 6, D_MODEL)[:, :1 + dec_batch]
    mod = jnp.pad(mod, ((0, 0), (0, 0), (0, SUBLANES - 6), (0, 0)))

    lane_id = np.arange(GROUP_TILE) // HEAD_DIM
    ones_bd = jnp.asarray(lane_id[:, None] == lane_id[None, :], BF16)
    cos, sin = _rope_tables(dec_seq)
    small = (
        jnp.concatenate([jnp.tile(q_norm_g, (1, N_Q_HEADS)), jnp.tile(k_norm_g, (1, N_KV_HEADS))], axis=1),
        conv_b, sgu_norm_g, ln1_g, ln1_b,
        ones_bd, conv_w,
        jnp.swapaxes(sgu_w, 1, 2).reshape(DEPTH, CHUNK, SGU_HEADS * CHUNK).astype(BF16),
        jnp.repeat(jnp.swapaxes(sgu_b, 1, 2), HEAD_DIM, axis=2),
    )
    kc = cache_k.reshape(dec_batch, DEPTH, past_len, KV_WIDTH)
    vc = cache_v.reshape(dec_batch, DEPTH, past_len, KV_WIDTH)

    xs = [x_prompt.reshape(n_ctx, D_MODEL), x_sample.reshape(n_lat, D_MODEL)]
    assert DEPTH == 2 and ffn_w1.shape[0] == 1 and moe_w1.shape[0] == 1
    side_casts = [[(ffn_w1[0],), (ffn_w3[0],), (ffn_w2[0],), ((w_in, 1),), ((w_out, 1),), (moe_w2.reshape(-1, D_MODEL),)],
                  [(moe_w1.reshape(-1, D_FF_EXPERT), moe_w3.reshape(-1, D_FF_EXPERT))]]
    kv = None
    for l in range(DEPTH):
        if l == 0:
            x_ctx, k_ctx, v_ctx, w_in_bf, w_out_bf, *ffn_bf, w_in_next, w_out_next, w2_bf = _mixer_ctx(
                xs[0], l, batch, seq, mod, w_in, w_out, small, kv, side_casts[l])
        else:
            w_in_bf, w_out_bf = w_in_next, w_out_next
            x_ctx, k_ctx, v_ctx, w13_bf = _mixer_ctx(
                xs[0], l, batch, seq, mod, w_in_bf, w_out_bf, small, kv, side_casts[l])
        kv = (k_ctx, v_ctx)
        x_lat = _mixer_lat(xs[1], l, dec_batch, dec_seq, mod, w_in_bf, w_out_bf, small, cos, sin, kc, vc)
        if l % 2 == 0:
            xs = _ffn_dense(x_ctx, x_lat, l, mod, *ffn_bf, ln2_g, ln2_b, dec_seq)
        else:
            ws = (w13_bf.reshape(N_EXPERTS, D_MODEL, 2 * D_FF_EXPERT), w2_bf.reshape(N_EXPERTS, D_FF_EXPERT, D_MODEL))
            xs = _ffn_moe(x_ctx, x_lat, l, mod, router_w[l // 2].T, *ws, ln2_g, ln2_b, dec_seq)
    y_p = xs[0].reshape(batch, seq, D_MODEL)
    y_s = xs[1].reshape(dec_batch, dec_seq, D_MODEL)
    new_k = kv[0].reshape(batch, DEPTH, seq, N_KV_HEADS, HEAD_DIM)
    new_v = kv[1].reshape(batch, DEPTH, seq, N_KV_HEADS, HEAD_DIM)
    return (y_p, y_s, new_k, new_v)
```
